```python
import jax, jax.numpy as jnp
from jax import lax
import numpy as np

D_MODEL = 1024
BATCH = 8
SEQ = 4096
DEPTH = 4

N_HEADS = 8
QK_NOPE = 64
QK_ROPE = 32
QK_HEAD = QK_NOPE + QK_ROPE
V_HEAD = 64
Q_LORA = 384
KV_LORA = 256
ATTN_WIDTH = N_HEADS * V_HEAD
CONV_WIDTH = D_MODEL - ATTN_WIDTH
CONV_TAPS = 3
IN_COLS = Q_LORA + KV_LORA + QK_ROPE + 3 * CONV_WIDTH
D_FF = 4 * D_MODEL
PLE_DIM = 256
ROPE_THETA = 10000.0
Q_BLOCK = 128
EPS = 1e-6
MAX_POS_OFFSET = 1024

kernel_name = "hybrid_mla_shortconv_trunk"


def rmsnorm(x, g):
    xf = x.astype(jnp.float32)
    y = xf * lax.rsqrt(jnp.mean(xf * xf, axis=-1, keepdims=True) + EPS)
    return (y * g.astype(jnp.float32)).astype(x.dtype)


def rope_tables(positions):
    inv_freq = 1.0 / (ROPE_THETA ** (jnp.arange(0, QK_ROPE, 2, dtype=jnp.float32) / QK_ROPE))
    ang = positions.astype(jnp.float32)[..., None] * inv_freq
    return jnp.cos(ang)[:, :, None, :], jnp.sin(ang)[:, :, None, :]


def apply_rope(x, cos, sin):
    half = QK_ROPE // 2
    x1 = x[..., :half].astype(jnp.float32)
    x2 = x[..., half:].astype(jnp.float32)
    return jnp.concatenate([x1 * cos - x2 * sin, x2 * cos + x1 * sin], axis=-1).astype(x.dtype)


def causal_block_attention(q, k, v):
    b, s = q.shape[0], q.shape[1]
    scale = QK_HEAD ** -0.5
    n_blocks = s // Q_BLOCK
    k_idx = jnp.arange(s)

    def one_block(i):
        start = i * Q_BLOCK
        qb = lax.dynamic_slice_in_dim(q, start, Q_BLOCK, axis=1)
        sc = jnp.einsum('bqhd,bkhd->bhqk', qb, k, preferred_element_type=jnp.float32) * scale
        q_idx = start + jnp.arange(Q_BLOCK)
        sc = jnp.where(k_idx[None, :] <= q_idx[:, None], sc, -jnp.inf)
        pr = jax.nn.softmax(sc, axis=-1).astype(v.dtype)
        return jnp.einsum('bhqk,bkhd->bqhd', pr, v)

    out = lax.map(one_block, jnp.arange(n_blocks))
    return jnp.moveaxis(out, 0, 1).reshape(b, s, N_HEADS * V_HEAD)


def mla_group(q_lat, kv_lat, k_pe, cos, sin, g_q_lat, w_uq, g_kv_lat, w_ukv,
              g_qn_nope, g_qn_rope, g_kn_nope, g_kn_rope):
    b, s = q_lat.shape[0], q_lat.shape[1]
    q = (rmsnorm(q_lat, g_q_lat) @ w_uq).reshape(b, s, N_HEADS, QK_HEAD)
    kv = (rmsnorm(kv_lat, g_kv_lat) @ w_ukv).reshape(b, s, N_HEADS, QK_NOPE + V_HEAD)
    k_nope, v = kv[..., :QK_NOPE], kv[..., QK_NOPE:]
    q_nope = rmsnorm(q[..., :QK_NOPE], g_qn_nope)
    q_pe = apply_rope(rmsnorm(q[..., QK_NOPE:], g_qn_rope), cos, sin)
    k_nope = rmsnorm(k_nope, g_kn_nope)
    k_pe = apply_rope(rmsnorm(k_pe.reshape(b, s, 1, QK_ROPE), g_kn_rope), cos, sin)
    qf = jnp.concatenate([q_nope, q_pe], axis=-1)
    kf = jnp.concatenate([k_nope, jnp.broadcast_to(k_pe, (b, s, N_HEADS, QK_ROPE))], axis=-1)
    return causal_block_attention(qf, kf, v)


def short_conv_group(gate_b, gate_c, x_in, conv_w):
    u = gate_c * x_in
    s = u.shape[1]
    up = jnp.pad(u, ((0, 0), (CONV_TAPS - 1, 0), (0, 0)))
    y = sum(conv_w[j] * up[:, CONV_TAPS - 1 - j: CONV_TAPS - 1 - j + s] for j in range(CONV_TAPS))
    return gate_b * y


def _fwd_setup_inputs(seed: int = 0) -> dict:
    key = jax.random.key(seed)
    ks = jax.random.split(key, 24)

    def w(k, shape, fan_in):
        return jax.random.normal(k, (DEPTH,) + shape, jnp.float32) * fan_in ** -0.5

    def gain(k, n):
        return 1.0 + 0.02 * jax.random.normal(k, (DEPTH, n), jnp.float32)

    x = jax.random.normal(ks[0], (BATCH, SEQ, D_MODEL), jnp.float32)
    p = jax.random.normal(ks[1], (DEPTH, BATCH, SEQ, PLE_DIM), jnp.float32)
    offs = jax.random.randint(ks[2], (BATCH, 1), 0, MAX_POS_OFFSET, dtype=jnp.int32)
    positions = (offs + jnp.arange(SEQ, dtype=jnp.int32)[None, :]).astype(jnp.int32)
    return {
        "x": x,
        "p": p,
        "positions": positions,
        "g_mix": gain(ks[3], D_MODEL),
        "w_in": w(ks[4], (D_MODEL, IN_COLS), D_MODEL),
        "g_q_lat": gain(ks[5], Q_LORA),
        "w_uq": w(ks[6], (Q_LORA, N_HEADS * QK_HEAD), Q_LORA),
        "g_kv_lat": gain(ks[7], KV_LORA),
        "w_ukv": w(ks[8], (KV_LORA, N_HEADS * (QK_NOPE + V_HEAD)), KV_LORA),
        "g_qn_nope": gain(ks[9], QK_NOPE),
        "g_qn_rope": gain(ks[10], QK_ROPE),
        "g_kn_nope": gain(ks[11], QK_NOPE),
        "g_kn_rope": gain(ks[12], QK_ROPE),
        "conv_w": w(ks[13], (CONV_TAPS, CONV_WIDTH), CONV_TAPS),
        "g_out_attn": gain(ks[14], ATTN_WIDTH),
        "g_out_conv": gain(ks[15], CONV_WIDTH),
        "w_o": w(ks[16], (D_MODEL, D_MODEL), D_MODEL),
        "g_mlp": gain(ks[17], D_MODEL),
        "w_up": w(ks[18], (D_MODEL, D_FF), D_MODEL),
        "w_down": w(ks[19], (D_FF, D_MODEL), D_FF),
        "g_ple": gain(ks[20], D_MODEL),
        "w_ple_gate": w(ks[21], (D_MODEL, D_MODEL), D_MODEL),
        "w_ple": w(ks[22], (PLE_DIM, D_MODEL), PLE_DIM),
    }


def _fwd_reference(x, p, positions, g_mix, w_in, g_q_lat, w_uq, g_kv_lat, w_ukv,
              g_qn_nope, g_qn_rope, g_kn_nope, g_kn_rope, conv_w, g_out_attn,
              g_out_conv, w_o, g_mlp, w_up, w_down, g_ple, w_ple_gate, w_ple):
    cos, sin = rope_tables(positions)
    o1 = Q_LORA
    o2 = o1 + KV_LORA
    o3 = o2 + QK_ROPE
    o4 = o3 + CONV_WIDTH
    o5 = o4 + CONV_WIDTH
    for i in range(DEPTH):
        h = rmsnorm(x, g_mix[i])
        z = h @ w_in[i]
        attn = mla_group(z[..., :o1], z[..., o1:o2], z[..., o2:o3], cos, sin,
                         g_q_lat[i], w_uq[i], g_kv_lat[i], w_ukv[i],
                         g_qn_nope[i], g_qn_rope[i], g_kn_nope[i], g_kn_rope[i])
        conv = short_conv_group(z[..., o3:o4], z[..., o4:o5], z[..., o5:], conv_w[i])
        mixed = jnp.concatenate([rmsnorm(attn, g_out_attn[i]), rmsnorm(conv, g_out_conv[i])], axis=-1)
        x = x + mixed @ w_o[i]
        h2 = rmsnorm(x, g_mlp[i])
        x = x + jnp.square(jax.nn.relu(h2 @ w_up[i])) @ w_down[i]
        gate = jax.nn.sigmoid(rmsnorm(x, g_ple[i]) @ w_ple_gate[i])
        x = x + gate * (p[i] @ w_ple[i])
    return x


import jax as _jax
import jax.numpy as _jnp

TWIN_FORMAT = 'train_step'
FWD_PARAMS = ['x', 'p', 'positions', 'g_mix', 'w_in', 'g_q_lat', 'w_uq', 'g_kv_lat', 'w_ukv', 'g_qn_nope', 'g_qn_rope', 'g_kn_nope', 'g_kn_rope', 'conv_w', 'g_out_attn', 'g_out_conv', 'w_o', 'g_mlp', 'w_up', 'w_down', 'g_ple', 'w_ple_gate', 'w_ple']
TWIN_WEIGHTS = ['g_mix', 'w_in', 'g_q_lat', 'w_uq', 'g_kv_lat', 'w_ukv', 'g_qn_nope', 'g_qn_rope', 'g_kn_nope', 'g_kn_rope', 'conv_w', 'g_out_attn', 'g_out_conv', 'w_o', 'g_mlp', 'w_up', 'w_down', 'g_ple', 'w_ple_gate', 'w_ple']
TWIN_DIFF_INPUT = 'x'
TWIN_INPUTS = ['x', 'p', 'positions', 'g_mix', 'w_in', 'g_q_lat', 'w_uq', 'g_kv_lat', 'w_ukv', 'g_qn_nope', 'g_qn_rope', 'g_kn_nope', 'g_kn_rope', 'conv_w', 'g_out_attn', 'g_out_conv', 'w_o', 'g_mlp', 'w_up', 'w_down', 'g_ple', 'w_ple_gate', 'w_ple', 'loss_target', 'm_g_mix', 'm_w_in', 'm_g_q_lat', 'm_w_uq', 'm_g_kv_lat', 'm_w_ukv', 'm_g_qn_nope', 'm_g_qn_rope', 'm_g_kn_nope', 'm_g_kn_rope', 'm_conv_w', 'm_g_out_attn', 'm_g_out_conv', 'm_w_o', 'm_g_mlp', 'm_w_up', 'm_w_down', 'm_g_ple', 'm_w_ple_gate', 'm_w_ple', 'v_g_mix', 'v_w_in', 'v_g_q_lat', 'v_w_uq', 'v_g_kv_lat', 'v_w_ukv', 'v_g_qn_nope', 'v_g_qn_rope', 'v_g_kn_nope', 'v_g_kn_rope', 'v_conv_w', 'v_g_out_attn', 'v_g_out_conv', 'v_w_o', 'v_g_mlp', 'v_w_up', 'v_w_down', 'v_g_ple', 'v_w_ple_gate', 'v_w_ple']
TWIN_OUTPUTS = ['loss', 'grad_x', 'grad_g_mix', 'grad_w_in', 'grad_g_q_lat', 'grad_w_uq', 'grad_g_kv_lat', 'grad_w_ukv', 'grad_g_qn_nope', 'grad_g_qn_rope', 'grad_g_kn_nope', 'grad_g_kn_rope', 'grad_conv_w', 'grad_g_out_attn', 'grad_g_out_conv', 'grad_w_o', 'grad_g_mlp', 'grad_w_up', 'grad_w_down', 'grad_g_ple', 'grad_w_ple_gate', 'grad_w_ple', 'delta_g_mix', 'delta_w_in', 'delta_g_q_lat', 'delta_w_uq', 'delta_g_kv_lat', 'delta_w_ukv', 'delta_g_qn_nope', 'delta_g_qn_rope', 'delta_g_kn_nope', 'delta_g_kn_rope', 'delta_conv_w', 'delta_g_out_attn', 'delta_g_out_conv', 'delta_w_o', 'delta_g_mlp', 'delta_w_up', 'delta_w_down', 'delta_g_ple', 'delta_w_ple_gate', 'delta_w_ple', 'new_m_g_mix', 'new_m_w_in', 'new_m_g_q_lat', 'new_m_w_uq', 'new_m_g_kv_lat', 'new_m_w_ukv', 'new_m_g_qn_nope', 'new_m_g_qn_rope', 'new_m_g_kn_nope', 'new_m_g_kn_rope', 'new_m_conv_w', 'new_m_g_out_attn', 'new_m_g_out_conv', 'new_m_w_o', 'new_m_g_mlp', 'new_m_w_up', 'new_m_w_down', 'new_m_g_ple', 'new_m_w_ple_gate', 'new_m_w_ple', 'new_v_g_mix', 'new_v_w_in', 'new_v_g_q_lat', 'new_v_w_uq', 'new_v_g_kv_lat', 'new_v_w_ukv', 'new_v_g_qn_nope', 'new_v_g_qn_rope', 'new_v_g_kn_nope', 'new_v_g_kn_rope', 'new_v_conv_w', 'new_v_g_out_attn', 'new_v_g_out_conv', 'new_v_w_o', 'new_v_g_mlp', 'new_v_w_up', 'new_v_w_down', 'new_v_g_ple', 'new_v_w_ple_gate', 'new_v_w_ple']
TWIN_LEAF_KINDS = {'loss': 'loss', 'grad_x': 'grad_x', 'grad_g_mix': 'grad_w', 'grad_w_in': 'grad_w', 'grad_g_q_lat': 'grad_w', 'grad_w_uq': 'grad_w', 'grad_g_kv_lat': 'grad_w', 'grad_w_ukv': 'grad_w', 'grad_g_qn_nope': 'grad_w', 'grad_g_qn_rope': 'grad_w', 'grad_g_kn_nope': 'grad_w', 'grad_g_kn_rope': 'grad_w', 'grad_conv_w': 'grad_w', 'grad_g_out_attn': 'grad_w', 'grad_g_out_conv': 'grad_w', 'grad_w_o': 'grad_w', 'grad_g_mlp': 'grad_w', 'grad_w_up': 'grad_w', 'grad_w_down': 'grad_w', 'grad_g_ple': 'grad_w', 'grad_w_ple_gate': 'grad_w', 'grad_w_ple': 'grad_w', 'delta_g_mix': 'delta_w', 'delta_w_in': 'delta_w', 'delta_g_q_lat': 'delta_w', 'delta_w_uq': 'delta_w', 'delta_g_kv_lat': 'delta_w', 'delta_w_ukv': 'delta_w', 'delta_g_qn_nope': 'delta_w', 'delta_g_qn_rope': 'delta_w', 'delta_g_kn_nope': 'delta_w', 'delta_g_kn_rope': 'delta_w', 'delta_conv_w': 'delta_w', 'delta_g_out_attn': 'delta_w', 'delta_g_out_conv': 'delta_w', 'delta_w_o': 'delta_w', 'delta_g_mlp': 'delta_w', 'delta_w_up': 'delta_w', 'delta_w_down': 'delta_w', 'delta_g_ple': 'delta_w', 'delta_w_ple_gate': 'delta_w', 'delta_w_ple': 'delta_w', 'new_m_g_mix': 'new_m', 'new_m_w_in': 'new_m', 'new_m_g_q_lat': 'new_m', 'new_m_w_uq': 'new_m', 'new_m_g_kv_lat': 'new_m', 'new_m_w_ukv': 'new_m', 'new_m_g_qn_nope': 'new_m', 'new_m_g_qn_rope': 'new_m', 'new_m_g_kn_nope': 'new_m', 'new_m_g_kn_rope': 'new_m', 'new_m_conv_w': 'new_m', 'new_m_g_out_attn': 'new_m', 'new_m_g_out_conv': 'new_m', 'new_m_w_o': 'new_m', 'new_m_g_mlp': 'new_m', 'new_m_w_up': 'new_m', 'new_m_w_down': 'new_m', 'new_m_g_ple': 'new_m', 'new_m_w_ple_gate': 'new_m', 'new_m_w_ple': 'new_m', 'new_v_g_mix': 'new_v', 'new_v_w_in': 'new_v', 'new_v_g_q_lat': 'new_v', 'new_v_w_uq': 'new_v', 'new_v_g_kv_lat': 'new_v', 'new_v_w_ukv': 'new_v', 'new_v_g_qn_nope': 'new_v', 'new_v_g_qn_rope': 'new_v', 'new_v_g_kn_nope': 'new_v', 'new_v_g_kn_rope': 'new_v', 'new_v_conv_w': 'new_v', 'new_v_g_out_attn': 'new_v', 'new_v_g_out_conv': 'new_v', 'new_v_w_o': 'new_v', 'new_v_g_mlp': 'new_v', 'new_v_w_up': 'new_v', 'new_v_w_down': 'new_v', 'new_v_g_ple': 'new_v', 'new_v_w_ple_gate': 'new_v', 'new_v_w_ple': 'new_v'}


def _forward(args):
    return _fwd_reference(*[args[k] for k in FWD_PARAMS])


def _output_shape():
    out = _jax.eval_shape(lambda: _forward(_fwd_setup_inputs(0)))
    return out.shape, out.dtype

N_MICROBATCH = 1
ADAM_LR = 0.001
ADAM_B1 = 0.9
ADAM_B2 = 0.999
ADAM_EPS = 1e-08
ADAM_WD = 0.01
ADAM_STEP = 10
PER_EXAMPLE_BATCH_AXIS = {'x': 0, 'p': 1, 'positions': 0, 'loss_target': 0}
SHARED_INPUTS = []
_WEIGHT_DTYPES = {'g_mix': _jnp.float32, 'w_in': _jnp.float32, 'g_q_lat': _jnp.float32, 'w_uq': _jnp.float32, 'g_kv_lat': _jnp.float32, 'w_ukv': _jnp.float32, 'g_qn_nope': _jnp.float32, 'g_qn_rope': _jnp.float32, 'g_kn_nope': _jnp.float32, 'g_kn_rope': _jnp.float32, 'conv_w': _jnp.float32, 'g_out_attn': _jnp.float32, 'g_out_conv': _jnp.float32, 'w_o': _jnp.float32, 'g_mlp': _jnp.float32, 'w_up': _jnp.float32, 'w_down': _jnp.float32, 'g_ple': _jnp.float32, 'w_ple_gate': _jnp.float32, 'w_ple': _jnp.float32}
MOMENT_SCALE = {'g_mix': 3.303652e+01, 'w_in': 2.285469e+01, 'g_q_lat': 2.538581e+00, 'w_uq': 1.823193e+00, 'g_kv_lat': 6.241906e+01, 'w_ukv': 3.257452e+01, 'g_qn_nope': 7.327508e+00, 'g_qn_rope': 3.490857e+00, 'g_kn_nope': 7.318968e+00, 'g_kn_rope': 6.731848e+00, 'conv_w': 2.484591e+00, 'g_out_attn': 6.085636e+01, 'g_out_conv': 4.189395e+01, 'w_o': 3.318091e+01, 'g_mlp': 1.007053e+02, 'w_up': 1.111999e+01, 'w_down': 4.009632e+01, 'g_ple': 1.374755e+00, 'w_ple_gate': 1.017262e+00, 'w_ple': 7.851031e-01}


def _to_microbatches(a, axis):
    t = _jnp.moveaxis(a, axis, 0)
    t = t.reshape((N_MICROBATCH, t.shape[0] // N_MICROBATCH) + t.shape[1:])
    return _jnp.moveaxis(t, 1, axis + 1)


def setup_inputs(seed: int = 0) -> dict:
    inp = _fwd_setup_inputs(seed)
    key = _jax.random.fold_in(_jax.random.key(seed), 7919)
    shape, _ = _output_shape()
    out = dict(inp)
    out["loss_target"] = _jax.random.normal(_jax.random.fold_in(key, 0), shape, _jnp.float32)
    for i, name in enumerate(TWIN_WEIGHTS):
        w = inp[name].astype(_jnp.float32)
        if MOMENT_SCALE is None:
            s = _jnp.sqrt(_jnp.mean(_jnp.square(w)) + 1e-30)
        else:
            s = MOMENT_SCALE[name]
        km, kv = _jax.random.split(_jax.random.fold_in(key, i + 1))
        out[name] = w
        out["m_" + name] = s * _jax.random.normal(km, w.shape, _jnp.float32)
        out["v_" + name] = (s * s) * _jax.random.uniform(kv, w.shape, _jnp.float32, 0.5, 1.5)
    if N_MICROBATCH > 1:
        for name, axis in PER_EXAMPLE_BATCH_AXIS.items():
            out[name] = _to_microbatches(out[name], axis)
    return {'x': out['x'], 'p': out['p'], 'positions': out['positions'], 'g_mix': out['g_mix'], 'w_in': out['w_in'], 'g_q_lat': out['g_q_lat'], 'w_uq': out['w_uq'], 'g_kv_lat': out['g_kv_lat'], 'w_ukv': out['w_ukv'], 'g_qn_nope': out['g_qn_nope'], 'g_qn_rope': out['g_qn_rope'], 'g_kn_nope': out['g_kn_nope'], 'g_kn_rope': out['g_kn_rope'], 'conv_w': out['conv_w'], 'g_out_attn': out['g_out_attn'], 'g_out_conv': out['g_out_conv'], 'w_o': out['w_o'], 'g_mlp': out['g_mlp'], 'w_up': out['w_up'], 'w_down': out['w_down'], 'g_ple': out['g_ple'], 'w_ple_gate': out['w_ple_gate'], 'w_ple': out['w_ple'], 'loss_target': out['loss_target'], 'm_g_mix': out['m_g_mix'], 'm_w_in': out['m_w_in'], 'm_g_q_lat': out['m_g_q_lat'], 'm_w_uq': out['m_w_uq'], 'm_g_kv_lat': out['m_g_kv_lat'], 'm_w_ukv': out['m_w_ukv'], 'm_g_qn_nope': out['m_g_qn_nope'], 'm_g_qn_rope': out['m_g_qn_rope'], 'm_g_kn_nope': out['m_g_kn_nope'], 'm_g_kn_rope': out['m_g_kn_rope'], 'm_conv_w': out['m_conv_w'], 'm_g_out_attn': out['m_g_out_attn'], 'm_g_out_conv': out['m_g_out_conv'], 'm_w_o': out['m_w_o'], 'm_g_mlp': out['m_g_mlp'], 'm_w_up': out['m_w_up'], 'm_w_down': out['m_w_down'], 'm_g_ple': out['m_g_ple'], 'm_w_ple_gate': out['m_w_ple_gate'], 'm_w_ple': out['m_w_ple'], 'v_g_mix': out['v_g_mix'], 'v_w_in': out['v_w_in'], 'v_g_q_lat': out['v_g_q_lat'], 'v_w_uq': out['v_w_uq'], 'v_g_kv_lat': out['v_g_kv_lat'], 'v_w_ukv': out['v_w_ukv'], 'v_g_qn_nope': out['v_g_qn_nope'], 'v_g_qn_rope': out['v_g_qn_rope'], 'v_g_kn_nope': out['v_g_kn_nope'], 'v_g_kn_rope': out['v_g_kn_rope'], 'v_conv_w': out['v_conv_w'], 'v_g_out_attn': out['v_g_out_attn'], 'v_g_out_conv': out['v_g_out_conv'], 'v_w_o': out['v_w_o'], 'v_g_mlp': out['v_g_mlp'], 'v_w_up': out['v_w_up'], 'v_w_down': out['v_w_down'], 'v_g_ple': out['v_g_ple'], 'v_w_ple_gate': out['v_w_ple_gate'], 'v_w_ple': out['v_w_ple']}


def _loss(weights, diff, rest, loss_target):
    with _jax.named_scope("forward"):
        args = {**rest, TWIN_DIFF_INPUT: diff, **{k: w.astype(_WEIGHT_DTYPES[k]) for k, w in weights.items()}}
        y = _forward(args)
    with _jax.named_scope("loss_head"):
        err = _jnp.square(y.astype(_jnp.float32) - loss_target)
        return 0.5 * _jnp.sum(_jnp.mean(err, axis=-1)) if err.ndim else 0.5 * err


def _adamw(w, g, m, v):
    m = ADAM_B1 * m + (1.0 - ADAM_B1) * g
    v = ADAM_B2 * v + (1.0 - ADAM_B2) * _jnp.square(g)
    m_hat = m / (1.0 - ADAM_B1 ** ADAM_STEP)
    v_hat = v / (1.0 - ADAM_B2 ** ADAM_STEP)
    delta = -ADAM_LR * (m_hat / (_jnp.sqrt(v_hat) + ADAM_EPS) + ADAM_WD * w)
    return delta, m, v


def reference(x, p, positions, g_mix, w_in, g_q_lat, w_uq, g_kv_lat, w_ukv, g_qn_nope, g_qn_rope, g_kn_nope, g_kn_rope, conv_w, g_out_attn, g_out_conv, w_o, g_mlp, w_up, w_down, g_ple, w_ple_gate, w_ple, loss_target, m_g_mix, m_w_in, m_g_q_lat, m_w_uq, m_g_kv_lat, m_w_ukv, m_g_qn_nope, m_g_qn_rope, m_g_kn_nope, m_g_kn_rope, m_conv_w, m_g_out_attn, m_g_out_conv, m_w_o, m_g_mlp, m_w_up, m_w_down, m_g_ple, m_w_ple_gate, m_w_ple, v_g_mix, v_w_in, v_g_q_lat, v_w_uq, v_g_kv_lat, v_w_ukv, v_g_qn_nope, v_g_qn_rope, v_g_kn_nope, v_g_kn_rope, v_conv_w, v_g_out_attn, v_g_out_conv, v_w_o, v_g_mlp, v_w_up, v_w_down, v_g_ple, v_w_ple_gate, v_w_ple):
    given = dict(x=x, p=p, positions=positions, g_mix=g_mix, w_in=w_in, g_q_lat=g_q_lat, w_uq=w_uq, g_kv_lat=g_kv_lat, w_ukv=w_ukv, g_qn_nope=g_qn_nope, g_qn_rope=g_qn_rope, g_kn_nope=g_kn_nope, g_kn_rope=g_kn_rope, conv_w=conv_w, g_out_attn=g_out_attn, g_out_conv=g_out_conv, w_o=w_o, g_mlp=g_mlp, w_up=w_up, w_down=w_down, g_ple=g_ple, w_ple_gate=w_ple_gate, w_ple=w_ple, loss_target=loss_target, m_g_mix=m_g_mix, m_w_in=m_w_in, m_g_q_lat=m_g_q_lat, m_w_uq=m_w_uq, m_g_kv_lat=m_g_kv_lat, m_w_ukv=m_w_ukv, m_g_qn_nope=m_g_qn_nope, m_g_qn_rope=m_g_qn_rope, m_g_kn_nope=m_g_kn_nope, m_g_kn_rope=m_g_kn_rope, m_conv_w=m_conv_w, m_g_out_attn=m_g_out_attn, m_g_out_conv=m_g_out_conv, m_w_o=m_w_o, m_g_mlp=m_g_mlp, m_w_up=m_w_up, m_w_down=m_w_down, m_g_ple=m_g_ple, m_w_ple_gate=m_w_ple_gate, m_w_ple=m_w_ple, v_g_mix=v_g_mix, v_w_in=v_w_in, v_g_q_lat=v_g_q_lat, v_w_uq=v_w_uq, v_g_kv_lat=v_g_kv_lat, v_w_ukv=v_w_ukv, v_g_qn_nope=v_g_qn_nope, v_g_qn_rope=v_g_qn_rope, v_g_kn_nope=v_g_kn_nope, v_g_kn_rope=v_g_kn_rope, v_conv_w=v_conv_w, v_g_out_attn=v_g_out_attn, v_g_out_conv=v_g_out_conv, v_w_o=v_w_o, v_g_mlp=v_g_mlp, v_w_up=v_w_up, v_w_down=v_w_down, v_g_ple=v_g_ple, v_w_ple_gate=v_w_ple_gate, v_w_ple=v_w_ple)
    weights = {n: given[n] for n in TWIN_WEIGHTS}
    shared = {n: given[n] for n in SHARED_INPUTS}
    per_example = {n: given[n] for n in ['x', 'p', 'positions']}
    grad_fn = _jax.value_and_grad(_loss, argnums=(0, 1))

    def one_microbatch(ex, loss_target):
        ex = dict(ex)
        diff = ex.pop(TWIN_DIFF_INPUT)
        return grad_fn(weights, diff, {**shared, **ex}, loss_target)

    if N_MICROBATCH == 1:
        loss, (grad_w, grad_x) = one_microbatch(per_example, given["loss_target"])
    else:
        def body(carry, xs):
            loss_sum, grad_sum = carry
            l_k, (gw_k, gx_k) = one_microbatch(xs[0], xs[1])
            with _jax.named_scope("update"):
                return (loss_sum + l_k, _jax.tree.map(_jnp.add, grad_sum, gw_k)), gx_k

        init = (_jnp.zeros((), _jnp.float32), _jax.tree.map(_jnp.zeros_like, weights))
        (loss, grad_w), grad_x = _jax.lax.scan(body, init, (per_example, given["loss_target"]))
    with _jax.named_scope("update"):
        delta_w, new_m, new_v = {}, {}, {}
        for n in TWIN_WEIGHTS:
            delta_w[n], new_m[n], new_v[n] = _adamw(weights[n], grad_w[n], given["m_" + n], given["v_" + n])
    return (loss, grad_x, *[grad_w[n] for n in TWIN_WEIGHTS], *[delta_w[n] for n in TWIN_WEIGHTS],
            *[new_m[n] for n in TWIN_WEIGHTS], *[new_v[n] for n in TWIN_WEIGHTS])
```

```python
import jax
import jax.numpy as jnp
from jax import lax
from jax.experimental import pallas as pl
from jax.experimental.pallas import tpu as pltpu

F32 = jnp.float32
BF16 = jnp.bfloat16
MESH = pl.DeviceIdType.MESH

N_HEADS = 8
QK_NOPE = 64
QK_ROPE = 32
QK_HEAD = QK_NOPE + QK_ROPE
V_HEAD = 64
HEAD_PAD = 128
Q_LORA = 384
KV_LORA = 256
CONV_WIDTH = 512
ATTN_WIDTH = N_HEADS * V_HEAD
ROPE_THETA = 10000.0
EPS = 1e-6
ADAM_LR, ADAM_B1, ADAM_B2, ADAM_EPS, ADAM_WD, ADAM_STEP = 0.001, 0.9, 0.999, 1e-08, 0.01, 10

Z_Q = (0, 384)
Z_KV = (384, 640)
Z_GB = (640, 1152)
Z_GC = (1152, 1664)
Z_XIN = (1664, 2176)
Z_KPE = (2176, 2304)
Z_COLS = 2304

N_DEV = 8
N_CHIP = 4
LANES = 128
V7X_VMEM_LIMIT = 52 * 1024 * 1024
TOKEN_TILE = 256
ATTN_BLOCK = 256
ROW_BLOCK = 512
HALO = 8

GAIN_NAMES = ("g_mix", "g_q_lat", "g_kv_lat", "g_qn_nope", "g_qn_rope", "g_kn_nope", "g_kn_rope",
              "g_out_attn", "g_out_conv", "g_mlp", "g_ple")
SHARD_NAMES = ("w_in", "w_uq", "w_ukv", "conv_w", "w_o", "w_up", "w_down", "w_ple_gate", "w_ple")
WEIGHT_NAMES = ("g_mix", "w_in", "g_q_lat", "w_uq", "g_kv_lat", "w_ukv", "g_qn_nope", "g_qn_rope", "g_kn_nope",
                "g_kn_rope", "conv_w", "g_out_attn", "g_out_conv", "w_o", "g_mlp", "w_up", "w_down", "g_ple",
                "w_ple_gate", "w_ple")


def _cparams(semantics=None):
    return pltpu.CompilerParams(dimension_semantics=semantics, vmem_limit_bytes=V7X_VMEM_LIMIT)


def _mm(a, b):
    return jnp.dot(a, b, preferred_element_type=F32)


def _mm_nt(a, b):
    return lax.dot_general(a, b, (((1,), (1,)), ((), ())), preferred_element_type=F32)


def _mm_tn(a, b):
    return lax.dot_general(a, b, (((0,), (0,)), ((), ())), preferred_element_type=F32)


def _rms(x, g):
    r = lax.rsqrt(jnp.mean(x * x, axis=-1, keepdims=True) + EPS)
    return (x * r) * g


def _rms_bwd(x, g, dy):
    r = lax.rsqrt(jnp.mean(x * x, axis=-1, keepdims=True) + EPS)
    xh = x * r
    dg = jnp.sum(dy * xh, axis=0, keepdims=True)
    dyg = dy * g
    dx = r * (dyg - xh * jnp.mean(dyg * xh, axis=-1, keepdims=True))
    return dx, dg


def _group_mean(t, gm):
    hi = t.astype(BF16)
    lo = (t - hi.astype(F32)).astype(BF16)
    return _mm(hi, gm) + _mm(lo, gm)


def _swap_rope_halves(x, lane):
    half = QK_ROPE // 2
    swapped = jnp.where(lane < QK_NOPE + half, pltpu.roll(x, LANES - half, 1), pltpu.roll(x, half, 1))
    return jnp.where((lane >= QK_NOPE) & (lane < QK_HEAD), swapped, 0.0)


def _qk_fwd(x, g, cos, sin, gm, lane):
    r = lax.rsqrt(_group_mean(x * x, gm) + EPS)
    n = (x * r) * g
    return n * cos + _swap_rope_halves(n, lane) * sin


def _qk_bwd(x, g, dy, cos, sin, gm, lane):
    r = lax.rsqrt(_group_mean(x * x, gm) + EPS)
    xh = x * r
    dn = dy * cos + _swap_rope_halves(dy * sin, lane)
    dg = jnp.sum(dn * xh, axis=0, keepdims=True)
    dng = dn * g
    dx = r * (dng - xh * _group_mean(dng * xh, gm))
    return dx, dg


def _rows(tm, n):
    return pl.BlockSpec((tm, n), lambda i: (i, 0))


def _whole(shape):
    zeros = (0,) * len(shape)
    return pl.BlockSpec(shape, lambda i: zeros)


def _of_layer(gathered, layer):
    n, _, a, b = gathered.shape
    return gathered, pl.BlockSpec((n, None, a, b), lambda i: (0, layer, 0, 0))


def _operands(values):
    arrays, specs = [], []
    for v in values:
        arr, spec = v if isinstance(v, tuple) else (v, _whole(v.shape))
        arrays.append(arr)
        specs.append(spec)
    return arrays, specs


def _accumulate(ref, first, value):
    @pl.when(first)
    def _():
        ref[...] = jnp.zeros_like(ref)
    ref[...] += value


def _front_fwd(x, w, tabs):
    t_len, d = x.shape
    tm = min(TOKEN_TILE, t_len)
    hp = N_HEADS * HEAD_PAD

    def body(x_ref, gmix, win, gq, wuq, gkv, wukv, gqn, gkn, cw_ref, gm_ref, cos_ref, sin_ref,
             z_ref, qf_ref, kf_ref, kv_ref, conv_ref, ubuf):
        i = pl.program_id(0)
        h = _rms(x_ref[...], gmix[...])
        z = _mm(h.astype(BF16), win[...])
        z_ref[...] = z
        qnb = _rms(z[:, Z_Q[0]:Z_Q[1]], gq[...]).astype(BF16)
        kvb = _rms(z[:, Z_KV[0]:Z_KV[1]], gkv[...]).astype(BF16)
        kpe = z[:, Z_KPE[0]:Z_KPE[1]]
        cos, sin, gm = cos_ref[...], sin_ref[...], gm_ref[...]
        lane = lax.broadcasted_iota(jnp.int32, (tm, LANES), 1)
        for hd in range(N_HEADS):
            sl = slice(hd * HEAD_PAD, (hd + 1) * HEAD_PAD)
            qf_ref[:, sl] = _qk_fwd(_mm(qnb, wuq[hd]), gqn[...], cos, sin, gm, lane).astype(BF16)
            kv = _mm(kvb, wukv[hd])
            kv_ref[:, sl] = kv.astype(BF16)
            kf_ref[:, sl] = _qk_fwd(jnp.where(lane < QK_NOPE, kv, 0.0) + kpe, gkn[...], cos, sin, gm, lane).astype(BF16)
        u = z[:, Z_GC[0]:Z_GC[1]] * z[:, Z_XIN[0]:Z_XIN[1]]

        @pl.when(i == 0)
        def _():
            ubuf[0:HALO, :] = jnp.zeros((HALO, CONV_WIDTH), F32)
        ubuf[HALO:HALO + tm, :] = u
        cw = cw_ref[...]
        y = cw[0:1] * u + cw[1:2] * ubuf[pl.ds(HALO - 1, tm), :] + cw[2:3] * ubuf[pl.ds(HALO - 2, tm), :]
        conv_ref[...] = z[:, Z_GB[0]:Z_GB[1]] * y
        ubuf[0:HALO, :] = u[tm - HALO:tm, :]

    consts, const_specs = _operands([w["g_mix"], w["w_in"], w["g_q_lat"], w["w_uq"], w["g_kv_lat"], w["w_ukv"],
                                     w["g_qn"], w["g_kn"], w["conv_w"], tabs["gm"]])
    out_shape = (jax.ShapeDtypeStruct((t_len, Z_COLS), F32), jax.ShapeDtypeStruct((t_len, hp), BF16),
                 jax.ShapeDtypeStruct((t_len, hp), BF16), jax.ShapeDtypeStruct((t_len, hp), BF16),
                 jax.ShapeDtypeStruct((t_len, CONV_WIDTH), F32))
    return pl.pallas_call(body, name="front_fwd", grid=(t_len // tm,),
                          in_specs=[_rows(tm, d)] + const_specs + [_rows(tm, LANES), _rows(tm, LANES)],
                          out_specs=tuple(_rows(tm, s.shape[1]) for s in out_shape), out_shape=out_shape,
                          scratch_shapes=[pltpu.VMEM((tm + HALO, CONV_WIDTH), F32)],
                          compiler_params=_cparams(("arbitrary",)))(x, *consts, tabs["cos"], tabs["sin"])


def _attn_fwd(qf, kf, kv):
    t_len = qf.shape[0]
    blk = min(ATTN_BLOCK, t_len)
    nb = t_len // blk
    scale = QK_HEAD ** -0.5

    def body(q_ref, k_ref, kv_ref, o_ref, lse_ref):
        hd = pl.program_id(0)
        lane = lax.broadcasted_iota(jnp.int32, (blk, LANES), 1)
        even = (lane * 0 + hd % 2) == 0
        mine = jnp.where(lane < V_HEAD, 0, 1) == hd % 2
        row = lax.broadcasted_iota(jnp.int32, (blk, blk), 0)
        col = lax.broadcasted_iota(jnp.int32, (blk, blk), 1)

        def qblock(i, carry):
            qs = pl.ds(pl.multiple_of(i * blk, blk), blk)
            q = q_ref[qs, :]

            def step(j, state, diagonal):
                m, l, acc = state
                ks = pl.ds(pl.multiple_of(j * blk, blk), blk)
                s = _mm_nt(q, k_ref[ks, :]) * scale
                if diagonal:
                    s = jnp.where(col <= row, s, -jnp.inf)
                m_new = jnp.maximum(m, jnp.max(s, axis=-1, keepdims=True))
                p = jnp.exp(s - m_new)
                alpha = jnp.exp(m - m_new)
                l = alpha * l + jnp.sum(p, axis=-1, keepdims=True)
                acc = alpha * acc + _mm(p.astype(BF16), kv_ref[ks, :])
                return m_new, l, acc

            init = (jnp.full((blk, 1), -jnp.inf, F32), jnp.zeros((blk, 1), F32), jnp.zeros((blk, LANES), F32))
            state = lax.fori_loop(0, i, lambda j, st: step(j, st, False), init)
            m, l, acc = step(i, state, True)
            out = acc / l
            out = jnp.where(mine, jnp.where(even, pltpu.roll(out, V_HEAD, 1), out), 0.0)

            @pl.when(hd % 2 == 0)
            def _():
                o_ref[qs, :] = out

            @pl.when(hd % 2 == 1)
            def _():
                o_ref[qs, :] += out
            lse_ref[qs, :] = jnp.broadcast_to(m + jnp.log(l), (blk, LANES))
            return carry

        lax.fori_loop(0, nb, qblock, 0)

    head = pl.BlockSpec((t_len, HEAD_PAD), lambda h: (0, h))
    pair = pl.BlockSpec((t_len, 2 * V_HEAD), lambda h: (0, h // 2))
    return pl.pallas_call(body, name="attn_fwd", grid=(N_HEADS,), in_specs=[head, head, head], out_specs=(pair, head),
                          out_shape=(jax.ShapeDtypeStruct((t_len, ATTN_WIDTH), F32),
                                     jax.ShapeDtypeStruct((t_len, N_HEADS * LANES), F32)),
                          compiler_params=_cparams(("arbitrary",)))(qf, kf, kv)


def _mix_out_fwd(x, attn, conv, w):
    t_len, d = x.shape
    tm = min(TOKEN_TILE, t_len)

    def body(x_ref, a_ref, c_ref, goa, goc, wo, x1_ref):
        mixed = jnp.concatenate([_rms(a_ref[...], goa[...]), _rms(c_ref[...], goc[...])], axis=1)
        x1_ref[...] = x_ref[...] + _mm(mixed.astype(BF16), wo[...])

    consts, const_specs = _operands([w["g_out_attn"], w["g_out_conv"], w["w_o"]])
    return pl.pallas_call(body, name="mix_out_fwd", grid=(t_len // tm,),
                          in_specs=[_rows(tm, d), _rows(tm, ATTN_WIDTH), _rows(tm, CONV_WIDTH)] + const_specs,
                          out_specs=_rows(tm, d), out_shape=jax.ShapeDtypeStruct((t_len, d), F32),
                          compiler_params=_cparams(("parallel",)))(x, attn, conv, *consts)


def _mlp_fwd(x1, w):
    t_len, d = x1.shape
    tm = min(TOKEN_TILE, t_len)

    def body(x_ref, g, wup, wdn, x2_ref):
        x1v = x_ref[...]
        hb = _rms(x1v, g[...]).astype(BF16)
        acc = x1v
        for k in range(N_DEV):
            a = jnp.maximum(_mm(hb, wup[k]), 0.0)
            acc = acc + _mm((a * a).astype(BF16), wdn[k])
        x2_ref[...] = acc

    consts, const_specs = _operands([w["g_mlp"], w["w_up"], w["w_down"]])
    return pl.pallas_call(body, name="mlp_fwd", grid=(t_len // tm,), in_specs=[_rows(tm, d)] + const_specs,
                          out_specs=_rows(tm, d), out_shape=jax.ShapeDtypeStruct((t_len, d), F32),
                          compiler_params=_cparams(("parallel",)))(x1, *consts)


def _ple_fwd(x2, p, w):
    t_len, d = x2.shape
    tm = min(TOKEN_TILE, t_len)

    def body(x_ref, p_ref, g, wg, wp, x3_ref):
        x2v = x_ref[...]
        gate = jax.nn.sigmoid(_mm(_rms(x2v, g[...]).astype(BF16), wg[...]))
        pb = p_ref[...].astype(BF16)
        e = jnp.concatenate([_mm(pb, wp[k]) for k in range(N_DEV)], axis=1)
        x3_ref[...] = x2v + gate * e

    consts, const_specs = _operands([w["g_ple"], w["w_ple_gate"], w["w_ple"]])
    return pl.pallas_call(body, name="ple_fwd", grid=(t_len // tm,),
                          in_specs=[_rows(tm, d), _rows(tm, p.shape[1])] + const_specs, out_specs=_rows(tm, d),
                          out_shape=jax.ShapeDtypeStruct((t_len, d), F32),
                          compiler_params=_cparams(("parallel",)))(x2, p, *consts)


def _loss_and_grad(y, target):
    t_len, d = y.shape
    tm = min(TOKEN_TILE, t_len)

    def body(y_ref, t_ref, sq_ref, dy_ref):
        err = y_ref[...] - t_ref[...]
        dy_ref[...] = err / d
        total = jnp.sum(jnp.sum(err * err, axis=0, keepdims=True), axis=1, keepdims=True)
        _accumulate(sq_ref, pl.program_id(0) == 0, jnp.broadcast_to(total, (HALO, LANES)))

    return pl.pallas_call(body, name="loss_grad", grid=(t_len // tm,), in_specs=[_rows(tm, d), _rows(tm, d)],
                          out_specs=(_whole((HALO, LANES)), _rows(tm, d)),
                          out_shape=(jax.ShapeDtypeStruct((HALO, LANES), F32), jax.ShapeDtypeStruct((t_len, d), F32)),
                          compiler_params=_cparams(("arbitrary",)))(y, target)


def _ple_bwd(dx3, x2, p, w):
    t_len, d = x2.shape
    tm = min(TOKEN_TILE, t_len)

    def body(dx3_ref, x_ref, p_ref, g, wg, wp, dx2_ref, de_ref, h3_ref, dpre_ref, dg_ref):
        x2v, dx3v = x_ref[...], dx3_ref[...]
        hb = _rms(x2v, g[...]).astype(BF16)
        h3_ref[...] = hb
        gate = jax.nn.sigmoid(_mm(hb, wg[...]))
        pb = p_ref[...].astype(BF16)
        e = jnp.concatenate([_mm(pb, wp[k]) for k in range(N_DEV)], axis=1)
        de_ref[...] = (dx3v * gate).astype(BF16)
        dpre = ((dx3v * e) * gate * (1.0 - gate)).astype(BF16)
        dpre_ref[...] = dpre
        dx, dg = _rms_bwd(x2v, g[...], _mm_nt(dpre, wg[...]))
        dx2_ref[...] = dx3v + dx
        _accumulate(dg_ref, pl.program_id(0) == 0, dg)

    consts, const_specs = _operands([w["g_ple"], w["w_ple_gate"], w["w_ple"]])
    out_shape = (jax.ShapeDtypeStruct((t_len, d), F32), jax.ShapeDtypeStruct((t_len, d), BF16),
                 jax.ShapeDtypeStruct((t_len, d), BF16), jax.ShapeDtypeStruct((t_len, d), BF16),
                 jax.ShapeDtypeStruct((1, d), F32))
    return pl.pallas_call(body, name="ple_bwd", grid=(t_len // tm,),
                          in_specs=[_rows(tm, d), _rows(tm, d), _rows(tm, p.shape[1])] + const_specs,
                          out_specs=(_rows(tm, d),) * 4 + (_whole((1, d)),), out_shape=out_shape,
                          compiler_params=_cparams(("arbitrary",)))(dx3, x2, p, *consts)


def _mlp_bwd(dx2, x1, w):
    t_len, d = x1.shape
    tm = min(TOKEN_TILE, t_len)
    fc = w["w_up"][0].shape[3]
    ff = N_DEV * fc

    def body(dx2_ref, x_ref, g, wup, wdn, dx1_ref, r_ref, da_ref, h2_ref, dg_ref):
        x1v, dx2v = x_ref[...], dx2_ref[...]
        hb = _rms(x1v, g[...]).astype(BF16)
        h2_ref[...] = hb
        dxb = dx2v.astype(BF16)
        dh = jnp.zeros((tm, d), F32)
        for k in range(N_DEV):
            a = jnp.maximum(_mm(hb, wup[k]), 0.0)
            r_ref[:, k * fc:(k + 1) * fc] = (a * a).astype(BF16)
            da = (_mm_nt(dxb, wdn[k]) * (2.0 * a)).astype(BF16)
            da_ref[:, k * fc:(k + 1) * fc] = da
            dh = dh + _mm_nt(da, wup[k])
        dx, dg = _rms_bwd(x1v, g[...], dh)
        dx1_ref[...] = dx2v + dx
        _accumulate(dg_ref, pl.program_id(0) == 0, dg)

    consts, const_specs = _operands([w["g_mlp"], w["w_up"], w["w_down"]])
    out_shape = (jax.ShapeDtypeStruct((t_len, d), F32), jax.ShapeDtypeStruct((t_len, ff), BF16),
                 jax.ShapeDtypeStruct((t_len, ff), BF16), jax.ShapeDtypeStruct((t_len, d), BF16),
                 jax.ShapeDtypeStruct((1, d), F32))
    return pl.pallas_call(body, name="mlp_bwd", grid=(t_len // tm,), in_specs=[_rows(tm, d), _rows(tm, d)] + const_specs,
                          out_specs=(_rows(tm, d), _rows(tm, ff), _rows(tm, ff), _rows(tm, d), _whole((1, d))),
                          out_shape=out_shape, compiler_params=_cparams(("arbitrary",)))(dx2, x1, *consts)


def _mix_out_bwd(dx1, attn, conv, w):
    t_len, d = dx1.shape
    tm = min(TOKEN_TILE, t_len)

    def body(dx1_ref, a_ref, c_ref, goa, goc, wo, mixed_ref, da_ref, dc_ref, dgoa_ref, dgoc_ref):
        av, cv = a_ref[...], c_ref[...]
        mixed_ref[...] = jnp.concatenate([_rms(av, goa[...]), _rms(cv, goc[...])], axis=1).astype(BF16)
        dmixed = _mm_nt(dx1_ref[...].astype(BF16), wo[...])
        da, dga = _rms_bwd(av, goa[...], dmixed[:, :ATTN_WIDTH])
        dc, dgc = _rms_bwd(cv, goc[...], dmixed[:, ATTN_WIDTH:])
        da_ref[...] = da
        dc_ref[...] = dc
        first = pl.program_id(0) == 0
        _accumulate(dgoa_ref, first, dga)
        _accumulate(dgoc_ref, first, dgc)

    consts, const_specs = _operands([w["g_out_attn"], w["g_out_conv"], w["w_o"]])
    out_shape = (jax.ShapeDtypeStruct((t_len, d), BF16), jax.ShapeDtypeStruct((t_len, ATTN_WIDTH), F32),
                 jax.ShapeDtypeStruct((t_len, CONV_WIDTH), F32), jax.ShapeDtypeStruct((1, ATTN_WIDTH), F32),
                 jax.ShapeDtypeStruct((1, CONV_WIDTH), F32))
    out_specs = (_rows(tm, d), _rows(tm, ATTN_WIDTH), _rows(tm, CONV_WIDTH), _whole((1, ATTN_WIDTH)),
                 _whole((1, CONV_WIDTH)))
    return pl.pallas_call(body, name="mix_out_bwd", grid=(t_len // tm,),
                          in_specs=[_rows(tm, d), _rows(tm, ATTN_WIDTH), _rows(tm, CONV_WIDTH)] + const_specs,
                          out_specs=out_specs, out_shape=out_shape,
                          compiler_params=_cparams(("arbitrary",)))(dx1, attn, conv, *consts)


def _attn_bwd(qf, kf, kv, o, do, lse):
    t_len = qf.shape[0]
    blk = min(ATTN_BLOCK, t_len)
    nb = t_len // blk
    reps = blk // LANES
    scale = QK_HEAD ** -0.5

    def body(q_ref, k_ref, kv_ref, o_ref, do_ref, lse_ref, dq_ref, dk_ref, dkv_ref, delta_ref, dob_ref):
        hd = pl.program_id(0)
        lane = lax.broadcasted_iota(jnp.int32, (blk, LANES), 1)
        even = (lane * 0 + hd % 2) == 0
        mine = jnp.where(lane < V_HEAD, 0, 1) == hd % 2
        row = lax.broadcasted_iota(jnp.int32, (blk, blk), 0)
        col = lax.broadcasted_iota(jnp.int32, (blk, blk), 1)
        dq_ref[...] = jnp.zeros_like(dq_ref)

        def prepare(i, carry):
            qs = pl.ds(pl.multiple_of(i * blk, blk), blk)
            dov = do_ref[qs, :]
            prod = jnp.where(mine, dov * o_ref[qs, :], 0.0)
            delta_ref[qs, :] = jnp.broadcast_to(jnp.sum(prod, axis=-1, keepdims=True), (blk, LANES))
            moved = jnp.where(even, pltpu.roll(dov, V_HEAD, 1), dov)
            dob_ref[qs, :] = jnp.where(lane >= V_HEAD, moved, 0.0).astype(BF16)
            return carry
        lax.fori_loop(0, nb, prepare, 0)

        def kvblock(j, carry):
            ks = pl.ds(pl.multiple_of(j * blk, blk), blk)
            k = k_ref[ks, :]
            kvv = kv_ref[ks, :]

            def qstep(i, acc):
                dk_acc, dv_acc = acc
                qs = pl.ds(pl.multiple_of(i * blk, blk), blk)
                q = q_ref[qs, :]
                dob = dob_ref[qs, :]
                s = _mm_nt(q, k) * scale
                s = jnp.where(col + j * blk <= row + i * blk, s, -jnp.inf)
                p = jnp.exp(s - jnp.concatenate([lse_ref[qs, :]] * reps, axis=1))
                dp = _mm_nt(dob, kvv)
                ds = (p * (dp - jnp.concatenate([delta_ref[qs, :]] * reps, axis=1)) * scale).astype(BF16)
                dv_acc = dv_acc + _mm_tn(p.astype(BF16), dob)
                dk_acc = dk_acc + _mm_tn(ds, q)
                dq_ref[qs, :] += _mm(ds, k)
                return dk_acc, dv_acc

            zero = jnp.zeros((blk, LANES), F32)
            dk_acc, dv_acc = lax.fori_loop(j, nb, qstep, (zero, zero))
            dk_ref[ks, :] = dk_acc
            dkv_ref[ks, :] = dv_acc
            return carry
        lax.fori_loop(0, nb, kvblock, 0)

    head = pl.BlockSpec((t_len, HEAD_PAD), lambda h: (0, h))
    pair = pl.BlockSpec((t_len, 2 * V_HEAD), lambda h: (0, h // 2))
    out = jax.ShapeDtypeStruct((t_len, N_HEADS * HEAD_PAD), F32)
    return pl.pallas_call(body, name="attn_bwd", grid=(N_HEADS,), in_specs=[head, head, head, pair, pair, head],
                          out_specs=(head, head, head), out_shape=(out, out, out),
                          scratch_shapes=[pltpu.VMEM((t_len, LANES), F32), pltpu.VMEM((t_len, LANES), BF16)],
                          compiler_params=_cparams(("arbitrary",)))(qf, kf, kv, o, do, lse)


def _front_bwd(x, z, dx1, dqf, dkf, dkv_in, dconv, w, tabs):
    t_len, d = x.shape
    tm = min(TOKEN_TILE, t_len)
    nt = t_len // tm
    hb_per_tile = tm // HALO
    n_halo = t_len // HALO
    hp = N_HEADS * HEAD_PAD

    def body(x_ref, z_ref, zp_ref, zn_ref, dx1_ref, dqf_ref, dkf_ref, dkv_ref, dc_ref, dcn_ref,
             gmix, win, gq, wuq, gkv, wukv, gqn, gkn, cw_ref, gm_ref, cos_ref, sin_ref,
             dx_ref, dz_ref, h_ref, qn_ref, kvn_ref, dqr_ref, dkvr_ref,
             dgmix_ref, dgq_ref, dgkv_ref, dgqn_ref, dgkn_ref, dcw_ref, ubuf, dybuf):
        i = pl.program_id(0)
        first = i == 0
        xv, zv = x_ref[...], z_ref[...]
        hb = _rms(xv, gmix[...]).astype(BF16)
        h_ref[...] = hb
        zq, zkv = zv[:, Z_Q[0]:Z_Q[1]], zv[:, Z_KV[0]:Z_KV[1]]
        qnb = _rms(zq, gq[...]).astype(BF16)
        qn_ref[...] = qnb
        kvb = _rms(zkv, gkv[...]).astype(BF16)
        kvn_ref[...] = kvb
        kpe = zv[:, Z_KPE[0]:Z_KPE[1]]
        cos, sin, gm = cos_ref[...], sin_ref[...], gm_ref[...]
        lane = lax.broadcasted_iota(jnp.int32, (tm, LANES), 1)
        is_nope = lane < QK_NOPE
        is_rope = (lane >= QK_NOPE) & (lane < QK_HEAD)
        dkpe = jnp.zeros((tm, LANES), F32)
        dgqn = jnp.zeros((1, LANES), F32)
        dgkn = jnp.zeros((1, LANES), F32)
        dqn = jnp.zeros((tm, Q_LORA), F32)
        dkvn = jnp.zeros((tm, KV_LORA), F32)
        for hd in range(N_HEADS):
            sl = slice(hd * HEAD_PAD, (hd + 1) * HEAD_PAD)
            dxq, dg = _qk_bwd(_mm(qnb, wuq[hd]), gqn[...], dqf_ref[:, sl], cos, sin, gm, lane)
            dxq = dxq.astype(BF16)
            dqr_ref[:, sl] = dxq
            dqn = dqn + _mm_nt(dxq, wuq[hd])
            dgqn = dgqn + dg
            k_raw = jnp.where(is_nope, _mm(kvb, wukv[hd]), 0.0) + kpe
            dxk, dg = _qk_bwd(k_raw, gkn[...], dkf_ref[:, sl], cos, sin, gm, lane)
            dkv = jnp.where(is_nope, dxk, dkv_ref[:, sl]).astype(BF16)
            dkvr_ref[:, sl] = dkv
            dkvn = dkvn + _mm_nt(dkv, wukv[hd])
            dkpe = dkpe + jnp.where(is_rope, dxk, 0.0)
            dgkn = dgkn + dg
        _accumulate(dgqn_ref, first, dgqn)
        _accumulate(dgkn_ref, first, dgkn)
        dzq, dg = _rms_bwd(zq, gq[...], dqn)
        _accumulate(dgq_ref, first, dg)
        dzkv, dg = _rms_bwd(zkv, gkv[...], dkvn)
        _accumulate(dgkv_ref, first, dg)

        gb, gc, xin = zv[:, Z_GB[0]:Z_GB[1]], zv[:, Z_GC[0]:Z_GC[1]], zv[:, Z_XIN[0]:Z_XIN[1]]
        u = gc * xin
        dcv = dc_ref[...]
        dy = dcv * gb
        zp, zn = zp_ref[...], zn_ref[...]
        ubuf[0:HALO, :] = (zp[:, Z_GC[0]:Z_GC[1]] * zp[:, Z_XIN[0]:Z_XIN[1]]) * jnp.where(first, 0.0, 1.0)
        ubuf[HALO:HALO + tm, :] = u
        dybuf[0:tm, :] = dy
        dybuf[tm:tm + HALO, :] = (dcn_ref[...] * zn[:, Z_GB[0]:Z_GB[1]]) * jnp.where(i == nt - 1, 0.0, 1.0)
        cw = cw_ref[...]
        u1, u2 = ubuf[pl.ds(HALO - 1, tm), :], ubuf[pl.ds(HALO - 2, tm), :]
        y = cw[0:1] * u + cw[1:2] * u1 + cw[2:3] * u2
        du = cw[0:1] * dy + cw[1:2] * dybuf[pl.ds(1, tm), :] + cw[2:3] * dybuf[pl.ds(2, tm), :]
        dcw = jnp.concatenate([jnp.sum(dy * u, axis=0, keepdims=True), jnp.sum(dy * u1, axis=0, keepdims=True),
                               jnp.sum(dy * u2, axis=0, keepdims=True), jnp.zeros((HALO - 3, CONV_WIDTH), F32)], axis=0)
        _accumulate(dcw_ref, first, dcw)

        dz_ref[:, Z_Q[0]:Z_Q[1]] = dzq.astype(BF16)
        dz_ref[:, Z_KV[0]:Z_KV[1]] = dzkv.astype(BF16)
        dz_ref[:, Z_GB[0]:Z_GB[1]] = (dcv * y).astype(BF16)
        dz_ref[:, Z_GC[0]:Z_GC[1]] = (du * xin).astype(BF16)
        dz_ref[:, Z_XIN[0]:Z_XIN[1]] = (du * gc).astype(BF16)
        dz_ref[:, Z_KPE[0]:Z_KPE[1]] = dkpe.astype(BF16)
        dx, dg = _rms_bwd(xv, gmix[...], _mm_nt(dz_ref[...], win[...]))
        dx_ref[...] = dx1_ref[...] + dx
        _accumulate(dgmix_ref, first, dg)

    prev_halo = lambda n: pl.BlockSpec((HALO, n), lambda i: (jnp.maximum(i * hb_per_tile - 1, 0), 0))
    next_halo = lambda n: pl.BlockSpec((HALO, n), lambda i: (jnp.minimum((i + 1) * hb_per_tile, n_halo - 1), 0))
    consts, const_specs = _operands([w["g_mix"], w["w_in"], w["g_q_lat"], w["w_uq"], w["g_kv_lat"], w["w_ukv"],
                                     w["g_qn"], w["g_kn"], w["conv_w"], tabs["gm"]])
    in_specs = ([_rows(tm, d), _rows(tm, Z_COLS), prev_halo(Z_COLS), next_halo(Z_COLS), _rows(tm, d), _rows(tm, hp),
                 _rows(tm, hp), _rows(tm, hp), _rows(tm, CONV_WIDTH), next_halo(CONV_WIDTH)]
                + const_specs + [_rows(tm, LANES), _rows(tm, LANES)])
    out_shape = (jax.ShapeDtypeStruct((t_len, d), F32), jax.ShapeDtypeStruct((t_len, Z_COLS), BF16),
                 jax.ShapeDtypeStruct((t_len, d), BF16), jax.ShapeDtypeStruct((t_len, Q_LORA), BF16),
                 jax.ShapeDtypeStruct((t_len, KV_LORA), BF16), jax.ShapeDtypeStruct((t_len, hp), BF16),
                 jax.ShapeDtypeStruct((t_len, hp), BF16),
                 jax.ShapeDtypeStruct((1, d), F32), jax.ShapeDtypeStruct((1, Q_LORA), F32),
                 jax.ShapeDtypeStruct((1, KV_LORA), F32), jax.ShapeDtypeStruct((1, LANES), F32),
                 jax.ShapeDtypeStruct((1, LANES), F32), jax.ShapeDtypeStruct((HALO, CONV_WIDTH), F32))
    out_specs = tuple(_rows(tm, s.shape[1]) for s in out_shape[:7]) + tuple(_whole(s.shape) for s in out_shape[7:])
    return pl.pallas_call(body, name="front_bwd", grid=(nt,), in_specs=in_specs, out_specs=out_specs, out_shape=out_shape,
                          scratch_shapes=[pltpu.VMEM((tm + HALO, CONV_WIDTH), F32), pltpu.VMEM((tm + HALO, CONV_WIDTH), F32)],
                          compiler_params=_cparams(("arbitrary",)))(
                              x, z, z, z, dx1, dqf, dkf, dkv_in, dconv, dconv, *consts, tabs["cos"], tabs["sin"])


def _wgrad(a, b, shard_cols=None, shard_rows=None, into=None, layer=None):
    t_len, kk = a.shape
    nn = b.shape[1]
    tk = kk if kk <= 512 else 512
    tn = shard_cols or next(c for c in (512, 384, 256, 128) if nn % c == 0)
    tt = min(t_len, 512)
    nt = t_len // tt
    per_block = tk // shard_rows if shard_rows else 1
    assert per_block in (1, 4) and (not shard_rows or shard_rows * per_block == tk)
    lay = () if layer is None else (layer,)
    lay_dim = () if layer is None else (None,)

    def body(*refs):
        a_ref, b_ref, o_ref, acc = refs[0], refs[1], refs[-2], refs[-1]
        t = pl.program_id(2)

        @pl.when(t == 0)
        def _():
            acc[...] = jnp.zeros_like(acc)
        acc[...] += _mm_tn(a_ref[...].astype(BF16), b_ref[...].astype(BF16))

        @pl.when(t == nt - 1)
        def _():
            if per_block == 4:
                for k in range(4):
                    o_ref[k % 2, k // 2] = acc[k * shard_rows:(k + 1) * shard_rows, :]
            else:
                o_ref[...] = acc[...]

    if shard_cols:
        shard_shape = (kk, tn)
        out_spec = pl.BlockSpec((None, None) + lay_dim + (tk, tn), lambda i, j, t: (j % 2, j // 2) + lay + (i, 0))
    elif shard_rows and per_block == 1:
        shard_shape = (shard_rows, nn)
        out_spec = pl.BlockSpec((None, None) + lay_dim + (tk, tn), lambda i, j, t: (i % 2, i // 2) + lay + (0, j))
    elif shard_rows:
        shard_shape = (shard_rows, nn)
        out_spec = pl.BlockSpec((2, 2) + lay_dim + (shard_rows, tn), lambda i, j, t: (0, i) + lay + (0, j))
    else:
        shard_shape = None
        out_spec = pl.BlockSpec((tk, tn), lambda i, j, t: (i, j))
    if shard_shape is None:
        out_shape = jax.ShapeDtypeStruct((kk, nn), F32)
    elif into is None:
        out_shape = jax.ShapeDtypeStruct((2, N_CHIP) + shard_shape, F32)
    else:
        out_shape = jax.ShapeDtypeStruct(into.shape, F32)
    in_specs = [pl.BlockSpec((tt, tk), lambda i, j, t: (t, i)), pl.BlockSpec((tt, tn), lambda i, j, t: (t, j))]
    args = [a, b]
    aliases = {}
    if into is not None:
        in_specs.append(pl.BlockSpec(memory_space=pl.ANY))
        args.append(into)
        aliases = {2: 0}
    return pl.pallas_call(body, name="wgrad", grid=(kk // tk, nn // tn, nt), in_specs=in_specs, out_specs=out_spec,
                          out_shape=out_shape, scratch_shapes=[pltpu.VMEM((tk, tn), F32)], input_output_aliases=aliases,
                          compiler_params=_cparams(("parallel", "parallel", "arbitrary")))(*args)


def _my_place():
    return lax.axis_index("x"), lax.axis_index("y"), lax.axis_index("c")


def _any_specs(n):
    return [pl.BlockSpec(memory_space=pl.ANY)] * n


def _all_gather(blocks):
    n = len(blocks)

    def body(*refs):
        x_refs, out_refs = refs[:n], refs[n:2 * n]
        send_sems, recv_sems, local_sems = refs[2 * n:]
        x, y, c = _my_place()
        me, sibling = (x, y, c), (x, y, 1 - c)
        chips = [(1 - x, y), (x, 1 - y), (1 - x, 1 - y)]

        def slot(a, px, py, pc):
            return out_refs[a].at[4 * px + 2 * py + pc]

        def copy(a, k, blk, to, src=None):
            return pltpu.make_async_remote_copy(src_ref=slot(a, *blk) if src is None else src, dst_ref=slot(a, *blk),
                                                send_sem=send_sems.at[a, k], recv_sem=recv_sems.at[a, k],
                                                device_id=to, device_id_type=MESH)

        mine = [pltpu.make_async_copy(x_refs[a], slot(a, *me), local_sems.at[a]) for a in range(n)]
        for cp in mine:
            cp.start()
        started = []
        for a in range(n):
            started.append(copy(a, 0, me, sibling, src=x_refs[a]))
            started += [copy(a, 1 + j, me, (*chip, c), src=x_refs[a]) for j, chip in enumerate(chips)]
        for cp in started:
            cp.start()
        for j, chip in enumerate(chips):
            for a in range(n):
                copy(a, 1 + j, (*chip, c), me).wait_recv()
                passed = copy(a, 4 + j, (*chip, c), sibling)
                passed.start()
                started.append(passed)
        for a in range(n):
            copy(a, 0, sibling, me).wait_recv()
        for j, chip in enumerate(chips):
            for a in range(n):
                copy(a, 4 + j, (*chip, 1 - c), me).wait_recv()
        for cp in started:
            cp.wait_send()
        for cp in mine:
            cp.wait()

    out_shape = tuple(jax.ShapeDtypeStruct((N_DEV,) + b.shape, b.dtype) for b in blocks)
    return pl.pallas_call(body, name="all_gather", out_shape=out_shape, in_specs=_any_specs(n),
                          out_specs=tuple(_any_specs(n)),
                          scratch_shapes=[pltpu.SemaphoreType.DMA((n, 7)), pltpu.SemaphoreType.DMA((n, 7)),
                                          pltpu.SemaphoreType.DMA((n,))])(*blocks)


def _sibling_swap(gs):
    n = len(gs)

    def body(*refs):
        g_refs, out_refs = refs[:n], refs[n:2 * n]
        send_sems, recv_sems = refs[2 * n:]
        x, y, c = _my_place()
        copies = [pltpu.make_async_remote_copy(src_ref=g_refs[a].at[1 - c], dst_ref=out_refs[a], send_sem=send_sems.at[a],
                                               recv_sem=recv_sems.at[a], device_id=(x, y, 1 - c), device_id_type=MESH)
                  for a in range(n)]
        for cp in copies:
            cp.start()
        for cp in copies:
            cp.wait()

    out_shape = tuple(jax.ShapeDtypeStruct(g.shape[1:], g.dtype) for g in gs)
    return pl.pallas_call(body, name="sibling_swap", out_shape=out_shape, in_specs=_any_specs(n),
                          out_specs=tuple(_any_specs(n)),
                          scratch_shapes=[pltpu.SemaphoreType.DMA((n,)), pltpu.SemaphoreType.DMA((n,))])(*gs)


def _chip_exchange(ss):
    n = len(ss)

    def body(*refs):
        s_refs, out_refs = refs[:n], refs[n:2 * n]
        send_sems, recv_sems, local_sems = refs[2 * n:]
        x, y, c = _my_place()
        my_q = 2 * x + y
        chips = [(1 - x, y), (x, 1 - y), (1 - x, 1 - y)]
        local = [pltpu.make_async_copy(s_refs[a].at[my_q], out_refs[a].at[my_q], local_sems.at[a]) for a in range(n)]
        for cp in local:
            cp.start()

        def copy(a, k, src_q, dst_q, chip):
            return pltpu.make_async_remote_copy(src_ref=s_refs[a].at[src_q], dst_ref=out_refs[a].at[dst_q],
                                                send_sem=send_sems.at[a, k], recv_sem=recv_sems.at[a, k],
                                                device_id=(*chip, c), device_id_type=MESH)

        sends = [copy(a, k, 2 * px + py, my_q, (px, py)) for a in range(n) for k, (px, py) in enumerate(chips)]
        for cp in sends:
            cp.start()
        for a in range(n):
            for k, (px, py) in enumerate(chips):
                copy(a, k, my_q, 2 * px + py, (px, py)).wait_recv()
        for cp in sends:
            cp.wait_send()
        for cp in local:
            cp.wait()

    out_shape = tuple(jax.ShapeDtypeStruct(s.shape, s.dtype) for s in ss)
    return pl.pallas_call(body, name="chip_exchange", out_shape=out_shape, in_specs=_any_specs(n),
                          out_specs=tuple(_any_specs(n)),
                          scratch_shapes=[pltpu.SemaphoreType.DMA((n, 3)), pltpu.SemaphoreType.DMA((n, 3)),
                                          pltpu.SemaphoreType.DMA((n,))])(*ss)


def _row_block(rows):
    return ROW_BLOCK if rows % ROW_BLOCK == 0 else rows


def _pair_sum(g, got, c):
    shape = got.shape
    cols = shape[-1]
    g2, got2 = g.reshape(2, -1, cols), got.reshape(-1, cols)
    rows = got2.shape[0]
    rb = _row_block(rows)

    def body(c_ref, g_ref, got_ref, o_ref):
        o_ref[...] = (g_ref[...] + got_ref[...]).astype(BF16)

    grid_spec = pltpu.PrefetchScalarGridSpec(
        num_scalar_prefetch=1, grid=(rows // rb,),
        in_specs=[pl.BlockSpec((None, rb, cols), lambda i, c_ref: (c_ref[0], i, 0)),
                  pl.BlockSpec((rb, cols), lambda i, c_ref: (i, 0))],
        out_specs=pl.BlockSpec((rb, cols), lambda i, c_ref: (i, 0)))
    out = pl.pallas_call(body, name="pair_sum", grid_spec=grid_spec, out_shape=jax.ShapeDtypeStruct((rows, cols), BF16),
                         compiler_params=_cparams(("parallel",)))(c, g2, got2)
    return out.reshape(shape)


def _sum_leading(parts):
    n_part, shape = parts.shape[0], parts.shape[1:]
    cols = shape[-1]
    p2 = parts.reshape(n_part, -1, cols)
    rows = p2.shape[1]
    rb = _row_block(rows)

    def body(p_ref, o_ref):
        acc = p_ref[0].astype(F32)
        for k in range(1, n_part):
            acc = acc + p_ref[k].astype(F32)
        o_ref[...] = acc

    out = pl.pallas_call(body, name="sum_leading", grid=(rows // rb,),
                         in_specs=[pl.BlockSpec((n_part, rb, cols), lambda i: (0, i, 0))], out_specs=_rows(rb, cols),
                         out_shape=jax.ShapeDtypeStruct((rows, cols), F32), compiler_params=_cparams(("parallel",)))(p2)
    return out.reshape(shape)


def _adamw(w, g, m, v):
    shape = w.shape
    two_d = (shape[0] * shape[1], shape[2]) if len(shape) == 3 else shape
    rows, cols = two_d
    rb = _row_block(rows)

    def body(w_ref, g_ref, m_ref, v_ref, d_ref, nm_ref, nv_ref):
        gv = g_ref[...]
        nm = ADAM_B1 * m_ref[...] + (1.0 - ADAM_B1) * gv
        nv = ADAM_B2 * v_ref[...] + (1.0 - ADAM_B2) * jnp.square(gv)
        m_hat = nm / (1.0 - ADAM_B1 ** ADAM_STEP)
        v_hat = nv / (1.0 - ADAM_B2 ** ADAM_STEP)
        d_ref[...] = -ADAM_LR * (m_hat / (jnp.sqrt(v_hat) + ADAM_EPS) + ADAM_WD * w_ref[...])
        nm_ref[...] = nm
        nv_ref[...] = nv

    spec = _rows(rb, cols)
    out = jax.ShapeDtypeStruct(two_d, F32)
    res = pl.pallas_call(body, name="adamw", grid=(rows // rb,), in_specs=[spec] * 4, out_specs=(spec,) * 3,
                         out_shape=(out,) * 3, compiler_params=_cparams(("parallel",)))(
                             *(a.reshape(two_d) for a in (w, g, m, v)))
    return tuple(a.reshape(shape) for a in res)


def _rope_tables(positions):
    t_len = positions.shape[0]
    inv_freq = 1.0 / (ROPE_THETA ** (jnp.arange(0, QK_ROPE, 2, dtype=F32) / QK_ROPE))
    ang = positions.astype(F32)[:, None] * inv_freq
    c, s = jnp.cos(ang), jnp.sin(ang)
    one, zero = jnp.ones((t_len, QK_NOPE), F32), jnp.zeros((t_len, QK_NOPE), F32)
    cos = jnp.concatenate([one, c, c, one[:, :LANES - QK_HEAD]], axis=1)
    sin = jnp.concatenate([zero, -s, s, zero[:, :LANES - QK_HEAD]], axis=1)
    idx = jnp.arange(LANES)
    grp = jnp.where(idx < QK_NOPE, 0, jnp.where(idx < QK_HEAD, 1, 2))
    val = jnp.where(grp == 0, 1.0 / QK_NOPE, jnp.where(grp == 1, 1.0 / QK_ROPE, 0.0))
    gm = jnp.where(grp[:, None] == grp[None, :], val[None, :], 0.0).astype(BF16)
    return {"cos": cos, "sin": sin, "gm": gm}


def _head_gain(g_nope, g_rope):
    return jnp.concatenate([g_nope, g_rope, jnp.zeros((LANES - QK_HEAD,), F32)]).reshape(1, LANES)


def _padded_w_in(shards):
    natural = jnp.concatenate([shards[k] for k in range(N_DEV)], axis=1)
    o2, o3 = Q_LORA + KV_LORA, Q_LORA + KV_LORA + QK_ROPE
    zeros = jnp.zeros((natural.shape[0], QK_NOPE), natural.dtype)
    return jnp.concatenate([natural[:, :o2], natural[:, o3:], zeros, natural[:, o2:o3], zeros[:, :LANES - QK_HEAD]], axis=1)


def _by_owner(shards):
    return jnp.swapaxes(shards.reshape((N_CHIP, 2) + shards.shape[1:]), 0, 1)


def _w_in_grad_shards(d_in):
    o2 = Q_LORA + KV_LORA
    nat = jnp.concatenate([d_in[:, :o2], d_in[:, Z_KPE[0] + QK_NOPE:Z_KPE[0] + QK_HEAD], d_in[:, o2:Z_XIN[1]]], axis=1)
    width = nat.shape[1] // N_DEV
    return _by_owner(jnp.stack([nat[:, k * width:(k + 1) * width] for k in range(N_DEV)]))


def kernel(x, p, positions, g_mix, w_in, g_q_lat, w_uq, g_kv_lat, w_ukv, g_qn_nope, g_qn_rope, g_kn_nope, g_kn_rope, conv_w, g_out_attn, g_out_conv, w_o, g_mlp, w_up, w_down, g_ple, w_ple_gate, w_ple, loss_target, m_g_mix, m_w_in, m_g_q_lat, m_w_uq, m_g_kv_lat, m_w_ukv, m_g_qn_nope, m_g_qn_rope, m_g_kn_nope, m_g_kn_rope, m_conv_w, m_g_out_attn, m_g_out_conv, m_w_o, m_g_mlp, m_w_up, m_w_down, m_g_ple, m_w_ple_gate, m_w_ple, v_g_mix, v_w_in, v_g_q_lat, v_w_uq, v_g_kv_lat, v_w_ukv, v_g_qn_nope, v_g_qn_rope, v_g_kn_nope, v_g_kn_rope, v_conv_w, v_g_out_attn, v_g_out_conv, v_w_o, v_g_mlp, v_w_up, v_w_down, v_g_ple, v_w_ple_gate, v_w_ple):
    given = dict(locals())
    weights = {n: given[n] for n in WEIGHT_NAMES}
    gains = {n: given[n] for n in GAIN_NAMES}
    depth = w_in.shape[0]
    xs, target = x[0], loss_target[0]
    d_model = xs.shape[1]
    c_idx = lax.axis_index("c").reshape(1).astype(jnp.int32)
    uq_cols = w_uq.shape[2]

    to_send = [weights[n].astype(BF16) for n in SHARD_NAMES if n != "conv_w"] + [conv_w]
    to_send[1] = jnp.pad(to_send[1], ((0, 0), (0, 0), (0, HEAD_PAD - uq_cols)))
    got = _all_gather(to_send)
    full = dict(zip([n for n in SHARD_NAMES if n != "conv_w"] + ["conv_w"], got))
    conv_full = jnp.transpose(full["conv_w"], (1, 2, 0, 3)).reshape(depth, conv_w.shape[1], -1)

    tabs = _rope_tables(positions[0])
    layer_w = []
    for layer in range(depth):
        lw = {n: gains[n][layer].reshape(1, -1) for n in GAIN_NAMES}
        lw.update({"w_in": _padded_w_in(full["w_in"][:, layer]), "w_uq": full["w_uq"][:, layer],
                   "w_ukv": full["w_ukv"][:, layer], "w_ple": full["w_ple"][:, layer],
                   "w_o": full["w_o"][:, layer].reshape(d_model, d_model),
                   "w_ple_gate": full["w_ple_gate"][:, layer].reshape(d_model, d_model),
                   "w_up": _of_layer(full["w_up"], layer), "w_down": _of_layer(full["w_down"], layer),
                   "conv_w": jnp.pad(conv_full[layer], ((0, HALO - conv_full.shape[1]), (0, 0))),
                   "g_qn": _head_gain(g_qn_nope[layer], g_qn_rope[layer]),
                   "g_kn": _head_gain(g_kn_nope[layer], g_kn_rope[layer])})
        layer_w.append(lw)

    saved = []
    cur = xs
    for layer in range(depth):
        w = layer_w[layer]
        z, qf, kf, kv, conv = _front_fwd(cur, w, tabs)
        attn, lse = _attn_fwd(qf, kf, kv)
        x1 = _mix_out_fwd(cur, attn, conv, w)
        x2 = _mlp_fwd(x1, w)
        x3 = _ple_fwd(x2, p[layer, 0], w)
        saved.append(dict(x=cur, z=z, qf=qf, kf=kf, kv=kv, conv=conv, attn=attn, lse=lse, x1=x1, x2=x2))
        cur = x3

    sq, dx = _loss_and_grad(cur, target)
    loss = lax.psum(0.5 / d_model * sq[0, 0], ("x", "y", "c"))

    direct = ("w_ukv", "w_o", "w_up", "w_down", "w_ple_gate", "w_ple")
    bufs = {n: lax.empty((2, N_CHIP, depth) + weights[n].shape[1:], F32) for n in direct}
    small = {"w_in": [None] * depth, "w_uq": [None] * depth, "conv_w": [None] * depth}
    gain_grads = [None] * depth
    for layer in reversed(range(depth)):
        w, s = layer_w[layer], saved[layer]
        pl_in = p[layer, 0]
        dx2, de, h3, dpre, dg_ple = _ple_bwd(dx, s["x2"], pl_in, w)
        dx1, r, da, h2, dg_mlp = _mlp_bwd(dx2, s["x1"], w)
        mixed, dattn, dconv, dg_oa, dg_oc = _mix_out_bwd(dx1, s["attn"], s["conv"], w)
        dqf, dkf, dkv = _attn_bwd(s["qf"], s["kf"], s["kv"], s["attn"], dattn, s["lse"])
        (dx0, dz, hb, qn, kvn, dqr, dkvr, dg_mix, dg_q, dg_kv, dg_qn, dg_kn, dcw) = _front_bwd(
            s["x"], s["z"], dx1, dqf, dkf, dkv, dconv, w, tabs)
        small["w_in"][layer] = _w_in_grad_shards(_wgrad(hb, dz))
        small["w_uq"][layer] = _wgrad(qn, dqr, shard_cols=HEAD_PAD)[..., :uq_cols]
        n_taps = conv_w.shape[1]
        small["conv_w"][layer] = _by_owner(jnp.transpose(dcw[:n_taps].reshape(n_taps, N_DEV, -1), (1, 0, 2)))
        bufs["w_ukv"] = _wgrad(kvn, dkvr, shard_cols=HEAD_PAD, into=bufs["w_ukv"], layer=layer)
        bufs["w_o"] = _wgrad(mixed, dx1, shard_rows=w_o.shape[1], into=bufs["w_o"], layer=layer)
        bufs["w_up"] = _wgrad(h2, da, shard_cols=w_up.shape[2], into=bufs["w_up"], layer=layer)
        bufs["w_down"] = _wgrad(r, dx2, shard_rows=w_down.shape[1], into=bufs["w_down"], layer=layer)
        bufs["w_ple_gate"] = _wgrad(h3, dpre, shard_rows=w_ple_gate.shape[1], into=bufs["w_ple_gate"], layer=layer)
        bufs["w_ple"] = _wgrad(pl_in, de, shard_cols=w_ple.shape[2], into=bufs["w_ple"], layer=layer)
        gain_grads[layer] = jnp.concatenate([
            dg_mix[0], dg_q[0], dg_kv[0], dg_qn[0, :QK_NOPE], dg_qn[0, QK_NOPE:QK_HEAD], dg_kn[0, :QK_NOPE],
            dg_kn[0, QK_NOPE:QK_HEAD], dg_oa[0], dg_oc[0], dg_mlp[0], dg_ple[0]])
        dx = dx0
    for n in small:
        bufs[n] = jnp.stack(small[n], axis=2)

    g_all = [bufs[n] for n in SHARD_NAMES]
    from_sibling = _sibling_swap(g_all)
    chip_parts = [_pair_sum(g, got_g, c_idx) for g, got_g in zip(g_all, from_sibling)]
    grads = {n: _sum_leading(part) for n, part in zip(SHARD_NAMES, _chip_exchange(chip_parts))}

    gg = jnp.stack(gain_grads)
    gg_rows = -(-gg.size // (HALO * LANES)) * HALO
    gg_pad = jnp.pad(gg.reshape(-1), (0, gg_rows * LANES - gg.size)).reshape(gg_rows, LANES)
    gg_sum = _sum_leading(_all_gather([gg_pad])[0]).reshape(-1)[:gg.size].reshape(gg.shape)
    off = 0
    for n in GAIN_NAMES:
        width = gains[n].shape[1]
        grads[n] = gg_sum[:, off:off + width]
        off += width

    deltas, new_m, new_v = {}, {}, {}
    for n in WEIGHT_NAMES:
        deltas[n], new_m[n], new_v[n] = _adamw(weights[n], grads[n], given["m_" + n], given["v_" + n])
    return (loss, dx[None], *[grads[n] for n in WEIGHT_NAMES], *[deltas[n] for n in WEIGHT_NAMES],
            *[new_m[n] for n in WEIGHT_NAMES], *[new_v[n] for n in WEIGHT_NAMES])
```

```python
import jax
import jax.numpy as jnp
from jax import lax
from jax.experimental import pallas as pl
from jax.experimental.pallas import tpu as pltpu

F32 = jnp.float32
BF16 = jnp.bfloat16
MESH = pl.DeviceIdType.MESH

N_HEADS = 8
QK_NOPE = 64
QK_ROPE = 32
QK_HEAD = QK_NOPE + QK_ROPE
V_HEAD = 64
HEAD_PAD = 128
Q_LORA = 384
KV_LORA = 256
CONV_WIDTH = 512
ATTN_WIDTH = N_HEADS * V_HEAD
ROPE_THETA = 10000.0
EPS = 1e-6
ADAM_LR, ADAM_B1, ADAM_B2, ADAM_EPS, ADAM_WD, ADAM_STEP = 0.001, 0.9, 0.999, 1e-08, 0.01, 10

Z_Q = (0, 384)
Z_KV = (384, 640)
Z_GB = (640, 1152)
Z_GC = (1152, 1664)
Z_XIN = (1664, 2176)
Z_KPE = (2176, 2304)
Z_COLS = 2304

N_DEV = 8
N_CHIP = 4
LANES = 128
V7X_VMEM_LIMIT = 52 * 1024 * 1024
TOKEN_TILE = 256
ATTN_BLOCK = 256
ATTN_Q_SUB = 2
ATTN_KV_SUB = 2
ROW_BLOCK = 512
WGRAD_TOKENS = 2048
HALO = 8

GAIN_NAMES = ("g_mix", "g_q_lat", "g_kv_lat", "g_qn_nope", "g_qn_rope", "g_kn_nope", "g_kn_rope",
              "g_out_attn", "g_out_conv", "g_mlp", "g_ple")
SHARD_NAMES = ("w_in", "w_uq", "w_ukv", "conv_w", "w_o", "w_up", "w_down", "w_ple_gate", "w_ple")
WEIGHT_NAMES = ("g_mix", "w_in", "g_q_lat", "w_uq", "g_kv_lat", "w_ukv", "g_qn_nope", "g_qn_rope", "g_kn_nope",
                "g_kn_rope", "conv_w", "g_out_attn", "g_out_conv", "w_o", "g_mlp", "w_up", "w_down", "g_ple",
                "w_ple_gate", "w_ple")


def _cparams(semantics=None):
    return pltpu.CompilerParams(dimension_semantics=semantics, vmem_limit_bytes=V7X_VMEM_LIMIT)


def _mm(a, b):
    return jnp.dot(a, b, preferred_element_type=F32)


def _mm_nt(a, b):
    return lax.dot_general(a, b, (((1,), (1,)), ((), ())), preferred_element_type=F32)


def _mm_tn(a, b):
    return lax.dot_general(a, b, (((0,), (0,)), ((), ())), preferred_element_type=F32)


def _rms(x, g):
    r = lax.rsqrt(jnp.mean(x * x, axis=-1, keepdims=True) + EPS)
    return (x * r) * g


def _rms_bwd(x, g, dy):
    r = lax.rsqrt(jnp.mean(x * x, axis=-1, keepdims=True) + EPS)
    xh = x * r
    dg = jnp.sum(dy * xh, axis=0, keepdims=True)
    dyg = dy * g
    dx = r * (dyg - xh * jnp.mean(dyg * xh, axis=-1, keepdims=True))
    return dx, dg


def _group_mean(t, gm):
    hi = t.astype(BF16)
    lo = (t - hi.astype(F32)).astype(BF16)
    return _mm(hi, gm) + _mm(lo, gm)


def _swap_rope_halves(x, lane):
    half = QK_ROPE // 2
    swapped = jnp.where(lane < QK_NOPE + half, pltpu.roll(x, LANES - half, 1), pltpu.roll(x, half, 1))
    return jnp.where((lane >= QK_NOPE) & (lane < QK_HEAD), swapped, 0.0)


def _qk_fwd(x, g, cos, sin, gm, lane):
    r = lax.rsqrt(_group_mean(x * x, gm) + EPS)
    n = (x * r) * g
    return n * cos + _swap_rope_halves(n, lane) * sin


def _qk_bwd(x, g, dy, cos, sin, gm, lane):
    r = lax.rsqrt(_group_mean(x * x, gm) + EPS)
    xh = x * r
    dn = dy * cos + _swap_rope_halves(dy * sin, lane)
    dg = jnp.sum(dn * xh, axis=0, keepdims=True)
    dng = dn * g
    dx = r * (dng - xh * _group_mean(dng * xh, gm))
    return dx, dg


def _rows(tm, n):
    return pl.BlockSpec((tm, n), lambda i: (i, 0))


def _whole(shape):
    zeros = (0,) * len(shape)
    return pl.BlockSpec(shape, lambda i: zeros)


def _of_layer(gathered, layer):
    n, _, a, b = gathered.shape
    return gathered, pl.BlockSpec((n, None, a, b), lambda i: (0, layer, 0, 0))


def _operands(values):
    arrays, specs = [], []
    for v in values:
        arr, spec = v if isinstance(v, tuple) else (v, _whole(v.shape))
        arrays.append(arr)
        specs.append(spec)
    return arrays, specs


def _accumulate(ref, first, value):
    @pl.when(first)
    def _():
        ref[...] = jnp.zeros_like(ref)
    ref[...] += value


def _front_fwd(x, w, tabs):
    t_len, d = x.shape
    tm = min(TOKEN_TILE, t_len)
    hp = N_HEADS * HEAD_PAD

    def body(x_ref, gmix, win, gq, wuq, gkv, wukv, gqn, gkn, cw_ref, gm_ref, cos_ref, sin_ref,
             z_ref, qf_ref, kf_ref, kv_ref, conv_ref, ubuf):
        i = pl.program_id(0)
        h = _rms(x_ref[...], gmix[...])
        z = _mm(h.astype(BF16), win[...])
        z_ref[...] = z
        qnb = _rms(z[:, Z_Q[0]:Z_Q[1]], gq[...]).astype(BF16)
        kvb = _rms(z[:, Z_KV[0]:Z_KV[1]], gkv[...]).astype(BF16)
        kpe = z[:, Z_KPE[0]:Z_KPE[1]]
        cos, sin, gm = cos_ref[...], sin_ref[...], gm_ref[...]
        lane = lax.broadcasted_iota(jnp.int32, (tm, LANES), 1)
        for hd in range(N_HEADS):
            sl = slice(hd * HEAD_PAD, (hd + 1) * HEAD_PAD)
            qf_ref[:, sl] = _qk_fwd(_mm(qnb, wuq[hd]), gqn[...], cos, sin, gm, lane).astype(BF16)
            kv = _mm(kvb, wukv[hd])
            kv_ref[:, sl] = kv.astype(BF16)
            kf_ref[:, sl] = _qk_fwd(jnp.where(lane < QK_NOPE, kv, 0.0) + kpe, gkn[...], cos, sin, gm, lane).astype(BF16)
        u = z[:, Z_GC[0]:Z_GC[1]] * z[:, Z_XIN[0]:Z_XIN[1]]

        @pl.when(i == 0)
        def _():
            ubuf[0:HALO, :] = jnp.zeros((HALO, CONV_WIDTH), F32)
        ubuf[HALO:HALO + tm, :] = u
        cw = cw_ref[...]
        y = cw[0:1] * u + cw[1:2] * ubuf[pl.ds(HALO - 1, tm), :] + cw[2:3] * ubuf[pl.ds(HALO - 2, tm), :]
        conv_ref[...] = z[:, Z_GB[0]:Z_GB[1]] * y
        ubuf[0:HALO, :] = u[tm - HALO:tm, :]

    consts, const_specs = _operands([w["g_mix"], w["w_in"], w["g_q_lat"], w["w_uq"], w["g_kv_lat"], w["w_ukv"],
                                     w["g_qn"], w["g_kn"], w["conv_w"], tabs["gm"]])
    out_shape = (jax.ShapeDtypeStruct((t_len, Z_COLS), F32), jax.ShapeDtypeStruct((t_len, hp), BF16),
                 jax.ShapeDtypeStruct((t_len, hp), BF16), jax.ShapeDtypeStruct((t_len, hp), BF16),
                 jax.ShapeDtypeStruct((t_len, CONV_WIDTH), F32))
    return pl.pallas_call(body, name="front_fwd", grid=(t_len // tm,),
                          in_specs=[_rows(tm, d)] + const_specs + [_rows(tm, LANES), _rows(tm, LANES)],
                          out_specs=tuple(_rows(tm, s.shape[1]) for s in out_shape), out_shape=out_shape,
                          scratch_shapes=[pltpu.VMEM((tm + HALO, CONV_WIDTH), F32)],
                          compiler_params=_cparams(("arbitrary",)))(x, *consts, tabs["cos"], tabs["sin"])


def _attn_fwd(qf, kf, kv):
    t_len = qf.shape[0]
    blk = min(ATTN_BLOCK, t_len)
    nb = t_len // blk
    n_sub = ATTN_Q_SUB
    bq = blk // n_sub
    scale = QK_HEAD ** -0.5
    chains = [(hh, a) for hh in range(2) for a in range(n_sub)]

    def body(q_ref, k_ref, kv_ref, o_ref, lse_ref):
        lane = lax.broadcasted_iota(jnp.int32, (bq, LANES), 1)
        row = lax.broadcasted_iota(jnp.int32, (bq, blk), 0)
        col = lax.broadcasted_iota(jnp.int32, (bq, blk), 1)

        def head_cols(hh):
            return slice(hh * HEAD_PAD, (hh + 1) * HEAD_PAD)

        def softmax_step(s, kvv, state, first_row=None):
            m, l, acc = state
            s = s * scale
            if first_row is not None:
                s = jnp.where(col <= row + first_row, s, -jnp.inf)
            m_new = jnp.maximum(m, jnp.max(s, axis=-1, keepdims=True))
            p = jnp.exp(s - m_new)
            alpha = jnp.exp(m - m_new)
            l = alpha * l + jnp.sum(p, axis=-1, keepdims=True)
            acc = alpha * acc + _mm(p.astype(BF16), kvv)
            return m_new, l, acc

        def qblock(i, carry):
            start = pl.multiple_of(i * blk, blk)
            rows = [pl.ds(pl.multiple_of(start + a * bq, bq), bq) for a in range(n_sub)]
            qs = {(hh, a): q_ref[rows[a], head_cols(hh)] for hh, a in chains}

            def scores(j):
                ks = pl.ds(pl.multiple_of(j * blk, blk), blk)
                return tuple(_mm_nt(qs[hh, a], k_ref[ks, head_cols(hh)]) for hh, a in chains)

            def kstep(j, carried, diagonal=False):
                ss, states = carried
                ss_next = ss if diagonal else scores(j + 1)
                ks = pl.ds(pl.multiple_of(j * blk, blk), blk)
                new = tuple(softmax_step(s, kv_ref[ks, head_cols(hh)], st, a * bq if diagonal else None)
                            for (hh, a), s, st in zip(chains, ss, states))
                return ss_next, new

            init = (jnp.full((bq, 1), -jnp.inf, F32), jnp.zeros((bq, 1), F32), jnp.zeros((bq, LANES), F32))
            carried = lax.fori_loop(0, i, kstep, (scores(0), (init,) * len(chains)))
            _, states = kstep(i, carried, diagonal=True)
            for a in range(n_sub):
                (m0, l0, acc0), (m1, l1, acc1) = states[chains.index((0, a))], states[chains.index((1, a))]
                o_ref[rows[a], :] = jnp.where(lane < V_HEAD, pltpu.roll(acc0 / l0, V_HEAD, 1), acc1 / l1)
                lse_ref[rows[a], head_cols(0)] = jnp.broadcast_to(m0 + jnp.log(l0), (bq, LANES))
                lse_ref[rows[a], head_cols(1)] = jnp.broadcast_to(m1 + jnp.log(l1), (bq, LANES))
            return carry

        lax.fori_loop(0, nb, qblock, 0)

    heads = pl.BlockSpec((t_len, 2 * HEAD_PAD), lambda h: (0, h))
    pair = pl.BlockSpec((t_len, 2 * V_HEAD), lambda h: (0, h))
    return pl.pallas_call(body, name="attn_fwd", grid=(N_HEADS // 2,), in_specs=[heads, heads, heads],
                          out_specs=(pair, heads),
                          out_shape=(jax.ShapeDtypeStruct((t_len, ATTN_WIDTH), F32),
                                     jax.ShapeDtypeStruct((t_len, N_HEADS * LANES), F32)),
                          compiler_params=_cparams(("parallel",)))(qf, kf, kv)


def _mix_out_fwd(x, attn, conv, w):
    t_len, d = x.shape
    tm = min(TOKEN_TILE, t_len)

    def body(x_ref, a_ref, c_ref, goa, goc, wo, x1_ref):
        mixed = jnp.concatenate([_rms(a_ref[...], goa[...]), _rms(c_ref[...], goc[...])], axis=1)
        x1_ref[...] = x_ref[...] + _mm(mixed.astype(BF16), wo[...])

    consts, const_specs = _operands([w["g_out_attn"], w["g_out_conv"], w["w_o"]])
    return pl.pallas_call(body, name="mix_out_fwd", grid=(t_len // tm,),
                          in_specs=[_rows(tm, d), _rows(tm, ATTN_WIDTH), _rows(tm, CONV_WIDTH)] + const_specs,
                          out_specs=_rows(tm, d), out_shape=jax.ShapeDtypeStruct((t_len, d), F32),
                          compiler_params=_cparams(("parallel",)))(x, attn, conv, *consts)


def _mlp_fwd(x1, w):
    t_len, d = x1.shape
    tm = min(TOKEN_TILE, t_len)

    def body(x_ref, g, wup, wdn, x2_ref):
        x1v = x_ref[...]
        hb = _rms(x1v, g[...]).astype(BF16)
        acc = x1v
        for k in range(N_DEV):
            a = jnp.maximum(_mm(hb, wup[k]), 0.0)
            acc = acc + _mm((a * a).astype(BF16), wdn[k])
        x2_ref[...] = acc

    consts, const_specs = _operands([w["g_mlp"], w["w_up"], w["w_down"]])
    return pl.pallas_call(body, name="mlp_fwd", grid=(t_len // tm,), in_specs=[_rows(tm, d)] + const_specs,
                          out_specs=_rows(tm, d), out_shape=jax.ShapeDtypeStruct((t_len, d), F32),
                          compiler_params=_cparams(("parallel",)))(x1, *consts)


def _ple_fwd(x2, p, w):
    t_len, d = x2.shape
    tm = min(TOKEN_TILE, t_len)

    def body(x_ref, p_ref, g, wg, wp, x3_ref):
        x2v = x_ref[...]
        gate = jax.nn.sigmoid(_mm(_rms(x2v, g[...]).astype(BF16), wg[...]))
        pb = p_ref[...].astype(BF16)
        e = jnp.concatenate([_mm(pb, wp[k]) for k in range(N_DEV)], axis=1)
        x3_ref[...] = x2v + gate * e

    consts, const_specs = _operands([w["g_ple"], w["w_ple_gate"], w["w_ple"]])
    return pl.pallas_call(body, name="ple_fwd", grid=(t_len // tm,),
                          in_specs=[_rows(tm, d), _rows(tm, p.shape[1])] + const_specs, out_specs=_rows(tm, d),
                          out_shape=jax.ShapeDtypeStruct((t_len, d), F32),
                          compiler_params=_cparams(("parallel",)))(x2, p, *consts)


def _loss_and_grad(y, target):
    t_len, d = y.shape
    tm = min(TOKEN_TILE, t_len)

    def body(y_ref, t_ref, sq_ref, dy_ref):
        err = y_ref[...] - t_ref[...]
        dy_ref[...] = err / d
        total = jnp.sum(jnp.sum(err * err, axis=0, keepdims=True), axis=1, keepdims=True)
        _accumulate(sq_ref, pl.program_id(0) == 0, jnp.broadcast_to(total, (HALO, LANES)))

    return pl.pallas_call(body, name="loss_grad", grid=(t_len // tm,), in_specs=[_rows(tm, d), _rows(tm, d)],
                          out_specs=(_whole((HALO, LANES)), _rows(tm, d)),
                          out_shape=(jax.ShapeDtypeStruct((HALO, LANES), F32), jax.ShapeDtypeStruct((t_len, d), F32)),
                          compiler_params=_cparams(("arbitrary",)))(y, target)


def _ple_bwd(dx3, x2, p, w):
    t_len, d = x2.shape
    tm = min(TOKEN_TILE, t_len)

    def body(dx3_ref, x_ref, p_ref, g, wg, wp, dx2_ref, de_ref, h3_ref, dpre_ref, dg_ref):
        x2v, dx3v = x_ref[...], dx3_ref[...]
        hb = _rms(x2v, g[...]).astype(BF16)
        h3_ref[...] = hb
        gate = jax.nn.sigmoid(_mm(hb, wg[...]))
        pb = p_ref[...].astype(BF16)
        e = jnp.concatenate([_mm(pb, wp[k]) for k in range(N_DEV)], axis=1)
        de_ref[...] = (dx3v * gate).astype(BF16)
        dpre = ((dx3v * e) * gate * (1.0 - gate)).astype(BF16)
        dpre_ref[...] = dpre
        dx, dg = _rms_bwd(x2v, g[...], _mm_nt(dpre, wg[...]))
        dx2_ref[...] = dx3v + dx
        _accumulate(dg_ref, pl.program_id(0) == 0, dg)

    consts, const_specs = _operands([w["g_ple"], w["w_ple_gate"], w["w_ple"]])
    out_shape = (jax.ShapeDtypeStruct((t_len, d), F32), jax.ShapeDtypeStruct((t_len, d), BF16),
                 jax.ShapeDtypeStruct((t_len, d), BF16), jax.ShapeDtypeStruct((t_len, d), BF16),
                 jax.ShapeDtypeStruct((1, d), F32))
    return pl.pallas_call(body, name="ple_bwd", grid=(t_len // tm,),
                          in_specs=[_rows(tm, d), _rows(tm, d), _rows(tm, p.shape[1])] + const_specs,
                          out_specs=(_rows(tm, d),) * 4 + (_whole((1, d)),), out_shape=out_shape,
                          compiler_params=_cparams(("arbitrary",)))(dx3, x2, p, *consts)


def _mlp_bwd(dx2, x1, w):
    t_len, d = x1.shape
    tm = min(TOKEN_TILE, t_len)
    fc = w["w_up"][0].shape[3]
    ff = N_DEV * fc

    def body(dx2_ref, x_ref, g, wup, wdn, dx1_ref, r_ref, da_ref, h2_ref, dg_ref):
        x1v, dx2v = x_ref[...], dx2_ref[...]
        hb = _rms(x1v, g[...]).astype(BF16)
        h2_ref[...] = hb
        dxb = dx2v.astype(BF16)
        dh = jnp.zeros((tm, d), F32)
        for k in range(N_DEV):
            a = jnp.maximum(_mm(hb, wup[k]), 0.0)
            r_ref[:, k * fc:(k + 1) * fc] = (a * a).astype(BF16)
            da = (_mm_nt(dxb, wdn[k]) * (2.0 * a)).astype(BF16)
            da_ref[:, k * fc:(k + 1) * fc] = da
            dh = dh + _mm_nt(da, wup[k])
        dx, dg = _rms_bwd(x1v, g[...], dh)
        dx1_ref[...] = dx2v + dx
        _accumulate(dg_ref, pl.program_id(0) == 0, dg)

    consts, const_specs = _operands([w["g_mlp"], w["w_up"], w["w_down"]])
    out_shape = (jax.ShapeDtypeStruct((t_len, d), F32), jax.ShapeDtypeStruct((t_len, ff), BF16),
                 jax.ShapeDtypeStruct((t_len, ff), BF16), jax.ShapeDtypeStruct((t_len, d), BF16),
                 jax.ShapeDtypeStruct((1, d), F32))
    return pl.pallas_call(body, name="mlp_bwd", grid=(t_len // tm,), in_specs=[_rows(tm, d), _rows(tm, d)] + const_specs,
                          out_specs=(_rows(tm, d), _rows(tm, ff), _rows(tm, ff), _rows(tm, d), _whole((1, d))),
                          out_shape=out_shape, compiler_params=_cparams(("arbitrary",)))(dx2, x1, *consts)


def _mix_out_bwd(dx1, attn, conv, w):
    t_len, d = dx1.shape
    tm = min(TOKEN_TILE, t_len)

    def body(dx1_ref, a_ref, c_ref, goa, goc, wo, mixed_ref, da_ref, dc_ref, dgoa_ref, dgoc_ref):
        av, cv = a_ref[...], c_ref[...]
        mixed_ref[...] = jnp.concatenate([_rms(av, goa[...]), _rms(cv, goc[...])], axis=1).astype(BF16)
        dmixed = _mm_nt(dx1_ref[...].astype(BF16), wo[...])
        da, dga = _rms_bwd(av, goa[...], dmixed[:, :ATTN_WIDTH])
        dc, dgc = _rms_bwd(cv, goc[...], dmixed[:, ATTN_WIDTH:])
        da_ref[...] = da
        dc_ref[...] = dc
        first = pl.program_id(0) == 0
        _accumulate(dgoa_ref, first, dga)
        _accumulate(dgoc_ref, first, dgc)

    consts, const_specs = _operands([w["g_out_attn"], w["g_out_conv"], w["w_o"]])
    out_shape = (jax.ShapeDtypeStruct((t_len, d), BF16), jax.ShapeDtypeStruct((t_len, ATTN_WIDTH), F32),
                 jax.ShapeDtypeStruct((t_len, CONV_WIDTH), F32), jax.ShapeDtypeStruct((1, ATTN_WIDTH), F32),
                 jax.ShapeDtypeStruct((1, CONV_WIDTH), F32))
    out_specs = (_rows(tm, d), _rows(tm, ATTN_WIDTH), _rows(tm, CONV_WIDTH), _whole((1, ATTN_WIDTH)),
                 _whole((1, CONV_WIDTH)))
    return pl.pallas_call(body, name="mix_out_bwd", grid=(t_len // tm,),
                          in_specs=[_rows(tm, d), _rows(tm, ATTN_WIDTH), _rows(tm, CONV_WIDTH)] + const_specs,
                          out_specs=out_specs, out_shape=out_shape,
                          compiler_params=_cparams(("arbitrary",)))(dx1, attn, conv, *consts)


def _attn_bwd(qf, kf, kv, o, do, lse):
    t_len = qf.shape[0]
    blk = min(ATTN_BLOCK, t_len)
    nb = t_len // blk
    n_sub = ATTN_KV_SUB if nb % ATTN_KV_SUB == 0 else 1
    reps = blk // LANES
    scale = QK_HEAD ** -0.5

    def body(q_ref, k_ref, kv_ref, o_ref, do_ref, lse_ref, dq_ref, dk_ref, dkv_ref, delta_ref, dob_ref):
        hd = pl.program_id(0)
        lane = lax.broadcasted_iota(jnp.int32, (blk, LANES), 1)
        even = (lane * 0 + hd % 2) == 0
        mine = jnp.where(lane < V_HEAD, 0, 1) == hd % 2
        row = lax.broadcasted_iota(jnp.int32, (blk, blk), 0)
        col = lax.broadcasted_iota(jnp.int32, (blk, blk), 1)
        dq_ref[...] = jnp.zeros_like(dq_ref)

        def prepare(i, carry):
            qs = pl.ds(pl.multiple_of(i * blk, blk), blk)
            dov = do_ref[qs, :]
            prod = jnp.where(mine, dov * o_ref[qs, :], 0.0)
            delta_ref[qs, :] = jnp.broadcast_to(jnp.sum(prod, axis=-1, keepdims=True), (blk, LANES))
            moved = jnp.where(even, pltpu.roll(dov, V_HEAD, 1), dov)
            dob_ref[qs, :] = jnp.where(lane >= V_HEAD, moved, 0.0).astype(BF16)
            return carry
        lax.fori_loop(0, nb, prepare, 0)

        def kvblock(jj, carry):
            base = jj * n_sub
            kss = [pl.ds(pl.multiple_of((base + a) * blk, blk), blk) for a in range(n_sub)]
            k = [k_ref[ks, :] for ks in kss]
            kvv = [kv_ref[ks, :] for ks in kss]

            def products(i):
                qs = pl.ds(pl.multiple_of(i * blk, blk), blk)
                q, dob = q_ref[qs, :], dob_ref[qs, :]
                return tuple((_mm_nt(q, k[a]), _mm_nt(dob, kvv[a])) for a in range(n_sub))

            def qstep(i, raw, accs, kinds):
                qs = pl.ds(pl.multiple_of(i * blk, blk), blk)
                q = q_ref[qs, :]
                dob = dob_ref[qs, :]
                lse_t = jnp.concatenate([lse_ref[qs, :]] * reps, axis=1)
                delta_t = jnp.concatenate([delta_ref[qs, :]] * reps, axis=1)
                new, dq_add = [], None
                for a in range(n_sub):
                    if kinds[a] is None:
                        new.append(accs[a])
                        continue
                    dk_acc, dv_acc = accs[a]
                    s, dp = raw[a]
                    s = s * scale
                    if kinds[a]:
                        s = jnp.where(col <= row, s, -jnp.inf)
                    p = jnp.exp(s - lse_t)
                    ds = (p * (dp - delta_t) * scale).astype(BF16)
                    new.append((dk_acc + _mm_tn(ds, q), dv_acc + _mm_tn(p.astype(BF16), dob)))
                    part = _mm(ds, k[a])
                    dq_add = part if dq_add is None else dq_add + part
                dq_ref[qs, :] += dq_add
                return tuple(new)

            zero = jnp.zeros((blk, LANES), F32)
            accs = ((zero, zero),) * n_sub
            for b in range(n_sub):
                accs = qstep(base + b, products(base + b), accs, tuple((a == b) if a <= b else None for a in range(n_sub)))

            def pipelined(i, carried):
                raw, acc = carried
                return products(jnp.minimum(i + 1, nb - 1)), qstep(i, raw, acc, (False,) * n_sub)

            first = base + n_sub
            _, accs = lax.fori_loop(first, nb, pipelined, (products(jnp.minimum(first, nb - 1)), accs))
            for a in range(n_sub):
                dk_ref[kss[a], :] = accs[a][0]
                dkv_ref[kss[a], :] = accs[a][1]
            return carry
        lax.fori_loop(0, nb // n_sub, kvblock, 0)

    head = pl.BlockSpec((t_len, HEAD_PAD), lambda h: (0, h))
    pair = pl.BlockSpec((t_len, 2 * V_HEAD), lambda h: (0, h // 2))
    out = jax.ShapeDtypeStruct((t_len, N_HEADS * HEAD_PAD), F32)
    return pl.pallas_call(body, name="attn_bwd", grid=(N_HEADS,), in_specs=[head, head, head, pair, pair, head],
                          out_specs=(head, head, head), out_shape=(out, out, out),
                          scratch_shapes=[pltpu.VMEM((t_len, LANES), F32), pltpu.VMEM((t_len, LANES), BF16)],
                          compiler_params=_cparams(("arbitrary",)))(qf, kf, kv, o, do, lse)


def _front_bwd(x, z, dx1, dqf, dkf, dkv_in, dconv, w, tabs):
    t_len, d = x.shape
    tm = min(TOKEN_TILE, t_len)
    nt = t_len // tm
    hb_per_tile = tm // HALO
    n_halo = t_len // HALO
    hp = N_HEADS * HEAD_PAD

    def body(x_ref, z_ref, zp_ref, zn_ref, dx1_ref, dqf_ref, dkf_ref, dkv_ref, dc_ref, dcn_ref,
             gmix, win, gq, wuq, gkv, wukv, gqn, gkn, cw_ref, gm_ref, cos_ref, sin_ref,
             dx_ref, dz_ref, h_ref, qn_ref, kvn_ref, dqr_ref, dkvr_ref,
             dgmix_ref, dgq_ref, dgkv_ref, dgqn_ref, dgkn_ref, dcw_ref, ubuf, dybuf):
        i = pl.program_id(0)
        first = i == 0
        xv, zv = x_ref[...], z_ref[...]
        hb = _rms(xv, gmix[...]).astype(BF16)
        h_ref[...] = hb
        zq, zkv = zv[:, Z_Q[0]:Z_Q[1]], zv[:, Z_KV[0]:Z_KV[1]]
        qnb = _rms(zq, gq[...]).astype(BF16)
        qn_ref[...] = qnb
        kvb = _rms(zkv, gkv[...]).astype(BF16)
        kvn_ref[...] = kvb
        kpe = zv[:, Z_KPE[0]:Z_KPE[1]]
        cos, sin, gm = cos_ref[...], sin_ref[...], gm_ref[...]
        lane = lax.broadcasted_iota(jnp.int32, (tm, LANES), 1)
        is_nope = lane < QK_NOPE
        is_rope = (lane >= QK_NOPE) & (lane < QK_HEAD)
        dkpe = jnp.zeros((tm, LANES), F32)
        dgqn = jnp.zeros((1, LANES), F32)
        dgkn = jnp.zeros((1, LANES), F32)
        dqn = jnp.zeros((tm, Q_LORA), F32)
        dkvn = jnp.zeros((tm, KV_LORA), F32)
        for hd in range(N_HEADS):
            sl = slice(hd * HEAD_PAD, (hd + 1) * HEAD_PAD)
            dxq, dg = _qk_bwd(_mm(qnb, wuq[hd]), gqn[...], dqf_ref[:, sl], cos, sin, gm, lane)
            dxq = dxq.astype(BF16)
            dqr_ref[:, sl] = dxq
            dqn = dqn + _mm_nt(dxq, wuq[hd])
            dgqn = dgqn + dg
            k_raw = jnp.where(is_nope, _mm(kvb, wukv[hd]), 0.0) + kpe
            dxk, dg = _qk_bwd(k_raw, gkn[...], dkf_ref[:, sl], cos, sin, gm, lane)
            dkv = jnp.where(is_nope, dxk, dkv_ref[:, sl]).astype(BF16)
            dkvr_ref[:, sl] = dkv
            dkvn = dkvn + _mm_nt(dkv, wukv[hd])
            dkpe = dkpe + jnp.where(is_rope, dxk, 0.0)
            dgkn = dgkn + dg
        _accumulate(dgqn_ref, first, dgqn)
        _accumulate(dgkn_ref, first, dgkn)
        dzq, dg = _rms_bwd(zq, gq[...], dqn)
        _accumulate(dgq_ref, first, dg)
        dzkv, dg = _rms_bwd(zkv, gkv[...], dkvn)
        _accumulate(dgkv_ref, first, dg)

        gb, gc, xin = zv[:, Z_GB[0]:Z_GB[1]], zv[:, Z_GC[0]:Z_GC[1]], zv[:, Z_XIN[0]:Z_XIN[1]]
        u = gc * xin
        dcv = dc_ref[...]
        dy = dcv * gb
        zp, zn = zp_ref[...], zn_ref[...]
        ubuf[0:HALO, :] = (zp[:, Z_GC[0]:Z_GC[1]] * zp[:, Z_XIN[0]:Z_XIN[1]]) * jnp.where(first, 0.0, 1.0)
        ubuf[HALO:HALO + tm, :] = u
        dybuf[0:tm, :] = dy
        dybuf[tm:tm + HALO, :] = (dcn_ref[...] * zn[:, Z_GB[0]:Z_GB[1]]) * jnp.where(i == nt - 1, 0.0, 1.0)
        cw = cw_ref[...]
        u1, u2 = ubuf[pl.ds(HALO - 1, tm), :], ubuf[pl.ds(HALO - 2, tm), :]
        y = cw[0:1] * u + cw[1:2] * u1 + cw[2:3] * u2
        du = cw[0:1] * dy + cw[1:2] * dybuf[pl.ds(1, tm), :] + cw[2:3] * dybuf[pl.ds(2, tm), :]
        dcw = jnp.concatenate([jnp.sum(dy * u, axis=0, keepdims=True), jnp.sum(dy * u1, axis=0, keepdims=True),
                               jnp.sum(dy * u2, axis=0, keepdims=True), jnp.zeros((HALO - 3, CONV_WIDTH), F32)], axis=0)
        _accumulate(dcw_ref, first, dcw)

        dz_ref[:, Z_Q[0]:Z_Q[1]] = dzq.astype(BF16)
        dz_ref[:, Z_KV[0]:Z_KV[1]] = dzkv.astype(BF16)
        dz_ref[:, Z_GB[0]:Z_GB[1]] = (dcv * y).astype(BF16)
        dz_ref[:, Z_GC[0]:Z_GC[1]] = (du * xin).astype(BF16)
        dz_ref[:, Z_XIN[0]:Z_XIN[1]] = (du * gc).astype(BF16)
        dz_ref[:, Z_KPE[0]:Z_KPE[1]] = dkpe.astype(BF16)
        dx, dg = _rms_bwd(xv, gmix[...], _mm_nt(dz_ref[...], win[...]))
        dx_ref[...] = dx1_ref[...] + dx
        _accumulate(dgmix_ref, first, dg)

    prev_halo = lambda n: pl.BlockSpec((HALO, n), lambda i: (jnp.maximum(i * hb_per_tile - 1, 0), 0))
    next_halo = lambda n: pl.BlockSpec((HALO, n), lambda i: (jnp.minimum((i + 1) * hb_per_tile, n_halo - 1), 0))
    consts, const_specs = _operands([w["g_mix"], w["w_in"], w["g_q_lat"], w["w_uq"], w["g_kv_lat"], w["w_ukv"],
                                     w["g_qn"], w["g_kn"], w["conv_w"], tabs["gm"]])
    in_specs = ([_rows(tm, d), _rows(tm, Z_COLS), prev_halo(Z_COLS), next_halo(Z_COLS), _rows(tm, d), _rows(tm, hp),
                 _rows(tm, hp), _rows(tm, hp), _rows(tm, CONV_WIDTH), next_halo(CONV_WIDTH)]
                + const_specs + [_rows(tm, LANES), _rows(tm, LANES)])
    out_shape = (jax.ShapeDtypeStruct((t_len, d), F32), jax.ShapeDtypeStruct((t_len, Z_COLS), BF16),
                 jax.ShapeDtypeStruct((t_len, d), BF16), jax.ShapeDtypeStruct((t_len, Q_LORA), BF16),
                 jax.ShapeDtypeStruct((t_len, KV_LORA), BF16), jax.ShapeDtypeStruct((t_len, hp), BF16),
                 jax.ShapeDtypeStruct((t_len, hp), BF16),
                 jax.ShapeDtypeStruct((1, d), F32), jax.ShapeDtypeStruct((1, Q_LORA), F32),
                 jax.ShapeDtypeStruct((1, KV_LORA), F32), jax.ShapeDtypeStruct((1, LANES), F32),
                 jax.ShapeDtypeStruct((1, LANES), F32), jax.ShapeDtypeStruct((HALO, CONV_WIDTH), F32))
    out_specs = tuple(_rows(tm, s.shape[1]) for s in out_shape[:7]) + tuple(_whole(s.shape) for s in out_shape[7:])
    return pl.pallas_call(body, name="front_bwd", grid=(nt,), in_specs=in_specs, out_specs=out_specs, out_shape=out_shape,
                          scratch_shapes=[pltpu.VMEM((tm + HALO, CONV_WIDTH), F32), pltpu.VMEM((tm + HALO, CONV_WIDTH), F32)],
                          compiler_params=_cparams(("arbitrary",)))(
                              x, z, z, z, dx1, dqf, dkf, dkv_in, dconv, dconv, *consts, tabs["cos"], tabs["sin"])


def _wgrad(a, b, shard_cols=None, shard_rows=None, into=None, layer=None):
    t_len, kk = a.shape
    nn = b.shape[1]
    tk = kk if kk <= 512 else 512
    tn = shard_cols or next(c for c in (512, 384, 256, 128) if nn % c == 0)
    tt = min(t_len, WGRAD_TOKENS)
    nt = t_len // tt
    per_block = tk // shard_rows if shard_rows else 1
    assert per_block in (1, 4) and (not shard_rows or shard_rows * per_block == tk)
    lay = () if layer is None else (layer,)
    lay_dim = () if layer is None else (None,)

    def body(*refs):
        a_ref, b_ref, o_ref, acc = refs[0], refs[1], refs[-2], refs[-1]
        t = pl.program_id(2)

        @pl.when(t == 0)
        def _():
            acc[...] = jnp.zeros_like(acc)
        acc[...] += _mm_tn(a_ref[...].astype(BF16), b_ref[...].astype(BF16))

        @pl.when(t == nt - 1)
        def _():
            if per_block == 4:
                for k in range(4):
                    o_ref[k % 2, k // 2] = acc[k * shard_rows:(k + 1) * shard_rows, :]
            else:
                o_ref[...] = acc[...]

    if shard_cols:
        shard_shape = (kk, tn)
        out_spec = pl.BlockSpec((None, None) + lay_dim + (tk, tn), lambda i, j, t: (j % 2, j // 2) + lay + (i, 0))
    elif shard_rows and per_block == 1:
        shard_shape = (shard_rows, nn)
        out_spec = pl.BlockSpec((None, None) + lay_dim + (tk, tn), lambda i, j, t: (i % 2, i // 2) + lay + (0, j))
    elif shard_rows:
        shard_shape = (shard_rows, nn)
        out_spec = pl.BlockSpec((2, 2) + lay_dim + (shard_rows, tn), lambda i, j, t: (0, i) + lay + (0, j))
    else:
        shard_shape = None
        out_spec = pl.BlockSpec((tk, tn), lambda i, j, t: (i, j))
    if shard_shape is None:
        out_shape = jax.ShapeDtypeStruct((kk, nn), F32)
    elif into is None:
        out_shape = jax.ShapeDtypeStruct((2, N_CHIP) + shard_shape, F32)
    else:
        out_shape = jax.ShapeDtypeStruct(into.shape, F32)
    in_specs = [pl.BlockSpec((tt, tk), lambda i, j, t: (t, i)), pl.BlockSpec((tt, tn), lambda i, j, t: (t, j))]
    args = [a, b]
    aliases = {}
    if into is not None:
        in_specs.append(pl.BlockSpec(memory_space=pl.ANY))
        args.append(into)
        aliases = {2: 0}
    return pl.pallas_call(body, name="wgrad", grid=(kk // tk, nn // tn, nt), in_specs=in_specs, out_specs=out_spec,
                          out_shape=out_shape, scratch_shapes=[pltpu.VMEM((tk, tn), F32)], input_output_aliases=aliases,
                          compiler_params=_cparams(("parallel", "parallel", "arbitrary")))(*args)


def _my_place():
    return lax.axis_index("x"), lax.axis_index("y"), lax.axis_index("c")


def _any_specs(n):
    return [pl.BlockSpec(memory_space=pl.ANY)] * n


def _all_gather(blocks):
    n = len(blocks)

    def body(*refs):
        x_refs, out_refs = refs[:n], refs[n:2 * n]
        send_sems, recv_sems, local_sems = refs[2 * n:]
        x, y, c = _my_place()
        me, sibling = (x, y, c), (x, y, 1 - c)
        chips = [(1 - x, y), (x, 1 - y), (1 - x, 1 - y)]

        def slot(a, px, py, pc):
            return out_refs[a].at[4 * px + 2 * py + pc]

        def copy(a, k, blk, to, src=None):
            return pltpu.make_async_remote_copy(src_ref=slot(a, *blk) if src is None else src, dst_ref=slot(a, *blk),
                                                send_sem=send_sems.at[a, k], recv_sem=recv_sems.at[a, k],
                                                device_id=to, device_id_type=MESH)

        mine = [pltpu.make_async_copy(x_refs[a], slot(a, *me), local_sems.at[a]) for a in range(n)]
        for cp in mine:
            cp.start()
        started = []
        for a in range(n):
            started.append(copy(a, 0, me, sibling, src=x_refs[a]))
            started += [copy(a, 1 + j, me, (*chip, c), src=x_refs[a]) for j, chip in enumerate(chips)]
        for cp in started:
            cp.start()
        for j, chip in enumerate(chips):
            for a in range(n):
                copy(a, 1 + j, (*chip, c), me).wait_recv()
                passed = copy(a, 4 + j, (*chip, c), sibling)
                passed.start()
                started.append(passed)
        for a in range(n):
            copy(a, 0, sibling, me).wait_recv()
        for j, chip in enumerate(chips):
            for a in range(n):
                copy(a, 4 + j, (*chip, 1 - c), me).wait_recv()
        for cp in started:
            cp.wait_send()
        for cp in mine:
            cp.wait()

    out_shape = tuple(jax.ShapeDtypeStruct((N_DEV,) + b.shape, b.dtype) for b in blocks)
    return pl.pallas_call(body, name="all_gather", out_shape=out_shape, in_specs=_any_specs(n),
                          out_specs=tuple(_any_specs(n)),
                          scratch_shapes=[pltpu.SemaphoreType.DMA((n, 7)), pltpu.SemaphoreType.DMA((n, 7)),
                                          pltpu.SemaphoreType.DMA((n,))])(*blocks)


def _sibling_swap(gs):
    n = len(gs)

    def body(*refs):
        g_refs, out_refs = refs[:n], refs[n:2 * n]
        send_sems, recv_sems = refs[2 * n:]
        x, y, c = _my_place()
        copies = [pltpu.make_async_remote_copy(src_ref=g_refs[a].at[1 - c], dst_ref=out_refs[a], send_sem=send_sems.at[a],
                                               recv_sem=recv_sems.at[a], device_id=(x, y, 1 - c), device_id_type=MESH)
                  for a in range(n)]
        for cp in copies:
            cp.start()
        for cp in copies:
            cp.wait()

    out_shape = tuple(jax.ShapeDtypeStruct(g.shape[1:], g.dtype) for g in gs)
    return pl.pallas_call(body, name="sibling_swap", out_shape=out_shape, in_specs=_any_specs(n),
                          out_specs=tuple(_any_specs(n)),
                          scratch_shapes=[pltpu.SemaphoreType.DMA((n,)), pltpu.SemaphoreType.DMA((n,))])(*gs)


def _chip_exchange(ss):
    n = len(ss)

    def body(*refs):
        s_refs, out_refs = refs[:n], refs[n:2 * n]
        send_sems, recv_sems, local_sems = refs[2 * n:]
        x, y, c = _my_place()
        my_q = 2 * x + y
        chips = [(1 - x, y), (x, 1 - y), (1 - x, 1 - y)]
        local = [pltpu.make_async_copy(s_refs[a].at[my_q], out_refs[a].at[my_q], local_sems.at[a]) for a in range(n)]
        for cp in local:
            cp.start()

        def copy(a, k, src_q, dst_q, chip):
            return pltpu.make_async_remote_copy(src_ref=s_refs[a].at[src_q], dst_ref=out_refs[a].at[dst_q],
                                                send_sem=send_sems.at[a, k], recv_sem=recv_sems.at[a, k],
                                                device_id=(*chip, c), device_id_type=MESH)

        sends = [copy(a, k, 2 * px + py, my_q, (px, py)) for a in range(n) for k, (px, py) in enumerate(chips)]
        for cp in sends:
            cp.start()
        for a in range(n):
            for k, (px, py) in enumerate(chips):
                copy(a, k, my_q, 2 * px + py, (px, py)).wait_recv()
        for cp in sends:
            cp.wait_send()
        for cp in local:
            cp.wait()

    out_shape = tuple(jax.ShapeDtypeStruct(s.shape, s.dtype) for s in ss)
    return pl.pallas_call(body, name="chip_exchange", out_shape=out_shape, in_specs=_any_specs(n),
                          out_specs=tuple(_any_specs(n)),
                          scratch_shapes=[pltpu.SemaphoreType.DMA((n, 3)), pltpu.SemaphoreType.DMA((n, 3)),
                                          pltpu.SemaphoreType.DMA((n,))])(*ss)


def _row_block(rows):
    return ROW_BLOCK if rows % ROW_BLOCK == 0 else rows


def _pair_sum(g, got, c):
    shape = got.shape
    cols = shape[-1]
    g2, got2 = g.reshape(2, -1, cols), got.reshape(-1, cols)
    rows = got2.shape[0]
    rb = _row_block(rows)

    def body(c_ref, g_ref, got_ref, o_ref):
        o_ref[...] = (g_ref[...] + got_ref[...]).astype(BF16)

    grid_spec = pltpu.PrefetchScalarGridSpec(
        num_scalar_prefetch=1, grid=(rows // rb,),
        in_specs=[pl.BlockSpec((None, rb, cols), lambda i, c_ref: (c_ref[0], i, 0)),
                  pl.BlockSpec((rb, cols), lambda i, c_ref: (i, 0))],
        out_specs=pl.BlockSpec((rb, cols), lambda i, c_ref: (i, 0)))
    out = pl.pallas_call(body, name="pair_sum", grid_spec=grid_spec, out_shape=jax.ShapeDtypeStruct((rows, cols), BF16),
                         compiler_params=_cparams(("parallel",)))(c, g2, got2)
    return out.reshape(shape)


def _sum_leading(parts):
    n_part, shape = parts.shape[0], parts.shape[1:]
    cols = shape[-1]
    p2 = parts.reshape(n_part, -1, cols)
    rows = p2.shape[1]
    rb = _row_block(rows)

    def body(p_ref, o_ref):
        acc = p_ref[0].astype(F32)
        for k in range(1, n_part):
            acc = acc + p_ref[k].astype(F32)
        o_ref[...] = acc

    out = pl.pallas_call(body, name="sum_leading", grid=(rows // rb,),
                         in_specs=[pl.BlockSpec((n_part, rb, cols), lambda i: (0, i, 0))], out_specs=_rows(rb, cols),
                         out_shape=jax.ShapeDtypeStruct((rows, cols), F32), compiler_params=_cparams(("parallel",)))(p2)
    return out.reshape(shape)


def _adamw(w, g, m, v):
    shape = w.shape
    two_d = (shape[0] * shape[1], shape[2]) if len(shape) == 3 else shape
    rows, cols = two_d
    rb = _row_block(rows)

    def body(w_ref, g_ref, m_ref, v_ref, d_ref, nm_ref, nv_ref):
        gv = g_ref[...]
        nm = ADAM_B1 * m_ref[...] + (1.0 - ADAM_B1) * gv
        nv = ADAM_B2 * v_ref[...] + (1.0 - ADAM_B2) * jnp.square(gv)
        m_hat = nm / (1.0 - ADAM_B1 ** ADAM_STEP)
        v_hat = nv / (1.0 - ADAM_B2 ** ADAM_STEP)
        d_ref[...] = -ADAM_LR * (m_hat / (jnp.sqrt(v_hat) + ADAM_EPS) + ADAM_WD * w_ref[...])
        nm_ref[...] = nm
        nv_ref[...] = nv

    spec = _rows(rb, cols)
    out = jax.ShapeDtypeStruct(two_d, F32)
    res = pl.pallas_call(body, name="adamw", grid=(rows // rb,), in_specs=[spec] * 4, out_specs=(spec,) * 3,
                         out_shape=(out,) * 3, compiler_params=_cparams(("parallel",)))(
                             *(a.reshape(two_d) for a in (w, g, m, v)))
    return tuple(a.reshape(shape) for a in res)


def _rope_tables(positions):
    t_len = positions.shape[0]
    inv_freq = 1.0 / (ROPE_THETA ** (jnp.arange(0, QK_ROPE, 2, dtype=F32) / QK_ROPE))
    ang = positions.astype(F32)[:, None] * inv_freq
    c, s = jnp.cos(ang), jnp.sin(ang)
    one, zero = jnp.ones((t_len, QK_NOPE), F32), jnp.zeros((t_len, QK_NOPE), F32)
    cos = jnp.concatenate([one, c, c, one[:, :LANES - QK_HEAD]], axis=1)
    sin = jnp.concatenate([zero, -s, s, zero[:, :LANES - QK_HEAD]], axis=1)
    idx = jnp.arange(LANES)
    grp = jnp.where(idx < QK_NOPE, 0, jnp.where(idx < QK_HEAD, 1, 2))
    val = jnp.where(grp == 0, 1.0 / QK_NOPE, jnp.where(grp == 1, 1.0 / QK_ROPE, 0.0))
    gm = jnp.where(grp[:, None] == grp[None, :], val[None, :], 0.0).astype(BF16)
    return {"cos": cos, "sin": sin, "gm": gm}


def _head_gain(g_nope, g_rope):
    return jnp.concatenate([g_nope, g_rope, jnp.zeros((LANES - QK_HEAD,), F32)]).reshape(1, LANES)


def _padded_w_in(shards):
    natural = jnp.concatenate([shards[k] for k in range(N_DEV)], axis=1)
    o2, o3 = Q_LORA + KV_LORA, Q_LORA + KV_LORA + QK_ROPE
    zeros = jnp.zeros((natural.shape[0], QK_NOPE), natural.dtype)
    return jnp.concatenate([natural[:, :o2], natural[:, o3:], zeros, natural[:, o2:o3], zeros[:, :LANES - QK_HEAD]], axis=1)


def _by_owner(shards):
    return jnp.swapaxes(shards.reshape((N_CHIP, 2) + shards.shape[1:]), 0, 1)


def _w_in_grad_shards(d_in):
    o2 = Q_LORA + KV_LORA
    nat = jnp.concatenate([d_in[:, :o2], d_in[:, Z_KPE[0] + QK_NOPE:Z_KPE[0] + QK_HEAD], d_in[:, o2:Z_XIN[1]]], axis=1)
    width = nat.shape[1] // N_DEV
    return _by_owner(jnp.stack([nat[:, k * width:(k + 1) * width] for k in range(N_DEV)]))


def kernel(x, p, positions, g_mix, w_in, g_q_lat, w_uq, g_kv_lat, w_ukv, g_qn_nope, g_qn_rope, g_kn_nope, g_kn_rope, conv_w, g_out_attn, g_out_conv, w_o, g_mlp, w_up, w_down, g_ple, w_ple_gate, w_ple, loss_target, m_g_mix, m_w_in, m_g_q_lat, m_w_uq, m_g_kv_lat, m_w_ukv, m_g_qn_nope, m_g_qn_rope, m_g_kn_nope, m_g_kn_rope, m_conv_w, m_g_out_attn, m_g_out_conv, m_w_o, m_g_mlp, m_w_up, m_w_down, m_g_ple, m_w_ple_gate, m_w_ple, v_g_mix, v_w_in, v_g_q_lat, v_w_uq, v_g_kv_lat, v_w_ukv, v_g_qn_nope, v_g_qn_rope, v_g_kn_nope, v_g_kn_rope, v_conv_w, v_g_out_attn, v_g_out_conv, v_w_o, v_g_mlp, v_w_up, v_w_down, v_g_ple, v_w_ple_gate, v_w_ple):
    given = dict(locals())
    weights = {n: given[n] for n in WEIGHT_NAMES}
    gains = {n: given[n] for n in GAIN_NAMES}
    depth = w_in.shape[0]
    xs, target = x[0], loss_target[0]
    d_model = xs.shape[1]
    c_idx = lax.axis_index("c").reshape(1).astype(jnp.int32)
    uq_cols = w_uq.shape[2]

    to_send = [weights[n].astype(BF16) for n in SHARD_NAMES if n != "conv_w"] + [conv_w]
    to_send[1] = jnp.pad(to_send[1], ((0, 0), (0, 0), (0, HEAD_PAD - uq_cols)))
    got = _all_gather(to_send)
    full = dict(zip([n for n in SHARD_NAMES if n != "conv_w"] + ["conv_w"], got))
    conv_full = jnp.transpose(full["conv_w"], (1, 2, 0, 3)).reshape(depth, conv_w.shape[1], -1)

    tabs = _rope_tables(positions[0])
    layer_w = []
    for layer in range(depth):
        lw = {n: gains[n][layer].reshape(1, -1) for n in GAIN_NAMES}
        lw.update({"w_in": _padded_w_in(full["w_in"][:, layer]), "w_uq": full["w_uq"][:, layer],
                   "w_ukv": full["w_ukv"][:, layer], "w_ple": full["w_ple"][:, layer],
                   "w_o": full["w_o"][:, layer].reshape(d_model, d_model),
                   "w_ple_gate": full["w_ple_gate"][:, layer].reshape(d_model, d_model),
                   "w_up": _of_layer(full["w_up"], layer), "w_down": _of_layer(full["w_down"], layer),
                   "conv_w": jnp.pad(conv_full[layer], ((0, HALO - conv_full.shape[1]), (0, 0))),
                   "g_qn": _head_gain(g_qn_nope[layer], g_qn_rope[layer]),
                   "g_kn": _head_gain(g_kn_nope[layer], g_kn_rope[layer])})
        layer_w.append(lw)

    saved = []
    cur = xs
    for layer in range(depth):
        w = layer_w[layer]
        z, qf, kf, kv, conv = _front_fwd(cur, w, tabs)
        attn, lse = _attn_fwd(qf, kf, kv)
        x1 = _mix_out_fwd(cur, attn, conv, w)
        x2 = _mlp_fwd(x1, w)
        x3 = _ple_fwd(x2, p[layer, 0], w)
        saved.append(dict(x=cur, z=z, qf=qf, kf=kf, kv=kv, conv=conv, attn=attn, lse=lse, x1=x1, x2=x2))
        cur = x3

    sq, dx = _loss_and_grad(cur, target)
    loss = lax.psum(0.5 / d_model * sq[0, 0], ("x", "y", "c"))

    direct = ("w_ukv", "w_o", "w_up", "w_down", "w_ple_gate", "w_ple")
    bufs = {n: lax.empty((2, N_CHIP, depth) + weights[n].shape[1:], F32) for n in direct}
    small = {"w_in": [None] * depth, "w_uq": [None] * depth, "conv_w": [None] * depth}
    gain_grads = [None] * depth
    for layer in reversed(range(depth)):
        w, s = layer_w[layer], saved[layer]
        pl_in = p[layer, 0]
        dx2, de, h3, dpre, dg_ple = _ple_bwd(dx, s["x2"], pl_in, w)
        dx1, r, da, h2, dg_mlp = _mlp_bwd(dx2, s["x1"], w)
        mixed, dattn, dconv, dg_oa, dg_oc = _mix_out_bwd(dx1, s["attn"], s["conv"], w)
        dqf, dkf, dkv = _attn_bwd(s["qf"], s["kf"], s["kv"], s["attn"], dattn, s["lse"])
        (dx0, dz, hb, qn, kvn, dqr, dkvr, dg_mix, dg_q, dg_kv, dg_qn, dg_kn, dcw) = _front_bwd(
            s["x"], s["z"], dx1, dqf, dkf, dkv, dconv, w, tabs)
        small["w_in"][layer] = _w_in_grad_shards(_wgrad(hb, dz))
        small["w_uq"][layer] = _wgrad(qn, dqr, shard_cols=HEAD_PAD)[..., :uq_cols]
        n_taps = conv_w.shape[1]
        small["conv_w"][layer] = _by_owner(jnp.transpose(dcw[:n_taps].reshape(n_taps, N_DEV, -1), (1, 0, 2)))
        bufs["w_ukv"] = _wgrad(kvn, dkvr, shard_cols=HEAD_PAD, into=bufs["w_ukv"], layer=layer)
        bufs["w_o"] = _wgrad(mixed, dx1, shard_rows=w_o.shape[1], into=bufs["w_o"], layer=layer)
        bufs["w_up"] = _wgrad(h2, da, shard_cols=w_up.shape[2], into=bufs["w_up"], layer=layer)
        bufs["w_down"] = _wgrad(r, dx2, shard_rows=w_down.shape[1], into=bufs["w_down"], layer=layer)
        bufs["w_ple_gate"] = _wgrad(h3, dpre, shard_rows=w_ple_gate.shape[1], into=bufs["w_ple_gate"], layer=layer)
        bufs["w_ple"] = _wgrad(pl_in, de, shard_cols=w_ple.shape[2], into=bufs["w_ple"], layer=layer)
        gain_grads[layer] = jnp.concatenate([
            dg_mix[0], dg_q[0], dg_kv[0], dg_qn[0, :QK_NOPE], dg_qn[0, QK_NOPE:QK_HEAD], dg_kn[0, :QK_NOPE],
            dg_kn[0, QK_NOPE:QK_HEAD], dg_oa[0], dg_oc[0], dg_mlp[0], dg_ple[0]])
        dx = dx0
    for n in small:
        bufs[n] = jnp.stack(small[n], axis=2)

    g_all = [bufs[n] for n in SHARD_NAMES]
    from_sibling = _sibling_swap(g_all)
    chip_parts = [_pair_sum(g, got_g, c_idx) for g, got_g in zip(g_all, from_sibling)]
    grads = {n: _sum_leading(part) for n, part in zip(SHARD_NAMES, _chip_exchange(chip_parts))}

    gg = jnp.stack(gain_grads)
    gg_rows = -(-gg.size // (HALO * LANES)) * HALO
    gg_pad = jnp.pad(gg.reshape(-1), (0, gg_rows * LANES - gg.size)).reshape(gg_rows, LANES)
    gg_sum = _sum_leading(_all_gather([gg_pad])[0]).reshape(-1)[:gg.size].reshape(gg.shape)
    off = 0
    for n in GAIN_NAMES:
        width = gains[n].shape[1]
        grads[n] = gg_sum[:, off:off + width]
        off += width

    deltas, new_m, new_v = {}, {}, {}
    for n in WEIGHT_NAMES:
        deltas[n], new_m[n], new_v[n] = _adamw(weights[n], grads[n], given["m_" + n], given["v_" + n])
    return (loss, dx[None], *[grads[n] for n in WEIGHT_NAMES], *[deltas[n] for n in WEIGHT_NAMES],
            *[new_m[n] for n in WEIGHT_NAMES], *[new_v[n] for n in WEIGHT_NAMES])
```

```python
import jax
import jax.numpy as jnp
from jax import lax
from jax.experimental import pallas as pl
from jax.experimental.pallas import tpu as pltpu

F32 = jnp.float32
BF16 = jnp.bfloat16
MESH = pl.DeviceIdType.MESH

N_HEADS = 8
QK_NOPE = 64
QK_ROPE = 32
QK_HEAD = QK_NOPE + QK_ROPE
V_HEAD = 64
HEAD_PAD = 128
Q_LORA = 384
KV_LORA = 256
CONV_WIDTH = 512
ATTN_WIDTH = N_HEADS * V_HEAD
ROPE_THETA = 10000.0
EPS = 1e-6
ADAM_LR, ADAM_B1, ADAM_B2, ADAM_EPS, ADAM_WD, ADAM_STEP = 0.001, 0.9, 0.999, 1e-08, 0.01, 10

Z_Q = (0, 384)
Z_KV = (384, 640)
Z_GB = (640, 1152)
Z_GC = (1152, 1664)
Z_XIN = (1664, 2176)
Z_KPE = (2176, 2304)
Z_COLS = 2304

N_DEV = 8
LANES = 128
V7X_VMEM_LIMIT = 52 * 1024 * 1024
TOKEN_TILE = 256
ATTN_BLOCK = 256
ATTN_Q_SUB = 2
ATTN_KV_SUB = 2
ROW_BLOCK = 512
WGRAD_TOKENS = 2048
HALO = 8

GAIN_NAMES = ("g_mix", "g_q_lat", "g_kv_lat", "g_qn_nope", "g_qn_rope", "g_kn_nope", "g_kn_rope",
              "g_out_attn", "g_out_conv", "g_mlp", "g_ple")
SHARD_NAMES = ("w_in", "w_uq", "w_ukv", "conv_w", "w_o", "w_up", "w_down", "w_ple_gate", "w_ple")
WEIGHT_NAMES = ("g_mix", "w_in", "g_q_lat", "w_uq", "g_kv_lat", "w_ukv", "g_qn_nope", "g_qn_rope", "g_kn_nope",
                "g_kn_rope", "conv_w", "g_out_attn", "g_out_conv", "w_o", "g_mlp", "w_up", "w_down", "g_ple",
                "w_ple_gate", "w_ple")


def _cparams(semantics=None):
    return pltpu.CompilerParams(dimension_semantics=semantics, vmem_limit_bytes=V7X_VMEM_LIMIT)


def _mm(a, b):
    return jnp.dot(a, b, preferred_element_type=F32)


def _mm_nt(a, b):
    return lax.dot_general(a, b, (((1,), (1,)), ((), ())), preferred_element_type=F32)


def _mm_tn(a, b):
    return lax.dot_general(a, b, (((0,), (0,)), ((), ())), preferred_element_type=F32)


def _rms(x, g):
    r = lax.rsqrt(jnp.mean(x * x, axis=-1, keepdims=True) + EPS)
    return (x * r) * g


def _rms_bwd(x, g, dy):
    r = lax.rsqrt(jnp.mean(x * x, axis=-1, keepdims=True) + EPS)
    xh = x * r
    dg = jnp.sum(dy * xh, axis=0, keepdims=True)
    dyg = dy * g
    dx = r * (dyg - xh * jnp.mean(dyg * xh, axis=-1, keepdims=True))
    return dx, dg


def _group_mean(t, gm):
    hi = t.astype(BF16)
    lo = (t - hi.astype(F32)).astype(BF16)
    return _mm(hi, gm) + _mm(lo, gm)


def _swap_rope_halves(x, lane):
    half = QK_ROPE // 2
    swapped = jnp.where(lane < QK_NOPE + half, pltpu.roll(x, LANES - half, 1), pltpu.roll(x, half, 1))
    return jnp.where((lane >= QK_NOPE) & (lane < QK_HEAD), swapped, 0.0)


def _qk_fwd(x, g, cos, sin, gm, lane):
    r = lax.rsqrt(_group_mean(x * x, gm) + EPS)
    n = (x * r) * g
    return n * cos + _swap_rope_halves(n, lane) * sin


def _qk_bwd(x, g, dy, cos, sin, gm, lane):
    r = lax.rsqrt(_group_mean(x * x, gm) + EPS)
    xh = x * r
    dn = dy * cos + _swap_rope_halves(dy * sin, lane)
    dg = jnp.sum(dn * xh, axis=0, keepdims=True)
    dng = dn * g
    dx = r * (dng - xh * _group_mean(dng * xh, gm))
    return dx, dg


def _rows(tm, n):
    return pl.BlockSpec((tm, n), lambda i: (i, 0))


def _whole(shape):
    zeros = (0,) * len(shape)
    return pl.BlockSpec(shape, lambda i: zeros)


def _operands(arrays):
    return list(arrays), [_whole(a.shape) for a in arrays]


def _accumulate(ref, first, value):
    @pl.when(first)
    def _():
        ref[...] = jnp.zeros_like(ref)
    ref[...] += value


def _front_fwd(x, w, tabs):
    t_len, d = x.shape
    tm = min(TOKEN_TILE, t_len)
    hp = N_HEADS * HEAD_PAD

    def body(x_ref, gmix, win, gq, wuq, gkv, wukv, gqn, gkn, cw_ref, gm_ref, cos_ref, sin_ref,
             z_ref, qf_ref, kf_ref, kv_ref, conv_ref, ubuf):
        i = pl.program_id(0)
        h = _rms(x_ref[...], gmix[...])
        z = _mm(h.astype(BF16), win[...])
        z_ref[...] = z
        qnb = _rms(z[:, Z_Q[0]:Z_Q[1]], gq[...]).astype(BF16)
        kvb = _rms(z[:, Z_KV[0]:Z_KV[1]], gkv[...]).astype(BF16)
        kpe = z[:, Z_KPE[0]:Z_KPE[1]]
        cos, sin, gm = cos_ref[...], sin_ref[...], gm_ref[...]
        lane = lax.broadcasted_iota(jnp.int32, (tm, LANES), 1)
        for hd in range(N_HEADS):
            sl = slice(hd * HEAD_PAD, (hd + 1) * HEAD_PAD)
            qf_ref[:, sl] = _qk_fwd(_mm(qnb, wuq[hd]), gqn[...], cos, sin, gm, lane).astype(BF16)
            kv = _mm(kvb, wukv[hd])
            kv_ref[:, sl] = kv.astype(BF16)
            kf_ref[:, sl] = _qk_fwd(jnp.where(lane < QK_NOPE, kv, 0.0) + kpe, gkn[...], cos, sin, gm, lane).astype(BF16)
        u = z[:, Z_GC[0]:Z_GC[1]] * z[:, Z_XIN[0]:Z_XIN[1]]

        @pl.when(i == 0)
        def _():
            ubuf[0:HALO, :] = jnp.zeros((HALO, CONV_WIDTH), F32)
        ubuf[HALO:HALO + tm, :] = u
        cw = cw_ref[...]
        y = cw[0:1] * u + cw[1:2] * ubuf[pl.ds(HALO - 1, tm), :] + cw[2:3] * ubuf[pl.ds(HALO - 2, tm), :]
        conv_ref[...] = z[:, Z_GB[0]:Z_GB[1]] * y
        ubuf[0:HALO, :] = u[tm - HALO:tm, :]

    consts, const_specs = _operands([w["g_mix"], w["w_in"], w["g_q_lat"], w["w_uq"], w["g_kv_lat"], w["w_ukv"],
                                     w["g_qn"], w["g_kn"], w["conv_w"], tabs["gm"]])
    out_shape = (jax.ShapeDtypeStruct((t_len, Z_COLS), F32), jax.ShapeDtypeStruct((t_len, hp), BF16),
                 jax.ShapeDtypeStruct((t_len, hp), BF16), jax.ShapeDtypeStruct((t_len, hp), BF16),
                 jax.ShapeDtypeStruct((t_len, CONV_WIDTH), F32))
    return pl.pallas_call(body, name="front_fwd", grid=(t_len // tm,),
                          in_specs=[_rows(tm, d)] + const_specs + [_rows(tm, LANES), _rows(tm, LANES)],
                          out_specs=tuple(_rows(tm, s.shape[1]) for s in out_shape), out_shape=out_shape,
                          scratch_shapes=[pltpu.VMEM((tm + HALO, CONV_WIDTH), F32)],
                          compiler_params=_cparams(("arbitrary",)))(x, *consts, tabs["cos"], tabs["sin"])


def _attn_fwd(qf, kf, kv, gather=None):
    t_len = qf.shape[0]
    blk = min(ATTN_BLOCK, t_len)
    nb = t_len // blk
    n_sub = ATTN_Q_SUB
    bq = blk // n_sub
    scale = QK_HEAD ** -0.5
    chains = [(hh, a) for hh in range(2) for a in range(n_sub)]

    def body(q_ref, k_ref, kv_ref, o_ref, lse_ref):
        lane = lax.broadcasted_iota(jnp.int32, (bq, LANES), 1)
        row = lax.broadcasted_iota(jnp.int32, (bq, blk), 0)
        col = lax.broadcasted_iota(jnp.int32, (bq, blk), 1)

        def head_cols(hh):
            return slice(hh * HEAD_PAD, (hh + 1) * HEAD_PAD)

        def softmax_step(s, kvv, state, first_row=None):
            m, l, acc = state
            s = s * scale
            if first_row is not None:
                s = jnp.where(col <= row + first_row, s, -jnp.inf)
            m_new = jnp.maximum(m, jnp.max(s, axis=-1, keepdims=True))
            p = jnp.exp(s - m_new)
            alpha = jnp.exp(m - m_new)
            l = alpha * l + jnp.sum(p, axis=-1, keepdims=True)
            acc = alpha * acc + _mm(p.astype(BF16), kvv)
            return m_new, l, acc

        def qblock(i, carry):
            start = pl.multiple_of(i * blk, blk)
            rows = [pl.ds(pl.multiple_of(start + a * bq, bq), bq) for a in range(n_sub)]
            qs = {(hh, a): q_ref[rows[a], head_cols(hh)] for hh, a in chains}

            def scores(j):
                ks = pl.ds(pl.multiple_of(j * blk, blk), blk)
                return tuple(_mm_nt(qs[hh, a], k_ref[ks, head_cols(hh)]) for hh, a in chains)

            def kstep(j, carried, diagonal=False):
                ss, states = carried
                ss_next = ss if diagonal else scores(j + 1)
                ks = pl.ds(pl.multiple_of(j * blk, blk), blk)
                new = tuple(softmax_step(s, kv_ref[ks, head_cols(hh)], st, a * bq if diagonal else None)
                            for (hh, a), s, st in zip(chains, ss, states))
                return ss_next, new

            init = (jnp.full((bq, 1), -jnp.inf, F32), jnp.zeros((bq, 1), F32), jnp.zeros((bq, LANES), F32))
            carried = lax.fori_loop(0, i, kstep, (scores(0), (init,) * len(chains)))
            _, states = kstep(i, carried, diagonal=True)
            for a in range(n_sub):
                (m0, l0, acc0), (m1, l1, acc1) = states[chains.index((0, a))], states[chains.index((1, a))]
                o_ref[rows[a], :] = jnp.where(lane < V_HEAD, pltpu.roll(acc0 / l0, V_HEAD, 1), acc1 / l1)
                lse_ref[rows[a], head_cols(0)] = jnp.broadcast_to(m0 + jnp.log(l0), (bq, LANES))
                lse_ref[rows[a], head_cols(1)] = jnp.broadcast_to(m1 + jnp.log(l1), (bq, LANES))
            return carry

        lax.fori_loop(0, nb, qblock, 0)

    n_steps = N_HEADS // 2
    n_gather = len(gather[0]) if gather else 0

    def hosting_body(*refs):
        q_ref, k_ref, kv_ref = refs[:3]
        x_refs = [r.at[gather[1]] for r in refs[3:3 + n_gather]]
        o_ref, lse_ref = refs[3 + n_gather:5 + n_gather]
        start, forward, finish = _gather_phases(x_refs, refs[5 + n_gather:5 + 2 * n_gather], *refs[5 + 2 * n_gather:])
        step = pl.program_id(0)
        pl.when(step == 0)(start)
        pl.when(step == n_steps // 2)(forward)
        body(q_ref, k_ref, kv_ref, o_ref, lse_ref)
        pl.when(step == n_steps - 1)(finish)

    heads = pl.BlockSpec((t_len, 2 * HEAD_PAD), lambda h: (0, h))
    pair = pl.BlockSpec((t_len, 2 * V_HEAD), lambda h: (0, h))
    out_shape = (jax.ShapeDtypeStruct((t_len, ATTN_WIDTH), F32), jax.ShapeDtypeStruct((t_len, N_HEADS * LANES), F32))
    if not gather:
        return pl.pallas_call(body, name="attn_fwd", grid=(n_steps,), in_specs=[heads, heads, heads],
                              out_specs=(pair, heads), out_shape=out_shape,
                              compiler_params=_cparams(("arbitrary",)))(qf, kf, kv)
    shards = [s[gather[1]] for s in gather[0]]
    res = pl.pallas_call(hosting_body, name="attn_fwd_gather", grid=(n_steps,),
                         in_specs=[heads, heads, heads] + _any_specs(n_gather),
                         out_specs=(pair, heads) + tuple(_any_specs(n_gather)),
                         out_shape=out_shape + _gather_out_shape(shards), scratch_shapes=_gather_semaphores(n_gather),
                         compiler_params=_cparams(("arbitrary",)))(qf, kf, kv, *gather[0])
    return res[0], res[1], res[2:]


def _mix_out_fwd(x, attn, conv, w):
    t_len, d = x.shape
    tm = min(TOKEN_TILE, t_len)

    def body(x_ref, a_ref, c_ref, goa, goc, wo, x1_ref):
        mixed = jnp.concatenate([_rms(a_ref[...], goa[...]), _rms(c_ref[...], goc[...])], axis=1)
        x1_ref[...] = x_ref[...] + _mm(mixed.astype(BF16), wo[...])

    consts, const_specs = _operands([w["g_out_attn"], w["g_out_conv"], w["w_o"]])
    return pl.pallas_call(body, name="mix_out_fwd", grid=(t_len // tm,),
                          in_specs=[_rows(tm, d), _rows(tm, ATTN_WIDTH), _rows(tm, CONV_WIDTH)] + const_specs,
                          out_specs=_rows(tm, d), out_shape=jax.ShapeDtypeStruct((t_len, d), F32),
                          compiler_params=_cparams(("parallel",)))(x, attn, conv, *consts)


def _mlp_fwd(x1, w):
    t_len, d = x1.shape
    tm = min(TOKEN_TILE, t_len)

    def body(x_ref, g, wup, wdn, x2_ref):
        x1v = x_ref[...]
        hb = _rms(x1v, g[...]).astype(BF16)
        acc = x1v
        for k in range(N_DEV):
            a = jnp.maximum(_mm(hb, wup[k]), 0.0)
            acc = acc + _mm((a * a).astype(BF16), wdn[k])
        x2_ref[...] = acc

    consts, const_specs = _operands([w["g_mlp"], w["w_up"], w["w_down"]])
    return pl.pallas_call(body, name="mlp_fwd", grid=(t_len // tm,), in_specs=[_rows(tm, d)] + const_specs,
                          out_specs=_rows(tm, d), out_shape=jax.ShapeDtypeStruct((t_len, d), F32),
                          compiler_params=_cparams(("parallel",)))(x1, *consts)


def _ple_fwd(x2, p, w):
    t_len, d = x2.shape
    tm = min(TOKEN_TILE, t_len)

    def body(x_ref, p_ref, g, wg, wp, x3_ref):
        x2v = x_ref[...]
        gate = jax.nn.sigmoid(_mm(_rms(x2v, g[...]).astype(BF16), wg[...]))
        pb = p_ref[...].astype(BF16)
        e = jnp.concatenate([_mm(pb, wp[k]) for k in range(N_DEV)], axis=1)
        x3_ref[...] = x2v + gate * e

    consts, const_specs = _operands([w["g_ple"], w["w_ple_gate"], w["w_ple"]])
    return pl.pallas_call(body, name="ple_fwd", grid=(t_len // tm,),
                          in_specs=[_rows(tm, d), _rows(tm, p.shape[1])] + const_specs, out_specs=_rows(tm, d),
                          out_shape=jax.ShapeDtypeStruct((t_len, d), F32),
                          compiler_params=_cparams(("parallel",)))(x2, p, *consts)


def _loss_and_grad(y, target):
    t_len, d = y.shape
    tm = min(TOKEN_TILE, t_len)

    def body(y_ref, t_ref, sq_ref, dy_ref):
        err = y_ref[...] - t_ref[...]
        dy_ref[...] = err / d
        total = jnp.sum(jnp.sum(err * err, axis=0, keepdims=True), axis=1, keepdims=True)
        _accumulate(sq_ref, pl.program_id(0) == 0, jnp.broadcast_to(total, (HALO, LANES)))

    return pl.pallas_call(body, name="loss_grad", grid=(t_len // tm,), in_specs=[_rows(tm, d), _rows(tm, d)],
                          out_specs=(_whole((HALO, LANES)), _rows(tm, d)),
                          out_shape=(jax.ShapeDtypeStruct((HALO, LANES), F32), jax.ShapeDtypeStruct((t_len, d), F32)),
                          compiler_params=_cparams(("arbitrary",)))(y, target)


def _ple_bwd(dx3, x2, p, w):
    t_len, d = x2.shape
    tm = min(TOKEN_TILE, t_len)

    def body(dx3_ref, x_ref, p_ref, g, wg, wp, dx2_ref, de_ref, h3_ref, dpre_ref, dg_ref):
        x2v, dx3v = x_ref[...], dx3_ref[...]
        hb = _rms(x2v, g[...]).astype(BF16)
        h3_ref[...] = hb
        gate = jax.nn.sigmoid(_mm(hb, wg[...]))
        pb = p_ref[...].astype(BF16)
        e = jnp.concatenate([_mm(pb, wp[k]) for k in range(N_DEV)], axis=1)
        de_ref[...] = (dx3v * gate).astype(BF16)
        dpre = ((dx3v * e) * gate * (1.0 - gate)).astype(BF16)
        dpre_ref[...] = dpre
        dx, dg = _rms_bwd(x2v, g[...], _mm_nt(dpre, wg[...]))
        dx2_ref[...] = dx3v + dx
        _accumulate(dg_ref, pl.program_id(0) == 0, dg)

    consts, const_specs = _operands([w["g_ple"], w["w_ple_gate"], w["w_ple"]])
    out_shape = (jax.ShapeDtypeStruct((t_len, d), F32), jax.ShapeDtypeStruct((t_len, d), BF16),
                 jax.ShapeDtypeStruct((t_len, d), BF16), jax.ShapeDtypeStruct((t_len, d), BF16),
                 jax.ShapeDtypeStruct((1, d), F32))
    return pl.pallas_call(body, name="ple_bwd", grid=(t_len // tm,),
                          in_specs=[_rows(tm, d), _rows(tm, d), _rows(tm, p.shape[1])] + const_specs,
                          out_specs=(_rows(tm, d),) * 4 + (_whole((1, d)),), out_shape=out_shape,
                          compiler_params=_cparams(("arbitrary",)))(dx3, x2, p, *consts)


def _mlp_bwd(dx2, x1, w):
    t_len, d = x1.shape
    tm = min(TOKEN_TILE, t_len)
    fc = w["w_up"].shape[2]
    ff = N_DEV * fc

    def body(dx2_ref, x_ref, g, wup, wdn, dx1_ref, r_ref, da_ref, h2_ref, dg_ref):
        x1v, dx2v = x_ref[...], dx2_ref[...]
        hb = _rms(x1v, g[...]).astype(BF16)
        h2_ref[...] = hb
        dxb = dx2v.astype(BF16)
        dh = jnp.zeros((tm, d), F32)
        for k in range(N_DEV):
            a = jnp.maximum(_mm(hb, wup[k]), 0.0)
            r_ref[:, k * fc:(k + 1) * fc] = (a * a).astype(BF16)
            da = (_mm_nt(dxb, wdn[k]) * (2.0 * a)).astype(BF16)
            da_ref[:, k * fc:(k + 1) * fc] = da
            dh = dh + _mm_nt(da, wup[k])
        dx, dg = _rms_bwd(x1v, g[...], dh)
        dx1_ref[...] = dx2v + dx
        _accumulate(dg_ref, pl.program_id(0) == 0, dg)

    consts, const_specs = _operands([w["g_mlp"], w["w_up"], w["w_down"]])
    out_shape = (jax.ShapeDtypeStruct((t_len, d), F32), jax.ShapeDtypeStruct((t_len, ff), BF16),
                 jax.ShapeDtypeStruct((t_len, ff), BF16), jax.ShapeDtypeStruct((t_len, d), BF16),
                 jax.ShapeDtypeStruct((1, d), F32))
    return pl.pallas_call(body, name="mlp_bwd", grid=(t_len // tm,), in_specs=[_rows(tm, d), _rows(tm, d)] + const_specs,
                          out_specs=(_rows(tm, d), _rows(tm, ff), _rows(tm, ff), _rows(tm, d), _whole((1, d))),
                          out_shape=out_shape, compiler_params=_cparams(("arbitrary",)))(dx2, x1, *consts)


def _mix_out_bwd(dx1, attn, conv, w):
    t_len, d = dx1.shape
    tm = min(TOKEN_TILE, t_len)

    def body(dx1_ref, a_ref, c_ref, goa, goc, wo, mixed_ref, da_ref, dc_ref, dgoa_ref, dgoc_ref):
        av, cv = a_ref[...], c_ref[...]
        mixed_ref[...] = jnp.concatenate([_rms(av, goa[...]), _rms(cv, goc[...])], axis=1).astype(BF16)
        dmixed = _mm_nt(dx1_ref[...].astype(BF16), wo[...])
        da, dga = _rms_bwd(av, goa[...], dmixed[:, :ATTN_WIDTH])
        dc, dgc = _rms_bwd(cv, goc[...], dmixed[:, ATTN_WIDTH:])
        da_ref[...] = da
        dc_ref[...] = dc
        first = pl.program_id(0) == 0
        _accumulate(dgoa_ref, first, dga)
        _accumulate(dgoc_ref, first, dgc)

    consts, const_specs = _operands([w["g_out_attn"], w["g_out_conv"], w["w_o"]])
    out_shape = (jax.ShapeDtypeStruct((t_len, d), BF16), jax.ShapeDtypeStruct((t_len, ATTN_WIDTH), F32),
                 jax.ShapeDtypeStruct((t_len, CONV_WIDTH), F32), jax.ShapeDtypeStruct((1, ATTN_WIDTH), F32),
                 jax.ShapeDtypeStruct((1, CONV_WIDTH), F32))
    out_specs = (_rows(tm, d), _rows(tm, ATTN_WIDTH), _rows(tm, CONV_WIDTH), _whole((1, ATTN_WIDTH)),
                 _whole((1, CONV_WIDTH)))
    return pl.pallas_call(body, name="mix_out_bwd", grid=(t_len // tm,),
                          in_specs=[_rows(tm, d), _rows(tm, ATTN_WIDTH), _rows(tm, CONV_WIDTH)] + const_specs,
                          out_specs=out_specs, out_shape=out_shape,
                          compiler_params=_cparams(("arbitrary",)))(dx1, attn, conv, *consts)


def _attn_bwd(qf, kf, kv, o, do, lse, scatter=None):
    t_len = qf.shape[0]
    blk = min(ATTN_BLOCK, t_len)
    nb = t_len // blk
    n_sub = ATTN_KV_SUB if nb % ATTN_KV_SUB == 0 else 1
    reps = blk // LANES
    scale = QK_HEAD ** -0.5

    def body(q_ref, k_ref, kv_ref, o_ref, do_ref, lse_ref, dq_ref, dk_ref, dkv_ref, delta_ref, dob_ref):
        hd = pl.program_id(0)
        lane = lax.broadcasted_iota(jnp.int32, (blk, LANES), 1)
        even = (lane * 0 + hd % 2) == 0
        mine = jnp.where(lane < V_HEAD, 0, 1) == hd % 2
        row = lax.broadcasted_iota(jnp.int32, (blk, blk), 0)
        col = lax.broadcasted_iota(jnp.int32, (blk, blk), 1)
        dq_ref[...] = jnp.zeros_like(dq_ref)

        def prepare(i, carry):
            qs = pl.ds(pl.multiple_of(i * blk, blk), blk)
            dov = do_ref[qs, :]
            prod = jnp.where(mine, dov * o_ref[qs, :], 0.0)
            delta_ref[qs, :] = jnp.broadcast_to(jnp.sum(prod, axis=-1, keepdims=True), (blk, LANES))
            moved = jnp.where(even, pltpu.roll(dov, V_HEAD, 1), dov)
            dob_ref[qs, :] = jnp.where(lane >= V_HEAD, moved, 0.0).astype(BF16)
            return carry
        lax.fori_loop(0, nb, prepare, 0)

        def kvblock(jj, carry):
            base = jj * n_sub
            kss = [pl.ds(pl.multiple_of((base + a) * blk, blk), blk) for a in range(n_sub)]
            k = [k_ref[ks, :] for ks in kss]
            kvv = [kv_ref[ks, :] for ks in kss]

            def products(i):
                qs = pl.ds(pl.multiple_of(i * blk, blk), blk)
                q, dob = q_ref[qs, :], dob_ref[qs, :]
                return tuple((_mm_nt(q, k[a]), _mm_nt(dob, kvv[a])) for a in range(n_sub))

            def qstep(i, raw, accs, kinds):
                qs = pl.ds(pl.multiple_of(i * blk, blk), blk)
                q = q_ref[qs, :]
                dob = dob_ref[qs, :]
                lse_t = jnp.concatenate([lse_ref[qs, :]] * reps, axis=1)
                delta_t = jnp.concatenate([delta_ref[qs, :]] * reps, axis=1)
                new, dq_add = [], None
                for a in range(n_sub):
                    if kinds[a] is None:
                        new.append(accs[a])
                        continue
                    dk_acc, dv_acc = accs[a]
                    s, dp = raw[a]
                    s = s * scale
                    if kinds[a]:
                        s = jnp.where(col <= row, s, -jnp.inf)
                    p = jnp.exp(s - lse_t)
                    ds = (p * (dp - delta_t) * scale).astype(BF16)
                    new.append((dk_acc + _mm_tn(ds, q), dv_acc + _mm_tn(p.astype(BF16), dob)))
                    part = _mm(ds, k[a])
                    dq_add = part if dq_add is None else dq_add + part
                dq_ref[qs, :] += dq_add
                return tuple(new)

            zero = jnp.zeros((blk, LANES), F32)
            accs = ((zero, zero),) * n_sub
            for b in range(n_sub):
                accs = qstep(base + b, products(base + b), accs, tuple((a == b) if a <= b else None for a in range(n_sub)))

            def pipelined(i, carried):
                raw, acc = carried
                return products(jnp.minimum(i + 1, nb - 1)), qstep(i, raw, acc, (False,) * n_sub)

            first = base + n_sub
            _, accs = lax.fori_loop(first, nb, pipelined, (products(jnp.minimum(first, nb - 1)), accs))
            for a in range(n_sub):
                dk_ref[kss[a], :] = accs[a][0]
                dkv_ref[kss[a], :] = accs[a][1]
            return carry
        lax.fori_loop(0, nb // n_sub, kvblock, 0)

    n_sc = len(scatter[0]) if scatter else 0

    def hosting_body(*refs):
        ins, rest = refs[:6], refs[6 + 2 * n_sc:]
        parts = refs[6:6 + n_sc]
        outs, landed, sems, scratch = rest[:3], rest[3:3 + n_sc], rest[3 + n_sc:6 + n_sc], rest[6 + n_sc:]
        start, finish = _scatter_phases(parts, landed, *sems, scatter[2])
        hd = pl.program_id(0)
        pl.when(hd == 0)(start)
        body(*ins, *outs, *scratch)
        pl.when(hd == N_HEADS - 1)(finish)

    head = pl.BlockSpec((t_len, HEAD_PAD), lambda h: (0, h))
    pair = pl.BlockSpec((t_len, 2 * V_HEAD), lambda h: (0, h // 2))
    out = jax.ShapeDtypeStruct((t_len, N_HEADS * HEAD_PAD), F32)
    vmem_scratch = [pltpu.VMEM((t_len, LANES), F32), pltpu.VMEM((t_len, LANES), BF16)]
    if not scatter:
        return pl.pallas_call(body, name="attn_bwd", grid=(N_HEADS,), in_specs=[head, head, head, pair, pair, head],
                              out_specs=(head, head, head), out_shape=(out, out, out), scratch_shapes=vmem_scratch,
                              compiler_params=_cparams(("arbitrary",)))(qf, kf, kv, o, do, lse)
    res = pl.pallas_call(hosting_body, name="attn_bwd_scatter", grid=(N_HEADS,),
                         in_specs=[head, head, head, pair, pair, head] + _any_specs(2 * n_sc),
                         out_specs=(head, head, head) + tuple(_any_specs(n_sc)),
                         out_shape=(out, out, out) + _same_shapes(scatter[1]),
                         scratch_shapes=_scatter_semaphores(n_sc) + vmem_scratch,
                         input_output_aliases={6 + n_sc + a: 3 + a for a in range(n_sc)},
                         compiler_params=_cparams(("arbitrary",)))(qf, kf, kv, o, do, lse, *scatter[0], *scatter[1])
    return res[0], res[1], res[2], res[3:]


def _front_bwd(x, z, dx1, dqf, dkf, dkv_in, dconv, w, tabs):
    t_len, d = x.shape
    tm = min(TOKEN_TILE, t_len)
    nt = t_len // tm
    hb_per_tile = tm // HALO
    n_halo = t_len // HALO
    hp = N_HEADS * HEAD_PAD

    def body(x_ref, z_ref, zp_ref, zn_ref, dx1_ref, dqf_ref, dkf_ref, dkv_ref, dc_ref, dcn_ref,
             gmix, win, gq, wuq, gkv, wukv, gqn, gkn, cw_ref, gm_ref, cos_ref, sin_ref,
             dx_ref, dz_ref, h_ref, qn_ref, kvn_ref, dqr_ref, dkvr_ref,
             dgmix_ref, dgq_ref, dgkv_ref, dgqn_ref, dgkn_ref, dcw_ref, ubuf, dybuf):
        i = pl.program_id(0)
        first = i == 0
        xv, zv = x_ref[...], z_ref[...]
        hb = _rms(xv, gmix[...]).astype(BF16)
        h_ref[...] = hb
        zq, zkv = zv[:, Z_Q[0]:Z_Q[1]], zv[:, Z_KV[0]:Z_KV[1]]
        qnb = _rms(zq, gq[...]).astype(BF16)
        qn_ref[...] = qnb
        kvb = _rms(zkv, gkv[...]).astype(BF16)
        kvn_ref[...] = kvb
        kpe = zv[:, Z_KPE[0]:Z_KPE[1]]
        cos, sin, gm = cos_ref[...], sin_ref[...], gm_ref[...]
        lane = lax.broadcasted_iota(jnp.int32, (tm, LANES), 1)
        is_nope = lane < QK_NOPE
        is_rope = (lane >= QK_NOPE) & (lane < QK_HEAD)
        dkpe = jnp.zeros((tm, LANES), F32)
        dgqn = jnp.zeros((1, LANES), F32)
        dgkn = jnp.zeros((1, LANES), F32)
        dqn = jnp.zeros((tm, Q_LORA), F32)
        dkvn = jnp.zeros((tm, KV_LORA), F32)
        for hd in range(N_HEADS):
            sl = slice(hd * HEAD_PAD, (hd + 1) * HEAD_PAD)
            dxq, dg = _qk_bwd(_mm(qnb, wuq[hd]), gqn[...], dqf_ref[:, sl], cos, sin, gm, lane)
            dxq = dxq.astype(BF16)
            dqr_ref[:, sl] = dxq
            dqn = dqn + _mm_nt(dxq, wuq[hd])
            dgqn = dgqn + dg
            k_raw = jnp.where(is_nope, _mm(kvb, wukv[hd]), 0.0) + kpe
            dxk, dg = _qk_bwd(k_raw, gkn[...], dkf_ref[:, sl], cos, sin, gm, lane)
            dkv = jnp.where(is_nope, dxk, dkv_ref[:, sl]).astype(BF16)
            dkvr_ref[:, sl] = dkv
            dkvn = dkvn + _mm_nt(dkv, wukv[hd])
            dkpe = dkpe + jnp.where(is_rope, dxk, 0.0)
            dgkn = dgkn + dg
        _accumulate(dgqn_ref, first, dgqn)
        _accumulate(dgkn_ref, first, dgkn)
        dzq, dg = _rms_bwd(zq, gq[...], dqn)
        _accumulate(dgq_ref, first, dg)
        dzkv, dg = _rms_bwd(zkv, gkv[...], dkvn)
        _accumulate(dgkv_ref, first, dg)

        gb, gc, xin = zv[:, Z_GB[0]:Z_GB[1]], zv[:, Z_GC[0]:Z_GC[1]], zv[:, Z_XIN[0]:Z_XIN[1]]
        u = gc * xin
        dcv = dc_ref[...]
        dy = dcv * gb
        zp, zn = zp_ref[...], zn_ref[...]
        ubuf[0:HALO, :] = (zp[:, Z_GC[0]:Z_GC[1]] * zp[:, Z_XIN[0]:Z_XIN[1]]) * jnp.where(first, 0.0, 1.0)
        ubuf[HALO:HALO + tm, :] = u
        dybuf[0:tm, :] = dy
        dybuf[tm:tm + HALO, :] = (dcn_ref[...] * zn[:, Z_GB[0]:Z_GB[1]]) * jnp.where(i == nt - 1, 0.0, 1.0)
        cw = cw_ref[...]
        u1, u2 = ubuf[pl.ds(HALO - 1, tm), :], ubuf[pl.ds(HALO - 2, tm), :]
        y = cw[0:1] * u + cw[1:2] * u1 + cw[2:3] * u2
        du = cw[0:1] * dy + cw[1:2] * dybuf[pl.ds(1, tm), :] + cw[2:3] * dybuf[pl.ds(2, tm), :]
        dcw = jnp.concatenate([jnp.sum(dy * u, axis=0, keepdims=True), jnp.sum(dy * u1, axis=0, keepdims=True),
                               jnp.sum(dy * u2, axis=0, keepdims=True), jnp.zeros((HALO - 3, CONV_WIDTH), F32)], axis=0)
        _accumulate(dcw_ref, first, dcw)

        dz_ref[:, Z_Q[0]:Z_Q[1]] = dzq.astype(BF16)
        dz_ref[:, Z_KV[0]:Z_KV[1]] = dzkv.astype(BF16)
        dz_ref[:, Z_GB[0]:Z_GB[1]] = (dcv * y).astype(BF16)
        dz_ref[:, Z_GC[0]:Z_GC[1]] = (du * xin).astype(BF16)
        dz_ref[:, Z_XIN[0]:Z_XIN[1]] = (du * gc).astype(BF16)
        dz_ref[:, Z_KPE[0]:Z_KPE[1]] = dkpe.astype(BF16)
        dx, dg = _rms_bwd(xv, gmix[...], _mm_nt(dz_ref[...], win[...]))
        dx_ref[...] = dx1_ref[...] + dx
        _accumulate(dgmix_ref, first, dg)

    prev_halo = lambda n: pl.BlockSpec((HALO, n), lambda i: (jnp.maximum(i * hb_per_tile - 1, 0), 0))
    next_halo = lambda n: pl.BlockSpec((HALO, n), lambda i: (jnp.minimum((i + 1) * hb_per_tile, n_halo - 1), 0))
    consts, const_specs = _operands([w["g_mix"], w["w_in"], w["g_q_lat"], w["w_uq"], w["g_kv_lat"], w["w_ukv"],
                                     w["g_qn"], w["g_kn"], w["conv_w"], tabs["gm"]])
    in_specs = ([_rows(tm, d), _rows(tm, Z_COLS), prev_halo(Z_COLS), next_halo(Z_COLS), _rows(tm, d), _rows(tm, hp),
                 _rows(tm, hp), _rows(tm, hp), _rows(tm, CONV_WIDTH), next_halo(CONV_WIDTH)]
                + const_specs + [_rows(tm, LANES), _rows(tm, LANES)])
    out_shape = (jax.ShapeDtypeStruct((t_len, d), F32), jax.ShapeDtypeStruct((t_len, Z_COLS), BF16),
                 jax.ShapeDtypeStruct((t_len, d), BF16), jax.ShapeDtypeStruct((t_len, Q_LORA), BF16),
                 jax.ShapeDtypeStruct((t_len, KV_LORA), BF16), jax.ShapeDtypeStruct((t_len, hp), BF16),
                 jax.ShapeDtypeStruct((t_len, hp), BF16),
                 jax.ShapeDtypeStruct((1, d), F32), jax.ShapeDtypeStruct((1, Q_LORA), F32),
                 jax.ShapeDtypeStruct((1, KV_LORA), F32), jax.ShapeDtypeStruct((1, LANES), F32),
                 jax.ShapeDtypeStruct((1, LANES), F32), jax.ShapeDtypeStruct((HALO, CONV_WIDTH), F32))
    out_specs = tuple(_rows(tm, s.shape[1]) for s in out_shape[:7]) + tuple(_whole(s.shape) for s in out_shape[7:])
    return pl.pallas_call(body, name="front_bwd", grid=(nt,), in_specs=in_specs, out_specs=out_specs, out_shape=out_shape,
                          scratch_shapes=[pltpu.VMEM((tm + HALO, CONV_WIDTH), F32), pltpu.VMEM((tm + HALO, CONV_WIDTH), F32)],
                          compiler_params=_cparams(("arbitrary",)))(
                              x, z, z, z, dx1, dqf, dkf, dkv_in, dconv, dconv, *consts, tabs["cos"], tabs["sin"])


def _wgrad(a, b, shard_cols=None, out_dtype=BF16):
    t_len, kk = a.shape
    nn = b.shape[1]
    tk = kk if kk <= 512 else 512
    tn = shard_cols or next(c for c in (512, 384, 256, 128) if nn % c == 0)
    tt = min(t_len, WGRAD_TOKENS)
    nt = t_len // tt

    def body(a_ref, b_ref, o_ref, acc):
        t = pl.program_id(2)

        @pl.when(t == 0)
        def _():
            acc[...] = jnp.zeros_like(acc)
        acc[...] += _mm_tn(a_ref[...].astype(BF16), b_ref[...].astype(BF16))

        @pl.when(t == nt - 1)
        def _():
            o_ref[...] = acc[...].astype(out_dtype)

    if shard_cols:
        out_shape = jax.ShapeDtypeStruct((nn // tn, kk, tn), out_dtype)
        out_spec = pl.BlockSpec((None, tk, tn), lambda i, j, t: (j, i, 0))
    else:
        out_shape = jax.ShapeDtypeStruct((kk, nn), out_dtype)
        out_spec = pl.BlockSpec((tk, tn), lambda i, j, t: (i, j))
    return pl.pallas_call(body, name="wgrad", grid=(kk // tk, nn // tn, nt),
                          in_specs=[pl.BlockSpec((tt, tk), lambda i, j, t: (t, i)),
                                    pl.BlockSpec((tt, tn), lambda i, j, t: (t, j))],
                          out_specs=out_spec, out_shape=out_shape, scratch_shapes=[pltpu.VMEM((tk, tn), F32)],
                          compiler_params=_cparams(("parallel", "parallel", "arbitrary")))(a, b)


def _my_place():
    return lax.axis_index("x"), lax.axis_index("y"), lax.axis_index("c")


def _any_specs(n):
    return [pl.BlockSpec(memory_space=pl.ANY)] * n


def _all_gather(blocks):
    n = len(blocks)

    def body(*refs):
        start, forward, finish = _gather_phases(refs[:n], refs[n:2 * n], *refs[2 * n:])
        start()
        forward()
        finish()

    return pl.pallas_call(body, name="all_gather", out_shape=_gather_out_shape(blocks), in_specs=_any_specs(n),
                          out_specs=tuple(_any_specs(n)), scratch_shapes=_gather_semaphores(n))(*blocks)


def _gather_out_shape(blocks):
    return tuple(jax.ShapeDtypeStruct((N_DEV,) + b.shape, b.dtype) for b in blocks)


def _gather_semaphores(n):
    return [pltpu.SemaphoreType.DMA((n, 7)), pltpu.SemaphoreType.DMA((n, 7)), pltpu.SemaphoreType.DMA((n,))]


def _gather_phases(x_refs, out_refs, send_sems, recv_sems, local_sems):
    n = len(x_refs)
    x, y, c = _my_place()
    me, sibling = (x, y, c), (x, y, 1 - c)
    chips = [(1 - x, y), (x, 1 - y), (1 - x, 1 - y)]

    def slot(a, px, py, pc):
        return out_refs[a].at[4 * px + 2 * py + pc]

    def copy(a, k, blk, to, src=None):
        return pltpu.make_async_remote_copy(src_ref=slot(a, *blk) if src is None else src, dst_ref=slot(a, *blk),
                                            send_sem=send_sems.at[a, k], recv_sem=recv_sems.at[a, k],
                                            device_id=to, device_id_type=MESH)

    def own(a):
        return pltpu.make_async_copy(x_refs[a], slot(a, *me), local_sems.at[a])

    def first_hop(a):
        return [copy(a, 0, me, sibling, src=x_refs[a])] + [copy(a, 1 + j, me, (*chip, c), src=x_refs[a])
                                                           for j, chip in enumerate(chips)]

    def passed_on(a):
        return [copy(a, 4 + j, (*chip, c), sibling) for j, chip in enumerate(chips)]

    def start():
        for a in range(n):
            own(a).start()
        for a in range(n):
            for cp in first_hop(a):
                cp.start()

    def forward():
        for j, chip in enumerate(chips):
            for a in range(n):
                copy(a, 1 + j, (*chip, c), me).wait_recv()
                passed_on(a)[j].start()

    def finish():
        for a in range(n):
            copy(a, 0, sibling, me).wait_recv()
        for j, chip in enumerate(chips):
            for a in range(n):
                copy(a, 4 + j, (*chip, 1 - c), me).wait_recv()
        for a in range(n):
            for cp in first_hop(a) + passed_on(a):
                cp.wait_send()
            own(a).wait()

    return start, forward, finish


def _scatter_exchange(parts, landed, layer):
    n = len(parts)

    def body(*refs):
        start, finish = _scatter_phases(refs[:n], refs[2 * n:3 * n], *refs[3 * n:], layer)
        start()
        finish()

    return pl.pallas_call(body, name="scatter_exchange", out_shape=_same_shapes(landed), in_specs=_any_specs(2 * n),
                          out_specs=tuple(_any_specs(n)), scratch_shapes=_scatter_semaphores(n),
                          input_output_aliases={n + a: a for a in range(n)})(*parts, *landed)


def _same_shapes(arrays):
    return tuple(jax.ShapeDtypeStruct(a.shape, a.dtype) for a in arrays)


def _scatter_semaphores(n):
    return [pltpu.SemaphoreType.DMA((n, N_DEV - 1)), pltpu.SemaphoreType.DMA((n, N_DEV - 1)), pltpu.SemaphoreType.DMA((n,))]


def _scatter_phases(part_refs, landed_refs, send_sems, recv_sems, local_sems, layer):
    n = len(part_refs)
    x, y, c = _my_place()
    flips = [(0, 0, 1), (1, 0, 0), (0, 1, 0), (1, 1, 0), (1, 0, 1), (0, 1, 1), (1, 1, 1)]
    peers = [((1 - x) if fx else x, (1 - y) if fy else y, (1 - c) if fc else c) for fx, fy, fc in flips]
    my_k = 4 * x + 2 * y + c

    def index(peer):
        return 4 * peer[0] + 2 * peer[1] + peer[2]

    def send(a, r):
        return pltpu.make_async_remote_copy(src_ref=part_refs[a].at[index(peers[r])], dst_ref=landed_refs[a].at[my_k, layer],
                                            send_sem=send_sems.at[a, r], recv_sem=recv_sems.at[a, r],
                                            device_id=peers[r], device_id_type=MESH)

    def arrival(a, r):
        return pltpu.make_async_remote_copy(src_ref=part_refs[a].at[my_k], dst_ref=landed_refs[a].at[index(peers[r]), layer],
                                            send_sem=send_sems.at[a, r], recv_sem=recv_sems.at[a, r],
                                            device_id=peers[r], device_id_type=MESH)

    def own(a):
        return pltpu.make_async_copy(part_refs[a].at[my_k], landed_refs[a].at[my_k, layer], local_sems.at[a])

    def start():
        for a in range(n):
            own(a).start()
        for r in range(len(peers)):
            for a in range(n):
                send(a, r).start()

    def finish():
        for r in range(len(peers)):
            for a in range(n):
                arrival(a, r).wait_recv()
        for r in range(len(peers)):
            for a in range(n):
                send(a, r).wait_send()
        for a in range(n):
            own(a).wait()

    return start, finish


def _row_block(rows):
    return ROW_BLOCK if rows % ROW_BLOCK == 0 else rows


def _sum_leading(parts):
    n_part, shape = parts.shape[0], parts.shape[1:]
    cols = shape[-1]
    p2 = parts.reshape(n_part, -1, cols)
    rows = p2.shape[1]
    rb = _row_block(rows)

    def body(p_ref, o_ref):
        acc = p_ref[0].astype(F32)
        for k in range(1, n_part):
            acc = acc + p_ref[k].astype(F32)
        o_ref[...] = acc

    out = pl.pallas_call(body, name="sum_leading", grid=(rows // rb,),
                         in_specs=[pl.BlockSpec((n_part, rb, cols), lambda i: (0, i, 0))], out_specs=_rows(rb, cols),
                         out_shape=jax.ShapeDtypeStruct((rows, cols), F32), compiler_params=_cparams(("parallel",)))(p2)
    return out.reshape(shape)


def _adamw(w, g, m, v):
    shape = w.shape
    two_d = (shape[0] * shape[1], shape[2]) if len(shape) == 3 else shape
    rows, cols = two_d
    rb = _row_block(rows)

    def body(w_ref, g_ref, m_ref, v_ref, d_ref, nm_ref, nv_ref):
        gv = g_ref[...]
        nm = ADAM_B1 * m_ref[...] + (1.0 - ADAM_B1) * gv
        nv = ADAM_B2 * v_ref[...] + (1.0 - ADAM_B2) * jnp.square(gv)
        m_hat = nm / (1.0 - ADAM_B1 ** ADAM_STEP)
        v_hat = nv / (1.0 - ADAM_B2 ** ADAM_STEP)
        d_ref[...] = -ADAM_LR * (m_hat / (jnp.sqrt(v_hat) + ADAM_EPS) + ADAM_WD * w_ref[...])
        nm_ref[...] = nm
        nv_ref[...] = nv

    spec = _rows(rb, cols)
    out = jax.ShapeDtypeStruct(two_d, F32)
    res = pl.pallas_call(body, name="adamw", grid=(rows // rb,), in_specs=[spec] * 4, out_specs=(spec,) * 3,
                         out_shape=(out,) * 3, compiler_params=_cparams(("parallel",)))(
                             *(a.reshape(two_d) for a in (w, g, m, v)))
    return tuple(a.reshape(shape) for a in res)


def _rope_tables(positions):
    t_len = positions.shape[0]
    inv_freq = 1.0 / (ROPE_THETA ** (jnp.arange(0, QK_ROPE, 2, dtype=F32) / QK_ROPE))
    ang = positions.astype(F32)[:, None] * inv_freq
    c, s = jnp.cos(ang), jnp.sin(ang)
    one, zero = jnp.ones((t_len, QK_NOPE), F32), jnp.zeros((t_len, QK_NOPE), F32)
    cos = jnp.concatenate([one, c, c, one[:, :LANES - QK_HEAD]], axis=1)
    sin = jnp.concatenate([zero, -s, s, zero[:, :LANES - QK_HEAD]], axis=1)
    idx = jnp.arange(LANES)
    grp = jnp.where(idx < QK_NOPE, 0, jnp.where(idx < QK_HEAD, 1, 2))
    val = jnp.where(grp == 0, 1.0 / QK_NOPE, jnp.where(grp == 1, 1.0 / QK_ROPE, 0.0))
    gm = jnp.where(grp[:, None] == grp[None, :], val[None, :], 0.0).astype(BF16)
    return {"cos": cos, "sin": sin, "gm": gm}


def _head_gain(g_nope, g_rope):
    return jnp.concatenate([g_nope, g_rope, jnp.zeros((LANES - QK_HEAD,), F32)]).reshape(1, LANES)


def _padded_w_in(shards):
    natural = jnp.concatenate([shards[k] for k in range(N_DEV)], axis=1)
    o2, o3 = Q_LORA + KV_LORA, Q_LORA + KV_LORA + QK_ROPE
    zeros = jnp.zeros((natural.shape[0], QK_NOPE), natural.dtype)
    return jnp.concatenate([natural[:, :o2], natural[:, o3:], zeros, natural[:, o2:o3], zeros[:, :LANES - QK_HEAD]], axis=1)


def _w_in_grad_shards(d_in):
    o2 = Q_LORA + KV_LORA
    nat = jnp.concatenate([d_in[:, :o2], d_in[:, Z_KPE[0] + QK_NOPE:Z_KPE[0] + QK_HEAD], d_in[:, o2:Z_XIN[1]]], axis=1)
    width = nat.shape[1] // N_DEV
    return jnp.stack([nat[:, k * width:(k + 1) * width] for k in range(N_DEV)])


def kernel(x, p, positions, g_mix, w_in, g_q_lat, w_uq, g_kv_lat, w_ukv, g_qn_nope, g_qn_rope, g_kn_nope, g_kn_rope, conv_w, g_out_attn, g_out_conv, w_o, g_mlp, w_up, w_down, g_ple, w_ple_gate, w_ple, loss_target, m_g_mix, m_w_in, m_g_q_lat, m_w_uq, m_g_kv_lat, m_w_ukv, m_g_qn_nope, m_g_qn_rope, m_g_kn_nope, m_g_kn_rope, m_conv_w, m_g_out_attn, m_g_out_conv, m_w_o, m_g_mlp, m_w_up, m_w_down, m_g_ple, m_w_ple_gate, m_w_ple, v_g_mix, v_w_in, v_g_q_lat, v_w_uq, v_g_kv_lat, v_w_ukv, v_g_qn_nope, v_g_qn_rope, v_g_kn_nope, v_g_kn_rope, v_conv_w, v_g_out_attn, v_g_out_conv, v_w_o, v_g_mlp, v_w_up, v_w_down, v_g_ple, v_w_ple_gate, v_w_ple):
    given = dict(locals())
    weights = {n: given[n] for n in WEIGHT_NAMES}
    gains = {n: given[n] for n in GAIN_NAMES}
    depth = w_in.shape[0]
    xs, target = x[0], loss_target[0]
    d_model = xs.shape[1]
    uq_cols = w_uq.shape[2]
    n_taps = conv_w.shape[1]

    mat_names = [n for n in SHARD_NAMES if n != "conv_w"]
    local = [weights[n].astype(BF16) for n in mat_names]
    local[1] = jnp.pad(local[1], ((0, 0), (0, 0), (0, HEAD_PAD - uq_cols)))
    first = _all_gather([s[0] for s in local] + [conv_w])
    conv_full = jnp.transpose(first[-1], (1, 2, 0, 3)).reshape(depth, n_taps, -1)
    tabs = _rope_tables(positions[0])

    def layer_weights(layer, full):
        lw = {n: gains[n][layer].reshape(1, -1) for n in GAIN_NAMES}
        lw.update({"w_in": _padded_w_in(full["w_in"]), "w_uq": full["w_uq"], "w_ukv": full["w_ukv"],
                   "w_ple": full["w_ple"], "w_up": full["w_up"], "w_down": full["w_down"],
                   "w_o": full["w_o"].reshape(d_model, d_model),
                   "w_ple_gate": full["w_ple_gate"].reshape(d_model, d_model),
                   "conv_w": jnp.pad(conv_full[layer], ((0, HALO - n_taps), (0, 0))),
                   "g_qn": _head_gain(g_qn_nope[layer], g_qn_rope[layer]),
                   "g_kn": _head_gain(g_kn_nope[layer], g_kn_rope[layer])})
        return lw

    saved, layer_w = [], []
    cur = xs
    gathered = first[:-1]
    for layer in range(depth):
        w = layer_weights(layer, dict(zip(mat_names, gathered)))
        layer_w.append(w)
        z, qf, kf, kv, conv = _front_fwd(cur, w, tabs)
        if layer + 1 < depth:
            attn, lse, gathered = _attn_fwd(qf, kf, kv, gather=(local, layer + 1))
        else:
            attn, lse = _attn_fwd(qf, kf, kv)
        x1 = _mix_out_fwd(cur, attn, conv, w)
        x2 = _mlp_fwd(x1, w)
        x3 = _ple_fwd(x2, p[layer, 0], w)
        saved.append(dict(x=cur, z=z, qf=qf, kf=kf, kv=kv, conv=conv, attn=attn, lse=lse, x1=x1, x2=x2))
        cur = x3

    sq, dx = _loss_and_grad(cur, target)
    loss = lax.psum(0.5 / d_model * sq[0, 0], ("x", "y", "c"))

    landed = [lax.empty((N_DEV, depth) + weights[n].shape[1:], BF16) for n in SHARD_NAMES]
    gain_grads = [None] * depth
    pending = None
    for layer in reversed(range(depth)):
        w, s = layer_w[layer], saved[layer]
        pl_in = p[layer, 0]
        dx2, de, h3, dpre, dg_ple = _ple_bwd(dx, s["x2"], pl_in, w)
        dx1, r, da, h2, dg_mlp = _mlp_bwd(dx2, s["x1"], w)
        mixed, dattn, dconv, dg_oa, dg_oc = _mix_out_bwd(dx1, s["attn"], s["conv"], w)
        if pending is None:
            dqf, dkf, dkv = _attn_bwd(s["qf"], s["kf"], s["kv"], s["attn"], dattn, s["lse"])
        else:
            dqf, dkf, dkv, landed = _attn_bwd(s["qf"], s["kf"], s["kv"], s["attn"], dattn, s["lse"],
                                              scatter=(pending, landed, layer + 1))
        (dx0, dz, hb, qn, kvn, dqr, dkvr, dg_mix, dg_q, dg_kv, dg_qn, dg_kn, dcw) = _front_bwd(
            s["x"], s["z"], dx1, dqf, dkf, dkv, dconv, w, tabs)
        parts = {"w_in": _w_in_grad_shards(_wgrad(hb, dz, out_dtype=F32)).astype(BF16),
                 "w_uq": _wgrad(qn, dqr, shard_cols=HEAD_PAD)[..., :uq_cols],
                 "w_ukv": _wgrad(kvn, dkvr, shard_cols=HEAD_PAD),
                 "conv_w": jnp.transpose(dcw[:n_taps].reshape(n_taps, N_DEV, -1), (1, 0, 2)).astype(BF16),
                 "w_o": _wgrad(mixed, dx1).reshape((N_DEV,) + w_o.shape[1:]),
                 "w_up": _wgrad(h2, da, shard_cols=w_up.shape[2]),
                 "w_down": _wgrad(r, dx2).reshape((N_DEV,) + w_down.shape[1:]),
                 "w_ple_gate": _wgrad(h3, dpre).reshape((N_DEV,) + w_ple_gate.shape[1:]),
                 "w_ple": _wgrad(pl_in, de, shard_cols=w_ple.shape[2])}
        pending = [parts[n] for n in SHARD_NAMES]
        gain_grads[layer] = jnp.concatenate([
            dg_mix[0], dg_q[0], dg_kv[0], dg_qn[0, :QK_NOPE], dg_qn[0, QK_NOPE:QK_HEAD], dg_kn[0, :QK_NOPE],
            dg_kn[0, QK_NOPE:QK_HEAD], dg_oa[0], dg_oc[0], dg_mlp[0], dg_ple[0]])
        dx = dx0
    landed = _scatter_exchange(pending, landed, 0)
    grads = {n: _sum_leading(got) for n, got in zip(SHARD_NAMES, landed)}

    gg = jnp.stack(gain_grads)
    gg_rows = -(-gg.size // (HALO * LANES)) * HALO
    gg_pad = jnp.pad(gg.reshape(-1), (0, gg_rows * LANES - gg.size)).reshape(gg_rows, LANES)
    gg_sum = _sum_leading(_all_gather([gg_pad])[0]).reshape(-1)[:gg.size].reshape(gg.shape)
    off = 0
    for n in GAIN_NAMES:
        width = gains[n].shape[1]
        grads[n] = gg_sum[:, off:off + width]
        off += width

    deltas, new_m, new_v = {}, {}, {}
    for n in WEIGHT_NAMES:
        deltas[n], new_m[n], new_v[n] = _adamw(weights[n], grads[n], given["m_" + n], given["v_" + n])
    return (loss, dx[None], *[grads[n] for n in WEIGHT_NAMES], *[deltas[n] for n in WEIGHT_NAMES],
            *[new_m[n] for n in WEIGHT_NAMES], *[new_v[n] for n in WEIGHT_NAMES])
```

```python
import jax
import jax.numpy as jnp
from jax import lax
from jax.experimental import pallas as pl
from jax.experimental.pallas import tpu as pltpu

F32 = jnp.float32
BF16 = jnp.bfloat16
MESH = pl.DeviceIdType.MESH

N_HEADS = 8
QK_NOPE = 64
QK_ROPE = 32
QK_HEAD = QK_NOPE + QK_ROPE
V_HEAD = 64
HEAD_PAD = 128
PAIR = 2
ATTN_SCALE = QK_HEAD ** -0.5
Q_LORA = 384
KV_LORA = 256
CONV_WIDTH = 512
ATTN_WIDTH = N_HEADS * V_HEAD
ROPE_THETA = 10000.0
EPS = 1e-6
ADAM_LR, ADAM_B1, ADAM_B2, ADAM_EPS, ADAM_WD, ADAM_STEP = 0.001, 0.9, 0.999, 1e-08, 0.01, 10

Z_Q = (0, 384)
Z_KV = (384, 640)
Z_GB = (640, 1152)
Z_GC = (1152, 1664)
Z_XIN = (1664, 2176)
Z_KPE = (2176, 2304)
Z_COLS = 2304

N_DEV = 8
LANES = 128
V7X_VMEM_LIMIT = 52 * 1024 * 1024
TOKEN_TILE = 256
ATTN_BLOCK = 256
ATTN_Q_SUB = 2
ATTN_KV_SUB = 2
ROW_BLOCK = 512
WGRAD_TOKENS = 2048
WGRAD_TILE = 1024
HALO = 8

GAIN_NAMES = ("g_mix", "g_q_lat", "g_kv_lat", "g_qn_nope", "g_qn_rope", "g_kn_nope", "g_kn_rope",
              "g_out_attn", "g_out_conv", "g_mlp", "g_ple")
SHARD_NAMES = ("w_in", "w_uq", "w_ukv", "conv_w", "w_o", "w_up", "w_down", "w_ple_gate", "w_ple")
WEIGHT_NAMES = ("g_mix", "w_in", "g_q_lat", "w_uq", "g_kv_lat", "w_ukv", "g_qn_nope", "g_qn_rope", "g_kn_nope",
                "g_kn_rope", "conv_w", "g_out_attn", "g_out_conv", "w_o", "g_mlp", "w_up", "w_down", "g_ple",
                "w_ple_gate", "w_ple")


def _cparams(semantics=None):
    return pltpu.CompilerParams(dimension_semantics=semantics, vmem_limit_bytes=V7X_VMEM_LIMIT)


def _mm(a, b):
    return jnp.dot(a, b, preferred_element_type=F32)


def _mm_nt(a, b):
    return lax.dot_general(a, b, (((1,), (1,)), ((), ())), preferred_element_type=F32)


def _mm_tn(a, b):
    return lax.dot_general(a, b, (((0,), (0,)), ((), ())), preferred_element_type=F32)


def _rms(x, g):
    r = lax.rsqrt(jnp.mean(x * x, axis=-1, keepdims=True) + EPS)
    return (x * r) * g


def _rms_bwd(x, g, dy):
    r = lax.rsqrt(jnp.mean(x * x, axis=-1, keepdims=True) + EPS)
    xh = x * r
    dg = jnp.sum(dy * xh, axis=0, keepdims=True)
    dyg = dy * g
    dx = r * (dyg - xh * jnp.mean(dyg * xh, axis=-1, keepdims=True))
    return dx, dg


def _group_mean(t, gm):
    hi = t.astype(BF16)
    lo = (t - hi.astype(F32)).astype(BF16)
    return _mm(hi, gm) + _mm(lo, gm)


def _swap_rope_halves(x, lane):
    half = QK_ROPE // 2
    swapped = jnp.where(lane < QK_NOPE + half, pltpu.roll(x, x.shape[1] - half, 1), pltpu.roll(x, half, 1))
    return jnp.where((lane >= QK_NOPE) & (lane < QK_HEAD), swapped, 0.0)


def _qk_fwd(x, g, cos, sin, gm, lane):
    r = lax.rsqrt(_group_mean(x * x, gm) + EPS)
    n = (x * r) * g
    return n * cos + _swap_rope_halves(n, lane) * sin


def _qk_bwd(x, g, dy, cos, sin, gm, lane):
    r = lax.rsqrt(_group_mean(x * x, gm) + EPS)
    xh = x * r
    dn = dy * cos + _swap_rope_halves(dy * sin, lane)
    dg = jnp.sum(dn * xh, axis=0, keepdims=True)
    dng = dn * g
    dx = r * (dng - xh * _group_mean(dng * xh, gm))
    return dx, dg


def _rows(tm, n):
    return pl.BlockSpec((tm, n), lambda i: (i, 0))


def _whole(shape):
    zeros = (0,) * len(shape)
    return pl.BlockSpec(shape, lambda i: zeros)


def _operands(arrays):
    return list(arrays), [_whole(a.shape) for a in arrays]


def _accumulate(ref, first, value):
    @pl.when(first)
    def _():
        ref[...] = jnp.zeros_like(ref)
    ref[...] += value


def _front_fwd(x, w, tabs):
    t_len, d = x.shape
    tm = min(TOKEN_TILE, t_len)
    hp = N_HEADS * HEAD_PAD

    def body(x_ref, gmix, win, gq, wuq, gkv, wukv, gqn, gkn, cw_ref, gm_ref, cos_ref, sin_ref,
             z_ref, qf_ref, kf_ref, kv_ref, conv_ref, ubuf):
        i = pl.program_id(0)
        h = _rms(x_ref[...], gmix[...])
        z = _mm(h.astype(BF16), win[...])
        z_ref[...] = z
        qnb = _rms(z[:, Z_Q[0]:Z_Q[1]], gq[...]).astype(BF16)
        kvb = _rms(z[:, Z_KV[0]:Z_KV[1]], gkv[...]).astype(BF16)
        kpe = z[:, Z_KPE[0]:Z_KPE[1]]
        kpe = jnp.concatenate([kpe] * PAIR, axis=1)
        cos, sin, gm = cos_ref[...], sin_ref[...], gm_ref[...]
        lane = lax.broadcasted_iota(jnp.int32, (tm, PAIR * HEAD_PAD), 1) & (HEAD_PAD - 1)
        for pr in range(N_HEADS // PAIR):
            sl = slice(pr * PAIR * HEAD_PAD, (pr + 1) * PAIR * HEAD_PAD)
            qf_ref[:, sl] = (_qk_fwd(_mm(qnb, wuq[pr]), gqn[...], cos, sin, gm, lane) * ATTN_SCALE).astype(BF16)
            kv = _mm(kvb, wukv[pr])
            kv_ref[:, sl] = kv.astype(BF16)
            kf_ref[:, sl] = _qk_fwd(jnp.where(lane < QK_NOPE, kv, 0.0) + kpe, gkn[...], cos, sin, gm, lane).astype(BF16)
        u = z[:, Z_GC[0]:Z_GC[1]] * z[:, Z_XIN[0]:Z_XIN[1]]

        @pl.when(i == 0)
        def _():
            ubuf[0:HALO, :] = jnp.zeros((HALO, CONV_WIDTH), F32)
        ubuf[HALO:HALO + tm, :] = u
        cw = cw_ref[...]
        y = cw[0:1] * u + cw[1:2] * ubuf[pl.ds(HALO - 1, tm), :] + cw[2:3] * ubuf[pl.ds(HALO - 2, tm), :]
        conv_ref[...] = z[:, Z_GB[0]:Z_GB[1]] * y
        ubuf[0:HALO, :] = u[tm - HALO:tm, :]

    consts, const_specs = _operands([w["g_mix"], w["w_in"], w["g_q_lat"], w["w_uq"], w["g_kv_lat"], w["w_ukv"],
                                     w["g_qn"], w["g_kn"], w["conv_w"], tabs["gm"]])
    out_shape = (jax.ShapeDtypeStruct((t_len, Z_COLS), F32), jax.ShapeDtypeStruct((t_len, hp), BF16),
                 jax.ShapeDtypeStruct((t_len, hp), BF16), jax.ShapeDtypeStruct((t_len, hp), BF16),
                 jax.ShapeDtypeStruct((t_len, CONV_WIDTH), F32))
    return pl.pallas_call(body, name="front_fwd", grid=(t_len // tm,),
                          in_specs=[_rows(tm, d)] + const_specs + [_rows(tm, PAIR * HEAD_PAD)] * 2,
                          out_specs=tuple(_rows(tm, s.shape[1]) for s in out_shape), out_shape=out_shape,
                          scratch_shapes=[pltpu.VMEM((tm + HALO, CONV_WIDTH), F32)],
                          compiler_params=_cparams(("arbitrary",)))(x, *consts, tabs["cos"], tabs["sin"])


def _attn_fwd(qf, kf, kv, gather=None):
    t_len = qf.shape[0]
    blk = min(ATTN_BLOCK, t_len)
    nb = t_len // blk
    n_sub = ATTN_Q_SUB
    bq = blk // n_sub
    chains = [(hh, a) for hh in range(2) for a in range(n_sub)]

    def body(q_ref, k_ref, kv_ref, o_ref, lse_ref):
        lane = lax.broadcasted_iota(jnp.int32, (bq, LANES), 1)
        row = lax.broadcasted_iota(jnp.int32, (bq, blk), 0)
        col = lax.broadcasted_iota(jnp.int32, (bq, blk), 1)

        def head_cols(hh):
            return slice(hh * HEAD_PAD, (hh + 1) * HEAD_PAD)

        def softmax_step(s, kvv, state, first_row=None):
            m, l, acc = state
            if first_row is not None:
                s = jnp.where(col <= row + first_row, s, -jnp.inf)
            m_new = jnp.maximum(m, jnp.max(s, axis=-1, keepdims=True))
            p = jnp.exp(s - m_new)
            alpha = jnp.exp(m - m_new)
            l = alpha * l + jnp.sum(p, axis=-1, keepdims=True)
            acc = alpha * acc + _mm(p.astype(BF16), kvv)
            return m_new, l, acc

        def qblock(i, carry):
            start = pl.multiple_of(i * blk, blk)
            rows = [pl.ds(pl.multiple_of(start + a * bq, bq), bq) for a in range(n_sub)]
            qs = {(hh, a): q_ref[rows[a], head_cols(hh)] for hh, a in chains}

            def scores(j):
                ks = pl.ds(pl.multiple_of(j * blk, blk), blk)
                return tuple(_mm_nt(qs[hh, a], k_ref[ks, head_cols(hh)]) for hh, a in chains)

            def kstep(j, carried, diagonal=False):
                ss, states = carried
                ss_next = ss if diagonal else scores(j + 1)
                ks = pl.ds(pl.multiple_of(j * blk, blk), blk)
                new = tuple(softmax_step(s, kv_ref[ks, head_cols(hh)], st, a * bq if diagonal else None)
                            for (hh, a), s, st in zip(chains, ss, states))
                return ss_next, new

            init = (jnp.full((bq, 1), -jnp.inf, F32), jnp.zeros((bq, 1), F32), jnp.zeros((bq, LANES), F32))
            carried = lax.fori_loop(0, i, kstep, (scores(0), (init,) * len(chains)))
            _, states = kstep(i, carried, diagonal=True)
            for a in range(n_sub):
                (m0, l0, acc0), (m1, l1, acc1) = states[chains.index((0, a))], states[chains.index((1, a))]
                o_ref[rows[a], :] = jnp.where(lane < V_HEAD, pltpu.roll(acc0 / l0, V_HEAD, 1), acc1 / l1)
                lse_ref[rows[a], head_cols(0)] = jnp.broadcast_to(m0 + jnp.log(l0), (bq, LANES))
                lse_ref[rows[a], head_cols(1)] = jnp.broadcast_to(m1 + jnp.log(l1), (bq, LANES))
            return carry

        lax.fori_loop(0, nb, qblock, 0)

    n_steps = N_HEADS // 2
    n_gather = len(gather) if gather else 0

    def hosting_body(*refs):
        q_ref, k_ref, kv_ref = refs[:3]
        x_refs = [r.at[layer] for r, (_, layer) in zip(refs[3:3 + n_gather], gather)]
        o_ref, lse_ref = refs[3 + n_gather:5 + n_gather]
        start, forward, finish = _gather_phases(x_refs, refs[5 + n_gather:5 + 2 * n_gather], *refs[5 + 2 * n_gather:])
        step = pl.program_id(0)
        pl.when(step == 0)(start)
        pl.when(step == n_steps // 2)(forward)
        body(q_ref, k_ref, kv_ref, o_ref, lse_ref)
        pl.when(step == n_steps - 1)(finish)

    heads = pl.BlockSpec((t_len, 2 * HEAD_PAD), lambda h: (0, h))
    pair = pl.BlockSpec((t_len, 2 * V_HEAD), lambda h: (0, h))
    out_shape = (jax.ShapeDtypeStruct((t_len, ATTN_WIDTH), F32), jax.ShapeDtypeStruct((t_len, N_HEADS * LANES), F32))
    if not gather:
        return pl.pallas_call(body, name="attn_fwd", grid=(n_steps,), in_specs=[heads, heads, heads],
                              out_specs=(pair, heads), out_shape=out_shape,
                              compiler_params=_cparams(("arbitrary",)))(qf, kf, kv)
    shards = [s[layer] for s, layer in gather]
    res = pl.pallas_call(hosting_body, name="attn_fwd_gather", grid=(n_steps,),
                         in_specs=[heads, heads, heads] + _any_specs(n_gather),
                         out_specs=(pair, heads) + tuple(_any_specs(n_gather)),
                         out_shape=out_shape + _gather_out_shape(shards), scratch_shapes=_gather_semaphores(n_gather),
                         compiler_params=_cparams(("arbitrary",)))(qf, kf, kv, *[s for s, _ in gather])
    return res[0], res[1], res[2:]


def _mix_out_fwd(x, attn, conv, w):
    t_len, d = x.shape
    tm = min(TOKEN_TILE, t_len)

    def body(x_ref, a_ref, c_ref, goa, goc, wo, x1_ref):
        mixed = jnp.concatenate([_rms(a_ref[...], goa[...]), _rms(c_ref[...], goc[...])], axis=1)
        x1_ref[...] = x_ref[...] + _mm(mixed.astype(BF16), wo[...])

    consts, const_specs = _operands([w["g_out_attn"], w["g_out_conv"], w["w_o"]])
    return pl.pallas_call(body, name="mix_out_fwd", grid=(t_len // tm,),
                          in_specs=[_rows(tm, d), _rows(tm, ATTN_WIDTH), _rows(tm, CONV_WIDTH)] + const_specs,
                          out_specs=_rows(tm, d), out_shape=jax.ShapeDtypeStruct((t_len, d), F32),
                          compiler_params=_cparams(("parallel",)))(x, attn, conv, *consts)


def _mlp_fwd(x1, w):
    t_len, d = x1.shape
    tm = min(TOKEN_TILE, t_len)

    def body(x_ref, g, wup, wdn, x2_ref):
        x1v = x_ref[...]
        hb = _rms(x1v, g[...]).astype(BF16)
        acc = x1v
        for k in range(N_DEV):
            a = jnp.maximum(_mm(hb, wup[k]), 0.0)
            acc = acc + _mm((a * a).astype(BF16), wdn[k])
        x2_ref[...] = acc

    consts, const_specs = _operands([w["g_mlp"], w["w_up"], w["w_down"]])
    return pl.pallas_call(body, name="mlp_fwd", grid=(t_len // tm,), in_specs=[_rows(tm, d)] + const_specs,
                          out_specs=_rows(tm, d), out_shape=jax.ShapeDtypeStruct((t_len, d), F32),
                          compiler_params=_cparams(("parallel",)))(x1, *consts)


def _ple_fwd(x2, p, w):
    t_len, d = x2.shape
    tm = min(TOKEN_TILE, t_len)

    def body(x_ref, p_ref, g, wg, wp, x3_ref):
        x2v = x_ref[...]
        gate = jax.nn.sigmoid(_mm(_rms(x2v, g[...]).astype(BF16), wg[...]))
        pb = p_ref[...].astype(BF16)
        e = jnp.concatenate([_mm(pb, wp[k]) for k in range(N_DEV)], axis=1)
        x3_ref[...] = x2v + gate * e

    consts, const_specs = _operands([w["g_ple"], w["w_ple_gate"], w["w_ple"]])
    return pl.pallas_call(body, name="ple_fwd", grid=(t_len // tm,),
                          in_specs=[_rows(tm, d), _rows(tm, p.shape[1])] + const_specs, out_specs=_rows(tm, d),
                          out_shape=jax.ShapeDtypeStruct((t_len, d), F32),
                          compiler_params=_cparams(("parallel",)))(x2, p, *consts)


def _loss_and_grad(y, target):
    t_len, d = y.shape
    tm = min(TOKEN_TILE, t_len)

    def body(y_ref, t_ref, sq_ref, dy_ref):
        err = y_ref[...] - t_ref[...]
        dy_ref[...] = err / d
        total = jnp.sum(jnp.sum(err * err, axis=0, keepdims=True), axis=1, keepdims=True)
        _accumulate(sq_ref, pl.program_id(0) == 0, jnp.broadcast_to(total, (HALO, LANES)))

    return pl.pallas_call(body, name="loss_grad", grid=(t_len // tm,), in_specs=[_rows(tm, d), _rows(tm, d)],
                          out_specs=(_whole((HALO, LANES)), _rows(tm, d)),
                          out_shape=(jax.ShapeDtypeStruct((HALO, LANES), F32), jax.ShapeDtypeStruct((t_len, d), F32)),
                          compiler_params=_cparams(("arbitrary",)))(y, target)


def _ple_bwd(dx3, x2, p, w):
    t_len, d = x2.shape
    tm = min(TOKEN_TILE, t_len)

    def body(dx3_ref, x_ref, p_ref, g, wg, wp, dx2_ref, de_ref, h3_ref, dpre_ref, dg_ref):
        x2v, dx3v = x_ref[...], dx3_ref[...]
        hb = _rms(x2v, g[...]).astype(BF16)
        h3_ref[...] = hb
        gate = jax.nn.sigmoid(_mm(hb, wg[...]))
        pb = p_ref[...].astype(BF16)
        e = jnp.concatenate([_mm(pb, wp[k]) for k in range(N_DEV)], axis=1)
        de_ref[...] = (dx3v * gate).astype(BF16)
        dpre = ((dx3v * e) * gate * (1.0 - gate)).astype(BF16)
        dpre_ref[...] = dpre
        dx, dg = _rms_bwd(x2v, g[...], _mm_nt(dpre, wg[...]))
        dx2_ref[...] = dx3v + dx
        _accumulate(dg_ref, pl.program_id(0) == 0, dg)

    consts, const_specs = _operands([w["g_ple"], w["w_ple_gate"], w["w_ple"]])
    out_shape = (jax.ShapeDtypeStruct((t_len, d), F32), jax.ShapeDtypeStruct((t_len, d), BF16),
                 jax.ShapeDtypeStruct((t_len, d), BF16), jax.ShapeDtypeStruct((t_len, d), BF16),
                 jax.ShapeDtypeStruct((1, d), F32))
    return pl.pallas_call(body, name="ple_bwd", grid=(t_len // tm,),
                          in_specs=[_rows(tm, d), _rows(tm, d), _rows(tm, p.shape[1])] + const_specs,
                          out_specs=(_rows(tm, d),) * 4 + (_whole((1, d)),), out_shape=out_shape,
                          compiler_params=_cparams(("arbitrary",)))(dx3, x2, p, *consts)


def _mlp_bwd(dx2, x1, w):
    t_len, d = x1.shape
    tm = min(TOKEN_TILE, t_len)
    fc = w["w_up"].shape[2]
    ff = N_DEV * fc

    def body(dx2_ref, x_ref, g, wup, wdn, dx1_ref, r_ref, da_ref, h2_ref, dg_ref):
        x1v, dx2v = x_ref[...], dx2_ref[...]
        hb = _rms(x1v, g[...]).astype(BF16)
        h2_ref[...] = hb
        dxb = dx2v.astype(BF16)
        dh = jnp.zeros((tm, d), F32)
        for k in range(N_DEV):
            a = jnp.maximum(_mm(hb, wup[k]), 0.0)
            r_ref[:, k * fc:(k + 1) * fc] = (a * a).astype(BF16)
            da = (_mm_nt(dxb, wdn[k]) * (2.0 * a)).astype(BF16)
            da_ref[:, k * fc:(k + 1) * fc] = da
            dh = dh + _mm_nt(da, wup[k])
        dx, dg = _rms_bwd(x1v, g[...], dh)
        dx1_ref[...] = dx2v + dx
        _accumulate(dg_ref, pl.program_id(0) == 0, dg)

    consts, const_specs = _operands([w["g_mlp"], w["w_up"], w["w_down"]])
    out_shape = (jax.ShapeDtypeStruct((t_len, d), F32), jax.ShapeDtypeStruct((t_len, ff), BF16),
                 jax.ShapeDtypeStruct((t_len, ff), BF16), jax.ShapeDtypeStruct((t_len, d), BF16),
                 jax.ShapeDtypeStruct((1, d), F32))
    return pl.pallas_call(body, name="mlp_bwd", grid=(t_len // tm,), in_specs=[_rows(tm, d), _rows(tm, d)] + const_specs,
                          out_specs=(_rows(tm, d), _rows(tm, ff), _rows(tm, ff), _rows(tm, d), _whole((1, d))),
                          out_shape=out_shape, compiler_params=_cparams(("arbitrary",)))(dx2, x1, *consts)


def _mix_out_bwd(dx1, attn, conv, w):
    t_len, d = dx1.shape
    tm = min(TOKEN_TILE, t_len)

    def body(dx1_ref, a_ref, c_ref, goa, goc, wo, mixed_ref, da_ref, dc_ref, dgoa_ref, dgoc_ref):
        av, cv = a_ref[...], c_ref[...]
        mixed_ref[...] = jnp.concatenate([_rms(av, goa[...]), _rms(cv, goc[...])], axis=1).astype(BF16)
        dmixed = _mm_nt(dx1_ref[...].astype(BF16), wo[...])
        da, dga = _rms_bwd(av, goa[...], dmixed[:, :ATTN_WIDTH])
        dc, dgc = _rms_bwd(cv, goc[...], dmixed[:, ATTN_WIDTH:])
        da_ref[...] = da
        dc_ref[...] = dc
        first = pl.program_id(0) == 0
        _accumulate(dgoa_ref, first, dga)
        _accumulate(dgoc_ref, first, dgc)

    consts, const_specs = _operands([w["g_out_attn"], w["g_out_conv"], w["w_o"]])
    out_shape = (jax.ShapeDtypeStruct((t_len, d), BF16), jax.ShapeDtypeStruct((t_len, ATTN_WIDTH), F32),
                 jax.ShapeDtypeStruct((t_len, CONV_WIDTH), F32), jax.ShapeDtypeStruct((1, ATTN_WIDTH), F32),
                 jax.ShapeDtypeStruct((1, CONV_WIDTH), F32))
    out_specs = (_rows(tm, d), _rows(tm, ATTN_WIDTH), _rows(tm, CONV_WIDTH), _whole((1, ATTN_WIDTH)),
                 _whole((1, CONV_WIDTH)))
    return pl.pallas_call(body, name="mix_out_bwd", grid=(t_len // tm,),
                          in_specs=[_rows(tm, d), _rows(tm, ATTN_WIDTH), _rows(tm, CONV_WIDTH)] + const_specs,
                          out_specs=out_specs, out_shape=out_shape,
                          compiler_params=_cparams(("arbitrary",)))(dx1, attn, conv, *consts)


def _attn_bwd(qf, kf, kv, o, do, lse, scatter=None):
    t_len = qf.shape[0]
    blk = min(ATTN_BLOCK, t_len)
    nb = t_len // blk
    n_sub = ATTN_KV_SUB if nb % ATTN_KV_SUB == 0 else 1
    reps = blk // LANES

    def body(q_ref, k_ref, kv_ref, o_ref, do_ref, lse_ref, dq_ref, dk_ref, dkv_ref, delta_ref, dob_ref):
        hd = pl.program_id(0)
        lane = lax.broadcasted_iota(jnp.int32, (blk, LANES), 1)
        even = (lane * 0 + hd % 2) == 0
        mine = jnp.where(lane < V_HEAD, 0, 1) == hd % 2
        row = lax.broadcasted_iota(jnp.int32, (blk, blk), 0)
        col = lax.broadcasted_iota(jnp.int32, (blk, blk), 1)
        dq_ref[...] = jnp.zeros_like(dq_ref)

        def prepare(i, carry):
            qs = pl.ds(pl.multiple_of(i * blk, blk), blk)
            dov = do_ref[qs, :]
            prod = jnp.where(mine, dov * o_ref[qs, :], 0.0)
            delta_ref[qs, :] = jnp.broadcast_to(jnp.sum(prod, axis=-1, keepdims=True), (blk, LANES))
            moved = jnp.where(even, pltpu.roll(dov, V_HEAD, 1), dov)
            dob_ref[qs, :] = jnp.where(lane >= V_HEAD, moved, 0.0).astype(BF16)
            return carry
        lax.fori_loop(0, nb, prepare, 0)

        def kvblock(jj, carry):
            base = jj * n_sub
            kss = [pl.ds(pl.multiple_of((base + a) * blk, blk), blk) for a in range(n_sub)]
            k = [k_ref[ks, :] for ks in kss]
            kvv = [kv_ref[ks, :] for ks in kss]

            def products(i):
                qs = pl.ds(pl.multiple_of(i * blk, blk), blk)
                q, dob = q_ref[qs, :], dob_ref[qs, :]
                return tuple((_mm_nt(q, k[a]), _mm_nt(dob, kvv[a])) for a in range(n_sub))

            def qstep(i, raw, accs, kinds):
                qs = pl.ds(pl.multiple_of(i * blk, blk), blk)
                q = q_ref[qs, :]
                dob = dob_ref[qs, :]
                lse_t = jnp.concatenate([lse_ref[qs, :]] * reps, axis=1)
                delta_t = jnp.concatenate([delta_ref[qs, :]] * reps, axis=1)
                new, dq_add = [], None
                for a in range(n_sub):
                    if kinds[a] is None:
                        new.append(accs[a])
                        continue
                    dk_acc, dv_acc = accs[a]
                    s, dp = raw[a]
                    if kinds[a]:
                        s = jnp.where(col <= row, s, -jnp.inf)
                    p = jnp.exp(s - lse_t)
                    ds = (p * (dp - delta_t)).astype(BF16)
                    new.append((dk_acc + _mm_tn(ds, q), dv_acc + _mm_tn(p.astype(BF16), dob)))
                    part = _mm(ds, k[a])
                    dq_add = part if dq_add is None else dq_add + part
                dq_ref[qs, :] += dq_add
                return tuple(new)

            zero = jnp.zeros((blk, LANES), F32)
            accs = ((zero, zero),) * n_sub
            for b in range(n_sub):
                accs = qstep(base + b, products(base + b), accs, tuple((a == b) if a <= b else None for a in range(n_sub)))

            def pipelined(i, carried):
                raw, acc = carried
                return products(jnp.minimum(i + 1, nb - 1)), qstep(i, raw, acc, (False,) * n_sub)

            first = base + n_sub
            _, accs = lax.fori_loop(first, nb, pipelined, (products(jnp.minimum(first, nb - 1)), accs))
            for a in range(n_sub):
                dk_ref[kss[a], :] = accs[a][0]
                dkv_ref[kss[a], :] = accs[a][1]
            return carry
        lax.fori_loop(0, nb // n_sub, kvblock, 0)

    n_sc = len(scatter[0]) if scatter else 0

    def hosting_body(*refs):
        ins, rest = refs[:6], refs[6 + 2 * n_sc:]
        parts = refs[6:6 + n_sc]
        outs, landed, sems, scratch = rest[:3], rest[3:3 + n_sc], rest[3 + n_sc:6 + n_sc], rest[6 + n_sc:]
        start, finish = _scatter_phases(parts, landed, *sems, scatter[2])
        hd = pl.program_id(0)
        pl.when(hd == 0)(start)
        body(*ins, *outs, *scratch)
        pl.when(hd == N_HEADS - 1)(finish)

    head = pl.BlockSpec((t_len, HEAD_PAD), lambda h: (0, h))
    pair = pl.BlockSpec((t_len, 2 * V_HEAD), lambda h: (0, h // 2))
    out = jax.ShapeDtypeStruct((t_len, N_HEADS * HEAD_PAD), F32)
    vmem_scratch = [pltpu.VMEM((t_len, LANES), F32), pltpu.VMEM((t_len, LANES), BF16)]
    if not scatter:
        return pl.pallas_call(body, name="attn_bwd", grid=(N_HEADS,), in_specs=[head, head, head, pair, pair, head],
                              out_specs=(head, head, head), out_shape=(out, out, out), scratch_shapes=vmem_scratch,
                              compiler_params=_cparams(("arbitrary",)))(qf, kf, kv, o, do, lse)
    res = pl.pallas_call(hosting_body, name="attn_bwd_scatter", grid=(N_HEADS,),
                         in_specs=[head, head, head, pair, pair, head] + _any_specs(2 * n_sc),
                         out_specs=(head, head, head) + tuple(_any_specs(n_sc)),
                         out_shape=(out, out, out) + _same_shapes(scatter[1]),
                         scratch_shapes=_scatter_semaphores(n_sc) + vmem_scratch,
                         input_output_aliases={6 + n_sc + a: 3 + a for a in range(n_sc)},
                         compiler_params=_cparams(("arbitrary",)))(qf, kf, kv, o, do, lse, *scatter[0], *scatter[1])
    return res[0], res[1], res[2], res[3:]


def _front_bwd(x, z, dx1, dqf, dkf, dkv_in, dconv, w, tabs):
    t_len, d = x.shape
    tm = min(TOKEN_TILE, t_len)
    nt = t_len // tm
    hb_per_tile = tm // HALO
    n_halo = t_len // HALO
    hp = N_HEADS * HEAD_PAD

    def body(x_ref, z_ref, zp_ref, zn_ref, dx1_ref, dqf_ref, dkf_ref, dkv_ref, dc_ref, dcn_ref,
             gmix, win, gq, wuq, gkv, wukv, gqn, gkn, cw_ref, gm_ref, cos_ref, sin_ref,
             dx_ref, dz_ref, h_ref, qn_ref, kvn_ref, dqr_ref, dkvr_ref,
             dgmix_ref, dgq_ref, dgkv_ref, dgqn_ref, dgkn_ref, dcw_ref, ubuf, dybuf):
        i = pl.program_id(0)
        first = i == 0
        xv, zv = x_ref[...], z_ref[...]
        hb = _rms(xv, gmix[...]).astype(BF16)
        h_ref[...] = hb
        zq, zkv = zv[:, Z_Q[0]:Z_Q[1]], zv[:, Z_KV[0]:Z_KV[1]]
        qnb = _rms(zq, gq[...]).astype(BF16)
        qn_ref[...] = qnb
        kvb = _rms(zkv, gkv[...]).astype(BF16)
        kvn_ref[...] = kvb
        kpe = zv[:, Z_KPE[0]:Z_KPE[1]]
        kpe = jnp.concatenate([kpe] * PAIR, axis=1)
        cos, sin, gm = cos_ref[...], sin_ref[...], gm_ref[...]
        width = PAIR * HEAD_PAD
        lane = lax.broadcasted_iota(jnp.int32, (tm, width), 1) & (HEAD_PAD - 1)
        is_nope = lane < QK_NOPE
        is_rope = (lane >= QK_NOPE) & (lane < QK_HEAD)
        dkpe = jnp.zeros((tm, width), F32)
        dgqn = jnp.zeros((1, width), F32)
        dgkn = jnp.zeros((1, width), F32)
        dqn = jnp.zeros((tm, Q_LORA), F32)
        dkvn = jnp.zeros((tm, KV_LORA), F32)
        for pr in range(N_HEADS // PAIR):
            sl = slice(pr * width, (pr + 1) * width)
            dxq, dg = _qk_bwd(_mm(qnb, wuq[pr]), gqn[...], dqf_ref[:, sl] * ATTN_SCALE, cos, sin, gm, lane)
            dxq = dxq.astype(BF16)
            dqr_ref[:, sl] = dxq
            dqn = dqn + _mm_nt(dxq, wuq[pr])
            dgqn = dgqn + dg
            k_raw = jnp.where(is_nope, _mm(kvb, wukv[pr]), 0.0) + kpe
            dxk, dg = _qk_bwd(k_raw, gkn[...], dkf_ref[:, sl], cos, sin, gm, lane)
            dkv = jnp.where(is_nope, dxk, dkv_ref[:, sl]).astype(BF16)
            dkvr_ref[:, sl] = dkv
            dkvn = dkvn + _mm_nt(dkv, wukv[pr])
            dkpe = dkpe + jnp.where(is_rope, dxk, 0.0)
            dgkn = dgkn + dg
        dkpe = dkpe[:, :HEAD_PAD] + dkpe[:, HEAD_PAD:]
        _accumulate(dgqn_ref, first, dgqn[:, :HEAD_PAD] + dgqn[:, HEAD_PAD:])
        _accumulate(dgkn_ref, first, dgkn[:, :HEAD_PAD] + dgkn[:, HEAD_PAD:])
        dzq, dg = _rms_bwd(zq, gq[...], dqn)
        _accumulate(dgq_ref, first, dg)
        dzkv, dg = _rms_bwd(zkv, gkv[...], dkvn)
        _accumulate(dgkv_ref, first, dg)

        gb, gc, xin = zv[:, Z_GB[0]:Z_GB[1]], zv[:, Z_GC[0]:Z_GC[1]], zv[:, Z_XIN[0]:Z_XIN[1]]
        u = gc * xin
        dcv = dc_ref[...]
        dy = dcv * gb
        zp, zn = zp_ref[...], zn_ref[...]
        ubuf[0:HALO, :] = (zp[:, Z_GC[0]:Z_GC[1]] * zp[:, Z_XIN[0]:Z_XIN[1]]) * jnp.where(first, 0.0, 1.0)
        ubuf[HALO:HALO + tm, :] = u
        dybuf[0:tm, :] = dy
        dybuf[tm:tm + HALO, :] = (dcn_ref[...] * zn[:, Z_GB[0]:Z_GB[1]]) * jnp.where(i == nt - 1, 0.0, 1.0)
        cw = cw_ref[...]
        u1, u2 = ubuf[pl.ds(HALO - 1, tm), :], ubuf[pl.ds(HALO - 2, tm), :]
        y = cw[0:1] * u + cw[1:2] * u1 + cw[2:3] * u2
        du = cw[0:1] * dy + cw[1:2] * dybuf[pl.ds(1, tm), :] + cw[2:3] * dybuf[pl.ds(2, tm), :]
        dcw = jnp.concatenate([jnp.sum(dy * u, axis=0, keepdims=True), jnp.sum(dy * u1, axis=0, keepdims=True),
                               jnp.sum(dy * u2, axis=0, keepdims=True), jnp.zeros((HALO - 3, CONV_WIDTH), F32)], axis=0)
        _accumulate(dcw_ref, first, dcw)

        dz_ref[:, Z_Q[0]:Z_Q[1]] = dzq.astype(BF16)
        dz_ref[:, Z_KV[0]:Z_KV[1]] = dzkv.astype(BF16)
        dz_ref[:, Z_GB[0]:Z_GB[1]] = (dcv * y).astype(BF16)
        dz_ref[:, Z_GC[0]:Z_GC[1]] = (du * xin).astype(BF16)
        dz_ref[:, Z_XIN[0]:Z_XIN[1]] = (du * gc).astype(BF16)
        dz_ref[:, Z_KPE[0]:Z_KPE[1]] = dkpe.astype(BF16)
        dx, dg = _rms_bwd(xv, gmix[...], _mm_nt(dz_ref[...], win[...]))
        dx_ref[...] = dx1_ref[...] + dx
        _accumulate(dgmix_ref, first, dg)

    prev_halo = lambda n: pl.BlockSpec((HALO, n), lambda i: (jnp.maximum(i * hb_per_tile - 1, 0), 0))
    next_halo = lambda n: pl.BlockSpec((HALO, n), lambda i: (jnp.minimum((i + 1) * hb_per_tile, n_halo - 1), 0))
    consts, const_specs = _operands([w["g_mix"], w["w_in"], w["g_q_lat"], w["w_uq"], w["g_kv_lat"], w["w_ukv"],
                                     w["g_qn"], w["g_kn"], w["conv_w"], tabs["gm"]])
    in_specs = ([_rows(tm, d), _rows(tm, Z_COLS), prev_halo(Z_COLS), next_halo(Z_COLS), _rows(tm, d), _rows(tm, hp),
                 _rows(tm, hp), _rows(tm, hp), _rows(tm, CONV_WIDTH), next_halo(CONV_WIDTH)]
                + const_specs + [_rows(tm, PAIR * HEAD_PAD)] * 2)
    out_shape = (jax.ShapeDtypeStruct((t_len, d), F32), jax.ShapeDtypeStruct((t_len, Z_COLS), BF16),
                 jax.ShapeDtypeStruct((t_len, d), BF16), jax.ShapeDtypeStruct((t_len, Q_LORA), BF16),
                 jax.ShapeDtypeStruct((t_len, KV_LORA), BF16), jax.ShapeDtypeStruct((t_len, hp), BF16),
                 jax.ShapeDtypeStruct((t_len, hp), BF16),
                 jax.ShapeDtypeStruct((1, d), F32), jax.ShapeDtypeStruct((1, Q_LORA), F32),
                 jax.ShapeDtypeStruct((1, KV_LORA), F32), jax.ShapeDtypeStruct((1, LANES), F32),
                 jax.ShapeDtypeStruct((1, LANES), F32), jax.ShapeDtypeStruct((HALO, CONV_WIDTH), F32))
    out_specs = tuple(_rows(tm, s.shape[1]) for s in out_shape[:7]) + tuple(_whole(s.shape) for s in out_shape[7:])
    return pl.pallas_call(body, name="front_bwd", grid=(nt,), in_specs=in_specs, out_specs=out_specs, out_shape=out_shape,
                          scratch_shapes=[pltpu.VMEM((tm + HALO, CONV_WIDTH), F32), pltpu.VMEM((tm + HALO, CONV_WIDTH), F32)],
                          compiler_params=_cparams(("arbitrary",)))(
                              x, z, z, z, dx1, dqf, dkf, dkv_in, dconv, dconv, *consts, tabs["cos"], tabs["sin"])


def _wgrad(a, b, shard_cols=None, out_dtype=BF16):
    t_len, kk = a.shape
    nn = b.shape[1]
    tk = min(kk, WGRAD_TILE)
    tn = next(c for c in range(min(nn, WGRAD_TILE), 0, -LANES) if nn % c == 0 and c % (shard_cols or LANES) == 0)
    tt = min(t_len, WGRAD_TOKENS)
    nt = t_len // tt
    per_block = tn // shard_cols if shard_cols else 1

    def body(a_ref, b_ref, o_ref, acc):
        t = pl.program_id(2)

        @pl.when(t == 0)
        def _():
            acc[...] = jnp.zeros_like(acc)
        acc[...] += _mm_tn(a_ref[...].astype(BF16), b_ref[...].astype(BF16))

        @pl.when(t == nt - 1)
        def _():
            if shard_cols:
                for s in range(per_block):
                    o_ref[s] = acc[:, s * shard_cols:(s + 1) * shard_cols].astype(out_dtype)
            else:
                o_ref[...] = acc[...].astype(out_dtype)

    if shard_cols:
        out_shape = jax.ShapeDtypeStruct((nn // shard_cols, kk, shard_cols), out_dtype)
        out_spec = pl.BlockSpec((per_block, tk, shard_cols), lambda i, j, t: (j, i, 0))
    else:
        out_shape = jax.ShapeDtypeStruct((kk, nn), out_dtype)
        out_spec = pl.BlockSpec((tk, tn), lambda i, j, t: (i, j))
    return pl.pallas_call(body, name="wgrad", grid=(kk // tk, nn // tn, nt),
                          in_specs=[pl.BlockSpec((tt, tk), lambda i, j, t: (t, i)),
                                    pl.BlockSpec((tt, tn), lambda i, j, t: (t, j))],
                          out_specs=out_spec, out_shape=out_shape, scratch_shapes=[pltpu.VMEM((tk, tn), F32)],
                          compiler_params=_cparams(("parallel", "parallel", "arbitrary")))(a, b)


def _my_place():
    return lax.axis_index("x"), lax.axis_index("y"), lax.axis_index("c")


def _any_specs(n):
    return [pl.BlockSpec(memory_space=pl.ANY)] * n


def _all_gather(blocks):
    n = len(blocks)

    def body(*refs):
        start, forward, finish = _gather_phases(refs[:n], refs[n:2 * n], *refs[2 * n:])
        start()
        forward()
        finish()

    return pl.pallas_call(body, name="all_gather", out_shape=_gather_out_shape(blocks), in_specs=_any_specs(n),
                          out_specs=tuple(_any_specs(n)), scratch_shapes=_gather_semaphores(n))(*blocks)


def _gather_out_shape(blocks):
    return tuple(jax.ShapeDtypeStruct((N_DEV,) + b.shape, b.dtype) for b in blocks)


def _gather_semaphores(n):
    return [pltpu.SemaphoreType.DMA((n, 7)), pltpu.SemaphoreType.DMA((n, 7)), pltpu.SemaphoreType.DMA((n,))]


def _gather_phases(x_refs, out_refs, send_sems, recv_sems, local_sems):
    n = len(x_refs)
    x, y, c = _my_place()
    me, sibling = (x, y, c), (x, y, 1 - c)
    chips = [(1 - x, y), (x, 1 - y), (1 - x, 1 - y)]

    def slot(a, px, py, pc):
        return out_refs[a].at[4 * px + 2 * py + pc]

    def copy(a, k, blk, to, src=None):
        return pltpu.make_async_remote_copy(src_ref=slot(a, *blk) if src is None else src, dst_ref=slot(a, *blk),
                                            send_sem=send_sems.at[a, k], recv_sem=recv_sems.at[a, k],
                                            device_id=to, device_id_type=MESH)

    def own(a):
        return pltpu.make_async_copy(x_refs[a], slot(a, *me), local_sems.at[a])

    def first_hop(a):
        return [copy(a, 0, me, sibling, src=x_refs[a])] + [copy(a, 1 + j, me, (*chip, c), src=x_refs[a])
                                                           for j, chip in enumerate(chips)]

    def passed_on(a):
        return [copy(a, 4 + j, (*chip, c), sibling) for j, chip in enumerate(chips)]

    def start():
        for a in range(n):
            own(a).start()
        for a in range(n):
            for cp in first_hop(a):
                cp.start()

    def forward():
        for j, chip in enumerate(chips):
            for a in range(n):
                copy(a, 1 + j, (*chip, c), me).wait_recv()
                passed_on(a)[j].start()

    def finish():
        for a in range(n):
            copy(a, 0, sibling, me).wait_recv()
        for j, chip in enumerate(chips):
            for a in range(n):
                copy(a, 4 + j, (*chip, 1 - c), me).wait_recv()
        for a in range(n):
            for cp in first_hop(a) + passed_on(a):
                cp.wait_send()
            own(a).wait()

    return start, forward, finish


def _scatter_exchange(parts, landed, layer):
    n = len(parts)

    def body(*refs):
        start, finish = _scatter_phases(refs[:n], refs[2 * n:3 * n], *refs[3 * n:], layer)
        start()
        finish()

    return pl.pallas_call(body, name="scatter_exchange", out_shape=_same_shapes(landed), in_specs=_any_specs(2 * n),
                          out_specs=tuple(_any_specs(n)), scratch_shapes=_scatter_semaphores(n),
                          input_output_aliases={n + a: a for a in range(n)})(*parts, *landed)


def _same_shapes(arrays):
    return tuple(jax.ShapeDtypeStruct(a.shape, a.dtype) for a in arrays)


def _scatter_semaphores(n):
    return [pltpu.SemaphoreType.DMA((n, N_DEV - 1)), pltpu.SemaphoreType.DMA((n, N_DEV - 1)), pltpu.SemaphoreType.DMA((n,))]


def _scatter_phases(part_refs, landed_refs, send_sems, recv_sems, local_sems, layer):
    n = len(part_refs)
    x, y, c = _my_place()
    flips = [(0, 0, 1), (1, 0, 0), (0, 1, 0), (1, 1, 0), (1, 0, 1), (0, 1, 1), (1, 1, 1)]
    peers = [((1 - x) if fx else x, (1 - y) if fy else y, (1 - c) if fc else c) for fx, fy, fc in flips]
    my_k = 4 * x + 2 * y + c

    def index(peer):
        return 4 * peer[0] + 2 * peer[1] + peer[2]

    def send(a, r):
        return pltpu.make_async_remote_copy(src_ref=part_refs[a].at[index(peers[r])], dst_ref=landed_refs[a].at[my_k, layer],
                                            send_sem=send_sems.at[a, r], recv_sem=recv_sems.at[a, r],
                                            device_id=peers[r], device_id_type=MESH)

    def arrival(a, r):
        return pltpu.make_async_remote_copy(src_ref=part_refs[a].at[my_k], dst_ref=landed_refs[a].at[index(peers[r]), layer],
                                            send_sem=send_sems.at[a, r], recv_sem=recv_sems.at[a, r],
                                            device_id=peers[r], device_id_type=MESH)

    def own(a):
        return pltpu.make_async_copy(part_refs[a].at[my_k], landed_refs[a].at[my_k, layer], local_sems.at[a])

    def start():
        for a in range(n):
            own(a).start()
        for r in range(len(peers)):
            for a in range(n):
                send(a, r).start()

    def finish():
        for r in range(len(peers)):
            for a in range(n):
                arrival(a, r).wait_recv()
        for r in range(len(peers)):
            for a in range(n):
                send(a, r).wait_send()
        for a in range(n):
            own(a).wait()

    return start, finish


def _row_block(rows):
    return ROW_BLOCK if rows % ROW_BLOCK == 0 else rows


def _sum_leading(parts):
    n_part, shape = parts.shape[0], parts.shape[1:]
    cols = shape[-1]
    p2 = parts.reshape(n_part, -1, cols)
    rows = p2.shape[1]
    rb = _row_block(rows)

    def body(p_ref, o_ref):
        acc = p_ref[0].astype(F32)
        for k in range(1, n_part):
            acc = acc + p_ref[k].astype(F32)
        o_ref[...] = acc

    out = pl.pallas_call(body, name="sum_leading", grid=(rows // rb,),
                         in_specs=[pl.BlockSpec((n_part, rb, cols), lambda i: (0, i, 0))], out_specs=_rows(rb, cols),
                         out_shape=jax.ShapeDtypeStruct((rows, cols), F32), compiler_params=_cparams(("parallel",)))(p2)
    return out.reshape(shape)


def _adamw(w, g, m, v):
    shape = w.shape
    two_d = (shape[0] * shape[1], shape[2]) if len(shape) == 3 else shape
    rows, cols = two_d
    rb = _row_block(rows)

    def body(w_ref, g_ref, m_ref, v_ref, d_ref, nm_ref, nv_ref):
        gv = g_ref[...]
        nm = ADAM_B1 * m_ref[...] + (1.0 - ADAM_B1) * gv
        nv = ADAM_B2 * v_ref[...] + (1.0 - ADAM_B2) * jnp.square(gv)
        m_hat = nm / (1.0 - ADAM_B1 ** ADAM_STEP)
        v_hat = nv / (1.0 - ADAM_B2 ** ADAM_STEP)
        d_ref[...] = -ADAM_LR * (m_hat / (jnp.sqrt(v_hat) + ADAM_EPS) + ADAM_WD * w_ref[...])
        nm_ref[...] = nm
        nv_ref[...] = nv

    spec = _rows(rb, cols)
    out = jax.ShapeDtypeStruct(two_d, F32)
    res = pl.pallas_call(body, name="adamw", grid=(rows // rb,), in_specs=[spec] * 4, out_specs=(spec,) * 3,
                         out_shape=(out,) * 3, compiler_params=_cparams(("parallel",)))(
                             *(a.reshape(two_d) for a in (w, g, m, v)))
    return tuple(a.reshape(shape) for a in res)


def _rope_tables(positions):
    t_len = positions.shape[0]
    inv_freq = 1.0 / (ROPE_THETA ** (jnp.arange(0, QK_ROPE, 2, dtype=F32) / QK_ROPE))
    ang = positions.astype(F32)[:, None] * inv_freq
    c, s = jnp.cos(ang), jnp.sin(ang)
    one, zero = jnp.ones((t_len, QK_NOPE), F32), jnp.zeros((t_len, QK_NOPE), F32)
    cos = jnp.concatenate([one, c, c, one[:, :LANES - QK_HEAD]], axis=1)
    sin = jnp.concatenate([zero, -s, s, zero[:, :LANES - QK_HEAD]], axis=1)
    idx = jnp.arange(PAIR * HEAD_PAD)
    lane, head = idx % HEAD_PAD, idx // HEAD_PAD
    grp = jnp.where(lane < QK_NOPE, 0, jnp.where(lane < QK_HEAD, 1, 2)) + 3 * head
    val = jnp.where(lane < QK_NOPE, 1.0 / QK_NOPE, jnp.where(lane < QK_HEAD, 1.0 / QK_ROPE, 0.0))
    gm = jnp.where(grp[:, None] == grp[None, :], val[None, :], 0.0).astype(BF16)
    return {"cos": jnp.concatenate([cos] * PAIR, axis=1), "sin": jnp.concatenate([sin] * PAIR, axis=1), "gm": gm}


def _head_gain(g_nope, g_rope):
    one = jnp.concatenate([g_nope, g_rope, jnp.zeros((HEAD_PAD - QK_HEAD,), F32)])
    return jnp.concatenate([one] * PAIR).reshape(1, PAIR * HEAD_PAD)


def _head_pairs(w):
    return jnp.concatenate([w[k::PAIR] for k in range(PAIR)], axis=2)


def _padded_w_in(shards):
    natural = jnp.concatenate([shards[k] for k in range(N_DEV)], axis=1)
    o2, o3 = Q_LORA + KV_LORA, Q_LORA + KV_LORA + QK_ROPE
    zeros = jnp.zeros((natural.shape[0], QK_NOPE), natural.dtype)
    return jnp.concatenate([natural[:, :o2], natural[:, o3:], zeros, natural[:, o2:o3], zeros[:, :LANES - QK_HEAD]], axis=1)


def _w_in_grad_shards(d_in):
    o2 = Q_LORA + KV_LORA
    nat = jnp.concatenate([d_in[:, :o2], d_in[:, Z_KPE[0] + QK_NOPE:Z_KPE[0] + QK_HEAD], d_in[:, o2:Z_XIN[1]]], axis=1)
    width = nat.shape[1] // N_DEV
    return jnp.stack([nat[:, k * width:(k + 1) * width] for k in range(N_DEV)])


def kernel(x, p, positions, g_mix, w_in, g_q_lat, w_uq, g_kv_lat, w_ukv, g_qn_nope, g_qn_rope, g_kn_nope, g_kn_rope, conv_w, g_out_attn, g_out_conv, w_o, g_mlp, w_up, w_down, g_ple, w_ple_gate, w_ple, loss_target, m_g_mix, m_w_in, m_g_q_lat, m_w_uq, m_g_kv_lat, m_w_ukv, m_g_qn_nope, m_g_qn_rope, m_g_kn_nope, m_g_kn_rope, m_conv_w, m_g_out_attn, m_g_out_conv, m_w_o, m_g_mlp, m_w_up, m_w_down, m_g_ple, m_w_ple_gate, m_w_ple, v_g_mix, v_w_in, v_g_q_lat, v_w_uq, v_g_kv_lat, v_w_ukv, v_g_qn_nope, v_g_qn_rope, v_g_kn_nope, v_g_kn_rope, v_conv_w, v_g_out_attn, v_g_out_conv, v_w_o, v_g_mlp, v_w_up, v_w_down, v_g_ple, v_w_ple_gate, v_w_ple):
    given = dict(locals())
    weights = {n: given[n] for n in WEIGHT_NAMES}
    gains = {n: given[n] for n in GAIN_NAMES}
    depth = w_in.shape[0]
    xs, target = x[0], loss_target[0]
    d_model = xs.shape[1]
    uq_cols = w_uq.shape[2]
    n_taps = conv_w.shape[1]

    mat_names = [n for n in SHARD_NAMES if n != "conv_w"]
    local = [weights[n].astype(BF16) for n in mat_names]
    local[1] = jnp.pad(local[1], ((0, 0), (0, 0), (0, HEAD_PAD - uq_cols)))
    local = dict(zip(mat_names, local))
    front_names = ("w_in", "w_uq", "w_ukv")
    first = _all_gather([local[n][0] for n in front_names] + [conv_w])
    conv_full = jnp.transpose(first[-1], (1, 2, 0, 3)).reshape(depth, n_taps, -1)
    tabs = _rope_tables(positions[0])

    def front_weights(layer, full):
        lw = {n: gains[n][layer].reshape(1, -1) for n in GAIN_NAMES}
        lw.update({"w_in": _padded_w_in(full["w_in"]), "w_uq": _head_pairs(full["w_uq"]),
                   "w_ukv": _head_pairs(full["w_ukv"]),
                   "conv_w": jnp.pad(conv_full[layer], ((0, HALO - n_taps), (0, 0))),
                   "g_qn": _head_gain(g_qn_nope[layer], g_qn_rope[layer]),
                   "g_kn": _head_gain(g_kn_nope[layer], g_kn_rope[layer])})
        return lw

    def rest_weights(full):
        return {"w_ple": full["w_ple"], "w_up": full["w_up"], "w_down": full["w_down"],
                "w_o": full["w_o"].reshape(d_model, d_model), "w_ple_gate": full["w_ple_gate"].reshape(d_model, d_model)}

    saved, layer_w = [], []
    cur = xs
    gathered = dict(zip(front_names, first[:-1]))
    for layer in range(depth):
        w = front_weights(layer, gathered)
        z, qf, kf, kv, conv = _front_fwd(cur, w, tabs)
        wanted = [(n, 0) for n in mat_names if n not in front_names] if layer == 0 else []
        wanted += [(n, layer + 1) for n in mat_names] if layer + 1 < depth else []
        coming = {}
        if wanted:
            attn, lse, got = _attn_fwd(qf, kf, kv, gather=[(local[n], at) for n, at in wanted])
            for (n, at), g in zip(wanted, got):
                if at == layer:
                    gathered[n] = g
                else:
                    coming[n] = g
        else:
            attn, lse = _attn_fwd(qf, kf, kv)
        w.update(rest_weights(gathered))
        layer_w.append(w)
        gathered = coming
        x1 = _mix_out_fwd(cur, attn, conv, w)
        x2 = _mlp_fwd(x1, w)
        x3 = _ple_fwd(x2, p[layer, 0], w)
        saved.append(dict(x=cur, z=z, qf=qf, kf=kf, kv=kv, conv=conv, attn=attn, lse=lse, x1=x1, x2=x2))
        cur = x3

    sq, dx = _loss_and_grad(cur, target)
    loss = lax.psum(0.5 / d_model * sq[0, 0], ("x", "y", "c"))

    landed = [lax.empty((N_DEV, depth) + weights[n].shape[1:], BF16) for n in SHARD_NAMES]
    gain_grads = [None] * depth
    pending = None
    for layer in reversed(range(depth)):
        w, s = layer_w[layer], saved[layer]
        pl_in = p[layer, 0]
        dx2, de, h3, dpre, dg_ple = _ple_bwd(dx, s["x2"], pl_in, w)
        dx1, r, da, h2, dg_mlp = _mlp_bwd(dx2, s["x1"], w)
        mixed, dattn, dconv, dg_oa, dg_oc = _mix_out_bwd(dx1, s["attn"], s["conv"], w)
        if pending is None:
            dqf, dkf, dkv = _attn_bwd(s["qf"], s["kf"], s["kv"], s["attn"], dattn, s["lse"])
        else:
            dqf, dkf, dkv, landed = _attn_bwd(s["qf"], s["kf"], s["kv"], s["attn"], dattn, s["lse"],
                                              scatter=(pending, landed, layer + 1))
        (dx0, dz, hb, qn, kvn, dqr, dkvr, dg_mix, dg_q, dg_kv, dg_qn, dg_kn, dcw) = _front_bwd(
            s["x"], s["z"], dx1, dqf, dkf, dkv, dconv, w, tabs)
        parts = {"w_in": _w_in_grad_shards(_wgrad(hb, dz, out_dtype=F32)).astype(BF16),
                 "w_uq": _wgrad(qn, dqr, shard_cols=HEAD_PAD)[..., :uq_cols],
                 "w_ukv": _wgrad(kvn, dkvr, shard_cols=HEAD_PAD),
                 "conv_w": jnp.transpose(dcw[:n_taps].reshape(n_taps, N_DEV, -1), (1, 0, 2)).astype(BF16),
                 "w_o": _wgrad(mixed, dx1).reshape((N_DEV,) + w_o.shape[1:]),
                 "w_up": _wgrad(h2, da, shard_cols=w_up.shape[2]),
                 "w_down": _wgrad(r, dx2).reshape((N_DEV,) + w_down.shape[1:]),
                 "w_ple_gate": _wgrad(h3, dpre).reshape((N_DEV,) + w_ple_gate.shape[1:]),
                 "w_ple": _wgrad(pl_in, de, shard_cols=w_ple.shape[2])}
        pending = [parts[n] for n in SHARD_NAMES]
        gain_grads[layer] = jnp.concatenate([
            dg_mix[0], dg_q[0], dg_kv[0], dg_qn[0, :QK_NOPE], dg_qn[0, QK_NOPE:QK_HEAD], dg_kn[0, :QK_NOPE],
            dg_kn[0, QK_NOPE:QK_HEAD], dg_oa[0], dg_oc[0], dg_mlp[0], dg_ple[0]])
        dx = dx0
    landed = _scatter_exchange(pending, landed, 0)
    grads = {n: _sum_leading(got) for n, got in zip(SHARD_NAMES, landed)}

    gg = jnp.stack(gain_grads)
    gg_rows = -(-gg.size // (HALO * LANES)) * HALO
    gg_pad = jnp.pad(gg.reshape(-1), (0, gg_rows * LANES - gg.size)).reshape(gg_rows, LANES)
    gg_sum = _sum_leading(_all_gather([gg_pad])[0]).reshape(-1)[:gg.size].reshape(gg.shape)
    off = 0
    for n in GAIN_NAMES:
        width = gains[n].shape[1]
        grads[n] = gg_sum[:, off:off + width]
        off += width

    deltas, new_m, new_v = {}, {}, {}
    for n in WEIGHT_NAMES:
        deltas[n], new_m[n], new_v[n] = _adamw(weights[n], grads[n], given["m_" + n], given["v_" + n])
    return (loss, dx[None], *[grads[n] for n in WEIGHT_NAMES], *[deltas[n] for n in WEIGHT_NAMES],
            *[new_m[n] for n in WEIGHT_NAMES], *[new_v[n] for n in WEIGHT_NAMES])
```

```python
import jax
import jax.numpy as jnp
from jax import lax
from jax.experimental import pallas as pl
from jax.experimental.pallas import tpu as pltpu

F32 = jnp.float32
BF16 = jnp.bfloat16
MESH = pl.DeviceIdType.MESH

N_HEADS = 8
QK_NOPE = 64
QK_ROPE = 32
QK_HEAD = QK_NOPE + QK_ROPE
V_HEAD = 64
HEAD_PAD = 128
PAIR = 2
ATTN_SCALE = QK_HEAD ** -0.5
Q_LORA = 384
KV_LORA = 256
CONV_WIDTH = 512
ATTN_WIDTH = N_HEADS * V_HEAD
ROPE_THETA = 10000.0
EPS = 1e-6
ADAM_LR, ADAM_B1, ADAM_B2, ADAM_EPS, ADAM_WD, ADAM_STEP = 0.001, 0.9, 0.999, 1e-08, 0.01, 10

Z_Q = (0, 384)
Z_KV = (384, 640)
Z_GB = (640, 1152)
Z_GC = (1152, 1664)
Z_XIN = (1664, 2176)
Z_KPE = (2176, 2304)
Z_COLS = 2304

N_DEV = 8
LANES = 128
V7X_VMEM_LIMIT = 52 * 1024 * 1024
TOKEN_TILE = 256
ATTN_BLOCK = 256
ATTN_Q_SUB = 2
ATTN_KV_SUB = 2
ROW_BLOCK = 512
WGRAD_TOKENS = 2048
WGRAD_TILE = 1024
HALO = 8

GAIN_NAMES = ("g_mix", "g_q_lat", "g_kv_lat", "g_qn_nope", "g_qn_rope", "g_kn_nope", "g_kn_rope",
              "g_out_attn", "g_out_conv", "g_mlp", "g_ple")
SHARD_NAMES = ("w_in", "w_uq", "w_ukv", "conv_w", "w_o", "w_up", "w_down", "w_ple_gate", "w_ple")
WEIGHT_NAMES = ("g_mix", "w_in", "g_q_lat", "w_uq", "g_kv_lat", "w_ukv", "g_qn_nope", "g_qn_rope", "g_kn_nope",
                "g_kn_rope", "conv_w", "g_out_attn", "g_out_conv", "w_o", "g_mlp", "w_up", "w_down", "g_ple",
                "w_ple_gate", "w_ple")


def _cparams(semantics=None):
    return pltpu.CompilerParams(dimension_semantics=semantics, vmem_limit_bytes=V7X_VMEM_LIMIT)


def _mm(a, b):
    return jnp.dot(a, b, preferred_element_type=F32)


def _mm_nt(a, b):
    return lax.dot_general(a, b, (((1,), (1,)), ((), ())), preferred_element_type=F32)


def _mm_tn(a, b):
    return lax.dot_general(a, b, (((0,), (0,)), ((), ())), preferred_element_type=F32)


def _rms(x, g):
    r = lax.rsqrt(jnp.mean(x * x, axis=-1, keepdims=True) + EPS)
    return (x * r) * g


def _rms_bwd(x, g, dy):
    r = lax.rsqrt(jnp.mean(x * x, axis=-1, keepdims=True) + EPS)
    xh = x * r
    dg = jnp.sum(dy * xh, axis=0, keepdims=True)
    dyg = dy * g
    dx = r * (dyg - xh * jnp.mean(dyg * xh, axis=-1, keepdims=True))
    return dx, dg


def _group_mean(t, gm):
    hi = t.astype(BF16)
    lo = (t - hi.astype(F32)).astype(BF16)
    return _mm(hi, gm) + _mm(lo, gm)


def _swap_rope_halves(x, lane):
    half = QK_ROPE // 2
    swapped = jnp.where(lane < QK_NOPE + half, pltpu.roll(x, x.shape[1] - half, 1), pltpu.roll(x, half, 1))
    return jnp.where((lane >= QK_NOPE) & (lane < QK_HEAD), swapped, 0.0)


def _qk_fwd(x, g, cos, sin, gm, lane):
    r = lax.rsqrt(_group_mean(x * x, gm) + EPS)
    n = (x * r) * g
    return n * cos + _swap_rope_halves(n, lane) * sin


def _qk_bwd(x, g, dy, cos, sin, gm, lane):
    r = lax.rsqrt(_group_mean(x * x, gm) + EPS)
    xh = x * r
    dn = dy * cos + _swap_rope_halves(dy * sin, lane)
    dg = jnp.sum(dn * xh, axis=0, keepdims=True)
    dng = dn * g
    dx = r * (dng - xh * _group_mean(dng * xh, gm))
    return dx, dg


def _rows(tm, n):
    return pl.BlockSpec((tm, n), lambda i: (i, 0))


def _whole(shape):
    zeros = (0,) * len(shape)
    return pl.BlockSpec(shape, lambda i: zeros)


def _operands(arrays):
    return list(arrays), [_whole(a.shape) for a in arrays]


def _accumulate(ref, first, value):
    @pl.when(first)
    def _():
        ref[...] = jnp.zeros_like(ref)
    ref[...] += value


def _front_fwd(x, w, tabs):
    t_len, d = x.shape
    tm = min(TOKEN_TILE, t_len)
    hp = N_HEADS * HEAD_PAD

    def body(x_ref, gmix, win, gq, wuq, gkv, wukv, gqn, gkn, cw_ref, gm_ref, cos_ref, sin_ref,
             z_ref, qf_ref, kf_ref, kv_ref, conv_ref, ubuf):
        i = pl.program_id(0)
        h = _rms(x_ref[...], gmix[...])
        z = _mm(h.astype(BF16), win[...])
        z_ref[...] = z
        qnb = _rms(z[:, Z_Q[0]:Z_Q[1]], gq[...]).astype(BF16)
        kvb = _rms(z[:, Z_KV[0]:Z_KV[1]], gkv[...]).astype(BF16)
        kpe = z[:, Z_KPE[0]:Z_KPE[1]]
        kpe = jnp.concatenate([kpe] * PAIR, axis=1)
        cos, sin, gm = cos_ref[...], sin_ref[...], gm_ref[...]
        lane = lax.broadcasted_iota(jnp.int32, (tm, PAIR * HEAD_PAD), 1) & (HEAD_PAD - 1)
        for pr in range(N_HEADS // PAIR):
            sl = slice(pr * PAIR * HEAD_PAD, (pr + 1) * PAIR * HEAD_PAD)
            qf_ref[:, sl] = (_qk_fwd(_mm(qnb, wuq[pr]), gqn[...], cos, sin, gm, lane) * ATTN_SCALE).astype(BF16)
            kv = _mm(kvb, wukv[pr])
            kv_ref[:, sl] = kv.astype(BF16)
            kf_ref[:, sl] = _qk_fwd(jnp.where(lane < QK_NOPE, kv, 0.0) + kpe, gkn[...], cos, sin, gm, lane).astype(BF16)
        u = z[:, Z_GC[0]:Z_GC[1]] * z[:, Z_XIN[0]:Z_XIN[1]]

        @pl.when(i == 0)
        def _():
            ubuf[0:HALO, :] = jnp.zeros((HALO, CONV_WIDTH), F32)
        ubuf[HALO:HALO + tm, :] = u
        cw = cw_ref[...]
        y = cw[0:1] * u + cw[1:2] * ubuf[pl.ds(HALO - 1, tm), :] + cw[2:3] * ubuf[pl.ds(HALO - 2, tm), :]
        conv_ref[...] = z[:, Z_GB[0]:Z_GB[1]] * y
        ubuf[0:HALO, :] = u[tm - HALO:tm, :]

    consts, const_specs = _operands([w["g_mix"], w["w_in"], w["g_q_lat"], w["w_uq"], w["g_kv_lat"], w["w_ukv"],
                                     w["g_qn"], w["g_kn"], w["conv_w"], tabs["gm"]])
    out_shape = (jax.ShapeDtypeStruct((t_len, Z_COLS), F32), jax.ShapeDtypeStruct((t_len, hp), BF16),
                 jax.ShapeDtypeStruct((t_len, hp), BF16), jax.ShapeDtypeStruct((t_len, hp), BF16),
                 jax.ShapeDtypeStruct((t_len, CONV_WIDTH), F32))
    return pl.pallas_call(body, name="front_fwd", grid=(t_len // tm,),
                          in_specs=[_rows(tm, d)] + const_specs + [_rows(tm, PAIR * HEAD_PAD)] * 2,
                          out_specs=tuple(_rows(tm, s.shape[1]) for s in out_shape), out_shape=out_shape,
                          scratch_shapes=[pltpu.VMEM((tm + HALO, CONV_WIDTH), F32)],
                          compiler_params=_cparams(("arbitrary",)))(x, *consts, tabs["cos"], tabs["sin"])


def _attn_fwd(qf, kf, kv, gather=None):
    t_len = qf.shape[0]
    blk = min(ATTN_BLOCK, t_len)
    nb = t_len // blk
    n_sub = ATTN_Q_SUB
    bq = blk // n_sub
    chains = [(hh, a) for hh in range(2) for a in range(n_sub)]

    def body(q_ref, k_ref, kv_ref, o_ref, lse_ref):
        lane = lax.broadcasted_iota(jnp.int32, (bq, LANES), 1)
        row = lax.broadcasted_iota(jnp.int32, (bq, blk), 0)
        col = lax.broadcasted_iota(jnp.int32, (bq, blk), 1)

        def head_cols(hh):
            return slice(hh * HEAD_PAD, (hh + 1) * HEAD_PAD)

        def softmax_step(s, kvv, state, first_row=None):
            m, l, acc = state
            if first_row is not None:
                s = jnp.where(col <= row + first_row, s, -jnp.inf)
            m_new = jnp.maximum(m, jnp.max(s, axis=-1, keepdims=True))
            p = jnp.exp(s - m_new)
            alpha = jnp.exp(m - m_new)
            l = alpha * l + jnp.sum(p, axis=-1, keepdims=True)
            acc = alpha * acc + _mm(p.astype(BF16), kvv)
            return m_new, l, acc

        def qblock(i, carry):
            start = pl.multiple_of(i * blk, blk)
            rows = [pl.ds(pl.multiple_of(start + a * bq, bq), bq) for a in range(n_sub)]
            qs = {(hh, a): q_ref[rows[a], head_cols(hh)] for hh, a in chains}

            def scores(j):
                ks = pl.ds(pl.multiple_of(j * blk, blk), blk)
                return tuple(_mm_nt(qs[hh, a], k_ref[ks, head_cols(hh)]) for hh, a in chains)

            def kstep(j, carried, diagonal=False):
                ss, states = carried
                ss_next = ss if diagonal else scores(j + 1)
                ks = pl.ds(pl.multiple_of(j * blk, blk), blk)
                new = tuple(softmax_step(s, kv_ref[ks, head_cols(hh)], st, a * bq if diagonal else None)
                            for (hh, a), s, st in zip(chains, ss, states))
                return ss_next, new

            init = (jnp.full((bq, 1), -jnp.inf, F32), jnp.zeros((bq, 1), F32), jnp.zeros((bq, LANES), F32))
            carried = lax.fori_loop(0, i, kstep, (scores(0), (init,) * len(chains)))
            _, states = kstep(i, carried, diagonal=True)
            for a in range(n_sub):
                (m0, l0, acc0), (m1, l1, acc1) = states[chains.index((0, a))], states[chains.index((1, a))]
                o_ref[rows[a], :] = jnp.where(lane < V_HEAD, pltpu.roll(acc0 / l0, V_HEAD, 1), acc1 / l1)
                lse_ref[rows[a], head_cols(0)] = jnp.broadcast_to(m0 + jnp.log(l0), (bq, LANES))
                lse_ref[rows[a], head_cols(1)] = jnp.broadcast_to(m1 + jnp.log(l1), (bq, LANES))
            return carry

        lax.fori_loop(0, nb, qblock, 0)

    n_steps = N_HEADS // 2
    n_gather = len(gather) if gather else 0

    def hosting_body(*refs):
        q_ref, k_ref, kv_ref = refs[:3]
        x_refs = [r.at[layer] for r, (_, layer) in zip(refs[3:3 + n_gather], gather)]
        o_ref, lse_ref = refs[3 + n_gather:5 + n_gather]
        start, forward, finish = _gather_phases(x_refs, refs[5 + n_gather:5 + 2 * n_gather], *refs[5 + 2 * n_gather:])
        step = pl.program_id(0)
        pl.when(step == 0)(start)
        pl.when(step == n_steps // 2)(forward)
        body(q_ref, k_ref, kv_ref, o_ref, lse_ref)
        pl.when(step == n_steps - 1)(finish)

    heads = pl.BlockSpec((t_len, 2 * HEAD_PAD), lambda h: (0, h))
    pair = pl.BlockSpec((t_len, 2 * V_HEAD), lambda h: (0, h))
    out_shape = (jax.ShapeDtypeStruct((t_len, ATTN_WIDTH), F32), jax.ShapeDtypeStruct((t_len, N_HEADS * LANES), F32))
    if not gather:
        return pl.pallas_call(body, name="attn_fwd", grid=(n_steps,), in_specs=[heads, heads, heads],
                              out_specs=(pair, heads), out_shape=out_shape,
                              compiler_params=_cparams(("arbitrary",)))(qf, kf, kv)
    shards = [s[layer] for s, layer in gather]
    res = pl.pallas_call(hosting_body, name="attn_fwd_gather", grid=(n_steps,),
                         in_specs=[heads, heads, heads] + _any_specs(n_gather),
                         out_specs=(pair, heads) + tuple(_any_specs(n_gather)),
                         out_shape=out_shape + _gather_out_shape(shards), scratch_shapes=_gather_semaphores(n_gather),
                         compiler_params=_cparams(("arbitrary",)))(qf, kf, kv, *[s for s, _ in gather])
    return res[0], res[1], res[2:]


def _mix_out_fwd(x, attn, conv, w):
    t_len, d = x.shape
    tm = min(TOKEN_TILE, t_len)

    def body(x_ref, a_ref, c_ref, goa, goc, wo, x1_ref):
        mixed = jnp.concatenate([_rms(a_ref[...], goa[...]), _rms(c_ref[...], goc[...])], axis=1)
        x1_ref[...] = x_ref[...] + _mm(mixed.astype(BF16), wo[...])

    consts, const_specs = _operands([w["g_out_attn"], w["g_out_conv"], w["w_o"]])
    return pl.pallas_call(body, name="mix_out_fwd", grid=(t_len // tm,),
                          in_specs=[_rows(tm, d), _rows(tm, ATTN_WIDTH), _rows(tm, CONV_WIDTH)] + const_specs,
                          out_specs=_rows(tm, d), out_shape=jax.ShapeDtypeStruct((t_len, d), F32),
                          compiler_params=_cparams(("parallel",)))(x, attn, conv, *consts)


def _mlp_fwd(x1, w):
    t_len, d = x1.shape
    tm = min(TOKEN_TILE, t_len)

    def body(x_ref, g, wup, wdn, x2_ref):
        x1v = x_ref[...]
        hb = _rms(x1v, g[...]).astype(BF16)
        acc = x1v
        for k in range(N_DEV):
            a = jnp.maximum(_mm(hb, wup[k]), 0.0)
            acc = acc + _mm((a * a).astype(BF16), wdn[k])
        x2_ref[...] = acc

    consts, const_specs = _operands([w["g_mlp"], w["w_up"], w["w_down"]])
    return pl.pallas_call(body, name="mlp_fwd", grid=(t_len // tm,), in_specs=[_rows(tm, d)] + const_specs,
                          out_specs=_rows(tm, d), out_shape=jax.ShapeDtypeStruct((t_len, d), F32),
                          compiler_params=_cparams(("parallel",)))(x1, *consts)


def _ple_fwd(x2, p, w):
    t_len, d = x2.shape
    tm = min(TOKEN_TILE, t_len)

    def body(x_ref, p_ref, g, wg, wp, x3_ref):
        x2v = x_ref[...]
        gate = jax.nn.sigmoid(_mm(_rms(x2v, g[...]).astype(BF16), wg[...]))
        pb = p_ref[...].astype(BF16)
        e = jnp.concatenate([_mm(pb, wp[k]) for k in range(N_DEV)], axis=1)
        x3_ref[...] = x2v + gate * e

    consts, const_specs = _operands([w["g_ple"], w["w_ple_gate"], w["w_ple"]])
    return pl.pallas_call(body, name="ple_fwd", grid=(t_len // tm,),
                          in_specs=[_rows(tm, d), _rows(tm, p.shape[1])] + const_specs, out_specs=_rows(tm, d),
                          out_shape=jax.ShapeDtypeStruct((t_len, d), F32),
                          compiler_params=_cparams(("parallel",)))(x2, p, *consts)


def _loss_and_grad(y, target):
    t_len, d = y.shape
    tm = min(TOKEN_TILE, t_len)

    def body(y_ref, t_ref, sq_ref, dy_ref):
        err = y_ref[...] - t_ref[...]
        dy_ref[...] = err / d
        total = jnp.sum(jnp.sum(err * err, axis=0, keepdims=True), axis=1, keepdims=True)
        _accumulate(sq_ref, pl.program_id(0) == 0, jnp.broadcast_to(total, (HALO, LANES)))

    return pl.pallas_call(body, name="loss_grad", grid=(t_len // tm,), in_specs=[_rows(tm, d), _rows(tm, d)],
                          out_specs=(_whole((HALO, LANES)), _rows(tm, d)),
                          out_shape=(jax.ShapeDtypeStruct((HALO, LANES), F32), jax.ShapeDtypeStruct((t_len, d), F32)),
                          compiler_params=_cparams(("arbitrary",)))(y, target)


def _ple_bwd(dx3, x2, p, w):
    t_len, d = x2.shape
    tm = min(TOKEN_TILE, t_len)

    def body(dx3_ref, x_ref, p_ref, g, wg, wp, dx2_ref, de_ref, h3_ref, dpre_ref, dg_ref):
        x2v, dx3v = x_ref[...], dx3_ref[...]
        hb = _rms(x2v, g[...]).astype(BF16)
        h3_ref[...] = hb
        gate = jax.nn.sigmoid(_mm(hb, wg[...]))
        pb = p_ref[...].astype(BF16)
        e = jnp.concatenate([_mm(pb, wp[k]) for k in range(N_DEV)], axis=1)
        de_ref[...] = (dx3v * gate).astype(BF16)
        dpre = ((dx3v * e) * gate * (1.0 - gate)).astype(BF16)
        dpre_ref[...] = dpre
        dx, dg = _rms_bwd(x2v, g[...], _mm_nt(dpre, wg[...]))
        dx2_ref[...] = dx3v + dx
        _accumulate(dg_ref, pl.program_id(0) == 0, dg)

    consts, const_specs = _operands([w["g_ple"], w["w_ple_gate"], w["w_ple"]])
    out_shape = (jax.ShapeDtypeStruct((t_len, d), F32), jax.ShapeDtypeStruct((t_len, d), BF16),
                 jax.ShapeDtypeStruct((t_len, d), BF16), jax.ShapeDtypeStruct((t_len, d), BF16),
                 jax.ShapeDtypeStruct((1, d), F32))
    return pl.pallas_call(body, name="ple_bwd", grid=(t_len // tm,),
                          in_specs=[_rows(tm, d), _rows(tm, d), _rows(tm, p.shape[1])] + const_specs,
                          out_specs=(_rows(tm, d),) * 4 + (_whole((1, d)),), out_shape=out_shape,
                          compiler_params=_cparams(("arbitrary",)))(dx3, x2, p, *consts)


def _mlp_bwd(dx2, x1, w):
    t_len, d = x1.shape
    tm = min(TOKEN_TILE, t_len)
    fc = w["w_up"].shape[2]
    ff = N_DEV * fc

    def body(dx2_ref, x_ref, g, wup, wdn, dx1_ref, r_ref, da_ref, h2_ref, dg_ref):
        x1v, dx2v = x_ref[...], dx2_ref[...]
        hb = _rms(x1v, g[...]).astype(BF16)
        h2_ref[...] = hb
        dxb = dx2v.astype(BF16)
        dh = jnp.zeros((tm, d), F32)
        for k in range(N_DEV):
            a = jnp.maximum(_mm(hb, wup[k]), 0.0)
            r_ref[:, k * fc:(k + 1) * fc] = (a * a).astype(BF16)
            da = (_mm_nt(dxb, wdn[k]) * (2.0 * a)).astype(BF16)
            da_ref[:, k * fc:(k + 1) * fc] = da
            dh = dh + _mm_nt(da, wup[k])
        dx, dg = _rms_bwd(x1v, g[...], dh)
        dx1_ref[...] = dx2v + dx
        _accumulate(dg_ref, pl.program_id(0) == 0, dg)

    consts, const_specs = _operands([w["g_mlp"], w["w_up"], w["w_down"]])
    out_shape = (jax.ShapeDtypeStruct((t_len, d), F32), jax.ShapeDtypeStruct((t_len, ff), BF16),
                 jax.ShapeDtypeStruct((t_len, ff), BF16), jax.ShapeDtypeStruct((t_len, d), BF16),
                 jax.ShapeDtypeStruct((1, d), F32))
    return pl.pallas_call(body, name="mlp_bwd", grid=(t_len // tm,), in_specs=[_rows(tm, d), _rows(tm, d)] + const_specs,
                          out_specs=(_rows(tm, d), _rows(tm, ff), _rows(tm, ff), _rows(tm, d), _whole((1, d))),
                          out_shape=out_shape, compiler_params=_cparams(("arbitrary",)))(dx2, x1, *consts)


def _mix_out_bwd(dx1, attn, conv, w):
    t_len, d = dx1.shape
    tm = min(TOKEN_TILE, t_len)

    def body(dx1_ref, a_ref, c_ref, goa, goc, wo, mixed_ref, da_ref, dc_ref, dgoa_ref, dgoc_ref):
        av, cv = a_ref[...], c_ref[...]
        mixed_ref[...] = jnp.concatenate([_rms(av, goa[...]), _rms(cv, goc[...])], axis=1).astype(BF16)
        dmixed = _mm_nt(dx1_ref[...].astype(BF16), wo[...])
        da, dga = _rms_bwd(av, goa[...], dmixed[:, :ATTN_WIDTH])
        dc, dgc = _rms_bwd(cv, goc[...], dmixed[:, ATTN_WIDTH:])
        da_ref[...] = da
        dc_ref[...] = dc
        first = pl.program_id(0) == 0
        _accumulate(dgoa_ref, first, dga)
        _accumulate(dgoc_ref, first, dgc)

    consts, const_specs = _operands([w["g_out_attn"], w["g_out_conv"], w["w_o"]])
    out_shape = (jax.ShapeDtypeStruct((t_len, d), BF16), jax.ShapeDtypeStruct((t_len, ATTN_WIDTH), F32),
                 jax.ShapeDtypeStruct((t_len, CONV_WIDTH), F32), jax.ShapeDtypeStruct((1, ATTN_WIDTH), F32),
                 jax.ShapeDtypeStruct((1, CONV_WIDTH), F32))
    out_specs = (_rows(tm, d), _rows(tm, ATTN_WIDTH), _rows(tm, CONV_WIDTH), _whole((1, ATTN_WIDTH)),
                 _whole((1, CONV_WIDTH)))
    return pl.pallas_call(body, name="mix_out_bwd", grid=(t_len // tm,),
                          in_specs=[_rows(tm, d), _rows(tm, ATTN_WIDTH), _rows(tm, CONV_WIDTH)] + const_specs,
                          out_specs=out_specs, out_shape=out_shape,
                          compiler_params=_cparams(("arbitrary",)))(dx1, attn, conv, *consts)


def _attn_bwd(qf, kf, kv, o, do, lse, scatter):
    t_len = qf.shape[0]
    blk = min(ATTN_BLOCK, t_len)
    nb = t_len // blk
    n_sub = ATTN_KV_SUB if nb % ATTN_KV_SUB == 0 else 1
    reps = blk // LANES

    def body(q_ref, k_ref, kv_ref, o_ref, do_ref, lse_ref, dq_ref, dk_ref, dkv_ref, delta_ref, dob_ref):
        hd = pl.program_id(0)
        lane = lax.broadcasted_iota(jnp.int32, (blk, LANES), 1)
        even = (lane * 0 + hd % 2) == 0
        mine = jnp.where(lane < V_HEAD, 0, 1) == hd % 2
        row = lax.broadcasted_iota(jnp.int32, (blk, blk), 0)
        col = lax.broadcasted_iota(jnp.int32, (blk, blk), 1)
        dq_ref[...] = jnp.zeros_like(dq_ref)

        def prepare(i, carry):
            qs = pl.ds(pl.multiple_of(i * blk, blk), blk)
            dov = do_ref[qs, :]
            prod = jnp.where(mine, dov * o_ref[qs, :], 0.0)
            delta_ref[qs, :] = jnp.broadcast_to(jnp.sum(prod, axis=-1, keepdims=True), (blk, LANES))
            moved = jnp.where(even, pltpu.roll(dov, V_HEAD, 1), dov)
            dob_ref[qs, :] = jnp.where(lane >= V_HEAD, moved, 0.0).astype(BF16)
            return carry
        lax.fori_loop(0, nb, prepare, 0)

        def kvblock(jj, carry):
            base = jj * n_sub
            kss = [pl.ds(pl.multiple_of((base + a) * blk, blk), blk) for a in range(n_sub)]
            k = [k_ref[ks, :] for ks in kss]
            kvv = [kv_ref[ks, :] for ks in kss]

            def products(i):
                qs = pl.ds(pl.multiple_of(i * blk, blk), blk)
                q, dob = q_ref[qs, :], dob_ref[qs, :]
                return tuple((_mm_nt(q, k[a]), _mm_nt(dob, kvv[a])) for a in range(n_sub))

            def qstep(i, raw, accs, kinds):
                qs = pl.ds(pl.multiple_of(i * blk, blk), blk)
                q = q_ref[qs, :]
                dob = dob_ref[qs, :]
                lse_t = jnp.concatenate([lse_ref[qs, :]] * reps, axis=1)
                delta_t = jnp.concatenate([delta_ref[qs, :]] * reps, axis=1)
                new, dq_add = [], None
                for a in range(n_sub):
                    if kinds[a] is None:
                        new.append(accs[a])
                        continue
                    dk_acc, dv_acc = accs[a]
                    s, dp = raw[a]
                    if kinds[a]:
                        s = jnp.where(col <= row, s, -jnp.inf)
                    p = jnp.exp(s - lse_t)
                    ds = (p * (dp - delta_t)).astype(BF16)
                    new.append((dk_acc + _mm_tn(ds, q), dv_acc + _mm_tn(p.astype(BF16), dob)))
                    part = _mm(ds, k[a])
                    dq_add = part if dq_add is None else dq_add + part
                dq_ref[qs, :] += dq_add
                return tuple(new)

            zero = jnp.zeros((blk, LANES), F32)
            accs = ((zero, zero),) * n_sub
            for b in range(n_sub):
                accs = qstep(base + b, products(base + b), accs, tuple((a == b) if a <= b else None for a in range(n_sub)))

            def pipelined(i, carried):
                raw, acc = carried
                return products(jnp.minimum(i + 1, nb - 1)), qstep(i, raw, acc, (False,) * n_sub)

            first = base + n_sub
            _, accs = lax.fori_loop(first, nb, pipelined, (products(jnp.minimum(first, nb - 1)), accs))
            for a in range(n_sub):
                dk_ref[kss[a], :] = accs[a][0]
                dkv_ref[kss[a], :] = accs[a][1]
            return carry
        lax.fori_loop(0, nb // n_sub, kvblock, 0)

    n_sc = len(scatter[0])

    def hosting_body(*refs):
        ins, rest = refs[:6], refs[6 + 2 * n_sc:]
        parts = refs[6:6 + n_sc]
        outs, landed, sems, scratch = rest[:3], rest[3:3 + n_sc], rest[3 + n_sc:6 + n_sc], rest[6 + n_sc:]
        start, finish = _scatter_phases(parts, landed, *sems, scatter[2])
        hd = pl.program_id(0)
        pl.when(hd == 0)(start)
        body(*ins, *outs, *scratch)
        pl.when(hd == N_HEADS - 1)(finish)

    head = pl.BlockSpec((t_len, HEAD_PAD), lambda h: (0, h))
    pair = pl.BlockSpec((t_len, 2 * V_HEAD), lambda h: (0, h // 2))
    out = jax.ShapeDtypeStruct((t_len, N_HEADS * HEAD_PAD), F32)
    vmem_scratch = [pltpu.VMEM((t_len, LANES), F32), pltpu.VMEM((t_len, LANES), BF16)]
    res = pl.pallas_call(hosting_body, name="attn_bwd_scatter", grid=(N_HEADS,),
                         in_specs=[head, head, head, pair, pair, head] + _any_specs(2 * n_sc),
                         out_specs=(head, head, head) + tuple(_any_specs(n_sc)),
                         out_shape=(out, out, out) + _same_shapes(scatter[1]),
                         scratch_shapes=_scatter_semaphores(n_sc) + vmem_scratch,
                         input_output_aliases={6 + n_sc + a: 3 + a for a in range(n_sc)},
                         compiler_params=_cparams(("arbitrary",)))(qf, kf, kv, o, do, lse, *scatter[0], *scatter[1])
    return res[0], res[1], res[2], res[3:]


def _front_bwd(x, z, dx1, dqf, dkf, dkv_in, dconv, w, tabs):
    t_len, d = x.shape
    tm = min(TOKEN_TILE, t_len)
    nt = t_len // tm
    hb_per_tile = tm // HALO
    n_halo = t_len // HALO
    hp = N_HEADS * HEAD_PAD

    def body(x_ref, z_ref, zp_ref, zn_ref, dx1_ref, dqf_ref, dkf_ref, dkv_ref, dc_ref, dcn_ref,
             gmix, win, gq, wuq, gkv, wukv, gqn, gkn, cw_ref, gm_ref, cos_ref, sin_ref,
             dx_ref, dz_ref, h_ref, qn_ref, kvn_ref, dqr_ref, dkvr_ref,
             dgmix_ref, dgq_ref, dgkv_ref, dgqn_ref, dgkn_ref, dcw_ref, ubuf, dybuf):
        i = pl.program_id(0)
        first = i == 0
        xv, zv = x_ref[...], z_ref[...]
        hb = _rms(xv, gmix[...]).astype(BF16)
        h_ref[...] = hb
        zq, zkv = zv[:, Z_Q[0]:Z_Q[1]], zv[:, Z_KV[0]:Z_KV[1]]
        qnb = _rms(zq, gq[...]).astype(BF16)
        qn_ref[...] = qnb
        kvb = _rms(zkv, gkv[...]).astype(BF16)
        kvn_ref[...] = kvb
        kpe = zv[:, Z_KPE[0]:Z_KPE[1]]
        kpe = jnp.concatenate([kpe] * PAIR, axis=1)
        cos, sin, gm = cos_ref[...], sin_ref[...], gm_ref[...]
        width = PAIR * HEAD_PAD
        lane = lax.broadcasted_iota(jnp.int32, (tm, width), 1) & (HEAD_PAD - 1)
        is_nope = lane < QK_NOPE
        is_rope = (lane >= QK_NOPE) & (lane < QK_HEAD)
        dkpe = jnp.zeros((tm, width), F32)
        dgqn = jnp.zeros((1, width), F32)
        dgkn = jnp.zeros((1, width), F32)
        dqn = jnp.zeros((tm, Q_LORA), F32)
        dkvn = jnp.zeros((tm, KV_LORA), F32)
        for pr in range(N_HEADS // PAIR):
            sl = slice(pr * width, (pr + 1) * width)
            dxq, dg = _qk_bwd(_mm(qnb, wuq[pr]), gqn[...], dqf_ref[:, sl] * ATTN_SCALE, cos, sin, gm, lane)
            dxq = dxq.astype(BF16)
            dqr_ref[:, sl] = dxq
            dqn = dqn + _mm_nt(dxq, wuq[pr])
            dgqn = dgqn + dg
            k_raw = jnp.where(is_nope, _mm(kvb, wukv[pr]), 0.0) + kpe
            dxk, dg = _qk_bwd(k_raw, gkn[...], dkf_ref[:, sl], cos, sin, gm, lane)
            dkv = jnp.where(is_nope, dxk, dkv_ref[:, sl]).astype(BF16)
            dkvr_ref[:, sl] = dkv
            dkvn = dkvn + _mm_nt(dkv, wukv[pr])
            dkpe = dkpe + jnp.where(is_rope, dxk, 0.0)
            dgkn = dgkn + dg
        dkpe = dkpe[:, :HEAD_PAD] + dkpe[:, HEAD_PAD:]
        _accumulate(dgqn_ref, first, dgqn[:, :HEAD_PAD] + dgqn[:, HEAD_PAD:])
        _accumulate(dgkn_ref, first, dgkn[:, :HEAD_PAD] + dgkn[:, HEAD_PAD:])
        dzq, dg = _rms_bwd(zq, gq[...], dqn)
        _accumulate(dgq_ref, first, dg)
        dzkv, dg = _rms_bwd(zkv, gkv[...], dkvn)
        _accumulate(dgkv_ref, first, dg)

        gb, gc, xin = zv[:, Z_GB[0]:Z_GB[1]], zv[:, Z_GC[0]:Z_GC[1]], zv[:, Z_XIN[0]:Z_XIN[1]]
        u = gc * xin
        dcv = dc_ref[...]
        dy = dcv * gb
        zp, zn = zp_ref[...], zn_ref[...]
        ubuf[0:HALO, :] = (zp[:, Z_GC[0]:Z_GC[1]] * zp[:, Z_XIN[0]:Z_XIN[1]]) * jnp.where(first, 0.0, 1.0)
        ubuf[HALO:HALO + tm, :] = u
        dybuf[0:tm, :] = dy
        dybuf[tm:tm + HALO, :] = (dcn_ref[...] * zn[:, Z_GB[0]:Z_GB[1]]) * jnp.where(i == nt - 1, 0.0, 1.0)
        cw = cw_ref[...]
        u1, u2 = ubuf[pl.ds(HALO - 1, tm), :], ubuf[pl.ds(HALO - 2, tm), :]
        y = cw[0:1] * u + cw[1:2] * u1 + cw[2:3] * u2
        du = cw[0:1] * dy + cw[1:2] * dybuf[pl.ds(1, tm), :] + cw[2:3] * dybuf[pl.ds(2, tm), :]
        dcw = jnp.concatenate([jnp.sum(dy * u, axis=0, keepdims=True), jnp.sum(dy * u1, axis=0, keepdims=True),
                               jnp.sum(dy * u2, axis=0, keepdims=True), jnp.zeros((HALO - 3, CONV_WIDTH), F32)], axis=0)
        _accumulate(dcw_ref, first, dcw)

        dz_ref[:, Z_Q[0]:Z_Q[1]] = dzq.astype(BF16)
        dz_ref[:, Z_KV[0]:Z_KV[1]] = dzkv.astype(BF16)
        dz_ref[:, Z_GB[0]:Z_GB[1]] = (dcv * y).astype(BF16)
        dz_ref[:, Z_GC[0]:Z_GC[1]] = (du * xin).astype(BF16)
        dz_ref[:, Z_XIN[0]:Z_XIN[1]] = (du * gc).astype(BF16)
        dz_ref[:, Z_KPE[0]:Z_KPE[1]] = dkpe.astype(BF16)
        dx, dg = _rms_bwd(xv, gmix[...], _mm_nt(dz_ref[...], win[...]))
        dx_ref[...] = dx1_ref[...] + dx
        _accumulate(dgmix_ref, first, dg)

    prev_halo = lambda n: pl.BlockSpec((HALO, n), lambda i: (jnp.maximum(i * hb_per_tile - 1, 0), 0))
    next_halo = lambda n: pl.BlockSpec((HALO, n), lambda i: (jnp.minimum((i + 1) * hb_per_tile, n_halo - 1), 0))
    consts, const_specs = _operands([w["g_mix"], w["w_in"], w["g_q_lat"], w["w_uq"], w["g_kv_lat"], w["w_ukv"],
                                     w["g_qn"], w["g_kn"], w["conv_w"], tabs["gm"]])
    in_specs = ([_rows(tm, d), _rows(tm, Z_COLS), prev_halo(Z_COLS), next_halo(Z_COLS), _rows(tm, d), _rows(tm, hp),
                 _rows(tm, hp), _rows(tm, hp), _rows(tm, CONV_WIDTH), next_halo(CONV_WIDTH)]
                + const_specs + [_rows(tm, PAIR * HEAD_PAD)] * 2)
    out_shape = (jax.ShapeDtypeStruct((t_len, d), F32), jax.ShapeDtypeStruct((t_len, Z_COLS), BF16),
                 jax.ShapeDtypeStruct((t_len, d), BF16), jax.ShapeDtypeStruct((t_len, Q_LORA), BF16),
                 jax.ShapeDtypeStruct((t_len, KV_LORA), BF16), jax.ShapeDtypeStruct((t_len, hp), BF16),
                 jax.ShapeDtypeStruct((t_len, hp), BF16),
                 jax.ShapeDtypeStruct((1, d), F32), jax.ShapeDtypeStruct((1, Q_LORA), F32),
                 jax.ShapeDtypeStruct((1, KV_LORA), F32), jax.ShapeDtypeStruct((1, LANES), F32),
                 jax.ShapeDtypeStruct((1, LANES), F32), jax.ShapeDtypeStruct((HALO, CONV_WIDTH), F32))
    out_specs = tuple(_rows(tm, s.shape[1]) for s in out_shape[:7]) + tuple(_whole(s.shape) for s in out_shape[7:])
    return pl.pallas_call(body, name="front_bwd", grid=(nt,), in_specs=in_specs, out_specs=out_specs, out_shape=out_shape,
                          scratch_shapes=[pltpu.VMEM((tm + HALO, CONV_WIDTH), F32), pltpu.VMEM((tm + HALO, CONV_WIDTH), F32)],
                          compiler_params=_cparams(("arbitrary",)))(
                              x, z, z, z, dx1, dqf, dkf, dkv_in, dconv, dconv, *consts, tabs["cos"], tabs["sin"])


def _wgrad(a, b, shard_cols=None, out_dtype=BF16):
    t_len, kk = a.shape
    nn = b.shape[1]
    tk = min(kk, WGRAD_TILE)
    tn = next(c for c in range(min(nn, WGRAD_TILE), 0, -LANES) if nn % c == 0 and c % (shard_cols or LANES) == 0)
    tt = min(t_len, WGRAD_TOKENS)
    nt = t_len // tt
    per_block = tn // shard_cols if shard_cols else 1

    def body(a_ref, b_ref, o_ref, acc):
        t = pl.program_id(2)

        @pl.when(t == 0)
        def _():
            acc[...] = jnp.zeros_like(acc)
        acc[...] += _mm_tn(a_ref[...].astype(BF16), b_ref[...].astype(BF16))

        @pl.when(t == nt - 1)
        def _():
            if shard_cols:
                for s in range(per_block):
                    o_ref[s] = acc[:, s * shard_cols:(s + 1) * shard_cols].astype(out_dtype)
            else:
                o_ref[...] = acc[...].astype(out_dtype)

    if shard_cols:
        out_shape = jax.ShapeDtypeStruct((nn // shard_cols, kk, shard_cols), out_dtype)
        out_spec = pl.BlockSpec((per_block, tk, shard_cols), lambda i, j, t: (j, i, 0))
    else:
        out_shape = jax.ShapeDtypeStruct((kk, nn), out_dtype)
        out_spec = pl.BlockSpec((tk, tn), lambda i, j, t: (i, j))
    return pl.pallas_call(body, name="wgrad", grid=(kk // tk, nn // tn, nt),
                          in_specs=[pl.BlockSpec((tt, tk), lambda i, j, t: (t, i)),
                                    pl.BlockSpec((tt, tn), lambda i, j, t: (t, j))],
                          out_specs=out_spec, out_shape=out_shape, scratch_shapes=[pltpu.VMEM((tk, tn), F32)],
                          compiler_params=_cparams(("parallel", "parallel", "arbitrary")))(a, b)


def _my_place():
    return lax.axis_index("x"), lax.axis_index("y"), lax.axis_index("c")


def _any_specs(n):
    return [pl.BlockSpec(memory_space=pl.ANY)] * n


def _all_gather(blocks):
    n = len(blocks)

    def body(*refs):
        start, forward, finish = _gather_phases(refs[:n], refs[n:2 * n], *refs[2 * n:])
        start()
        forward()
        finish()

    return pl.pallas_call(body, name="all_gather", out_shape=_gather_out_shape(blocks), in_specs=_any_specs(n),
                          out_specs=tuple(_any_specs(n)), scratch_shapes=_gather_semaphores(n))(*blocks)


def _gather_out_shape(blocks):
    return tuple(jax.ShapeDtypeStruct((N_DEV,) + b.shape, b.dtype) for b in blocks)


def _gather_semaphores(n):
    return [pltpu.SemaphoreType.DMA((n, 7)), pltpu.SemaphoreType.DMA((n, 7)), pltpu.SemaphoreType.DMA((n,))]


def _gather_phases(x_refs, out_refs, send_sems, recv_sems, local_sems):
    n = len(x_refs)
    x, y, c = _my_place()
    me, sibling = (x, y, c), (x, y, 1 - c)
    chips = [(1 - x, y), (x, 1 - y), (1 - x, 1 - y)]

    def slot(a, px, py, pc):
        return out_refs[a].at[4 * px + 2 * py + pc]

    def copy(a, k, blk, to, src=None):
        return pltpu.make_async_remote_copy(src_ref=slot(a, *blk) if src is None else src, dst_ref=slot(a, *blk),
                                            send_sem=send_sems.at[a, k], recv_sem=recv_sems.at[a, k],
                                            device_id=to, device_id_type=MESH)

    def own(a):
        return pltpu.make_async_copy(x_refs[a], slot(a, *me), local_sems.at[a])

    def first_hop(a):
        return [copy(a, 0, me, sibling, src=x_refs[a])] + [copy(a, 1 + j, me, (*chip, c), src=x_refs[a])
                                                           for j, chip in enumerate(chips)]

    def passed_on(a):
        return [copy(a, 4 + j, (*chip, c), sibling) for j, chip in enumerate(chips)]

    def start():
        for a in range(n):
            own(a).start()
        for a in range(n):
            for cp in first_hop(a):
                cp.start()

    def forward():
        for j, chip in enumerate(chips):
            for a in range(n):
                copy(a, 1 + j, (*chip, c), me).wait_recv()
                passed_on(a)[j].start()

    def finish():
        for a in range(n):
            copy(a, 0, sibling, me).wait_recv()
        for j, chip in enumerate(chips):
            for a in range(n):
                copy(a, 4 + j, (*chip, 1 - c), me).wait_recv()
        for a in range(n):
            for cp in first_hop(a) + passed_on(a):
                cp.wait_send()
            own(a).wait()

    return start, forward, finish


def _scatter_exchange(parts, landed, layers):
    n = len(parts)

    def body(*refs):
        start, finish = _scatter_phases(refs[:n], refs[2 * n:3 * n], *refs[3 * n:], layers)
        start()
        finish()

    return pl.pallas_call(body, name="scatter_exchange", out_shape=_same_shapes(landed), in_specs=_any_specs(2 * n),
                          out_specs=tuple(_any_specs(n)), scratch_shapes=_scatter_semaphores(n),
                          input_output_aliases={n + a: a for a in range(n)})(*parts, *landed)


def _same_shapes(arrays):
    return tuple(jax.ShapeDtypeStruct(a.shape, a.dtype) for a in arrays)


def _scatter_semaphores(n):
    return [pltpu.SemaphoreType.DMA((n, N_DEV - 1)), pltpu.SemaphoreType.DMA((n, N_DEV - 1)), pltpu.SemaphoreType.DMA((n,))]


def _scatter_phases(part_refs, landed_refs, send_sems, recv_sems, local_sems, layers):
    n = len(part_refs)
    x, y, c = _my_place()
    flips = [(0, 0, 1), (1, 0, 0), (0, 1, 0), (1, 1, 0), (1, 0, 1), (0, 1, 1), (1, 1, 1)]
    peers = [((1 - x) if fx else x, (1 - y) if fy else y, (1 - c) if fc else c) for fx, fy, fc in flips]
    my_k = 4 * x + 2 * y + c

    def index(peer):
        return 4 * peer[0] + 2 * peer[1] + peer[2]

    def send(a, r):
        return pltpu.make_async_remote_copy(src_ref=part_refs[a].at[index(peers[r])], dst_ref=landed_refs[a].at[my_k, layers[a]],
                                            send_sem=send_sems.at[a, r], recv_sem=recv_sems.at[a, r],
                                            device_id=peers[r], device_id_type=MESH)

    def arrival(a, r):
        return pltpu.make_async_remote_copy(src_ref=part_refs[a].at[my_k], dst_ref=landed_refs[a].at[index(peers[r]), layers[a]],
                                            send_sem=send_sems.at[a, r], recv_sem=recv_sems.at[a, r],
                                            device_id=peers[r], device_id_type=MESH)

    def own(a):
        return pltpu.make_async_copy(part_refs[a].at[my_k], landed_refs[a].at[my_k, layers[a]], local_sems.at[a])

    def start():
        for a in range(n):
            own(a).start()
        for r in range(len(peers)):
            for a in range(n):
                send(a, r).start()

    def finish():
        for r in range(len(peers)):
            for a in range(n):
                arrival(a, r).wait_recv()
        for r in range(len(peers)):
            for a in range(n):
                send(a, r).wait_send()
        for a in range(n):
            own(a).wait()

    return start, finish


def _row_block(rows):
    return ROW_BLOCK if rows % ROW_BLOCK == 0 else rows


def _sum_leading(parts):
    n_part, shape = parts.shape[0], parts.shape[1:]
    cols = shape[-1]
    p2 = parts.reshape(n_part, -1, cols)
    rows = p2.shape[1]
    rb = _row_block(rows)

    def body(p_ref, o_ref):
        acc = p_ref[0].astype(F32)
        for k in range(1, n_part):
            acc = acc + p_ref[k].astype(F32)
        o_ref[...] = acc

    out = pl.pallas_call(body, name="sum_leading", grid=(rows // rb,),
                         in_specs=[pl.BlockSpec((n_part, rb, cols), lambda i: (0, i, 0))], out_specs=_rows(rb, cols),
                         out_shape=jax.ShapeDtypeStruct((rows, cols), F32), compiler_params=_cparams(("parallel",)))(p2)
    return out.reshape(shape)


def _adamw(w, g, m, v):
    shape = w.shape
    two_d = (shape[0] * shape[1], shape[2]) if len(shape) == 3 else shape
    rows, cols = two_d
    rb = _row_block(rows)

    def body(w_ref, g_ref, m_ref, v_ref, d_ref, nm_ref, nv_ref):
        gv = g_ref[...]
        nm = ADAM_B1 * m_ref[...] + (1.0 - ADAM_B1) * gv
        nv = ADAM_B2 * v_ref[...] + (1.0 - ADAM_B2) * jnp.square(gv)
        m_hat = nm / (1.0 - ADAM_B1 ** ADAM_STEP)
        v_hat = nv / (1.0 - ADAM_B2 ** ADAM_STEP)
        d_ref[...] = -ADAM_LR * (m_hat / (jnp.sqrt(v_hat) + ADAM_EPS) + ADAM_WD * w_ref[...])
        nm_ref[...] = nm
        nv_ref[...] = nv

    spec = _rows(rb, cols)
    out = jax.ShapeDtypeStruct(two_d, F32)
    res = pl.pallas_call(body, name="adamw", grid=(rows // rb,), in_specs=[spec] * 4, out_specs=(spec,) * 3,
                         out_shape=(out,) * 3, compiler_params=_cparams(("parallel",)))(
                             *(a.reshape(two_d) for a in (w, g, m, v)))
    return tuple(a.reshape(shape) for a in res)


def _rope_tables(positions):
    t_len = positions.shape[0]
    inv_freq = 1.0 / (ROPE_THETA ** (jnp.arange(0, QK_ROPE, 2, dtype=F32) / QK_ROPE))
    ang = positions.astype(F32)[:, None] * inv_freq
    c, s = jnp.cos(ang), jnp.sin(ang)
    one, zero = jnp.ones((t_len, QK_NOPE), F32), jnp.zeros((t_len, QK_NOPE), F32)
    cos = jnp.concatenate([one, c, c, one[:, :LANES - QK_HEAD]], axis=1)
    sin = jnp.concatenate([zero, -s, s, zero[:, :LANES - QK_HEAD]], axis=1)
    idx = jnp.arange(PAIR * HEAD_PAD)
    lane, head = idx % HEAD_PAD, idx // HEAD_PAD
    grp = jnp.where(lane < QK_NOPE, 0, jnp.where(lane < QK_HEAD, 1, 2)) + 3 * head
    val = jnp.where(lane < QK_NOPE, 1.0 / QK_NOPE, jnp.where(lane < QK_HEAD, 1.0 / QK_ROPE, 0.0))
    gm = jnp.where(grp[:, None] == grp[None, :], val[None, :], 0.0).astype(BF16)
    return {"cos": jnp.concatenate([cos] * PAIR, axis=1), "sin": jnp.concatenate([sin] * PAIR, axis=1), "gm": gm}


def _head_gain(g_nope, g_rope):
    one = jnp.concatenate([g_nope, g_rope, jnp.zeros((HEAD_PAD - QK_HEAD,), F32)])
    return jnp.concatenate([one] * PAIR).reshape(1, PAIR * HEAD_PAD)


def _head_pairs(w):
    return jnp.concatenate([w[k::PAIR] for k in range(PAIR)], axis=2)


def _padded_w_in(shards):
    natural = jnp.concatenate([shards[k] for k in range(N_DEV)], axis=1)
    o2, o3 = Q_LORA + KV_LORA, Q_LORA + KV_LORA + QK_ROPE
    zeros = jnp.zeros((natural.shape[0], QK_NOPE), natural.dtype)
    return jnp.concatenate([natural[:, :o2], natural[:, o3:], zeros, natural[:, o2:o3], zeros[:, :LANES - QK_HEAD]], axis=1)


def _w_in_grad_shards(d_in):
    o2 = Q_LORA + KV_LORA
    nat = jnp.concatenate([d_in[:, :o2], d_in[:, Z_KPE[0] + QK_NOPE:Z_KPE[0] + QK_HEAD], d_in[:, o2:Z_XIN[1]]], axis=1)
    width = nat.shape[1] // N_DEV
    return jnp.stack([nat[:, k * width:(k + 1) * width] for k in range(N_DEV)])


def kernel(x, p, positions, g_mix, w_in, g_q_lat, w_uq, g_kv_lat, w_ukv, g_qn_nope, g_qn_rope, g_kn_nope, g_kn_rope, conv_w, g_out_attn, g_out_conv, w_o, g_mlp, w_up, w_down, g_ple, w_ple_gate, w_ple, loss_target, m_g_mix, m_w_in, m_g_q_lat, m_w_uq, m_g_kv_lat, m_w_ukv, m_g_qn_nope, m_g_qn_rope, m_g_kn_nope, m_g_kn_rope, m_conv_w, m_g_out_attn, m_g_out_conv, m_w_o, m_g_mlp, m_w_up, m_w_down, m_g_ple, m_w_ple_gate, m_w_ple, v_g_mix, v_w_in, v_g_q_lat, v_w_uq, v_g_kv_lat, v_w_ukv, v_g_qn_nope, v_g_qn_rope, v_g_kn_nope, v_g_kn_rope, v_conv_w, v_g_out_attn, v_g_out_conv, v_w_o, v_g_mlp, v_w_up, v_w_down, v_g_ple, v_w_ple_gate, v_w_ple):
    given = dict(locals())
    weights = {n: given[n] for n in WEIGHT_NAMES}
    gains = {n: given[n] for n in GAIN_NAMES}
    depth = w_in.shape[0]
    xs, target = x[0], loss_target[0]
    d_model = xs.shape[1]
    uq_cols = w_uq.shape[2]
    n_taps = conv_w.shape[1]

    mat_names = [n for n in SHARD_NAMES if n != "conv_w"]
    local = [weights[n].astype(BF16) for n in mat_names]
    local[1] = jnp.pad(local[1], ((0, 0), (0, 0), (0, HEAD_PAD - uq_cols)))
    local = dict(zip(mat_names, local))
    front_names = ("w_in", "w_uq", "w_ukv")
    first = _all_gather([local[n][0] for n in front_names] + [conv_w])
    conv_full = jnp.transpose(first[-1], (1, 2, 0, 3)).reshape(depth, n_taps, -1)
    tabs = _rope_tables(positions[0])

    def front_weights(layer, full):
        lw = {n: gains[n][layer].reshape(1, -1) for n in GAIN_NAMES}
        lw.update({"w_in": _padded_w_in(full["w_in"]), "w_uq": _head_pairs(full["w_uq"]),
                   "w_ukv": _head_pairs(full["w_ukv"]),
                   "conv_w": jnp.pad(conv_full[layer], ((0, HALO - n_taps), (0, 0))),
                   "g_qn": _head_gain(g_qn_nope[layer], g_qn_rope[layer]),
                   "g_kn": _head_gain(g_kn_nope[layer], g_kn_rope[layer])})
        return lw

    def rest_weights(full):
        return {"w_ple": full["w_ple"], "w_up": full["w_up"], "w_down": full["w_down"],
                "w_o": full["w_o"].reshape(d_model, d_model), "w_ple_gate": full["w_ple_gate"].reshape(d_model, d_model)}

    saved, layer_w = [], []
    cur = xs
    gathered = dict(zip(front_names, first[:-1]))
    for layer in range(depth):
        w = front_weights(layer, gathered)
        z, qf, kf, kv, conv = _front_fwd(cur, w, tabs)
        wanted = [(n, 0) for n in mat_names if n not in front_names] if layer == 0 else []
        wanted += [(n, layer + 1) for n in mat_names] if layer + 1 < depth else []
        coming = {}
        if wanted:
            attn, lse, got = _attn_fwd(qf, kf, kv, gather=[(local[n], at) for n, at in wanted])
            for (n, at), g in zip(wanted, got):
                if at == layer:
                    gathered[n] = g
                else:
                    coming[n] = g
        else:
            attn, lse = _attn_fwd(qf, kf, kv)
        w.update(rest_weights(gathered))
        layer_w.append(w)
        gathered = coming
        x1 = _mix_out_fwd(cur, attn, conv, w)
        x2 = _mlp_fwd(x1, w)
        x3 = _ple_fwd(x2, p[layer, 0], w)
        saved.append(dict(x=cur, z=z, qf=qf, kf=kf, kv=kv, conv=conv, attn=attn, lse=lse, x1=x1, x2=x2))
        cur = x3

    sq, dx = _loss_and_grad(cur, target)
    loss = lax.psum(0.5 / d_model * sq[0, 0], ("x", "y", "c"))

    landed = {n: lax.empty((N_DEV, depth) + weights[n].shape[1:], BF16) for n in SHARD_NAMES}
    gain_grads = [None] * depth
    late = {}
    for layer in reversed(range(depth)):
        w, s = layer_w[layer], saved[layer]
        pl_in = p[layer, 0]
        dx2, de, h3, dpre, dg_ple = _ple_bwd(dx, s["x2"], pl_in, w)
        dx1, r, da, h2, dg_mlp = _mlp_bwd(dx2, s["x1"], w)
        mixed, dattn, dconv, dg_oa, dg_oc = _mix_out_bwd(dx1, s["attn"], s["conv"], w)
        sending = {"w_o": (_wgrad(mixed, dx1).reshape((N_DEV,) + w_o.shape[1:]), layer),
                   "w_up": (_wgrad(h2, da, shard_cols=w_up.shape[2]), layer),
                   "w_down": (_wgrad(r, dx2).reshape((N_DEV,) + w_down.shape[1:]), layer),
                   "w_ple_gate": (_wgrad(h3, dpre).reshape((N_DEV,) + w_ple_gate.shape[1:]), layer),
                   "w_ple": (_wgrad(pl_in, de, shard_cols=w_ple.shape[2]), layer), **late}
        names = list(sending)
        dqf, dkf, dkv, got = _attn_bwd(s["qf"], s["kf"], s["kv"], s["attn"], dattn, s["lse"],
                                       scatter=([sending[n][0] for n in names], [landed[n] for n in names],
                                                [sending[n][1] for n in names]))
        landed.update(zip(names, got))
        (dx0, dz, hb, qn, kvn, dqr, dkvr, dg_mix, dg_q, dg_kv, dg_qn, dg_kn, dcw) = _front_bwd(
            s["x"], s["z"], dx1, dqf, dkf, dkv, dconv, w, tabs)
        late = {"w_in": (_w_in_grad_shards(_wgrad(hb, dz, out_dtype=F32)).astype(BF16), layer),
                "w_uq": (_wgrad(qn, dqr, shard_cols=HEAD_PAD)[..., :uq_cols], layer),
                "w_ukv": (_wgrad(kvn, dkvr, shard_cols=HEAD_PAD), layer),
                "conv_w": (jnp.transpose(dcw[:n_taps].reshape(n_taps, N_DEV, -1), (1, 0, 2)).astype(BF16), layer)}
        gain_grads[layer] = jnp.concatenate([
            dg_mix[0], dg_q[0], dg_kv[0], dg_qn[0, :QK_NOPE], dg_qn[0, QK_NOPE:QK_HEAD], dg_kn[0, :QK_NOPE],
            dg_kn[0, QK_NOPE:QK_HEAD], dg_oa[0], dg_oc[0], dg_mlp[0], dg_ple[0]])
        dx = dx0
    names = list(late)
    landed.update(zip(names, _scatter_exchange([late[n][0] for n in names], [landed[n] for n in names],
                                               [late[n][1] for n in names])))
    grads = {n: _sum_leading(landed[n]) for n in SHARD_NAMES}

    gg = jnp.stack(gain_grads)
    gg_rows = -(-gg.size // (HALO * LANES)) * HALO
    gg_pad = jnp.pad(gg.reshape(-1), (0, gg_rows * LANES - gg.size)).reshape(gg_rows, LANES)
    gg_sum = _sum_leading(_all_gather([gg_pad])[0]).reshape(-1)[:gg.size].reshape(gg.shape)
    off = 0
    for n in GAIN_NAMES:
        width = gains[n].shape[1]
        grads[n] = gg_sum[:, off:off + width]
        off += width

    deltas, new_m, new_v = {}, {}, {}
    for n in WEIGHT_NAMES:
        deltas[n], new_m[n], new_v[n] = _adamw(weights[n], grads[n], given["m_" + n], given["v_" + n])
    return (loss, dx[None], *[grads[n] for n in WEIGHT_NAMES], *[deltas[n] for n in WEIGHT_NAMES],
            *[new_m[n] for n in WEIGHT_NAMES], *[new_v[n] for n in WEIGHT_NAMES])
```

```python
import jax
import jax.numpy as jnp
from jax import lax
from jax.experimental import pallas as pl
from jax.experimental.pallas import tpu as pltpu

F32 = jnp.float32
BF16 = jnp.bfloat16
MESH = pl.DeviceIdType.MESH

N_HEADS = 8
QK_NOPE = 64
QK_ROPE = 32
QK_HEAD = QK_NOPE + QK_ROPE
V_HEAD = 64
HEAD_PAD = 128
PAIR = 2
ATTN_SCALE = QK_HEAD ** -0.5
Q_LORA = 384
KV_LORA = 256
CONV_WIDTH = 512
ATTN_WIDTH = N_HEADS * V_HEAD
ROPE_THETA = 10000.0
EPS = 1e-6
ADAM_LR, ADAM_B1, ADAM_B2, ADAM_EPS, ADAM_WD, ADAM_STEP = 0.001, 0.9, 0.999, 1e-08, 0.01, 10

Z_Q = (0, 384)
Z_KV = (384, 640)
Z_GB = (640, 1152)
Z_GC = (1152, 1664)
Z_XIN = (1664, 2176)
Z_KPE = (2176, 2304)
Z_COLS = 2304

N_DEV = 8
LANES = 128
V7X_VMEM_LIMIT = 52 * 1024 * 1024
TOKEN_TILE = 256
ATTN_BLOCK = 256
ATTN_FWD_BLOCK = 512
ATTN_Q_SUB = 2
ATTN_KV_SUB = 2
ROW_BLOCK = 512
WGRAD_TOKENS = 2048
WGRAD_TILE = 1024
HALO = 8

GAIN_NAMES = ("g_mix", "g_q_lat", "g_kv_lat", "g_qn_nope", "g_qn_rope", "g_kn_nope", "g_kn_rope",
              "g_out_attn", "g_out_conv", "g_mlp", "g_ple")
SHARD_NAMES = ("w_in", "w_uq", "w_ukv", "conv_w", "w_o", "w_up", "w_down", "w_ple_gate", "w_ple")
WEIGHT_NAMES = ("g_mix", "w_in", "g_q_lat", "w_uq", "g_kv_lat", "w_ukv", "g_qn_nope", "g_qn_rope", "g_kn_nope",
                "g_kn_rope", "conv_w", "g_out_attn", "g_out_conv", "w_o", "g_mlp", "w_up", "w_down", "g_ple",
                "w_ple_gate", "w_ple")


def _cparams(semantics=None):
    return pltpu.CompilerParams(dimension_semantics=semantics, vmem_limit_bytes=V7X_VMEM_LIMIT)


def _mm(a, b):
    return jnp.dot(a, b, preferred_element_type=F32)


def _mm_nt(a, b):
    return lax.dot_general(a, b, (((1,), (1,)), ((), ())), preferred_element_type=F32)


def _mm_tn(a, b):
    return lax.dot_general(a, b, (((0,), (0,)), ((), ())), preferred_element_type=F32)


def _rms(x, g):
    r = lax.rsqrt(jnp.mean(x * x, axis=-1, keepdims=True) + EPS)
    return (x * r) * g


def _rms_bwd(x, g, dy):
    r = lax.rsqrt(jnp.mean(x * x, axis=-1, keepdims=True) + EPS)
    xh = x * r
    dg = jnp.sum(dy * xh, axis=0, keepdims=True)
    dyg = dy * g
    dx = r * (dyg - xh * jnp.mean(dyg * xh, axis=-1, keepdims=True))
    return dx, dg


def _group_mean(t, gm):
    hi = t.astype(BF16)
    lo = (t - hi.astype(F32)).astype(BF16)
    return _mm(hi, gm) + _mm(lo, gm)


def _swap_rope_halves(x, lane):
    half = QK_ROPE // 2
    swapped = jnp.where(lane < QK_NOPE + half, pltpu.roll(x, x.shape[1] - half, 1), pltpu.roll(x, half, 1))
    return jnp.where((lane >= QK_NOPE) & (lane < QK_HEAD), swapped, 0.0)


def _qk_fwd(x, g, cos, sin, gm, lane):
    r = lax.rsqrt(_group_mean(x * x, gm) + EPS)
    n = (x * r) * g
    return n * cos + _swap_rope_halves(n, lane) * sin


def _qk_bwd(x, g, dy, cos, sin, gm, lane):
    r = lax.rsqrt(_group_mean(x * x, gm) + EPS)
    xh = x * r
    dn = dy * cos + _swap_rope_halves(dy * sin, lane)
    dg = jnp.sum(dn * xh, axis=0, keepdims=True)
    dng = dn * g
    dx = r * (dng - xh * _group_mean(dng * xh, gm))
    return dx, dg


def _rows(tm, n):
    return pl.BlockSpec((tm, n), lambda i: (i, 0))


def _whole(shape):
    zeros = (0,) * len(shape)
    return pl.BlockSpec(shape, lambda i: zeros)


def _operands(arrays):
    return list(arrays), [_whole(a.shape) for a in arrays]


def _accumulate(ref, first, value):
    @pl.when(first)
    def _():
        ref[...] = jnp.zeros_like(ref)
    ref[...] += value


def _front_fwd(x, w, tabs):
    t_len, d = x.shape
    tm = min(TOKEN_TILE, t_len)
    hp = N_HEADS * HEAD_PAD

    def body(x_ref, gmix, win, gq, wuq, gkv, wukv, gqn, gkn, cw_ref, gm_ref, cos_ref, sin_ref,
             z_ref, qf_ref, kf_ref, kv_ref, conv_ref, ubuf):
        i = pl.program_id(0)
        h = _rms(x_ref[...], gmix[...])
        z = _mm(h.astype(BF16), win[...])
        z_ref[...] = z
        qnb = _rms(z[:, Z_Q[0]:Z_Q[1]], gq[...]).astype(BF16)
        kvb = _rms(z[:, Z_KV[0]:Z_KV[1]], gkv[...]).astype(BF16)
        kpe = z[:, Z_KPE[0]:Z_KPE[1]]
        kpe = jnp.concatenate([kpe] * PAIR, axis=1)
        cos, sin, gm = cos_ref[...], sin_ref[...], gm_ref[...]
        lane = lax.broadcasted_iota(jnp.int32, (tm, PAIR * HEAD_PAD), 1) & (HEAD_PAD - 1)
        for pr in range(N_HEADS // PAIR):
            sl = slice(pr * PAIR * HEAD_PAD, (pr + 1) * PAIR * HEAD_PAD)
            qf_ref[:, sl] = (_qk_fwd(_mm(qnb, wuq[pr]), gqn[...], cos, sin, gm, lane) * ATTN_SCALE).astype(BF16)
            kv = _mm(kvb, wukv[pr])
            kv_ref[:, sl] = jnp.where(lane < QK_NOPE, jnp.where(lane == 0, 1.0, 0.0), kv).astype(BF16)
            kf_ref[:, sl] = _qk_fwd(jnp.where(lane < QK_NOPE, kv, 0.0) + kpe, gkn[...], cos, sin, gm, lane).astype(BF16)
        u = z[:, Z_GC[0]:Z_GC[1]] * z[:, Z_XIN[0]:Z_XIN[1]]

        @pl.when(i == 0)
        def _():
            ubuf[0:HALO, :] = jnp.zeros((HALO, CONV_WIDTH), F32)
        ubuf[HALO:HALO + tm, :] = u
        cw = cw_ref[...]
        y = cw[0:1] * u + cw[1:2] * ubuf[pl.ds(HALO - 1, tm), :] + cw[2:3] * ubuf[pl.ds(HALO - 2, tm), :]
        conv_ref[...] = z[:, Z_GB[0]:Z_GB[1]] * y
        ubuf[0:HALO, :] = u[tm - HALO:tm, :]

    consts, const_specs = _operands([w["g_mix"], w["w_in"], w["g_q_lat"], w["w_uq"], w["g_kv_lat"], w["w_ukv"],
                                     w["g_qn"], w["g_kn"], w["conv_w"], tabs["gm"]])
    out_shape = (jax.ShapeDtypeStruct((t_len, Z_COLS), F32), jax.ShapeDtypeStruct((t_len, hp), BF16),
                 jax.ShapeDtypeStruct((t_len, hp), BF16), jax.ShapeDtypeStruct((t_len, hp), BF16),
                 jax.ShapeDtypeStruct((t_len, CONV_WIDTH), F32))
    return pl.pallas_call(body, name="front_fwd", grid=(t_len // tm,),
                          in_specs=[_rows(tm, d)] + const_specs + [_rows(tm, PAIR * HEAD_PAD)] * 2,
                          out_specs=tuple(_rows(tm, s.shape[1]) for s in out_shape), out_shape=out_shape,
                          scratch_shapes=[pltpu.VMEM((tm + HALO, CONV_WIDTH), F32)],
                          compiler_params=_cparams(("arbitrary",)))(x, *consts, tabs["cos"], tabs["sin"])


def _attn_fwd(qf, kf, kv, gather=None):
    t_len = qf.shape[0]
    blk = min(ATTN_FWD_BLOCK, t_len)
    nb = t_len // blk
    n_sub = ATTN_Q_SUB
    bq = blk // n_sub
    chains = [(hh, a) for hh in range(2) for a in range(n_sub)]

    def body(q_ref, k_ref, kv_ref, o_ref, lse_ref):
        lane = lax.broadcasted_iota(jnp.int32, (bq, LANES), 1)
        row = lax.broadcasted_iota(jnp.int32, (bq, blk), 0)
        col = lax.broadcasted_iota(jnp.int32, (bq, blk), 1)

        def head_cols(hh):
            return slice(hh * HEAD_PAD, (hh + 1) * HEAD_PAD)

        def softmax_step(s, kvv, state, first_row=None):
            m, acc = state
            if first_row is not None:
                s = jnp.where(col <= row + first_row, s, -jnp.inf)
            m_new = jnp.maximum(m, jnp.max(s, axis=-1, keepdims=True))
            p = jnp.exp(s - m_new)
            acc = jnp.exp(m - m_new) * acc + _mm(p.astype(BF16), kvv)
            return m_new, acc

        def finish(state):
            m, acc = state
            l = jnp.sum(jnp.where(lane == 0, acc, 0.0), axis=-1, keepdims=True)
            return acc / l, jnp.broadcast_to(m + jnp.log(l), (bq, LANES))

        def qblock(i, carry):
            start = pl.multiple_of(i * blk, blk)
            rows = [pl.ds(pl.multiple_of(start + a * bq, bq), bq) for a in range(n_sub)]
            qs = {(hh, a): q_ref[rows[a], head_cols(hh)] for hh, a in chains}

            def scores(j):
                ks = pl.ds(pl.multiple_of(j * blk, blk), blk)
                return tuple(_mm_nt(qs[hh, a], k_ref[ks, head_cols(hh)]) for hh, a in chains)

            def kstep(j, carried, diagonal=False):
                ss, states = carried
                ss_next = ss if diagonal else scores(j + 1)
                ks = pl.ds(pl.multiple_of(j * blk, blk), blk)
                new = tuple(softmax_step(s, kv_ref[ks, head_cols(hh)], st, a * bq if diagonal else None)
                            for (hh, a), s, st in zip(chains, ss, states))
                return ss_next, new

            init = (jnp.full((bq, 1), -jnp.inf, F32), jnp.zeros((bq, LANES), F32))
            carried = lax.fori_loop(0, i, kstep, (scores(0), (init,) * len(chains)))
            _, states = kstep(i, carried, diagonal=True)
            for a in range(n_sub):
                (o0, lse0), (o1, lse1) = finish(states[chains.index((0, a))]), finish(states[chains.index((1, a))])
                o_ref[rows[a], :] = jnp.where(lane < V_HEAD, pltpu.roll(o0, V_HEAD, 1), o1)
                lse_ref[rows[a], head_cols(0)] = lse0
                lse_ref[rows[a], head_cols(1)] = lse1
            return carry

        lax.fori_loop(0, nb, qblock, 0)

    n_steps = N_HEADS // 2
    n_gather = len(gather) if gather else 0

    def hosting_body(*refs):
        q_ref, k_ref, kv_ref = refs[:3]
        x_refs = [r.at[layer] for r, (_, layer) in zip(refs[3:3 + n_gather], gather)]
        o_ref, lse_ref = refs[3 + n_gather:5 + n_gather]
        start, forward, finish = _gather_phases(x_refs, refs[5 + n_gather:5 + 2 * n_gather], *refs[5 + 2 * n_gather:])
        step = pl.program_id(0)
        pl.when(step == 0)(start)
        pl.when(step == n_steps // 2)(forward)
        body(q_ref, k_ref, kv_ref, o_ref, lse_ref)
        pl.when(step == n_steps - 1)(finish)

    heads = pl.BlockSpec((t_len, 2 * HEAD_PAD), lambda h: (0, h))
    pair = pl.BlockSpec((t_len, 2 * V_HEAD), lambda h: (0, h))
    out_shape = (jax.ShapeDtypeStruct((t_len, ATTN_WIDTH), F32), jax.ShapeDtypeStruct((t_len, N_HEADS * LANES), F32))
    if not gather:
        return pl.pallas_call(body, name="attn_fwd", grid=(n_steps,), in_specs=[heads, heads, heads],
                              out_specs=(pair, heads), out_shape=out_shape,
                              compiler_params=_cparams(("arbitrary",)))(qf, kf, kv)
    shards = [s[layer] for s, layer in gather]
    res = pl.pallas_call(hosting_body, name="attn_fwd_gather", grid=(n_steps,),
                         in_specs=[heads, heads, heads] + _any_specs(n_gather),
                         out_specs=(pair, heads) + tuple(_any_specs(n_gather)),
                         out_shape=out_shape + _gather_out_shape(shards), scratch_shapes=_gather_semaphores(n_gather),
                         compiler_params=_cparams(("arbitrary",)))(qf, kf, kv, *[s for s, _ in gather])
    return res[0], res[1], res[2:]


def _mix_out_fwd(x, attn, conv, w):
    t_len, d = x.shape
    tm = min(TOKEN_TILE, t_len)

    def body(x_ref, a_ref, c_ref, goa, goc, wo, x1_ref):
        mixed = jnp.concatenate([_rms(a_ref[...], goa[...]), _rms(c_ref[...], goc[...])], axis=1)
        x1_ref[...] = x_ref[...] + _mm(mixed.astype(BF16), wo[...])

    consts, const_specs = _operands([w["g_out_attn"], w["g_out_conv"], w["w_o"]])
    return pl.pallas_call(body, name="mix_out_fwd", grid=(t_len // tm,),
                          in_specs=[_rows(tm, d), _rows(tm, ATTN_WIDTH), _rows(tm, CONV_WIDTH)] + const_specs,
                          out_specs=_rows(tm, d), out_shape=jax.ShapeDtypeStruct((t_len, d), F32),
                          compiler_params=_cparams(("parallel",)))(x, attn, conv, *consts)


def _mlp_fwd(x1, w):
    t_len, d = x1.shape
    tm = min(TOKEN_TILE, t_len)

    def body(x_ref, g, wup, wdn, x2_ref):
        x1v = x_ref[...]
        hb = _rms(x1v, g[...]).astype(BF16)
        acc = x1v
        for k in range(N_DEV):
            a = jnp.maximum(_mm(hb, wup[k]), 0.0)
            acc = acc + _mm((a * a).astype(BF16), wdn[k])
        x2_ref[...] = acc

    consts, const_specs = _operands([w["g_mlp"], w["w_up"], w["w_down"]])
    return pl.pallas_call(body, name="mlp_fwd", grid=(t_len // tm,), in_specs=[_rows(tm, d)] + const_specs,
                          out_specs=_rows(tm, d), out_shape=jax.ShapeDtypeStruct((t_len, d), F32),
                          compiler_params=_cparams(("parallel",)))(x1, *consts)


def _ple_fwd(x2, p, w):
    t_len, d = x2.shape
    tm = min(TOKEN_TILE, t_len)

    def body(x_ref, p_ref, g, wg, wp, x3_ref):
        x2v = x_ref[...]
        gate = jax.nn.sigmoid(_mm(_rms(x2v, g[...]).astype(BF16), wg[...]))
        pb = p_ref[...].astype(BF16)
        e = jnp.concatenate([_mm(pb, wp[k]) for k in range(N_DEV)], axis=1)
        x3_ref[...] = x2v + gate * e

    consts, const_specs = _operands([w["g_ple"], w["w_ple_gate"], w["w_ple"]])
    return pl.pallas_call(body, name="ple_fwd", grid=(t_len // tm,),
                          in_specs=[_rows(tm, d), _rows(tm, p.shape[1])] + const_specs, out_specs=_rows(tm, d),
                          out_shape=jax.ShapeDtypeStruct((t_len, d), F32),
                          compiler_params=_cparams(("parallel",)))(x2, p, *consts)


def _loss_and_grad(y, target):
    t_len, d = y.shape
    tm = min(TOKEN_TILE, t_len)

    def body(y_ref, t_ref, sq_ref, dy_ref):
        err = y_ref[...] - t_ref[...]
        dy_ref[...] = err / d
        total = jnp.sum(jnp.sum(err * err, axis=0, keepdims=True), axis=1, keepdims=True)
        _accumulate(sq_ref, pl.program_id(0) == 0, jnp.broadcast_to(total, (HALO, LANES)))

    return pl.pallas_call(body, name="loss_grad", grid=(t_len // tm,), in_specs=[_rows(tm, d), _rows(tm, d)],
                          out_specs=(_whole((HALO, LANES)), _rows(tm, d)),
                          out_shape=(jax.ShapeDtypeStruct((HALO, LANES), F32), jax.ShapeDtypeStruct((t_len, d), F32)),
                          compiler_params=_cparams(("arbitrary",)))(y, target)


def _ple_bwd(dx3, x2, p, w):
    t_len, d = x2.shape
    tm = min(TOKEN_TILE, t_len)

    def body(dx3_ref, x_ref, p_ref, g, wg, wp, dx2_ref, de_ref, h3_ref, dpre_ref, dg_ref):
        x2v, dx3v = x_ref[...], dx3_ref[...]
        hb = _rms(x2v, g[...]).astype(BF16)
        h3_ref[...] = hb
        gate = jax.nn.sigmoid(_mm(hb, wg[...]))
        pb = p_ref[...].astype(BF16)
        e = jnp.concatenate([_mm(pb, wp[k]) for k in range(N_DEV)], axis=1)
        de_ref[...] = (dx3v * gate).astype(BF16)
        dpre = ((dx3v * e) * gate * (1.0 - gate)).astype(BF16)
        dpre_ref[...] = dpre
        dx, dg = _rms_bwd(x2v, g[...], _mm_nt(dpre, wg[...]))
        dx2_ref[...] = dx3v + dx
        _accumulate(dg_ref, pl.program_id(0) == 0, dg)

    consts, const_specs = _operands([w["g_ple"], w["w_ple_gate"], w["w_ple"]])
    out_shape = (jax.ShapeDtypeStruct((t_len, d), F32), jax.ShapeDtypeStruct((t_len, d), BF16),
                 jax.ShapeDtypeStruct((t_len, d), BF16), jax.ShapeDtypeStruct((t_len, d), BF16),
                 jax.ShapeDtypeStruct((1, d), F32))
    return pl.pallas_call(body, name="ple_bwd", grid=(t_len // tm,),
                          in_specs=[_rows(tm, d), _rows(tm, d), _rows(tm, p.shape[1])] + const_specs,
                          out_specs=(_rows(tm, d),) * 4 + (_whole((1, d)),), out_shape=out_shape,
                          compiler_params=_cparams(("arbitrary",)))(dx3, x2, p, *consts)


def _mlp_bwd(dx2, x1, w):
    t_len, d = x1.shape
    tm = min(TOKEN_TILE, t_len)
    fc = w["w_up"].shape[2]
    ff = N_DEV * fc

    def body(dx2_ref, x_ref, g, wup, wdn, dx1_ref, r_ref, da_ref, h2_ref, dg_ref):
        x1v, dx2v = x_ref[...], dx2_ref[...]
        hb = _rms(x1v, g[...]).astype(BF16)
        h2_ref[...] = hb
        dxb = dx2v.astype(BF16)
        dh = jnp.zeros((tm, d), F32)
        for k in range(N_DEV):
            a = jnp.maximum(_mm(hb, wup[k]), 0.0)
            r_ref[:, k * fc:(k + 1) * fc] = (a * a).astype(BF16)
            da = (_mm_nt(dxb, wdn[k]) * (2.0 * a)).astype(BF16)
            da_ref[:, k * fc:(k + 1) * fc] = da
            dh = dh + _mm_nt(da, wup[k])
        dx, dg = _rms_bwd(x1v, g[...], dh)
        dx1_ref[...] = dx2v + dx
        _accumulate(dg_ref, pl.program_id(0) == 0, dg)

    consts, const_specs = _operands([w["g_mlp"], w["w_up"], w["w_down"]])
    out_shape = (jax.ShapeDtypeStruct((t_len, d), F32), jax.ShapeDtypeStruct((t_len, ff), BF16),
                 jax.ShapeDtypeStruct((t_len, ff), BF16), jax.ShapeDtypeStruct((t_len, d), BF16),
                 jax.ShapeDtypeStruct((1, d), F32))
    return pl.pallas_call(body, name="mlp_bwd", grid=(t_len // tm,), in_specs=[_rows(tm, d), _rows(tm, d)] + const_specs,
                          out_specs=(_rows(tm, d), _rows(tm, ff), _rows(tm, ff), _rows(tm, d), _whole((1, d))),
                          out_shape=out_shape, compiler_params=_cparams(("arbitrary",)))(dx2, x1, *consts)


def _mix_out_bwd(dx1, attn, conv, w):
    t_len, d = dx1.shape
    tm = min(TOKEN_TILE, t_len)

    def body(dx1_ref, a_ref, c_ref, goa, goc, wo, mixed_ref, da_ref, dc_ref, dgoa_ref, dgoc_ref):
        av, cv = a_ref[...], c_ref[...]
        mixed_ref[...] = jnp.concatenate([_rms(av, goa[...]), _rms(cv, goc[...])], axis=1).astype(BF16)
        dmixed = _mm_nt(dx1_ref[...].astype(BF16), wo[...])
        da, dga = _rms_bwd(av, goa[...], dmixed[:, :ATTN_WIDTH])
        dc, dgc = _rms_bwd(cv, goc[...], dmixed[:, ATTN_WIDTH:])
        da_ref[...] = da
        dc_ref[...] = dc
        first = pl.program_id(0) == 0
        _accumulate(dgoa_ref, first, dga)
        _accumulate(dgoc_ref, first, dgc)

    consts, const_specs = _operands([w["g_out_attn"], w["g_out_conv"], w["w_o"]])
    out_shape = (jax.ShapeDtypeStruct((t_len, d), BF16), jax.ShapeDtypeStruct((t_len, ATTN_WIDTH), F32),
                 jax.ShapeDtypeStruct((t_len, CONV_WIDTH), F32), jax.ShapeDtypeStruct((1, ATTN_WIDTH), F32),
                 jax.ShapeDtypeStruct((1, CONV_WIDTH), F32))
    out_specs = (_rows(tm, d), _rows(tm, ATTN_WIDTH), _rows(tm, CONV_WIDTH), _whole((1, ATTN_WIDTH)),
                 _whole((1, CONV_WIDTH)))
    return pl.pallas_call(body, name="mix_out_bwd", grid=(t_len // tm,),
                          in_specs=[_rows(tm, d), _rows(tm, ATTN_WIDTH), _rows(tm, CONV_WIDTH)] + const_specs,
                          out_specs=out_specs, out_shape=out_shape,
                          compiler_params=_cparams(("arbitrary",)))(dx1, attn, conv, *consts)


def _attn_bwd(qf, kf, kv, o, do, lse, scatter):
    t_len = qf.shape[0]
    blk = min(ATTN_BLOCK, t_len)
    nb = t_len // blk
    n_sub = ATTN_KV_SUB if nb % ATTN_KV_SUB == 0 else 1
    reps = blk // LANES

    def body(q_ref, k_ref, kv_ref, o_ref, do_ref, lse_ref, dq_ref, dk_ref, dkv_ref, delta_ref, dob_ref):
        hd = pl.program_id(0)
        lane = lax.broadcasted_iota(jnp.int32, (blk, LANES), 1)
        even = (lane * 0 + hd % 2) == 0
        mine = jnp.where(lane < V_HEAD, 0, 1) == hd % 2
        row = lax.broadcasted_iota(jnp.int32, (blk, blk), 0)
        col = lax.broadcasted_iota(jnp.int32, (blk, blk), 1)
        dq_ref[...] = jnp.zeros_like(dq_ref)

        def prepare(i, carry):
            qs = pl.ds(pl.multiple_of(i * blk, blk), blk)
            dov = do_ref[qs, :]
            prod = jnp.where(mine, dov * o_ref[qs, :], 0.0)
            delta_ref[qs, :] = jnp.broadcast_to(jnp.sum(prod, axis=-1, keepdims=True), (blk, LANES))
            moved = jnp.where(even, pltpu.roll(dov, V_HEAD, 1), dov)
            dob_ref[qs, :] = jnp.where(lane >= V_HEAD, moved, 0.0).astype(BF16)
            return carry
        lax.fori_loop(0, nb, prepare, 0)

        def kvblock(jj, carry):
            base = jj * n_sub
            kss = [pl.ds(pl.multiple_of((base + a) * blk, blk), blk) for a in range(n_sub)]
            k = [k_ref[ks, :] for ks in kss]
            kvv = [kv_ref[ks, :] for ks in kss]

            def products(i):
                qs = pl.ds(pl.multiple_of(i * blk, blk), blk)
                q, dob = q_ref[qs, :], dob_ref[qs, :]
                return tuple((_mm_nt(q, k[a]), _mm_nt(dob, kvv[a])) for a in range(n_sub))

            def qstep(i, raw, accs, kinds):
                qs = pl.ds(pl.multiple_of(i * blk, blk), blk)
                q = q_ref[qs, :]
                dob = dob_ref[qs, :]
                lse_t = jnp.concatenate([lse_ref[qs, :]] * reps, axis=1)
                delta_t = jnp.concatenate([delta_ref[qs, :]] * reps, axis=1)
                new, dq_add = [], None
                for a in range(n_sub):
                    if kinds[a] is None:
                        new.append(accs[a])
                        continue
                    dk_acc, dv_acc = accs[a]
                    s, dp = raw[a]
                    if kinds[a]:
                        s = jnp.where(col <= row, s, -jnp.inf)
                    p = jnp.exp(s - lse_t)
                    ds = (p * (dp - delta_t)).astype(BF16)
                    new.append((dk_acc + _mm_tn(ds, q), dv_acc + _mm_tn(p.astype(BF16), dob)))
                    part = _mm(ds, k[a])
                    dq_add = part if dq_add is None else dq_add + part
                dq_ref[qs, :] += dq_add
                return tuple(new)

            zero = jnp.zeros((blk, LANES), F32)
            accs = ((zero, zero),) * n_sub
            for b in range(n_sub):
                accs = qstep(base + b, products(base + b), accs, tuple((a == b) if a <= b else None for a in range(n_sub)))

            def pipelined(i, carried):
                raw, acc = carried
                return products(jnp.minimum(i + 1, nb - 1)), qstep(i, raw, acc, (False,) * n_sub)

            first = base + n_sub
            _, accs = lax.fori_loop(first, nb, pipelined, (products(jnp.minimum(first, nb - 1)), accs))
            for a in range(n_sub):
                dk_ref[kss[a], :] = accs[a][0]
                dkv_ref[kss[a], :] = accs[a][1]
            return carry
        lax.fori_loop(0, nb // n_sub, kvblock, 0)

    n_sc = len(scatter[0])

    def hosting_body(*refs):
        ins, rest = refs[:6], refs[6 + 2 * n_sc:]
        parts = refs[6:6 + n_sc]
        outs, landed, sems, scratch = rest[:3], rest[3:3 + n_sc], rest[3 + n_sc:6 + n_sc], rest[6 + n_sc:]
        start, finish = _scatter_phases(parts, landed, *sems, scatter[2])
        hd = pl.program_id(0)
        pl.when(hd == 0)(start)
        body(*ins, *outs, *scratch)
        pl.when(hd == N_HEADS - 1)(finish)

    head = pl.BlockSpec((t_len, HEAD_PAD), lambda h: (0, h))
    pair = pl.BlockSpec((t_len, 2 * V_HEAD), lambda h: (0, h // 2))
    out = jax.ShapeDtypeStruct((t_len, N_HEADS * HEAD_PAD), F32)
    vmem_scratch = [pltpu.VMEM((t_len, LANES), F32), pltpu.VMEM((t_len, LANES), BF16)]
    res = pl.pallas_call(hosting_body, name="attn_bwd_scatter", grid=(N_HEADS,),
                         in_specs=[head, head, head, pair, pair, head] + _any_specs(2 * n_sc),
                         out_specs=(head, head, head) + tuple(_any_specs(n_sc)),
                         out_shape=(out, out, out) + _same_shapes(scatter[1]),
                         scratch_shapes=_scatter_semaphores(n_sc) + vmem_scratch,
                         input_output_aliases={6 + n_sc + a: 3 + a for a in range(n_sc)},
                         compiler_params=_cparams(("arbitrary",)))(qf, kf, kv, o, do, lse, *scatter[0], *scatter[1])
    return res[0], res[1], res[2], res[3:]


def _front_bwd(x, z, dx1, dqf, dkf, dkv_in, dconv, w, tabs):
    t_len, d = x.shape
    tm = min(TOKEN_TILE, t_len)
    nt = t_len // tm
    hb_per_tile = tm // HALO
    n_halo = t_len // HALO
    hp = N_HEADS * HEAD_PAD

    def body(x_ref, z_ref, zp_ref, zn_ref, dx1_ref, dqf_ref, dkf_ref, dkv_ref, dc_ref, dcn_ref,
             gmix, win, gq, wuq, gkv, wukv, gqn, gkn, cw_ref, gm_ref, cos_ref, sin_ref,
             dx_ref, dz_ref, h_ref, qn_ref, kvn_ref, dqr_ref, dkvr_ref,
             dgmix_ref, dgq_ref, dgkv_ref, dgqn_ref, dgkn_ref, dcw_ref, ubuf, dybuf):
        i = pl.program_id(0)
        first = i == 0
        xv, zv = x_ref[...], z_ref[...]
        hb = _rms(xv, gmix[...]).astype(BF16)
        h_ref[...] = hb
        zq, zkv = zv[:, Z_Q[0]:Z_Q[1]], zv[:, Z_KV[0]:Z_KV[1]]
        qnb = _rms(zq, gq[...]).astype(BF16)
        qn_ref[...] = qnb
        kvb = _rms(zkv, gkv[...]).astype(BF16)
        kvn_ref[...] = kvb
        kpe = zv[:, Z_KPE[0]:Z_KPE[1]]
        kpe = jnp.concatenate([kpe] * PAIR, axis=1)
        cos, sin, gm = cos_ref[...], sin_ref[...], gm_ref[...]
        width = PAIR * HEAD_PAD
        lane = lax.broadcasted_iota(jnp.int32, (tm, width), 1) & (HEAD_PAD - 1)
        is_nope = lane < QK_NOPE
        is_rope = (lane >= QK_NOPE) & (lane < QK_HEAD)
        dkpe = jnp.zeros((tm, width), F32)
        dgqn = jnp.zeros((1, width), F32)
        dgkn = jnp.zeros((1, width), F32)
        dqn = jnp.zeros((tm, Q_LORA), F32)
        dkvn = jnp.zeros((tm, KV_LORA), F32)
        for pr in range(N_HEADS // PAIR):
            sl = slice(pr * width, (pr + 1) * width)
            dxq, dg = _qk_bwd(_mm(qnb, wuq[pr]), gqn[...], dqf_ref[:, sl] * ATTN_SCALE, cos, sin, gm, lane)
            dxq = dxq.astype(BF16)
            dqr_ref[:, sl] = dxq
            dqn = dqn + _mm_nt(dxq, wuq[pr])
            dgqn = dgqn + dg
            k_raw = jnp.where(is_nope, _mm(kvb, wukv[pr]), 0.0) + kpe
            dxk, dg = _qk_bwd(k_raw, gkn[...], dkf_ref[:, sl], cos, sin, gm, lane)
            dkv = jnp.where(is_nope, dxk, dkv_ref[:, sl]).astype(BF16)
            dkvr_ref[:, sl] = dkv
            dkvn = dkvn + _mm_nt(dkv, wukv[pr])
            dkpe = dkpe + jnp.where(is_rope, dxk, 0.0)
            dgkn = dgkn + dg
        dkpe = dkpe[:, :HEAD_PAD] + dkpe[:, HEAD_PAD:]
        _accumulate(dgqn_ref, first, dgqn[:, :HEAD_PAD] + dgqn[:, HEAD_PAD:])
        _accumulate(dgkn_ref, first, dgkn[:, :HEAD_PAD] + dgkn[:, HEAD_PAD:])
        dzq, dg = _rms_bwd(zq, gq[...], dqn)
        _accumulate(dgq_ref, first, dg)
        dzkv, dg = _rms_bwd(zkv, gkv[...], dkvn)
        _accumulate(dgkv_ref, first, dg)

        gb, gc, xin = zv[:, Z_GB[0]:Z_GB[1]], zv[:, Z_GC[0]:Z_GC[1]], zv[:, Z_XIN[0]:Z_XIN[1]]
        u = gc * xin
        dcv = dc_ref[...]
        dy = dcv * gb
        zp, zn = zp_ref[...], zn_ref[...]
        ubuf[0:HALO, :] = (zp[:, Z_GC[0]:Z_GC[1]] * zp[:, Z_XIN[0]:Z_XIN[1]]) * jnp.where(first, 0.0, 1.0)
        ubuf[HALO:HALO + tm, :] = u
        dybuf[0:tm, :] = dy
        dybuf[tm:tm + HALO, :] = (dcn_ref[...] * zn[:, Z_GB[0]:Z_GB[1]]) * jnp.where(i == nt - 1, 0.0, 1.0)
        cw = cw_ref[...]
        u1, u2 = ubuf[pl.ds(HALO - 1, tm), :], ubuf[pl.ds(HALO - 2, tm), :]
        y = cw[0:1] * u + cw[1:2] * u1 + cw[2:3] * u2
        du = cw[0:1] * dy + cw[1:2] * dybuf[pl.ds(1, tm), :] + cw[2:3] * dybuf[pl.ds(2, tm), :]
        dcw = jnp.concatenate([jnp.sum(dy * u, axis=0, keepdims=True), jnp.sum(dy * u1, axis=0, keepdims=True),
                               jnp.sum(dy * u2, axis=0, keepdims=True), jnp.zeros((HALO - 3, CONV_WIDTH), F32)], axis=0)
        _accumulate(dcw_ref, first, dcw)

        dz_ref[:, Z_Q[0]:Z_Q[1]] = dzq.astype(BF16)
        dz_ref[:, Z_KV[0]:Z_KV[1]] = dzkv.astype(BF16)
        dz_ref[:, Z_GB[0]:Z_GB[1]] = (dcv * y).astype(BF16)
        dz_ref[:, Z_GC[0]:Z_GC[1]] = (du * xin).astype(BF16)
        dz_ref[:, Z_XIN[0]:Z_XIN[1]] = (du * gc).astype(BF16)
        dz_ref[:, Z_KPE[0]:Z_KPE[1]] = dkpe.astype(BF16)
        dx, dg = _rms_bwd(xv, gmix[...], _mm_nt(dz_ref[...], win[...]))
        dx_ref[...] = dx1_ref[...] + dx
        _accumulate(dgmix_ref, first, dg)

    prev_halo = lambda n: pl.BlockSpec((HALO, n), lambda i: (jnp.maximum(i * hb_per_tile - 1, 0), 0))
    next_halo = lambda n: pl.BlockSpec((HALO, n), lambda i: (jnp.minimum((i + 1) * hb_per_tile, n_halo - 1), 0))
    consts, const_specs = _operands([w["g_mix"], w["w_in"], w["g_q_lat"], w["w_uq"], w["g_kv_lat"], w["w_ukv"],
                                     w["g_qn"], w["g_kn"], w["conv_w"], tabs["gm"]])
    in_specs = ([_rows(tm, d), _rows(tm, Z_COLS), prev_halo(Z_COLS), next_halo(Z_COLS), _rows(tm, d), _rows(tm, hp),
                 _rows(tm, hp), _rows(tm, hp), _rows(tm, CONV_WIDTH), next_halo(CONV_WIDTH)]
                + const_specs + [_rows(tm, PAIR * HEAD_PAD)] * 2)
    out_shape = (jax.ShapeDtypeStruct((t_len, d), F32), jax.ShapeDtypeStruct((t_len, Z_COLS), BF16),
                 jax.ShapeDtypeStruct((t_len, d), BF16), jax.ShapeDtypeStruct((t_len, Q_LORA), BF16),
                 jax.ShapeDtypeStruct((t_len, KV_LORA), BF16), jax.ShapeDtypeStruct((t_len, hp), BF16),
                 jax.ShapeDtypeStruct((t_len, hp), BF16),
                 jax.ShapeDtypeStruct((1, d), F32), jax.ShapeDtypeStruct((1, Q_LORA), F32),
                 jax.ShapeDtypeStruct((1, KV_LORA), F32), jax.ShapeDtypeStruct((1, LANES), F32),
                 jax.ShapeDtypeStruct((1, LANES), F32), jax.ShapeDtypeStruct((HALO, CONV_WIDTH), F32))
    out_specs = tuple(_rows(tm, s.shape[1]) for s in out_shape[:7]) + tuple(_whole(s.shape) for s in out_shape[7:])
    return pl.pallas_call(body, name="front_bwd", grid=(nt,), in_specs=in_specs, out_specs=out_specs, out_shape=out_shape,
                          scratch_shapes=[pltpu.VMEM((tm + HALO, CONV_WIDTH), F32), pltpu.VMEM((tm + HALO, CONV_WIDTH), F32)],
                          compiler_params=_cparams(("arbitrary",)))(
                              x, z, z, z, dx1, dqf, dkf, dkv_in, dconv, dconv, *consts, tabs["cos"], tabs["sin"])


def _wgrad(a, b, shard_cols=None, out_dtype=BF16):
    t_len, kk = a.shape
    nn = b.shape[1]
    tk = min(kk, WGRAD_TILE)
    tn = next(c for c in range(min(nn, WGRAD_TILE), 0, -LANES) if nn % c == 0 and c % (shard_cols or LANES) == 0)
    tt = min(t_len, WGRAD_TOKENS)
    nt = t_len // tt
    per_block = tn // shard_cols if shard_cols else 1

    def body(a_ref, b_ref, o_ref, acc):
        t = pl.program_id(2)

        @pl.when(t == 0)
        def _():
            acc[...] = jnp.zeros_like(acc)
        acc[...] += _mm_tn(a_ref[...].astype(BF16), b_ref[...].astype(BF16))

        @pl.when(t == nt - 1)
        def _():
            if shard_cols:
                for s in range(per_block):
                    o_ref[s] = acc[:, s * shard_cols:(s + 1) * shard_cols].astype(out_dtype)
            else:
                o_ref[...] = acc[...].astype(out_dtype)

    if shard_cols:
        out_shape = jax.ShapeDtypeStruct((nn // shard_cols, kk, shard_cols), out_dtype)
        out_spec = pl.BlockSpec((per_block, tk, shard_cols), lambda i, j, t: (j, i, 0))
    else:
        out_shape = jax.ShapeDtypeStruct((kk, nn), out_dtype)
        out_spec = pl.BlockSpec((tk, tn), lambda i, j, t: (i, j))
    return pl.pallas_call(body, name="wgrad", grid=(kk // tk, nn // tn, nt),
                          in_specs=[pl.BlockSpec((tt, tk), lambda i, j, t: (t, i)),
                                    pl.BlockSpec((tt, tn), lambda i, j, t: (t, j))],
                          out_specs=out_spec, out_shape=out_shape, scratch_shapes=[pltpu.VMEM((tk, tn), F32)],
                          compiler_params=_cparams(("parallel", "parallel", "arbitrary")))(a, b)


def _my_place():
    return lax.axis_index("x"), lax.axis_index("y"), lax.axis_index("c")


def _any_specs(n):
    return [pl.BlockSpec(memory_space=pl.ANY)] * n


def _all_gather(blocks):
    n = len(blocks)

    def body(*refs):
        start, forward, finish = _gather_phases(refs[:n], refs[n:2 * n], *refs[2 * n:])
        start()
        forward()
        finish()

    return pl.pallas_call(body, name="all_gather", out_shape=_gather_out_shape(blocks), in_specs=_any_specs(n),
                          out_specs=tuple(_any_specs(n)), scratch_shapes=_gather_semaphores(n))(*blocks)


def _gather_out_shape(blocks):
    return tuple(jax.ShapeDtypeStruct((N_DEV,) + b.shape, b.dtype) for b in blocks)


def _gather_semaphores(n):
    return [pltpu.SemaphoreType.DMA((n, 7)), pltpu.SemaphoreType.DMA((n, 7)), pltpu.SemaphoreType.DMA((n,))]


def _gather_phases(x_refs, out_refs, send_sems, recv_sems, local_sems):
    n = len(x_refs)
    x, y, c = _my_place()
    me, sibling = (x, y, c), (x, y, 1 - c)
    chips = [(1 - x, y), (x, 1 - y), (1 - x, 1 - y)]

    def slot(a, px, py, pc):
        return out_refs[a].at[4 * px + 2 * py + pc]

    def copy(a, k, blk, to, src=None):
        return pltpu.make_async_remote_copy(src_ref=slot(a, *blk) if src is None else src, dst_ref=slot(a, *blk),
                                            send_sem=send_sems.at[a, k], recv_sem=recv_sems.at[a, k],
                                            device_id=to, device_id_type=MESH)

    def own(a):
        return pltpu.make_async_copy(x_refs[a], slot(a, *me), local_sems.at[a])

    def first_hop(a):
        return [copy(a, 0, me, sibling, src=x_refs[a])] + [copy(a, 1 + j, me, (*chip, c), src=x_refs[a])
                                                           for j, chip in enumerate(chips)]

    def passed_on(a):
        return [copy(a, 4 + j, (*chip, c), sibling) for j, chip in enumerate(chips)]

    def start():
        for a in range(n):
            own(a).start()
        for a in range(n):
            for cp in first_hop(a):
                cp.start()

    def forward():
        for j, chip in enumerate(chips):
            for a in range(n):
                copy(a, 1 + j, (*chip, c), me).wait_recv()
                passed_on(a)[j].start()

    def finish():
        for a in range(n):
            copy(a, 0, sibling, me).wait_recv()
        for j, chip in enumerate(chips):
            for a in range(n):
                copy(a, 4 + j, (*chip, 1 - c), me).wait_recv()
        for a in range(n):
            for cp in first_hop(a) + passed_on(a):
                cp.wait_send()
            own(a).wait()

    return start, forward, finish


def _scatter_exchange(parts, landed, layers):
    n = len(parts)

    def body(*refs):
        start, finish = _scatter_phases(refs[:n], refs[2 * n:3 * n], *refs[3 * n:], layers)
        start()
        finish()

    return pl.pallas_call(body, name="scatter_exchange", out_shape=_same_shapes(landed), in_specs=_any_specs(2 * n),
                          out_specs=tuple(_any_specs(n)), scratch_shapes=_scatter_semaphores(n),
                          input_output_aliases={n + a: a for a in range(n)})(*parts, *landed)


def _same_shapes(arrays):
    return tuple(jax.ShapeDtypeStruct(a.shape, a.dtype) for a in arrays)


def _scatter_semaphores(n):
    return [pltpu.SemaphoreType.DMA((n, N_DEV - 1)), pltpu.SemaphoreType.DMA((n, N_DEV - 1)), pltpu.SemaphoreType.DMA((n,))]


def _scatter_phases(part_refs, landed_refs, send_sems, recv_sems, local_sems, layers):
    n = len(part_refs)
    x, y, c = _my_place()
    flips = [(0, 0, 1), (1, 0, 0), (0, 1, 0), (1, 1, 0), (1, 0, 1), (0, 1, 1), (1, 1, 1)]
    peers = [((1 - x) if fx else x, (1 - y) if fy else y, (1 - c) if fc else c) for fx, fy, fc in flips]
    my_k = 4 * x + 2 * y + c

    def index(peer):
        return 4 * peer[0] + 2 * peer[1] + peer[2]

    def send(a, r):
        return pltpu.make_async_remote_copy(src_ref=part_refs[a].at[index(peers[r])], dst_ref=landed_refs[a].at[my_k, layers[a]],
                                            send_sem=send_sems.at[a, r], recv_sem=recv_sems.at[a, r],
                                            device_id=peers[r], device_id_type=MESH)

    def arrival(a, r):
        return pltpu.make_async_remote_copy(src_ref=part_refs[a].at[my_k], dst_ref=landed_refs[a].at[index(peers[r]), layers[a]],
                                            send_sem=send_sems.at[a, r], recv_sem=recv_sems.at[a, r],
                                            device_id=peers[r], device_id_type=MESH)

    def own(a):
        return pltpu.make_async_copy(part_refs[a].at[my_k], landed_refs[a].at[my_k, layers[a]], local_sems.at[a])

    def start():
        for a in range(n):
            own(a).start()
        for r in range(len(peers)):
            for a in range(n):
                send(a, r).start()

    def finish():
        for r in range(len(peers)):
            for a in range(n):
                arrival(a, r).wait_recv()
        for r in range(len(peers)):
            for a in range(n):
                send(a, r).wait_send()
        for a in range(n):
            own(a).wait()

    return start, finish


def _row_block(rows):
    return ROW_BLOCK if rows % ROW_BLOCK == 0 else rows


def _sum_leading(parts):
    n_part, shape = parts.shape[0], parts.shape[1:]
    cols = shape[-1]
    p2 = parts.reshape(n_part, -1, cols)
    rows = p2.shape[1]
    rb = _row_block(rows)

    def body(p_ref, o_ref):
        acc = p_ref[0].astype(F32)
        for k in range(1, n_part):
            acc = acc + p_ref[k].astype(F32)
        o_ref[...] = acc

    out = pl.pallas_call(body, name="sum_leading", grid=(rows // rb,),
                         in_specs=[pl.BlockSpec((n_part, rb, cols), lambda i: (0, i, 0))], out_specs=_rows(rb, cols),
                         out_shape=jax.ShapeDtypeStruct((rows, cols), F32), compiler_params=_cparams(("parallel",)))(p2)
    return out.reshape(shape)


def _adamw(w, g, m, v):
    shape = w.shape
    two_d = (shape[0] * shape[1], shape[2]) if len(shape) == 3 else shape
    rows, cols = two_d
    rb = _row_block(rows)

    def body(w_ref, g_ref, m_ref, v_ref, d_ref, nm_ref, nv_ref):
        gv = g_ref[...]
        nm = ADAM_B1 * m_ref[...] + (1.0 - ADAM_B1) * gv
        nv = ADAM_B2 * v_ref[...] + (1.0 - ADAM_B2) * jnp.square(gv)
        m_hat = nm / (1.0 - ADAM_B1 ** ADAM_STEP)
        v_hat = nv / (1.0 - ADAM_B2 ** ADAM_STEP)
        d_ref[...] = -ADAM_LR * (m_hat / (jnp.sqrt(v_hat) + ADAM_EPS) + ADAM_WD * w_ref[...])
        nm_ref[...] = nm
        nv_ref[...] = nv

    spec = _rows(rb, cols)
    out = jax.ShapeDtypeStruct(two_d, F32)
    res = pl.pallas_call(body, name="adamw", grid=(rows // rb,), in_specs=[spec] * 4, out_specs=(spec,) * 3,
                         out_shape=(out,) * 3, compiler_params=_cparams(("parallel",)))(
                             *(a.reshape(two_d) for a in (w, g, m, v)))
    return tuple(a.reshape(shape) for a in res)


def _rope_tables(positions):
    t_len = positions.shape[0]
    inv_freq = 1.0 / (ROPE_THETA ** (jnp.arange(0, QK_ROPE, 2, dtype=F32) / QK_ROPE))
    ang = positions.astype(F32)[:, None] * inv_freq
    c, s = jnp.cos(ang), jnp.sin(ang)
    one, zero = jnp.ones((t_len, QK_NOPE), F32), jnp.zeros((t_len, QK_NOPE), F32)
    cos = jnp.concatenate([one, c, c, one[:, :LANES - QK_HEAD]], axis=1)
    sin = jnp.concatenate([zero, -s, s, zero[:, :LANES - QK_HEAD]], axis=1)
    idx = jnp.arange(PAIR * HEAD_PAD)
    lane, head = idx % HEAD_PAD, idx // HEAD_PAD
    grp = jnp.where(lane < QK_NOPE, 0, jnp.where(lane < QK_HEAD, 1, 2)) + 3 * head
    val = jnp.where(lane < QK_NOPE, 1.0 / QK_NOPE, jnp.where(lane < QK_HEAD, 1.0 / QK_ROPE, 0.0))
    gm = jnp.where(grp[:, None] == grp[None, :], val[None, :], 0.0).astype(BF16)
    return {"cos": jnp.concatenate([cos] * PAIR, axis=1), "sin": jnp.concatenate([sin] * PAIR, axis=1), "gm": gm}


def _head_gain(g_nope, g_rope):
    one = jnp.concatenate([g_nope, g_rope, jnp.zeros((HEAD_PAD - QK_HEAD,), F32)])
    return jnp.concatenate([one] * PAIR).reshape(1, PAIR * HEAD_PAD)


def _head_pairs(w):
    return jnp.concatenate([w[k::PAIR] for k in range(PAIR)], axis=2)


def _padded_w_in(shards):
    natural = jnp.concatenate([shards[k] for k in range(N_DEV)], axis=1)
    o2, o3 = Q_LORA + KV_LORA, Q_LORA + KV_LORA + QK_ROPE
    zeros = jnp.zeros((natural.shape[0], QK_NOPE), natural.dtype)
    return jnp.concatenate([natural[:, :o2], natural[:, o3:], zeros, natural[:, o2:o3], zeros[:, :LANES - QK_HEAD]], axis=1)


def _w_in_grad_shards(d_in):
    o2 = Q_LORA + KV_LORA
    nat = jnp.concatenate([d_in[:, :o2], d_in[:, Z_KPE[0] + QK_NOPE:Z_KPE[0] + QK_HEAD], d_in[:, o2:Z_XIN[1]]], axis=1)
    width = nat.shape[1] // N_DEV
    return jnp.stack([nat[:, k * width:(k + 1) * width] for k in range(N_DEV)])


def kernel(x, p, positions, g_mix, w_in, g_q_lat, w_uq, g_kv_lat, w_ukv, g_qn_nope, g_qn_rope, g_kn_nope, g_kn_rope, conv_w, g_out_attn, g_out_conv, w_o, g_mlp, w_up, w_down, g_ple, w_ple_gate, w_ple, loss_target, m_g_mix, m_w_in, m_g_q_lat, m_w_uq, m_g_kv_lat, m_w_ukv, m_g_qn_nope, m_g_qn_rope, m_g_kn_nope, m_g_kn_rope, m_conv_w, m_g_out_attn, m_g_out_conv, m_w_o, m_g_mlp, m_w_up, m_w_down, m_g_ple, m_w_ple_gate, m_w_ple, v_g_mix, v_w_in, v_g_q_lat, v_w_uq, v_g_kv_lat, v_w_ukv, v_g_qn_nope, v_g_qn_rope, v_g_kn_nope, v_g_kn_rope, v_conv_w, v_g_out_attn, v_g_out_conv, v_w_o, v_g_mlp, v_w_up, v_w_down, v_g_ple, v_w_ple_gate, v_w_ple):
    given = dict(locals())
    weights = {n: given[n] for n in WEIGHT_NAMES}
    gains = {n: given[n] for n in GAIN_NAMES}
    depth = w_in.shape[0]
    xs, target = x[0], loss_target[0]
    d_model = xs.shape[1]
    uq_cols = w_uq.shape[2]
    n_taps = conv_w.shape[1]

    mat_names = [n for n in SHARD_NAMES if n != "conv_w"]
    local = [weights[n].astype(BF16) for n in mat_names]
    local[1] = jnp.pad(local[1], ((0, 0), (0, 0), (0, HEAD_PAD - uq_cols)))
    local = dict(zip(mat_names, local))
    front_names = ("w_in", "w_uq", "w_ukv")
    first = _all_gather([local[n][0] for n in front_names] + [conv_w])
    conv_full = jnp.transpose(first[-1], (1, 2, 0, 3)).reshape(depth, n_taps, -1)
    tabs = _rope_tables(positions[0])

    def front_weights(layer, full):
        lw = {n: gains[n][layer].reshape(1, -1) for n in GAIN_NAMES}
        lw.update({"w_in": _padded_w_in(full["w_in"]), "w_uq": _head_pairs(full["w_uq"]),
                   "w_ukv": _head_pairs(full["w_ukv"]),
                   "conv_w": jnp.pad(conv_full[layer], ((0, HALO - n_taps), (0, 0))),
                   "g_qn": _head_gain(g_qn_nope[layer], g_qn_rope[layer]),
                   "g_kn": _head_gain(g_kn_nope[layer], g_kn_rope[layer])})
        return lw

    def rest_weights(full):
        return {"w_ple": full["w_ple"], "w_up": full["w_up"], "w_down": full["w_down"],
                "w_o": full["w_o"].reshape(d_model, d_model), "w_ple_gate": full["w_ple_gate"].reshape(d_model, d_model)}

    saved, layer_w = [], []
    cur = xs
    gathered = dict(zip(front_names, first[:-1]))
    for layer in range(depth):
        w = front_weights(layer, gathered)
        z, qf, kf, kv, conv = _front_fwd(cur, w, tabs)
        wanted = [(n, 0) for n in mat_names if n not in front_names] if layer == 0 else []
        wanted += [(n, layer + 1) for n in mat_names] if layer + 1 < depth else []
        coming = {}
        if wanted:
            attn, lse, got = _attn_fwd(qf, kf, kv, gather=[(local[n], at) for n, at in wanted])
            for (n, at), g in zip(wanted, got):
                if at == layer:
                    gathered[n] = g
                else:
                    coming[n] = g
        else:
            attn, lse = _attn_fwd(qf, kf, kv)
        w.update(rest_weights(gathered))
        layer_w.append(w)
        gathered = coming
        x1 = _mix_out_fwd(cur, attn, conv, w)
        x2 = _mlp_fwd(x1, w)
        x3 = _ple_fwd(x2, p[layer, 0], w)
        saved.append(dict(x=cur, z=z, qf=qf, kf=kf, kv=kv, conv=conv, attn=attn, lse=lse, x1=x1, x2=x2))
        cur = x3

    sq, dx = _loss_and_grad(cur, target)
    loss = lax.psum(0.5 / d_model * sq[0, 0], ("x", "y", "c"))

    landed = {n: lax.empty((N_DEV, depth) + weights[n].shape[1:], BF16) for n in SHARD_NAMES}
    gain_grads = [None] * depth
    late = {}
    for layer in reversed(range(depth)):
        w, s = layer_w[layer], saved[layer]
        pl_in = p[layer, 0]
        dx2, de, h3, dpre, dg_ple = _ple_bwd(dx, s["x2"], pl_in, w)
        dx1, r, da, h2, dg_mlp = _mlp_bwd(dx2, s["x1"], w)
        mixed, dattn, dconv, dg_oa, dg_oc = _mix_out_bwd(dx1, s["attn"], s["conv"], w)
        sending = {"w_o": (_wgrad(mixed, dx1).reshape((N_DEV,) + w_o.shape[1:]), layer),
                   "w_up": (_wgrad(h2, da, shard_cols=w_up.shape[2]), layer),
                   "w_down": (_wgrad(r, dx2).reshape((N_DEV,) + w_down.shape[1:]), layer),
                   "w_ple_gate": (_wgrad(h3, dpre).reshape((N_DEV,) + w_ple_gate.shape[1:]), layer),
                   "w_ple": (_wgrad(pl_in, de, shard_cols=w_ple.shape[2]), layer), **late}
        names = list(sending)
        dqf, dkf, dkv, got = _attn_bwd(s["qf"], s["kf"], s["kv"], s["attn"], dattn, s["lse"],
                                       scatter=([sending[n][0] for n in names], [landed[n] for n in names],
                                                [sending[n][1] for n in names]))
        landed.update(zip(names, got))
        (dx0, dz, hb, qn, kvn, dqr, dkvr, dg_mix, dg_q, dg_kv, dg_qn, dg_kn, dcw) = _front_bwd(
            s["x"], s["z"], dx1, dqf, dkf, dkv, dconv, w, tabs)
        late = {"w_in": (_w_in_grad_shards(_wgrad(hb, dz, out_dtype=F32)).astype(BF16), layer),
                "w_uq": (_wgrad(qn, dqr, shard_cols=HEAD_PAD)[..., :uq_cols], layer),
                "w_ukv": (_wgrad(kvn, dkvr, shard_cols=HEAD_PAD), layer),
                "conv_w": (jnp.transpose(dcw[:n_taps].reshape(n_taps, N_DEV, -1), (1, 0, 2)).astype(BF16), layer)}
        gain_grads[layer] = jnp.concatenate([
            dg_mix[0], dg_q[0], dg_kv[0], dg_qn[0, :QK_NOPE], dg_qn[0, QK_NOPE:QK_HEAD], dg_kn[0, :QK_NOPE],
            dg_kn[0, QK_NOPE:QK_HEAD], dg_oa[0], dg_oc[0], dg_mlp[0], dg_ple[0]])
        dx = dx0
    names = list(late)
    landed.update(zip(names, _scatter_exchange([late[n][0] for n in names], [landed[n] for n in names],
                                               [late[n][1] for n in names])))
    grads = {n: _sum_leading(landed[n]) for n in SHARD_NAMES}

    gg = jnp.stack(gain_grads)
    gg_rows = -(-gg.size // (HALO * LANES)) * HALO
    gg_pad = jnp.pad(gg.reshape(-1), (0, gg_rows * LANES - gg.size)).reshape(gg_rows, LANES)
    gg_sum = _sum_leading(_all_gather([gg_pad])[0]).reshape(-1)[:gg.size].reshape(gg.shape)
    off = 0
    for n in GAIN_NAMES:
        width = gains[n].shape[1]
        grads[n] = gg_sum[:, off:off + width]
        off += width

    deltas, new_m, new_v = {}, {}, {}
    for n in WEIGHT_NAMES:
        deltas[n], new_m[n], new_v[n] = _adamw(weights[n], grads[n], given["m_" + n], given["v_" + n])
    return (loss, dx[None], *[grads[n] for n in WEIGHT_NAMES], *[deltas[n] for n in WEIGHT_NAMES],
            *[new_m[n] for n in WEIGHT_NAMES], *[new_v[n] for n in WEIGHT_NAMES])
```

```python
import jax
import jax.numpy as jnp
from jax import lax
from jax.experimental import pallas as pl
from jax.experimental.pallas import tpu as pltpu

F32 = jnp.float32
BF16 = jnp.bfloat16
MESH = pl.DeviceIdType.MESH

N_HEADS = 8
QK_NOPE = 64
QK_ROPE = 32
QK_HEAD = QK_NOPE + QK_ROPE
V_HEAD = 64
HEAD_PAD = 128
PAIR = 2
ATTN_SCALE = QK_HEAD ** -0.5
Q_LORA = 384
KV_LORA = 256
CONV_WIDTH = 512
ATTN_WIDTH = N_HEADS * V_HEAD
ROPE_THETA = 10000.0
EPS = 1e-6
ADAM_LR, ADAM_B1, ADAM_B2, ADAM_EPS, ADAM_WD, ADAM_STEP = 0.001, 0.9, 0.999, 1e-08, 0.01, 10

Z_Q = (0, 384)
Z_KV = (384, 640)
Z_GB = (640, 1152)
Z_GC = (1152, 1664)
Z_XIN = (1664, 2176)
Z_KPE = (2176, 2304)
Z_COLS = 2304

N_DEV = 8
LANES = 128
V7X_VMEM_LIMIT = 52 * 1024 * 1024
TOKEN_TILE = 256
ATTN_BLOCK = 256
ATTN_FWD_BLOCK = 512
ATTN_Q_SUB = 2
ATTN_KV_SUB = 2
ROW_BLOCK = 512
WGRAD_TOKENS = 2048
WGRAD_TILE = 1024
HALO = 8

GAIN_NAMES = ("g_mix", "g_q_lat", "g_kv_lat", "g_qn_nope", "g_qn_rope", "g_kn_nope", "g_kn_rope",
              "g_out_attn", "g_out_conv", "g_mlp", "g_ple")
SHARD_NAMES = ("w_in", "w_uq", "w_ukv", "conv_w", "w_o", "w_up", "w_down", "w_ple_gate", "w_ple")
WEIGHT_NAMES = ("g_mix", "w_in", "g_q_lat", "w_uq", "g_kv_lat", "w_ukv", "g_qn_nope", "g_qn_rope", "g_kn_nope",
                "g_kn_rope", "conv_w", "g_out_attn", "g_out_conv", "w_o", "g_mlp", "w_up", "w_down", "g_ple",
                "w_ple_gate", "w_ple")


def _cparams(semantics=None):
    return pltpu.CompilerParams(dimension_semantics=semantics, vmem_limit_bytes=V7X_VMEM_LIMIT)


def _mm(a, b):
    return jnp.dot(a, b, preferred_element_type=F32)


def _mm_nt(a, b):
    return lax.dot_general(a, b, (((1,), (1,)), ((), ())), preferred_element_type=F32)


def _mm_tn(a, b):
    return lax.dot_general(a, b, (((0,), (0,)), ((), ())), preferred_element_type=F32)


def _rms(x, g):
    r = lax.rsqrt(jnp.mean(x * x, axis=-1, keepdims=True) + EPS)
    return (x * r) * g


def _rms_bwd(x, g, dy):
    r = lax.rsqrt(jnp.mean(x * x, axis=-1, keepdims=True) + EPS)
    xh = x * r
    dg = jnp.sum(dy * xh, axis=0, keepdims=True)
    dyg = dy * g
    dx = r * (dyg - xh * jnp.mean(dyg * xh, axis=-1, keepdims=True))
    return dx, dg


def _group_mean(t, gm):
    hi = t.astype(BF16)
    lo = (t - hi.astype(F32)).astype(BF16)
    return _mm(hi, gm) + _mm(lo, gm)


def _swap_rope_halves(x, lane):
    half = QK_ROPE // 2
    swapped = jnp.where(lane < QK_NOPE + half, pltpu.roll(x, x.shape[1] - half, 1), pltpu.roll(x, half, 1))
    return jnp.where((lane >= QK_NOPE) & (lane < QK_HEAD), swapped, 0.0)


def _qk_fwd(x, g, cos, sin, gm, lane):
    r = lax.rsqrt(_group_mean(x * x, gm) + EPS)
    n = (x * r) * g
    return n * cos + _swap_rope_halves(n, lane) * sin


def _qk_bwd(x, g, dy, cos, sin, gm, lane):
    r = lax.rsqrt(_group_mean(x * x, gm) + EPS)
    xh = x * r
    dn = dy * cos + _swap_rope_halves(dy * sin, lane)
    dg = jnp.sum(dn * xh, axis=0, keepdims=True)
    dng = dn * g
    dx = r * (dng - xh * _group_mean(dng * xh, gm))
    return dx, dg


def _rows(tm, n):
    return pl.BlockSpec((tm, n), lambda i: (i, 0))


def _whole(shape):
    zeros = (0,) * len(shape)
    return pl.BlockSpec(shape, lambda i: zeros)


def _operands(arrays):
    return list(arrays), [_whole(a.shape) for a in arrays]


def _accumulate(ref, first, value):
    @pl.when(first)
    def _():
        ref[...] = jnp.zeros_like(ref)
    ref[...] += value


def _front_fwd(x, w, tabs):
    t_len, d = x.shape
    tm = min(TOKEN_TILE, t_len)
    hp = N_HEADS * HEAD_PAD

    def body(x_ref, gmix, win, gq, wuq, gkv, wukv, gqn, gkn, cw_ref, gm_ref, cos_ref, sin_ref,
             z_ref, qf_ref, kf_ref, kv_ref, conv_ref, ubuf):
        i = pl.program_id(0)
        h = _rms(x_ref[...], gmix[...])
        z = _mm(h.astype(BF16), win[...])
        z_ref[...] = z
        qnb = _rms(z[:, Z_Q[0]:Z_Q[1]], gq[...]).astype(BF16)
        kvb = _rms(z[:, Z_KV[0]:Z_KV[1]], gkv[...]).astype(BF16)
        kpe = z[:, Z_KPE[0]:Z_KPE[1]]
        kpe = jnp.concatenate([kpe] * PAIR, axis=1)
        cos, sin, gm = cos_ref[...], sin_ref[...], gm_ref[...]
        lane = lax.broadcasted_iota(jnp.int32, (tm, PAIR * HEAD_PAD), 1) & (HEAD_PAD - 1)
        for pr in range(N_HEADS // PAIR):
            sl = slice(pr * PAIR * HEAD_PAD, (pr + 1) * PAIR * HEAD_PAD)
            qf_ref[:, sl] = (_qk_fwd(_mm(qnb, wuq[pr]), gqn[...], cos, sin, gm, lane) * ATTN_SCALE).astype(BF16)
            kv = _mm(kvb, wukv[pr])
            kv_ref[:, sl] = jnp.where(lane < QK_NOPE, jnp.where(lane == 0, 1.0, 0.0), kv).astype(BF16)
            kf_ref[:, sl] = _qk_fwd(jnp.where(lane < QK_NOPE, kv, 0.0) + kpe, gkn[...], cos, sin, gm, lane).astype(BF16)
        u = z[:, Z_GC[0]:Z_GC[1]] * z[:, Z_XIN[0]:Z_XIN[1]]

        @pl.when(i == 0)
        def _():
            ubuf[0:HALO, :] = jnp.zeros((HALO, CONV_WIDTH), F32)
        ubuf[HALO:HALO + tm, :] = u
        cw = cw_ref[...]
        y = cw[0:1] * u + cw[1:2] * ubuf[pl.ds(HALO - 1, tm), :] + cw[2:3] * ubuf[pl.ds(HALO - 2, tm), :]
        conv_ref[...] = z[:, Z_GB[0]:Z_GB[1]] * y
        ubuf[0:HALO, :] = u[tm - HALO:tm, :]

    consts, const_specs = _operands([w["g_mix"], w["w_in"], w["g_q_lat"], w["w_uq"], w["g_kv_lat"], w["w_ukv"],
                                     w["g_qn"], w["g_kn"], w["conv_w"], tabs["gm"]])
    out_shape = (jax.ShapeDtypeStruct((t_len, Z_COLS), F32), jax.ShapeDtypeStruct((t_len, hp), BF16),
                 jax.ShapeDtypeStruct((t_len, hp), BF16), jax.ShapeDtypeStruct((t_len, hp), BF16),
                 jax.ShapeDtypeStruct((t_len, CONV_WIDTH), F32))
    return pl.pallas_call(body, name="front_fwd", grid=(t_len // tm,),
                          in_specs=[_rows(tm, d)] + const_specs + [_rows(tm, PAIR * HEAD_PAD)] * 2,
                          out_specs=tuple(_rows(tm, s.shape[1]) for s in out_shape), out_shape=out_shape,
                          scratch_shapes=[pltpu.VMEM((tm + HALO, CONV_WIDTH), F32)],
                          compiler_params=_cparams(("arbitrary",)))(x, *consts, tabs["cos"], tabs["sin"])


def _attn_fwd(qf, kf, kv, gather=None):
    t_len = qf.shape[0]
    blk = min(ATTN_FWD_BLOCK, t_len)
    nb = t_len // blk
    n_sub = ATTN_Q_SUB
    bq = blk // n_sub
    chains = [(hh, a) for hh in range(2) for a in range(n_sub)]

    def body(q_ref, k_ref, kv_ref, o_ref, lse_ref):
        lane = lax.broadcasted_iota(jnp.int32, (bq, LANES), 1)
        row = lax.broadcasted_iota(jnp.int32, (bq, blk), 0)
        col = lax.broadcasted_iota(jnp.int32, (bq, blk), 1)

        def head_cols(hh):
            return slice(hh * HEAD_PAD, (hh + 1) * HEAD_PAD)

        def softmax_step(s, kvv, state, first_row=None):
            m, acc = state
            if first_row is not None:
                s = jnp.where(col <= row + first_row, s, -jnp.inf)
            m_new = jnp.maximum(m, jnp.max(s, axis=-1, keepdims=True))
            p = jnp.exp(s - m_new)
            acc = jnp.exp(m - m_new) * acc + _mm(p.astype(BF16), kvv)
            return m_new, acc

        def finish(state):
            m, acc = state
            l = jnp.sum(jnp.where(lane == 0, acc, 0.0), axis=-1, keepdims=True)
            return acc / l, jnp.broadcast_to(m + jnp.log(l), (bq, LANES))

        def qblock(i, carry):
            start = pl.multiple_of(i * blk, blk)
            rows = [pl.ds(pl.multiple_of(start + a * bq, bq), bq) for a in range(n_sub)]
            qs = {(hh, a): q_ref[rows[a], head_cols(hh)] for hh, a in chains}

            def scores(j):
                ks = pl.ds(pl.multiple_of(j * blk, blk), blk)
                return tuple(_mm_nt(qs[hh, a], k_ref[ks, head_cols(hh)]) for hh, a in chains)

            def kstep(j, carried, diagonal=False):
                ss, states = carried
                ss_next = ss if diagonal else scores(j + 1)
                ks = pl.ds(pl.multiple_of(j * blk, blk), blk)
                new = tuple(softmax_step(s, kv_ref[ks, head_cols(hh)], st, a * bq if diagonal else None)
                            for (hh, a), s, st in zip(chains, ss, states))
                return ss_next, new

            init = (jnp.full((bq, 1), -jnp.inf, F32), jnp.zeros((bq, LANES), F32))
            carried = lax.fori_loop(0, i, kstep, (scores(0), (init,) * len(chains)))
            _, states = kstep(i, carried, diagonal=True)
            for a in range(n_sub):
                (o0, lse0), (o1, lse1) = finish(states[chains.index((0, a))]), finish(states[chains.index((1, a))])
                o_ref[rows[a], :] = jnp.where(lane < V_HEAD, pltpu.roll(o0, V_HEAD, 1), o1)
                lse_ref[rows[a], head_cols(0)] = lse0
                lse_ref[rows[a], head_cols(1)] = lse1
            return carry

        lax.fori_loop(0, nb, qblock, 0)

    heads = pl.BlockSpec((t_len, 2 * HEAD_PAD), lambda h: (0, h))
    pair = pl.BlockSpec((t_len, 2 * V_HEAD), lambda h: (0, h))
    out_shape = (jax.ShapeDtypeStruct((t_len, ATTN_WIDTH), F32), jax.ShapeDtypeStruct((t_len, N_HEADS * LANES), F32))
    (attn, lse), gathered = _call_hosting_gather("attn_fwd", body, N_HEADS // 2, [heads, heads, heads], (pair, heads),
                                                 out_shape, [], (qf, kf, kv), gather)
    return attn, lse, gathered


def _call_hosting_gather(name, body, n_steps, in_specs, out_specs, out_shape, scratch_shapes, args, gather):
    if not gather:
        return pl.pallas_call(body, name=name, grid=(n_steps,), in_specs=list(in_specs), out_specs=tuple(out_specs),
                              out_shape=tuple(out_shape), scratch_shapes=list(scratch_shapes),
                              compiler_params=_cparams(("arbitrary",)))(*args), ()
    n_in, n_out, n_g = len(in_specs), len(out_shape), len(gather)

    def hosting_body(*refs):
        ins, refs = refs[:n_in], refs[n_in:]
        x_refs = [r.at[layer] for r, (_, layer) in zip(refs[:n_g], gather)]
        outs, landing, sems, scratch = (refs[n_g:n_g + n_out], refs[n_g + n_out:2 * n_g + n_out],
                                        refs[2 * n_g + n_out:2 * n_g + n_out + 3], refs[2 * n_g + n_out + 3:])
        start, forward, finish = _gather_phases(x_refs, landing, *sems)
        step = pl.program_id(0)
        pl.when(step == 0)(start)
        pl.when(step == n_steps // 2)(forward)
        body(*ins, *outs, *scratch)
        pl.when(step == n_steps - 1)(finish)

    res = pl.pallas_call(hosting_body, name=name + "_gather", grid=(n_steps,), in_specs=list(in_specs) + _any_specs(n_g),
                         out_specs=tuple(out_specs) + tuple(_any_specs(n_g)),
                         out_shape=tuple(out_shape) + _gather_out_shape([s[layer] for s, layer in gather]),
                         scratch_shapes=_gather_semaphores(n_g) + list(scratch_shapes),
                         compiler_params=_cparams(("arbitrary",)))(*args, *[s for s, _ in gather])
    return res[:n_out], res[n_out:]


def _mix_out_fwd(x, attn, conv, w):
    t_len, d = x.shape
    tm = min(TOKEN_TILE, t_len)

    def body(x_ref, a_ref, c_ref, goa, goc, wo, x1_ref):
        mixed = jnp.concatenate([_rms(a_ref[...], goa[...]), _rms(c_ref[...], goc[...])], axis=1)
        x1_ref[...] = x_ref[...] + _mm(mixed.astype(BF16), wo[...])

    consts, const_specs = _operands([w["g_out_attn"], w["g_out_conv"], w["w_o"]])
    return pl.pallas_call(body, name="mix_out_fwd", grid=(t_len // tm,),
                          in_specs=[_rows(tm, d), _rows(tm, ATTN_WIDTH), _rows(tm, CONV_WIDTH)] + const_specs,
                          out_specs=_rows(tm, d), out_shape=jax.ShapeDtypeStruct((t_len, d), F32),
                          compiler_params=_cparams(("parallel",)))(x, attn, conv, *consts)


def _mlp_fwd(x1, w, gather=None):
    t_len, d = x1.shape
    tm = min(TOKEN_TILE, t_len)

    def body(x_ref, g, wup, wdn, x2_ref):
        x1v = x_ref[...]
        hb = _rms(x1v, g[...]).astype(BF16)
        acc = x1v
        for k in range(N_DEV):
            a = jnp.maximum(_mm(hb, wup[k]), 0.0)
            acc = acc + _mm((a * a).astype(BF16), wdn[k])
        x2_ref[...] = acc

    consts, const_specs = _operands([w["g_mlp"], w["w_up"], w["w_down"]])
    (x2,), gathered = _call_hosting_gather("mlp_fwd", body, t_len // tm, [_rows(tm, d)] + const_specs, (_rows(tm, d),),
                                           (jax.ShapeDtypeStruct((t_len, d), F32),), [], (x1, *consts), gather)
    return x2, gathered


def _ple_fwd(x2, p, w):
    t_len, d = x2.shape
    tm = min(TOKEN_TILE, t_len)

    def body(x_ref, p_ref, g, wg, wp, x3_ref):
        x2v = x_ref[...]
        gate = jax.nn.sigmoid(_mm(_rms(x2v, g[...]).astype(BF16), wg[...]))
        pb = p_ref[...].astype(BF16)
        e = jnp.concatenate([_mm(pb, wp[k]) for k in range(N_DEV)], axis=1)
        x3_ref[...] = x2v + gate * e

    consts, const_specs = _operands([w["g_ple"], w["w_ple_gate"], w["w_ple"]])
    return pl.pallas_call(body, name="ple_fwd", grid=(t_len // tm,),
                          in_specs=[_rows(tm, d), _rows(tm, p.shape[1])] + const_specs, out_specs=_rows(tm, d),
                          out_shape=jax.ShapeDtypeStruct((t_len, d), F32),
                          compiler_params=_cparams(("parallel",)))(x2, p, *consts)


def _loss_and_grad(y, target):
    t_len, d = y.shape
    tm = min(TOKEN_TILE, t_len)

    def body(y_ref, t_ref, sq_ref, dy_ref):
        err = y_ref[...] - t_ref[...]
        dy_ref[...] = err / d
        total = jnp.sum(jnp.sum(err * err, axis=0, keepdims=True), axis=1, keepdims=True)
        _accumulate(sq_ref, pl.program_id(0) == 0, jnp.broadcast_to(total, (HALO, LANES)))

    return pl.pallas_call(body, name="loss_grad", grid=(t_len // tm,), in_specs=[_rows(tm, d), _rows(tm, d)],
                          out_specs=(_whole((HALO, LANES)), _rows(tm, d)),
                          out_shape=(jax.ShapeDtypeStruct((HALO, LANES), F32), jax.ShapeDtypeStruct((t_len, d), F32)),
                          compiler_params=_cparams(("arbitrary",)))(y, target)


def _ple_bwd(dx3, x2, p, w):
    t_len, d = x2.shape
    tm = min(TOKEN_TILE, t_len)

    def body(dx3_ref, x_ref, p_ref, g, wg, wp, dx2_ref, de_ref, h3_ref, dpre_ref, dg_ref):
        x2v, dx3v = x_ref[...], dx3_ref[...]
        hb = _rms(x2v, g[...]).astype(BF16)
        h3_ref[...] = hb
        gate = jax.nn.sigmoid(_mm(hb, wg[...]))
        pb = p_ref[...].astype(BF16)
        e = jnp.concatenate([_mm(pb, wp[k]) for k in range(N_DEV)], axis=1)
        de_ref[...] = (dx3v * gate).astype(BF16)
        dpre = ((dx3v * e) * gate * (1.0 - gate)).astype(BF16)
        dpre_ref[...] = dpre
        dx, dg = _rms_bwd(x2v, g[...], _mm_nt(dpre, wg[...]))
        dx2_ref[...] = dx3v + dx
        _accumulate(dg_ref, pl.program_id(0) == 0, dg)

    consts, const_specs = _operands([w["g_ple"], w["w_ple_gate"], w["w_ple"]])
    out_shape = (jax.ShapeDtypeStruct((t_len, d), F32), jax.ShapeDtypeStruct((t_len, d), BF16),
                 jax.ShapeDtypeStruct((t_len, d), BF16), jax.ShapeDtypeStruct((t_len, d), BF16),
                 jax.ShapeDtypeStruct((1, d), F32))
    return pl.pallas_call(body, name="ple_bwd", grid=(t_len // tm,),
                          in_specs=[_rows(tm, d), _rows(tm, d), _rows(tm, p.shape[1])] + const_specs,
                          out_specs=(_rows(tm, d),) * 4 + (_whole((1, d)),), out_shape=out_shape,
                          compiler_params=_cparams(("arbitrary",)))(dx3, x2, p, *consts)


def _mlp_bwd(dx2, x1, w):
    t_len, d = x1.shape
    tm = min(TOKEN_TILE, t_len)
    fc = w["w_up"].shape[2]
    ff = N_DEV * fc

    def body(dx2_ref, x_ref, g, wup, wdn, dx1_ref, r_ref, da_ref, h2_ref, dg_ref):
        x1v, dx2v = x_ref[...], dx2_ref[...]
        hb = _rms(x1v, g[...]).astype(BF16)
        h2_ref[...] = hb
        dxb = dx2v.astype(BF16)
        dh = jnp.zeros((tm, d), F32)
        for k in range(N_DEV):
            a = jnp.maximum(_mm(hb, wup[k]), 0.0)
            r_ref[:, k * fc:(k + 1) * fc] = (a * a).astype(BF16)
            da = (_mm_nt(dxb, wdn[k]) * (2.0 * a)).astype(BF16)
            da_ref[:, k * fc:(k + 1) * fc] = da
            dh = dh + _mm_nt(da, wup[k])
        dx, dg = _rms_bwd(x1v, g[...], dh)
        dx1_ref[...] = dx2v + dx
        _accumulate(dg_ref, pl.program_id(0) == 0, dg)

    consts, const_specs = _operands([w["g_mlp"], w["w_up"], w["w_down"]])
    out_shape = (jax.ShapeDtypeStruct((t_len, d), F32), jax.ShapeDtypeStruct((t_len, ff), BF16),
                 jax.ShapeDtypeStruct((t_len, ff), BF16), jax.ShapeDtypeStruct((t_len, d), BF16),
                 jax.ShapeDtypeStruct((1, d), F32))
    return pl.pallas_call(body, name="mlp_bwd", grid=(t_len // tm,), in_specs=[_rows(tm, d), _rows(tm, d)] + const_specs,
                          out_specs=(_rows(tm, d), _rows(tm, ff), _rows(tm, ff), _rows(tm, d), _whole((1, d))),
                          out_shape=out_shape, compiler_params=_cparams(("arbitrary",)))(dx2, x1, *consts)


def _mix_out_bwd(dx1, attn, conv, w):
    t_len, d = dx1.shape
    tm = min(TOKEN_TILE, t_len)

    def body(dx1_ref, a_ref, c_ref, goa, goc, wo, mixed_ref, da_ref, dc_ref, dgoa_ref, dgoc_ref):
        av, cv = a_ref[...], c_ref[...]
        mixed_ref[...] = jnp.concatenate([_rms(av, goa[...]), _rms(cv, goc[...])], axis=1).astype(BF16)
        dmixed = _mm_nt(dx1_ref[...].astype(BF16), wo[...])
        da, dga = _rms_bwd(av, goa[...], dmixed[:, :ATTN_WIDTH])
        dc, dgc = _rms_bwd(cv, goc[...], dmixed[:, ATTN_WIDTH:])
        da_ref[...] = da
        dc_ref[...] = dc
        first = pl.program_id(0) == 0
        _accumulate(dgoa_ref, first, dga)
        _accumulate(dgoc_ref, first, dgc)

    consts, const_specs = _operands([w["g_out_attn"], w["g_out_conv"], w["w_o"]])
    out_shape = (jax.ShapeDtypeStruct((t_len, d), BF16), jax.ShapeDtypeStruct((t_len, ATTN_WIDTH), F32),
                 jax.ShapeDtypeStruct((t_len, CONV_WIDTH), F32), jax.ShapeDtypeStruct((1, ATTN_WIDTH), F32),
                 jax.ShapeDtypeStruct((1, CONV_WIDTH), F32))
    out_specs = (_rows(tm, d), _rows(tm, ATTN_WIDTH), _rows(tm, CONV_WIDTH), _whole((1, ATTN_WIDTH)),
                 _whole((1, CONV_WIDTH)))
    return pl.pallas_call(body, name="mix_out_bwd", grid=(t_len // tm,),
                          in_specs=[_rows(tm, d), _rows(tm, ATTN_WIDTH), _rows(tm, CONV_WIDTH)] + const_specs,
                          out_specs=out_specs, out_shape=out_shape,
                          compiler_params=_cparams(("arbitrary",)))(dx1, attn, conv, *consts)


def _attn_bwd(qf, kf, kv, o, do, lse, scatter):
    t_len = qf.shape[0]
    blk = min(ATTN_BLOCK, t_len)
    nb = t_len // blk
    n_sub = ATTN_KV_SUB if nb % ATTN_KV_SUB == 0 else 1
    reps = blk // LANES

    def body(q_ref, k_ref, kv_ref, o_ref, do_ref, lse_ref, dq_ref, dk_ref, dkv_ref, delta_ref, dob_ref):
        hd = pl.program_id(0)
        lane = lax.broadcasted_iota(jnp.int32, (blk, LANES), 1)
        even = (lane * 0 + hd % 2) == 0
        mine = jnp.where(lane < V_HEAD, 0, 1) == hd % 2
        row = lax.broadcasted_iota(jnp.int32, (blk, blk), 0)
        col = lax.broadcasted_iota(jnp.int32, (blk, blk), 1)
        dq_ref[...] = jnp.zeros_like(dq_ref)

        def prepare(i, carry):
            qs = pl.ds(pl.multiple_of(i * blk, blk), blk)
            dov = do_ref[qs, :]
            prod = jnp.where(mine, dov * o_ref[qs, :], 0.0)
            delta_ref[qs, :] = jnp.broadcast_to(jnp.sum(prod, axis=-1, keepdims=True), (blk, LANES))
            moved = jnp.where(even, pltpu.roll(dov, V_HEAD, 1), dov)
            dob_ref[qs, :] = jnp.where(lane >= V_HEAD, moved, 0.0).astype(BF16)
            return carry
        lax.fori_loop(0, nb, prepare, 0)

        def kvblock(jj, carry):
            base = jj * n_sub
            kss = [pl.ds(pl.multiple_of((base + a) * blk, blk), blk) for a in range(n_sub)]
            k = [k_ref[ks, :] for ks in kss]
            kvv = [kv_ref[ks, :] for ks in kss]

            def products(i):
                qs = pl.ds(pl.multiple_of(i * blk, blk), blk)
                q, dob = q_ref[qs, :], dob_ref[qs, :]
                return tuple((_mm_nt(q, k[a]), _mm_nt(dob, kvv[a])) for a in range(n_sub))

            def qstep(i, raw, accs, kinds):
                qs = pl.ds(pl.multiple_of(i * blk, blk), blk)
                q = q_ref[qs, :]
                dob = dob_ref[qs, :]
                lse_t = jnp.concatenate([lse_ref[qs, :]] * reps, axis=1)
                delta_t = jnp.concatenate([delta_ref[qs, :]] * reps, axis=1)
                new, dq_add = [], None
                for a in range(n_sub):
                    if kinds[a] is None:
                        new.append(accs[a])
                        continue
                    dk_acc, dv_acc = accs[a]
                    s, dp = raw[a]
                    if kinds[a]:
                        s = jnp.where(col <= row, s, -jnp.inf)
                    p = jnp.exp(s - lse_t)
                    ds = (p * (dp - delta_t)).astype(BF16)
                    new.append((dk_acc + _mm_tn(ds, q), dv_acc + _mm_tn(p.astype(BF16), dob)))
                    part = _mm(ds, k[a])
                    dq_add = part if dq_add is None else dq_add + part
                dq_ref[qs, :] += dq_add
                return tuple(new)

            zero = jnp.zeros((blk, LANES), F32)
            accs = ((zero, zero),) * n_sub
            for b in range(n_sub):
                accs = qstep(base + b, products(base + b), accs, tuple((a == b) if a <= b else None for a in range(n_sub)))

            def pipelined(i, carried):
                raw, acc = carried
                return products(jnp.minimum(i + 1, nb - 1)), qstep(i, raw, acc, (False,) * n_sub)

            first = base + n_sub
            _, accs = lax.fori_loop(first, nb, pipelined, (products(jnp.minimum(first, nb - 1)), accs))
            for a in range(n_sub):
                dk_ref[kss[a], :] = accs[a][0]
                dkv_ref[kss[a], :] = accs[a][1]
            return carry
        lax.fori_loop(0, nb // n_sub, kvblock, 0)

    n_sc = len(scatter[0])

    def hosting_body(*refs):
        ins, rest = refs[:6], refs[6 + 2 * n_sc:]
        parts = refs[6:6 + n_sc]
        outs, landed, sems, scratch = rest[:3], rest[3:3 + n_sc], rest[3 + n_sc:6 + n_sc], rest[6 + n_sc:]
        start, finish = _scatter_phases(parts, landed, *sems, scatter[2])
        hd = pl.program_id(0)
        pl.when(hd == 0)(start)
        body(*ins, *outs, *scratch)
        pl.when(hd == N_HEADS - 1)(finish)

    head = pl.BlockSpec((t_len, HEAD_PAD), lambda h: (0, h))
    pair = pl.BlockSpec((t_len, 2 * V_HEAD), lambda h: (0, h // 2))
    out = jax.ShapeDtypeStruct((t_len, N_HEADS * HEAD_PAD), F32)
    vmem_scratch = [pltpu.VMEM((t_len, LANES), F32), pltpu.VMEM((t_len, LANES), BF16)]
    res = pl.pallas_call(hosting_body, name="attn_bwd_scatter", grid=(N_HEADS,),
                         in_specs=[head, head, head, pair, pair, head] + _any_specs(2 * n_sc),
                         out_specs=(head, head, head) + tuple(_any_specs(n_sc)),
                         out_shape=(out, out, out) + _same_shapes(scatter[1]),
                         scratch_shapes=_scatter_semaphores(n_sc) + vmem_scratch,
                         input_output_aliases={6 + n_sc + a: 3 + a for a in range(n_sc)},
                         compiler_params=_cparams(("arbitrary",)))(qf, kf, kv, o, do, lse, *scatter[0], *scatter[1])
    return res[0], res[1], res[2], res[3:]


def _front_bwd(x, z, dx1, dqf, dkf, dkv_in, dconv, w, tabs):
    t_len, d = x.shape
    tm = min(TOKEN_TILE, t_len)
    nt = t_len // tm
    hb_per_tile = tm // HALO
    n_halo = t_len // HALO
    hp = N_HEADS * HEAD_PAD

    def body(x_ref, z_ref, zp_ref, zn_ref, dx1_ref, dqf_ref, dkf_ref, dkv_ref, dc_ref, dcn_ref,
             gmix, win, gq, wuq, gkv, wukv, gqn, gkn, cw_ref, gm_ref, cos_ref, sin_ref,
             dx_ref, dz_ref, h_ref, qn_ref, kvn_ref, dqr_ref, dkvr_ref,
             dgmix_ref, dgq_ref, dgkv_ref, dgqn_ref, dgkn_ref, dcw_ref, ubuf, dybuf):
        i = pl.program_id(0)
        first = i == 0
        xv, zv = x_ref[...], z_ref[...]
        hb = _rms(xv, gmix[...]).astype(BF16)
        h_ref[...] = hb
        zq, zkv = zv[:, Z_Q[0]:Z_Q[1]], zv[:, Z_KV[0]:Z_KV[1]]
        qnb = _rms(zq, gq[...]).astype(BF16)
        qn_ref[...] = qnb
        kvb = _rms(zkv, gkv[...]).astype(BF16)
        kvn_ref[...] = kvb
        kpe = zv[:, Z_KPE[0]:Z_KPE[1]]
        kpe = jnp.concatenate([kpe] * PAIR, axis=1)
        cos, sin, gm = cos_ref[...], sin_ref[...], gm_ref[...]
        width = PAIR * HEAD_PAD
        lane = lax.broadcasted_iota(jnp.int32, (tm, width), 1) & (HEAD_PAD - 1)
        is_nope = lane < QK_NOPE
        is_rope = (lane >= QK_NOPE) & (lane < QK_HEAD)
        dkpe = jnp.zeros((tm, width), F32)
        dgqn = jnp.zeros((1, width), F32)
        dgkn = jnp.zeros((1, width), F32)
        dqn = jnp.zeros((tm, Q_LORA), F32)
        dkvn = jnp.zeros((tm, KV_LORA), F32)
        for pr in range(N_HEADS // PAIR):
            sl = slice(pr * width, (pr + 1) * width)
            dxq, dg = _qk_bwd(_mm(qnb, wuq[pr]), gqn[...], dqf_ref[:, sl] * ATTN_SCALE, cos, sin, gm, lane)
            dxq = dxq.astype(BF16)
            dqr_ref[:, sl] = dxq
            dqn = dqn + _mm_nt(dxq, wuq[pr])
            dgqn = dgqn + dg
            k_raw = jnp.where(is_nope, _mm(kvb, wukv[pr]), 0.0) + kpe
            dxk, dg = _qk_bwd(k_raw, gkn[...], dkf_ref[:, sl], cos, sin, gm, lane)
            dkv = jnp.where(is_nope, dxk, dkv_ref[:, sl]).astype(BF16)
            dkvr_ref[:, sl] = dkv
            dkvn = dkvn + _mm_nt(dkv, wukv[pr])
            dkpe = dkpe + jnp.where(is_rope, dxk, 0.0)
            dgkn = dgkn + dg
        dkpe = dkpe[:, :HEAD_PAD] + dkpe[:, HEAD_PAD:]
        _accumulate(dgqn_ref, first, dgqn[:, :HEAD_PAD] + dgqn[:, HEAD_PAD:])
        _accumulate(dgkn_ref, first, dgkn[:, :HEAD_PAD] + dgkn[:, HEAD_PAD:])
        dzq, dg = _rms_bwd(zq, gq[...], dqn)
        _accumulate(dgq_ref, first, dg)
        dzkv, dg = _rms_bwd(zkv, gkv[...], dkvn)
        _accumulate(dgkv_ref, first, dg)

        gb, gc, xin = zv[:, Z_GB[0]:Z_GB[1]], zv[:, Z_GC[0]:Z_GC[1]], zv[:, Z_XIN[0]:Z_XIN[1]]
        u = gc * xin
        dcv = dc_ref[...]
        dy = dcv * gb
        zp, zn = zp_ref[...], zn_ref[...]
        ubuf[0:HALO, :] = (zp[:, Z_GC[0]:Z_GC[1]] * zp[:, Z_XIN[0]:Z_XIN[1]]) * jnp.where(first, 0.0, 1.0)
        ubuf[HALO:HALO + tm, :] = u
        dybuf[0:tm, :] = dy
        dybuf[tm:tm + HALO, :] = (dcn_ref[...] * zn[:, Z_GB[0]:Z_GB[1]]) * jnp.where(i == nt - 1, 0.0, 1.0)
        cw = cw_ref[...]
        u1, u2 = ubuf[pl.ds(HALO - 1, tm), :], ubuf[pl.ds(HALO - 2, tm), :]
        y = cw[0:1] * u + cw[1:2] * u1 + cw[2:3] * u2
        du = cw[0:1] * dy + cw[1:2] * dybuf[pl.ds(1, tm), :] + cw[2:3] * dybuf[pl.ds(2, tm), :]
        dcw = jnp.concatenate([jnp.sum(dy * u, axis=0, keepdims=True), jnp.sum(dy * u1, axis=0, keepdims=True),
                               jnp.sum(dy * u2, axis=0, keepdims=True), jnp.zeros((HALO - 3, CONV_WIDTH), F32)], axis=0)
        _accumulate(dcw_ref, first, dcw)

        dz_ref[:, Z_Q[0]:Z_Q[1]] = dzq.astype(BF16)
        dz_ref[:, Z_KV[0]:Z_KV[1]] = dzkv.astype(BF16)
        dz_ref[:, Z_GB[0]:Z_GB[1]] = (dcv * y).astype(BF16)
        dz_ref[:, Z_GC[0]:Z_GC[1]] = (du * xin).astype(BF16)
        dz_ref[:, Z_XIN[0]:Z_XIN[1]] = (du * gc).astype(BF16)
        dz_ref[:, Z_KPE[0]:Z_KPE[1]] = dkpe.astype(BF16)
        dx, dg = _rms_bwd(xv, gmix[...], _mm_nt(dz_ref[...], win[...]))
        dx_ref[...] = dx1_ref[...] + dx
        _accumulate(dgmix_ref, first, dg)

    prev_halo = lambda n: pl.BlockSpec((HALO, n), lambda i: (jnp.maximum(i * hb_per_tile - 1, 0), 0))
    next_halo = lambda n: pl.BlockSpec((HALO, n), lambda i: (jnp.minimum((i + 1) * hb_per_tile, n_halo - 1), 0))
    consts, const_specs = _operands([w["g_mix"], w["w_in"], w["g_q_lat"], w["w_uq"], w["g_kv_lat"], w["w_ukv"],
                                     w["g_qn"], w["g_kn"], w["conv_w"], tabs["gm"]])
    in_specs = ([_rows(tm, d), _rows(tm, Z_COLS), prev_halo(Z_COLS), next_halo(Z_COLS), _rows(tm, d), _rows(tm, hp),
                 _rows(tm, hp), _rows(tm, hp), _rows(tm, CONV_WIDTH), next_halo(CONV_WIDTH)]
                + const_specs + [_rows(tm, PAIR * HEAD_PAD)] * 2)
    out_shape = (jax.ShapeDtypeStruct((t_len, d), F32), jax.ShapeDtypeStruct((t_len, Z_COLS), BF16),
                 jax.ShapeDtypeStruct((t_len, d), BF16), jax.ShapeDtypeStruct((t_len, Q_LORA), BF16),
                 jax.ShapeDtypeStruct((t_len, KV_LORA), BF16), jax.ShapeDtypeStruct((t_len, hp), BF16),
                 jax.ShapeDtypeStruct((t_len, hp), BF16),
                 jax.ShapeDtypeStruct((1, d), F32), jax.ShapeDtypeStruct((1, Q_LORA), F32),
                 jax.ShapeDtypeStruct((1, KV_LORA), F32), jax.ShapeDtypeStruct((1, LANES), F32),
                 jax.ShapeDtypeStruct((1, LANES), F32), jax.ShapeDtypeStruct((HALO, CONV_WIDTH), F32))
    out_specs = tuple(_rows(tm, s.shape[1]) for s in out_shape[:7]) + tuple(_whole(s.shape) for s in out_shape[7:])
    return pl.pallas_call(body, name="front_bwd", grid=(nt,), in_specs=in_specs, out_specs=out_specs, out_shape=out_shape,
                          scratch_shapes=[pltpu.VMEM((tm + HALO, CONV_WIDTH), F32), pltpu.VMEM((tm + HALO, CONV_WIDTH), F32)],
                          compiler_params=_cparams(("arbitrary",)))(
                              x, z, z, z, dx1, dqf, dkf, dkv_in, dconv, dconv, *consts, tabs["cos"], tabs["sin"])


def _wgrad(a, b, shard_cols=None, out_dtype=BF16):
    t_len, kk = a.shape
    nn = b.shape[1]
    tk = min(kk, WGRAD_TILE)
    tn = next(c for c in range(min(nn, WGRAD_TILE), 0, -LANES) if nn % c == 0 and c % (shard_cols or LANES) == 0)
    tt = min(t_len, WGRAD_TOKENS)
    nt = t_len // tt
    per_block = tn // shard_cols if shard_cols else 1

    def body(a_ref, b_ref, o_ref, acc):
        t = pl.program_id(2)

        @pl.when(t == 0)
        def _():
            acc[...] = jnp.zeros_like(acc)
        acc[...] += _mm_tn(a_ref[...].astype(BF16), b_ref[...].astype(BF16))

        @pl.when(t == nt - 1)
        def _():
            if shard_cols:
                for s in range(per_block):
                    o_ref[s] = acc[:, s * shard_cols:(s + 1) * shard_cols].astype(out_dtype)
            else:
                o_ref[...] = acc[...].astype(out_dtype)

    if shard_cols:
        out_shape = jax.ShapeDtypeStruct((nn // shard_cols, kk, shard_cols), out_dtype)
        out_spec = pl.BlockSpec((per_block, tk, shard_cols), lambda i, j, t: (j, i, 0))
    else:
        out_shape = jax.ShapeDtypeStruct((kk, nn), out_dtype)
        out_spec = pl.BlockSpec((tk, tn), lambda i, j, t: (i, j))
    return pl.pallas_call(body, name="wgrad", grid=(kk // tk, nn // tn, nt),
                          in_specs=[pl.BlockSpec((tt, tk), lambda i, j, t: (t, i)),
                                    pl.BlockSpec((tt, tn), lambda i, j, t: (t, j))],
                          out_specs=out_spec, out_shape=out_shape, scratch_shapes=[pltpu.VMEM((tk, tn), F32)],
                          compiler_params=_cparams(("parallel", "parallel", "arbitrary")))(a, b)


def _my_place():
    return lax.axis_index("x"), lax.axis_index("y"), lax.axis_index("c")


def _any_specs(n):
    return [pl.BlockSpec(memory_space=pl.ANY)] * n


def _all_gather(blocks):
    n = len(blocks)

    def body(*refs):
        start, forward, finish = _gather_phases(refs[:n], refs[n:2 * n], *refs[2 * n:])
        start()
        forward()
        finish()

    return pl.pallas_call(body, name="all_gather", out_shape=_gather_out_shape(blocks), in_specs=_any_specs(n),
                          out_specs=tuple(_any_specs(n)), scratch_shapes=_gather_semaphores(n))(*blocks)


def _gather_out_shape(blocks):
    return tuple(jax.ShapeDtypeStruct((N_DEV,) + b.shape, b.dtype) for b in blocks)


def _gather_semaphores(n):
    return [pltpu.SemaphoreType.DMA((n, 7)), pltpu.SemaphoreType.DMA((n, 7)), pltpu.SemaphoreType.DMA((n,))]


def _gather_phases(x_refs, out_refs, send_sems, recv_sems, local_sems):
    n = len(x_refs)
    x, y, c = _my_place()
    me, sibling = (x, y, c), (x, y, 1 - c)
    chips = [(1 - x, y), (x, 1 - y), (1 - x, 1 - y)]

    def slot(a, px, py, pc):
        return out_refs[a].at[4 * px + 2 * py + pc]

    def copy(a, k, blk, to, src=None):
        return pltpu.make_async_remote_copy(src_ref=slot(a, *blk) if src is None else src, dst_ref=slot(a, *blk),
                                            send_sem=send_sems.at[a, k], recv_sem=recv_sems.at[a, k],
                                            device_id=to, device_id_type=MESH)

    def own(a):
        return pltpu.make_async_copy(x_refs[a], slot(a, *me), local_sems.at[a])

    def first_hop(a):
        return [copy(a, 0, me, sibling, src=x_refs[a])] + [copy(a, 1 + j, me, (*chip, c), src=x_refs[a])
                                                           for j, chip in enumerate(chips)]

    def passed_on(a):
        return [copy(a, 4 + j, (*chip, c), sibling) for j, chip in enumerate(chips)]

    def start():
        for a in range(n):
            own(a).start()
        for a in range(n):
            for cp in first_hop(a):
                cp.start()

    def forward():
        for j, chip in enumerate(chips):
            for a in range(n):
                copy(a, 1 + j, (*chip, c), me).wait_recv()
                passed_on(a)[j].start()

    def finish():
        for a in range(n):
            copy(a, 0, sibling, me).wait_recv()
        for j, chip in enumerate(chips):
            for a in range(n):
                copy(a, 4 + j, (*chip, 1 - c), me).wait_recv()
        for a in range(n):
            for cp in first_hop(a) + passed_on(a):
                cp.wait_send()
            own(a).wait()

    return start, forward, finish


def _scatter_exchange(parts, landed, layers):
    n = len(parts)

    def body(*refs):
        start, finish = _scatter_phases(refs[:n], refs[2 * n:3 * n], *refs[3 * n:], layers)
        start()
        finish()

    return pl.pallas_call(body, name="scatter_exchange", out_shape=_same_shapes(landed), in_specs=_any_specs(2 * n),
                          out_specs=tuple(_any_specs(n)), scratch_shapes=_scatter_semaphores(n),
                          input_output_aliases={n + a: a for a in range(n)})(*parts, *landed)


def _same_shapes(arrays):
    return tuple(jax.ShapeDtypeStruct(a.shape, a.dtype) for a in arrays)


def _scatter_semaphores(n):
    return [pltpu.SemaphoreType.DMA((n, N_DEV - 1)), pltpu.SemaphoreType.DMA((n, N_DEV - 1)), pltpu.SemaphoreType.DMA((n,))]


def _scatter_phases(part_refs, landed_refs, send_sems, recv_sems, local_sems, layers):
    n = len(part_refs)
    x, y, c = _my_place()
    flips = [(0, 0, 1), (1, 0, 0), (0, 1, 0), (1, 1, 0), (1, 0, 1), (0, 1, 1), (1, 1, 1)]
    peers = [((1 - x) if fx else x, (1 - y) if fy else y, (1 - c) if fc else c) for fx, fy, fc in flips]
    my_k = 4 * x + 2 * y + c

    def index(peer):
        return 4 * peer[0] + 2 * peer[1] + peer[2]

    def send(a, r):
        return pltpu.make_async_remote_copy(src_ref=part_refs[a].at[index(peers[r])], dst_ref=landed_refs[a].at[my_k, layers[a]],
                                            send_sem=send_sems.at[a, r], recv_sem=recv_sems.at[a, r],
                                            device_id=peers[r], device_id_type=MESH)

    def arrival(a, r):
        return pltpu.make_async_remote_copy(src_ref=part_refs[a].at[my_k], dst_ref=landed_refs[a].at[index(peers[r]), layers[a]],
                                            send_sem=send_sems.at[a, r], recv_sem=recv_sems.at[a, r],
                                            device_id=peers[r], device_id_type=MESH)

    def own(a):
        return pltpu.make_async_copy(part_refs[a].at[my_k], landed_refs[a].at[my_k, layers[a]], local_sems.at[a])

    def start():
        for a in range(n):
            own(a).start()
        for r in range(len(peers)):
            for a in range(n):
                send(a, r).start()

    def finish():
        for r in range(len(peers)):
            for a in range(n):
                arrival(a, r).wait_recv()
        for r in range(len(peers)):
            for a in range(n):
                send(a, r).wait_send()
        for a in range(n):
            own(a).wait()

    return start, finish


def _row_block(rows):
    return ROW_BLOCK if rows % ROW_BLOCK == 0 else rows


def _sum_leading(parts):
    n_part, shape = parts.shape[0], parts.shape[1:]
    cols = shape[-1]
    p2 = parts.reshape(n_part, -1, cols)
    rows = p2.shape[1]
    rb = _row_block(rows)

    def body(p_ref, o_ref):
        acc = p_ref[0].astype(F32)
        for k in range(1, n_part):
            acc = acc + p_ref[k].astype(F32)
        o_ref[...] = acc

    out = pl.pallas_call(body, name="sum_leading", grid=(rows // rb,),
                         in_specs=[pl.BlockSpec((n_part, rb, cols), lambda i: (0, i, 0))], out_specs=_rows(rb, cols),
                         out_shape=jax.ShapeDtypeStruct((rows, cols), F32), compiler_params=_cparams(("parallel",)))(p2)
    return out.reshape(shape)


def _adamw(w, g, m, v):
    shape = w.shape
    two_d = (shape[0] * shape[1], shape[2]) if len(shape) == 3 else shape
    rows, cols = two_d
    rb = _row_block(rows)

    def body(w_ref, g_ref, m_ref, v_ref, d_ref, nm_ref, nv_ref):
        gv = g_ref[...]
        nm = ADAM_B1 * m_ref[...] + (1.0 - ADAM_B1) * gv
        nv = ADAM_B2 * v_ref[...] + (1.0 - ADAM_B2) * jnp.square(gv)
        m_hat = nm / (1.0 - ADAM_B1 ** ADAM_STEP)
        v_hat = nv / (1.0 - ADAM_B2 ** ADAM_STEP)
        d_ref[...] = -ADAM_LR * (m_hat / (jnp.sqrt(v_hat) + ADAM_EPS) + ADAM_WD * w_ref[...])
        nm_ref[...] = nm
        nv_ref[...] = nv

    spec = _rows(rb, cols)
    out = jax.ShapeDtypeStruct(two_d, F32)
    res = pl.pallas_call(body, name="adamw", grid=(rows // rb,), in_specs=[spec] * 4, out_specs=(spec,) * 3,
                         out_shape=(out,) * 3, compiler_params=_cparams(("parallel",)))(
                             *(a.reshape(two_d) for a in (w, g, m, v)))
    return tuple(a.reshape(shape) for a in res)


def _rope_tables(positions):
    t_len = positions.shape[0]
    inv_freq = 1.0 / (ROPE_THETA ** (jnp.arange(0, QK_ROPE, 2, dtype=F32) / QK_ROPE))
    ang = positions.astype(F32)[:, None] * inv_freq
    c, s = jnp.cos(ang), jnp.sin(ang)
    one, zero = jnp.ones((t_len, QK_NOPE), F32), jnp.zeros((t_len, QK_NOPE), F32)
    cos = jnp.concatenate([one, c, c, one[:, :LANES - QK_HEAD]], axis=1)
    sin = jnp.concatenate([zero, -s, s, zero[:, :LANES - QK_HEAD]], axis=1)
    idx = jnp.arange(PAIR * HEAD_PAD)
    lane, head = idx % HEAD_PAD, idx // HEAD_PAD
    grp = jnp.where(lane < QK_NOPE, 0, jnp.where(lane < QK_HEAD, 1, 2)) + 3 * head
    val = jnp.where(lane < QK_NOPE, 1.0 / QK_NOPE, jnp.where(lane < QK_HEAD, 1.0 / QK_ROPE, 0.0))
    gm = jnp.where(grp[:, None] == grp[None, :], val[None, :], 0.0).astype(BF16)
    return {"cos": jnp.concatenate([cos] * PAIR, axis=1), "sin": jnp.concatenate([sin] * PAIR, axis=1), "gm": gm}


def _head_gain(g_nope, g_rope):
    one = jnp.concatenate([g_nope, g_rope, jnp.zeros((HEAD_PAD - QK_HEAD,), F32)])
    return jnp.concatenate([one] * PAIR).reshape(1, PAIR * HEAD_PAD)


def _head_pairs(w):
    return jnp.concatenate([w[k::PAIR] for k in range(PAIR)], axis=2)


def _padded_w_in(shards):
    natural = jnp.concatenate([shards[k] for k in range(N_DEV)], axis=1)
    o2, o3 = Q_LORA + KV_LORA, Q_LORA + KV_LORA + QK_ROPE
    zeros = jnp.zeros((natural.shape[0], QK_NOPE), natural.dtype)
    return jnp.concatenate([natural[:, :o2], natural[:, o3:], zeros, natural[:, o2:o3], zeros[:, :LANES - QK_HEAD]], axis=1)


def _w_in_grad_shards(d_in):
    o2 = Q_LORA + KV_LORA
    nat = jnp.concatenate([d_in[:, :o2], d_in[:, Z_KPE[0] + QK_NOPE:Z_KPE[0] + QK_HEAD], d_in[:, o2:Z_XIN[1]]], axis=1)
    width = nat.shape[1] // N_DEV
    return jnp.stack([nat[:, k * width:(k + 1) * width] for k in range(N_DEV)])


def kernel(x, p, positions, g_mix, w_in, g_q_lat, w_uq, g_kv_lat, w_ukv, g_qn_nope, g_qn_rope, g_kn_nope, g_kn_rope, conv_w, g_out_attn, g_out_conv, w_o, g_mlp, w_up, w_down, g_ple, w_ple_gate, w_ple, loss_target, m_g_mix, m_w_in, m_g_q_lat, m_w_uq, m_g_kv_lat, m_w_ukv, m_g_qn_nope, m_g_qn_rope, m_g_kn_nope, m_g_kn_rope, m_conv_w, m_g_out_attn, m_g_out_conv, m_w_o, m_g_mlp, m_w_up, m_w_down, m_g_ple, m_w_ple_gate, m_w_ple, v_g_mix, v_w_in, v_g_q_lat, v_w_uq, v_g_kv_lat, v_w_ukv, v_g_qn_nope, v_g_qn_rope, v_g_kn_nope, v_g_kn_rope, v_conv_w, v_g_out_attn, v_g_out_conv, v_w_o, v_g_mlp, v_w_up, v_w_down, v_g_ple, v_w_ple_gate, v_w_ple):
    given = dict(locals())
    weights = {n: given[n] for n in WEIGHT_NAMES}
    gains = {n: given[n] for n in GAIN_NAMES}
    depth = w_in.shape[0]
    xs, target = x[0], loss_target[0]
    d_model = xs.shape[1]
    uq_cols = w_uq.shape[2]
    n_taps = conv_w.shape[1]

    mat_names = [n for n in SHARD_NAMES if n != "conv_w"]
    local = [weights[n].astype(BF16) for n in mat_names]
    local[1] = jnp.pad(local[1], ((0, 0), (0, 0), (0, HEAD_PAD - uq_cols)))
    local = dict(zip(mat_names, local))
    front_names = ("w_in", "w_uq", "w_ukv")
    first = _all_gather([local[n][0] for n in front_names] + [conv_w])
    conv_full = jnp.transpose(first[-1], (1, 2, 0, 3)).reshape(depth, n_taps, -1)
    tabs = _rope_tables(positions[0])

    def front_weights(layer, full):
        lw = {n: gains[n][layer].reshape(1, -1) for n in GAIN_NAMES}
        lw.update({"w_in": _padded_w_in(full["w_in"]), "w_uq": _head_pairs(full["w_uq"]),
                   "w_ukv": _head_pairs(full["w_ukv"]),
                   "conv_w": jnp.pad(conv_full[layer], ((0, HALO - n_taps), (0, 0))),
                   "g_qn": _head_gain(g_qn_nope[layer], g_qn_rope[layer]),
                   "g_kn": _head_gain(g_kn_nope[layer], g_kn_rope[layer])})
        return lw

    def rest_weights(full):
        return {"w_ple": full["w_ple"], "w_up": full["w_up"], "w_down": full["w_down"],
                "w_o": full["w_o"].reshape(d_model, d_model), "w_ple_gate": full["w_ple_gate"].reshape(d_model, d_model)}

    saved, layer_w = [], []
    cur = xs
    gathered = dict(zip(front_names, first[:-1]))
    mlp_names = ("w_up", "w_down")
    for layer in range(depth):
        w = front_weights(layer, gathered)
        z, qf, kf, kv, conv = _front_fwd(cur, w, tabs)
        more = layer + 1 < depth
        wanted = [(n, 0) for n in mat_names if n not in front_names] if layer == 0 else []
        wanted += [(n, layer + 1) for n in mat_names if n not in mlp_names] if more else []
        attn, lse, got = _attn_fwd(qf, kf, kv, gather=[(local[n], at) for n, at in wanted])
        coming = {}
        for (n, at), g in zip(wanted, got):
            (gathered if at == layer else coming)[n] = g
        w.update(rest_weights(gathered))
        layer_w.append(w)
        x1 = _mix_out_fwd(cur, attn, conv, w)
        x2, got = _mlp_fwd(x1, w, gather=[(local[n], layer + 1) for n in mlp_names] if more else None)
        coming.update(zip(mlp_names, got))
        gathered = coming
        x3 = _ple_fwd(x2, p[layer, 0], w)
        saved.append(dict(x=cur, z=z, qf=qf, kf=kf, kv=kv, conv=conv, attn=attn, lse=lse, x1=x1, x2=x2))
        cur = x3

    sq, dx = _loss_and_grad(cur, target)
    loss = lax.psum(0.5 / d_model * sq[0, 0], ("x", "y", "c"))

    landed = {n: lax.empty((N_DEV, depth) + weights[n].shape[1:], BF16) for n in SHARD_NAMES}
    gain_grads = [None] * depth
    late = {}
    for layer in reversed(range(depth)):
        w, s = layer_w[layer], saved[layer]
        pl_in = p[layer, 0]
        dx2, de, h3, dpre, dg_ple = _ple_bwd(dx, s["x2"], pl_in, w)
        dx1, r, da, h2, dg_mlp = _mlp_bwd(dx2, s["x1"], w)
        mixed, dattn, dconv, dg_oa, dg_oc = _mix_out_bwd(dx1, s["attn"], s["conv"], w)
        sending = {"w_o": (_wgrad(mixed, dx1).reshape((N_DEV,) + w_o.shape[1:]), layer),
                   "w_up": (_wgrad(h2, da, shard_cols=w_up.shape[2]), layer),
                   "w_down": (_wgrad(r, dx2).reshape((N_DEV,) + w_down.shape[1:]), layer),
                   "w_ple_gate": (_wgrad(h3, dpre).reshape((N_DEV,) + w_ple_gate.shape[1:]), layer),
                   "w_ple": (_wgrad(pl_in, de, shard_cols=w_ple.shape[2]), layer), **late}
        names = list(sending)
        dqf, dkf, dkv, got = _attn_bwd(s["qf"], s["kf"], s["kv"], s["attn"], dattn, s["lse"],
                                       scatter=([sending[n][0] for n in names], [landed[n] for n in names],
                                                [sending[n][1] for n in names]))
        landed.update(zip(names, got))
        (dx0, dz, hb, qn, kvn, dqr, dkvr, dg_mix, dg_q, dg_kv, dg_qn, dg_kn, dcw) = _front_bwd(
            s["x"], s["z"], dx1, dqf, dkf, dkv, dconv, w, tabs)
        late = {"w_in": (_w_in_grad_shards(_wgrad(hb, dz, out_dtype=F32)).astype(BF16), layer),
                "w_uq": (_wgrad(qn, dqr, shard_cols=HEAD_PAD)[..., :uq_cols], layer),
                "w_ukv": (_wgrad(kvn, dkvr, shard_cols=HEAD_PAD), layer),
                "conv_w": (jnp.transpose(dcw[:n_taps].reshape(n_taps, N_DEV, -1), (1, 0, 2)).astype(BF16), layer)}
        gain_grads[layer] = jnp.concatenate([
            dg_mix[0], dg_q[0], dg_kv[0], dg_qn[0, :QK_NOPE], dg_qn[0, QK_NOPE:QK_HEAD], dg_kn[0, :QK_NOPE],
            dg_kn[0, QK_NOPE:QK_HEAD], dg_oa[0], dg_oc[0], dg_mlp[0], dg_ple[0]])
        dx = dx0
    names = list(late)
    landed.update(zip(names, _scatter_exchange([late[n][0] for n in names], [landed[n] for n in names],
                                               [late[n][1] for n in names])))
    grads = {n: _sum_leading(landed[n]) for n in SHARD_NAMES}

    gg = jnp.stack(gain_grads)
    gg_rows = -(-gg.size // (HALO * LANES)) * HALO
    gg_pad = jnp.pad(gg.reshape(-1), (0, gg_rows * LANES - gg.size)).reshape(gg_rows, LANES)
    gg_sum = _sum_leading(_all_gather([gg_pad])[0]).reshape(-1)[:gg.size].reshape(gg.shape)
    off = 0
    for n in GAIN_NAMES:
        width = gains[n].shape[1]
        grads[n] = gg_sum[:, off:off + width]
        off += width

    deltas, new_m, new_v = {}, {}, {}
    for n in WEIGHT_NAMES:
        deltas[n], new_m[n], new_v[n] = _adamw(weights[n], grads[n], given["m_" + n], given["v_" + n])
    return (loss, dx[None], *[grads[n] for n in WEIGHT_NAMES], *[deltas[n] for n in WEIGHT_NAMES],
            *[new_m[n] for n in WEIGHT_NAMES], *[new_v[n] for n in WEIGHT_NAMES])
```

```python
import jax
import jax.numpy as jnp
from jax import lax
from jax.experimental import pallas as pl
from jax.experimental.pallas import tpu as pltpu

F32 = jnp.float32
BF16 = jnp.bfloat16
MESH = pl.DeviceIdType.MESH

N_HEADS = 8
QK_NOPE = 64
QK_ROPE = 32
QK_HEAD = QK_NOPE + QK_ROPE
V_HEAD = 64
HEAD_PAD = 128
PAIR = 2
ATTN_SCALE = QK_HEAD ** -0.5
Q_LORA = 384
KV_LORA = 256
CONV_WIDTH = 512
ATTN_WIDTH = N_HEADS * V_HEAD
ROPE_THETA = 10000.0
EPS = 1e-6
ADAM_LR, ADAM_B1, ADAM_B2, ADAM_EPS, ADAM_WD, ADAM_STEP = 0.001, 0.9, 0.999, 1e-08, 0.01, 10

Z_Q = (0, 384)
Z_KV = (384, 640)
Z_GB = (640, 1152)
Z_GC = (1152, 1664)
Z_XIN = (1664, 2176)
Z_KPE = (2176, 2304)
Z_COLS = 2304

N_DEV = 8
LANES = 128
V7X_VMEM_LIMIT = 52 * 1024 * 1024
TOKEN_TILE = 256
ATTN_BLOCK = 256
ATTN_FWD_BLOCK = 512
ATTN_Q_SUB = 2
ATTN_KV_SUB = 2
ROW_BLOCK = 512
WGRAD_TOKENS = 2048
WGRAD_TILE = 1024
HALO = 8

GAIN_NAMES = ("g_mix", "g_q_lat", "g_kv_lat", "g_qn_nope", "g_qn_rope", "g_kn_nope", "g_kn_rope",
              "g_out_attn", "g_out_conv", "g_mlp", "g_ple")
SHARD_NAMES = ("w_in", "w_uq", "w_ukv", "conv_w", "w_o", "w_up", "w_down", "w_ple_gate", "w_ple")
WEIGHT_NAMES = ("g_mix", "w_in", "g_q_lat", "w_uq", "g_kv_lat", "w_ukv", "g_qn_nope", "g_qn_rope", "g_kn_nope",
                "g_kn_rope", "conv_w", "g_out_attn", "g_out_conv", "w_o", "g_mlp", "w_up", "w_down", "g_ple",
                "w_ple_gate", "w_ple")


def _cparams(semantics=None):
    return pltpu.CompilerParams(dimension_semantics=semantics, vmem_limit_bytes=V7X_VMEM_LIMIT)


def _mm(a, b):
    return jnp.dot(a, b, preferred_element_type=F32)


def _mm_nt(a, b):
    return lax.dot_general(a, b, (((1,), (1,)), ((), ())), preferred_element_type=F32)


def _mm_tn(a, b):
    return lax.dot_general(a, b, (((0,), (0,)), ((), ())), preferred_element_type=F32)


def _rms(x, g):
    r = lax.rsqrt(jnp.mean(x * x, axis=-1, keepdims=True) + EPS)
    return (x * r) * g


def _rms_bwd(x, g, dy):
    r = lax.rsqrt(jnp.mean(x * x, axis=-1, keepdims=True) + EPS)
    xh = x * r
    dg = jnp.sum(dy * xh, axis=0, keepdims=True)
    dyg = dy * g
    dx = r * (dyg - xh * jnp.mean(dyg * xh, axis=-1, keepdims=True))
    return dx, dg


def _group_mean(t, gm):
    hi = t.astype(BF16)
    lo = (t - hi.astype(F32)).astype(BF16)
    return _mm(hi, gm) + _mm(lo, gm)


def _swap_rope_halves(x, lane):
    half = QK_ROPE // 2
    swapped = jnp.where(lane < QK_NOPE + half, pltpu.roll(x, x.shape[1] - half, 1), pltpu.roll(x, half, 1))
    return jnp.where((lane >= QK_NOPE) & (lane < QK_HEAD), swapped, 0.0)


def _qk_fwd(x, g, cos, sin, gm, lane):
    r = lax.rsqrt(_group_mean(x * x, gm) + EPS)
    n = (x * r) * g
    return n * cos + _swap_rope_halves(n, lane) * sin


def _qk_bwd(x, g, dy, cos, sin, gm, lane):
    r = lax.rsqrt(_group_mean(x * x, gm) + EPS)
    xh = x * r
    dn = dy * cos + _swap_rope_halves(dy * sin, lane)
    dg = jnp.sum(dn * xh, axis=0, keepdims=True)
    dng = dn * g
    dx = r * (dng - xh * _group_mean(dng * xh, gm))
    return dx, dg


def _rows(tm, n):
    return pl.BlockSpec((tm, n), lambda i: (i, 0))


def _whole(shape):
    zeros = (0,) * len(shape)
    return pl.BlockSpec(shape, lambda i: zeros)


def _operands(arrays):
    return list(arrays), [_whole(a.shape) for a in arrays]


def _accumulate(ref, first, value):
    @pl.when(first)
    def _():
        ref[...] = jnp.zeros_like(ref)
    ref[...] += value


def _front_fwd(x, w, tabs):
    t_len, d = x.shape
    tm = min(TOKEN_TILE, t_len)
    hp = N_HEADS * HEAD_PAD

    def body(x_ref, gmix, win, gq, wuq, gkv, wukv, gqn, gkn, cw_ref, gm_ref, cos_ref, sin_ref,
             z_ref, qf_ref, kf_ref, kv_ref, conv_ref, ubuf):
        i = pl.program_id(0)
        h = _rms(x_ref[...], gmix[...])
        z = _mm(h.astype(BF16), win[...])
        z_ref[...] = z
        qnb = _rms(z[:, Z_Q[0]:Z_Q[1]], gq[...]).astype(BF16)
        kvb = _rms(z[:, Z_KV[0]:Z_KV[1]], gkv[...]).astype(BF16)
        kpe = z[:, Z_KPE[0]:Z_KPE[1]]
        kpe = jnp.concatenate([kpe] * PAIR, axis=1)
        cos, sin, gm = cos_ref[...], sin_ref[...], gm_ref[...]
        lane = lax.broadcasted_iota(jnp.int32, (tm, PAIR * HEAD_PAD), 1) & (HEAD_PAD - 1)
        for pr in range(N_HEADS // PAIR):
            sl = slice(pr * PAIR * HEAD_PAD, (pr + 1) * PAIR * HEAD_PAD)
            qf_ref[:, sl] = (_qk_fwd(_mm(qnb, wuq[pr]), gqn[...], cos, sin, gm, lane) * ATTN_SCALE).astype(BF16)
            kv = _mm(kvb, wukv[pr])
            kv_ref[:, sl] = jnp.where(lane < QK_NOPE, jnp.where(lane == 0, 1.0, 0.0), kv).astype(BF16)
            kf_ref[:, sl] = _qk_fwd(jnp.where(lane < QK_NOPE, kv, 0.0) + kpe, gkn[...], cos, sin, gm, lane).astype(BF16)
        u = z[:, Z_GC[0]:Z_GC[1]] * z[:, Z_XIN[0]:Z_XIN[1]]

        @pl.when(i == 0)
        def _():
            ubuf[0:HALO, :] = jnp.zeros((HALO, CONV_WIDTH), F32)
        ubuf[HALO:HALO + tm, :] = u
        cw = cw_ref[...]
        y = cw[0:1] * u + cw[1:2] * ubuf[pl.ds(HALO - 1, tm), :] + cw[2:3] * ubuf[pl.ds(HALO - 2, tm), :]
        conv_ref[...] = z[:, Z_GB[0]:Z_GB[1]] * y
        ubuf[0:HALO, :] = u[tm - HALO:tm, :]

    consts, const_specs = _operands([w["g_mix"], w["w_in"], w["g_q_lat"], w["w_uq"], w["g_kv_lat"], w["w_ukv"],
                                     w["g_qn"], w["g_kn"], w["conv_w"], tabs["gm"]])
    out_shape = (jax.ShapeDtypeStruct((t_len, Z_COLS), F32), jax.ShapeDtypeStruct((t_len, hp), BF16),
                 jax.ShapeDtypeStruct((t_len, hp), BF16), jax.ShapeDtypeStruct((t_len, hp), BF16),
                 jax.ShapeDtypeStruct((t_len, CONV_WIDTH), F32))
    return pl.pallas_call(body, name="front_fwd", grid=(t_len // tm,),
                          in_specs=[_rows(tm, d)] + const_specs + [_rows(tm, PAIR * HEAD_PAD)] * 2,
                          out_specs=tuple(_rows(tm, s.shape[1]) for s in out_shape), out_shape=out_shape,
                          scratch_shapes=[pltpu.VMEM((tm + HALO, CONV_WIDTH), F32)],
                          compiler_params=_cparams(("arbitrary",)))(x, *consts, tabs["cos"], tabs["sin"])


def _attn_fwd(qf, kf, kv, gather=None):
    t_len = qf.shape[0]
    blk = min(ATTN_FWD_BLOCK, t_len)
    nb = t_len // blk
    n_sub = ATTN_Q_SUB
    bq = blk // n_sub
    chains = [(hh, a) for hh in range(2) for a in range(n_sub)]

    def body(q_ref, k_ref, kv_ref, o_ref, lse_ref):
        lane = lax.broadcasted_iota(jnp.int32, (bq, LANES), 1)
        row = lax.broadcasted_iota(jnp.int32, (bq, blk), 0)
        col = lax.broadcasted_iota(jnp.int32, (bq, blk), 1)

        def head_cols(hh):
            return slice(hh * HEAD_PAD, (hh + 1) * HEAD_PAD)

        def softmax_step(s, kvv, state, first_row=None):
            m, acc = state
            if first_row is not None:
                s = jnp.where(col <= row + first_row, s, -jnp.inf)
            m_new = jnp.maximum(m, jnp.max(s, axis=-1, keepdims=True))
            p = jnp.exp(s - m_new)
            acc = jnp.exp(m - m_new) * acc + _mm(p.astype(BF16), kvv)
            return m_new, acc

        def finish(state):
            m, acc = state
            l = jnp.sum(jnp.where(lane == 0, acc, 0.0), axis=-1, keepdims=True)
            return acc / l, jnp.broadcast_to(m + jnp.log(l), (bq, LANES))

        def qblock(i, carry):
            start = pl.multiple_of(i * blk, blk)
            rows = [pl.ds(pl.multiple_of(start + a * bq, bq), bq) for a in range(n_sub)]
            qs = {(hh, a): q_ref[rows[a], head_cols(hh)] for hh, a in chains}

            def scores(j):
                ks = pl.ds(pl.multiple_of(j * blk, blk), blk)
                return tuple(_mm_nt(qs[hh, a], k_ref[ks, head_cols(hh)]) for hh, a in chains)

            def kstep(j, carried, diagonal=False):
                ss, states = carried
                ss_next = ss if diagonal else scores(j + 1)
                ks = pl.ds(pl.multiple_of(j * blk, blk), blk)
                new = tuple(softmax_step(s, kv_ref[ks, head_cols(hh)], st, a * bq if diagonal else None)
                            for (hh, a), s, st in zip(chains, ss, states))
                return ss_next, new

            init = (jnp.full((bq, 1), -jnp.inf, F32), jnp.zeros((bq, LANES), F32))
            carried = lax.fori_loop(0, i, kstep, (scores(0), (init,) * len(chains)))
            _, states = kstep(i, carried, diagonal=True)
            for a in range(n_sub):
                (o0, lse0), (o1, lse1) = finish(states[chains.index((0, a))]), finish(states[chains.index((1, a))])
                o_ref[rows[a], :] = jnp.where(lane < V_HEAD, pltpu.roll(o0, V_HEAD, 1), o1)
                lse_ref[rows[a], head_cols(0)] = lse0
                lse_ref[rows[a], head_cols(1)] = lse1
            return carry

        lax.fori_loop(0, nb, qblock, 0)

    heads = pl.BlockSpec((t_len, 2 * HEAD_PAD), lambda h: (0, h))
    pair = pl.BlockSpec((t_len, 2 * V_HEAD), lambda h: (0, h))
    out_shape = (jax.ShapeDtypeStruct((t_len, ATTN_WIDTH), F32), jax.ShapeDtypeStruct((t_len, N_HEADS * LANES), F32))
    (attn, lse), gathered = _call_hosting_gather("attn_fwd", body, N_HEADS // 2, [heads, heads, heads], (pair, heads),
                                                 out_shape, [], (qf, kf, kv), gather)
    return attn, lse, gathered


def _call_hosting_gather(name, body, n_steps, in_specs, out_specs, out_shape, scratch_shapes, args, gather):
    if not gather:
        return pl.pallas_call(body, name=name, grid=(n_steps,), in_specs=list(in_specs), out_specs=tuple(out_specs),
                              out_shape=tuple(out_shape), scratch_shapes=list(scratch_shapes),
                              compiler_params=_cparams(("arbitrary",)))(*args), ()
    n_in, n_out, n_g = len(in_specs), len(out_shape), len(gather)

    def hosting_body(*refs):
        ins, refs = refs[:n_in], refs[n_in:]
        x_refs = [r.at[layer] for r, (_, layer) in zip(refs[:n_g], gather)]
        outs, landing, sems, scratch = (refs[n_g:n_g + n_out], refs[n_g + n_out:2 * n_g + n_out],
                                        refs[2 * n_g + n_out:2 * n_g + n_out + 3], refs[2 * n_g + n_out + 3:])
        start, forward, finish = _gather_phases(x_refs, landing, *sems)
        step = pl.program_id(0)
        pl.when(step == 0)(start)
        pl.when(step == n_steps - 1)(forward)
        body(*ins, *outs, *scratch)
        pl.when(step == n_steps - 1)(finish)

    res = pl.pallas_call(hosting_body, name=name + "_gather", grid=(n_steps,), in_specs=list(in_specs) + _any_specs(n_g),
                         out_specs=tuple(out_specs) + tuple(_any_specs(n_g)),
                         out_shape=tuple(out_shape) + _gather_out_shape([s[layer] for s, layer in gather]),
                         scratch_shapes=_gather_semaphores(n_g) + list(scratch_shapes),
                         compiler_params=_cparams(("arbitrary",)))(*args, *[s for s, _ in gather])
    return res[:n_out], res[n_out:]


def _mix_out_fwd(x, attn, conv, w):
    t_len, d = x.shape
    tm = min(TOKEN_TILE, t_len)

    def body(x_ref, a_ref, c_ref, goa, goc, wo, x1_ref):
        mixed = jnp.concatenate([_rms(a_ref[...], goa[...]), _rms(c_ref[...], goc[...])], axis=1)
        x1_ref[...] = x_ref[...] + _mm(mixed.astype(BF16), wo[...])

    consts, const_specs = _operands([w["g_out_attn"], w["g_out_conv"], w["w_o"]])
    return pl.pallas_call(body, name="mix_out_fwd", grid=(t_len // tm,),
                          in_specs=[_rows(tm, d), _rows(tm, ATTN_WIDTH), _rows(tm, CONV_WIDTH)] + const_specs,
                          out_specs=_rows(tm, d), out_shape=jax.ShapeDtypeStruct((t_len, d), F32),
                          compiler_params=_cparams(("parallel",)))(x, attn, conv, *consts)


def _mlp_fwd(x1, w, gather=None):
    t_len, d = x1.shape
    tm = min(TOKEN_TILE, t_len)

    def body(x_ref, g, wup, wdn, x2_ref):
        x1v = x_ref[...]
        hb = _rms(x1v, g[...]).astype(BF16)
        acc = x1v
        for k in range(N_DEV):
            a = jnp.maximum(_mm(hb, wup[k]), 0.0)
            acc = acc + _mm((a * a).astype(BF16), wdn[k])
        x2_ref[...] = acc

    consts, const_specs = _operands([w["g_mlp"], w["w_up"], w["w_down"]])
    (x2,), gathered = _call_hosting_gather("mlp_fwd", body, t_len // tm, [_rows(tm, d)] + const_specs, (_rows(tm, d),),
                                           (jax.ShapeDtypeStruct((t_len, d), F32),), [], (x1, *consts), gather)
    return x2, gathered


def _ple_fwd(x2, p, w):
    t_len, d = x2.shape
    tm = min(TOKEN_TILE, t_len)

    def body(x_ref, p_ref, g, wg, wp, x3_ref):
        x2v = x_ref[...]
        gate = jax.nn.sigmoid(_mm(_rms(x2v, g[...]).astype(BF16), wg[...]))
        pb = p_ref[...].astype(BF16)
        e = jnp.concatenate([_mm(pb, wp[k]) for k in range(N_DEV)], axis=1)
        x3_ref[...] = x2v + gate * e

    consts, const_specs = _operands([w["g_ple"], w["w_ple_gate"], w["w_ple"]])
    return pl.pallas_call(body, name="ple_fwd", grid=(t_len // tm,),
                          in_specs=[_rows(tm, d), _rows(tm, p.shape[1])] + const_specs, out_specs=_rows(tm, d),
                          out_shape=jax.ShapeDtypeStruct((t_len, d), F32),
                          compiler_params=_cparams(("parallel",)))(x2, p, *consts)


def _loss_and_grad(y, target):
    t_len, d = y.shape
    tm = min(TOKEN_TILE, t_len)

    def body(y_ref, t_ref, sq_ref, dy_ref):
        err = y_ref[...] - t_ref[...]
        dy_ref[...] = err / d
        total = jnp.sum(jnp.sum(err * err, axis=0, keepdims=True), axis=1, keepdims=True)
        _accumulate(sq_ref, pl.program_id(0) == 0, jnp.broadcast_to(total, (HALO, LANES)))

    return pl.pallas_call(body, name="loss_grad", grid=(t_len // tm,), in_specs=[_rows(tm, d), _rows(tm, d)],
                          out_specs=(_whole((HALO, LANES)), _rows(tm, d)),
                          out_shape=(jax.ShapeDtypeStruct((HALO, LANES), F32), jax.ShapeDtypeStruct((t_len, d), F32)),
                          compiler_params=_cparams(("arbitrary",)))(y, target)


def _ple_bwd(dx3, x2, p, w):
    t_len, d = x2.shape
    tm = min(TOKEN_TILE, t_len)

    def body(dx3_ref, x_ref, p_ref, g, wg, wp, dx2_ref, de_ref, h3_ref, dpre_ref, dg_ref):
        x2v, dx3v = x_ref[...], dx3_ref[...]
        hb = _rms(x2v, g[...]).astype(BF16)
        h3_ref[...] = hb
        gate = jax.nn.sigmoid(_mm(hb, wg[...]))
        pb = p_ref[...].astype(BF16)
        e = jnp.concatenate([_mm(pb, wp[k]) for k in range(N_DEV)], axis=1)
        de_ref[...] = (dx3v * gate).astype(BF16)
        dpre = ((dx3v * e) * gate * (1.0 - gate)).astype(BF16)
        dpre_ref[...] = dpre
        dx, dg = _rms_bwd(x2v, g[...], _mm_nt(dpre, wg[...]))
        dx2_ref[...] = dx3v + dx
        _accumulate(dg_ref, pl.program_id(0) == 0, dg)

    consts, const_specs = _operands([w["g_ple"], w["w_ple_gate"], w["w_ple"]])
    out_shape = (jax.ShapeDtypeStruct((t_len, d), F32), jax.ShapeDtypeStruct((t_len, d), BF16),
                 jax.ShapeDtypeStruct((t_len, d), BF16), jax.ShapeDtypeStruct((t_len, d), BF16),
                 jax.ShapeDtypeStruct((1, d), F32))
    return pl.pallas_call(body, name="ple_bwd", grid=(t_len // tm,),
                          in_specs=[_rows(tm, d), _rows(tm, d), _rows(tm, p.shape[1])] + const_specs,
                          out_specs=(_rows(tm, d),) * 4 + (_whole((1, d)),), out_shape=out_shape,
                          compiler_params=_cparams(("arbitrary",)))(dx3, x2, p, *consts)


def _mlp_bwd(dx2, x1, w):
    t_len, d = x1.shape
    tm = min(TOKEN_TILE, t_len)
    fc = w["w_up"].shape[2]
    ff = N_DEV * fc

    def body(dx2_ref, x_ref, g, wup, wdn, dx1_ref, r_ref, da_ref, h2_ref, dg_ref):
        x1v, dx2v = x_ref[...], dx2_ref[...]
        hb = _rms(x1v, g[...]).astype(BF16)
        h2_ref[...] = hb
        dxb = dx2v.astype(BF16)
        dh = jnp.zeros((tm, d), F32)
        for k in range(N_DEV):
            a = jnp.maximum(_mm(hb, wup[k]), 0.0)
            r_ref[:, k * fc:(k + 1) * fc] = (a * a).astype(BF16)
            da = (_mm_nt(dxb, wdn[k]) * (2.0 * a)).astype(BF16)
            da_ref[:, k * fc:(k + 1) * fc] = da
            dh = dh + _mm_nt(da, wup[k])
        dx, dg = _rms_bwd(x1v, g[...], dh)
        dx1_ref[...] = dx2v + dx
        _accumulate(dg_ref, pl.program_id(0) == 0, dg)

    consts, const_specs = _operands([w["g_mlp"], w["w_up"], w["w_down"]])
    out_shape = (jax.ShapeDtypeStruct((t_len, d), F32), jax.ShapeDtypeStruct((t_len, ff), BF16),
                 jax.ShapeDtypeStruct((t_len, ff), BF16), jax.ShapeDtypeStruct((t_len, d), BF16),
                 jax.ShapeDtypeStruct((1, d), F32))
    return pl.pallas_call(body, name="mlp_bwd", grid=(t_len // tm,), in_specs=[_rows(tm, d), _rows(tm, d)] + const_specs,
                          out_specs=(_rows(tm, d), _rows(tm, ff), _rows(tm, ff), _rows(tm, d), _whole((1, d))),
                          out_shape=out_shape, compiler_params=_cparams(("arbitrary",)))(dx2, x1, *consts)


def _mix_out_bwd(dx1, attn, conv, w):
    t_len, d = dx1.shape
    tm = min(TOKEN_TILE, t_len)

    def body(dx1_ref, a_ref, c_ref, goa, goc, wo, mixed_ref, da_ref, dc_ref, dgoa_ref, dgoc_ref):
        av, cv = a_ref[...], c_ref[...]
        mixed_ref[...] = jnp.concatenate([_rms(av, goa[...]), _rms(cv, goc[...])], axis=1).astype(BF16)
        dmixed = _mm_nt(dx1_ref[...].astype(BF16), wo[...])
        da, dga = _rms_bwd(av, goa[...], dmixed[:, :ATTN_WIDTH])
        dc, dgc = _rms_bwd(cv, goc[...], dmixed[:, ATTN_WIDTH:])
        da_ref[...] = da
        dc_ref[...] = dc
        first = pl.program_id(0) == 0
        _accumulate(dgoa_ref, first, dga)
        _accumulate(dgoc_ref, first, dgc)

    consts, const_specs = _operands([w["g_out_attn"], w["g_out_conv"], w["w_o"]])
    out_shape = (jax.ShapeDtypeStruct((t_len, d), BF16), jax.ShapeDtypeStruct((t_len, ATTN_WIDTH), F32),
                 jax.ShapeDtypeStruct((t_len, CONV_WIDTH), F32), jax.ShapeDtypeStruct((1, ATTN_WIDTH), F32),
                 jax.ShapeDtypeStruct((1, CONV_WIDTH), F32))
    out_specs = (_rows(tm, d), _rows(tm, ATTN_WIDTH), _rows(tm, CONV_WIDTH), _whole((1, ATTN_WIDTH)),
                 _whole((1, CONV_WIDTH)))
    return pl.pallas_call(body, name="mix_out_bwd", grid=(t_len // tm,),
                          in_specs=[_rows(tm, d), _rows(tm, ATTN_WIDTH), _rows(tm, CONV_WIDTH)] + const_specs,
                          out_specs=out_specs, out_shape=out_shape,
                          compiler_params=_cparams(("arbitrary",)))(dx1, attn, conv, *consts)


def _attn_bwd(qf, kf, kv, o, do, lse, scatter):
    t_len = qf.shape[0]
    blk = min(ATTN_BLOCK, t_len)
    nb = t_len // blk
    n_sub = ATTN_KV_SUB if nb % ATTN_KV_SUB == 0 else 1
    reps = blk // LANES

    def body(q_ref, k_ref, kv_ref, o_ref, do_ref, lse_ref, dq_ref, dk_ref, dkv_ref, delta_ref, dob_ref):
        hd = pl.program_id(0)
        lane = lax.broadcasted_iota(jnp.int32, (blk, LANES), 1)
        even = (lane * 0 + hd % 2) == 0
        mine = jnp.where(lane < V_HEAD, 0, 1) == hd % 2
        row = lax.broadcasted_iota(jnp.int32, (blk, blk), 0)
        col = lax.broadcasted_iota(jnp.int32, (blk, blk), 1)
        dq_ref[...] = jnp.zeros_like(dq_ref)

        def prepare(i, carry):
            qs = pl.ds(pl.multiple_of(i * blk, blk), blk)
            dov = do_ref[qs, :]
            prod = jnp.where(mine, dov * o_ref[qs, :], 0.0)
            delta_ref[qs, :] = jnp.broadcast_to(jnp.sum(prod, axis=-1, keepdims=True), (blk, LANES))
            moved = jnp.where(even, pltpu.roll(dov, V_HEAD, 1), dov)
            dob_ref[qs, :] = jnp.where(lane >= V_HEAD, moved, 0.0).astype(BF16)
            return carry
        lax.fori_loop(0, nb, prepare, 0)

        def kvblock(jj, carry):
            base = jj * n_sub
            kss = [pl.ds(pl.multiple_of((base + a) * blk, blk), blk) for a in range(n_sub)]
            k = [k_ref[ks, :] for ks in kss]
            kvv = [kv_ref[ks, :] for ks in kss]

            def products(i):
                qs = pl.ds(pl.multiple_of(i * blk, blk), blk)
                q, dob = q_ref[qs, :], dob_ref[qs, :]
                return tuple((_mm_nt(q, k[a]), _mm_nt(dob, kvv[a])) for a in range(n_sub))

            def qstep(i, raw, accs, kinds):
                qs = pl.ds(pl.multiple_of(i * blk, blk), blk)
                q = q_ref[qs, :]
                dob = dob_ref[qs, :]
                lse_t = jnp.concatenate([lse_ref[qs, :]] * reps, axis=1)
                delta_t = jnp.concatenate([delta_ref[qs, :]] * reps, axis=1)
                new, dq_add = [], None
                for a in range(n_sub):
                    if kinds[a] is None:
                        new.append(accs[a])
                        continue
                    dk_acc, dv_acc = accs[a]
                    s, dp = raw[a]
                    if kinds[a]:
                        s = jnp.where(col <= row, s, -jnp.inf)
                    p = jnp.exp(s - lse_t)
                    ds = (p * (dp - delta_t)).astype(BF16)
                    new.append((dk_acc + _mm_tn(ds, q), dv_acc + _mm_tn(p.astype(BF16), dob)))
                    part = _mm(ds, k[a])
                    dq_add = part if dq_add is None else dq_add + part
                dq_ref[qs, :] += dq_add
                return tuple(new)

            zero = jnp.zeros((blk, LANES), F32)
            accs = ((zero, zero),) * n_sub
            for b in range(n_sub):
                accs = qstep(base + b, products(base + b), accs, tuple((a == b) if a <= b else None for a in range(n_sub)))

            def pipelined(i, carried):
                raw, acc = carried
                return products(jnp.minimum(i + 1, nb - 1)), qstep(i, raw, acc, (False,) * n_sub)

            first = base + n_sub
            _, accs = lax.fori_loop(first, nb, pipelined, (products(jnp.minimum(first, nb - 1)), accs))
            for a in range(n_sub):
                dk_ref[kss[a], :] = accs[a][0]
                dkv_ref[kss[a], :] = accs[a][1]
            return carry
        lax.fori_loop(0, nb // n_sub, kvblock, 0)

    n_sc = len(scatter[0])

    def hosting_body(*refs):
        ins, rest = refs[:6], refs[6 + 2 * n_sc:]
        parts = refs[6:6 + n_sc]
        outs, landed, sems, scratch = rest[:3], rest[3:3 + n_sc], rest[3 + n_sc:6 + n_sc], rest[6 + n_sc:]
        start, finish = _scatter_phases(parts, landed, *sems, scatter[2])
        hd = pl.program_id(0)
        pl.when(hd == 0)(start)
        body(*ins, *outs, *scratch)
        pl.when(hd == N_HEADS - 1)(finish)

    head = pl.BlockSpec((t_len, HEAD_PAD), lambda h: (0, h))
    pair = pl.BlockSpec((t_len, 2 * V_HEAD), lambda h: (0, h // 2))
    out = jax.ShapeDtypeStruct((t_len, N_HEADS * HEAD_PAD), F32)
    vmem_scratch = [pltpu.VMEM((t_len, LANES), F32), pltpu.VMEM((t_len, LANES), BF16)]
    res = pl.pallas_call(hosting_body, name="attn_bwd_scatter", grid=(N_HEADS,),
                         in_specs=[head, head, head, pair, pair, head] + _any_specs(2 * n_sc),
                         out_specs=(head, head, head) + tuple(_any_specs(n_sc)),
                         out_shape=(out, out, out) + _same_shapes(scatter[1]),
                         scratch_shapes=_scatter_semaphores(n_sc) + vmem_scratch,
                         input_output_aliases={6 + n_sc + a: 3 + a for a in range(n_sc)},
                         compiler_params=_cparams(("arbitrary",)))(qf, kf, kv, o, do, lse, *scatter[0], *scatter[1])
    return res[0], res[1], res[2], res[3:]


def _front_bwd(x, z, dx1, dqf, dkf, dkv_in, dconv, w, tabs):
    t_len, d = x.shape
    tm = min(TOKEN_TILE, t_len)
    nt = t_len // tm
    hb_per_tile = tm // HALO
    n_halo = t_len // HALO
    hp = N_HEADS * HEAD_PAD

    def body(x_ref, z_ref, zp_ref, zn_ref, dx1_ref, dqf_ref, dkf_ref, dkv_ref, dc_ref, dcn_ref,
             gmix, win, gq, wuq, gkv, wukv, gqn, gkn, cw_ref, gm_ref, cos_ref, sin_ref,
             dx_ref, dz_ref, h_ref, qn_ref, kvn_ref, dqr_ref, dkvr_ref,
             dgmix_ref, dgq_ref, dgkv_ref, dgqn_ref, dgkn_ref, dcw_ref, ubuf, dybuf):
        i = pl.program_id(0)
        first = i == 0
        xv, zv = x_ref[...], z_ref[...]
        hb = _rms(xv, gmix[...]).astype(BF16)
        h_ref[...] = hb
        zq, zkv = zv[:, Z_Q[0]:Z_Q[1]], zv[:, Z_KV[0]:Z_KV[1]]
        qnb = _rms(zq, gq[...]).astype(BF16)
        qn_ref[...] = qnb
        kvb = _rms(zkv, gkv[...]).astype(BF16)
        kvn_ref[...] = kvb
        kpe = zv[:, Z_KPE[0]:Z_KPE[1]]
        kpe = jnp.concatenate([kpe] * PAIR, axis=1)
        cos, sin, gm = cos_ref[...], sin_ref[...], gm_ref[...]
        width = PAIR * HEAD_PAD
        lane = lax.broadcasted_iota(jnp.int32, (tm, width), 1) & (HEAD_PAD - 1)
        is_nope = lane < QK_NOPE
        is_rope = (lane >= QK_NOPE) & (lane < QK_HEAD)
        dkpe = jnp.zeros((tm, width), F32)
        dgqn = jnp.zeros((1, width), F32)
        dgkn = jnp.zeros((1, width), F32)
        dqn = jnp.zeros((tm, Q_LORA), F32)
        dkvn = jnp.zeros((tm, KV_LORA), F32)
        for pr in range(N_HEADS // PAIR):
            sl = slice(pr * width, (pr + 1) * width)
            dxq, dg = _qk_bwd(_mm(qnb, wuq[pr]), gqn[...], dqf_ref[:, sl] * ATTN_SCALE, cos, sin, gm, lane)
            dxq = dxq.astype(BF16)
            dqr_ref[:, sl] = dxq
            dqn = dqn + _mm_nt(dxq, wuq[pr])
            dgqn = dgqn + dg
            k_raw = jnp.where(is_nope, _mm(kvb, wukv[pr]), 0.0) + kpe
            dxk, dg = _qk_bwd(k_raw, gkn[...], dkf_ref[:, sl], cos, sin, gm, lane)
            dkv = jnp.where(is_nope, dxk, dkv_ref[:, sl]).astype(BF16)
            dkvr_ref[:, sl] = dkv
            dkvn = dkvn + _mm_nt(dkv, wukv[pr])
            dkpe = dkpe + jnp.where(is_rope, dxk, 0.0)
            dgkn = dgkn + dg
        dkpe = dkpe[:, :HEAD_PAD] + dkpe[:, HEAD_PAD:]
        _accumulate(dgqn_ref, first, dgqn[:, :HEAD_PAD] + dgqn[:, HEAD_PAD:])
        _accumulate(dgkn_ref, first, dgkn[:, :HEAD_PAD] + dgkn[:, HEAD_PAD:])
        dzq, dg = _rms_bwd(zq, gq[...], dqn)
        _accumulate(dgq_ref, first, dg)
        dzkv, dg = _rms_bwd(zkv, gkv[...], dkvn)
        _accumulate(dgkv_ref, first, dg)

        gb, gc, xin = zv[:, Z_GB[0]:Z_GB[1]], zv[:, Z_GC[0]:Z_GC[1]], zv[:, Z_XIN[0]:Z_XIN[1]]
        u = gc * xin
        dcv = dc_ref[...]
        dy = dcv * gb
        zp, zn = zp_ref[...], zn_ref[...]
        ubuf[0:HALO, :] = (zp[:, Z_GC[0]:Z_GC[1]] * zp[:, Z_XIN[0]:Z_XIN[1]]) * jnp.where(first, 0.0, 1.0)
        ubuf[HALO:HALO + tm, :] = u
        dybuf[0:tm, :] = dy
        dybuf[tm:tm + HALO, :] = (dcn_ref[...] * zn[:, Z_GB[0]:Z_GB[1]]) * jnp.where(i == nt - 1, 0.0, 1.0)
        cw = cw_ref[...]
        u1, u2 = ubuf[pl.ds(HALO - 1, tm), :], ubuf[pl.ds(HALO - 2, tm), :]
        y = cw[0:1] * u + cw[1:2] * u1 + cw[2:3] * u2
        du = cw[0:1] * dy + cw[1:2] * dybuf[pl.ds(1, tm), :] + cw[2:3] * dybuf[pl.ds(2, tm), :]
        dcw = jnp.concatenate([jnp.sum(dy * u, axis=0, keepdims=True), jnp.sum(dy * u1, axis=0, keepdims=True),
                               jnp.sum(dy * u2, axis=0, keepdims=True), jnp.zeros((HALO - 3, CONV_WIDTH), F32)], axis=0)
        _accumulate(dcw_ref, first, dcw)

        dz_ref[:, Z_Q[0]:Z_Q[1]] = dzq.astype(BF16)
        dz_ref[:, Z_KV[0]:Z_KV[1]] = dzkv.astype(BF16)
        dz_ref[:, Z_GB[0]:Z_GB[1]] = (dcv * y).astype(BF16)
        dz_ref[:, Z_GC[0]:Z_GC[1]] = (du * xin).astype(BF16)
        dz_ref[:, Z_XIN[0]:Z_XIN[1]] = (du * gc).astype(BF16)
        dz_ref[:, Z_KPE[0]:Z_KPE[1]] = dkpe.astype(BF16)
        dx, dg = _rms_bwd(xv, gmix[...], _mm_nt(dz_ref[...], win[...]))
        dx_ref[...] = dx1_ref[...] + dx
        _accumulate(dgmix_ref, first, dg)

    prev_halo = lambda n: pl.BlockSpec((HALO, n), lambda i: (jnp.maximum(i * hb_per_tile - 1, 0), 0))
    next_halo = lambda n: pl.BlockSpec((HALO, n), lambda i: (jnp.minimum((i + 1) * hb_per_tile, n_halo - 1), 0))
    consts, const_specs = _operands([w["g_mix"], w["w_in"], w["g_q_lat"], w["w_uq"], w["g_kv_lat"], w["w_ukv"],
                                     w["g_qn"], w["g_kn"], w["conv_w"], tabs["gm"]])
    in_specs = ([_rows(tm, d), _rows(tm, Z_COLS), prev_halo(Z_COLS), next_halo(Z_COLS), _rows(tm, d), _rows(tm, hp),
                 _rows(tm, hp), _rows(tm, hp), _rows(tm, CONV_WIDTH), next_halo(CONV_WIDTH)]
                + const_specs + [_rows(tm, PAIR * HEAD_PAD)] * 2)
    out_shape = (jax.ShapeDtypeStruct((t_len, d), F32), jax.ShapeDtypeStruct((t_len, Z_COLS), BF16),
                 jax.ShapeDtypeStruct((t_len, d), BF16), jax.ShapeDtypeStruct((t_len, Q_LORA), BF16),
                 jax.ShapeDtypeStruct((t_len, KV_LORA), BF16), jax.ShapeDtypeStruct((t_len, hp), BF16),
                 jax.ShapeDtypeStruct((t_len, hp), BF16),
                 jax.ShapeDtypeStruct((1, d), F32), jax.ShapeDtypeStruct((1, Q_LORA), F32),
                 jax.ShapeDtypeStruct((1, KV_LORA), F32), jax.ShapeDtypeStruct((1, LANES), F32),
                 jax.ShapeDtypeStruct((1, LANES), F32), jax.ShapeDtypeStruct((HALO, CONV_WIDTH), F32))
    out_specs = tuple(_rows(tm, s.shape[1]) for s in out_shape[:7]) + tuple(_whole(s.shape) for s in out_shape[7:])
    return pl.pallas_call(body, name="front_bwd", grid=(nt,), in_specs=in_specs, out_specs=out_specs, out_shape=out_shape,
                          scratch_shapes=[pltpu.VMEM((tm + HALO, CONV_WIDTH), F32), pltpu.VMEM((tm + HALO, CONV_WIDTH), F32)],
                          compiler_params=_cparams(("arbitrary",)))(
                              x, z, z, z, dx1, dqf, dkf, dkv_in, dconv, dconv, *consts, tabs["cos"], tabs["sin"])


def _wgrad(a, b, shard_cols=None, out_dtype=BF16):
    t_len, kk = a.shape
    nn = b.shape[1]
    tk = min(kk, WGRAD_TILE)
    tn = next(c for c in range(min(nn, WGRAD_TILE), 0, -LANES) if nn % c == 0 and c % (shard_cols or LANES) == 0)
    tt = min(t_len, WGRAD_TOKENS)
    nt = t_len // tt
    per_block = tn // shard_cols if shard_cols else 1

    def body(a_ref, b_ref, o_ref, acc):
        t = pl.program_id(2)

        @pl.when(t == 0)
        def _():
            acc[...] = jnp.zeros_like(acc)
        acc[...] += _mm_tn(a_ref[...].astype(BF16), b_ref[...].astype(BF16))

        @pl.when(t == nt - 1)
        def _():
            if shard_cols:
                for s in range(per_block):
                    o_ref[s] = acc[:, s * shard_cols:(s + 1) * shard_cols].astype(out_dtype)
            else:
                o_ref[...] = acc[...].astype(out_dtype)

    if shard_cols:
        out_shape = jax.ShapeDtypeStruct((nn // shard_cols, kk, shard_cols), out_dtype)
        out_spec = pl.BlockSpec((per_block, tk, shard_cols), lambda i, j, t: (j, i, 0))
    else:
        out_shape = jax.ShapeDtypeStruct((kk, nn), out_dtype)
        out_spec = pl.BlockSpec((tk, tn), lambda i, j, t: (i, j))
    return pl.pallas_call(body, name="wgrad", grid=(kk // tk, nn // tn, nt),
                          in_specs=[pl.BlockSpec((tt, tk), lambda i, j, t: (t, i)),
                                    pl.BlockSpec((tt, tn), lambda i, j, t: (t, j))],
                          out_specs=out_spec, out_shape=out_shape, scratch_shapes=[pltpu.VMEM((tk, tn), F32)],
                          compiler_params=_cparams(("parallel", "parallel", "arbitrary")))(a, b)


def _my_place():
    return lax.axis_index("x"), lax.axis_index("y"), lax.axis_index("c")


def _any_specs(n):
    return [pl.BlockSpec(memory_space=pl.ANY)] * n


def _all_gather(blocks):
    n = len(blocks)

    def body(*refs):
        start, forward, finish = _gather_phases(refs[:n], refs[n:2 * n], *refs[2 * n:])
        start()
        forward()
        finish()

    return pl.pallas_call(body, name="all_gather", out_shape=_gather_out_shape(blocks), in_specs=_any_specs(n),
                          out_specs=tuple(_any_specs(n)), scratch_shapes=_gather_semaphores(n))(*blocks)


def _gather_out_shape(blocks):
    return tuple(jax.ShapeDtypeStruct((N_DEV,) + b.shape, b.dtype) for b in blocks)


def _gather_semaphores(n):
    return [pltpu.SemaphoreType.DMA((n, 7)), pltpu.SemaphoreType.DMA((n, 7)), pltpu.SemaphoreType.DMA((n,))]


def _gather_phases(x_refs, out_refs, send_sems, recv_sems, local_sems):
    n = len(x_refs)
    x, y, c = _my_place()
    me, sibling = (x, y, c), (x, y, 1 - c)
    chips = [(1 - x, y), (x, 1 - y), (1 - x, 1 - y)]

    def slot(a, px, py, pc):
        return out_refs[a].at[4 * px + 2 * py + pc]

    def copy(a, k, blk, to, src=None):
        return pltpu.make_async_remote_copy(src_ref=slot(a, *blk) if src is None else src, dst_ref=slot(a, *blk),
                                            send_sem=send_sems.at[a, k], recv_sem=recv_sems.at[a, k],
                                            device_id=to, device_id_type=MESH)

    def own(a):
        return pltpu.make_async_copy(x_refs[a], slot(a, *me), local_sems.at[a])

    def first_hop(a):
        return [copy(a, 0, me, sibling, src=x_refs[a])] + [copy(a, 1 + j, me, (*chip, c), src=x_refs[a])
                                                           for j, chip in enumerate(chips)]

    def passed_on(a):
        return [copy(a, 4 + j, (*chip, c), sibling) for j, chip in enumerate(chips)]

    def start():
        for a in range(n):
            own(a).start()
        for a in range(n):
            for cp in first_hop(a):
                cp.start()

    def forward():
        for j, chip in enumerate(chips):
            for a in range(n):
                copy(a, 1 + j, (*chip, c), me).wait_recv()
                passed_on(a)[j].start()

    def finish():
        for a in range(n):
            copy(a, 0, sibling, me).wait_recv()
        for j, chip in enumerate(chips):
            for a in range(n):
                copy(a, 4 + j, (*chip, 1 - c), me).wait_recv()
        for a in range(n):
            for cp in first_hop(a) + passed_on(a):
                cp.wait_send()
            own(a).wait()

    return start, forward, finish


def _scatter_exchange(parts, landed, layers):
    n = len(parts)

    def body(*refs):
        start, finish = _scatter_phases(refs[:n], refs[2 * n:3 * n], *refs[3 * n:], layers)
        start()
        finish()

    return pl.pallas_call(body, name="scatter_exchange", out_shape=_same_shapes(landed), in_specs=_any_specs(2 * n),
                          out_specs=tuple(_any_specs(n)), scratch_shapes=_scatter_semaphores(n),
                          input_output_aliases={n + a: a for a in range(n)})(*parts, *landed)


def _same_shapes(arrays):
    return tuple(jax.ShapeDtypeStruct(a.shape, a.dtype) for a in arrays)


def _scatter_semaphores(n):
    return [pltpu.SemaphoreType.DMA((n, N_DEV - 1)), pltpu.SemaphoreType.DMA((n, N_DEV - 1)), pltpu.SemaphoreType.DMA((n,))]


def _scatter_phases(part_refs, landed_refs, send_sems, recv_sems, local_sems, layers):
    n = len(part_refs)
    x, y, c = _my_place()
    flips = [(0, 0, 1), (1, 0, 0), (0, 1, 0), (1, 1, 0), (1, 0, 1), (0, 1, 1), (1, 1, 1)]
    peers = [((1 - x) if fx else x, (1 - y) if fy else y, (1 - c) if fc else c) for fx, fy, fc in flips]
    my_k = 4 * x + 2 * y + c

    def index(peer):
        return 4 * peer[0] + 2 * peer[1] + peer[2]

    def send(a, r):
        return pltpu.make_async_remote_copy(src_ref=part_refs[a].at[index(peers[r])], dst_ref=landed_refs[a].at[my_k, layers[a]],
                                            send_sem=send_sems.at[a, r], recv_sem=recv_sems.at[a, r],
                                            device_id=peers[r], device_id_type=MESH)

    def arrival(a, r):
        return pltpu.make_async_remote_copy(src_ref=part_refs[a].at[my_k], dst_ref=landed_refs[a].at[index(peers[r]), layers[a]],
                                            send_sem=send_sems.at[a, r], recv_sem=recv_sems.at[a, r],
                                            device_id=peers[r], device_id_type=MESH)

    def own(a):
        return pltpu.make_async_copy(part_refs[a].at[my_k], landed_refs[a].at[my_k, layers[a]], local_sems.at[a])

    def start():
        for a in range(n):
            own(a).start()
        for r in range(len(peers)):
            for a in range(n):
                send(a, r).start()

    def finish():
        for r in range(len(peers)):
            for a in range(n):
                arrival(a, r).wait_recv()
        for r in range(len(peers)):
            for a in range(n):
                send(a, r).wait_send()
        for a in range(n):
            own(a).wait()

    return start, finish


def _row_block(rows):
    return ROW_BLOCK if rows % ROW_BLOCK == 0 else rows


def _sum_leading(parts):
    n_part, shape = parts.shape[0], parts.shape[1:]
    cols = shape[-1]
    p2 = parts.reshape(n_part, -1, cols)
    rows = p2.shape[1]
    rb = _row_block(rows)

    def body(p_ref, o_ref):
        acc = p_ref[0].astype(F32)
        for k in range(1, n_part):
            acc = acc + p_ref[k].astype(F32)
        o_ref[...] = acc

    out = pl.pallas_call(body, name="sum_leading", grid=(rows // rb,),
                         in_specs=[pl.BlockSpec((n_part, rb, cols), lambda i: (0, i, 0))], out_specs=_rows(rb, cols),
                         out_shape=jax.ShapeDtypeStruct((rows, cols), F32), compiler_params=_cparams(("parallel",)))(p2)
    return out.reshape(shape)


def _adamw(w, g, m, v):
    shape = w.shape
    two_d = (shape[0] * shape[1], shape[2]) if len(shape) == 3 else shape
    rows, cols = two_d
    rb = _row_block(rows)

    def body(w_ref, g_ref, m_ref, v_ref, d_ref, nm_ref, nv_ref):
        gv = g_ref[...]
        nm = ADAM_B1 * m_ref[...] + (1.0 - ADAM_B1) * gv
        nv = ADAM_B2 * v_ref[...] + (1.0 - ADAM_B2) * jnp.square(gv)
        m_hat = nm / (1.0 - ADAM_B1 ** ADAM_STEP)
        v_hat = nv / (1.0 - ADAM_B2 ** ADAM_STEP)
        d_ref[...] = -ADAM_LR * (m_hat / (jnp.sqrt(v_hat) + ADAM_EPS) + ADAM_WD * w_ref[...])
        nm_ref[...] = nm
        nv_ref[...] = nv

    spec = _rows(rb, cols)
    out = jax.ShapeDtypeStruct(two_d, F32)
    res = pl.pallas_call(body, name="adamw", grid=(rows // rb,), in_specs=[spec] * 4, out_specs=(spec,) * 3,
                         out_shape=(out,) * 3, compiler_params=_cparams(("parallel",)))(
                             *(a.reshape(two_d) for a in (w, g, m, v)))
    return tuple(a.reshape(shape) for a in res)


def _rope_tables(positions):
    t_len = positions.shape[0]
    inv_freq = 1.0 / (ROPE_THETA ** (jnp.arange(0, QK_ROPE, 2, dtype=F32) / QK_ROPE))
    ang = positions.astype(F32)[:, None] * inv_freq
    c, s = jnp.cos(ang), jnp.sin(ang)
    one, zero = jnp.ones((t_len, QK_NOPE), F32), jnp.zeros((t_len, QK_NOPE), F32)
    cos = jnp.concatenate([one, c, c, one[:, :LANES - QK_HEAD]], axis=1)
    sin = jnp.concatenate([zero, -s, s, zero[:, :LANES - QK_HEAD]], axis=1)
    idx = jnp.arange(PAIR * HEAD_PAD)
    lane, head = idx % HEAD_PAD, idx // HEAD_PAD
    grp = jnp.where(lane < QK_NOPE, 0, jnp.where(lane < QK_HEAD, 1, 2)) + 3 * head
    val = jnp.where(lane < QK_NOPE, 1.0 / QK_NOPE, jnp.where(lane < QK_HEAD, 1.0 / QK_ROPE, 0.0))
    gm = jnp.where(grp[:, None] == grp[None, :], val[None, :], 0.0).astype(BF16)
    return {"cos": jnp.concatenate([cos] * PAIR, axis=1), "sin": jnp.concatenate([sin] * PAIR, axis=1), "gm": gm}


def _head_gain(g_nope, g_rope):
    one = jnp.concatenate([g_nope, g_rope, jnp.zeros((HEAD_PAD - QK_HEAD,), F32)])
    return jnp.concatenate([one] * PAIR).reshape(1, PAIR * HEAD_PAD)


def _head_pairs(w):
    return jnp.concatenate([w[k::PAIR] for k in range(PAIR)], axis=2)


def _padded_w_in(shards):
    natural = jnp.concatenate([shards[k] for k in range(N_DEV)], axis=1)
    o2, o3 = Q_LORA + KV_LORA, Q_LORA + KV_LORA + QK_ROPE
    zeros = jnp.zeros((natural.shape[0], QK_NOPE), natural.dtype)
    return jnp.concatenate([natural[:, :o2], natural[:, o3:], zeros, natural[:, o2:o3], zeros[:, :LANES - QK_HEAD]], axis=1)


def _w_in_grad_shards(d_in):
    o2 = Q_LORA + KV_LORA
    nat = jnp.concatenate([d_in[:, :o2], d_in[:, Z_KPE[0] + QK_NOPE:Z_KPE[0] + QK_HEAD], d_in[:, o2:Z_XIN[1]]], axis=1)
    width = nat.shape[1] // N_DEV
    return jnp.stack([nat[:, k * width:(k + 1) * width] for k in range(N_DEV)])


def kernel(x, p, positions, g_mix, w_in, g_q_lat, w_uq, g_kv_lat, w_ukv, g_qn_nope, g_qn_rope, g_kn_nope, g_kn_rope, conv_w, g_out_attn, g_out_conv, w_o, g_mlp, w_up, w_down, g_ple, w_ple_gate, w_ple, loss_target, m_g_mix, m_w_in, m_g_q_lat, m_w_uq, m_g_kv_lat, m_w_ukv, m_g_qn_nope, m_g_qn_rope, m_g_kn_nope, m_g_kn_rope, m_conv_w, m_g_out_attn, m_g_out_conv, m_w_o, m_g_mlp, m_w_up, m_w_down, m_g_ple, m_w_ple_gate, m_w_ple, v_g_mix, v_w_in, v_g_q_lat, v_w_uq, v_g_kv_lat, v_w_ukv, v_g_qn_nope, v_g_qn_rope, v_g_kn_nope, v_g_kn_rope, v_conv_w, v_g_out_attn, v_g_out_conv, v_w_o, v_g_mlp, v_w_up, v_w_down, v_g_ple, v_w_ple_gate, v_w_ple):
    given = dict(locals())
    weights = {n: given[n] for n in WEIGHT_NAMES}
    gains = {n: given[n] for n in GAIN_NAMES}
    depth = w_in.shape[0]
    xs, target = x[0], loss_target[0]
    d_model = xs.shape[1]
    uq_cols = w_uq.shape[2]
    n_taps = conv_w.shape[1]

    mat_names = [n for n in SHARD_NAMES if n != "conv_w"]
    local = [weights[n].astype(BF16) for n in mat_names]
    local[1] = jnp.pad(local[1], ((0, 0), (0, 0), (0, HEAD_PAD - uq_cols)))
    local = dict(zip(mat_names, local))
    front_names = ("w_in", "w_uq", "w_ukv")
    first = _all_gather([local[n][0] for n in front_names] + [conv_w])
    conv_full = jnp.transpose(first[-1], (1, 2, 0, 3)).reshape(depth, n_taps, -1)
    tabs = _rope_tables(positions[0])

    def front_weights(layer, full):
        lw = {n: gains[n][layer].reshape(1, -1) for n in GAIN_NAMES}
        lw.update({"w_in": _padded_w_in(full["w_in"]), "w_uq": _head_pairs(full["w_uq"]),
                   "w_ukv": _head_pairs(full["w_ukv"]),
                   "conv_w": jnp.pad(conv_full[layer], ((0, HALO - n_taps), (0, 0))),
                   "g_qn": _head_gain(g_qn_nope[layer], g_qn_rope[layer]),
                   "g_kn": _head_gain(g_kn_nope[layer], g_kn_rope[layer])})
        return lw

    def rest_weights(full):
        return {"w_ple": full["w_ple"], "w_up": full["w_up"], "w_down": full["w_down"],
                "w_o": full["w_o"].reshape(d_model, d_model), "w_ple_gate": full["w_ple_gate"].reshape(d_model, d_model)}

    saved, layer_w = [], []
    cur = xs
    gathered = dict(zip(front_names, first[:-1]))
    mlp_names = ("w_up", "w_down")
    for layer in range(depth):
        w = front_weights(layer, gathered)
        z, qf, kf, kv, conv = _front_fwd(cur, w, tabs)
        lots = [[n for n in mat_names if n in mlp_names], [n for n in mat_names if n not in mlp_names]]
        behind_attn, behind_mlp = (lots[1], lots[0]) if layer == 0 else lots
        if layer + 1 == depth:
            behind_attn, behind_mlp = [], []
        wanted = [(n, 0) for n in mat_names if n not in front_names] if layer == 0 else []
        wanted += [(n, layer + 1) for n in behind_attn]
        attn, lse, got = _attn_fwd(qf, kf, kv, gather=[(local[n], at) for n, at in wanted])
        coming = {}
        for (n, at), g in zip(wanted, got):
            (gathered if at == layer else coming)[n] = g
        w.update(rest_weights(gathered))
        layer_w.append(w)
        x1 = _mix_out_fwd(cur, attn, conv, w)
        x2, got = _mlp_fwd(x1, w, gather=[(local[n], layer + 1) for n in behind_mlp])
        coming.update(zip(behind_mlp, got))
        gathered = coming
        x3 = _ple_fwd(x2, p[layer, 0], w)
        saved.append(dict(x=cur, z=z, qf=qf, kf=kf, kv=kv, conv=conv, attn=attn, lse=lse, x1=x1, x2=x2))
        cur = x3

    sq, dx = _loss_and_grad(cur, target)
    loss = lax.psum(0.5 / d_model * sq[0, 0], ("x", "y", "c"))

    landed = {n: lax.empty((N_DEV, depth) + weights[n].shape[1:], BF16) for n in SHARD_NAMES}
    gain_grads = [None] * depth
    late = {}
    for layer in reversed(range(depth)):
        w, s = layer_w[layer], saved[layer]
        pl_in = p[layer, 0]
        dx2, de, h3, dpre, dg_ple = _ple_bwd(dx, s["x2"], pl_in, w)
        dx1, r, da, h2, dg_mlp = _mlp_bwd(dx2, s["x1"], w)
        mixed, dattn, dconv, dg_oa, dg_oc = _mix_out_bwd(dx1, s["attn"], s["conv"], w)
        sending = {"w_o": (_wgrad(mixed, dx1).reshape((N_DEV,) + w_o.shape[1:]), layer),
                   "w_up": (_wgrad(h2, da, shard_cols=w_up.shape[2]), layer),
                   "w_down": (_wgrad(r, dx2).reshape((N_DEV,) + w_down.shape[1:]), layer),
                   "w_ple_gate": (_wgrad(h3, dpre).reshape((N_DEV,) + w_ple_gate.shape[1:]), layer),
                   "w_ple": (_wgrad(pl_in, de, shard_cols=w_ple.shape[2]), layer), **late}
        names = list(sending)
        dqf, dkf, dkv, got = _attn_bwd(s["qf"], s["kf"], s["kv"], s["attn"], dattn, s["lse"],
                                       scatter=([sending[n][0] for n in names], [landed[n] for n in names],
                                                [sending[n][1] for n in names]))
        landed.update(zip(names, got))
        (dx0, dz, hb, qn, kvn, dqr, dkvr, dg_mix, dg_q, dg_kv, dg_qn, dg_kn, dcw) = _front_bwd(
            s["x"], s["z"], dx1, dqf, dkf, dkv, dconv, w, tabs)
        late = {"w_in": (_w_in_grad_shards(_wgrad(hb, dz, out_dtype=F32)).astype(BF16), layer),
                "w_uq": (_wgrad(qn, dqr, shard_cols=HEAD_PAD)[..., :uq_cols], layer),
                "w_ukv": (_wgrad(kvn, dkvr, shard_cols=HEAD_PAD), layer),
                "conv_w": (jnp.transpose(dcw[:n_taps].reshape(n_taps, N_DEV, -1), (1, 0, 2)).astype(BF16), layer)}
        gain_grads[layer] = jnp.concatenate([
            dg_mix[0], dg_q[0], dg_kv[0], dg_qn[0, :QK_NOPE], dg_qn[0, QK_NOPE:QK_HEAD], dg_kn[0, :QK_NOPE],
            dg_kn[0, QK_NOPE:QK_HEAD], dg_oa[0], dg_oc[0], dg_mlp[0], dg_ple[0]])
        dx = dx0
    names = list(late)
    landed.update(zip(names, _scatter_exchange([late[n][0] for n in names], [landed[n] for n in names],
                                               [late[n][1] for n in names])))
    grads = {n: _sum_leading(landed[n]) for n in SHARD_NAMES}

    gg = jnp.stack(gain_grads)
    gg_rows = -(-gg.size // (HALO * LANES)) * HALO
    gg_pad = jnp.pad(gg.reshape(-1), (0, gg_rows * LANES - gg.size)).reshape(gg_rows, LANES)
    gg_sum = _sum_leading(_all_gather([gg_pad])[0]).reshape(-1)[:gg.size].reshape(gg.shape)
    off = 0
    for n in GAIN_NAMES:
        width = gains[n].shape[1]
        grads[n] = gg_sum[:, off:off + width]
        off += width

    deltas, new_m, new_v = {}, {}, {}
    for n in WEIGHT_NAMES:
        deltas[n], new_m[n], new_v[n] = _adamw(weights[n], grads[n], given["m_" + n], given["v_" + n])
    return (loss, dx[None], *[grads[n] for n in WEIGHT_NAMES], *[deltas[n] for n in WEIGHT_NAMES],
            *[new_m[n] for n in WEIGHT_NAMES], *[new_v[n] for n in WEIGHT_NAMES])
```

```python
import jax
import jax.numpy as jnp
from jax import lax
from jax.experimental import pallas as pl
from jax.experimental.pallas import tpu as pltpu

F32 = jnp.float32
BF16 = jnp.bfloat16
MESH = pl.DeviceIdType.MESH

N_HEADS = 8
QK_NOPE = 64
QK_ROPE = 32
QK_HEAD = QK_NOPE + QK_ROPE
V_HEAD = 64
HEAD_PAD = 128
PAIR = 2
ATTN_SCALE = QK_HEAD ** -0.5
Q_LORA = 384
KV_LORA = 256
CONV_WIDTH = 512
ATTN_WIDTH = N_HEADS * V_HEAD
ROPE_THETA = 10000.0
EPS = 1e-6
ADAM_LR, ADAM_B1, ADAM_B2, ADAM_EPS, ADAM_WD, ADAM_STEP = 0.001, 0.9, 0.999, 1e-08, 0.01, 10

Z_Q = (0, 384)
Z_KV = (384, 640)
Z_GB = (640, 1152)
Z_GC = (1152, 1664)
Z_XIN = (1664, 2176)
Z_KPE = (2176, 2304)
Z_COLS = 2304

N_DEV = 8
LANES = 128
V7X_VMEM_LIMIT = 52 * 1024 * 1024
TOKEN_TILE = 256
LIGHT_TOKEN_TILE = 512
ATTN_BLOCK = 256
ATTN_FWD_BLOCK = 512
ATTN_Q_SUB = 2
ATTN_KV_SUB = 2
ROW_BLOCK = 512
WGRAD_TOKENS = 2048
WGRAD_TILE = 1024
HALO = 8

GAIN_NAMES = ("g_mix", "g_q_lat", "g_kv_lat", "g_qn_nope", "g_qn_rope", "g_kn_nope", "g_kn_rope",
              "g_out_attn", "g_out_conv", "g_mlp", "g_ple")
SHARD_NAMES = ("w_in", "w_uq", "w_ukv", "conv_w", "w_o", "w_up", "w_down", "w_ple_gate", "w_ple")
WEIGHT_NAMES = ("g_mix", "w_in", "g_q_lat", "w_uq", "g_kv_lat", "w_ukv", "g_qn_nope", "g_qn_rope", "g_kn_nope",
                "g_kn_rope", "conv_w", "g_out_attn", "g_out_conv", "w_o", "g_mlp", "w_up", "w_down", "g_ple",
                "w_ple_gate", "w_ple")


def _cparams(semantics=None):
    return pltpu.CompilerParams(dimension_semantics=semantics, vmem_limit_bytes=V7X_VMEM_LIMIT)


def _mm(a, b):
    return jnp.dot(a, b, preferred_element_type=F32)


def _mm_nt(a, b):
    return lax.dot_general(a, b, (((1,), (1,)), ((), ())), preferred_element_type=F32)


def _mm_tn(a, b):
    return lax.dot_general(a, b, (((0,), (0,)), ((), ())), preferred_element_type=F32)


def _rms(x, g):
    r = lax.rsqrt(jnp.mean(x * x, axis=-1, keepdims=True) + EPS)
    return (x * r) * g


def _rms_bwd(x, g, dy):
    r = lax.rsqrt(jnp.mean(x * x, axis=-1, keepdims=True) + EPS)
    xh = x * r
    dg = jnp.sum(dy * xh, axis=0, keepdims=True)
    dyg = dy * g
    dx = r * (dyg - xh * jnp.mean(dyg * xh, axis=-1, keepdims=True))
    return dx, dg


def _group_mean(t, gm):
    hi = t.astype(BF16)
    lo = (t - hi.astype(F32)).astype(BF16)
    return _mm(hi, gm) + _mm(lo, gm)


def _swap_rope_halves(x, lane):
    half = QK_ROPE // 2
    swapped = jnp.where(lane < QK_NOPE + half, pltpu.roll(x, x.shape[1] - half, 1), pltpu.roll(x, half, 1))
    return jnp.where((lane >= QK_NOPE) & (lane < QK_HEAD), swapped, 0.0)


def _qk_fwd(x, g, cos, sin, gm, lane):
    r = lax.rsqrt(_group_mean(x * x, gm) + EPS)
    n = (x * r) * g
    return n * cos + _swap_rope_halves(n, lane) * sin


def _qk_bwd(x, g, dy, cos, sin, gm, lane):
    r = lax.rsqrt(_group_mean(x * x, gm) + EPS)
    xh = x * r
    dn = dy * cos + _swap_rope_halves(dy * sin, lane)
    dg = jnp.sum(dn * xh, axis=0, keepdims=True)
    dng = dn * g
    dx = r * (dng - xh * _group_mean(dng * xh, gm))
    return dx, dg


def _row_shards_joined(ref):
    n, rows, cols = ref.shape
    return ref[...].reshape(n * rows, cols)


def _rows(tm, n):
    return pl.BlockSpec((tm, n), lambda i: (i, 0))


def _whole(shape):
    zeros = (0,) * len(shape)
    return pl.BlockSpec(shape, lambda i: zeros)


def _operands(arrays):
    return list(arrays), [_whole(a.shape) for a in arrays]


def _accumulate(ref, first, value):
    @pl.when(first)
    def _():
        ref[...] = jnp.zeros_like(ref)
    ref[...] += value


def _front_fwd(x, w, tabs):
    t_len, d = x.shape
    tm = min(TOKEN_TILE, t_len)
    hp = N_HEADS * HEAD_PAD

    def body(x_ref, gmix, win, gq, wuq, gkv, wukv, gqn, gkn, cw_ref, gm_ref, cos_ref, sin_ref,
             z_ref, qf_ref, kf_ref, kv_ref, conv_ref, ubuf):
        i = pl.program_id(0)
        h = _rms(x_ref[...], gmix[...])
        z = _mm(h.astype(BF16), win[...])
        z_ref[...] = z
        qnb = _rms(z[:, Z_Q[0]:Z_Q[1]], gq[...]).astype(BF16)
        kvb = _rms(z[:, Z_KV[0]:Z_KV[1]], gkv[...]).astype(BF16)
        kpe = z[:, Z_KPE[0]:Z_KPE[1]]
        kpe = jnp.concatenate([kpe] * PAIR, axis=1)
        cos, sin, gm = cos_ref[...], sin_ref[...], gm_ref[...]
        lane = lax.broadcasted_iota(jnp.int32, (tm, PAIR * HEAD_PAD), 1) & (HEAD_PAD - 1)
        for pr in range(N_HEADS // PAIR):
            sl = slice(pr * PAIR * HEAD_PAD, (pr + 1) * PAIR * HEAD_PAD)
            qf_ref[:, sl] = (_qk_fwd(_mm(qnb, wuq[pr]), gqn[...], cos, sin, gm, lane) * ATTN_SCALE).astype(BF16)
            kv = _mm(kvb, wukv[pr])
            kv_ref[:, sl] = jnp.where(lane < QK_NOPE, jnp.where(lane == 0, 1.0, 0.0), kv).astype(BF16)
            kf_ref[:, sl] = _qk_fwd(jnp.where(lane < QK_NOPE, kv, 0.0) + kpe, gkn[...], cos, sin, gm, lane).astype(BF16)
        u = z[:, Z_GC[0]:Z_GC[1]] * z[:, Z_XIN[0]:Z_XIN[1]]

        @pl.when(i == 0)
        def _():
            ubuf[0:HALO, :] = jnp.zeros((HALO, CONV_WIDTH), F32)
        ubuf[HALO:HALO + tm, :] = u
        cw = cw_ref[...]
        y = cw[0:1] * u + cw[1:2] * ubuf[pl.ds(HALO - 1, tm), :] + cw[2:3] * ubuf[pl.ds(HALO - 2, tm), :]
        conv_ref[...] = z[:, Z_GB[0]:Z_GB[1]] * y
        ubuf[0:HALO, :] = u[tm - HALO:tm, :]

    consts, const_specs = _operands([w["g_mix"], w["w_in"], w["g_q_lat"], w["w_uq"], w["g_kv_lat"], w["w_ukv"],
                                     w["g_qn"], w["g_kn"], w["conv_w"], tabs["gm"]])
    out_shape = (jax.ShapeDtypeStruct((t_len, Z_COLS), F32), jax.ShapeDtypeStruct((t_len, hp), BF16),
                 jax.ShapeDtypeStruct((t_len, hp), BF16), jax.ShapeDtypeStruct((t_len, hp), BF16),
                 jax.ShapeDtypeStruct((t_len, CONV_WIDTH), F32))
    return pl.pallas_call(body, name="front_fwd", grid=(t_len // tm,),
                          in_specs=[_rows(tm, d)] + const_specs + [_rows(tm, PAIR * HEAD_PAD)] * 2,
                          out_specs=tuple(_rows(tm, s.shape[1]) for s in out_shape), out_shape=out_shape,
                          scratch_shapes=[pltpu.VMEM((tm + HALO, CONV_WIDTH), F32)],
                          compiler_params=_cparams(("arbitrary",)))(x, *consts, tabs["cos"], tabs["sin"])


def _attn_fwd(qf, kf, kv, gather=None):
    t_len = qf.shape[0]
    blk = min(ATTN_FWD_BLOCK, t_len)
    nb = t_len // blk
    n_sub = ATTN_Q_SUB
    bq = blk // n_sub
    chains = [(hh, a) for hh in range(2) for a in range(n_sub)]

    def body(q_ref, k_ref, kv_ref, o_ref, lse_ref):
        lane = lax.broadcasted_iota(jnp.int32, (bq, LANES), 1)
        row = lax.broadcasted_iota(jnp.int32, (bq, blk), 0)
        col = lax.broadcasted_iota(jnp.int32, (bq, blk), 1)

        def head_cols(hh):
            return slice(hh * HEAD_PAD, (hh + 1) * HEAD_PAD)

        def softmax_step(s, kvv, state, first_row=None):
            m, acc = state
            if first_row is not None:
                s = jnp.where(col <= row + first_row, s, -jnp.inf)
            m_new = jnp.maximum(m, jnp.max(s, axis=-1, keepdims=True))
            p = jnp.exp(s - m_new)
            acc = jnp.exp(m - m_new) * acc + _mm(p.astype(BF16), kvv)
            return m_new, acc

        def finish(state):
            m, acc = state
            l = jnp.sum(jnp.where(lane == 0, acc, 0.0), axis=-1, keepdims=True)
            return acc / l, jnp.broadcast_to(m + jnp.log(l), (bq, LANES))

        def qblock(i, carry):
            start = pl.multiple_of(i * blk, blk)
            rows = [pl.ds(pl.multiple_of(start + a * bq, bq), bq) for a in range(n_sub)]
            qs = {(hh, a): q_ref[rows[a], head_cols(hh)] for hh, a in chains}

            def scores(j):
                ks = pl.ds(pl.multiple_of(j * blk, blk), blk)
                return tuple(_mm_nt(qs[hh, a], k_ref[ks, head_cols(hh)]) for hh, a in chains)

            def kstep(j, carried, diagonal=False):
                ss, states = carried
                ss_next = ss if diagonal else scores(j + 1)
                ks = pl.ds(pl.multiple_of(j * blk, blk), blk)
                new = tuple(softmax_step(s, kv_ref[ks, head_cols(hh)], st, a * bq if diagonal else None)
                            for (hh, a), s, st in zip(chains, ss, states))
                return ss_next, new

            init = (jnp.full((bq, 1), -jnp.inf, F32), jnp.zeros((bq, LANES), F32))
            carried = lax.fori_loop(0, i, kstep, (scores(0), (init,) * len(chains)))
            _, states = kstep(i, carried, diagonal=True)
            for a in range(n_sub):
                (o0, lse0), (o1, lse1) = finish(states[chains.index((0, a))]), finish(states[chains.index((1, a))])
                o_ref[rows[a], :] = jnp.where(lane < V_HEAD, pltpu.roll(o0, V_HEAD, 1), o1)
                lse_ref[rows[a], head_cols(0)] = lse0
                lse_ref[rows[a], head_cols(1)] = lse1
            return carry

        lax.fori_loop(0, nb, qblock, 0)

    heads = pl.BlockSpec((t_len, 2 * HEAD_PAD), lambda h: (0, h))
    pair = pl.BlockSpec((t_len, 2 * V_HEAD), lambda h: (0, h))
    out_shape = (jax.ShapeDtypeStruct((t_len, ATTN_WIDTH), F32), jax.ShapeDtypeStruct((t_len, N_HEADS * LANES), F32))
    (attn, lse), gathered = _call_hosting_gather("attn_fwd", body, N_HEADS // 2, [heads, heads, heads], (pair, heads),
                                                 out_shape, [], (qf, kf, kv), gather)
    return attn, lse, gathered


def _call_hosting_gather(name, body, n_steps, in_specs, out_specs, out_shape, scratch_shapes, args, gather):
    if not gather:
        return pl.pallas_call(body, name=name, grid=(n_steps,), in_specs=list(in_specs), out_specs=tuple(out_specs),
                              out_shape=tuple(out_shape), scratch_shapes=list(scratch_shapes),
                              compiler_params=_cparams(("arbitrary",)))(*args), ()
    n_in, n_out, n_g = len(in_specs), len(out_shape), len(gather)

    def hosting_body(*refs):
        ins, refs = refs[:n_in], refs[n_in:]
        x_refs = [r.at[layer] for r, (_, layer) in zip(refs[:n_g], gather)]
        outs, landing, sems, scratch = (refs[n_g:n_g + n_out], refs[n_g + n_out:2 * n_g + n_out],
                                        refs[2 * n_g + n_out:2 * n_g + n_out + 3], refs[2 * n_g + n_out + 3:])
        start, forward, finish = _gather_phases(x_refs, landing, *sems)
        step = pl.program_id(0)
        pl.when(step == 0)(start)
        pl.when(step == n_steps - 1)(forward)
        body(*ins, *outs, *scratch)
        pl.when(step == n_steps - 1)(finish)

    res = pl.pallas_call(hosting_body, name=name + "_gather", grid=(n_steps,), in_specs=list(in_specs) + _any_specs(n_g),
                         out_specs=tuple(out_specs) + tuple(_any_specs(n_g)),
                         out_shape=tuple(out_shape) + _gather_out_shape([s[layer] for s, layer in gather]),
                         scratch_shapes=_gather_semaphores(n_g) + list(scratch_shapes),
                         compiler_params=_cparams(("arbitrary",)))(*args, *[s for s, _ in gather])
    return res[:n_out], res[n_out:]


def _mix_out_fwd(x, attn, conv, w):
    t_len, d = x.shape
    tm = min(LIGHT_TOKEN_TILE, t_len)

    def body(x_ref, a_ref, c_ref, goa, goc, wo, x1_ref):
        mixed = jnp.concatenate([_rms(a_ref[...], goa[...]), _rms(c_ref[...], goc[...])], axis=1)
        x1_ref[...] = x_ref[...] + _mm(mixed.astype(BF16), _row_shards_joined(wo))

    consts, const_specs = _operands([w["g_out_attn"], w["g_out_conv"], w["w_o"]])
    return pl.pallas_call(body, name="mix_out_fwd", grid=(t_len // tm,),
                          in_specs=[_rows(tm, d), _rows(tm, ATTN_WIDTH), _rows(tm, CONV_WIDTH)] + const_specs,
                          out_specs=_rows(tm, d), out_shape=jax.ShapeDtypeStruct((t_len, d), F32),
                          compiler_params=_cparams(("parallel",)))(x, attn, conv, *consts)


def _mlp_fwd(x1, w, gather=None):
    t_len, d = x1.shape
    tm = min(TOKEN_TILE, t_len)

    def body(x_ref, g, wup, wdn, x2_ref):
        x1v = x_ref[...]
        hb = _rms(x1v, g[...]).astype(BF16)
        acc = x1v
        for k in range(N_DEV):
            a = jnp.maximum(_mm(hb, wup[k]), 0.0)
            acc = acc + _mm((a * a).astype(BF16), wdn[k])
        x2_ref[...] = acc

    consts, const_specs = _operands([w["g_mlp"], w["w_up"], w["w_down"]])
    (x2,), gathered = _call_hosting_gather("mlp_fwd", body, t_len // tm, [_rows(tm, d)] + const_specs, (_rows(tm, d),),
                                           (jax.ShapeDtypeStruct((t_len, d), F32),), [], (x1, *consts), gather)
    return x2, gathered


def _ple_fwd(x2, p, w):
    t_len, d = x2.shape
    tm = min(LIGHT_TOKEN_TILE, t_len)

    def body(x_ref, p_ref, g, wg, wp, x3_ref):
        x2v = x_ref[...]
        gate = jax.nn.sigmoid(_mm(_rms(x2v, g[...]).astype(BF16), _row_shards_joined(wg)))
        pb = p_ref[...].astype(BF16)
        e = jnp.concatenate([_mm(pb, wp[k]) for k in range(N_DEV)], axis=1)
        x3_ref[...] = x2v + gate * e

    consts, const_specs = _operands([w["g_ple"], w["w_ple_gate"], w["w_ple"]])
    return pl.pallas_call(body, name="ple_fwd", grid=(t_len // tm,),
                          in_specs=[_rows(tm, d), _rows(tm, p.shape[1])] + const_specs, out_specs=_rows(tm, d),
                          out_shape=jax.ShapeDtypeStruct((t_len, d), F32),
                          compiler_params=_cparams(("parallel",)))(x2, p, *consts)


def _loss_and_grad(y, target):
    t_len, d = y.shape
    tm = min(TOKEN_TILE, t_len)

    def body(y_ref, t_ref, sq_ref, dy_ref):
        err = y_ref[...] - t_ref[...]
        dy_ref[...] = err / d
        total = jnp.sum(jnp.sum(err * err, axis=0, keepdims=True), axis=1, keepdims=True)
        _accumulate(sq_ref, pl.program_id(0) == 0, jnp.broadcast_to(total, (HALO, LANES)))

    return pl.pallas_call(body, name="loss_grad", grid=(t_len // tm,), in_specs=[_rows(tm, d), _rows(tm, d)],
                          out_specs=(_whole((HALO, LANES)), _rows(tm, d)),
                          out_shape=(jax.ShapeDtypeStruct((HALO, LANES), F32), jax.ShapeDtypeStruct((t_len, d), F32)),
                          compiler_params=_cparams(("arbitrary",)))(y, target)


def _ple_bwd(dx3, x2, p, w):
    t_len, d = x2.shape
    tm = min(LIGHT_TOKEN_TILE, t_len)

    def body(dx3_ref, x_ref, p_ref, g, wg, wp, dx2_ref, de_ref, h3_ref, dpre_ref, dg_ref):
        x2v, dx3v = x_ref[...], dx3_ref[...]
        hb = _rms(x2v, g[...]).astype(BF16)
        h3_ref[...] = hb
        w_gate = _row_shards_joined(wg)
        gate = jax.nn.sigmoid(_mm(hb, w_gate))
        pb = p_ref[...].astype(BF16)
        e = jnp.concatenate([_mm(pb, wp[k]) for k in range(N_DEV)], axis=1)
        de_ref[...] = (dx3v * gate).astype(BF16)
        dpre = ((dx3v * e) * gate * (1.0 - gate)).astype(BF16)
        dpre_ref[...] = dpre
        dx, dg = _rms_bwd(x2v, g[...], _mm_nt(dpre, w_gate))
        dx2_ref[...] = dx3v + dx
        _accumulate(dg_ref, pl.program_id(0) == 0, dg)

    consts, const_specs = _operands([w["g_ple"], w["w_ple_gate"], w["w_ple"]])
    out_shape = (jax.ShapeDtypeStruct((t_len, d), F32), jax.ShapeDtypeStruct((t_len, d), BF16),
                 jax.ShapeDtypeStruct((t_len, d), BF16), jax.ShapeDtypeStruct((t_len, d), BF16),
                 jax.ShapeDtypeStruct((1, d), F32))
    return pl.pallas_call(body, name="ple_bwd", grid=(t_len // tm,),
                          in_specs=[_rows(tm, d), _rows(tm, d), _rows(tm, p.shape[1])] + const_specs,
                          out_specs=(_rows(tm, d),) * 4 + (_whole((1, d)),), out_shape=out_shape,
                          compiler_params=_cparams(("arbitrary",)))(dx3, x2, p, *consts)


def _mlp_bwd(dx2, x1, w):
    t_len, d = x1.shape
    tm = min(TOKEN_TILE, t_len)
    fc = w["w_up"].shape[2]
    ff = N_DEV * fc

    def body(dx2_ref, x_ref, g, wup, wdn, dx1_ref, r_ref, da_ref, h2_ref, dg_ref):
        x1v, dx2v = x_ref[...], dx2_ref[...]
        hb = _rms(x1v, g[...]).astype(BF16)
        h2_ref[...] = hb
        dxb = dx2v.astype(BF16)
        dh = jnp.zeros((tm, d), F32)
        for k in range(N_DEV):
            a = jnp.maximum(_mm(hb, wup[k]), 0.0)
            r_ref[:, k * fc:(k + 1) * fc] = (a * a).astype(BF16)
            da = (_mm_nt(dxb, wdn[k]) * (2.0 * a)).astype(BF16)
            da_ref[:, k * fc:(k + 1) * fc] = da
            dh = dh + _mm_nt(da, wup[k])
        dx, dg = _rms_bwd(x1v, g[...], dh)
        dx1_ref[...] = dx2v + dx
        _accumulate(dg_ref, pl.program_id(0) == 0, dg)

    consts, const_specs = _operands([w["g_mlp"], w["w_up"], w["w_down"]])
    out_shape = (jax.ShapeDtypeStruct((t_len, d), F32), jax.ShapeDtypeStruct((t_len, ff), BF16),
                 jax.ShapeDtypeStruct((t_len, ff), BF16), jax.ShapeDtypeStruct((t_len, d), BF16),
                 jax.ShapeDtypeStruct((1, d), F32))
    return pl.pallas_call(body, name="mlp_bwd", grid=(t_len // tm,), in_specs=[_rows(tm, d), _rows(tm, d)] + const_specs,
                          out_specs=(_rows(tm, d), _rows(tm, ff), _rows(tm, ff), _rows(tm, d), _whole((1, d))),
                          out_shape=out_shape, compiler_params=_cparams(("arbitrary",)))(dx2, x1, *consts)


def _mix_out_bwd(dx1, attn, conv, w):
    t_len, d = dx1.shape
    tm = min(LIGHT_TOKEN_TILE, t_len)

    def body(dx1_ref, a_ref, c_ref, goa, goc, wo, mixed_ref, da_ref, dc_ref, dgoa_ref, dgoc_ref):
        av, cv = a_ref[...], c_ref[...]
        mixed_ref[...] = jnp.concatenate([_rms(av, goa[...]), _rms(cv, goc[...])], axis=1).astype(BF16)
        dmixed = _mm_nt(dx1_ref[...].astype(BF16), _row_shards_joined(wo))
        da, dga = _rms_bwd(av, goa[...], dmixed[:, :ATTN_WIDTH])
        dc, dgc = _rms_bwd(cv, goc[...], dmixed[:, ATTN_WIDTH:])
        da_ref[...] = da
        dc_ref[...] = dc
        first = pl.program_id(0) == 0
        _accumulate(dgoa_ref, first, dga)
        _accumulate(dgoc_ref, first, dgc)

    consts, const_specs = _operands([w["g_out_attn"], w["g_out_conv"], w["w_o"]])
    out_shape = (jax.ShapeDtypeStruct((t_len, d), BF16), jax.ShapeDtypeStruct((t_len, ATTN_WIDTH), F32),
                 jax.ShapeDtypeStruct((t_len, CONV_WIDTH), F32), jax.ShapeDtypeStruct((1, ATTN_WIDTH), F32),
                 jax.ShapeDtypeStruct((1, CONV_WIDTH), F32))
    out_specs = (_rows(tm, d), _rows(tm, ATTN_WIDTH), _rows(tm, CONV_WIDTH), _whole((1, ATTN_WIDTH)),
                 _whole((1, CONV_WIDTH)))
    return pl.pallas_call(body, name="mix_out_bwd", grid=(t_len // tm,),
                          in_specs=[_rows(tm, d), _rows(tm, ATTN_WIDTH), _rows(tm, CONV_WIDTH)] + const_specs,
                          out_specs=out_specs, out_shape=out_shape,
                          compiler_params=_cparams(("arbitrary",)))(dx1, attn, conv, *consts)


def _attn_bwd(qf, kf, kv, o, do, lse, scatter):
    t_len = qf.shape[0]
    blk = min(ATTN_BLOCK, t_len)
    nb = t_len // blk
    n_sub = ATTN_KV_SUB if nb % ATTN_KV_SUB == 0 else 1
    reps = blk // LANES

    def body(q_ref, k_ref, kv_ref, o_ref, do_ref, lse_ref, dq_ref, dk_ref, dkv_ref, delta_ref, dob_ref):
        hd = pl.program_id(0)
        lane = lax.broadcasted_iota(jnp.int32, (blk, LANES), 1)
        even = (lane * 0 + hd % 2) == 0
        mine = jnp.where(lane < V_HEAD, 0, 1) == hd % 2
        row = lax.broadcasted_iota(jnp.int32, (blk, blk), 0)
        col = lax.broadcasted_iota(jnp.int32, (blk, blk), 1)
        dq_ref[...] = jnp.zeros_like(dq_ref)

        def prepare(i, carry):
            qs = pl.ds(pl.multiple_of(i * blk, blk), blk)
            dov = do_ref[qs, :]
            prod = jnp.where(mine, dov * o_ref[qs, :], 0.0)
            delta_ref[qs, :] = jnp.broadcast_to(jnp.sum(prod, axis=-1, keepdims=True), (blk, LANES))
            moved = jnp.where(even, pltpu.roll(dov, V_HEAD, 1), dov)
            dob_ref[qs, :] = jnp.where(lane >= V_HEAD, moved, 0.0).astype(BF16)
            return carry
        lax.fori_loop(0, nb, prepare, 0)

        def kvblock(jj, carry):
            base = jj * n_sub
            kss = [pl.ds(pl.multiple_of((base + a) * blk, blk), blk) for a in range(n_sub)]
            k = [k_ref[ks, :] for ks in kss]
            kvv = [kv_ref[ks, :] for ks in kss]

            def products(i):
                qs = pl.ds(pl.multiple_of(i * blk, blk), blk)
                q, dob = q_ref[qs, :], dob_ref[qs, :]
                return tuple((_mm_nt(q, k[a]), _mm_nt(dob, kvv[a])) for a in range(n_sub))

            def qstep(i, raw, accs, kinds):
                qs = pl.ds(pl.multiple_of(i * blk, blk), blk)
                q = q_ref[qs, :]
                dob = dob_ref[qs, :]
                lse_t = jnp.concatenate([lse_ref[qs, :]] * reps, axis=1)
                delta_t = jnp.concatenate([delta_ref[qs, :]] * reps, axis=1)
                new, dq_add = [], None
                for a in range(n_sub):
                    if kinds[a] is None:
                        new.append(accs[a])
                        continue
                    dk_acc, dv_acc = accs[a]
                    s, dp = raw[a]
                    if kinds[a]:
                        s = jnp.where(col <= row, s, -jnp.inf)
                    p = jnp.exp(s - lse_t)
                    ds = (p * (dp - delta_t)).astype(BF16)
                    new.append((dk_acc + _mm_tn(ds, q), dv_acc + _mm_tn(p.astype(BF16), dob)))
                    part = _mm(ds, k[a])
                    dq_add = part if dq_add is None else dq_add + part
                dq_ref[qs, :] += dq_add
                return tuple(new)

            zero = jnp.zeros((blk, LANES), F32)
            accs = ((zero, zero),) * n_sub
            for b in range(n_sub):
                accs = qstep(base + b, products(base + b), accs, tuple((a == b) if a <= b else None for a in range(n_sub)))

            def pipelined(i, carried):
                raw, acc = carried
                return products(jnp.minimum(i + 1, nb - 1)), qstep(i, raw, acc, (False,) * n_sub)

            first = base + n_sub
            _, accs = lax.fori_loop(first, nb, pipelined, (products(jnp.minimum(first, nb - 1)), accs))
            for a in range(n_sub):
                dk_ref[kss[a], :] = accs[a][0]
                dkv_ref[kss[a], :] = accs[a][1]
            return carry
        lax.fori_loop(0, nb // n_sub, kvblock, 0)

    n_sc = len(scatter[0])

    def hosting_body(*refs):
        ins, rest = refs[:6], refs[6 + 2 * n_sc:]
        parts = refs[6:6 + n_sc]
        outs, landed, sems, scratch = rest[:3], rest[3:3 + n_sc], rest[3 + n_sc:6 + n_sc], rest[6 + n_sc:]
        start, finish = _scatter_phases(parts, landed, *sems, scatter[2])
        hd = pl.program_id(0)
        pl.when(hd == 0)(start)
        body(*ins, *outs, *scratch)
        pl.when(hd == N_HEADS - 1)(finish)

    head = pl.BlockSpec((t_len, HEAD_PAD), lambda h: (0, h))
    pair = pl.BlockSpec((t_len, 2 * V_HEAD), lambda h: (0, h // 2))
    out = jax.ShapeDtypeStruct((t_len, N_HEADS * HEAD_PAD), F32)
    vmem_scratch = [pltpu.VMEM((t_len, LANES), F32), pltpu.VMEM((t_len, LANES), BF16)]
    res = pl.pallas_call(hosting_body, name="attn_bwd_scatter", grid=(N_HEADS,),
                         in_specs=[head, head, head, pair, pair, head] + _any_specs(2 * n_sc),
                         out_specs=(head, head, head) + tuple(_any_specs(n_sc)),
                         out_shape=(out, out, out) + _same_shapes(scatter[1]),
                         scratch_shapes=_scatter_semaphores(n_sc) + vmem_scratch,
                         input_output_aliases={6 + n_sc + a: 3 + a for a in range(n_sc)},
                         compiler_params=_cparams(("arbitrary",)))(qf, kf, kv, o, do, lse, *scatter[0], *scatter[1])
    return res[0], res[1], res[2], res[3:]


def _front_bwd(x, z, dx1, dqf, dkf, dkv_in, dconv, w, tabs):
    t_len, d = x.shape
    tm = min(TOKEN_TILE, t_len)
    nt = t_len // tm
    hb_per_tile = tm // HALO
    n_halo = t_len // HALO
    hp = N_HEADS * HEAD_PAD

    def body(x_ref, z_ref, zp_ref, zn_ref, dx1_ref, dqf_ref, dkf_ref, dkv_ref, dc_ref, dcn_ref,
             gmix, win, gq, wuq, gkv, wukv, gqn, gkn, cw_ref, gm_ref, cos_ref, sin_ref,
             dx_ref, dz_ref, h_ref, qn_ref, kvn_ref, dqr_ref, dkvr_ref,
             dgmix_ref, dgq_ref, dgkv_ref, dgqn_ref, dgkn_ref, dcw_ref, ubuf, dybuf):
        i = pl.program_id(0)
        first = i == 0
        xv, zv = x_ref[...], z_ref[...]
        hb = _rms(xv, gmix[...]).astype(BF16)
        h_ref[...] = hb
        zq, zkv = zv[:, Z_Q[0]:Z_Q[1]], zv[:, Z_KV[0]:Z_KV[1]]
        qnb = _rms(zq, gq[...]).astype(BF16)
        qn_ref[...] = qnb
        kvb = _rms(zkv, gkv[...]).astype(BF16)
        kvn_ref[...] = kvb
        kpe = zv[:, Z_KPE[0]:Z_KPE[1]]
        kpe = jnp.concatenate([kpe] * PAIR, axis=1)
        cos, sin, gm = cos_ref[...], sin_ref[...], gm_ref[...]
        width = PAIR * HEAD_PAD
        lane = lax.broadcasted_iota(jnp.int32, (tm, width), 1) & (HEAD_PAD - 1)
        is_nope = lane < QK_NOPE
        is_rope = (lane >= QK_NOPE) & (lane < QK_HEAD)
        dkpe = jnp.zeros((tm, width), F32)
        dgqn = jnp.zeros((1, width), F32)
        dgkn = jnp.zeros((1, width), F32)
        dqn = jnp.zeros((tm, Q_LORA), F32)
        dkvn = jnp.zeros((tm, KV_LORA), F32)
        for pr in range(N_HEADS // PAIR):
            sl = slice(pr * width, (pr + 1) * width)
            dxq, dg = _qk_bwd(_mm(qnb, wuq[pr]), gqn[...], dqf_ref[:, sl] * ATTN_SCALE, cos, sin, gm, lane)
            dxq = dxq.astype(BF16)
            dqr_ref[:, sl] = dxq
            dqn = dqn + _mm_nt(dxq, wuq[pr])
            dgqn = dgqn + dg
            k_raw = jnp.where(is_nope, _mm(kvb, wukv[pr]), 0.0) + kpe
            dxk, dg = _qk_bwd(k_raw, gkn[...], dkf_ref[:, sl], cos, sin, gm, lane)
            dkv = jnp.where(is_nope, dxk, dkv_ref[:, sl]).astype(BF16)
            dkvr_ref[:, sl] = dkv
            dkvn = dkvn + _mm_nt(dkv, wukv[pr])
            dkpe = dkpe + jnp.where(is_rope, dxk, 0.0)
            dgkn = dgkn + dg
        dkpe = dkpe[:, :HEAD_PAD] + dkpe[:, HEAD_PAD:]
        _accumulate(dgqn_ref, first, dgqn[:, :HEAD_PAD] + dgqn[:, HEAD_PAD:])
        _accumulate(dgkn_ref, first, dgkn[:, :HEAD_PAD] + dgkn[:, HEAD_PAD:])
        dzq, dg = _rms_bwd(zq, gq[...], dqn)
        _accumulate(dgq_ref, first, dg)
        dzkv, dg = _rms_bwd(zkv, gkv[...], dkvn)
        _accumulate(dgkv_ref, first, dg)

        gb, gc, xin = zv[:, Z_GB[0]:Z_GB[1]], zv[:, Z_GC[0]:Z_GC[1]], zv[:, Z_XIN[0]:Z_XIN[1]]
        u = gc * xin
        dcv = dc_ref[...]
        dy = dcv * gb
        zp, zn = zp_ref[...], zn_ref[...]
        ubuf[0:HALO, :] = (zp[:, Z_GC[0]:Z_GC[1]] * zp[:, Z_XIN[0]:Z_XIN[1]]) * jnp.where(first, 0.0, 1.0)
        ubuf[HALO:HALO + tm, :] = u
        dybuf[0:tm, :] = dy
        dybuf[tm:tm + HALO, :] = (dcn_ref[...] * zn[:, Z_GB[0]:Z_GB[1]]) * jnp.where(i == nt - 1, 0.0, 1.0)
        cw = cw_ref[...]
        u1, u2 = ubuf[pl.ds(HALO - 1, tm), :], ubuf[pl.ds(HALO - 2, tm), :]
        y = cw[0:1] * u + cw[1:2] * u1 + cw[2:3] * u2
        du = cw[0:1] * dy + cw[1:2] * dybuf[pl.ds(1, tm), :] + cw[2:3] * dybuf[pl.ds(2, tm), :]
        dcw = jnp.concatenate([jnp.sum(dy * u, axis=0, keepdims=True), jnp.sum(dy * u1, axis=0, keepdims=True),
                               jnp.sum(dy * u2, axis=0, keepdims=True), jnp.zeros((HALO - 3, CONV_WIDTH), F32)], axis=0)
        _accumulate(dcw_ref, first, dcw)

        dz_ref[:, Z_Q[0]:Z_Q[1]] = dzq.astype(BF16)
        dz_ref[:, Z_KV[0]:Z_KV[1]] = dzkv.astype(BF16)
        dz_ref[:, Z_GB[0]:Z_GB[1]] = (dcv * y).astype(BF16)
        dz_ref[:, Z_GC[0]:Z_GC[1]] = (du * xin).astype(BF16)
        dz_ref[:, Z_XIN[0]:Z_XIN[1]] = (du * gc).astype(BF16)
        dz_ref[:, Z_KPE[0]:Z_KPE[1]] = dkpe.astype(BF16)
        dx, dg = _rms_bwd(xv, gmix[...], _mm_nt(dz_ref[...], win[...]))
        dx_ref[...] = dx1_ref[...] + dx
        _accumulate(dgmix_ref, first, dg)

    prev_halo = lambda n: pl.BlockSpec((HALO, n), lambda i: (jnp.maximum(i * hb_per_tile - 1, 0), 0))
    next_halo = lambda n: pl.BlockSpec((HALO, n), lambda i: (jnp.minimum((i + 1) * hb_per_tile, n_halo - 1), 0))
    consts, const_specs = _operands([w["g_mix"], w["w_in"], w["g_q_lat"], w["w_uq"], w["g_kv_lat"], w["w_ukv"],
                                     w["g_qn"], w["g_kn"], w["conv_w"], tabs["gm"]])
    in_specs = ([_rows(tm, d), _rows(tm, Z_COLS), prev_halo(Z_COLS), next_halo(Z_COLS), _rows(tm, d), _rows(tm, hp),
                 _rows(tm, hp), _rows(tm, hp), _rows(tm, CONV_WIDTH), next_halo(CONV_WIDTH)]
                + const_specs + [_rows(tm, PAIR * HEAD_PAD)] * 2)
    out_shape = (jax.ShapeDtypeStruct((t_len, d), F32), jax.ShapeDtypeStruct((t_len, Z_COLS), BF16),
                 jax.ShapeDtypeStruct((t_len, d), BF16), jax.ShapeDtypeStruct((t_len, Q_LORA), BF16),
                 jax.ShapeDtypeStruct((t_len, KV_LORA), BF16), jax.ShapeDtypeStruct((t_len, hp), BF16),
                 jax.ShapeDtypeStruct((t_len, hp), BF16),
                 jax.ShapeDtypeStruct((1, d), F32), jax.ShapeDtypeStruct((1, Q_LORA), F32),
                 jax.ShapeDtypeStruct((1, KV_LORA), F32), jax.ShapeDtypeStruct((1, LANES), F32),
                 jax.ShapeDtypeStruct((1, LANES), F32), jax.ShapeDtypeStruct((HALO, CONV_WIDTH), F32))
    out_specs = tuple(_rows(tm, s.shape[1]) for s in out_shape[:7]) + tuple(_whole(s.shape) for s in out_shape[7:])
    return pl.pallas_call(body, name="front_bwd", grid=(nt,), in_specs=in_specs, out_specs=out_specs, out_shape=out_shape,
                          scratch_shapes=[pltpu.VMEM((tm + HALO, CONV_WIDTH), F32), pltpu.VMEM((tm + HALO, CONV_WIDTH), F32)],
                          compiler_params=_cparams(("arbitrary",)))(
                              x, z, z, z, dx1, dqf, dkf, dkv_in, dconv, dconv, *consts, tabs["cos"], tabs["sin"])


def _wgrad(a, b, shard_cols=None, shard_rows=None, out_dtype=BF16):
    t_len, kk = a.shape
    nn = b.shape[1]
    tk = min(kk, WGRAD_TILE)
    tn = next(c for c in range(min(nn, WGRAD_TILE), 0, -LANES) if nn % c == 0 and c % (shard_cols or LANES) == 0)
    tt = min(t_len, WGRAD_TOKENS)
    nt = t_len // tt
    per_block = tn // shard_cols if shard_cols else tk // shard_rows if shard_rows else 1

    def body(a_ref, b_ref, o_ref, acc):
        t = pl.program_id(2)

        @pl.when(t == 0)
        def _():
            acc[...] = jnp.zeros_like(acc)
        acc[...] += _mm_tn(a_ref[...].astype(BF16), b_ref[...].astype(BF16))

        @pl.when(t == nt - 1)
        def _():
            if shard_cols:
                for s in range(per_block):
                    o_ref[s] = acc[:, s * shard_cols:(s + 1) * shard_cols].astype(out_dtype)
            elif shard_rows:
                for s in range(per_block):
                    o_ref[s] = acc[s * shard_rows:(s + 1) * shard_rows, :].astype(out_dtype)
            else:
                o_ref[...] = acc[...].astype(out_dtype)

    if shard_cols:
        out_shape = jax.ShapeDtypeStruct((nn // shard_cols, kk, shard_cols), out_dtype)
        out_spec = pl.BlockSpec((per_block, tk, shard_cols), lambda i, j, t: (j, i, 0))
    elif shard_rows:
        out_shape = jax.ShapeDtypeStruct((kk // shard_rows, shard_rows, nn), out_dtype)
        out_spec = pl.BlockSpec((per_block, shard_rows, tn), lambda i, j, t: (i, 0, j))
    else:
        out_shape = jax.ShapeDtypeStruct((kk, nn), out_dtype)
        out_spec = pl.BlockSpec((tk, tn), lambda i, j, t: (i, j))
    return pl.pallas_call(body, name="wgrad", grid=(kk // tk, nn // tn, nt),
                          in_specs=[pl.BlockSpec((tt, tk), lambda i, j, t: (t, i)),
                                    pl.BlockSpec((tt, tn), lambda i, j, t: (t, j))],
                          out_specs=out_spec, out_shape=out_shape, scratch_shapes=[pltpu.VMEM((tk, tn), F32)],
                          compiler_params=_cparams(("parallel", "parallel", "arbitrary")))(a, b)


def _my_place():
    return lax.axis_index("x"), lax.axis_index("y"), lax.axis_index("c")


def _any_specs(n):
    return [pl.BlockSpec(memory_space=pl.ANY)] * n


def _all_gather(blocks):
    n = len(blocks)

    def body(*refs):
        start, forward, finish = _gather_phases(refs[:n], refs[n:2 * n], *refs[2 * n:])
        start()
        forward()
        finish()

    return pl.pallas_call(body, name="all_gather", out_shape=_gather_out_shape(blocks), in_specs=_any_specs(n),
                          out_specs=tuple(_any_specs(n)), scratch_shapes=_gather_semaphores(n))(*blocks)


def _gather_out_shape(blocks):
    return tuple(jax.ShapeDtypeStruct((N_DEV,) + b.shape, b.dtype) for b in blocks)


def _gather_semaphores(n):
    return [pltpu.SemaphoreType.DMA((n, 7)), pltpu.SemaphoreType.DMA((n, 7)), pltpu.SemaphoreType.DMA((n,))]


def _gather_phases(x_refs, out_refs, send_sems, recv_sems, local_sems):
    n = len(x_refs)
    x, y, c = _my_place()
    me, sibling = (x, y, c), (x, y, 1 - c)
    chips = [(1 - x, y), (x, 1 - y), (1 - x, 1 - y)]

    def slot(a, px, py, pc):
        return out_refs[a].at[4 * px + 2 * py + pc]

    def copy(a, k, blk, to, src=None):
        return pltpu.make_async_remote_copy(src_ref=slot(a, *blk) if src is None else src, dst_ref=slot(a, *blk),
                                            send_sem=send_sems.at[a, k], recv_sem=recv_sems.at[a, k],
                                            device_id=to, device_id_type=MESH)

    def own(a):
        return pltpu.make_async_copy(x_refs[a], slot(a, *me), local_sems.at[a])

    def first_hop(a):
        return [copy(a, 0, me, sibling, src=x_refs[a])] + [copy(a, 1 + j, me, (*chip, c), src=x_refs[a])
                                                           for j, chip in enumerate(chips)]

    def passed_on(a):
        return [copy(a, 4 + j, (*chip, c), sibling) for j, chip in enumerate(chips)]

    def start():
        for a in range(n):
            own(a).start()
        for a in range(n):
            for cp in first_hop(a):
                cp.start()

    def forward():
        for j, chip in enumerate(chips):
            for a in range(n):
                copy(a, 1 + j, (*chip, c), me).wait_recv()
                passed_on(a)[j].start()

    def finish():
        for a in range(n):
            copy(a, 0, sibling, me).wait_recv()
        for j, chip in enumerate(chips):
            for a in range(n):
                copy(a, 4 + j, (*chip, 1 - c), me).wait_recv()
        for a in range(n):
            for cp in first_hop(a) + passed_on(a):
                cp.wait_send()
            own(a).wait()

    return start, forward, finish


def _scatter_exchange(parts, landed, layers):
    n = len(parts)

    def body(*refs):
        start, finish = _scatter_phases(refs[:n], refs[2 * n:3 * n], *refs[3 * n:], layers)
        start()
        finish()

    return pl.pallas_call(body, name="scatter_exchange", out_shape=_same_shapes(landed), in_specs=_any_specs(2 * n),
                          out_specs=tuple(_any_specs(n)), scratch_shapes=_scatter_semaphores(n),
                          input_output_aliases={n + a: a for a in range(n)})(*parts, *landed)


def _same_shapes(arrays):
    return tuple(jax.ShapeDtypeStruct(a.shape, a.dtype) for a in arrays)


def _scatter_semaphores(n):
    return [pltpu.SemaphoreType.DMA((n, N_DEV - 1)), pltpu.SemaphoreType.DMA((n, N_DEV - 1)), pltpu.SemaphoreType.DMA((n,))]


def _scatter_phases(part_refs, landed_refs, send_sems, recv_sems, local_sems, layers):
    n = len(part_refs)
    x, y, c = _my_place()
    flips = [(0, 0, 1), (1, 0, 0), (0, 1, 0), (1, 1, 0), (1, 0, 1), (0, 1, 1), (1, 1, 1)]
    peers = [((1 - x) if fx else x, (1 - y) if fy else y, (1 - c) if fc else c) for fx, fy, fc in flips]
    my_k = 4 * x + 2 * y + c

    def index(peer):
        return 4 * peer[0] + 2 * peer[1] + peer[2]

    def send(a, r):
        return pltpu.make_async_remote_copy(src_ref=part_refs[a].at[index(peers[r])], dst_ref=landed_refs[a].at[my_k, layers[a]],
                                            send_sem=send_sems.at[a, r], recv_sem=recv_sems.at[a, r],
                                            device_id=peers[r], device_id_type=MESH)

    def arrival(a, r):
        return pltpu.make_async_remote_copy(src_ref=part_refs[a].at[my_k], dst_ref=landed_refs[a].at[index(peers[r]), layers[a]],
                                            send_sem=send_sems.at[a, r], recv_sem=recv_sems.at[a, r],
                                            device_id=peers[r], device_id_type=MESH)

    def own(a):
        return pltpu.make_async_copy(part_refs[a].at[my_k], landed_refs[a].at[my_k, layers[a]], local_sems.at[a])

    def start():
        for a in range(n):
            own(a).start()
        for r in range(len(peers)):
            for a in range(n):
                send(a, r).start()

    def finish():
        for r in range(len(peers)):
            for a in range(n):
                arrival(a, r).wait_recv()
        for r in range(len(peers)):
            for a in range(n):
                send(a, r).wait_send()
        for a in range(n):
            own(a).wait()

    return start, finish


def _row_block(rows):
    return ROW_BLOCK if rows % ROW_BLOCK == 0 else rows


def _sum_leading(parts):
    n_part, shape = parts.shape[0], parts.shape[1:]
    cols = shape[-1]
    p2 = parts.reshape(n_part, -1, cols)
    rows = p2.shape[1]
    rb = _row_block(rows)

    def body(p_ref, o_ref):
        acc = p_ref[0].astype(F32)
        for k in range(1, n_part):
            acc = acc + p_ref[k].astype(F32)
        o_ref[...] = acc

    out = pl.pallas_call(body, name="sum_leading", grid=(rows // rb,),
                         in_specs=[pl.BlockSpec((n_part, rb, cols), lambda i: (0, i, 0))], out_specs=_rows(rb, cols),
                         out_shape=jax.ShapeDtypeStruct((rows, cols), F32), compiler_params=_cparams(("parallel",)))(p2)
    return out.reshape(shape)


def _adamw(w, g, m, v):
    shape = w.shape
    two_d = (shape[0] * shape[1], shape[2]) if len(shape) == 3 else shape
    rows, cols = two_d
    rb = _row_block(rows)

    def body(w_ref, g_ref, m_ref, v_ref, d_ref, nm_ref, nv_ref):
        gv = g_ref[...]
        nm = ADAM_B1 * m_ref[...] + (1.0 - ADAM_B1) * gv
        nv = ADAM_B2 * v_ref[...] + (1.0 - ADAM_B2) * jnp.square(gv)
        m_hat = nm / (1.0 - ADAM_B1 ** ADAM_STEP)
        v_hat = nv / (1.0 - ADAM_B2 ** ADAM_STEP)
        d_ref[...] = -ADAM_LR * (m_hat / (jnp.sqrt(v_hat) + ADAM_EPS) + ADAM_WD * w_ref[...])
        nm_ref[...] = nm
        nv_ref[...] = nv

    spec = _rows(rb, cols)
    out = jax.ShapeDtypeStruct(two_d, F32)
    res = pl.pallas_call(body, name="adamw", grid=(rows // rb,), in_specs=[spec] * 4, out_specs=(spec,) * 3,
                         out_shape=(out,) * 3, compiler_params=_cparams(("parallel",)))(
                             *(a.reshape(two_d) for a in (w, g, m, v)))
    return tuple(a.reshape(shape) for a in res)


def _rope_tables(positions):
    t_len = positions.shape[0]
    inv_freq = 1.0 / (ROPE_THETA ** (jnp.arange(0, QK_ROPE, 2, dtype=F32) / QK_ROPE))
    ang = positions.astype(F32)[:, None] * inv_freq
    c, s = jnp.cos(ang), jnp.sin(ang)
    one, zero = jnp.ones((t_len, QK_NOPE), F32), jnp.zeros((t_len, QK_NOPE), F32)
    cos = jnp.concatenate([one, c, c, one[:, :LANES - QK_HEAD]], axis=1)
    sin = jnp.concatenate([zero, -s, s, zero[:, :LANES - QK_HEAD]], axis=1)
    idx = jnp.arange(PAIR * HEAD_PAD)
    lane, head = idx % HEAD_PAD, idx // HEAD_PAD
    grp = jnp.where(lane < QK_NOPE, 0, jnp.where(lane < QK_HEAD, 1, 2)) + 3 * head
    val = jnp.where(lane < QK_NOPE, 1.0 / QK_NOPE, jnp.where(lane < QK_HEAD, 1.0 / QK_ROPE, 0.0))
    gm = jnp.where(grp[:, None] == grp[None, :], val[None, :], 0.0).astype(BF16)
    return {"cos": jnp.concatenate([cos] * PAIR, axis=1), "sin": jnp.concatenate([sin] * PAIR, axis=1), "gm": gm}


def _head_gain(g_nope, g_rope):
    one = jnp.concatenate([g_nope, g_rope, jnp.zeros((HEAD_PAD - QK_HEAD,), F32)])
    return jnp.concatenate([one] * PAIR).reshape(1, PAIR * HEAD_PAD)


def _head_pairs(w):
    return jnp.concatenate([w[k::PAIR] for k in range(PAIR)], axis=2)


def _padded_w_in(shards):
    natural = jnp.concatenate([shards[k] for k in range(N_DEV)], axis=1)
    o2, o3 = Q_LORA + KV_LORA, Q_LORA + KV_LORA + QK_ROPE
    zeros = jnp.zeros((natural.shape[0], QK_NOPE), natural.dtype)
    return jnp.concatenate([natural[:, :o2], natural[:, o3:], zeros, natural[:, o2:o3], zeros[:, :LANES - QK_HEAD]], axis=1)


def _w_in_grad_shards(d_in):
    o2 = Q_LORA + KV_LORA
    nat = jnp.concatenate([d_in[:, :o2], d_in[:, Z_KPE[0] + QK_NOPE:Z_KPE[0] + QK_HEAD], d_in[:, o2:Z_XIN[1]]], axis=1)
    width = nat.shape[1] // N_DEV
    return jnp.stack([nat[:, k * width:(k + 1) * width] for k in range(N_DEV)])


def kernel(x, p, positions, g_mix, w_in, g_q_lat, w_uq, g_kv_lat, w_ukv, g_qn_nope, g_qn_rope, g_kn_nope, g_kn_rope, conv_w, g_out_attn, g_out_conv, w_o, g_mlp, w_up, w_down, g_ple, w_ple_gate, w_ple, loss_target, m_g_mix, m_w_in, m_g_q_lat, m_w_uq, m_g_kv_lat, m_w_ukv, m_g_qn_nope, m_g_qn_rope, m_g_kn_nope, m_g_kn_rope, m_conv_w, m_g_out_attn, m_g_out_conv, m_w_o, m_g_mlp, m_w_up, m_w_down, m_g_ple, m_w_ple_gate, m_w_ple, v_g_mix, v_w_in, v_g_q_lat, v_w_uq, v_g_kv_lat, v_w_ukv, v_g_qn_nope, v_g_qn_rope, v_g_kn_nope, v_g_kn_rope, v_conv_w, v_g_out_attn, v_g_out_conv, v_w_o, v_g_mlp, v_w_up, v_w_down, v_g_ple, v_w_ple_gate, v_w_ple):
    given = dict(locals())
    weights = {n: given[n] for n in WEIGHT_NAMES}
    gains = {n: given[n] for n in GAIN_NAMES}
    depth = w_in.shape[0]
    xs, target = x[0], loss_target[0]
    d_model = xs.shape[1]
    uq_cols = w_uq.shape[2]
    n_taps = conv_w.shape[1]

    mat_names = [n for n in SHARD_NAMES if n != "conv_w"]
    local = [weights[n].astype(BF16) for n in mat_names]
    local[1] = jnp.pad(local[1], ((0, 0), (0, 0), (0, HEAD_PAD - uq_cols)))
    local = dict(zip(mat_names, local))
    front_names = ("w_in", "w_uq", "w_ukv")
    first = _all_gather([local[n][0] for n in front_names] + [conv_w])
    conv_full = jnp.transpose(first[-1], (1, 2, 0, 3)).reshape(depth, n_taps, -1)
    tabs = _rope_tables(positions[0])

    def front_weights(layer, full):
        lw = {n: gains[n][layer].reshape(1, -1) for n in GAIN_NAMES}
        lw.update({"w_in": _padded_w_in(full["w_in"]), "w_uq": _head_pairs(full["w_uq"]),
                   "w_ukv": _head_pairs(full["w_ukv"]),
                   "conv_w": jnp.pad(conv_full[layer], ((0, HALO - n_taps), (0, 0))),
                   "g_qn": _head_gain(g_qn_nope[layer], g_qn_rope[layer]),
                   "g_kn": _head_gain(g_kn_nope[layer], g_kn_rope[layer])})
        return lw

    def rest_weights(full):
        return {"w_ple": full["w_ple"], "w_up": full["w_up"], "w_down": full["w_down"],
                "w_o": full["w_o"], "w_ple_gate": full["w_ple_gate"]}

    saved, layer_w = [], []
    cur = xs
    gathered = dict(zip(front_names, first[:-1]))
    mlp_names = ("w_up", "w_down")
    for layer in range(depth):
        w = front_weights(layer, gathered)
        z, qf, kf, kv, conv = _front_fwd(cur, w, tabs)
        lots = [[n for n in mat_names if n in mlp_names], [n for n in mat_names if n not in mlp_names]]
        behind_attn, behind_mlp = (lots[1], lots[0]) if layer == 0 else lots
        if layer + 1 == depth:
            behind_attn, behind_mlp = [], []
        wanted = [(n, 0) for n in mat_names if n not in front_names] if layer == 0 else []
        wanted += [(n, layer + 1) for n in behind_attn]
        attn, lse, got = _attn_fwd(qf, kf, kv, gather=[(local[n], at) for n, at in wanted])
        coming = {}
        for (n, at), g in zip(wanted, got):
            (gathered if at == layer else coming)[n] = g
        w.update(rest_weights(gathered))
        layer_w.append(w)
        x1 = _mix_out_fwd(cur, attn, conv, w)
        x2, got = _mlp_fwd(x1, w, gather=[(local[n], layer + 1) for n in behind_mlp])
        coming.update(zip(behind_mlp, got))
        gathered = coming
        x3 = _ple_fwd(x2, p[layer, 0], w)
        saved.append(dict(x=cur, z=z, qf=qf, kf=kf, kv=kv, conv=conv, attn=attn, lse=lse, x1=x1, x2=x2))
        cur = x3

    sq, dx = _loss_and_grad(cur, target)
    loss = lax.psum(0.5 / d_model * sq[0, 0], ("x", "y", "c"))

    landed = {n: lax.empty((N_DEV, depth) + weights[n].shape[1:], BF16) for n in SHARD_NAMES}
    gain_grads = [None] * depth
    late = {}
    for layer in reversed(range(depth)):
        w, s = layer_w[layer], saved[layer]
        pl_in = p[layer, 0]
        dx2, de, h3, dpre, dg_ple = _ple_bwd(dx, s["x2"], pl_in, w)
        dx1, r, da, h2, dg_mlp = _mlp_bwd(dx2, s["x1"], w)
        mixed, dattn, dconv, dg_oa, dg_oc = _mix_out_bwd(dx1, s["attn"], s["conv"], w)
        sending = {"w_o": (_wgrad(mixed, dx1, shard_rows=w_o.shape[1]), layer),
                   "w_up": (_wgrad(h2, da, shard_cols=w_up.shape[2]), layer),
                   "w_down": (_wgrad(r, dx2, shard_rows=w_down.shape[1]), layer),
                   "w_ple_gate": (_wgrad(h3, dpre, shard_rows=w_ple_gate.shape[1]), layer),
                   "w_ple": (_wgrad(pl_in, de, shard_cols=w_ple.shape[2]), layer), **late}
        names = list(sending)
        dqf, dkf, dkv, got = _attn_bwd(s["qf"], s["kf"], s["kv"], s["attn"], dattn, s["lse"],
                                       scatter=([sending[n][0] for n in names], [landed[n] for n in names],
                                                [sending[n][1] for n in names]))
        landed.update(zip(names, got))
        (dx0, dz, hb, qn, kvn, dqr, dkvr, dg_mix, dg_q, dg_kv, dg_qn, dg_kn, dcw) = _front_bwd(
            s["x"], s["z"], dx1, dqf, dkf, dkv, dconv, w, tabs)
        late = {"w_in": (_w_in_grad_shards(_wgrad(hb, dz, out_dtype=F32)).astype(BF16), layer),
                "w_uq": (_wgrad(qn, dqr, shard_cols=HEAD_PAD)[..., :uq_cols], layer),
                "w_ukv": (_wgrad(kvn, dkvr, shard_cols=HEAD_PAD), layer),
                "conv_w": (jnp.transpose(dcw[:n_taps].reshape(n_taps, N_DEV, -1), (1, 0, 2)).astype(BF16), layer)}
        gain_grads[layer] = jnp.concatenate([
            dg_mix[0], dg_q[0], dg_kv[0], dg_qn[0, :QK_NOPE], dg_qn[0, QK_NOPE:QK_HEAD], dg_kn[0, :QK_NOPE],
            dg_kn[0, QK_NOPE:QK_HEAD], dg_oa[0], dg_oc[0], dg_mlp[0], dg_ple[0]])
        dx = dx0
    names = list(late)
    landed.update(zip(names, _scatter_exchange([late[n][0] for n in names], [landed[n] for n in names],
                                               [late[n][1] for n in names])))
    grads = {n: _sum_leading(landed[n]) for n in SHARD_NAMES}

    gg = jnp.stack(gain_grads)
    gg_rows = -(-gg.size // (HALO * LANES)) * HALO
    gg_pad = jnp.pad(gg.reshape(-1), (0, gg_rows * LANES - gg.size)).reshape(gg_rows, LANES)
    gg_sum = _sum_leading(_all_gather([gg_pad])[0]).reshape(-1)[:gg.size].reshape(gg.shape)
    off = 0
    for n in GAIN_NAMES:
        width = gains[n].shape[1]
        grads[n] = gg_sum[:, off:off + width]
        off += width

    deltas, new_m, new_v = {}, {}, {}
    for n in WEIGHT_NAMES:
        deltas[n], new_m[n], new_v[n] = _adamw(weights[n], grads[n], given["m_" + n], given["v_" + n])
    return (loss, dx[None], *[grads[n] for n in WEIGHT_NAMES], *[deltas[n] for n in WEIGHT_NAMES],
            *[new_m[n] for n in WEIGHT_NAMES], *[new_v[n] for n in WEIGHT_NAMES])
```

```python
import jax
import jax.numpy as jnp
from jax import lax
from jax.experimental import pallas as pl
from jax.experimental.pallas import tpu as pltpu

F32 = jnp.float32
BF16 = jnp.bfloat16
MESH = pl.DeviceIdType.MESH

N_HEADS = 8
QK_NOPE = 64
QK_ROPE = 32
QK_HEAD = QK_NOPE + QK_ROPE
V_HEAD = 64
HEAD_PAD = 128
PAIR = 2
ATTN_SCALE = QK_HEAD ** -0.5
Q_LORA = 384
KV_LORA = 256
CONV_WIDTH = 512
ATTN_WIDTH = N_HEADS * V_HEAD
ROPE_THETA = 10000.0
EPS = 1e-6
ADAM_LR, ADAM_B1, ADAM_B2, ADAM_EPS, ADAM_WD, ADAM_STEP = 0.001, 0.9, 0.999, 1e-08, 0.01, 10

Z_Q = (0, 384)
Z_KV = (384, 640)
Z_GB = (640, 1152)
Z_GC = (1152, 1664)
Z_XIN = (1664, 2176)
Z_KPE = (2176, 2304)
Z_COLS = 2304

N_DEV = 8
LANES = 128
V7X_VMEM_LIMIT = 52 * 1024 * 1024
TOKEN_TILE = 256
LIGHT_TOKEN_TILE = 512
ATTN_BLOCK = 256
ATTN_FWD_BLOCK = 512
ATTN_Q_SUB = 2
ATTN_KV_SUB = 2
ROW_BLOCK = 512
WGRAD_TOKENS = 2048
WGRAD_TILE = 1024
HALO = 8

GAIN_NAMES = ("g_mix", "g_q_lat", "g_kv_lat", "g_qn_nope", "g_qn_rope", "g_kn_nope", "g_kn_rope",
              "g_out_attn", "g_out_conv", "g_mlp", "g_ple")
SHARD_NAMES = ("w_in", "w_uq", "w_ukv", "conv_w", "w_o", "w_up", "w_down", "w_ple_gate", "w_ple")
WEIGHT_NAMES = ("g_mix", "w_in", "g_q_lat", "w_uq", "g_kv_lat", "w_ukv", "g_qn_nope", "g_qn_rope", "g_kn_nope",
                "g_kn_rope", "conv_w", "g_out_attn", "g_out_conv", "w_o", "g_mlp", "w_up", "w_down", "g_ple",
                "w_ple_gate", "w_ple")


def _cparams(semantics=None):
    return pltpu.CompilerParams(dimension_semantics=semantics, vmem_limit_bytes=V7X_VMEM_LIMIT)


def _mm(a, b):
    return jnp.dot(a, b, preferred_element_type=F32)


def _mm_nt(a, b):
    return lax.dot_general(a, b, (((1,), (1,)), ((), ())), preferred_element_type=F32)


def _mm_tn(a, b):
    return lax.dot_general(a, b, (((0,), (0,)), ((), ())), preferred_element_type=F32)


def _rms(x, g):
    r = lax.rsqrt(jnp.mean(x * x, axis=-1, keepdims=True) + EPS)
    return (x * r) * g


def _rms_bwd(x, g, dy):
    r = lax.rsqrt(jnp.mean(x * x, axis=-1, keepdims=True) + EPS)
    xh = x * r
    dg = jnp.sum(dy * xh, axis=0, keepdims=True)
    dyg = dy * g
    dx = r * (dyg - xh * jnp.mean(dyg * xh, axis=-1, keepdims=True))
    return dx, dg


def _group_mean(t, gm):
    hi = t.astype(BF16)
    lo = (t - hi.astype(F32)).astype(BF16)
    return _mm(hi, gm) + _mm(lo, gm)


def _swap_rope_halves(x, lane):
    half = QK_ROPE // 2
    swapped = jnp.where(lane < QK_NOPE + half, pltpu.roll(x, x.shape[1] - half, 1), pltpu.roll(x, half, 1))
    return jnp.where((lane >= QK_NOPE) & (lane < QK_HEAD), swapped, 0.0)


def _qk_fwd(x, g, cos, sin, gm, lane):
    r = lax.rsqrt(_group_mean(x * x, gm) + EPS)
    n = (x * r) * g
    return n * cos + _swap_rope_halves(n, lane) * sin


def _qk_bwd(x, g, dy, cos, sin, gm, lane):
    r = lax.rsqrt(_group_mean(x * x, gm) + EPS)
    xh = x * r
    dn = dy * cos + _swap_rope_halves(dy * sin, lane)
    dg = jnp.sum(dn * xh, axis=0, keepdims=True)
    dng = dn * g
    dx = r * (dng - xh * _group_mean(dng * xh, gm))
    return dx, dg


def _row_shards_joined(ref):
    n, rows, cols = ref.shape
    return ref[...].reshape(n * rows, cols)


def _rows(tm, n):
    return pl.BlockSpec((tm, n), lambda i: (i, 0))


def _whole(shape):
    zeros = (0,) * len(shape)
    return pl.BlockSpec(shape, lambda i: zeros)


def _operands(arrays):
    return list(arrays), [_whole(a.shape) for a in arrays]


def _accumulate(ref, first, value):
    @pl.when(first)
    def _():
        ref[...] = jnp.zeros_like(ref)
    ref[...] += value


def _front_fwd(x, w, tabs):
    t_len, d = x.shape
    tm = min(TOKEN_TILE, t_len)
    hp = N_HEADS * HEAD_PAD

    def body(x_ref, gmix, win, gq, wuq, gkv, wukv, gqn, gkn, cw_ref, gm_ref, cos_ref, sin_ref,
             z_ref, qf_ref, kf_ref, kv_ref, conv_ref, ubuf):
        i = pl.program_id(0)
        h = _rms(x_ref[...], gmix[...])
        z = _mm(h.astype(BF16), win[...])
        z_ref[...] = z
        qnb = _rms(z[:, Z_Q[0]:Z_Q[1]], gq[...]).astype(BF16)
        kvb = _rms(z[:, Z_KV[0]:Z_KV[1]], gkv[...]).astype(BF16)
        kpe = z[:, Z_KPE[0]:Z_KPE[1]]
        kpe = jnp.concatenate([kpe] * PAIR, axis=1)
        cos, sin = (jnp.concatenate([t[...]] * PAIR, axis=1) for t in (cos_ref, sin_ref))
        gm = gm_ref[...]
        lane = lax.broadcasted_iota(jnp.int32, (tm, PAIR * HEAD_PAD), 1) & (HEAD_PAD - 1)
        for pr in range(N_HEADS // PAIR):
            sl = slice(pr * PAIR * HEAD_PAD, (pr + 1) * PAIR * HEAD_PAD)
            qf_ref[:, sl] = (_qk_fwd(_mm(qnb, wuq[pr]), gqn[...], cos, sin, gm, lane) * ATTN_SCALE).astype(BF16)
            kv = _mm(kvb, wukv[pr])
            kv_ref[:, sl] = jnp.where(lane < QK_NOPE, jnp.where(lane == 0, 1.0, 0.0), kv).astype(BF16)
            kf_ref[:, sl] = _qk_fwd(jnp.where(lane < QK_NOPE, kv, 0.0) + kpe, gkn[...], cos, sin, gm, lane).astype(BF16)
        u = z[:, Z_GC[0]:Z_GC[1]] * z[:, Z_XIN[0]:Z_XIN[1]]

        @pl.when(i == 0)
        def _():
            ubuf[0:HALO, :] = jnp.zeros((HALO, CONV_WIDTH), F32)
        ubuf[HALO:HALO + tm, :] = u
        cw = cw_ref[...]
        y = cw[0:1] * u + cw[1:2] * ubuf[pl.ds(HALO - 1, tm), :] + cw[2:3] * ubuf[pl.ds(HALO - 2, tm), :]
        conv_ref[...] = z[:, Z_GB[0]:Z_GB[1]] * y
        ubuf[0:HALO, :] = u[tm - HALO:tm, :]

    consts, const_specs = _operands([w["g_mix"], w["w_in"], w["g_q_lat"], w["w_uq"], w["g_kv_lat"], w["w_ukv"],
                                     w["g_qn"], w["g_kn"], w["conv_w"], tabs["gm"]])
    out_shape = (jax.ShapeDtypeStruct((t_len, Z_COLS), F32), jax.ShapeDtypeStruct((t_len, hp), BF16),
                 jax.ShapeDtypeStruct((t_len, hp), BF16), jax.ShapeDtypeStruct((t_len, hp), BF16),
                 jax.ShapeDtypeStruct((t_len, CONV_WIDTH), F32))
    return pl.pallas_call(body, name="front_fwd", grid=(t_len // tm,),
                          in_specs=[_rows(tm, d)] + const_specs + [_rows(tm, HEAD_PAD)] * 2,
                          out_specs=tuple(_rows(tm, s.shape[1]) for s in out_shape), out_shape=out_shape,
                          scratch_shapes=[pltpu.VMEM((tm + HALO, CONV_WIDTH), F32)],
                          compiler_params=_cparams(("arbitrary",)))(x, *consts, tabs["cos"], tabs["sin"])


def _attn_fwd(qf, kf, kv, gather=None):
    t_len = qf.shape[0]
    blk = min(ATTN_FWD_BLOCK, t_len)
    nb = t_len // blk
    n_sub = ATTN_Q_SUB
    bq = blk // n_sub
    chains = [(hh, a) for hh in range(2) for a in range(n_sub)]

    def body(q_ref, k_ref, kv_ref, o_ref, lse_ref):
        lane = lax.broadcasted_iota(jnp.int32, (bq, LANES), 1)
        row = lax.broadcasted_iota(jnp.int32, (bq, blk), 0)
        col = lax.broadcasted_iota(jnp.int32, (bq, blk), 1)

        def head_cols(hh):
            return slice(hh * HEAD_PAD, (hh + 1) * HEAD_PAD)

        def softmax_step(s, kvv, state, first_row=None):
            m, acc = state
            if first_row is not None:
                s = jnp.where(col <= row + first_row, s, -jnp.inf)
            m_new = jnp.maximum(m, jnp.max(s, axis=-1, keepdims=True))
            p = jnp.exp(s - m_new)
            acc = jnp.exp(m - m_new) * acc + _mm(p.astype(BF16), kvv)
            return m_new, acc

        def finish(state):
            m, acc = state
            l = jnp.sum(jnp.where(lane == 0, acc, 0.0), axis=-1, keepdims=True)
            return acc / l, jnp.broadcast_to(m + jnp.log(l), (bq, LANES))

        def qblock(i, carry):
            start = pl.multiple_of(i * blk, blk)
            rows = [pl.ds(pl.multiple_of(start + a * bq, bq), bq) for a in range(n_sub)]
            qs = {(hh, a): q_ref[rows[a], head_cols(hh)] for hh, a in chains}

            def scores(j):
                ks = pl.ds(pl.multiple_of(j * blk, blk), blk)
                return tuple(_mm_nt(qs[hh, a], k_ref[ks, head_cols(hh)]) for hh, a in chains)

            def kstep(j, carried, diagonal=False):
                ss, states = carried
                ss_next = ss if diagonal else scores(j + 1)
                ks = pl.ds(pl.multiple_of(j * blk, blk), blk)
                new = tuple(softmax_step(s, kv_ref[ks, head_cols(hh)], st, a * bq if diagonal else None)
                            for (hh, a), s, st in zip(chains, ss, states))
                return ss_next, new

            init = (jnp.full((bq, 1), -jnp.inf, F32), jnp.zeros((bq, LANES), F32))
            carried = lax.fori_loop(0, i, kstep, (scores(0), (init,) * len(chains)))
            _, states = kstep(i, carried, diagonal=True)
            for a in range(n_sub):
                (o0, lse0), (o1, lse1) = finish(states[chains.index((0, a))]), finish(states[chains.index((1, a))])
                o_ref[rows[a], :] = jnp.where(lane < V_HEAD, pltpu.roll(o0, V_HEAD, 1), o1)
                lse_ref[rows[a], head_cols(0)] = lse0
                lse_ref[rows[a], head_cols(1)] = lse1
            return carry

        lax.fori_loop(0, nb, qblock, 0)

    heads = pl.BlockSpec((t_len, 2 * HEAD_PAD), lambda h: (0, h))
    pair = pl.BlockSpec((t_len, 2 * V_HEAD), lambda h: (0, h))
    out_shape = (jax.ShapeDtypeStruct((t_len, ATTN_WIDTH), F32), jax.ShapeDtypeStruct((t_len, N_HEADS * LANES), F32))
    (attn, lse), gathered = _call_hosting_gather("attn_fwd", body, N_HEADS // 2, [heads, heads, heads], (pair, heads),
                                                 out_shape, [], (qf, kf, kv), gather)
    return attn, lse, gathered


def _call_hosting_gather(name, body, n_steps, in_specs, out_specs, out_shape, scratch_shapes, args, gather):
    if not gather:
        return pl.pallas_call(body, name=name, grid=(n_steps,), in_specs=list(in_specs), out_specs=tuple(out_specs),
                              out_shape=tuple(out_shape), scratch_shapes=list(scratch_shapes),
                              compiler_params=_cparams(("arbitrary",)))(*args), ()
    n_in, n_out, n_g = len(in_specs), len(out_shape), len(gather)

    def hosting_body(*refs):
        ins, refs = refs[:n_in], refs[n_in:]
        x_refs = [r.at[layer] for r, (_, layer) in zip(refs[:n_g], gather)]
        outs, landing, sems, scratch = (refs[n_g:n_g + n_out], refs[n_g + n_out:2 * n_g + n_out],
                                        refs[2 * n_g + n_out:2 * n_g + n_out + 3], refs[2 * n_g + n_out + 3:])
        start, forward, finish = _gather_phases(x_refs, landing, *sems)
        step = pl.program_id(0)
        pl.when(step == 0)(start)
        pl.when(step == n_steps - 1)(forward)
        body(*ins, *outs, *scratch)
        pl.when(step == n_steps - 1)(finish)

    res = pl.pallas_call(hosting_body, name=name + "_gather", grid=(n_steps,), in_specs=list(in_specs) + _any_specs(n_g),
                         out_specs=tuple(out_specs) + tuple(_any_specs(n_g)),
                         out_shape=tuple(out_shape) + _gather_out_shape([s[layer] for s, layer in gather]),
                         scratch_shapes=_gather_semaphores(n_g) + list(scratch_shapes),
                         compiler_params=_cparams(("arbitrary",)))(*args, *[s for s, _ in gather])
    return res[:n_out], res[n_out:]


def _mix_out_fwd(x, attn, conv, w):
    t_len, d = x.shape
    tm = min(LIGHT_TOKEN_TILE, t_len)

    def body(x_ref, a_ref, c_ref, goa, goc, wo, x1_ref):
        mixed = jnp.concatenate([_rms(a_ref[...], goa[...]), _rms(c_ref[...], goc[...])], axis=1)
        x1_ref[...] = x_ref[...] + _mm(mixed.astype(BF16), _row_shards_joined(wo))

    consts, const_specs = _operands([w["g_out_attn"], w["g_out_conv"], w["w_o"]])
    return pl.pallas_call(body, name="mix_out_fwd", grid=(t_len // tm,),
                          in_specs=[_rows(tm, d), _rows(tm, ATTN_WIDTH), _rows(tm, CONV_WIDTH)] + const_specs,
                          out_specs=_rows(tm, d), out_shape=jax.ShapeDtypeStruct((t_len, d), F32),
                          compiler_params=_cparams(("parallel",)))(x, attn, conv, *consts)


def _mlp_fwd(x1, w, gather=None):
    t_len, d = x1.shape
    tm = min(TOKEN_TILE, t_len)

    def body(x_ref, g, wup, wdn, x2_ref):
        x1v = x_ref[...]
        hb = _rms(x1v, g[...]).astype(BF16)
        acc = x1v
        for k in range(N_DEV):
            a = jnp.maximum(_mm(hb, wup[k]), 0.0)
            acc = acc + _mm((a * a).astype(BF16), wdn[k])
        x2_ref[...] = acc

    consts, const_specs = _operands([w["g_mlp"], w["w_up"], w["w_down"]])
    (x2,), gathered = _call_hosting_gather("mlp_fwd", body, t_len // tm, [_rows(tm, d)] + const_specs, (_rows(tm, d),),
                                           (jax.ShapeDtypeStruct((t_len, d), F32),), [], (x1, *consts), gather)
    return x2, gathered


def _ple_fwd(x2, p, w):
    t_len, d = x2.shape
    tm = min(LIGHT_TOKEN_TILE, t_len)

    def body(x_ref, p_ref, g, wg, wp, x3_ref):
        x2v = x_ref[...]
        gate = jax.nn.sigmoid(_mm(_rms(x2v, g[...]).astype(BF16), _row_shards_joined(wg)))
        pb = p_ref[...].astype(BF16)
        e = jnp.concatenate([_mm(pb, wp[k]) for k in range(N_DEV)], axis=1)
        x3_ref[...] = x2v + gate * e

    consts, const_specs = _operands([w["g_ple"], w["w_ple_gate"], w["w_ple"]])
    return pl.pallas_call(body, name="ple_fwd", grid=(t_len // tm,),
                          in_specs=[_rows(tm, d), _rows(tm, p.shape[1])] + const_specs, out_specs=_rows(tm, d),
                          out_shape=jax.ShapeDtypeStruct((t_len, d), F32),
                          compiler_params=_cparams(("parallel",)))(x2, p, *consts)


def _loss_and_grad(y, target):
    t_len, d = y.shape
    tm = min(TOKEN_TILE, t_len)

    def body(y_ref, t_ref, sq_ref, dy_ref):
        err = y_ref[...] - t_ref[...]
        dy_ref[...] = err / d
        total = jnp.sum(jnp.sum(err * err, axis=0, keepdims=True), axis=1, keepdims=True)
        _accumulate(sq_ref, pl.program_id(0) == 0, jnp.broadcast_to(total, (HALO, LANES)))

    return pl.pallas_call(body, name="loss_grad", grid=(t_len // tm,), in_specs=[_rows(tm, d), _rows(tm, d)],
                          out_specs=(_whole((HALO, LANES)), _rows(tm, d)),
                          out_shape=(jax.ShapeDtypeStruct((HALO, LANES), F32), jax.ShapeDtypeStruct((t_len, d), F32)),
                          compiler_params=_cparams(("arbitrary",)))(y, target)


def _ple_bwd(dx3, x2, p, w):
    t_len, d = x2.shape
    tm = min(LIGHT_TOKEN_TILE, t_len)

    def body(dx3_ref, x_ref, p_ref, g, wg, wp, dx2_ref, de_ref, h3_ref, dpre_ref, dg_ref):
        x2v, dx3v = x_ref[...], dx3_ref[...]
        hb = _rms(x2v, g[...]).astype(BF16)
        h3_ref[...] = hb
        w_gate = _row_shards_joined(wg)
        gate = jax.nn.sigmoid(_mm(hb, w_gate))
        pb = p_ref[...].astype(BF16)
        e = jnp.concatenate([_mm(pb, wp[k]) for k in range(N_DEV)], axis=1)
        de_ref[...] = (dx3v * gate).astype(BF16)
        dpre = ((dx3v * e) * gate * (1.0 - gate)).astype(BF16)
        dpre_ref[...] = dpre
        dx, dg = _rms_bwd(x2v, g[...], _mm_nt(dpre, w_gate))
        dx2_ref[...] = dx3v + dx
        _accumulate(dg_ref, pl.program_id(0) == 0, dg)

    consts, const_specs = _operands([w["g_ple"], w["w_ple_gate"], w["w_ple"]])
    out_shape = (jax.ShapeDtypeStruct((t_len, d), F32), jax.ShapeDtypeStruct((t_len, d), BF16),
                 jax.ShapeDtypeStruct((t_len, d), BF16), jax.ShapeDtypeStruct((t_len, d), BF16),
                 jax.ShapeDtypeStruct((1, d), F32))
    return pl.pallas_call(body, name="ple_bwd", grid=(t_len // tm,),
                          in_specs=[_rows(tm, d), _rows(tm, d), _rows(tm, p.shape[1])] + const_specs,
                          out_specs=(_rows(tm, d),) * 4 + (_whole((1, d)),), out_shape=out_shape,
                          compiler_params=_cparams(("arbitrary",)))(dx3, x2, p, *consts)


def _mlp_bwd(dx2, x1, w):
    t_len, d = x1.shape
    tm = min(TOKEN_TILE, t_len)
    fc = w["w_up"].shape[2]
    ff = N_DEV * fc

    def body(dx2_ref, x_ref, g, wup, wdn, dx1_ref, r_ref, da_ref, h2_ref, dg_ref):
        x1v, dx2v = x_ref[...], dx2_ref[...]
        hb = _rms(x1v, g[...]).astype(BF16)
        h2_ref[...] = hb
        dxb = dx2v.astype(BF16)
        dh = jnp.zeros((tm, d), F32)
        for k in range(N_DEV):
            a = jnp.maximum(_mm(hb, wup[k]), 0.0)
            r_ref[:, k * fc:(k + 1) * fc] = (a * a).astype(BF16)
            da = (_mm_nt(dxb, wdn[k]) * (2.0 * a)).astype(BF16)
            da_ref[:, k * fc:(k + 1) * fc] = da
            dh = dh + _mm_nt(da, wup[k])
        dx, dg = _rms_bwd(x1v, g[...], dh)
        dx1_ref[...] = dx2v + dx
        _accumulate(dg_ref, pl.program_id(0) == 0, dg)

    consts, const_specs = _operands([w["g_mlp"], w["w_up"], w["w_down"]])
    out_shape = (jax.ShapeDtypeStruct((t_len, d), F32), jax.ShapeDtypeStruct((t_len, ff), BF16),
                 jax.ShapeDtypeStruct((t_len, ff), BF16), jax.ShapeDtypeStruct((t_len, d), BF16),
                 jax.ShapeDtypeStruct((1, d), F32))
    return pl.pallas_call(body, name="mlp_bwd", grid=(t_len // tm,), in_specs=[_rows(tm, d), _rows(tm, d)] + const_specs,
                          out_specs=(_rows(tm, d), _rows(tm, ff), _rows(tm, ff), _rows(tm, d), _whole((1, d))),
                          out_shape=out_shape, compiler_params=_cparams(("arbitrary",)))(dx2, x1, *consts)


def _mix_out_bwd(dx1, attn, conv, w):
    t_len, d = dx1.shape
    tm = min(LIGHT_TOKEN_TILE, t_len)

    def body(dx1_ref, a_ref, c_ref, goa, goc, wo, mixed_ref, da_ref, dc_ref, dgoa_ref, dgoc_ref):
        av, cv = a_ref[...], c_ref[...]
        mixed_ref[...] = jnp.concatenate([_rms(av, goa[...]), _rms(cv, goc[...])], axis=1).astype(BF16)
        dmixed = _mm_nt(dx1_ref[...].astype(BF16), _row_shards_joined(wo))
        da, dga = _rms_bwd(av, goa[...], dmixed[:, :ATTN_WIDTH])
        dc, dgc = _rms_bwd(cv, goc[...], dmixed[:, ATTN_WIDTH:])
        da_ref[...] = da
        dc_ref[...] = dc
        first = pl.program_id(0) == 0
        _accumulate(dgoa_ref, first, dga)
        _accumulate(dgoc_ref, first, dgc)

    consts, const_specs = _operands([w["g_out_attn"], w["g_out_conv"], w["w_o"]])
    out_shape = (jax.ShapeDtypeStruct((t_len, d), BF16), jax.ShapeDtypeStruct((t_len, ATTN_WIDTH), F32),
                 jax.ShapeDtypeStruct((t_len, CONV_WIDTH), F32), jax.ShapeDtypeStruct((1, ATTN_WIDTH), F32),
                 jax.ShapeDtypeStruct((1, CONV_WIDTH), F32))
    out_specs = (_rows(tm, d), _rows(tm, ATTN_WIDTH), _rows(tm, CONV_WIDTH), _whole((1, ATTN_WIDTH)),
                 _whole((1, CONV_WIDTH)))
    return pl.pallas_call(body, name="mix_out_bwd", grid=(t_len // tm,),
                          in_specs=[_rows(tm, d), _rows(tm, ATTN_WIDTH), _rows(tm, CONV_WIDTH)] + const_specs,
                          out_specs=out_specs, out_shape=out_shape,
                          compiler_params=_cparams(("arbitrary",)))(dx1, attn, conv, *consts)


def _attn_bwd(qf, kf, kv, o, do, lse, scatter):
    t_len = qf.shape[0]
    blk = min(ATTN_BLOCK, t_len)
    nb = t_len // blk
    n_sub = ATTN_KV_SUB if nb % ATTN_KV_SUB == 0 else 1
    reps = blk // LANES

    def body(q_ref, k_ref, kv_ref, o_ref, do_ref, lse_ref, dq_ref, dk_ref, dkv_ref, delta_ref, dob_ref):
        hd = pl.program_id(0)
        lane = lax.broadcasted_iota(jnp.int32, (blk, LANES), 1)
        even = (lane * 0 + hd % 2) == 0
        mine = jnp.where(lane < V_HEAD, 0, 1) == hd % 2
        row = lax.broadcasted_iota(jnp.int32, (blk, blk), 0)
        col = lax.broadcasted_iota(jnp.int32, (blk, blk), 1)
        dq_ref[...] = jnp.zeros_like(dq_ref)

        def prepare(i, carry):
            qs = pl.ds(pl.multiple_of(i * blk, blk), blk)
            dov = do_ref[qs, :]
            prod = jnp.where(mine, dov * o_ref[qs, :], 0.0)
            delta_ref[qs, :] = jnp.broadcast_to(jnp.sum(prod, axis=-1, keepdims=True), (blk, LANES))
            moved = jnp.where(even, pltpu.roll(dov, V_HEAD, 1), dov)
            dob_ref[qs, :] = jnp.where(lane >= V_HEAD, moved, 0.0).astype(BF16)
            return carry
        lax.fori_loop(0, nb, prepare, 0)

        def kvblock(jj, carry):
            base = jj * n_sub
            kss = [pl.ds(pl.multiple_of((base + a) * blk, blk), blk) for a in range(n_sub)]
            k = [k_ref[ks, :] for ks in kss]
            kvv = [kv_ref[ks, :] for ks in kss]

            def products(i):
                qs = pl.ds(pl.multiple_of(i * blk, blk), blk)
                q, dob = q_ref[qs, :], dob_ref[qs, :]
                return tuple((_mm_nt(q, k[a]), _mm_nt(dob, kvv[a])) for a in range(n_sub))

            def qstep(i, raw, accs, kinds):
                qs = pl.ds(pl.multiple_of(i * blk, blk), blk)
                q = q_ref[qs, :]
                dob = dob_ref[qs, :]
                lse_t = jnp.concatenate([lse_ref[qs, :]] * reps, axis=1)
                delta_t = jnp.concatenate([delta_ref[qs, :]] * reps, axis=1)
                new, dq_add = [], None
                for a in range(n_sub):
                    if kinds[a] is None:
                        new.append(accs[a])
                        continue
                    dk_acc, dv_acc = accs[a]
                    s, dp = raw[a]
                    if kinds[a]:
                        s = jnp.where(col <= row, s, -jnp.inf)
                    p = jnp.exp(s - lse_t)
                    ds = (p * (dp - delta_t)).astype(BF16)
                    new.append((dk_acc + _mm_tn(ds, q), dv_acc + _mm_tn(p.astype(BF16), dob)))
                    part = _mm(ds, k[a])
                    dq_add = part if dq_add is None else dq_add + part
                dq_ref[qs, :] += dq_add
                return tuple(new)

            zero = jnp.zeros((blk, LANES), F32)
            accs = ((zero, zero),) * n_sub
            for b in range(n_sub):
                accs = qstep(base + b, products(base + b), accs, tuple((a == b) if a <= b else None for a in range(n_sub)))

            def pipelined(i, carried):
                raw, acc = carried
                return products(jnp.minimum(i + 1, nb - 1)), qstep(i, raw, acc, (False,) * n_sub)

            first = base + n_sub
            _, accs = lax.fori_loop(first, nb, pipelined, (products(jnp.minimum(first, nb - 1)), accs))
            for a in range(n_sub):
                dk_ref[kss[a], :] = accs[a][0]
                dkv_ref[kss[a], :] = accs[a][1]
            return carry
        lax.fori_loop(0, nb // n_sub, kvblock, 0)

    n_sc = len(scatter[0])

    def hosting_body(*refs):
        ins, rest = refs[:6], refs[6 + 2 * n_sc:]
        parts = refs[6:6 + n_sc]
        outs, landed, sems, scratch = rest[:3], rest[3:3 + n_sc], rest[3 + n_sc:6 + n_sc], rest[6 + n_sc:]
        start, finish = _scatter_phases(parts, landed, *sems, scatter[2])
        hd = pl.program_id(0)
        pl.when(hd == 0)(start)
        body(*ins, *outs, *scratch)
        pl.when(hd == N_HEADS - 1)(finish)

    head = pl.BlockSpec((t_len, HEAD_PAD), lambda h: (0, h))
    pair = pl.BlockSpec((t_len, 2 * V_HEAD), lambda h: (0, h // 2))
    out = jax.ShapeDtypeStruct((t_len, N_HEADS * HEAD_PAD), F32)
    vmem_scratch = [pltpu.VMEM((t_len, LANES), F32), pltpu.VMEM((t_len, LANES), BF16)]
    res = pl.pallas_call(hosting_body, name="attn_bwd_scatter", grid=(N_HEADS,),
                         in_specs=[head, head, head, pair, pair, head] + _any_specs(2 * n_sc),
                         out_specs=(head, head, head) + tuple(_any_specs(n_sc)),
                         out_shape=(out, out, out) + _same_shapes(scatter[1]),
                         scratch_shapes=_scatter_semaphores(n_sc) + vmem_scratch,
                         input_output_aliases={6 + n_sc + a: 3 + a for a in range(n_sc)},
                         compiler_params=_cparams(("arbitrary",)))(qf, kf, kv, o, do, lse, *scatter[0], *scatter[1])
    return res[0], res[1], res[2], res[3:]


def _front_bwd(x, z, dx1, dqf, dkf, dkv_in, dconv, w, tabs):
    t_len, d = x.shape
    tm = min(TOKEN_TILE, t_len)
    nt = t_len // tm
    hb_per_tile = tm // HALO
    n_halo = t_len // HALO
    hp = N_HEADS * HEAD_PAD

    def body(x_ref, z_ref, zp_ref, zn_ref, dx1_ref, dqf_ref, dkf_ref, dkv_ref, dc_ref, dcn_ref,
             gmix, win, gq, wuq, gkv, wukv, gqn, gkn, cw_ref, gm_ref, cos_ref, sin_ref,
             dx_ref, dz_ref, h_ref, qn_ref, kvn_ref, dqr_ref, dkvr_ref,
             dgmix_ref, dgq_ref, dgkv_ref, dgqn_ref, dgkn_ref, dcw_ref, ubuf, dybuf):
        i = pl.program_id(0)
        first = i == 0
        xv, zv = x_ref[...], z_ref[...]
        hb = _rms(xv, gmix[...]).astype(BF16)
        h_ref[...] = hb
        zq, zkv = zv[:, Z_Q[0]:Z_Q[1]], zv[:, Z_KV[0]:Z_KV[1]]
        qnb = _rms(zq, gq[...]).astype(BF16)
        qn_ref[...] = qnb
        kvb = _rms(zkv, gkv[...]).astype(BF16)
        kvn_ref[...] = kvb
        kpe = zv[:, Z_KPE[0]:Z_KPE[1]]
        kpe = jnp.concatenate([kpe] * PAIR, axis=1)
        cos, sin = (jnp.concatenate([t[...]] * PAIR, axis=1) for t in (cos_ref, sin_ref))
        gm = gm_ref[...]
        width = PAIR * HEAD_PAD
        lane = lax.broadcasted_iota(jnp.int32, (tm, width), 1) & (HEAD_PAD - 1)
        is_nope = lane < QK_NOPE
        is_rope = (lane >= QK_NOPE) & (lane < QK_HEAD)
        dkpe = jnp.zeros((tm, width), F32)
        dgqn = jnp.zeros((1, width), F32)
        dgkn = jnp.zeros((1, width), F32)
        dqn = jnp.zeros((tm, Q_LORA), F32)
        dkvn = jnp.zeros((tm, KV_LORA), F32)
        for pr in range(N_HEADS // PAIR):
            sl = slice(pr * width, (pr + 1) * width)
            dxq, dg = _qk_bwd(_mm(qnb, wuq[pr]), gqn[...], dqf_ref[:, sl] * ATTN_SCALE, cos, sin, gm, lane)
            dxq = dxq.astype(BF16)
            dqr_ref[:, sl] = dxq
            dqn = dqn + _mm_nt(dxq, wuq[pr])
            dgqn = dgqn + dg
            k_raw = jnp.where(is_nope, _mm(kvb, wukv[pr]), 0.0) + kpe
            dxk, dg = _qk_bwd(k_raw, gkn[...], dkf_ref[:, sl], cos, sin, gm, lane)
            dkv = jnp.where(is_nope, dxk, dkv_ref[:, sl]).astype(BF16)
            dkvr_ref[:, sl] = dkv
            dkvn = dkvn + _mm_nt(dkv, wukv[pr])
            dkpe = dkpe + jnp.where(is_rope, dxk, 0.0)
            dgkn = dgkn + dg
        dkpe = dkpe[:, :HEAD_PAD] + dkpe[:, HEAD_PAD:]
        _accumulate(dgqn_ref, first, dgqn[:, :HEAD_PAD] + dgqn[:, HEAD_PAD:])
        _accumulate(dgkn_ref, first, dgkn[:, :HEAD_PAD] + dgkn[:, HEAD_PAD:])
        dzq, dg = _rms_bwd(zq, gq[...], dqn)
        _accumulate(dgq_ref, first, dg)
        dzkv, dg = _rms_bwd(zkv, gkv[...], dkvn)
        _accumulate(dgkv_ref, first, dg)

        gb, gc, xin = zv[:, Z_GB[0]:Z_GB[1]], zv[:, Z_GC[0]:Z_GC[1]], zv[:, Z_XIN[0]:Z_XIN[1]]
        u = gc * xin
        dcv = dc_ref[...]
        dy = dcv * gb
        zp, zn = zp_ref[...], zn_ref[...]
        ubuf[0:HALO, :] = (zp[:, Z_GC[0]:Z_GC[1]] * zp[:, Z_XIN[0]:Z_XIN[1]]) * jnp.where(first, 0.0, 1.0)
        ubuf[HALO:HALO + tm, :] = u
        dybuf[0:tm, :] = dy
        dybuf[tm:tm + HALO, :] = (dcn_ref[...] * zn[:, Z_GB[0]:Z_GB[1]]) * jnp.where(i == nt - 1, 0.0, 1.0)
        cw = cw_ref[...]
        u1, u2 = ubuf[pl.ds(HALO - 1, tm), :], ubuf[pl.ds(HALO - 2, tm), :]
        y = cw[0:1] * u + cw[1:2] * u1 + cw[2:3] * u2
        du = cw[0:1] * dy + cw[1:2] * dybuf[pl.ds(1, tm), :] + cw[2:3] * dybuf[pl.ds(2, tm), :]
        dcw = jnp.concatenate([jnp.sum(dy * u, axis=0, keepdims=True), jnp.sum(dy * u1, axis=0, keepdims=True),
                               jnp.sum(dy * u2, axis=0, keepdims=True), jnp.zeros((HALO - 3, CONV_WIDTH), F32)], axis=0)
        _accumulate(dcw_ref, first, dcw)

        dz_ref[:, Z_Q[0]:Z_Q[1]] = dzq.astype(BF16)
        dz_ref[:, Z_KV[0]:Z_KV[1]] = dzkv.astype(BF16)
        dz_ref[:, Z_GB[0]:Z_GB[1]] = (dcv * y).astype(BF16)
        dz_ref[:, Z_GC[0]:Z_GC[1]] = (du * xin).astype(BF16)
        dz_ref[:, Z_XIN[0]:Z_XIN[1]] = (du * gc).astype(BF16)
        dz_ref[:, Z_KPE[0]:Z_KPE[1]] = dkpe.astype(BF16)
        dx, dg = _rms_bwd(xv, gmix[...], _mm_nt(dz_ref[...], win[...]))
        dx_ref[...] = dx1_ref[...] + dx
        _accumulate(dgmix_ref, first, dg)

    prev_halo = lambda n: pl.BlockSpec((HALO, n), lambda i: (jnp.maximum(i * hb_per_tile - 1, 0), 0))
    next_halo = lambda n: pl.BlockSpec((HALO, n), lambda i: (jnp.minimum((i + 1) * hb_per_tile, n_halo - 1), 0))
    consts, const_specs = _operands([w["g_mix"], w["w_in"], w["g_q_lat"], w["w_uq"], w["g_kv_lat"], w["w_ukv"],
                                     w["g_qn"], w["g_kn"], w["conv_w"], tabs["gm"]])
    in_specs = ([_rows(tm, d), _rows(tm, Z_COLS), prev_halo(Z_COLS), next_halo(Z_COLS), _rows(tm, d), _rows(tm, hp),
                 _rows(tm, hp), _rows(tm, hp), _rows(tm, CONV_WIDTH), next_halo(CONV_WIDTH)]
                + const_specs + [_rows(tm, HEAD_PAD)] * 2)
    out_shape = (jax.ShapeDtypeStruct((t_len, d), F32), jax.ShapeDtypeStruct((t_len, Z_COLS), BF16),
                 jax.ShapeDtypeStruct((t_len, d), BF16), jax.ShapeDtypeStruct((t_len, Q_LORA), BF16),
                 jax.ShapeDtypeStruct((t_len, KV_LORA), BF16), jax.ShapeDtypeStruct((t_len, hp), BF16),
                 jax.ShapeDtypeStruct((t_len, hp), BF16),
                 jax.ShapeDtypeStruct((1, d), F32), jax.ShapeDtypeStruct((1, Q_LORA), F32),
                 jax.ShapeDtypeStruct((1, KV_LORA), F32), jax.ShapeDtypeStruct((1, LANES), F32),
                 jax.ShapeDtypeStruct((1, LANES), F32), jax.ShapeDtypeStruct((HALO, CONV_WIDTH), F32))
    out_specs = tuple(_rows(tm, s.shape[1]) for s in out_shape[:7]) + tuple(_whole(s.shape) for s in out_shape[7:])
    return pl.pallas_call(body, name="front_bwd", grid=(nt,), in_specs=in_specs, out_specs=out_specs, out_shape=out_shape,
                          scratch_shapes=[pltpu.VMEM((tm + HALO, CONV_WIDTH), F32), pltpu.VMEM((tm + HALO, CONV_WIDTH), F32)],
                          compiler_params=_cparams(("arbitrary",)))(
                              x, z, z, z, dx1, dqf, dkf, dkv_in, dconv, dconv, *consts, tabs["cos"], tabs["sin"])


def _wgrad(a, b, shard_cols=None, shard_rows=None, out_dtype=BF16):
    t_len, kk = a.shape
    nn = b.shape[1]
    tk = min(kk, WGRAD_TILE)
    tn = next(c for c in range(min(nn, WGRAD_TILE), 0, -LANES) if nn % c == 0 and c % (shard_cols or LANES) == 0)
    tt = min(t_len, WGRAD_TOKENS)
    nt = t_len // tt
    per_block = tn // shard_cols if shard_cols else tk // shard_rows if shard_rows else 1

    def body(a_ref, b_ref, o_ref, acc):
        t = pl.program_id(2)

        @pl.when(t == 0)
        def _():
            acc[...] = jnp.zeros_like(acc)
        acc[...] += _mm_tn(a_ref[...].astype(BF16), b_ref[...].astype(BF16))

        @pl.when(t == nt - 1)
        def _():
            if shard_cols:
                for s in range(per_block):
                    o_ref[s] = acc[:, s * shard_cols:(s + 1) * shard_cols].astype(out_dtype)
            elif shard_rows:
                for s in range(per_block):
                    o_ref[s] = acc[s * shard_rows:(s + 1) * shard_rows, :].astype(out_dtype)
            else:
                o_ref[...] = acc[...].astype(out_dtype)

    if shard_cols:
        out_shape = jax.ShapeDtypeStruct((nn // shard_cols, kk, shard_cols), out_dtype)
        out_spec = pl.BlockSpec((per_block, tk, shard_cols), lambda i, j, t: (j, i, 0))
    elif shard_rows:
        out_shape = jax.ShapeDtypeStruct((kk // shard_rows, shard_rows, nn), out_dtype)
        out_spec = pl.BlockSpec((per_block, shard_rows, tn), lambda i, j, t: (i, 0, j))
    else:
        out_shape = jax.ShapeDtypeStruct((kk, nn), out_dtype)
        out_spec = pl.BlockSpec((tk, tn), lambda i, j, t: (i, j))
    return pl.pallas_call(body, name="wgrad", grid=(kk // tk, nn // tn, nt),
                          in_specs=[pl.BlockSpec((tt, tk), lambda i, j, t: (t, i)),
                                    pl.BlockSpec((tt, tn), lambda i, j, t: (t, j))],
                          out_specs=out_spec, out_shape=out_shape, scratch_shapes=[pltpu.VMEM((tk, tn), F32)],
                          compiler_params=_cparams(("parallel", "parallel", "arbitrary")))(a, b)


def _my_place():
    return lax.axis_index("x"), lax.axis_index("y"), lax.axis_index("c")


def _any_specs(n):
    return [pl.BlockSpec(memory_space=pl.ANY)] * n


def _all_gather(blocks):
    n = len(blocks)

    def body(*refs):
        start, forward, finish = _gather_phases(refs[:n], refs[n:2 * n], *refs[2 * n:])
        start()
        forward()
        finish()

    return pl.pallas_call(body, name="all_gather", out_shape=_gather_out_shape(blocks), in_specs=_any_specs(n),
                          out_specs=tuple(_any_specs(n)), scratch_shapes=_gather_semaphores(n))(*blocks)


def _gather_out_shape(blocks):
    return tuple(jax.ShapeDtypeStruct((N_DEV,) + b.shape, b.dtype) for b in blocks)


def _gather_semaphores(n):
    return [pltpu.SemaphoreType.DMA((n, 7)), pltpu.SemaphoreType.DMA((n, 7)), pltpu.SemaphoreType.DMA((n,))]


def _gather_phases(x_refs, out_refs, send_sems, recv_sems, local_sems):
    n = len(x_refs)
    x, y, c = _my_place()
    me, sibling = (x, y, c), (x, y, 1 - c)
    chips = [(1 - x, y), (x, 1 - y), (1 - x, 1 - y)]

    def slot(a, px, py, pc):
        return out_refs[a].at[4 * px + 2 * py + pc]

    def copy(a, k, blk, to, src=None):
        return pltpu.make_async_remote_copy(src_ref=slot(a, *blk) if src is None else src, dst_ref=slot(a, *blk),
                                            send_sem=send_sems.at[a, k], recv_sem=recv_sems.at[a, k],
                                            device_id=to, device_id_type=MESH)

    def own(a):
        return pltpu.make_async_copy(x_refs[a], slot(a, *me), local_sems.at[a])

    def first_hop(a):
        return [copy(a, 0, me, sibling, src=x_refs[a])] + [copy(a, 1 + j, me, (*chip, c), src=x_refs[a])
                                                           for j, chip in enumerate(chips)]

    def passed_on(a):
        return [copy(a, 4 + j, (*chip, c), sibling) for j, chip in enumerate(chips)]

    def start():
        for a in range(n):
            own(a).start()
        for a in range(n):
            for cp in first_hop(a):
                cp.start()

    def forward():
        for j, chip in enumerate(chips):
            for a in range(n):
                copy(a, 1 + j, (*chip, c), me).wait_recv()
                passed_on(a)[j].start()

    def finish():
        for a in range(n):
            copy(a, 0, sibling, me).wait_recv()
        for j, chip in enumerate(chips):
            for a in range(n):
                copy(a, 4 + j, (*chip, 1 - c), me).wait_recv()
        for a in range(n):
            for cp in first_hop(a) + passed_on(a):
                cp.wait_send()
            own(a).wait()

    return start, forward, finish


def _scatter_exchange(parts, landed, layers):
    n = len(parts)

    def body(*refs):
        start, finish = _scatter_phases(refs[:n], refs[2 * n:3 * n], *refs[3 * n:], layers)
        start()
        finish()

    return pl.pallas_call(body, name="scatter_exchange", out_shape=_same_shapes(landed), in_specs=_any_specs(2 * n),
                          out_specs=tuple(_any_specs(n)), scratch_shapes=_scatter_semaphores(n),
                          input_output_aliases={n + a: a for a in range(n)})(*parts, *landed)


def _same_shapes(arrays):
    return tuple(jax.ShapeDtypeStruct(a.shape, a.dtype) for a in arrays)


def _scatter_semaphores(n):
    return [pltpu.SemaphoreType.DMA((n, N_DEV - 1)), pltpu.SemaphoreType.DMA((n, N_DEV - 1)), pltpu.SemaphoreType.DMA((n,))]


def _scatter_phases(part_refs, landed_refs, send_sems, recv_sems, local_sems, layers):
    n = len(part_refs)
    x, y, c = _my_place()
    flips = [(0, 0, 1), (1, 0, 0), (0, 1, 0), (1, 1, 0), (1, 0, 1), (0, 1, 1), (1, 1, 1)]
    peers = [((1 - x) if fx else x, (1 - y) if fy else y, (1 - c) if fc else c) for fx, fy, fc in flips]
    my_k = 4 * x + 2 * y + c

    def index(peer):
        return 4 * peer[0] + 2 * peer[1] + peer[2]

    def send(a, r):
        return pltpu.make_async_remote_copy(src_ref=part_refs[a].at[index(peers[r])], dst_ref=landed_refs[a].at[my_k, layers[a]],
                                            send_sem=send_sems.at[a, r], recv_sem=recv_sems.at[a, r],
                                            device_id=peers[r], device_id_type=MESH)

    def arrival(a, r):
        return pltpu.make_async_remote_copy(src_ref=part_refs[a].at[my_k], dst_ref=landed_refs[a].at[index(peers[r]), layers[a]],
                                            send_sem=send_sems.at[a, r], recv_sem=recv_sems.at[a, r],
                                            device_id=peers[r], device_id_type=MESH)

    def own(a):
        return pltpu.make_async_copy(part_refs[a].at[my_k], landed_refs[a].at[my_k, layers[a]], local_sems.at[a])

    def start():
        for a in range(n):
            own(a).start()
        for r in range(len(peers)):
            for a in range(n):
                send(a, r).start()

    def finish():
        for r in range(len(peers)):
            for a in range(n):
                arrival(a, r).wait_recv()
        for r in range(len(peers)):
            for a in range(n):
                send(a, r).wait_send()
        for a in range(n):
            own(a).wait()

    return start, finish


def _row_block(rows):
    return ROW_BLOCK if rows % ROW_BLOCK == 0 else rows


def _sum_leading(parts):
    n_part, shape = parts.shape[0], parts.shape[1:]
    rows, cols = shape[-2:]
    rb = _row_block(rows)

    def body(p_ref, o_ref):
        acc = p_ref[0].astype(F32)
        for k in range(1, n_part):
            acc = acc + p_ref[k].astype(F32)
        o_ref[...] = acc

    if len(shape) == 3:
        grid = (shape[0], rows // rb)
        in_spec = pl.BlockSpec((n_part, None, rb, cols), lambda l, i: (0, l, i, 0))
        out_spec = pl.BlockSpec((None, rb, cols), lambda l, i: (l, i, 0))
    else:
        grid = (rows // rb,)
        in_spec, out_spec = pl.BlockSpec((n_part, rb, cols), lambda i: (0, i, 0)), _rows(rb, cols)
    return pl.pallas_call(body, name="sum_leading", grid=grid, in_specs=[in_spec], out_specs=out_spec,
                          out_shape=jax.ShapeDtypeStruct(shape, F32),
                          compiler_params=_cparams(("parallel",) * len(grid)))(parts)


def _adamw(w, g, m, v):
    shape = w.shape
    rows, cols = shape[-2:]
    rb = _row_block(rows)

    def body(w_ref, g_ref, m_ref, v_ref, d_ref, nm_ref, nv_ref):
        gv = g_ref[...]
        nm = ADAM_B1 * m_ref[...] + (1.0 - ADAM_B1) * gv
        nv = ADAM_B2 * v_ref[...] + (1.0 - ADAM_B2) * jnp.square(gv)
        m_hat = nm / (1.0 - ADAM_B1 ** ADAM_STEP)
        v_hat = nv / (1.0 - ADAM_B2 ** ADAM_STEP)
        d_ref[...] = -ADAM_LR * (m_hat / (jnp.sqrt(v_hat) + ADAM_EPS) + ADAM_WD * w_ref[...])
        nm_ref[...] = nm
        nv_ref[...] = nv

    if len(shape) == 3:
        grid, spec = (shape[0], rows // rb), pl.BlockSpec((None, rb, cols), lambda l, i: (l, i, 0))
    else:
        grid, spec = (rows // rb,), _rows(rb, cols)
    out = jax.ShapeDtypeStruct(shape, F32)
    return pl.pallas_call(body, name="adamw", grid=grid, in_specs=[spec] * 4, out_specs=(spec,) * 3,
                          out_shape=(out,) * 3, compiler_params=_cparams(("parallel",) * len(grid)))(w, g, m, v)


def _rope_tables(positions):
    t_len = positions.shape[0]
    inv_freq = 1.0 / (ROPE_THETA ** (jnp.arange(0, QK_ROPE, 2, dtype=F32) / QK_ROPE))
    ang = positions.astype(F32)[:, None] * inv_freq
    c, s = jnp.cos(ang), jnp.sin(ang)
    one, zero = jnp.ones((t_len, QK_NOPE), F32), jnp.zeros((t_len, QK_NOPE), F32)
    cos = jnp.concatenate([one, c, c, one[:, :LANES - QK_HEAD]], axis=1)
    sin = jnp.concatenate([zero, -s, s, zero[:, :LANES - QK_HEAD]], axis=1)
    idx = jnp.arange(PAIR * HEAD_PAD)
    lane, head = idx % HEAD_PAD, idx // HEAD_PAD
    grp = jnp.where(lane < QK_NOPE, 0, jnp.where(lane < QK_HEAD, 1, 2)) + 3 * head
    val = jnp.where(lane < QK_NOPE, 1.0 / QK_NOPE, jnp.where(lane < QK_HEAD, 1.0 / QK_ROPE, 0.0))
    gm = jnp.where(grp[:, None] == grp[None, :], val[None, :], 0.0).astype(BF16)
    return {"cos": cos, "sin": sin, "gm": gm}


def _head_gain(g_nope, g_rope):
    one = jnp.concatenate([g_nope, g_rope, jnp.zeros((HEAD_PAD - QK_HEAD,), F32)])
    return jnp.concatenate([one] * PAIR).reshape(1, PAIR * HEAD_PAD)


def _head_pairs(w):
    return jnp.concatenate([w[k::PAIR] for k in range(PAIR)], axis=2)


def _padded_w_in(shards):
    width = shards.shape[2]
    zeros = jnp.zeros((shards.shape[1], QK_NOPE), shards.dtype)

    def natural(start, end):
        pieces = []
        for k in range(N_DEV):
            lo, hi = max(start, k * width), min(end, (k + 1) * width)
            if lo < hi:
                pieces.append(shards[k][:, lo - k * width:hi - k * width])
        return pieces

    o2, o3 = Q_LORA + KV_LORA, Q_LORA + KV_LORA + QK_ROPE
    return jnp.concatenate(natural(0, o2) + natural(o3, N_DEV * width) + [zeros] + natural(o2, o3)
                           + [zeros[:, :LANES - QK_HEAD]], axis=1)


def _w_in_grad_shards(d_in):
    o2, o3 = Q_LORA + KV_LORA, Q_LORA + KV_LORA + QK_ROPE
    width = (Z_XIN[1] + QK_ROPE) // N_DEV
    runs = [(0, o2, 0), (o2, o3, Z_KPE[0] + QK_NOPE), (o3, N_DEV * width, o2)]
    shards = []
    for k in range(N_DEV):
        pieces = []
        for start, end, at in runs:
            lo, hi = max(start, k * width), min(end, (k + 1) * width)
            if lo < hi:
                pieces.append(d_in[:, at + lo - start:at + hi - start])
        shards.append(pieces[0] if len(pieces) == 1 else jnp.concatenate(pieces, axis=1))
    return jnp.stack(shards)


def kernel(x, p, positions, g_mix, w_in, g_q_lat, w_uq, g_kv_lat, w_ukv, g_qn_nope, g_qn_rope, g_kn_nope, g_kn_rope, conv_w, g_out_attn, g_out_conv, w_o, g_mlp, w_up, w_down, g_ple, w_ple_gate, w_ple, loss_target, m_g_mix, m_w_in, m_g_q_lat, m_w_uq, m_g_kv_lat, m_w_ukv, m_g_qn_nope, m_g_qn_rope, m_g_kn_nope, m_g_kn_rope, m_conv_w, m_g_out_attn, m_g_out_conv, m_w_o, m_g_mlp, m_w_up, m_w_down, m_g_ple, m_w_ple_gate, m_w_ple, v_g_mix, v_w_in, v_g_q_lat, v_w_uq, v_g_kv_lat, v_w_ukv, v_g_qn_nope, v_g_qn_rope, v_g_kn_nope, v_g_kn_rope, v_conv_w, v_g_out_attn, v_g_out_conv, v_w_o, v_g_mlp, v_w_up, v_w_down, v_g_ple, v_w_ple_gate, v_w_ple):
    given = dict(locals())
    weights = {n: given[n] for n in WEIGHT_NAMES}
    gains = {n: given[n] for n in GAIN_NAMES}
    depth = w_in.shape[0]
    xs, target = x[0], loss_target[0]
    d_model = xs.shape[1]
    uq_cols = w_uq.shape[2]
    n_taps = conv_w.shape[1]

    mat_names = [n for n in SHARD_NAMES if n != "conv_w"]
    local = [weights[n].astype(BF16) for n in mat_names]
    local[1] = jnp.pad(local[1], ((0, 0), (0, 0), (0, HEAD_PAD - uq_cols)))
    local = dict(zip(mat_names, local))
    front_names = ("w_in", "w_uq", "w_ukv")
    first = _all_gather([local[n][0] for n in front_names] + [conv_w])
    conv_full = jnp.transpose(first[-1], (1, 2, 0, 3)).reshape(depth, n_taps, -1)
    tabs = _rope_tables(positions[0])

    def front_weights(layer, full):
        lw = {n: gains[n][layer].reshape(1, -1) for n in GAIN_NAMES}
        lw.update({"w_in": _padded_w_in(full["w_in"]), "w_uq": _head_pairs(full["w_uq"]),
                   "w_ukv": _head_pairs(full["w_ukv"]),
                   "conv_w": jnp.pad(conv_full[layer], ((0, HALO - n_taps), (0, 0))),
                   "g_qn": _head_gain(g_qn_nope[layer], g_qn_rope[layer]),
                   "g_kn": _head_gain(g_kn_nope[layer], g_kn_rope[layer])})
        return lw

    def rest_weights(full):
        return {"w_ple": full["w_ple"], "w_up": full["w_up"], "w_down": full["w_down"],
                "w_o": full["w_o"], "w_ple_gate": full["w_ple_gate"]}

    saved, layer_w = [], []
    cur = xs
    gathered = dict(zip(front_names, first[:-1]))
    mlp_names = ("w_up", "w_down")
    for layer in range(depth):
        w = front_weights(layer, gathered)
        z, qf, kf, kv, conv = _front_fwd(cur, w, tabs)
        lots = [[n for n in mat_names if n in mlp_names], [n for n in mat_names if n not in mlp_names]]
        behind_attn, behind_mlp = (lots[1], lots[0]) if layer == 0 else lots
        if layer + 1 == depth:
            behind_attn, behind_mlp = [], []
        wanted = [(n, 0) for n in mat_names if n not in front_names] if layer == 0 else []
        wanted += [(n, layer + 1) for n in behind_attn]
        attn, lse, got = _attn_fwd(qf, kf, kv, gather=[(local[n], at) for n, at in wanted])
        coming = {}
        for (n, at), g in zip(wanted, got):
            (gathered if at == layer else coming)[n] = g
        w.update(rest_weights(gathered))
        layer_w.append(w)
        x1 = _mix_out_fwd(cur, attn, conv, w)
        x2, got = _mlp_fwd(x1, w, gather=[(local[n], layer + 1) for n in behind_mlp])
        coming.update(zip(behind_mlp, got))
        gathered = coming
        x3 = _ple_fwd(x2, p[layer, 0], w)
        saved.append(dict(x=cur, z=z, qf=qf, kf=kf, kv=kv, conv=conv, attn=attn, lse=lse, x1=x1, x2=x2))
        cur = x3

    sq, dx = _loss_and_grad(cur, target)
    loss = lax.psum(0.5 / d_model * sq[0, 0], ("x", "y", "c"))

    landed = {n: lax.empty((N_DEV, depth) + weights[n].shape[1:], BF16) for n in SHARD_NAMES}
    gain_grads = [None] * depth
    late = {}
    for layer in reversed(range(depth)):
        w, s = layer_w[layer], saved[layer]
        pl_in = p[layer, 0]
        dx2, de, h3, dpre, dg_ple = _ple_bwd(dx, s["x2"], pl_in, w)
        dx1, r, da, h2, dg_mlp = _mlp_bwd(dx2, s["x1"], w)
        mixed, dattn, dconv, dg_oa, dg_oc = _mix_out_bwd(dx1, s["attn"], s["conv"], w)
        sending = {"w_o": (_wgrad(mixed, dx1, shard_rows=w_o.shape[1]), layer),
                   "w_up": (_wgrad(h2, da, shard_cols=w_up.shape[2]), layer),
                   "w_down": (_wgrad(r, dx2, shard_rows=w_down.shape[1]), layer),
                   "w_ple_gate": (_wgrad(h3, dpre, shard_rows=w_ple_gate.shape[1]), layer),
                   "w_ple": (_wgrad(pl_in, de, shard_cols=w_ple.shape[2]), layer), **late}
        names = list(sending)
        dqf, dkf, dkv, got = _attn_bwd(s["qf"], s["kf"], s["kv"], s["attn"], dattn, s["lse"],
                                       scatter=([sending[n][0] for n in names], [landed[n] for n in names],
                                                [sending[n][1] for n in names]))
        landed.update(zip(names, got))
        (dx0, dz, hb, qn, kvn, dqr, dkvr, dg_mix, dg_q, dg_kv, dg_qn, dg_kn, dcw) = _front_bwd(
            s["x"], s["z"], dx1, dqf, dkf, dkv, dconv, w, tabs)
        late = {"w_in": (_w_in_grad_shards(_wgrad(hb, dz, out_dtype=F32)).astype(BF16), layer),
                "w_uq": (_wgrad(qn, dqr, shard_cols=HEAD_PAD)[..., :uq_cols], layer),
                "w_ukv": (_wgrad(kvn, dkvr, shard_cols=HEAD_PAD), layer),
                "conv_w": (jnp.transpose(dcw[:n_taps].reshape(n_taps, N_DEV, -1), (1, 0, 2)).astype(BF16), layer)}
        gain_grads[layer] = jnp.concatenate([
            dg_mix[0], dg_q[0], dg_kv[0], dg_qn[0, :QK_NOPE], dg_qn[0, QK_NOPE:QK_HEAD], dg_kn[0, :QK_NOPE],
            dg_kn[0, QK_NOPE:QK_HEAD], dg_oa[0], dg_oc[0], dg_mlp[0], dg_ple[0]])
        dx = dx0
    names = list(late)
    landed.update(zip(names, _scatter_exchange([late[n][0] for n in names], [landed[n] for n in names],
                                               [late[n][1] for n in names])))
    grads = {n: _sum_leading(landed[n]) for n in SHARD_NAMES}

    gg = jnp.stack(gain_grads)
    gg_rows = -(-gg.size // (HALO * LANES)) * HALO
    gg_pad = jnp.pad(gg.reshape(-1), (0, gg_rows * LANES - gg.size)).reshape(gg_rows, LANES)
    gg_sum = _sum_leading(_all_gather([gg_pad])[0]).reshape(-1)[:gg.size].reshape(gg.shape)
    off = 0
    for n in GAIN_NAMES:
        width = gains[n].shape[1]
        grads[n] = gg_sum[:, off:off + width]
        off += width

    deltas, new_m, new_v = {}, {}, {}
    for n in WEIGHT_NAMES:
        deltas[n], new_m[n], new_v[n] = _adamw(weights[n], grads[n], given["m_" + n], given["v_" + n])
    return (loss, dx[None], *[grads[n] for n in WEIGHT_NAMES], *[deltas[n] for n in WEIGHT_NAMES],
            *[new_m[n] for n in WEIGHT_NAMES], *[new_v[n] for n in WEIGHT_NAMES])
```

```python
import jax
import jax.numpy as jnp
from jax import lax
from jax.experimental import pallas as pl
from jax.experimental.pallas import tpu as pltpu

F32 = jnp.float32
BF16 = jnp.bfloat16
MESH = pl.DeviceIdType.MESH

N_HEADS = 8
QK_NOPE = 64
QK_ROPE = 32
QK_HEAD = QK_NOPE + QK_ROPE
V_HEAD = 64
HEAD_PAD = 128
PAIR = 2
ATTN_SCALE = QK_HEAD ** -0.5
Q_LORA = 384
KV_LORA = 256
CONV_WIDTH = 512
ATTN_WIDTH = N_HEADS * V_HEAD
ROPE_THETA = 10000.0
EPS = 1e-6
ADAM_LR, ADAM_B1, ADAM_B2, ADAM_EPS, ADAM_WD, ADAM_STEP = 0.001, 0.9, 0.999, 1e-08, 0.01, 10

Z_Q = (0, 384)
Z_KV = (384, 640)
Z_GB = (640, 1152)
Z_GC = (1152, 1664)
Z_XIN = (1664, 2176)
Z_KPE = (2176, 2304)
Z_COLS = 2304

N_DEV = 8
LANES = 128
V7X_VMEM_LIMIT = 52 * 1024 * 1024
TOKEN_TILE = 256
LIGHT_TOKEN_TILE = 512
ATTN_BLOCK = 256
ATTN_FWD_BLOCK = 512
ATTN_Q_SUB = 2
ATTN_KV_SUB = 16
ROW_BLOCK = 512
WGRAD_TOKENS = 2048
WGRAD_TILE = 1024
HALO = 8

GAIN_NAMES = ("g_mix", "g_q_lat", "g_kv_lat", "g_qn_nope", "g_qn_rope", "g_kn_nope", "g_kn_rope",
              "g_out_attn", "g_out_conv", "g_mlp", "g_ple")
SHARD_NAMES = ("w_in", "w_uq", "w_ukv", "conv_w", "w_o", "w_up", "w_down", "w_ple_gate", "w_ple")
WEIGHT_NAMES = ("g_mix", "w_in", "g_q_lat", "w_uq", "g_kv_lat", "w_ukv", "g_qn_nope", "g_qn_rope", "g_kn_nope",
                "g_kn_rope", "conv_w", "g_out_attn", "g_out_conv", "w_o", "g_mlp", "w_up", "w_down", "g_ple",
                "w_ple_gate", "w_ple")


def _cparams(semantics=None):
    return pltpu.CompilerParams(dimension_semantics=semantics, vmem_limit_bytes=V7X_VMEM_LIMIT)


def _mm(a, b):
    return jnp.dot(a, b, preferred_element_type=F32)


def _mm_nt(a, b):
    return lax.dot_general(a, b, (((1,), (1,)), ((), ())), preferred_element_type=F32)


def _mm_tn(a, b):
    return lax.dot_general(a, b, (((0,), (0,)), ((), ())), preferred_element_type=F32)


def _rms(x, g):
    r = lax.rsqrt(jnp.mean(x * x, axis=-1, keepdims=True) + EPS)
    return (x * r) * g


def _rms_bwd(x, g, dy):
    r = lax.rsqrt(jnp.mean(x * x, axis=-1, keepdims=True) + EPS)
    xh = x * r
    dg = jnp.sum(dy * xh, axis=0, keepdims=True)
    dyg = dy * g
    dx = r * (dyg - xh * jnp.mean(dyg * xh, axis=-1, keepdims=True))
    return dx, dg


def _group_mean(t, gm):
    hi = t.astype(BF16)
    lo = (t - hi.astype(F32)).astype(BF16)
    return _mm(hi, gm) + _mm(lo, gm)


def _swap_rope_halves(x, lane):
    half = QK_ROPE // 2
    swapped = jnp.where(lane < QK_NOPE + half, pltpu.roll(x, x.shape[1] - half, 1), pltpu.roll(x, half, 1))
    return jnp.where((lane >= QK_NOPE) & (lane < QK_HEAD), swapped, 0.0)


def _qk_fwd(x, g, cos, sin, gm, lane):
    r = lax.rsqrt(_group_mean(x * x, gm) + EPS)
    n = (x * r) * g
    return n * cos + _swap_rope_halves(n, lane) * sin


def _qk_bwd(x, g, dy, cos, sin, gm, lane):
    r = lax.rsqrt(_group_mean(x * x, gm) + EPS)
    xh = x * r
    dn = dy * cos + _swap_rope_halves(dy * sin, lane)
    dg = jnp.sum(dn * xh, axis=0, keepdims=True)
    dng = dn * g
    dx = r * (dng - xh * _group_mean(dng * xh, gm))
    return dx, dg


def _row_shards_joined(ref):
    n, rows, cols = ref.shape
    return ref[...].reshape(n * rows, cols)


def _rows(tm, n):
    return pl.BlockSpec((tm, n), lambda i: (i, 0))


def _whole(shape):
    zeros = (0,) * len(shape)
    return pl.BlockSpec(shape, lambda i: zeros)


def _operands(arrays):
    return list(arrays), [_whole(a.shape) for a in arrays]


def _accumulate(ref, first, value):
    @pl.when(first)
    def _():
        ref[...] = jnp.zeros_like(ref)
    ref[...] += value


def _front_fwd(x, w, tabs):
    t_len, d = x.shape
    tm = min(TOKEN_TILE, t_len)
    hp = N_HEADS * HEAD_PAD

    def body(x_ref, gmix, win, gq, wuq, gkv, wukv, gqn, gkn, cw_ref, gm_ref, cos_ref, sin_ref,
             z_ref, qf_ref, kf_ref, kv_ref, conv_ref, ubuf):
        i = pl.program_id(0)
        h = _rms(x_ref[...], gmix[...])
        z = _mm(h.astype(BF16), win[...])
        z_ref[...] = z
        qnb = _rms(z[:, Z_Q[0]:Z_Q[1]], gq[...]).astype(BF16)
        kvb = _rms(z[:, Z_KV[0]:Z_KV[1]], gkv[...]).astype(BF16)
        kpe = z[:, Z_KPE[0]:Z_KPE[1]]
        kpe = jnp.concatenate([kpe] * PAIR, axis=1)
        cos, sin = (jnp.concatenate([t[...]] * PAIR, axis=1) for t in (cos_ref, sin_ref))
        gm = gm_ref[...]
        lane = lax.broadcasted_iota(jnp.int32, (tm, PAIR * HEAD_PAD), 1) & (HEAD_PAD - 1)
        for pr in range(N_HEADS // PAIR):
            sl = slice(pr * PAIR * HEAD_PAD, (pr + 1) * PAIR * HEAD_PAD)
            qf_ref[:, sl] = (_qk_fwd(_mm(qnb, wuq[pr]), gqn[...], cos, sin, gm, lane) * ATTN_SCALE).astype(BF16)
            kv = _mm(kvb, wukv[pr])
            kv_ref[:, sl] = jnp.where(lane < QK_NOPE, jnp.where(lane == 0, 1.0, 0.0), kv).astype(BF16)
            kf_ref[:, sl] = _qk_fwd(jnp.where(lane < QK_NOPE, kv, 0.0) + kpe, gkn[...], cos, sin, gm, lane).astype(BF16)
        u = z[:, Z_GC[0]:Z_GC[1]] * z[:, Z_XIN[0]:Z_XIN[1]]

        @pl.when(i == 0)
        def _():
            ubuf[0:HALO, :] = jnp.zeros((HALO, CONV_WIDTH), F32)
        ubuf[HALO:HALO + tm, :] = u
        cw = cw_ref[...]
        y = cw[0:1] * u + cw[1:2] * ubuf[pl.ds(HALO - 1, tm), :] + cw[2:3] * ubuf[pl.ds(HALO - 2, tm), :]
        conv_ref[...] = z[:, Z_GB[0]:Z_GB[1]] * y
        ubuf[0:HALO, :] = u[tm - HALO:tm, :]

    consts, const_specs = _operands([w["g_mix"], w["w_in"], w["g_q_lat"], w["w_uq"], w["g_kv_lat"], w["w_ukv"],
                                     w["g_qn"], w["g_kn"], w["conv_w"], tabs["gm"]])
    out_shape = (jax.ShapeDtypeStruct((t_len, Z_COLS), F32), jax.ShapeDtypeStruct((t_len, hp), BF16),
                 jax.ShapeDtypeStruct((t_len, hp), BF16), jax.ShapeDtypeStruct((t_len, hp), BF16),
                 jax.ShapeDtypeStruct((t_len, CONV_WIDTH), F32))
    return pl.pallas_call(body, name="front_fwd", grid=(t_len // tm,),
                          in_specs=[_rows(tm, d)] + const_specs + [_rows(tm, HEAD_PAD)] * 2,
                          out_specs=tuple(_rows(tm, s.shape[1]) for s in out_shape), out_shape=out_shape,
                          scratch_shapes=[pltpu.VMEM((tm + HALO, CONV_WIDTH), F32)],
                          compiler_params=_cparams(("arbitrary",)))(x, *consts, tabs["cos"], tabs["sin"])


def _attn_fwd(qf, kf, kv, gather=None):
    t_len = qf.shape[0]
    blk = min(ATTN_FWD_BLOCK, t_len)
    nb = t_len // blk
    n_sub = ATTN_Q_SUB
    bq = blk // n_sub
    chains = [(hh, a) for hh in range(2) for a in range(n_sub)]

    def body(q_ref, k_ref, kv_ref, o_ref, lse_ref):
        lane = lax.broadcasted_iota(jnp.int32, (bq, LANES), 1)
        row = lax.broadcasted_iota(jnp.int32, (bq, blk), 0)
        col = lax.broadcasted_iota(jnp.int32, (bq, blk), 1)

        def head_cols(hh):
            return slice(hh * HEAD_PAD, (hh + 1) * HEAD_PAD)

        def softmax_step(s, kvv, state, first_row=None):
            m, acc = state
            if first_row is not None:
                s = jnp.where(col <= row + first_row, s, -jnp.inf)
            m_new = jnp.maximum(m, jnp.max(s, axis=-1, keepdims=True))
            p = jnp.exp(s - m_new)
            acc = jnp.exp(m - m_new) * acc + _mm(p.astype(BF16), kvv)
            return m_new, acc

        def finish(state):
            m, acc = state
            l = jnp.sum(jnp.where(lane == 0, acc, 0.0), axis=-1, keepdims=True)
            return acc / l, jnp.broadcast_to(m + jnp.log(l), (bq, LANES))

        def qblock(i, carry):
            start = pl.multiple_of(i * blk, blk)
            rows = [pl.ds(pl.multiple_of(start + a * bq, bq), bq) for a in range(n_sub)]
            qs = {(hh, a): q_ref[rows[a], head_cols(hh)] for hh, a in chains}

            def scores(j):
                ks = pl.ds(pl.multiple_of(j * blk, blk), blk)
                return tuple(_mm_nt(qs[hh, a], k_ref[ks, head_cols(hh)]) for hh, a in chains)

            def kstep(j, carried, diagonal=False):
                ss, states = carried
                ss_next = ss if diagonal else scores(j + 1)
                ks = pl.ds(pl.multiple_of(j * blk, blk), blk)
                new = tuple(softmax_step(s, kv_ref[ks, head_cols(hh)], st, a * bq if diagonal else None)
                            for (hh, a), s, st in zip(chains, ss, states))
                return ss_next, new

            init = (jnp.full((bq, 1), -jnp.inf, F32), jnp.zeros((bq, LANES), F32))
            carried = lax.fori_loop(0, i, kstep, (scores(0), (init,) * len(chains)))
            _, states = kstep(i, carried, diagonal=True)
            for a in range(n_sub):
                (o0, lse0), (o1, lse1) = finish(states[chains.index((0, a))]), finish(states[chains.index((1, a))])
                o_ref[rows[a], :] = jnp.where(lane < V_HEAD, pltpu.roll(o0, V_HEAD, 1), o1)
                lse_ref[rows[a], head_cols(0)] = lse0
                lse_ref[rows[a], head_cols(1)] = lse1
            return carry

        lax.fori_loop(0, nb, qblock, 0)

    heads = pl.BlockSpec((t_len, 2 * HEAD_PAD), lambda h: (0, h))
    pair = pl.BlockSpec((t_len, 2 * V_HEAD), lambda h: (0, h))
    out_shape = (jax.ShapeDtypeStruct((t_len, ATTN_WIDTH), F32), jax.ShapeDtypeStruct((t_len, N_HEADS * LANES), F32))
    (attn, lse), gathered = _call_hosting_gather("attn_fwd", body, N_HEADS // 2, [heads, heads, heads], (pair, heads),
                                                 out_shape, [], (qf, kf, kv), gather)
    return attn, lse, gathered


def _call_hosting_gather(name, body, n_steps, in_specs, out_specs, out_shape, scratch_shapes, args, gather):
    if not gather:
        return pl.pallas_call(body, name=name, grid=(n_steps,), in_specs=list(in_specs), out_specs=tuple(out_specs),
                              out_shape=tuple(out_shape), scratch_shapes=list(scratch_shapes),
                              compiler_params=_cparams(("arbitrary",)))(*args), ()
    n_in, n_out, n_g = len(in_specs), len(out_shape), len(gather)

    def hosting_body(*refs):
        ins, refs = refs[:n_in], refs[n_in:]
        x_refs = [r.at[layer] for r, (_, layer) in zip(refs[:n_g], gather)]
        outs, landing, sems, scratch = (refs[n_g:n_g + n_out], refs[n_g + n_out:2 * n_g + n_out],
                                        refs[2 * n_g + n_out:2 * n_g + n_out + 3], refs[2 * n_g + n_out + 3:])
        start, forward, finish = _gather_phases(x_refs, landing, *sems)
        step = pl.program_id(0)
        pl.when(step == 0)(start)
        pl.when(step == n_steps - 1)(forward)
        body(*ins, *outs, *scratch)
        pl.when(step == n_steps - 1)(finish)

    res = pl.pallas_call(hosting_body, name=name + "_gather", grid=(n_steps,), in_specs=list(in_specs) + _any_specs(n_g),
                         out_specs=tuple(out_specs) + tuple(_any_specs(n_g)),
                         out_shape=tuple(out_shape) + _gather_out_shape([s[layer] for s, layer in gather]),
                         scratch_shapes=_gather_semaphores(n_g) + list(scratch_shapes),
                         compiler_params=_cparams(("arbitrary",)))(*args, *[s for s, _ in gather])
    return res[:n_out], res[n_out:]


def _mix_out_fwd(x, attn, conv, w):
    t_len, d = x.shape
    tm = min(LIGHT_TOKEN_TILE, t_len)

    def body(x_ref, a_ref, c_ref, goa, goc, wo, x1_ref):
        mixed = jnp.concatenate([_rms(a_ref[...], goa[...]), _rms(c_ref[...], goc[...])], axis=1)
        x1_ref[...] = x_ref[...] + _mm(mixed.astype(BF16), _row_shards_joined(wo))

    consts, const_specs = _operands([w["g_out_attn"], w["g_out_conv"], w["w_o"]])
    return pl.pallas_call(body, name="mix_out_fwd", grid=(t_len // tm,),
                          in_specs=[_rows(tm, d), _rows(tm, ATTN_WIDTH), _rows(tm, CONV_WIDTH)] + const_specs,
                          out_specs=_rows(tm, d), out_shape=jax.ShapeDtypeStruct((t_len, d), F32),
                          compiler_params=_cparams(("parallel",)))(x, attn, conv, *consts)


def _mlp_fwd(x1, w, gather=None):
    t_len, d = x1.shape
    tm = min(TOKEN_TILE, t_len)

    def body(x_ref, g, wup, wdn, x2_ref):
        x1v = x_ref[...]
        hb = _rms(x1v, g[...]).astype(BF16)
        acc = x1v
        for k in range(N_DEV):
            a = jnp.maximum(_mm(hb, wup[k]), 0.0)
            acc = acc + _mm((a * a).astype(BF16), wdn[k])
        x2_ref[...] = acc

    consts, const_specs = _operands([w["g_mlp"], w["w_up"], w["w_down"]])
    (x2,), gathered = _call_hosting_gather("mlp_fwd", body, t_len // tm, [_rows(tm, d)] + const_specs, (_rows(tm, d),),
                                           (jax.ShapeDtypeStruct((t_len, d), F32),), [], (x1, *consts), gather)
    return x2, gathered


def _ple_fwd(x2, p, w):
    t_len, d = x2.shape
    tm = min(LIGHT_TOKEN_TILE, t_len)

    def body(x_ref, p_ref, g, wg, wp, x3_ref):
        x2v = x_ref[...]
        gate = jax.nn.sigmoid(_mm(_rms(x2v, g[...]).astype(BF16), _row_shards_joined(wg)))
        pb = p_ref[...].astype(BF16)
        e = jnp.concatenate([_mm(pb, wp[k]) for k in range(N_DEV)], axis=1)
        x3_ref[...] = x2v + gate * e

    consts, const_specs = _operands([w["g_ple"], w["w_ple_gate"], w["w_ple"]])
    return pl.pallas_call(body, name="ple_fwd", grid=(t_len // tm,),
                          in_specs=[_rows(tm, d), _rows(tm, p.shape[1])] + const_specs, out_specs=_rows(tm, d),
                          out_shape=jax.ShapeDtypeStruct((t_len, d), F32),
                          compiler_params=_cparams(("parallel",)))(x2, p, *consts)


def _loss_and_grad(y, target):
    t_len, d = y.shape
    tm = min(TOKEN_TILE, t_len)

    def body(y_ref, t_ref, sq_ref, dy_ref):
        err = y_ref[...] - t_ref[...]
        dy_ref[...] = err / d
        total = jnp.sum(jnp.sum(err * err, axis=0, keepdims=True), axis=1, keepdims=True)
        _accumulate(sq_ref, pl.program_id(0) == 0, jnp.broadcast_to(total, (HALO, LANES)))

    return pl.pallas_call(body, name="loss_grad", grid=(t_len // tm,), in_specs=[_rows(tm, d), _rows(tm, d)],
                          out_specs=(_whole((HALO, LANES)), _rows(tm, d)),
                          out_shape=(jax.ShapeDtypeStruct((HALO, LANES), F32), jax.ShapeDtypeStruct((t_len, d), F32)),
                          compiler_params=_cparams(("arbitrary",)))(y, target)


def _ple_bwd(dx3, x2, p, w):
    t_len, d = x2.shape
    tm = min(LIGHT_TOKEN_TILE, t_len)

    def body(dx3_ref, x_ref, p_ref, g, wg, wp, dx2_ref, de_ref, h3_ref, dpre_ref, dg_ref):
        x2v, dx3v = x_ref[...], dx3_ref[...]
        hb = _rms(x2v, g[...]).astype(BF16)
        h3_ref[...] = hb
        w_gate = _row_shards_joined(wg)
        gate = jax.nn.sigmoid(_mm(hb, w_gate))
        pb = p_ref[...].astype(BF16)
        e = jnp.concatenate([_mm(pb, wp[k]) for k in range(N_DEV)], axis=1)
        de_ref[...] = (dx3v * gate).astype(BF16)
        dpre = ((dx3v * e) * gate * (1.0 - gate)).astype(BF16)
        dpre_ref[...] = dpre
        dx, dg = _rms_bwd(x2v, g[...], _mm_nt(dpre, w_gate))
        dx2_ref[...] = dx3v + dx
        _accumulate(dg_ref, pl.program_id(0) == 0, dg)

    consts, const_specs = _operands([w["g_ple"], w["w_ple_gate"], w["w_ple"]])
    out_shape = (jax.ShapeDtypeStruct((t_len, d), F32), jax.ShapeDtypeStruct((t_len, d), BF16),
                 jax.ShapeDtypeStruct((t_len, d), BF16), jax.ShapeDtypeStruct((t_len, d), BF16),
                 jax.ShapeDtypeStruct((1, d), F32))
    return pl.pallas_call(body, name="ple_bwd", grid=(t_len // tm,),
                          in_specs=[_rows(tm, d), _rows(tm, d), _rows(tm, p.shape[1])] + const_specs,
                          out_specs=(_rows(tm, d),) * 4 + (_whole((1, d)),), out_shape=out_shape,
                          compiler_params=_cparams(("arbitrary",)))(dx3, x2, p, *consts)


def _mlp_bwd(dx2, x1, w):
    t_len, d = x1.shape
    tm = min(TOKEN_TILE, t_len)
    fc = w["w_up"].shape[2]
    ff = N_DEV * fc

    def body(dx2_ref, x_ref, g, wup, wdn, dx1_ref, r_ref, da_ref, h2_ref, dg_ref):
        x1v, dx2v = x_ref[...], dx2_ref[...]
        hb = _rms(x1v, g[...]).astype(BF16)
        h2_ref[...] = hb
        dxb = dx2v.astype(BF16)
        dh = jnp.zeros((tm, d), F32)
        for k in range(N_DEV):
            a = jnp.maximum(_mm(hb, wup[k]), 0.0)
            r_ref[:, k * fc:(k + 1) * fc] = (a * a).astype(BF16)
            da = (_mm_nt(dxb, wdn[k]) * (2.0 * a)).astype(BF16)
            da_ref[:, k * fc:(k + 1) * fc] = da
            dh = dh + _mm_nt(da, wup[k])
        dx, dg = _rms_bwd(x1v, g[...], dh)
        dx1_ref[...] = dx2v + dx
        _accumulate(dg_ref, pl.program_id(0) == 0, dg)

    consts, const_specs = _operands([w["g_mlp"], w["w_up"], w["w_down"]])
    out_shape = (jax.ShapeDtypeStruct((t_len, d), F32), jax.ShapeDtypeStruct((t_len, ff), BF16),
                 jax.ShapeDtypeStruct((t_len, ff), BF16), jax.ShapeDtypeStruct((t_len, d), BF16),
                 jax.ShapeDtypeStruct((1, d), F32))
    return pl.pallas_call(body, name="mlp_bwd", grid=(t_len // tm,), in_specs=[_rows(tm, d), _rows(tm, d)] + const_specs,
                          out_specs=(_rows(tm, d), _rows(tm, ff), _rows(tm, ff), _rows(tm, d), _whole((1, d))),
                          out_shape=out_shape, compiler_params=_cparams(("arbitrary",)))(dx2, x1, *consts)


def _mix_out_bwd(dx1, attn, conv, w):
    t_len, d = dx1.shape
    tm = min(LIGHT_TOKEN_TILE, t_len)

    def body(dx1_ref, a_ref, c_ref, goa, goc, wo, mixed_ref, da_ref, dc_ref, dgoa_ref, dgoc_ref):
        av, cv = a_ref[...], c_ref[...]
        mixed_ref[...] = jnp.concatenate([_rms(av, goa[...]), _rms(cv, goc[...])], axis=1).astype(BF16)
        dmixed = _mm_nt(dx1_ref[...].astype(BF16), _row_shards_joined(wo))
        da, dga = _rms_bwd(av, goa[...], dmixed[:, :ATTN_WIDTH])
        dc, dgc = _rms_bwd(cv, goc[...], dmixed[:, ATTN_WIDTH:])
        da_ref[...] = da
        dc_ref[...] = dc
        first = pl.program_id(0) == 0
        _accumulate(dgoa_ref, first, dga)
        _accumulate(dgoc_ref, first, dgc)

    consts, const_specs = _operands([w["g_out_attn"], w["g_out_conv"], w["w_o"]])
    out_shape = (jax.ShapeDtypeStruct((t_len, d), BF16), jax.ShapeDtypeStruct((t_len, ATTN_WIDTH), F32),
                 jax.ShapeDtypeStruct((t_len, CONV_WIDTH), F32), jax.ShapeDtypeStruct((1, ATTN_WIDTH), F32),
                 jax.ShapeDtypeStruct((1, CONV_WIDTH), F32))
    out_specs = (_rows(tm, d), _rows(tm, ATTN_WIDTH), _rows(tm, CONV_WIDTH), _whole((1, ATTN_WIDTH)),
                 _whole((1, CONV_WIDTH)))
    return pl.pallas_call(body, name="mix_out_bwd", grid=(t_len // tm,),
                          in_specs=[_rows(tm, d), _rows(tm, ATTN_WIDTH), _rows(tm, CONV_WIDTH)] + const_specs,
                          out_specs=out_specs, out_shape=out_shape,
                          compiler_params=_cparams(("arbitrary",)))(dx1, attn, conv, *consts)


def _attn_bwd(qf, kf, kv, o, do, lse, scatter):
    t_len = qf.shape[0]
    blk = min(ATTN_BLOCK, t_len)
    nb = t_len // blk
    n_sub = min(ATTN_KV_SUB, nb)
    reps = blk // LANES

    def body(q_ref, k_ref, kv_ref, o_ref, do_ref, lse_ref, dq_ref, dk_ref, dkv_ref, delta_ref, dob_ref):
        hd = pl.program_id(0)
        lane = lax.broadcasted_iota(jnp.int32, (blk, LANES), 1)
        even = (lane * 0 + hd % 2) == 0
        mine = jnp.where(lane < V_HEAD, 0, 1) == hd % 2
        row = lax.broadcasted_iota(jnp.int32, (blk, blk), 0)
        col = lax.broadcasted_iota(jnp.int32, (blk, blk), 1)
        dq_ref[...] = jnp.zeros_like(dq_ref)

        def prepare(i, carry):
            qs = pl.ds(pl.multiple_of(i * blk, blk), blk)
            dov = do_ref[qs, :]
            prod = jnp.where(mine, dov * o_ref[qs, :], 0.0)
            delta_ref[qs, :] = jnp.broadcast_to(jnp.sum(prod, axis=-1, keepdims=True), (blk, LANES))
            moved = jnp.where(even, pltpu.roll(dov, V_HEAD, 1), dov)
            dob_ref[qs, :] = jnp.where(lane >= V_HEAD, moved, 0.0).astype(BF16)
            return carry
        lax.fori_loop(0, nb, prepare, 0)

        def kvblock(jj, carry):
            base = jj * n_sub
            kss = [pl.ds(pl.multiple_of((base + a) * blk, blk), blk) for a in range(n_sub)]
            k = [k_ref[ks, :] for ks in kss]
            kvv = [kv_ref[ks, :] for ks in kss]

            def products(i):
                qs = pl.ds(pl.multiple_of(i * blk, blk), blk)
                q, dob = q_ref[qs, :], dob_ref[qs, :]
                return tuple((_mm_nt(q, k[a]), _mm_nt(dob, kvv[a])) for a in range(n_sub))

            def qstep(i, raw, accs, kinds):
                qs = pl.ds(pl.multiple_of(i * blk, blk), blk)
                q = q_ref[qs, :]
                dob = dob_ref[qs, :]
                lse_t = jnp.concatenate([lse_ref[qs, :]] * reps, axis=1)
                delta_t = jnp.concatenate([delta_ref[qs, :]] * reps, axis=1)
                new, dq_add = [], None
                for a in range(n_sub):
                    if kinds[a] is None:
                        new.append(accs[a])
                        continue
                    dk_acc, dv_acc = accs[a]
                    s, dp = raw[a]
                    if kinds[a]:
                        s = jnp.where(col <= row, s, -jnp.inf)
                    p = jnp.exp(s - lse_t)
                    ds = (p * (dp - delta_t)).astype(BF16)
                    new.append((dk_acc + _mm_tn(ds, q), dv_acc + _mm_tn(p.astype(BF16), dob)))
                    part = _mm(ds, k[a])
                    dq_add = part if dq_add is None else dq_add + part
                dq_ref[qs, :] += dq_add
                return tuple(new)

            zero = jnp.zeros((blk, LANES), F32)
            accs = ((zero, zero),) * n_sub
            for b in range(n_sub):
                accs = qstep(base + b, products(base + b), accs, tuple((a == b) if a <= b else None for a in range(n_sub)))

            def pipelined(i, carried):
                raw, acc = carried
                return products(jnp.minimum(i + 1, nb - 1)), qstep(i, raw, acc, (False,) * n_sub)

            first = base + n_sub
            _, accs = lax.fori_loop(first, nb, pipelined, (products(jnp.minimum(first, nb - 1)), accs))
            for a in range(n_sub):
                dk_ref[kss[a], :] = accs[a][0]
                dkv_ref[kss[a], :] = accs[a][1]
            return carry
        lax.fori_loop(0, nb // n_sub, kvblock, 0)

    n_sc = len(scatter[0])

    def hosting_body(*refs):
        ins, rest = refs[:6], refs[6 + 2 * n_sc:]
        parts = refs[6:6 + n_sc]
        outs, landed, sems, scratch = rest[:3], rest[3:3 + n_sc], rest[3 + n_sc:6 + n_sc], rest[6 + n_sc:]
        start, finish = _scatter_phases(parts, landed, *sems, scatter[2])
        hd = pl.program_id(0)
        pl.when(hd == 0)(start)
        body(*ins, *outs, *scratch)
        pl.when(hd == N_HEADS - 1)(finish)

    head = pl.BlockSpec((t_len, HEAD_PAD), lambda h: (0, h))
    pair = pl.BlockSpec((t_len, 2 * V_HEAD), lambda h: (0, h // 2))
    out = jax.ShapeDtypeStruct((t_len, N_HEADS * HEAD_PAD), F32)
    vmem_scratch = [pltpu.VMEM((t_len, LANES), F32), pltpu.VMEM((t_len, LANES), BF16)]
    res = pl.pallas_call(hosting_body, name="attn_bwd_scatter", grid=(N_HEADS,),
                         in_specs=[head, head, head, pair, pair, head] + _any_specs(2 * n_sc),
                         out_specs=(head, head, head) + tuple(_any_specs(n_sc)),
                         out_shape=(out, out, out) + _same_shapes(scatter[1]),
                         scratch_shapes=_scatter_semaphores(n_sc) + vmem_scratch,
                         input_output_aliases={6 + n_sc + a: 3 + a for a in range(n_sc)},
                         compiler_params=_cparams(("arbitrary",)))(qf, kf, kv, o, do, lse, *scatter[0], *scatter[1])
    return res[0], res[1], res[2], res[3:]


def _front_bwd(x, z, dx1, dqf, dkf, dkv_in, dconv, w, tabs):
    t_len, d = x.shape
    tm = min(TOKEN_TILE, t_len)
    nt = t_len // tm
    hb_per_tile = tm // HALO
    n_halo = t_len // HALO
    hp = N_HEADS * HEAD_PAD

    def body(x_ref, z_ref, zp_ref, zn_ref, dx1_ref, dqf_ref, dkf_ref, dkv_ref, dc_ref, dcn_ref,
             gmix, win, gq, wuq, gkv, wukv, gqn, gkn, cw_ref, gm_ref, cos_ref, sin_ref,
             dx_ref, dz_ref, h_ref, qn_ref, kvn_ref, dqr_ref, dkvr_ref,
             dgmix_ref, dgq_ref, dgkv_ref, dgqn_ref, dgkn_ref, dcw_ref, ubuf, dybuf):
        i = pl.program_id(0)
        first = i == 0
        xv, zv = x_ref[...], z_ref[...]
        hb = _rms(xv, gmix[...]).astype(BF16)
        h_ref[...] = hb
        zq, zkv = zv[:, Z_Q[0]:Z_Q[1]], zv[:, Z_KV[0]:Z_KV[1]]
        qnb = _rms(zq, gq[...]).astype(BF16)
        qn_ref[...] = qnb
        kvb = _rms(zkv, gkv[...]).astype(BF16)
        kvn_ref[...] = kvb
        kpe = zv[:, Z_KPE[0]:Z_KPE[1]]
        kpe = jnp.concatenate([kpe] * PAIR, axis=1)
        cos, sin = (jnp.concatenate([t[...]] * PAIR, axis=1) for t in (cos_ref, sin_ref))
        gm = gm_ref[...]
        width = PAIR * HEAD_PAD
        lane = lax.broadcasted_iota(jnp.int32, (tm, width), 1) & (HEAD_PAD - 1)
        is_nope = lane < QK_NOPE
        is_rope = (lane >= QK_NOPE) & (lane < QK_HEAD)
        dkpe = jnp.zeros((tm, width), F32)
        dgqn = jnp.zeros((1, width), F32)
        dgkn = jnp.zeros((1, width), F32)
        dqn = jnp.zeros((tm, Q_LORA), F32)
        dkvn = jnp.zeros((tm, KV_LORA), F32)
        for pr in range(N_HEADS // PAIR):
            sl = slice(pr * width, (pr + 1) * width)
            dxq, dg = _qk_bwd(_mm(qnb, wuq[pr]), gqn[...], dqf_ref[:, sl] * ATTN_SCALE, cos, sin, gm, lane)
            dxq = dxq.astype(BF16)
            dqr_ref[:, sl] = dxq
            dqn = dqn + _mm_nt(dxq, wuq[pr])
            dgqn = dgqn + dg
            k_raw = jnp.where(is_nope, _mm(kvb, wukv[pr]), 0.0) + kpe
            dxk, dg = _qk_bwd(k_raw, gkn[...], dkf_ref[:, sl], cos, sin, gm, lane)
            dkv = jnp.where(is_nope, dxk, dkv_ref[:, sl]).astype(BF16)
            dkvr_ref[:, sl] = dkv
            dkvn = dkvn + _mm_nt(dkv, wukv[pr])
            dkpe = dkpe + jnp.where(is_rope, dxk, 0.0)
            dgkn = dgkn + dg
        dkpe = dkpe[:, :HEAD_PAD] + dkpe[:, HEAD_PAD:]
        _accumulate(dgqn_ref, first, dgqn[:, :HEAD_PAD] + dgqn[:, HEAD_PAD:])
        _accumulate(dgkn_ref, first, dgkn[:, :HEAD_PAD] + dgkn[:, HEAD_PAD:])
        dzq, dg = _rms_bwd(zq, gq[...], dqn)
        _accumulate(dgq_ref, first, dg)
        dzkv, dg = _rms_bwd(zkv, gkv[...], dkvn)
        _accumulate(dgkv_ref, first, dg)

        gb, gc, xin = zv[:, Z_GB[0]:Z_GB[1]], zv[:, Z_GC[0]:Z_GC[1]], zv[:, Z_XIN[0]:Z_XIN[1]]
        u = gc * xin
        dcv = dc_ref[...]
        dy = dcv * gb
        zp, zn = zp_ref[...], zn_ref[...]
        ubuf[0:HALO, :] = (zp[:, Z_GC[0]:Z_GC[1]] * zp[:, Z_XIN[0]:Z_XIN[1]]) * jnp.where(first, 0.0, 1.0)
        ubuf[HALO:HALO + tm, :] = u
        dybuf[0:tm, :] = dy
        dybuf[tm:tm + HALO, :] = (dcn_ref[...] * zn[:, Z_GB[0]:Z_GB[1]]) * jnp.where(i == nt - 1, 0.0, 1.0)
        cw = cw_ref[...]
        u1, u2 = ubuf[pl.ds(HALO - 1, tm), :], ubuf[pl.ds(HALO - 2, tm), :]
        y = cw[0:1] * u + cw[1:2] * u1 + cw[2:3] * u2
        du = cw[0:1] * dy + cw[1:2] * dybuf[pl.ds(1, tm), :] + cw[2:3] * dybuf[pl.ds(2, tm), :]
        dcw = jnp.concatenate([jnp.sum(dy * u, axis=0, keepdims=True), jnp.sum(dy * u1, axis=0, keepdims=True),
                               jnp.sum(dy * u2, axis=0, keepdims=True), jnp.zeros((HALO - 3, CONV_WIDTH), F32)], axis=0)
        _accumulate(dcw_ref, first, dcw)

        dz_ref[:, Z_Q[0]:Z_Q[1]] = dzq.astype(BF16)
        dz_ref[:, Z_KV[0]:Z_KV[1]] = dzkv.astype(BF16)
        dz_ref[:, Z_GB[0]:Z_GB[1]] = (dcv * y).astype(BF16)
        dz_ref[:, Z_GC[0]:Z_GC[1]] = (du * xin).astype(BF16)
        dz_ref[:, Z_XIN[0]:Z_XIN[1]] = (du * gc).astype(BF16)
        dz_ref[:, Z_KPE[0]:Z_KPE[1]] = dkpe.astype(BF16)
        dx, dg = _rms_bwd(xv, gmix[...], _mm_nt(dz_ref[...], win[...]))
        dx_ref[...] = dx1_ref[...] + dx
        _accumulate(dgmix_ref, first, dg)

    prev_halo = lambda n: pl.BlockSpec((HALO, n), lambda i: (jnp.maximum(i * hb_per_tile - 1, 0), 0))
    next_halo = lambda n: pl.BlockSpec((HALO, n), lambda i: (jnp.minimum((i + 1) * hb_per_tile, n_halo - 1), 0))
    consts, const_specs = _operands([w["g_mix"], w["w_in"], w["g_q_lat"], w["w_uq"], w["g_kv_lat"], w["w_ukv"],
                                     w["g_qn"], w["g_kn"], w["conv_w"], tabs["gm"]])
    in_specs = ([_rows(tm, d), _rows(tm, Z_COLS), prev_halo(Z_COLS), next_halo(Z_COLS), _rows(tm, d), _rows(tm, hp),
                 _rows(tm, hp), _rows(tm, hp), _rows(tm, CONV_WIDTH), next_halo(CONV_WIDTH)]
                + const_specs + [_rows(tm, HEAD_PAD)] * 2)
    out_shape = (jax.ShapeDtypeStruct((t_len, d), F32), jax.ShapeDtypeStruct((t_len, Z_COLS), BF16),
                 jax.ShapeDtypeStruct((t_len, d), BF16), jax.ShapeDtypeStruct((t_len, Q_LORA), BF16),
                 jax.ShapeDtypeStruct((t_len, KV_LORA), BF16), jax.ShapeDtypeStruct((t_len, hp), BF16),
                 jax.ShapeDtypeStruct((t_len, hp), BF16),
                 jax.ShapeDtypeStruct((1, d), F32), jax.ShapeDtypeStruct((1, Q_LORA), F32),
                 jax.ShapeDtypeStruct((1, KV_LORA), F32), jax.ShapeDtypeStruct((1, LANES), F32),
                 jax.ShapeDtypeStruct((1, LANES), F32), jax.ShapeDtypeStruct((HALO, CONV_WIDTH), F32))
    out_specs = tuple(_rows(tm, s.shape[1]) for s in out_shape[:7]) + tuple(_whole(s.shape) for s in out_shape[7:])
    return pl.pallas_call(body, name="front_bwd", grid=(nt,), in_specs=in_specs, out_specs=out_specs, out_shape=out_shape,
                          scratch_shapes=[pltpu.VMEM((tm + HALO, CONV_WIDTH), F32), pltpu.VMEM((tm + HALO, CONV_WIDTH), F32)],
                          compiler_params=_cparams(("arbitrary",)))(
                              x, z, z, z, dx1, dqf, dkf, dkv_in, dconv, dconv, *consts, tabs["cos"], tabs["sin"])


def _wgrad(a, b, shard_cols=None, shard_rows=None, out_dtype=BF16):
    t_len, kk = a.shape
    nn = b.shape[1]
    tk = min(kk, WGRAD_TILE)
    tn = next(c for c in range(min(nn, WGRAD_TILE), 0, -LANES) if nn % c == 0 and c % (shard_cols or LANES) == 0)
    tt = min(t_len, WGRAD_TOKENS)
    nt = t_len // tt
    per_block = tn // shard_cols if shard_cols else tk // shard_rows if shard_rows else 1

    def body(a_ref, b_ref, o_ref, acc):
        t = pl.program_id(2)

        @pl.when(t == 0)
        def _():
            acc[...] = jnp.zeros_like(acc)
        acc[...] += _mm_tn(a_ref[...].astype(BF16), b_ref[...].astype(BF16))

        @pl.when(t == nt - 1)
        def _():
            if shard_cols:
                for s in range(per_block):
                    o_ref[s] = acc[:, s * shard_cols:(s + 1) * shard_cols].astype(out_dtype)
            elif shard_rows:
                for s in range(per_block):
                    o_ref[s] = acc[s * shard_rows:(s + 1) * shard_rows, :].astype(out_dtype)
            else:
                o_ref[...] = acc[...].astype(out_dtype)

    if shard_cols:
        out_shape = jax.ShapeDtypeStruct((nn // shard_cols, kk, shard_cols), out_dtype)
        out_spec = pl.BlockSpec((per_block, tk, shard_cols), lambda i, j, t: (j, i, 0))
    elif shard_rows:
        out_shape = jax.ShapeDtypeStruct((kk // shard_rows, shard_rows, nn), out_dtype)
        out_spec = pl.BlockSpec((per_block, shard_rows, tn), lambda i, j, t: (i, 0, j))
    else:
        out_shape = jax.ShapeDtypeStruct((kk, nn), out_dtype)
        out_spec = pl.BlockSpec((tk, tn), lambda i, j, t: (i, j))
    return pl.pallas_call(body, name="wgrad", grid=(kk // tk, nn // tn, nt),
                          in_specs=[pl.BlockSpec((tt, tk), lambda i, j, t: (t, i)),
                                    pl.BlockSpec((tt, tn), lambda i, j, t: (t, j))],
                          out_specs=out_spec, out_shape=out_shape, scratch_shapes=[pltpu.VMEM((tk, tn), F32)],
                          compiler_params=_cparams(("parallel", "parallel", "arbitrary")))(a, b)


def _my_place():
    return lax.axis_index("x"), lax.axis_index("y"), lax.axis_index("c")


def _any_specs(n):
    return [pl.BlockSpec(memory_space=pl.ANY)] * n


def _all_gather(blocks):
    n = len(blocks)

    def body(*refs):
        start, forward, finish = _gather_phases(refs[:n], refs[n:2 * n], *refs[2 * n:])
        start()
        forward()
        finish()

    return pl.pallas_call(body, name="all_gather", out_shape=_gather_out_shape(blocks), in_specs=_any_specs(n),
                          out_specs=tuple(_any_specs(n)), scratch_shapes=_gather_semaphores(n))(*blocks)


def _gather_out_shape(blocks):
    return tuple(jax.ShapeDtypeStruct((N_DEV,) + b.shape, b.dtype) for b in blocks)


def _gather_semaphores(n):
    return [pltpu.SemaphoreType.DMA((n, 7)), pltpu.SemaphoreType.DMA((n, 7)), pltpu.SemaphoreType.DMA((n,))]


def _gather_phases(x_refs, out_refs, send_sems, recv_sems, local_sems):
    n = len(x_refs)
    x, y, c = _my_place()
    me, sibling = (x, y, c), (x, y, 1 - c)
    chips = [(1 - x, y), (x, 1 - y), (1 - x, 1 - y)]

    def slot(a, px, py, pc):
        return out_refs[a].at[4 * px + 2 * py + pc]

    def copy(a, k, blk, to, src=None):
        return pltpu.make_async_remote_copy(src_ref=slot(a, *blk) if src is None else src, dst_ref=slot(a, *blk),
                                            send_sem=send_sems.at[a, k], recv_sem=recv_sems.at[a, k],
                                            device_id=to, device_id_type=MESH)

    def own(a):
        return pltpu.make_async_copy(x_refs[a], slot(a, *me), local_sems.at[a])

    def first_hop(a):
        return [copy(a, 0, me, sibling, src=x_refs[a])] + [copy(a, 1 + j, me, (*chip, c), src=x_refs[a])
                                                           for j, chip in enumerate(chips)]

    def passed_on(a):
        return [copy(a, 4 + j, (*chip, c), sibling) for j, chip in enumerate(chips)]

    def start():
        for a in range(n):
            own(a).start()
        for a in range(n):
            for cp in first_hop(a):
                cp.start()

    def forward():
        for j, chip in enumerate(chips):
            for a in range(n):
                copy(a, 1 + j, (*chip, c), me).wait_recv()
                passed_on(a)[j].start()

    def finish():
        for a in range(n):
            copy(a, 0, sibling, me).wait_recv()
        for j, chip in enumerate(chips):
            for a in range(n):
                copy(a, 4 + j, (*chip, 1 - c), me).wait_recv()
        for a in range(n):
            for cp in first_hop(a) + passed_on(a):
                cp.wait_send()
            own(a).wait()

    return start, forward, finish


def _scatter_exchange(parts, landed, layers):
    n = len(parts)

    def body(*refs):
        start, finish = _scatter_phases(refs[:n], refs[2 * n:3 * n], *refs[3 * n:], layers)
        start()
        finish()

    return pl.pallas_call(body, name="scatter_exchange", out_shape=_same_shapes(landed), in_specs=_any_specs(2 * n),
                          out_specs=tuple(_any_specs(n)), scratch_shapes=_scatter_semaphores(n),
                          input_output_aliases={n + a: a for a in range(n)})(*parts, *landed)


def _same_shapes(arrays):
    return tuple(jax.ShapeDtypeStruct(a.shape, a.dtype) for a in arrays)


def _scatter_semaphores(n):
    return [pltpu.SemaphoreType.DMA((n, N_DEV - 1)), pltpu.SemaphoreType.DMA((n, N_DEV - 1)), pltpu.SemaphoreType.DMA((n,))]


def _scatter_phases(part_refs, landed_refs, send_sems, recv_sems, local_sems, layers):
    n = len(part_refs)
    x, y, c = _my_place()
    flips = [(0, 0, 1), (1, 0, 0), (0, 1, 0), (1, 1, 0), (1, 0, 1), (0, 1, 1), (1, 1, 1)]
    peers = [((1 - x) if fx else x, (1 - y) if fy else y, (1 - c) if fc else c) for fx, fy, fc in flips]
    my_k = 4 * x + 2 * y + c

    def index(peer):
        return 4 * peer[0] + 2 * peer[1] + peer[2]

    def send(a, r):
        return pltpu.make_async_remote_copy(src_ref=part_refs[a].at[index(peers[r])], dst_ref=landed_refs[a].at[my_k, layers[a]],
                                            send_sem=send_sems.at[a, r], recv_sem=recv_sems.at[a, r],
                                            device_id=peers[r], device_id_type=MESH)

    def arrival(a, r):
        return pltpu.make_async_remote_copy(src_ref=part_refs[a].at[my_k], dst_ref=landed_refs[a].at[index(peers[r]), layers[a]],
                                            send_sem=send_sems.at[a, r], recv_sem=recv_sems.at[a, r],
                                            device_id=peers[r], device_id_type=MESH)

    def own(a):
        return pltpu.make_async_copy(part_refs[a].at[my_k], landed_refs[a].at[my_k, layers[a]], local_sems.at[a])

    def start():
        for a in range(n):
            own(a).start()
        for r in range(len(peers)):
            for a in range(n):
                send(a, r).start()

    def finish():
        for r in range(len(peers)):
            for a in range(n):
                arrival(a, r).wait_recv()
        for r in range(len(peers)):
            for a in range(n):
                send(a, r).wait_send()
        for a in range(n):
            own(a).wait()

    return start, finish


def _row_block(rows):
    return ROW_BLOCK if rows % ROW_BLOCK == 0 else rows


def _sum_leading(parts):
    n_part, shape = parts.shape[0], parts.shape[1:]
    rows, cols = shape[-2:]
    rb = _row_block(rows)

    def body(p_ref, o_ref):
        acc = p_ref[0].astype(F32)
        for k in range(1, n_part):
            acc = acc + p_ref[k].astype(F32)
        o_ref[...] = acc

    if len(shape) == 3:
        grid = (shape[0], rows // rb)
        in_spec = pl.BlockSpec((n_part, None, rb, cols), lambda l, i: (0, l, i, 0))
        out_spec = pl.BlockSpec((None, rb, cols), lambda l, i: (l, i, 0))
    else:
        grid = (rows // rb,)
        in_spec, out_spec = pl.BlockSpec((n_part, rb, cols), lambda i: (0, i, 0)), _rows(rb, cols)
    return pl.pallas_call(body, name="sum_leading", grid=grid, in_specs=[in_spec], out_specs=out_spec,
                          out_shape=jax.ShapeDtypeStruct(shape, F32),
                          compiler_params=_cparams(("parallel",) * len(grid)))(parts)


def _adamw(w, g, m, v):
    shape = w.shape
    rows, cols = shape[-2:]
    rb = _row_block(rows)

    def body(w_ref, g_ref, m_ref, v_ref, d_ref, nm_ref, nv_ref):
        gv = g_ref[...]
        nm = ADAM_B1 * m_ref[...] + (1.0 - ADAM_B1) * gv
        nv = ADAM_B2 * v_ref[...] + (1.0 - ADAM_B2) * jnp.square(gv)
        m_hat = nm / (1.0 - ADAM_B1 ** ADAM_STEP)
        v_hat = nv / (1.0 - ADAM_B2 ** ADAM_STEP)
        d_ref[...] = -ADAM_LR * (m_hat / (jnp.sqrt(v_hat) + ADAM_EPS) + ADAM_WD * w_ref[...])
        nm_ref[...] = nm
        nv_ref[...] = nv

    if len(shape) == 3:
        grid, spec = (shape[0], rows // rb), pl.BlockSpec((None, rb, cols), lambda l, i: (l, i, 0))
    else:
        grid, spec = (rows // rb,), _rows(rb, cols)
    out = jax.ShapeDtypeStruct(shape, F32)
    return pl.pallas_call(body, name="adamw", grid=grid, in_specs=[spec] * 4, out_specs=(spec,) * 3,
                          out_shape=(out,) * 3, compiler_params=_cparams(("parallel",) * len(grid)))(w, g, m, v)


def _rope_tables(positions):
    t_len = positions.shape[0]
    inv_freq = 1.0 / (ROPE_THETA ** (jnp.arange(0, QK_ROPE, 2, dtype=F32) / QK_ROPE))
    ang = positions.astype(F32)[:, None] * inv_freq
    c, s = jnp.cos(ang), jnp.sin(ang)
    one, zero = jnp.ones((t_len, QK_NOPE), F32), jnp.zeros((t_len, QK_NOPE), F32)
    cos = jnp.concatenate([one, c, c, one[:, :LANES - QK_HEAD]], axis=1)
    sin = jnp.concatenate([zero, -s, s, zero[:, :LANES - QK_HEAD]], axis=1)
    idx = jnp.arange(PAIR * HEAD_PAD)
    lane, head = idx % HEAD_PAD, idx // HEAD_PAD
    grp = jnp.where(lane < QK_NOPE, 0, jnp.where(lane < QK_HEAD, 1, 2)) + 3 * head
    val = jnp.where(lane < QK_NOPE, 1.0 / QK_NOPE, jnp.where(lane < QK_HEAD, 1.0 / QK_ROPE, 0.0))
    gm = jnp.where(grp[:, None] == grp[None, :], val[None, :], 0.0).astype(BF16)
    return {"cos": cos, "sin": sin, "gm": gm}


def _head_gain(g_nope, g_rope):
    one = jnp.concatenate([g_nope, g_rope, jnp.zeros((HEAD_PAD - QK_HEAD,), F32)])
    return jnp.concatenate([one] * PAIR).reshape(1, PAIR * HEAD_PAD)


def _head_pairs(w):
    return jnp.concatenate([w[k::PAIR] for k in range(PAIR)], axis=2)


def _padded_w_in(shards):
    width = shards.shape[2]
    zeros = jnp.zeros((shards.shape[1], QK_NOPE), shards.dtype)

    def natural(start, end):
        pieces = []
        for k in range(N_DEV):
            lo, hi = max(start, k * width), min(end, (k + 1) * width)
            if lo < hi:
                pieces.append(shards[k][:, lo - k * width:hi - k * width])
        return pieces

    o2, o3 = Q_LORA + KV_LORA, Q_LORA + KV_LORA + QK_ROPE
    return jnp.concatenate(natural(0, o2) + natural(o3, N_DEV * width) + [zeros] + natural(o2, o3)
                           + [zeros[:, :LANES - QK_HEAD]], axis=1)


def _w_in_grad_shards(d_in):
    o2, o3 = Q_LORA + KV_LORA, Q_LORA + KV_LORA + QK_ROPE
    width = (Z_XIN[1] + QK_ROPE) // N_DEV
    runs = [(0, o2, 0), (o2, o3, Z_KPE[0] + QK_NOPE), (o3, N_DEV * width, o2)]
    shards = []
    for k in range(N_DEV):
        pieces = []
        for start, end, at in runs:
            lo, hi = max(start, k * width), min(end, (k + 1) * width)
            if lo < hi:
                pieces.append(d_in[:, at + lo - start:at + hi - start])
        shards.append(pieces[0] if len(pieces) == 1 else jnp.concatenate(pieces, axis=1))
    return jnp.stack(shards)


def kernel(x, p, positions, g_mix, w_in, g_q_lat, w_uq, g_kv_lat, w_ukv, g_qn_nope, g_qn_rope, g_kn_nope, g_kn_rope, conv_w, g_out_attn, g_out_conv, w_o, g_mlp, w_up, w_down, g_ple, w_ple_gate, w_ple, loss_target, m_g_mix, m_w_in, m_g_q_lat, m_w_uq, m_g_kv_lat, m_w_ukv, m_g_qn_nope, m_g_qn_rope, m_g_kn_nope, m_g_kn_rope, m_conv_w, m_g_out_attn, m_g_out_conv, m_w_o, m_g_mlp, m_w_up, m_w_down, m_g_ple, m_w_ple_gate, m_w_ple, v_g_mix, v_w_in, v_g_q_lat, v_w_uq, v_g_kv_lat, v_w_ukv, v_g_qn_nope, v_g_qn_rope, v_g_kn_nope, v_g_kn_rope, v_conv_w, v_g_out_attn, v_g_out_conv, v_w_o, v_g_mlp, v_w_up, v_w_down, v_g_ple, v_w_ple_gate, v_w_ple):
    given = dict(locals())
    weights = {n: given[n] for n in WEIGHT_NAMES}
    gains = {n: given[n] for n in GAIN_NAMES}
    depth = w_in.shape[0]
    xs, target = x[0], loss_target[0]
    d_model = xs.shape[1]
    uq_cols = w_uq.shape[2]
    n_taps = conv_w.shape[1]

    mat_names = [n for n in SHARD_NAMES if n != "conv_w"]
    local = [weights[n].astype(BF16) for n in mat_names]
    local[1] = jnp.pad(local[1], ((0, 0), (0, 0), (0, HEAD_PAD - uq_cols)))
    local = dict(zip(mat_names, local))
    front_names = ("w_in", "w_uq", "w_ukv")
    first = _all_gather([local[n][0] for n in front_names] + [conv_w])
    conv_full = jnp.transpose(first[-1], (1, 2, 0, 3)).reshape(depth, n_taps, -1)
    tabs = _rope_tables(positions[0])

    def front_weights(layer, full):
        lw = {n: gains[n][layer].reshape(1, -1) for n in GAIN_NAMES}
        lw.update({"w_in": _padded_w_in(full["w_in"]), "w_uq": _head_pairs(full["w_uq"]),
                   "w_ukv": _head_pairs(full["w_ukv"]),
                   "conv_w": jnp.pad(conv_full[layer], ((0, HALO - n_taps), (0, 0))),
                   "g_qn": _head_gain(g_qn_nope[layer], g_qn_rope[layer]),
                   "g_kn": _head_gain(g_kn_nope[layer], g_kn_rope[layer])})
        return lw

    def rest_weights(full):
        return {"w_ple": full["w_ple"], "w_up": full["w_up"], "w_down": full["w_down"],
                "w_o": full["w_o"], "w_ple_gate": full["w_ple_gate"]}

    saved, layer_w = [], []
    cur = xs
    gathered = dict(zip(front_names, first[:-1]))
    mlp_names = ("w_up", "w_down")
    for layer in range(depth):
        w = front_weights(layer, gathered)
        z, qf, kf, kv, conv = _front_fwd(cur, w, tabs)
        lots = [[n for n in mat_names if n in mlp_names], [n for n in mat_names if n not in mlp_names]]
        behind_attn, behind_mlp = (lots[1], lots[0]) if layer == 0 else lots
        if layer + 1 == depth:
            behind_attn, behind_mlp = [], []
        wanted = [(n, 0) for n in mat_names if n not in front_names] if layer == 0 else []
        wanted += [(n, layer + 1) for n in behind_attn]
        attn, lse, got = _attn_fwd(qf, kf, kv, gather=[(local[n], at) for n, at in wanted])
        coming = {}
        for (n, at), g in zip(wanted, got):
            (gathered if at == layer else coming)[n] = g
        w.update(rest_weights(gathered))
        layer_w.append(w)
        x1 = _mix_out_fwd(cur, attn, conv, w)
        x2, got = _mlp_fwd(x1, w, gather=[(local[n], layer + 1) for n in behind_mlp])
        coming.update(zip(behind_mlp, got))
        gathered = coming
        x3 = _ple_fwd(x2, p[layer, 0], w)
        saved.append(dict(x=cur, z=z, qf=qf, kf=kf, kv=kv, conv=conv, attn=attn, lse=lse, x1=x1, x2=x2))
        cur = x3

    sq, dx = _loss_and_grad(cur, target)
    loss = lax.psum(0.5 / d_model * sq[0, 0], ("x", "y", "c"))

    landed = {n: lax.empty((N_DEV, depth) + weights[n].shape[1:], BF16) for n in SHARD_NAMES}
    gain_grads = [None] * depth
    late = {}
    for layer in reversed(range(depth)):
        w, s = layer_w[layer], saved[layer]
        pl_in = p[layer, 0]
        dx2, de, h3, dpre, dg_ple = _ple_bwd(dx, s["x2"], pl_in, w)
        dx1, r, da, h2, dg_mlp = _mlp_bwd(dx2, s["x1"], w)
        mixed, dattn, dconv, dg_oa, dg_oc = _mix_out_bwd(dx1, s["attn"], s["conv"], w)
        sending = {"w_o": (_wgrad(mixed, dx1, shard_rows=w_o.shape[1]), layer),
                   "w_up": (_wgrad(h2, da, shard_cols=w_up.shape[2]), layer),
                   "w_down": (_wgrad(r, dx2, shard_rows=w_down.shape[1]), layer),
                   "w_ple_gate": (_wgrad(h3, dpre, shard_rows=w_ple_gate.shape[1]), layer),
                   "w_ple": (_wgrad(pl_in, de, shard_cols=w_ple.shape[2]), layer), **late}
        names = list(sending)
        dqf, dkf, dkv, got = _attn_bwd(s["qf"], s["kf"], s["kv"], s["attn"], dattn, s["lse"],
                                       scatter=([sending[n][0] for n in names], [landed[n] for n in names],
                                                [sending[n][1] for n in names]))
        landed.update(zip(names, got))
        (dx0, dz, hb, qn, kvn, dqr, dkvr, dg_mix, dg_q, dg_kv, dg_qn, dg_kn, dcw) = _front_bwd(
            s["x"], s["z"], dx1, dqf, dkf, dkv, dconv, w, tabs)
        late = {"w_in": (_w_in_grad_shards(_wgrad(hb, dz, out_dtype=F32)).astype(BF16), layer),
                "w_uq": (_wgrad(qn, dqr, shard_cols=HEAD_PAD)[..., :uq_cols], layer),
                "w_ukv": (_wgrad(kvn, dkvr, shard_cols=HEAD_PAD), layer),
                "conv_w": (jnp.transpose(dcw[:n_taps].reshape(n_taps, N_DEV, -1), (1, 0, 2)).astype(BF16), layer)}
        gain_grads[layer] = jnp.concatenate([
            dg_mix[0], dg_q[0], dg_kv[0], dg_qn[0, :QK_NOPE], dg_qn[0, QK_NOPE:QK_HEAD], dg_kn[0, :QK_NOPE],
            dg_kn[0, QK_NOPE:QK_HEAD], dg_oa[0], dg_oc[0], dg_mlp[0], dg_ple[0]])
        dx = dx0
    names = list(late)
    landed.update(zip(names, _scatter_exchange([late[n][0] for n in names], [landed[n] for n in names],
                                               [late[n][1] for n in names])))
    grads = {n: _sum_leading(landed[n]) for n in SHARD_NAMES}

    gg = jnp.stack(gain_grads)
    gg_rows = -(-gg.size // (HALO * LANES)) * HALO
    gg_pad = jnp.pad(gg.reshape(-1), (0, gg_rows * LANES - gg.size)).reshape(gg_rows, LANES)
    gg_sum = _sum_leading(_all_gather([gg_pad])[0]).reshape(-1)[:gg.size].reshape(gg.shape)
    off = 0
    for n in GAIN_NAMES:
        width = gains[n].shape[1]
        grads[n] = gg_sum[:, off:off + width]
        off += width

    deltas, new_m, new_v = {}, {}, {}
    for n in WEIGHT_NAMES:
        deltas[n], new_m[n], new_v[n] = _adamw(weights[n], grads[n], given["m_" + n], given["v_" + n])
    return (loss, dx[None], *[grads[n] for n in WEIGHT_NAMES], *[deltas[n] for n in WEIGHT_NAMES],
            *[new_m[n] for n in WEIGHT_NAMES], *[new_v[n] for n in WEIGHT_NAMES])
```

```python
import functools

import jax
import jax.numpy as jnp
from jax import lax
from jax.experimental import pallas as pl
from jax.experimental.pallas import tpu as pltpu

F32 = jnp.float32
BF16 = jnp.bfloat16
MESH = pl.DeviceIdType.MESH

N_HEADS = 8
QK_NOPE = 64
QK_ROPE = 32
QK_HEAD = QK_NOPE + QK_ROPE
V_HEAD = 64
HEAD_PAD = 128
PAIR = 2
ATTN_SCALE = QK_HEAD ** -0.5
Q_LORA = 384
KV_LORA = 256
CONV_WIDTH = 512
ATTN_WIDTH = N_HEADS * V_HEAD
ROPE_THETA = 10000.0
EPS = 1e-6
ADAM_LR, ADAM_B1, ADAM_B2, ADAM_EPS, ADAM_WD, ADAM_STEP = 0.001, 0.9, 0.999, 1e-08, 0.01, 10

Z_Q = (0, 384)
Z_KV = (384, 640)
Z_GB = (640, 1152)
Z_GC = (1152, 1664)
Z_XIN = (1664, 2176)
Z_KPE = (2176, 2304)
Z_COLS = 2304

N_DEV = 8
LANES = 128
V7X_VMEM_LIMIT = 52 * 1024 * 1024
TOKEN_TILE = 256
LIGHT_TOKEN_TILE = 512
ATTN_BLOCK = 256
ATTN_FWD_BLOCK = 512
ATTN_FWD_ROWS = 1024
ATTN_Q_SUB = 2
ATTN_KV_SUB = 16
ROW_BLOCK = 512
WGRAD_TOKENS = 2048
WGRAD_TILE = 1024
HALO = 8

GAIN_NAMES = ("g_mix", "g_q_lat", "g_kv_lat", "g_qn_nope", "g_qn_rope", "g_kn_nope", "g_kn_rope",
              "g_out_attn", "g_out_conv", "g_mlp", "g_ple")
SHARD_NAMES = ("w_in", "w_uq", "w_ukv", "conv_w", "w_o", "w_up", "w_down", "w_ple_gate", "w_ple")
WEIGHT_NAMES = ("g_mix", "w_in", "g_q_lat", "w_uq", "g_kv_lat", "w_ukv", "g_qn_nope", "g_qn_rope", "g_kn_nope",
                "g_kn_rope", "conv_w", "g_out_attn", "g_out_conv", "w_o", "g_mlp", "w_up", "w_down", "g_ple",
                "w_ple_gate", "w_ple")


def _cparams(semantics=None):
    return pltpu.CompilerParams(dimension_semantics=semantics, vmem_limit_bytes=V7X_VMEM_LIMIT)


def _mm(a, b):
    return jnp.dot(a, b, preferred_element_type=F32)


def _mm_nt(a, b):
    return lax.dot_general(a, b, (((1,), (1,)), ((), ())), preferred_element_type=F32)


def _mm_tn(a, b):
    return lax.dot_general(a, b, (((0,), (0,)), ((), ())), preferred_element_type=F32)


def _rms(x, g):
    r = lax.rsqrt(jnp.mean(x * x, axis=-1, keepdims=True) + EPS)
    return (x * r) * g


def _rms_bwd(x, g, dy):
    r = lax.rsqrt(jnp.mean(x * x, axis=-1, keepdims=True) + EPS)
    xh = x * r
    dg = jnp.sum(dy * xh, axis=0, keepdims=True)
    dyg = dy * g
    dx = r * (dyg - xh * jnp.mean(dyg * xh, axis=-1, keepdims=True))
    return dx, dg


def _group_mean(t, gm):
    hi = t.astype(BF16)
    lo = (t - hi.astype(F32)).astype(BF16)
    return _mm(hi, gm) + _mm(lo, gm)


def _swap_rope_halves(x, lane):
    half = QK_ROPE // 2
    swapped = jnp.where(lane < QK_NOPE + half, pltpu.roll(x, x.shape[1] - half, 1), pltpu.roll(x, half, 1))
    return jnp.where((lane >= QK_NOPE) & (lane < QK_HEAD), swapped, 0.0)


def _qk_fwd(x, g, cos, sin, gm, lane):
    r = lax.rsqrt(_group_mean(x * x, gm) + EPS)
    n = (x * r) * g
    return n * cos + _swap_rope_halves(n, lane) * sin


def _qk_bwd(x, g, dy, cos, sin, gm, lane):
    r = lax.rsqrt(_group_mean(x * x, gm) + EPS)
    xh = x * r
    dn = dy * cos + _swap_rope_halves(dy * sin, lane)
    dg = jnp.sum(dn * xh, axis=0, keepdims=True)
    dng = dn * g
    dx = r * (dng - xh * _group_mean(dng * xh, gm))
    return dx, dg


def _row_shards_joined(ref):
    n, rows, cols = ref.shape
    return ref[...].reshape(n * rows, cols)


def _rows(tm, n):
    return pl.BlockSpec((tm, n), lambda i: (i, 0))


def _whole(shape):
    zeros = (0,) * len(shape)
    return pl.BlockSpec(shape, lambda i: zeros)


def _operands(arrays):
    return list(arrays), [_whole(a.shape) for a in arrays]


def _accumulate(ref, first, value):
    @pl.when(first)
    def _():
        ref[...] = jnp.zeros_like(ref)
    ref[...] += value


def _front_fwd(x, w, tabs):
    t_len, d = x.shape
    tm = min(TOKEN_TILE, t_len)
    hp = N_HEADS * HEAD_PAD

    def body(x_ref, gmix, win, gq, wuq, gkv, wukv, gqn, gkn, cw_ref, gm_ref, cos_ref, sin_ref,
             z_ref, qf_ref, kf_ref, kv_ref, conv_ref, ubuf):
        i = pl.program_id(0)
        h = _rms(x_ref[...], gmix[...])
        z = _mm(h.astype(BF16), win[...])
        z_ref[...] = z
        qnb = _rms(z[:, Z_Q[0]:Z_Q[1]], gq[...]).astype(BF16)
        kvb = _rms(z[:, Z_KV[0]:Z_KV[1]], gkv[...]).astype(BF16)
        kpe = z[:, Z_KPE[0]:Z_KPE[1]]
        kpe = jnp.concatenate([kpe] * PAIR, axis=1)
        cos, sin = (jnp.concatenate([t[...]] * PAIR, axis=1) for t in (cos_ref, sin_ref))
        gm = gm_ref[...]
        lane = lax.broadcasted_iota(jnp.int32, (tm, PAIR * HEAD_PAD), 1) & (HEAD_PAD - 1)
        for pr in range(N_HEADS // PAIR):
            sl = slice(pr * PAIR * HEAD_PAD, (pr + 1) * PAIR * HEAD_PAD)
            qf_ref[:, sl] = (_qk_fwd(_mm(qnb, wuq[pr]), gqn[...], cos, sin, gm, lane) * ATTN_SCALE).astype(BF16)
            kv = _mm(kvb, wukv[pr])
            kv_ref[:, sl] = jnp.where(lane < QK_NOPE, jnp.where(lane == 0, 1.0, 0.0), kv).astype(BF16)
            kf_ref[:, sl] = _qk_fwd(jnp.where(lane < QK_NOPE, kv, 0.0) + kpe, gkn[...], cos, sin, gm, lane).astype(BF16)
        u = z[:, Z_GC[0]:Z_GC[1]] * z[:, Z_XIN[0]:Z_XIN[1]]

        @pl.when(i == 0)
        def _():
            ubuf[0:HALO, :] = jnp.zeros((HALO, CONV_WIDTH), F32)
        ubuf[HALO:HALO + tm, :] = u
        cw = cw_ref[...]
        y = cw[0:1] * u + cw[1:2] * ubuf[pl.ds(HALO - 1, tm), :] + cw[2:3] * ubuf[pl.ds(HALO - 2, tm), :]
        conv_ref[...] = z[:, Z_GB[0]:Z_GB[1]] * y
        ubuf[0:HALO, :] = u[tm - HALO:tm, :]

    consts, const_specs = _operands([w["g_mix"], w["w_in"], w["g_q_lat"], w["w_uq"], w["g_kv_lat"], w["w_ukv"],
                                     w["g_qn"], w["g_kn"], w["conv_w"], tabs["gm"]])
    out_shape = (jax.ShapeDtypeStruct((t_len, Z_COLS), F32), jax.ShapeDtypeStruct((t_len, hp), BF16),
                 jax.ShapeDtypeStruct((t_len, hp), BF16), jax.ShapeDtypeStruct((t_len, hp), BF16),
                 jax.ShapeDtypeStruct((t_len, CONV_WIDTH), F32))
    return pl.pallas_call(body, name="front_fwd", grid=(t_len // tm,),
                          in_specs=[_rows(tm, d)] + const_specs + [_rows(tm, HEAD_PAD)] * 2,
                          out_specs=tuple(_rows(tm, s.shape[1]) for s in out_shape), out_shape=out_shape,
                          scratch_shapes=[pltpu.VMEM((tm + HALO, CONV_WIDTH), F32)],
                          compiler_params=_cparams(("arbitrary",)))(x, *consts, tabs["cos"], tabs["sin"])


def _attn_fwd(qf, kf, kv, gather=None):
    t_len = qf.shape[0]
    blk = min(ATTN_FWD_BLOCK, t_len)
    bq = blk // ATTN_Q_SUB
    span = min(ATTN_FWD_ROWS, t_len)
    nb = t_len // span
    n_sub = span // bq
    chains = [(hh, a) for hh in range(2) for a in range(n_sub)]

    def body(q_ref, k_ref, kv_ref, o_ref, lse_ref):
        lane = lax.broadcasted_iota(jnp.int32, (bq, LANES), 1)
        row = lax.broadcasted_iota(jnp.int32, (bq, blk), 0)
        col = lax.broadcasted_iota(jnp.int32, (bq, blk), 1)

        def head_cols(hh):
            return slice(hh * HEAD_PAD, (hh + 1) * HEAD_PAD)

        def softmax_step(s, kvv, state, first_row=None):
            m, acc = state
            if first_row is not None:
                s = jnp.where(col <= row + first_row, s, -jnp.inf)
            m_new = jnp.maximum(m, jnp.max(s, axis=-1, keepdims=True))
            p = jnp.exp(s - m_new)
            acc = jnp.exp(m - m_new) * acc + _mm(p.astype(BF16), kvv)
            return m_new, acc

        def finish(state):
            m, acc = state
            l = jnp.sum(jnp.where(lane == 0, acc, 0.0), axis=-1, keepdims=True)
            return acc / l, jnp.broadcast_to(m + jnp.log(l), (bq, LANES))

        def qblock(i):
            rows = [pl.ds(a * bq, bq) for a in range(n_sub)]
            first = [i * span + a * bq for a in range(n_sub)]
            states = [(jnp.full((bq, 1), -jnp.inf, F32), jnp.zeros((bq, LANES), F32))] * len(chains)
            steps = []
            for j in range((first[-1] + bq - 1) // blk + 1):
                for n, (hh, a) in enumerate(chains):
                    if j * blk <= first[a] + bq - 1:
                        diagonal = (j + 1) * blk - 1 > first[a]
                        steps.append((n, j, first[a] - j * blk if diagonal else None))
            ahead = len(chains)
            scores = []
            for t in range(len(steps) + ahead):
                if t < len(steps):
                    n, j, _ = steps[t]
                    hh, a = chains[n]
                    scores.append(_mm_nt(q_ref[rows[a], head_cols(hh)], k_ref[pl.ds(j * blk, blk), head_cols(hh)]))
                if t >= ahead:
                    n, j, offset = steps[t - ahead]
                    states[n] = softmax_step(scores[t - ahead], kv_ref[pl.ds(j * blk, blk), head_cols(chains[n][0])],
                                             states[n], offset)
            for a in range(n_sub):
                (o0, lse0), (o1, lse1) = finish(states[chains.index((0, a))]), finish(states[chains.index((1, a))])
                o_ref[rows[a], :] = jnp.where(lane < V_HEAD, pltpu.roll(o0, V_HEAD, 1), o1)
                lse_ref[rows[a], head_cols(0)] = lse0
                lse_ref[rows[a], head_cols(1)] = lse1

        mine = pl.program_id(0) % nb
        for i in range(nb):
            pl.when(mine == i)(functools.partial(qblock, i))

    q_spec = pl.BlockSpec((span, 2 * HEAD_PAD), lambda g: (g % nb, g // nb))
    keys = pl.BlockSpec((t_len, 2 * HEAD_PAD), lambda g: (0, g // nb))
    o_spec = pl.BlockSpec((span, 2 * V_HEAD), lambda g: (g % nb, g // nb))
    out_shape = (jax.ShapeDtypeStruct((t_len, ATTN_WIDTH), F32), jax.ShapeDtypeStruct((t_len, N_HEADS * LANES), F32))
    (attn, lse), gathered = _call_hosting_gather("attn_fwd", body, N_HEADS // 2 * nb, [q_spec, keys, keys],
                                                 (o_spec, q_spec), out_shape, [], (qf, kf, kv), gather)
    return attn, lse, gathered


def _call_hosting_gather(name, body, n_steps, in_specs, out_specs, out_shape, scratch_shapes, args, gather):
    if not gather:
        return pl.pallas_call(body, name=name, grid=(n_steps,), in_specs=list(in_specs), out_specs=tuple(out_specs),
                              out_shape=tuple(out_shape), scratch_shapes=list(scratch_shapes),
                              compiler_params=_cparams(("arbitrary",)))(*args), ()
    n_in, n_out, n_g = len(in_specs), len(out_shape), len(gather)

    def hosting_body(*refs):
        ins, refs = refs[:n_in], refs[n_in:]
        x_refs = [r.at[layer] for r, (_, layer) in zip(refs[:n_g], gather)]
        outs, landing, sems, scratch = (refs[n_g:n_g + n_out], refs[n_g + n_out:2 * n_g + n_out],
                                        refs[2 * n_g + n_out:2 * n_g + n_out + 3], refs[2 * n_g + n_out + 3:])
        start, forward, finish = _gather_phases(x_refs, landing, *sems)
        step = pl.program_id(0)
        pl.when(step == 0)(start)
        pl.when(step == n_steps - 1)(forward)
        body(*ins, *outs, *scratch)
        pl.when(step == n_steps - 1)(finish)

    res = pl.pallas_call(hosting_body, name=name + "_gather", grid=(n_steps,), in_specs=list(in_specs) + _any_specs(n_g),
                         out_specs=tuple(out_specs) + tuple(_any_specs(n_g)),
                         out_shape=tuple(out_shape) + _gather_out_shape([s[layer] for s, layer in gather]),
                         scratch_shapes=_gather_semaphores(n_g) + list(scratch_shapes),
                         compiler_params=_cparams(("arbitrary",)))(*args, *[s for s, _ in gather])
    return res[:n_out], res[n_out:]


def _mix_out_fwd(x, attn, conv, w):
    t_len, d = x.shape
    tm = min(LIGHT_TOKEN_TILE, t_len)

    def body(x_ref, a_ref, c_ref, goa, goc, wo, x1_ref):
        mixed = jnp.concatenate([_rms(a_ref[...], goa[...]), _rms(c_ref[...], goc[...])], axis=1)
        x1_ref[...] = x_ref[...] + _mm(mixed.astype(BF16), _row_shards_joined(wo))

    consts, const_specs = _operands([w["g_out_attn"], w["g_out_conv"], w["w_o"]])
    return pl.pallas_call(body, name="mix_out_fwd", grid=(t_len // tm,),
                          in_specs=[_rows(tm, d), _rows(tm, ATTN_WIDTH), _rows(tm, CONV_WIDTH)] + const_specs,
                          out_specs=_rows(tm, d), out_shape=jax.ShapeDtypeStruct((t_len, d), F32),
                          compiler_params=_cparams(("parallel",)))(x, attn, conv, *consts)


def _mlp_fwd(x1, w, gather=None):
    t_len, d = x1.shape
    tm = min(TOKEN_TILE, t_len)

    def body(x_ref, g, wup, wdn, x2_ref):
        x1v = x_ref[...]
        hb = _rms(x1v, g[...]).astype(BF16)
        acc = x1v
        for k in range(N_DEV):
            a = jnp.maximum(_mm(hb, wup[k]), 0.0)
            acc = acc + _mm((a * a).astype(BF16), wdn[k])
        x2_ref[...] = acc

    consts, const_specs = _operands([w["g_mlp"], w["w_up"], w["w_down"]])
    (x2,), gathered = _call_hosting_gather("mlp_fwd", body, t_len // tm, [_rows(tm, d)] + const_specs, (_rows(tm, d),),
                                           (jax.ShapeDtypeStruct((t_len, d), F32),), [], (x1, *consts), gather)
    return x2, gathered


def _ple_fwd(x2, p, w):
    t_len, d = x2.shape
    tm = min(LIGHT_TOKEN_TILE, t_len)

    def body(x_ref, p_ref, g, wg, wp, x3_ref):
        x2v = x_ref[...]
        gate = jax.nn.sigmoid(_mm(_rms(x2v, g[...]).astype(BF16), _row_shards_joined(wg)))
        pb = p_ref[...].astype(BF16)
        e = jnp.concatenate([_mm(pb, wp[k]) for k in range(N_DEV)], axis=1)
        x3_ref[...] = x2v + gate * e

    consts, const_specs = _operands([w["g_ple"], w["w_ple_gate"], w["w_ple"]])
    return pl.pallas_call(body, name="ple_fwd", grid=(t_len // tm,),
                          in_specs=[_rows(tm, d), _rows(tm, p.shape[1])] + const_specs, out_specs=_rows(tm, d),
                          out_shape=jax.ShapeDtypeStruct((t_len, d), F32),
                          compiler_params=_cparams(("parallel",)))(x2, p, *consts)


def _loss_and_grad(y, target):
    t_len, d = y.shape
    tm = min(TOKEN_TILE, t_len)

    def body(y_ref, t_ref, sq_ref, dy_ref):
        err = y_ref[...] - t_ref[...]
        dy_ref[...] = err / d
        total = jnp.sum(jnp.sum(err * err, axis=0, keepdims=True), axis=1, keepdims=True)
        _accumulate(sq_ref, pl.program_id(0) == 0, jnp.broadcast_to(total, (HALO, LANES)))

    return pl.pallas_call(body, name="loss_grad", grid=(t_len // tm,), in_specs=[_rows(tm, d), _rows(tm, d)],
                          out_specs=(_whole((HALO, LANES)), _rows(tm, d)),
                          out_shape=(jax.ShapeDtypeStruct((HALO, LANES), F32), jax.ShapeDtypeStruct((t_len, d), F32)),
                          compiler_params=_cparams(("arbitrary",)))(y, target)


def _ple_bwd(dx3, x2, p, w):
    t_len, d = x2.shape
    tm = min(LIGHT_TOKEN_TILE, t_len)

    def body(dx3_ref, x_ref, p_ref, g, wg, wp, dx2_ref, de_ref, h3_ref, dpre_ref, dg_ref):
        x2v, dx3v = x_ref[...], dx3_ref[...]
        hb = _rms(x2v, g[...]).astype(BF16)
        h3_ref[...] = hb
        w_gate = _row_shards_joined(wg)
        gate = jax.nn.sigmoid(_mm(hb, w_gate))
        pb = p_ref[...].astype(BF16)
        e = jnp.concatenate([_mm(pb, wp[k]) for k in range(N_DEV)], axis=1)
        de_ref[...] = (dx3v * gate).astype(BF16)
        dpre = ((dx3v * e) * gate * (1.0 - gate)).astype(BF16)
        dpre_ref[...] = dpre
        dx, dg = _rms_bwd(x2v, g[...], _mm_nt(dpre, w_gate))
        dx2_ref[...] = dx3v + dx
        _accumulate(dg_ref, pl.program_id(0) == 0, dg)

    consts, const_specs = _operands([w["g_ple"], w["w_ple_gate"], w["w_ple"]])
    out_shape = (jax.ShapeDtypeStruct((t_len, d), F32), jax.ShapeDtypeStruct((t_len, d), BF16),
                 jax.ShapeDtypeStruct((t_len, d), BF16), jax.ShapeDtypeStruct((t_len, d), BF16),
                 jax.ShapeDtypeStruct((1, d), F32))
    return pl.pallas_call(body, name="ple_bwd", grid=(t_len // tm,),
                          in_specs=[_rows(tm, d), _rows(tm, d), _rows(tm, p.shape[1])] + const_specs,
                          out_specs=(_rows(tm, d),) * 4 + (_whole((1, d)),), out_shape=out_shape,
                          compiler_params=_cparams(("arbitrary",)))(dx3, x2, p, *consts)


def _mlp_bwd(dx2, x1, w):
    t_len, d = x1.shape
    tm = min(TOKEN_TILE, t_len)
    fc = w["w_up"].shape[2]
    ff = N_DEV * fc

    def body(dx2_ref, x_ref, g, wup, wdn, dx1_ref, r_ref, da_ref, h2_ref, dg_ref):
        x1v, dx2v = x_ref[...], dx2_ref[...]
        hb = _rms(x1v, g[...]).astype(BF16)
        h2_ref[...] = hb
        dxb = dx2v.astype(BF16)
        dh = jnp.zeros((tm, d), F32)
        for k in range(N_DEV):
            a = jnp.maximum(_mm(hb, wup[k]), 0.0)
            r_ref[:, k * fc:(k + 1) * fc] = (a * a).astype(BF16)
            da = (_mm_nt(dxb, wdn[k]) * (2.0 * a)).astype(BF16)
            da_ref[:, k * fc:(k + 1) * fc] = da
            dh = dh + _mm_nt(da, wup[k])
        dx, dg = _rms_bwd(x1v, g[...], dh)
        dx1_ref[...] = dx2v + dx
        _accumulate(dg_ref, pl.program_id(0) == 0, dg)

    consts, const_specs = _operands([w["g_mlp"], w["w_up"], w["w_down"]])
    out_shape = (jax.ShapeDtypeStruct((t_len, d), F32), jax.ShapeDtypeStruct((t_len, ff), BF16),
                 jax.ShapeDtypeStruct((t_len, ff), BF16), jax.ShapeDtypeStruct((t_len, d), BF16),
                 jax.ShapeDtypeStruct((1, d), F32))
    return pl.pallas_call(body, name="mlp_bwd", grid=(t_len // tm,), in_specs=[_rows(tm, d), _rows(tm, d)] + const_specs,
                          out_specs=(_rows(tm, d), _rows(tm, ff), _rows(tm, ff), _rows(tm, d), _whole((1, d))),
                          out_shape=out_shape, compiler_params=_cparams(("arbitrary",)))(dx2, x1, *consts)


def _mix_out_bwd(dx1, attn, conv, w):
    t_len, d = dx1.shape
    tm = min(LIGHT_TOKEN_TILE, t_len)

    def body(dx1_ref, a_ref, c_ref, goa, goc, wo, mixed_ref, da_ref, dc_ref, dgoa_ref, dgoc_ref):
        av, cv = a_ref[...], c_ref[...]
        mixed_ref[...] = jnp.concatenate([_rms(av, goa[...]), _rms(cv, goc[...])], axis=1).astype(BF16)
        dmixed = _mm_nt(dx1_ref[...].astype(BF16), _row_shards_joined(wo))
        da, dga = _rms_bwd(av, goa[...], dmixed[:, :ATTN_WIDTH])
        dc, dgc = _rms_bwd(cv, goc[...], dmixed[:, ATTN_WIDTH:])
        da_ref[...] = da
        dc_ref[...] = dc
        first = pl.program_id(0) == 0
        _accumulate(dgoa_ref, first, dga)
        _accumulate(dgoc_ref, first, dgc)

    consts, const_specs = _operands([w["g_out_attn"], w["g_out_conv"], w["w_o"]])
    out_shape = (jax.ShapeDtypeStruct((t_len, d), BF16), jax.ShapeDtypeStruct((t_len, ATTN_WIDTH), F32),
                 jax.ShapeDtypeStruct((t_len, CONV_WIDTH), F32), jax.ShapeDtypeStruct((1, ATTN_WIDTH), F32),
                 jax.ShapeDtypeStruct((1, CONV_WIDTH), F32))
    out_specs = (_rows(tm, d), _rows(tm, ATTN_WIDTH), _rows(tm, CONV_WIDTH), _whole((1, ATTN_WIDTH)),
                 _whole((1, CONV_WIDTH)))
    return pl.pallas_call(body, name="mix_out_bwd", grid=(t_len // tm,),
                          in_specs=[_rows(tm, d), _rows(tm, ATTN_WIDTH), _rows(tm, CONV_WIDTH)] + const_specs,
                          out_specs=out_specs, out_shape=out_shape,
                          compiler_params=_cparams(("arbitrary",)))(dx1, attn, conv, *consts)


def _attn_bwd(qf, kf, kv, o, do, lse, scatter):
    t_len = qf.shape[0]
    blk = min(ATTN_BLOCK, t_len)
    nb = t_len // blk
    n_sub = min(ATTN_KV_SUB, nb)
    reps = blk // LANES

    def body(q_ref, k_ref, kv_ref, o_ref, do_ref, lse_ref, dq_ref, dk_ref, dkv_ref, delta_ref, dob_ref):
        hd = pl.program_id(0)
        lane = lax.broadcasted_iota(jnp.int32, (blk, LANES), 1)
        even = (lane * 0 + hd % 2) == 0
        mine = jnp.where(lane < V_HEAD, 0, 1) == hd % 2
        row = lax.broadcasted_iota(jnp.int32, (blk, blk), 0)
        col = lax.broadcasted_iota(jnp.int32, (blk, blk), 1)
        dq_ref[...] = jnp.zeros_like(dq_ref)

        def prepare(i, carry):
            qs = pl.ds(pl.multiple_of(i * blk, blk), blk)
            dov = do_ref[qs, :]
            prod = jnp.where(mine, dov * o_ref[qs, :], 0.0)
            delta_ref[qs, :] = jnp.broadcast_to(jnp.sum(prod, axis=-1, keepdims=True), (blk, LANES))
            moved = jnp.where(even, pltpu.roll(dov, V_HEAD, 1), dov)
            dob_ref[qs, :] = jnp.where(lane >= V_HEAD, moved, 0.0).astype(BF16)
            return carry
        lax.fori_loop(0, nb, prepare, 0)

        def kvblock(jj, carry):
            base = jj * n_sub
            kss = [pl.ds(pl.multiple_of((base + a) * blk, blk), blk) for a in range(n_sub)]
            k = [k_ref[ks, :] for ks in kss]
            kvv = [kv_ref[ks, :] for ks in kss]

            def products(i):
                qs = pl.ds(pl.multiple_of(i * blk, blk), blk)
                q, dob = q_ref[qs, :], dob_ref[qs, :]
                return tuple((_mm_nt(q, k[a]), _mm_nt(dob, kvv[a])) for a in range(n_sub))

            def qstep(i, raw, accs, kinds):
                qs = pl.ds(pl.multiple_of(i * blk, blk), blk)
                q = q_ref[qs, :]
                dob = dob_ref[qs, :]
                lse_t = jnp.concatenate([lse_ref[qs, :]] * reps, axis=1)
                delta_t = jnp.concatenate([delta_ref[qs, :]] * reps, axis=1)
                new, dq_add = [], None
                for a in range(n_sub):
                    if kinds[a] is None:
                        new.append(accs[a])
                        continue
                    dk_acc, dv_acc = accs[a]
                    s, dp = raw[a]
                    if kinds[a]:
                        s = jnp.where(col <= row, s, -jnp.inf)
                    p = jnp.exp(s - lse_t)
                    ds = (p * (dp - delta_t)).astype(BF16)
                    new.append((dk_acc + _mm_tn(ds, q), dv_acc + _mm_tn(p.astype(BF16), dob)))
                    part = _mm(ds, k[a])
                    dq_add = part if dq_add is None else dq_add + part
                dq_ref[qs, :] += dq_add
                return tuple(new)

            zero = jnp.zeros((blk, LANES), F32)
            accs = ((zero, zero),) * n_sub
            for b in range(n_sub):
                accs = qstep(base + b, products(base + b), accs, tuple((a == b) if a <= b else None for a in range(n_sub)))

            def pipelined(i, carried):
                raw, acc = carried
                return products(jnp.minimum(i + 1, nb - 1)), qstep(i, raw, acc, (False,) * n_sub)

            first = base + n_sub
            _, accs = lax.fori_loop(first, nb, pipelined, (products(jnp.minimum(first, nb - 1)), accs))
            for a in range(n_sub):
                dk_ref[kss[a], :] = accs[a][0]
                dkv_ref[kss[a], :] = accs[a][1]
            return carry
        lax.fori_loop(0, nb // n_sub, kvblock, 0)

    n_sc = len(scatter[0])

    def hosting_body(*refs):
        ins, rest = refs[:6], refs[6 + 2 * n_sc:]
        parts = refs[6:6 + n_sc]
        outs, landed, sems, scratch = rest[:3], rest[3:3 + n_sc], rest[3 + n_sc:6 + n_sc], rest[6 + n_sc:]
        start, finish = _scatter_phases(parts, landed, *sems, scatter[2])
        hd = pl.program_id(0)
        pl.when(hd == 0)(start)
        body(*ins, *outs, *scratch)
        pl.when(hd == N_HEADS - 1)(finish)

    head = pl.BlockSpec((t_len, HEAD_PAD), lambda h: (0, h))
    pair = pl.BlockSpec((t_len, 2 * V_HEAD), lambda h: (0, h // 2))
    out = jax.ShapeDtypeStruct((t_len, N_HEADS * HEAD_PAD), F32)
    vmem_scratch = [pltpu.VMEM((t_len, LANES), F32), pltpu.VMEM((t_len, LANES), BF16)]
    res = pl.pallas_call(hosting_body, name="attn_bwd_scatter", grid=(N_HEADS,),
                         in_specs=[head, head, head, pair, pair, head] + _any_specs(2 * n_sc),
                         out_specs=(head, head, head) + tuple(_any_specs(n_sc)),
                         out_shape=(out, out, out) + _same_shapes(scatter[1]),
                         scratch_shapes=_scatter_semaphores(n_sc) + vmem_scratch,
                         input_output_aliases={6 + n_sc + a: 3 + a for a in range(n_sc)},
                         compiler_params=_cparams(("arbitrary",)))(qf, kf, kv, o, do, lse, *scatter[0], *scatter[1])
    return res[0], res[1], res[2], res[3:]


def _front_bwd(x, z, dx1, dqf, dkf, dkv_in, dconv, w, tabs):
    t_len, d = x.shape
    tm = min(TOKEN_TILE, t_len)
    nt = t_len // tm
    hb_per_tile = tm // HALO
    n_halo = t_len // HALO
    hp = N_HEADS * HEAD_PAD

    def body(x_ref, z_ref, zp_ref, zn_ref, dx1_ref, dqf_ref, dkf_ref, dkv_ref, dc_ref, dcn_ref,
             gmix, win, gq, wuq, gkv, wukv, gqn, gkn, cw_ref, gm_ref, cos_ref, sin_ref,
             dx_ref, dz_ref, h_ref, qn_ref, kvn_ref, dqr_ref, dkvr_ref,
             dgmix_ref, dgq_ref, dgkv_ref, dgqn_ref, dgkn_ref, dcw_ref, ubuf, dybuf):
        i = pl.program_id(0)
        first = i == 0
        xv, zv = x_ref[...], z_ref[...]
        hb = _rms(xv, gmix[...]).astype(BF16)
        h_ref[...] = hb
        zq, zkv = zv[:, Z_Q[0]:Z_Q[1]], zv[:, Z_KV[0]:Z_KV[1]]
        qnb = _rms(zq, gq[...]).astype(BF16)
        qn_ref[...] = qnb
        kvb = _rms(zkv, gkv[...]).astype(BF16)
        kvn_ref[...] = kvb
        kpe = zv[:, Z_KPE[0]:Z_KPE[1]]
        kpe = jnp.concatenate([kpe] * PAIR, axis=1)
        cos, sin = (jnp.concatenate([t[...]] * PAIR, axis=1) for t in (cos_ref, sin_ref))
        gm = gm_ref[...]
        width = PAIR * HEAD_PAD
        lane = lax.broadcasted_iota(jnp.int32, (tm, width), 1) & (HEAD_PAD - 1)
        is_nope = lane < QK_NOPE
        is_rope = (lane >= QK_NOPE) & (lane < QK_HEAD)
        dkpe = jnp.zeros((tm, width), F32)
        dgqn = jnp.zeros((1, width), F32)
        dgkn = jnp.zeros((1, width), F32)
        dqn = jnp.zeros((tm, Q_LORA), F32)
        dkvn = jnp.zeros((tm, KV_LORA), F32)
        for pr in range(N_HEADS // PAIR):
            sl = slice(pr * width, (pr + 1) * width)
            dxq, dg = _qk_bwd(_mm(qnb, wuq[pr]), gqn[...], dqf_ref[:, sl] * ATTN_SCALE, cos, sin, gm, lane)
            dxq = dxq.astype(BF16)
            dqr_ref[:, sl] = dxq
            dqn = dqn + _mm_nt(dxq, wuq[pr])
            dgqn = dgqn + dg
            k_raw = jnp.where(is_nope, _mm(kvb, wukv[pr]), 0.0) + kpe
            dxk, dg = _qk_bwd(k_raw, gkn[...], dkf_ref[:, sl], cos, sin, gm, lane)
            dkv = jnp.where(is_nope, dxk, dkv_ref[:, sl]).astype(BF16)
            dkvr_ref[:, sl] = dkv
            dkvn = dkvn + _mm_nt(dkv, wukv[pr])
            dkpe = dkpe + jnp.where(is_rope, dxk, 0.0)
            dgkn = dgkn + dg
        dkpe = dkpe[:, :HEAD_PAD] + dkpe[:, HEAD_PAD:]
        _accumulate(dgqn_ref, first, dgqn[:, :HEAD_PAD] + dgqn[:, HEAD_PAD:])
        _accumulate(dgkn_ref, first, dgkn[:, :HEAD_PAD] + dgkn[:, HEAD_PAD:])
        dzq, dg = _rms_bwd(zq, gq[...], dqn)
        _accumulate(dgq_ref, first, dg)
        dzkv, dg = _rms_bwd(zkv, gkv[...], dkvn)
        _accumulate(dgkv_ref, first, dg)

        gb, gc, xin = zv[:, Z_GB[0]:Z_GB[1]], zv[:, Z_GC[0]:Z_GC[1]], zv[:, Z_XIN[0]:Z_XIN[1]]
        u = gc * xin
        dcv = dc_ref[...]
        dy = dcv * gb
        zp, zn = zp_ref[...], zn_ref[...]
        ubuf[0:HALO, :] = (zp[:, Z_GC[0]:Z_GC[1]] * zp[:, Z_XIN[0]:Z_XIN[1]]) * jnp.where(first, 0.0, 1.0)
        ubuf[HALO:HALO + tm, :] = u
        dybuf[0:tm, :] = dy
        dybuf[tm:tm + HALO, :] = (dcn_ref[...] * zn[:, Z_GB[0]:Z_GB[1]]) * jnp.where(i == nt - 1, 0.0, 1.0)
        cw = cw_ref[...]
        u1, u2 = ubuf[pl.ds(HALO - 1, tm), :], ubuf[pl.ds(HALO - 2, tm), :]
        y = cw[0:1] * u + cw[1:2] * u1 + cw[2:3] * u2
        du = cw[0:1] * dy + cw[1:2] * dybuf[pl.ds(1, tm), :] + cw[2:3] * dybuf[pl.ds(2, tm), :]
        dcw = jnp.concatenate([jnp.sum(dy * u, axis=0, keepdims=True), jnp.sum(dy * u1, axis=0, keepdims=True),
                               jnp.sum(dy * u2, axis=0, keepdims=True), jnp.zeros((HALO - 3, CONV_WIDTH), F32)], axis=0)
        _accumulate(dcw_ref, first, dcw)

        dz_ref[:, Z_Q[0]:Z_Q[1]] = dzq.astype(BF16)
        dz_ref[:, Z_KV[0]:Z_KV[1]] = dzkv.astype(BF16)
        dz_ref[:, Z_GB[0]:Z_GB[1]] = (dcv * y).astype(BF16)
        dz_ref[:, Z_GC[0]:Z_GC[1]] = (du * xin).astype(BF16)
        dz_ref[:, Z_XIN[0]:Z_XIN[1]] = (du * gc).astype(BF16)
        dz_ref[:, Z_KPE[0]:Z_KPE[1]] = dkpe.astype(BF16)
        dx, dg = _rms_bwd(xv, gmix[...], _mm_nt(dz_ref[...], win[...]))
        dx_ref[...] = dx1_ref[...] + dx
        _accumulate(dgmix_ref, first, dg)

    prev_halo = lambda n: pl.BlockSpec((HALO, n), lambda i: (jnp.maximum(i * hb_per_tile - 1, 0), 0))
    next_halo = lambda n: pl.BlockSpec((HALO, n), lambda i: (jnp.minimum((i + 1) * hb_per_tile, n_halo - 1), 0))
    consts, const_specs = _operands([w["g_mix"], w["w_in"], w["g_q_lat"], w["w_uq"], w["g_kv_lat"], w["w_ukv"],
                                     w["g_qn"], w["g_kn"], w["conv_w"], tabs["gm"]])
    in_specs = ([_rows(tm, d), _rows(tm, Z_COLS), prev_halo(Z_COLS), next_halo(Z_COLS), _rows(tm, d), _rows(tm, hp),
                 _rows(tm, hp), _rows(tm, hp), _rows(tm, CONV_WIDTH), next_halo(CONV_WIDTH)]
                + const_specs + [_rows(tm, HEAD_PAD)] * 2)
    out_shape = (jax.ShapeDtypeStruct((t_len, d), F32), jax.ShapeDtypeStruct((t_len, Z_COLS), BF16),
                 jax.ShapeDtypeStruct((t_len, d), BF16), jax.ShapeDtypeStruct((t_len, Q_LORA), BF16),
                 jax.ShapeDtypeStruct((t_len, KV_LORA), BF16), jax.ShapeDtypeStruct((t_len, hp), BF16),
                 jax.ShapeDtypeStruct((t_len, hp), BF16),
                 jax.ShapeDtypeStruct((1, d), F32), jax.ShapeDtypeStruct((1, Q_LORA), F32),
                 jax.ShapeDtypeStruct((1, KV_LORA), F32), jax.ShapeDtypeStruct((1, LANES), F32),
                 jax.ShapeDtypeStruct((1, LANES), F32), jax.ShapeDtypeStruct((HALO, CONV_WIDTH), F32))
    out_specs = tuple(_rows(tm, s.shape[1]) for s in out_shape[:7]) + tuple(_whole(s.shape) for s in out_shape[7:])
    return pl.pallas_call(body, name="front_bwd", grid=(nt,), in_specs=in_specs, out_specs=out_specs, out_shape=out_shape,
                          scratch_shapes=[pltpu.VMEM((tm + HALO, CONV_WIDTH), F32), pltpu.VMEM((tm + HALO, CONV_WIDTH), F32)],
                          compiler_params=_cparams(("arbitrary",)))(
                              x, z, z, z, dx1, dqf, dkf, dkv_in, dconv, dconv, *consts, tabs["cos"], tabs["sin"])


def _wgrad(a, b, shard_cols=None, shard_rows=None, out_dtype=BF16):
    t_len, kk = a.shape
    nn = b.shape[1]
    tk = min(kk, WGRAD_TILE)
    tn = next(c for c in range(min(nn, WGRAD_TILE), 0, -LANES) if nn % c == 0 and c % (shard_cols or LANES) == 0)
    tt = min(t_len, WGRAD_TOKENS)
    nt = t_len // tt
    per_block = tn // shard_cols if shard_cols else tk // shard_rows if shard_rows else 1

    def body(a_ref, b_ref, o_ref, acc):
        t = pl.program_id(2)

        @pl.when(t == 0)
        def _():
            acc[...] = jnp.zeros_like(acc)
        acc[...] += _mm_tn(a_ref[...].astype(BF16), b_ref[...].astype(BF16))

        @pl.when(t == nt - 1)
        def _():
            if shard_cols:
                for s in range(per_block):
                    o_ref[s] = acc[:, s * shard_cols:(s + 1) * shard_cols].astype(out_dtype)
            elif shard_rows:
                for s in range(per_block):
                    o_ref[s] = acc[s * shard_rows:(s + 1) * shard_rows, :].astype(out_dtype)
            else:
                o_ref[...] = acc[...].astype(out_dtype)

    if shard_cols:
        out_shape = jax.ShapeDtypeStruct((nn // shard_cols, kk, shard_cols), out_dtype)
        out_spec = pl.BlockSpec((per_block, tk, shard_cols), lambda i, j, t: (j, i, 0))
    elif shard_rows:
        out_shape = jax.ShapeDtypeStruct((kk // shard_rows, shard_rows, nn), out_dtype)
        out_spec = pl.BlockSpec((per_block, shard_rows, tn), lambda i, j, t: (i, 0, j))
    else:
        out_shape = jax.ShapeDtypeStruct((kk, nn), out_dtype)
        out_spec = pl.BlockSpec((tk, tn), lambda i, j, t: (i, j))
    return pl.pallas_call(body, name="wgrad", grid=(kk // tk, nn // tn, nt),
                          in_specs=[pl.BlockSpec((tt, tk), lambda i, j, t: (t, i)),
                                    pl.BlockSpec((tt, tn), lambda i, j, t: (t, j))],
                          out_specs=out_spec, out_shape=out_shape, scratch_shapes=[pltpu.VMEM((tk, tn), F32)],
                          compiler_params=_cparams(("parallel", "parallel", "arbitrary")))(a, b)


def _my_place():
    return lax.axis_index("x"), lax.axis_index("y"), lax.axis_index("c")


def _any_specs(n):
    return [pl.BlockSpec(memory_space=pl.ANY)] * n


def _all_gather(blocks):
    n = len(blocks)

    def body(*refs):
        start, forward, finish = _gather_phases(refs[:n], refs[n:2 * n], *refs[2 * n:])
        start()
        forward()
        finish()

    return pl.pallas_call(body, name="all_gather", out_shape=_gather_out_shape(blocks), in_specs=_any_specs(n),
                          out_specs=tuple(_any_specs(n)), scratch_shapes=_gather_semaphores(n))(*blocks)


def _gather_out_shape(blocks):
    return tuple(jax.ShapeDtypeStruct((N_DEV,) + b.shape, b.dtype) for b in blocks)


def _gather_semaphores(n):
    return [pltpu.SemaphoreType.DMA((n, 7)), pltpu.SemaphoreType.DMA((n, 7)), pltpu.SemaphoreType.DMA((n,))]


def _gather_phases(x_refs, out_refs, send_sems, recv_sems, local_sems):
    n = len(x_refs)
    x, y, c = _my_place()
    me, sibling = (x, y, c), (x, y, 1 - c)
    chips = [(1 - x, y), (x, 1 - y), (1 - x, 1 - y)]

    def slot(a, px, py, pc):
        return out_refs[a].at[4 * px + 2 * py + pc]

    def copy(a, k, blk, to, src=None):
        return pltpu.make_async_remote_copy(src_ref=slot(a, *blk) if src is None else src, dst_ref=slot(a, *blk),
                                            send_sem=send_sems.at[a, k], recv_sem=recv_sems.at[a, k],
                                            device_id=to, device_id_type=MESH)

    def own(a):
        return pltpu.make_async_copy(x_refs[a], slot(a, *me), local_sems.at[a])

    def first_hop(a):
        return [copy(a, 0, me, sibling, src=x_refs[a])] + [copy(a, 1 + j, me, (*chip, c), src=x_refs[a])
                                                           for j, chip in enumerate(chips)]

    def passed_on(a):
        return [copy(a, 4 + j, (*chip, c), sibling) for j, chip in enumerate(chips)]

    def start():
        for a in range(n):
            own(a).start()
        for a in range(n):
            for cp in first_hop(a):
                cp.start()

    def forward():
        for j, chip in enumerate(chips):
            for a in range(n):
                copy(a, 1 + j, (*chip, c), me).wait_recv()
                passed_on(a)[j].start()

    def finish():
        for a in range(n):
            copy(a, 0, sibling, me).wait_recv()
        for j, chip in enumerate(chips):
            for a in range(n):
                copy(a, 4 + j, (*chip, 1 - c), me).wait_recv()
        for a in range(n):
            for cp in first_hop(a) + passed_on(a):
                cp.wait_send()
            own(a).wait()

    return start, forward, finish


def _scatter_exchange(parts, landed, layers):
    n = len(parts)

    def body(*refs):
        start, finish = _scatter_phases(refs[:n], refs[2 * n:3 * n], *refs[3 * n:], layers)
        start()
        finish()

    return pl.pallas_call(body, name="scatter_exchange", out_shape=_same_shapes(landed), in_specs=_any_specs(2 * n),
                          out_specs=tuple(_any_specs(n)), scratch_shapes=_scatter_semaphores(n),
                          input_output_aliases={n + a: a for a in range(n)})(*parts, *landed)


def _same_shapes(arrays):
    return tuple(jax.ShapeDtypeStruct(a.shape, a.dtype) for a in arrays)


def _scatter_semaphores(n):
    return [pltpu.SemaphoreType.DMA((n, N_DEV - 1)), pltpu.SemaphoreType.DMA((n, N_DEV - 1)), pltpu.SemaphoreType.DMA((n,))]


def _scatter_phases(part_refs, landed_refs, send_sems, recv_sems, local_sems, layers):
    n = len(part_refs)
    x, y, c = _my_place()
    flips = [(0, 0, 1), (1, 0, 0), (0, 1, 0), (1, 1, 0), (1, 0, 1), (0, 1, 1), (1, 1, 1)]
    peers = [((1 - x) if fx else x, (1 - y) if fy else y, (1 - c) if fc else c) for fx, fy, fc in flips]
    my_k = 4 * x + 2 * y + c

    def index(peer):
        return 4 * peer[0] + 2 * peer[1] + peer[2]

    def send(a, r):
        return pltpu.make_async_remote_copy(src_ref=part_refs[a].at[index(peers[r])], dst_ref=landed_refs[a].at[my_k, layers[a]],
                                            send_sem=send_sems.at[a, r], recv_sem=recv_sems.at[a, r],
                                            device_id=peers[r], device_id_type=MESH)

    def arrival(a, r):
        return pltpu.make_async_remote_copy(src_ref=part_refs[a].at[my_k], dst_ref=landed_refs[a].at[index(peers[r]), layers[a]],
                                            send_sem=send_sems.at[a, r], recv_sem=recv_sems.at[a, r],
                                            device_id=peers[r], device_id_type=MESH)

    def own(a):
        return pltpu.make_async_copy(part_refs[a].at[my_k], landed_refs[a].at[my_k, layers[a]], local_sems.at[a])

    def start():
        for a in range(n):
            own(a).start()
        for r in range(len(peers)):
            for a in range(n):
                send(a, r).start()

    def finish():
        for r in range(len(peers)):
            for a in range(n):
                arrival(a, r).wait_recv()
        for r in range(len(peers)):
            for a in range(n):
                send(a, r).wait_send()
        for a in range(n):
            own(a).wait()

    return start, finish


def _row_block(rows):
    return ROW_BLOCK if rows % ROW_BLOCK == 0 else rows


def _sum_leading(parts):
    n_part, shape = parts.shape[0], parts.shape[1:]
    rows, cols = shape[-2:]
    rb = _row_block(rows)

    def body(p_ref, o_ref):
        acc = p_ref[0].astype(F32)
        for k in range(1, n_part):
            acc = acc + p_ref[k].astype(F32)
        o_ref[...] = acc

    if len(shape) == 3:
        grid = (shape[0], rows // rb)
        in_spec = pl.BlockSpec((n_part, None, rb, cols), lambda l, i: (0, l, i, 0))
        out_spec = pl.BlockSpec((None, rb, cols), lambda l, i: (l, i, 0))
    else:
        grid = (rows // rb,)
        in_spec, out_spec = pl.BlockSpec((n_part, rb, cols), lambda i: (0, i, 0)), _rows(rb, cols)
    return pl.pallas_call(body, name="sum_leading", grid=grid, in_specs=[in_spec], out_specs=out_spec,
                          out_shape=jax.ShapeDtypeStruct(shape, F32),
                          compiler_params=_cparams(("parallel",) * len(grid)))(parts)


def _adamw(w, g, m, v):
    shape = w.shape
    rows, cols = shape[-2:]
    rb = _row_block(rows)

    def body(w_ref, g_ref, m_ref, v_ref, d_ref, nm_ref, nv_ref):
        gv = g_ref[...]
        nm = ADAM_B1 * m_ref[...] + (1.0 - ADAM_B1) * gv
        nv = ADAM_B2 * v_ref[...] + (1.0 - ADAM_B2) * jnp.square(gv)
        m_hat = nm / (1.0 - ADAM_B1 ** ADAM_STEP)
        v_hat = nv / (1.0 - ADAM_B2 ** ADAM_STEP)
        d_ref[...] = -ADAM_LR * (m_hat / (jnp.sqrt(v_hat) + ADAM_EPS) + ADAM_WD * w_ref[...])
        nm_ref[...] = nm
        nv_ref[...] = nv

    if len(shape) == 3:
        grid, spec = (shape[0], rows // rb), pl.BlockSpec((None, rb, cols), lambda l, i: (l, i, 0))
    else:
        grid, spec = (rows // rb,), _rows(rb, cols)
    out = jax.ShapeDtypeStruct(shape, F32)
    return pl.pallas_call(body, name="adamw", grid=grid, in_specs=[spec] * 4, out_specs=(spec,) * 3,
                          out_shape=(out,) * 3, compiler_params=_cparams(("parallel",) * len(grid)))(w, g, m, v)


def _rope_tables(positions):
    t_len = positions.shape[0]
    inv_freq = 1.0 / (ROPE_THETA ** (jnp.arange(0, QK_ROPE, 2, dtype=F32) / QK_ROPE))
    ang = positions.astype(F32)[:, None] * inv_freq
    c, s = jnp.cos(ang), jnp.sin(ang)
    one, zero = jnp.ones((t_len, QK_NOPE), F32), jnp.zeros((t_len, QK_NOPE), F32)
    cos = jnp.concatenate([one, c, c, one[:, :LANES - QK_HEAD]], axis=1)
    sin = jnp.concatenate([zero, -s, s, zero[:, :LANES - QK_HEAD]], axis=1)
    idx = jnp.arange(PAIR * HEAD_PAD)
    lane, head = idx % HEAD_PAD, idx // HEAD_PAD
    grp = jnp.where(lane < QK_NOPE, 0, jnp.where(lane < QK_HEAD, 1, 2)) + 3 * head
    val = jnp.where(lane < QK_NOPE, 1.0 / QK_NOPE, jnp.where(lane < QK_HEAD, 1.0 / QK_ROPE, 0.0))
    gm = jnp.where(grp[:, None] == grp[None, :], val[None, :], 0.0).astype(BF16)
    return {"cos": cos, "sin": sin, "gm": gm}


def _head_gain(g_nope, g_rope):
    one = jnp.concatenate([g_nope, g_rope, jnp.zeros((HEAD_PAD - QK_HEAD,), F32)])
    return jnp.concatenate([one] * PAIR).reshape(1, PAIR * HEAD_PAD)


def _head_pairs(w):
    return jnp.concatenate([w[k::PAIR] for k in range(PAIR)], axis=2)


def _padded_w_in(shards):
    width = shards.shape[2]
    zeros = jnp.zeros((shards.shape[1], QK_NOPE), shards.dtype)

    def natural(start, end):
        pieces = []
        for k in range(N_DEV):
            lo, hi = max(start, k * width), min(end, (k + 1) * width)
            if lo < hi:
                pieces.append(shards[k][:, lo - k * width:hi - k * width])
        return pieces

    o2, o3 = Q_LORA + KV_LORA, Q_LORA + KV_LORA + QK_ROPE
    return jnp.concatenate(natural(0, o2) + natural(o3, N_DEV * width) + [zeros] + natural(o2, o3)
                           + [zeros[:, :LANES - QK_HEAD]], axis=1)


def _w_in_grad_shards(d_in):
    o2, o3 = Q_LORA + KV_LORA, Q_LORA + KV_LORA + QK_ROPE
    width = (Z_XIN[1] + QK_ROPE) // N_DEV
    runs = [(0, o2, 0), (o2, o3, Z_KPE[0] + QK_NOPE), (o3, N_DEV * width, o2)]
    shards = []
    for k in range(N_DEV):
        pieces = []
        for start, end, at in runs:
            lo, hi = max(start, k * width), min(end, (k + 1) * width)
            if lo < hi:
                pieces.append(d_in[:, at + lo - start:at + hi - start])
        shards.append(pieces[0] if len(pieces) == 1 else jnp.concatenate(pieces, axis=1))
    return jnp.stack(shards)


def kernel(x, p, positions, g_mix, w_in, g_q_lat, w_uq, g_kv_lat, w_ukv, g_qn_nope, g_qn_rope, g_kn_nope, g_kn_rope, conv_w, g_out_attn, g_out_conv, w_o, g_mlp, w_up, w_down, g_ple, w_ple_gate, w_ple, loss_target, m_g_mix, m_w_in, m_g_q_lat, m_w_uq, m_g_kv_lat, m_w_ukv, m_g_qn_nope, m_g_qn_rope, m_g_kn_nope, m_g_kn_rope, m_conv_w, m_g_out_attn, m_g_out_conv, m_w_o, m_g_mlp, m_w_up, m_w_down, m_g_ple, m_w_ple_gate, m_w_ple, v_g_mix, v_w_in, v_g_q_lat, v_w_uq, v_g_kv_lat, v_w_ukv, v_g_qn_nope, v_g_qn_rope, v_g_kn_nope, v_g_kn_rope, v_conv_w, v_g_out_attn, v_g_out_conv, v_w_o, v_g_mlp, v_w_up, v_w_down, v_g_ple, v_w_ple_gate, v_w_ple):
    given = dict(locals())
    weights = {n: given[n] for n in WEIGHT_NAMES}
    gains = {n: given[n] for n in GAIN_NAMES}
    depth = w_in.shape[0]
    xs, target = x[0], loss_target[0]
    d_model = xs.shape[1]
    uq_cols = w_uq.shape[2]
    n_taps = conv_w.shape[1]

    mat_names = [n for n in SHARD_NAMES if n != "conv_w"]
    local = [weights[n].astype(BF16) for n in mat_names]
    local[1] = jnp.pad(local[1], ((0, 0), (0, 0), (0, HEAD_PAD - uq_cols)))
    local = dict(zip(mat_names, local))
    front_names = ("w_in", "w_uq", "w_ukv")
    first = _all_gather([local[n][0] for n in front_names] + [conv_w])
    conv_full = jnp.transpose(first[-1], (1, 2, 0, 3)).reshape(depth, n_taps, -1)
    tabs = _rope_tables(positions[0])

    def front_weights(layer, full):
        lw = {n: gains[n][layer].reshape(1, -1) for n in GAIN_NAMES}
        lw.update({"w_in": _padded_w_in(full["w_in"]), "w_uq": _head_pairs(full["w_uq"]),
                   "w_ukv": _head_pairs(full["w_ukv"]),
                   "conv_w": jnp.pad(conv_full[layer], ((0, HALO - n_taps), (0, 0))),
                   "g_qn": _head_gain(g_qn_nope[layer], g_qn_rope[layer]),
                   "g_kn": _head_gain(g_kn_nope[layer], g_kn_rope[layer])})
        return lw

    def rest_weights(full):
        return {"w_ple": full["w_ple"], "w_up": full["w_up"], "w_down": full["w_down"],
                "w_o": full["w_o"], "w_ple_gate": full["w_ple_gate"]}

    saved, layer_w = [], []
    cur = xs
    gathered = dict(zip(front_names, first[:-1]))
    mlp_names = ("w_up", "w_down")
    for layer in range(depth):
        w = front_weights(layer, gathered)
        z, qf, kf, kv, conv = _front_fwd(cur, w, tabs)
        lots = [[n for n in mat_names if n in mlp_names], [n for n in mat_names if n not in mlp_names]]
        behind_attn, behind_mlp = (lots[1], lots[0]) if layer == 0 else lots
        if layer + 1 == depth:
            behind_attn, behind_mlp = [], []
        wanted = [(n, 0) for n in mat_names if n not in front_names] if layer == 0 else []
        wanted += [(n, layer + 1) for n in behind_attn]
        attn, lse, got = _attn_fwd(qf, kf, kv, gather=[(local[n], at) for n, at in wanted])
        coming = {}
        for (n, at), g in zip(wanted, got):
            (gathered if at == layer else coming)[n] = g
        w.update(rest_weights(gathered))
        layer_w.append(w)
        x1 = _mix_out_fwd(cur, attn, conv, w)
        x2, got = _mlp_fwd(x1, w, gather=[(local[n], layer + 1) for n in behind_mlp])
        coming.update(zip(behind_mlp, got))
        gathered = coming
        x3 = _ple_fwd(x2, p[layer, 0], w)
        saved.append(dict(x=cur, z=z, qf=qf, kf=kf, kv=kv, conv=conv, attn=attn, lse=lse, x1=x1, x2=x2))
        cur = x3

    sq, dx = _loss_and_grad(cur, target)
    loss = lax.psum(0.5 / d_model * sq[0, 0], ("x", "y", "c"))

    landed = {n: lax.empty((N_DEV, depth) + weights[n].shape[1:], BF16) for n in SHARD_NAMES}
    gain_grads = [None] * depth
    late = {}
    for layer in reversed(range(depth)):
        w, s = layer_w[layer], saved[layer]
        pl_in = p[layer, 0]
        dx2, de, h3, dpre, dg_ple = _ple_bwd(dx, s["x2"], pl_in, w)
        dx1, r, da, h2, dg_mlp = _mlp_bwd(dx2, s["x1"], w)
        mixed, dattn, dconv, dg_oa, dg_oc = _mix_out_bwd(dx1, s["attn"], s["conv"], w)
        sending = {"w_o": (_wgrad(mixed, dx1, shard_rows=w_o.shape[1]), layer),
                   "w_up": (_wgrad(h2, da, shard_cols=w_up.shape[2]), layer),
                   "w_down": (_wgrad(r, dx2, shard_rows=w_down.shape[1]), layer),
                   "w_ple_gate": (_wgrad(h3, dpre, shard_rows=w_ple_gate.shape[1]), layer),
                   "w_ple": (_wgrad(pl_in, de, shard_cols=w_ple.shape[2]), layer), **late}
        names = list(sending)
        dqf, dkf, dkv, got = _attn_bwd(s["qf"], s["kf"], s["kv"], s["attn"], dattn, s["lse"],
                                       scatter=([sending[n][0] for n in names], [landed[n] for n in names],
                                                [sending[n][1] for n in names]))
        landed.update(zip(names, got))
        (dx0, dz, hb, qn, kvn, dqr, dkvr, dg_mix, dg_q, dg_kv, dg_qn, dg_kn, dcw) = _front_bwd(
            s["x"], s["z"], dx1, dqf, dkf, dkv, dconv, w, tabs)
        late = {"w_in": (_w_in_grad_shards(_wgrad(hb, dz, out_dtype=F32)).astype(BF16), layer),
                "w_uq": (_wgrad(qn, dqr, shard_cols=HEAD_PAD)[..., :uq_cols], layer),
                "w_ukv": (_wgrad(kvn, dkvr, shard_cols=HEAD_PAD), layer),
                "conv_w": (jnp.transpose(dcw[:n_taps].reshape(n_taps, N_DEV, -1), (1, 0, 2)).astype(BF16), layer)}
        gain_grads[layer] = jnp.concatenate([
            dg_mix[0], dg_q[0], dg_kv[0], dg_qn[0, :QK_NOPE], dg_qn[0, QK_NOPE:QK_HEAD], dg_kn[0, :QK_NOPE],
            dg_kn[0, QK_NOPE:QK_HEAD], dg_oa[0], dg_oc[0], dg_mlp[0], dg_ple[0]])
        dx = dx0
    names = list(late)
    landed.update(zip(names, _scatter_exchange([late[n][0] for n in names], [landed[n] for n in names],
                                               [late[n][1] for n in names])))
    grads = {n: _sum_leading(landed[n]) for n in SHARD_NAMES}

    gg = jnp.stack(gain_grads)
    gg_rows = -(-gg.size // (HALO * LANES)) * HALO
    gg_pad = jnp.pad(gg.reshape(-1), (0, gg_rows * LANES - gg.size)).reshape(gg_rows, LANES)
    gg_sum = _sum_leading(_all_gather([gg_pad])[0]).reshape(-1)[:gg.size].reshape(gg.shape)
    off = 0
    for n in GAIN_NAMES:
        width = gains[n].shape[1]
        grads[n] = gg_sum[:, off:off + width]
        off += width

    deltas, new_m, new_v = {}, {}, {}
    for n in WEIGHT_NAMES:
        deltas[n], new_m[n], new_v[n] = _adamw(weights[n], grads[n], given["m_" + n], given["v_" + n])
    return (loss, dx[None], *[grads[n] for n in WEIGHT_NAMES], *[deltas[n] for n in WEIGHT_NAMES],
            *[new_m[n] for n in WEIGHT_NAMES], *[new_v[n] for n in WEIGHT_NAMES])
```

```python
import functools

import jax
import jax.numpy as jnp
from jax import lax
from jax.experimental import pallas as pl
from jax.experimental.pallas import tpu as pltpu

F32 = jnp.float32
BF16 = jnp.bfloat16
MESH = pl.DeviceIdType.MESH

N_HEADS = 8
QK_NOPE = 64
QK_ROPE = 32
QK_HEAD = QK_NOPE + QK_ROPE
V_HEAD = 64
HEAD_PAD = 128
PAIR = 2
ATTN_SCALE = QK_HEAD ** -0.5
Q_LORA = 384
KV_LORA = 256
CONV_WIDTH = 512
ATTN_WIDTH = N_HEADS * V_HEAD
ROPE_THETA = 10000.0
EPS = 1e-6
ADAM_LR, ADAM_B1, ADAM_B2, ADAM_EPS, ADAM_WD, ADAM_STEP = 0.001, 0.9, 0.999, 1e-08, 0.01, 10

Z_Q = (0, 384)
Z_KV = (384, 640)
Z_GB = (640, 1152)
Z_GC = (1152, 1664)
Z_XIN = (1664, 2176)
Z_KPE = (2176, 2304)
Z_COLS = 2304

N_DEV = 8
LANES = 128
V7X_VMEM_LIMIT = 52 * 1024 * 1024
TOKEN_TILE = 256
LIGHT_TOKEN_TILE = 512
ATTN_BLOCK = 256
ATTN_FWD_BLOCK = 512
ATTN_FWD_ROWS = 1024
ATTN_Q_SUB = 2
ATTN_KV_SUB = 16
ROW_BLOCK = 512
WGRAD_TOKENS = 2048
WGRAD_TILE = 1024
HALO = 8

GAIN_NAMES = ("g_mix", "g_q_lat", "g_kv_lat", "g_qn_nope", "g_qn_rope", "g_kn_nope", "g_kn_rope",
              "g_out_attn", "g_out_conv", "g_mlp", "g_ple")
SHARD_NAMES = ("w_in", "w_uq", "w_ukv", "conv_w", "w_o", "w_up", "w_down", "w_ple_gate", "w_ple")
WEIGHT_NAMES = ("g_mix", "w_in", "g_q_lat", "w_uq", "g_kv_lat", "w_ukv", "g_qn_nope", "g_qn_rope", "g_kn_nope",
                "g_kn_rope", "conv_w", "g_out_attn", "g_out_conv", "w_o", "g_mlp", "w_up", "w_down", "g_ple",
                "w_ple_gate", "w_ple")


def _cparams(semantics=None):
    return pltpu.CompilerParams(dimension_semantics=semantics, vmem_limit_bytes=V7X_VMEM_LIMIT)


def _mm(a, b):
    return jnp.dot(a, b, preferred_element_type=F32)


def _mm_nt(a, b):
    return lax.dot_general(a, b, (((1,), (1,)), ((), ())), preferred_element_type=F32)


def _mm_tn(a, b):
    return lax.dot_general(a, b, (((0,), (0,)), ((), ())), preferred_element_type=F32)


def _rms(x, g):
    r = lax.rsqrt(jnp.mean(x * x, axis=-1, keepdims=True) + EPS)
    return (x * r) * g


def _rms_bwd(x, g, dy):
    r = lax.rsqrt(jnp.mean(x * x, axis=-1, keepdims=True) + EPS)
    xh = x * r
    dg = jnp.sum(dy * xh, axis=0, keepdims=True)
    dyg = dy * g
    dx = r * (dyg - xh * jnp.mean(dyg * xh, axis=-1, keepdims=True))
    return dx, dg


def _group_mean(t, gm):
    hi = t.astype(BF16)
    lo = (t - hi.astype(F32)).astype(BF16)
    return _mm(hi, gm) + _mm(lo, gm)


def _swap_rope_halves(x, lane):
    half = QK_ROPE // 2
    swapped = jnp.where(lane < QK_NOPE + half, pltpu.roll(x, x.shape[1] - half, 1), pltpu.roll(x, half, 1))
    return jnp.where((lane >= QK_NOPE) & (lane < QK_HEAD), swapped, 0.0)


def _qk_fwd(x, g, cos, sin, gm, lane):
    r = lax.rsqrt(_group_mean(x * x, gm) + EPS)
    n = (x * r) * g
    return n * cos + _swap_rope_halves(n, lane) * sin


def _qk_bwd(x, g, dy, cos, sin, gm, lane):
    r = lax.rsqrt(_group_mean(x * x, gm) + EPS)
    xh = x * r
    dn = dy * cos + _swap_rope_halves(dy * sin, lane)
    dg = jnp.sum(dn * xh, axis=0, keepdims=True)
    dng = dn * g
    dx = r * (dng - xh * _group_mean(dng * xh, gm))
    return dx, dg


def _row_shards_joined(ref):
    n, rows, cols = ref.shape
    return ref[...].reshape(n * rows, cols)


def _rows(tm, n):
    return pl.BlockSpec((tm, n), lambda i: (i, 0))


def _whole(shape):
    zeros = (0,) * len(shape)
    return pl.BlockSpec(shape, lambda i: zeros)


def _operands(arrays):
    return list(arrays), [_whole(a.shape) for a in arrays]


def _accumulate(ref, first, value):
    @pl.when(first)
    def _():
        ref[...] = jnp.zeros_like(ref)
    ref[...] += value


def _front_fwd(x, w, tabs):
    t_len, d = x.shape
    tm = min(TOKEN_TILE, t_len)
    hp = N_HEADS * HEAD_PAD

    def body(x_ref, gmix, win, gq, wuq, gkv, wukv, gqn, gkn, cw_ref, gm_ref, cos_ref, sin_ref,
             z_ref, qf_ref, kf_ref, kv_ref, conv_ref, ubuf):
        i = pl.program_id(0)
        h = _rms(x_ref[...], gmix[...])
        z = _mm(h.astype(BF16), win[...])
        z_ref[...] = z
        qnb = _rms(z[:, Z_Q[0]:Z_Q[1]], gq[...]).astype(BF16)
        kvb = _rms(z[:, Z_KV[0]:Z_KV[1]], gkv[...]).astype(BF16)
        kpe = z[:, Z_KPE[0]:Z_KPE[1]]
        kpe = jnp.concatenate([kpe] * PAIR, axis=1)
        cos, sin = (jnp.concatenate([t[...]] * PAIR, axis=1) for t in (cos_ref, sin_ref))
        gm = gm_ref[...]
        lane = lax.broadcasted_iota(jnp.int32, (tm, PAIR * HEAD_PAD), 1) & (HEAD_PAD - 1)
        for pr in range(N_HEADS // PAIR):
            sl = slice(pr * PAIR * HEAD_PAD, (pr + 1) * PAIR * HEAD_PAD)
            qf_ref[:, sl] = (_qk_fwd(_mm(qnb, wuq[pr]), gqn[...], cos, sin, gm, lane) * ATTN_SCALE).astype(BF16)
            kv = _mm(kvb, wukv[pr])
            kv_ref[:, sl] = jnp.where(lane < QK_NOPE, jnp.where(lane == 0, 1.0, 0.0), kv).astype(BF16)
            kf_ref[:, sl] = _qk_fwd(jnp.where(lane < QK_NOPE, kv, 0.0) + kpe, gkn[...], cos, sin, gm, lane).astype(BF16)
        u = z[:, Z_GC[0]:Z_GC[1]] * z[:, Z_XIN[0]:Z_XIN[1]]

        @pl.when(i == 0)
        def _():
            ubuf[0:HALO, :] = jnp.zeros((HALO, CONV_WIDTH), F32)
        ubuf[HALO:HALO + tm, :] = u
        cw = cw_ref[...]
        y = cw[0:1] * u + cw[1:2] * ubuf[pl.ds(HALO - 1, tm), :] + cw[2:3] * ubuf[pl.ds(HALO - 2, tm), :]
        conv_ref[...] = z[:, Z_GB[0]:Z_GB[1]] * y
        ubuf[0:HALO, :] = u[tm - HALO:tm, :]

    consts, const_specs = _operands([w["g_mix"], w["w_in"], w["g_q_lat"], w["w_uq"], w["g_kv_lat"], w["w_ukv"],
                                     w["g_qn"], w["g_kn"], w["conv_w"], tabs["gm"]])
    out_shape = (jax.ShapeDtypeStruct((t_len, Z_COLS), F32), jax.ShapeDtypeStruct((t_len, hp), BF16),
                 jax.ShapeDtypeStruct((t_len, hp), BF16), jax.ShapeDtypeStruct((t_len, hp), BF16),
                 jax.ShapeDtypeStruct((t_len, CONV_WIDTH), F32))
    return pl.pallas_call(body, name="front_fwd", grid=(t_len // tm,),
                          in_specs=[_rows(tm, d)] + const_specs + [_rows(tm, HEAD_PAD)] * 2,
                          out_specs=tuple(_rows(tm, s.shape[1]) for s in out_shape), out_shape=out_shape,
                          scratch_shapes=[pltpu.VMEM((tm + HALO, CONV_WIDTH), F32)],
                          compiler_params=_cparams(("arbitrary",)))(x, *consts, tabs["cos"], tabs["sin"])


def _attn_fwd(qf, kf, kv, gather=None):
    t_len = qf.shape[0]
    blk = min(ATTN_FWD_BLOCK, t_len)
    bq = blk // ATTN_Q_SUB
    span = min(ATTN_FWD_ROWS, t_len)
    nb = t_len // span
    n_sub = span // bq
    chains = [(hh, a) for hh in range(2) for a in range(n_sub)]

    def body(q_ref, k_ref, kv_ref, o_ref, lse_ref):
        lane = lax.broadcasted_iota(jnp.int32, (bq, LANES), 1)
        row = lax.broadcasted_iota(jnp.int32, (bq, blk), 0)
        col = lax.broadcasted_iota(jnp.int32, (bq, blk), 1)

        def head_cols(hh):
            return slice(hh * HEAD_PAD, (hh + 1) * HEAD_PAD)

        def softmax_step(s, kvv, state, first_row=None):
            m, acc = state
            if first_row is not None:
                s = jnp.where(col <= row + first_row, s, -jnp.inf)
            m_new = jnp.maximum(m, jnp.max(s, axis=-1, keepdims=True))
            p = jnp.exp(s - m_new)
            acc = jnp.exp(m - m_new) * acc + _mm(p.astype(BF16), kvv)
            return m_new, acc

        def finish(state):
            m, acc = state
            l = jnp.sum(jnp.where(lane == 0, acc, 0.0), axis=-1, keepdims=True)
            return acc / l, jnp.broadcast_to(m + jnp.log(l), (bq, LANES))

        def qblock(i):
            rows = [pl.ds(a * bq, bq) for a in range(n_sub)]
            first = [i * span + a * bq for a in range(n_sub)]
            states = [(jnp.full((bq, 1), -jnp.inf, F32), jnp.zeros((bq, LANES), F32))] * len(chains)
            steps = []
            for j in range((first[-1] + bq - 1) // blk + 1):
                for n, (hh, a) in enumerate(chains):
                    if j * blk <= first[a] + bq - 1:
                        diagonal = (j + 1) * blk - 1 > first[a]
                        steps.append((n, j, first[a] - j * blk if diagonal else None))
            ahead = len(chains)
            scores = []
            for t in range(len(steps) + ahead):
                if t < len(steps):
                    n, j, _ = steps[t]
                    hh, a = chains[n]
                    scores.append(_mm_nt(q_ref[rows[a], head_cols(hh)], k_ref[pl.ds(j * blk, blk), head_cols(hh)]))
                if t >= ahead:
                    n, j, offset = steps[t - ahead]
                    states[n] = softmax_step(scores[t - ahead], kv_ref[pl.ds(j * blk, blk), head_cols(chains[n][0])],
                                             states[n], offset)
            for a in range(n_sub):
                (o0, lse0), (o1, lse1) = finish(states[chains.index((0, a))]), finish(states[chains.index((1, a))])
                o_ref[rows[a], :] = jnp.where(lane < V_HEAD, pltpu.roll(o0, V_HEAD, 1), o1)
                lse_ref[rows[a], head_cols(0)] = lse0
                lse_ref[rows[a], head_cols(1)] = lse1

        mine = pl.program_id(0) % nb
        for i in range(nb):
            pl.when(mine == i)(functools.partial(qblock, i))

    q_spec = pl.BlockSpec((span, 2 * HEAD_PAD), lambda g: (g % nb, g // nb))
    keys = pl.BlockSpec((t_len, 2 * HEAD_PAD), lambda g: (0, g // nb))
    o_spec = pl.BlockSpec((span, 2 * V_HEAD), lambda g: (g % nb, g // nb))
    out_shape = (jax.ShapeDtypeStruct((t_len, ATTN_WIDTH), F32), jax.ShapeDtypeStruct((t_len, N_HEADS * LANES), F32))
    (attn, lse), gathered = _call_hosting_gather("attn_fwd", body, N_HEADS // 2 * nb, [q_spec, keys, keys],
                                                 (o_spec, q_spec), out_shape, [], (qf, kf, kv), gather)
    return attn, lse, gathered


def _call_hosting_gather(name, body, n_steps, in_specs, out_specs, out_shape, scratch_shapes, args, gather):
    if not gather:
        return pl.pallas_call(body, name=name, grid=(n_steps,), in_specs=list(in_specs), out_specs=tuple(out_specs),
                              out_shape=tuple(out_shape), scratch_shapes=list(scratch_shapes),
                              compiler_params=_cparams(("arbitrary",)))(*args), ()
    n_in, n_out, n_g = len(in_specs), len(out_shape), len(gather)

    def hosting_body(*refs):
        ins, refs = refs[:n_in], refs[n_in:]
        x_refs = [r.at[layer] for r, (_, layer) in zip(refs[:n_g], gather)]
        outs, landing, sems, scratch = (refs[n_g:n_g + n_out], refs[n_g + n_out:2 * n_g + n_out],
                                        refs[2 * n_g + n_out:2 * n_g + n_out + 3], refs[2 * n_g + n_out + 3:])
        start, forward, finish = _gather_phases(x_refs, landing, *sems)
        step = pl.program_id(0)
        pl.when(step == 0)(start)
        pl.when(step == n_steps - 1)(forward)
        body(*ins, *outs, *scratch)
        pl.when(step == n_steps - 1)(finish)

    res = pl.pallas_call(hosting_body, name=name + "_gather", grid=(n_steps,), in_specs=list(in_specs) + _any_specs(n_g),
                         out_specs=tuple(out_specs) + tuple(_any_specs(n_g)),
                         out_shape=tuple(out_shape) + _gather_out_shape([s[layer] for s, layer in gather]),
                         scratch_shapes=_gather_semaphores(n_g) + list(scratch_shapes),
                         compiler_params=_cparams(("arbitrary",)))(*args, *[s for s, _ in gather])
    return res[:n_out], res[n_out:]


def _mix_out_fwd(x, attn, conv, w):
    t_len, d = x.shape
    tm = min(LIGHT_TOKEN_TILE, t_len)

    def body(x_ref, a_ref, c_ref, goa, goc, wo, x1_ref):
        mixed = jnp.concatenate([_rms(a_ref[...], goa[...]), _rms(c_ref[...], goc[...])], axis=1)
        x1_ref[...] = x_ref[...] + _mm(mixed.astype(BF16), _row_shards_joined(wo))

    consts, const_specs = _operands([w["g_out_attn"], w["g_out_conv"], w["w_o"]])
    return pl.pallas_call(body, name="mix_out_fwd", grid=(t_len // tm,),
                          in_specs=[_rows(tm, d), _rows(tm, ATTN_WIDTH), _rows(tm, CONV_WIDTH)] + const_specs,
                          out_specs=_rows(tm, d), out_shape=jax.ShapeDtypeStruct((t_len, d), F32),
                          compiler_params=_cparams(("parallel",)))(x, attn, conv, *consts)


def _mlp_fwd(x1, w, gather=None):
    t_len, d = x1.shape
    tm = min(TOKEN_TILE, t_len)

    def body(x_ref, g, wup, wdn, x2_ref):
        x1v = x_ref[...]
        hb = _rms(x1v, g[...]).astype(BF16)
        acc = x1v
        for k in range(N_DEV):
            a = jnp.maximum(_mm(hb, wup[k]), 0.0)
            acc = acc + _mm((a * a).astype(BF16), wdn[k])
        x2_ref[...] = acc

    consts, const_specs = _operands([w["g_mlp"], w["w_up"], w["w_down"]])
    (x2,), gathered = _call_hosting_gather("mlp_fwd", body, t_len // tm, [_rows(tm, d)] + const_specs, (_rows(tm, d),),
                                           (jax.ShapeDtypeStruct((t_len, d), F32),), [], (x1, *consts), gather)
    return x2, gathered


def _ple_fwd(x2, p, w):
    t_len, d = x2.shape
    tm = min(LIGHT_TOKEN_TILE, t_len)

    def body(x_ref, p_ref, g, wg, wp, x3_ref):
        x2v = x_ref[...]
        gate = jax.nn.sigmoid(_mm(_rms(x2v, g[...]).astype(BF16), _row_shards_joined(wg)))
        pb = p_ref[...].astype(BF16)
        e = jnp.concatenate([_mm(pb, wp[k]) for k in range(N_DEV)], axis=1)
        x3_ref[...] = x2v + gate * e

    consts, const_specs = _operands([w["g_ple"], w["w_ple_gate"], w["w_ple"]])
    return pl.pallas_call(body, name="ple_fwd", grid=(t_len // tm,),
                          in_specs=[_rows(tm, d), _rows(tm, p.shape[1])] + const_specs, out_specs=_rows(tm, d),
                          out_shape=jax.ShapeDtypeStruct((t_len, d), F32),
                          compiler_params=_cparams(("parallel",)))(x2, p, *consts)


def _loss_and_grad(y, target):
    t_len, d = y.shape
    tm = min(TOKEN_TILE, t_len)

    def body(y_ref, t_ref, sq_ref, dy_ref):
        err = y_ref[...] - t_ref[...]
        dy_ref[...] = err / d
        total = jnp.sum(jnp.sum(err * err, axis=0, keepdims=True), axis=1, keepdims=True)
        _accumulate(sq_ref, pl.program_id(0) == 0, jnp.broadcast_to(total, (HALO, LANES)))

    return pl.pallas_call(body, name="loss_grad", grid=(t_len // tm,), in_specs=[_rows(tm, d), _rows(tm, d)],
                          out_specs=(_whole((HALO, LANES)), _rows(tm, d)),
                          out_shape=(jax.ShapeDtypeStruct((HALO, LANES), F32), jax.ShapeDtypeStruct((t_len, d), F32)),
                          compiler_params=_cparams(("arbitrary",)))(y, target)


def _ple_bwd(dx3, x2, p, w):
    t_len, d = x2.shape
    tm = min(LIGHT_TOKEN_TILE, t_len)

    def body(dx3_ref, x_ref, p_ref, g, wg, wp, dx2_ref, de_ref, h3_ref, dpre_ref, dg_ref):
        x2v, dx3v = x_ref[...], dx3_ref[...]
        hb = _rms(x2v, g[...]).astype(BF16)
        h3_ref[...] = hb
        w_gate = _row_shards_joined(wg)
        gate = jax.nn.sigmoid(_mm(hb, w_gate))
        pb = p_ref[...].astype(BF16)
        e = jnp.concatenate([_mm(pb, wp[k]) for k in range(N_DEV)], axis=1)
        de_ref[...] = (dx3v * gate).astype(BF16)
        dpre = ((dx3v * e) * gate * (1.0 - gate)).astype(BF16)
        dpre_ref[...] = dpre
        dx, dg = _rms_bwd(x2v, g[...], _mm_nt(dpre, w_gate))
        dx2_ref[...] = dx3v + dx
        _accumulate(dg_ref, pl.program_id(0) == 0, dg)

    consts, const_specs = _operands([w["g_ple"], w["w_ple_gate"], w["w_ple"]])
    out_shape = (jax.ShapeDtypeStruct((t_len, d), F32), jax.ShapeDtypeStruct((t_len, d), BF16),
                 jax.ShapeDtypeStruct((t_len, d), BF16), jax.ShapeDtypeStruct((t_len, d), BF16),
                 jax.ShapeDtypeStruct((1, d), F32))
    return pl.pallas_call(body, name="ple_bwd", grid=(t_len // tm,),
                          in_specs=[_rows(tm, d), _rows(tm, d), _rows(tm, p.shape[1])] + const_specs,
                          out_specs=(_rows(tm, d),) * 4 + (_whole((1, d)),), out_shape=out_shape,
                          compiler_params=_cparams(("arbitrary",)))(dx3, x2, p, *consts)


def _mlp_bwd(dx2, x1, w):
    t_len, d = x1.shape
    tm = min(TOKEN_TILE, t_len)
    fc = w["w_up"].shape[2]
    ff = N_DEV * fc

    def body(dx2_ref, x_ref, g, wup, wdn, dx1_ref, r_ref, da_ref, h2_ref, dg_ref):
        x1v, dx2v = x_ref[...], dx2_ref[...]
        hb = _rms(x1v, g[...]).astype(BF16)
        h2_ref[...] = hb
        dxb = dx2v.astype(BF16)
        dh = jnp.zeros((tm, d), F32)
        for k in range(N_DEV):
            a = jnp.maximum(_mm(hb, wup[k]), 0.0)
            r_ref[:, k * fc:(k + 1) * fc] = (a * a).astype(BF16)
            da = (_mm_nt(dxb, wdn[k]) * (2.0 * a)).astype(BF16)
            da_ref[:, k * fc:(k + 1) * fc] = da
            dh = dh + _mm_nt(da, wup[k])
        dx, dg = _rms_bwd(x1v, g[...], dh)
        dx1_ref[...] = dx2v + dx
        _accumulate(dg_ref, pl.program_id(0) == 0, dg)

    consts, const_specs = _operands([w["g_mlp"], w["w_up"], w["w_down"]])
    out_shape = (jax.ShapeDtypeStruct((t_len, d), F32), jax.ShapeDtypeStruct((t_len, ff), BF16),
                 jax.ShapeDtypeStruct((t_len, ff), BF16), jax.ShapeDtypeStruct((t_len, d), BF16),
                 jax.ShapeDtypeStruct((1, d), F32))
    return pl.pallas_call(body, name="mlp_bwd", grid=(t_len // tm,), in_specs=[_rows(tm, d), _rows(tm, d)] + const_specs,
                          out_specs=(_rows(tm, d), _rows(tm, ff), _rows(tm, ff), _rows(tm, d), _whole((1, d))),
                          out_shape=out_shape, compiler_params=_cparams(("arbitrary",)))(dx2, x1, *consts)


def _mix_out_bwd(dx1, attn, conv, w):
    t_len, d = dx1.shape
    tm = min(LIGHT_TOKEN_TILE, t_len)

    def body(dx1_ref, a_ref, c_ref, goa, goc, wo, mixed_ref, da_ref, dc_ref, dgoa_ref, dgoc_ref):
        av, cv = a_ref[...], c_ref[...]
        mixed_ref[...] = jnp.concatenate([_rms(av, goa[...]), _rms(cv, goc[...])], axis=1).astype(BF16)
        dmixed = _mm_nt(dx1_ref[...].astype(BF16), _row_shards_joined(wo))
        da, dga = _rms_bwd(av, goa[...], dmixed[:, :ATTN_WIDTH])
        dc, dgc = _rms_bwd(cv, goc[...], dmixed[:, ATTN_WIDTH:])
        da_ref[...] = da
        dc_ref[...] = dc
        first = pl.program_id(0) == 0
        _accumulate(dgoa_ref, first, dga)
        _accumulate(dgoc_ref, first, dgc)

    consts, const_specs = _operands([w["g_out_attn"], w["g_out_conv"], w["w_o"]])
    out_shape = (jax.ShapeDtypeStruct((t_len, d), BF16), jax.ShapeDtypeStruct((t_len, ATTN_WIDTH), F32),
                 jax.ShapeDtypeStruct((t_len, CONV_WIDTH), F32), jax.ShapeDtypeStruct((1, ATTN_WIDTH), F32),
                 jax.ShapeDtypeStruct((1, CONV_WIDTH), F32))
    out_specs = (_rows(tm, d), _rows(tm, ATTN_WIDTH), _rows(tm, CONV_WIDTH), _whole((1, ATTN_WIDTH)),
                 _whole((1, CONV_WIDTH)))
    return pl.pallas_call(body, name="mix_out_bwd", grid=(t_len // tm,),
                          in_specs=[_rows(tm, d), _rows(tm, ATTN_WIDTH), _rows(tm, CONV_WIDTH)] + const_specs,
                          out_specs=out_specs, out_shape=out_shape,
                          compiler_params=_cparams(("arbitrary",)))(dx1, attn, conv, *consts)


def _attn_bwd(qf, kf, kv, o, do, lse, scatter):
    t_len = qf.shape[0]
    blk = min(ATTN_BLOCK, t_len)
    nb = t_len // blk
    n_sub = min(ATTN_KV_SUB, nb)
    reps = blk // LANES

    def body(q_ref, k_ref, kv_ref, o_ref, do_ref, lse_ref, dq_ref, dk_ref, dkv_ref, delta_ref, dob_ref):
        hd = pl.program_id(0)
        lane = lax.broadcasted_iota(jnp.int32, (blk, LANES), 1)
        even = (lane * 0 + hd % 2) == 0
        mine = jnp.where(lane < V_HEAD, 0, 1) == hd % 2
        row = lax.broadcasted_iota(jnp.int32, (blk, blk), 0)
        col = lax.broadcasted_iota(jnp.int32, (blk, blk), 1)
        dq_ref[...] = jnp.zeros_like(dq_ref)

        def prepare(i, carry):
            qs = pl.ds(pl.multiple_of(i * blk, blk), blk)
            dov = do_ref[qs, :]
            prod = jnp.where(mine, dov * o_ref[qs, :], 0.0)
            delta_ref[qs, :] = jnp.broadcast_to(jnp.sum(prod, axis=-1, keepdims=True), (blk, LANES))
            moved = jnp.where(even, pltpu.roll(dov, V_HEAD, 1), dov)
            dob_ref[qs, :] = jnp.where(lane >= V_HEAD, moved, 0.0).astype(BF16)
            return carry
        lax.fori_loop(0, nb, prepare, 0)

        def kvblock(jj, carry):
            base = jj * n_sub
            kss = [pl.ds(pl.multiple_of((base + a) * blk, blk), blk) for a in range(n_sub)]
            k = [k_ref[ks, :] for ks in kss]
            kvv = [kv_ref[ks, :] for ks in kss]

            def products(i):
                qs = pl.ds(pl.multiple_of(i * blk, blk), blk)
                q, dob = q_ref[qs, :], dob_ref[qs, :]
                return tuple((_mm_nt(q, k[a]), _mm_nt(dob, kvv[a])) for a in range(n_sub))

            def qstep(i, raw, accs, kinds):
                qs = pl.ds(pl.multiple_of(i * blk, blk), blk)
                q = q_ref[qs, :]
                dob = dob_ref[qs, :]
                lse_t = jnp.concatenate([lse_ref[qs, :]] * reps, axis=1)
                delta_t = jnp.concatenate([delta_ref[qs, :]] * reps, axis=1)
                new, dq_add = [], None
                for a in range(n_sub):
                    if kinds[a] is None:
                        new.append(accs[a])
                        continue
                    dk_acc, dv_acc = accs[a]
                    s, dp = raw[a]
                    if kinds[a]:
                        s = jnp.where(col <= row, s, -jnp.inf)
                    p = jnp.exp(s - lse_t)
                    ds = (p * (dp - delta_t)).astype(BF16)
                    new.append((dk_acc + _mm_tn(ds, q), dv_acc + _mm_tn(p.astype(BF16), dob)))
                    part = _mm(ds, k[a])
                    dq_add = part if dq_add is None else dq_add + part
                dq_ref[qs, :] += dq_add
                return tuple(new)

            zero = jnp.zeros((blk, LANES), F32)
            accs = ((zero, zero),) * n_sub
            for b in range(n_sub):
                accs = qstep(base + b, products(base + b), accs, tuple((a == b) if a <= b else None for a in range(n_sub)))

            def pipelined(i, carried):
                raw, acc = carried
                return products(jnp.minimum(i + 1, nb - 1)), qstep(i, raw, acc, (False,) * n_sub)

            first = base + n_sub
            _, accs = lax.fori_loop(first, nb, pipelined, (products(jnp.minimum(first, nb - 1)), accs))
            for a in range(n_sub):
                dk_ref[kss[a], :] = accs[a][0]
                dkv_ref[kss[a], :] = accs[a][1]
            return carry
        lax.fori_loop(0, nb // n_sub, kvblock, 0)

    head = pl.BlockSpec((t_len, HEAD_PAD), lambda h: (0, h))
    pair = pl.BlockSpec((t_len, 2 * V_HEAD), lambda h: (0, h // 2))
    out = jax.ShapeDtypeStruct((t_len, N_HEADS * HEAD_PAD), F32)
    vmem_scratch = [pltpu.VMEM((t_len, LANES), F32), pltpu.VMEM((t_len, LANES), BF16)]
    (dq, dk, dkv), landed = _call_hosting_scatter("attn_bwd", body, N_HEADS, [head, head, head, pair, pair, head],
                                                  (head, head, head), (out, out, out), vmem_scratch,
                                                  (qf, kf, kv, o, do, lse), scatter)
    return dq, dk, dkv, landed


def _call_hosting_scatter(name, body, n_steps, in_specs, out_specs, out_shape, scratch_shapes, args, scatter):
    parts, landed, layers = scatter
    n_in, n_out, n_sc = len(in_specs), len(out_shape), len(parts)

    def hosting_body(*refs):
        ins, part_refs, refs = refs[:n_in], refs[n_in:n_in + n_sc], refs[n_in + 2 * n_sc:]
        outs, landed_refs, sems, scratch = (refs[:n_out], refs[n_out:n_out + n_sc], refs[n_out + n_sc:n_out + n_sc + 3],
                                            refs[n_out + n_sc + 3:])
        start, finish = _scatter_phases(part_refs, landed_refs, *sems, layers)
        step = pl.program_id(0)
        pl.when(step == 0)(start)
        body(*ins, *outs, *scratch)
        pl.when(step == n_steps - 1)(finish)

    res = pl.pallas_call(hosting_body, name=name + "_scatter", grid=(n_steps,),
                         in_specs=list(in_specs) + _any_specs(2 * n_sc),
                         out_specs=tuple(out_specs) + tuple(_any_specs(n_sc)),
                         out_shape=tuple(out_shape) + _same_shapes(landed),
                         scratch_shapes=_scatter_semaphores(n_sc) + list(scratch_shapes),
                         input_output_aliases={n_in + n_sc + a: n_out + a for a in range(n_sc)},
                         compiler_params=_cparams(("arbitrary",)))(*args, *parts, *landed)
    return res[:n_out], res[n_out:]


def _front_bwd(x, z, dx1, dqf, dkf, dkv_in, dconv, w, tabs, scatter):
    t_len, d = x.shape
    tm = min(TOKEN_TILE, t_len)
    nt = t_len // tm
    hb_per_tile = tm // HALO
    n_halo = t_len // HALO
    hp = N_HEADS * HEAD_PAD

    def body(x_ref, z_ref, zp_ref, zn_ref, dx1_ref, dqf_ref, dkf_ref, dkv_ref, dc_ref, dcn_ref,
             gmix, win, gq, wuq, gkv, wukv, gqn, gkn, cw_ref, gm_ref, cos_ref, sin_ref,
             dx_ref, dz_ref, h_ref, qn_ref, kvn_ref, dqr_ref, dkvr_ref,
             dgmix_ref, dgq_ref, dgkv_ref, dgqn_ref, dgkn_ref, dcw_ref, ubuf, dybuf):
        i = pl.program_id(0)
        first = i == 0
        xv, zv = x_ref[...], z_ref[...]
        hb = _rms(xv, gmix[...]).astype(BF16)
        h_ref[...] = hb
        zq, zkv = zv[:, Z_Q[0]:Z_Q[1]], zv[:, Z_KV[0]:Z_KV[1]]
        qnb = _rms(zq, gq[...]).astype(BF16)
        qn_ref[...] = qnb
        kvb = _rms(zkv, gkv[...]).astype(BF16)
        kvn_ref[...] = kvb
        kpe = zv[:, Z_KPE[0]:Z_KPE[1]]
        kpe = jnp.concatenate([kpe] * PAIR, axis=1)
        cos, sin = (jnp.concatenate([t[...]] * PAIR, axis=1) for t in (cos_ref, sin_ref))
        gm = gm_ref[...]
        width = PAIR * HEAD_PAD
        lane = lax.broadcasted_iota(jnp.int32, (tm, width), 1) & (HEAD_PAD - 1)
        is_nope = lane < QK_NOPE
        is_rope = (lane >= QK_NOPE) & (lane < QK_HEAD)
        dkpe = jnp.zeros((tm, width), F32)
        dgqn = jnp.zeros((1, width), F32)
        dgkn = jnp.zeros((1, width), F32)
        dqn = jnp.zeros((tm, Q_LORA), F32)
        dkvn = jnp.zeros((tm, KV_LORA), F32)
        for pr in range(N_HEADS // PAIR):
            sl = slice(pr * width, (pr + 1) * width)
            dxq, dg = _qk_bwd(_mm(qnb, wuq[pr]), gqn[...], dqf_ref[:, sl] * ATTN_SCALE, cos, sin, gm, lane)
            dxq = dxq.astype(BF16)
            dqr_ref[:, sl] = dxq
            dqn = dqn + _mm_nt(dxq, wuq[pr])
            dgqn = dgqn + dg
            k_raw = jnp.where(is_nope, _mm(kvb, wukv[pr]), 0.0) + kpe
            dxk, dg = _qk_bwd(k_raw, gkn[...], dkf_ref[:, sl], cos, sin, gm, lane)
            dkv = jnp.where(is_nope, dxk, dkv_ref[:, sl]).astype(BF16)
            dkvr_ref[:, sl] = dkv
            dkvn = dkvn + _mm_nt(dkv, wukv[pr])
            dkpe = dkpe + jnp.where(is_rope, dxk, 0.0)
            dgkn = dgkn + dg
        dkpe = dkpe[:, :HEAD_PAD] + dkpe[:, HEAD_PAD:]
        _accumulate(dgqn_ref, first, dgqn[:, :HEAD_PAD] + dgqn[:, HEAD_PAD:])
        _accumulate(dgkn_ref, first, dgkn[:, :HEAD_PAD] + dgkn[:, HEAD_PAD:])
        dzq, dg = _rms_bwd(zq, gq[...], dqn)
        _accumulate(dgq_ref, first, dg)
        dzkv, dg = _rms_bwd(zkv, gkv[...], dkvn)
        _accumulate(dgkv_ref, first, dg)

        gb, gc, xin = zv[:, Z_GB[0]:Z_GB[1]], zv[:, Z_GC[0]:Z_GC[1]], zv[:, Z_XIN[0]:Z_XIN[1]]
        u = gc * xin
        dcv = dc_ref[...]
        dy = dcv * gb
        zp, zn = zp_ref[...], zn_ref[...]
        ubuf[0:HALO, :] = (zp[:, Z_GC[0]:Z_GC[1]] * zp[:, Z_XIN[0]:Z_XIN[1]]) * jnp.where(first, 0.0, 1.0)
        ubuf[HALO:HALO + tm, :] = u
        dybuf[0:tm, :] = dy
        dybuf[tm:tm + HALO, :] = (dcn_ref[...] * zn[:, Z_GB[0]:Z_GB[1]]) * jnp.where(i == nt - 1, 0.0, 1.0)
        cw = cw_ref[...]
        u1, u2 = ubuf[pl.ds(HALO - 1, tm), :], ubuf[pl.ds(HALO - 2, tm), :]
        y = cw[0:1] * u + cw[1:2] * u1 + cw[2:3] * u2
        du = cw[0:1] * dy + cw[1:2] * dybuf[pl.ds(1, tm), :] + cw[2:3] * dybuf[pl.ds(2, tm), :]
        dcw = jnp.concatenate([jnp.sum(dy * u, axis=0, keepdims=True), jnp.sum(dy * u1, axis=0, keepdims=True),
                               jnp.sum(dy * u2, axis=0, keepdims=True), jnp.zeros((HALO - 3, CONV_WIDTH), F32)], axis=0)
        _accumulate(dcw_ref, first, dcw)

        dz_ref[:, Z_Q[0]:Z_Q[1]] = dzq.astype(BF16)
        dz_ref[:, Z_KV[0]:Z_KV[1]] = dzkv.astype(BF16)
        dz_ref[:, Z_GB[0]:Z_GB[1]] = (dcv * y).astype(BF16)
        dz_ref[:, Z_GC[0]:Z_GC[1]] = (du * xin).astype(BF16)
        dz_ref[:, Z_XIN[0]:Z_XIN[1]] = (du * gc).astype(BF16)
        dz_ref[:, Z_KPE[0]:Z_KPE[1]] = dkpe.astype(BF16)
        dx, dg = _rms_bwd(xv, gmix[...], _mm_nt(dz_ref[...], win[...]))
        dx_ref[...] = dx1_ref[...] + dx
        _accumulate(dgmix_ref, first, dg)

    prev_halo = lambda n: pl.BlockSpec((HALO, n), lambda i: (jnp.maximum(i * hb_per_tile - 1, 0), 0))
    next_halo = lambda n: pl.BlockSpec((HALO, n), lambda i: (jnp.minimum((i + 1) * hb_per_tile, n_halo - 1), 0))
    consts, const_specs = _operands([w["g_mix"], w["w_in"], w["g_q_lat"], w["w_uq"], w["g_kv_lat"], w["w_ukv"],
                                     w["g_qn"], w["g_kn"], w["conv_w"], tabs["gm"]])
    in_specs = ([_rows(tm, d), _rows(tm, Z_COLS), prev_halo(Z_COLS), next_halo(Z_COLS), _rows(tm, d), _rows(tm, hp),
                 _rows(tm, hp), _rows(tm, hp), _rows(tm, CONV_WIDTH), next_halo(CONV_WIDTH)]
                + const_specs + [_rows(tm, HEAD_PAD)] * 2)
    out_shape = (jax.ShapeDtypeStruct((t_len, d), F32), jax.ShapeDtypeStruct((t_len, Z_COLS), BF16),
                 jax.ShapeDtypeStruct((t_len, d), BF16), jax.ShapeDtypeStruct((t_len, Q_LORA), BF16),
                 jax.ShapeDtypeStruct((t_len, KV_LORA), BF16), jax.ShapeDtypeStruct((t_len, hp), BF16),
                 jax.ShapeDtypeStruct((t_len, hp), BF16),
                 jax.ShapeDtypeStruct((1, d), F32), jax.ShapeDtypeStruct((1, Q_LORA), F32),
                 jax.ShapeDtypeStruct((1, KV_LORA), F32), jax.ShapeDtypeStruct((1, LANES), F32),
                 jax.ShapeDtypeStruct((1, LANES), F32), jax.ShapeDtypeStruct((HALO, CONV_WIDTH), F32))
    out_specs = tuple(_rows(tm, s.shape[1]) for s in out_shape[:7]) + tuple(_whole(s.shape) for s in out_shape[7:])
    conv_scratch = [pltpu.VMEM((tm + HALO, CONV_WIDTH), F32), pltpu.VMEM((tm + HALO, CONV_WIDTH), F32)]
    args = (x, z, z, z, dx1, dqf, dkf, dkv_in, dconv, dconv, *consts, tabs["cos"], tabs["sin"])
    return _call_hosting_scatter("front_bwd", body, nt, in_specs, out_specs, out_shape, conv_scratch, args, scatter)


def _wgrad(a, b, shard_cols=None, shard_rows=None, out_dtype=BF16):
    t_len, kk = a.shape
    nn = b.shape[1]
    tk = min(kk, WGRAD_TILE)
    tn = next(c for c in range(min(nn, WGRAD_TILE), 0, -LANES) if nn % c == 0 and c % (shard_cols or LANES) == 0)
    tt = min(t_len, WGRAD_TOKENS)
    nt = t_len // tt
    per_block = tn // shard_cols if shard_cols else tk // shard_rows if shard_rows else 1

    def body(a_ref, b_ref, o_ref, acc):
        t = pl.program_id(2)

        @pl.when(t == 0)
        def _():
            acc[...] = jnp.zeros_like(acc)
        acc[...] += _mm_tn(a_ref[...].astype(BF16), b_ref[...].astype(BF16))

        @pl.when(t == nt - 1)
        def _():
            if shard_cols:
                for s in range(per_block):
                    o_ref[s] = acc[:, s * shard_cols:(s + 1) * shard_cols].astype(out_dtype)
            elif shard_rows:
                for s in range(per_block):
                    o_ref[s] = acc[s * shard_rows:(s + 1) * shard_rows, :].astype(out_dtype)
            else:
                o_ref[...] = acc[...].astype(out_dtype)

    if shard_cols:
        out_shape = jax.ShapeDtypeStruct((nn // shard_cols, kk, shard_cols), out_dtype)
        out_spec = pl.BlockSpec((per_block, tk, shard_cols), lambda i, j, t: (j, i, 0))
    elif shard_rows:
        out_shape = jax.ShapeDtypeStruct((kk // shard_rows, shard_rows, nn), out_dtype)
        out_spec = pl.BlockSpec((per_block, shard_rows, tn), lambda i, j, t: (i, 0, j))
    else:
        out_shape = jax.ShapeDtypeStruct((kk, nn), out_dtype)
        out_spec = pl.BlockSpec((tk, tn), lambda i, j, t: (i, j))
    return pl.pallas_call(body, name="wgrad", grid=(kk // tk, nn // tn, nt),
                          in_specs=[pl.BlockSpec((tt, tk), lambda i, j, t: (t, i)),
                                    pl.BlockSpec((tt, tn), lambda i, j, t: (t, j))],
                          out_specs=out_spec, out_shape=out_shape, scratch_shapes=[pltpu.VMEM((tk, tn), F32)],
                          compiler_params=_cparams(("parallel", "parallel", "arbitrary")))(a, b)


def _my_place():
    return lax.axis_index("x"), lax.axis_index("y"), lax.axis_index("c")


def _any_specs(n):
    return [pl.BlockSpec(memory_space=pl.ANY)] * n


def _all_gather(blocks):
    n = len(blocks)

    def body(*refs):
        start, forward, finish = _gather_phases(refs[:n], refs[n:2 * n], *refs[2 * n:])
        start()
        forward()
        finish()

    return pl.pallas_call(body, name="all_gather", out_shape=_gather_out_shape(blocks), in_specs=_any_specs(n),
                          out_specs=tuple(_any_specs(n)), scratch_shapes=_gather_semaphores(n))(*blocks)


def _gather_out_shape(blocks):
    return tuple(jax.ShapeDtypeStruct((N_DEV,) + b.shape, b.dtype) for b in blocks)


def _gather_semaphores(n):
    return [pltpu.SemaphoreType.DMA((n, 7)), pltpu.SemaphoreType.DMA((n, 7)), pltpu.SemaphoreType.DMA((n,))]


def _gather_phases(x_refs, out_refs, send_sems, recv_sems, local_sems):
    n = len(x_refs)
    x, y, c = _my_place()
    me, sibling = (x, y, c), (x, y, 1 - c)
    chips = [(1 - x, y), (x, 1 - y), (1 - x, 1 - y)]

    def slot(a, px, py, pc):
        return out_refs[a].at[4 * px + 2 * py + pc]

    def copy(a, k, blk, to, src=None):
        return pltpu.make_async_remote_copy(src_ref=slot(a, *blk) if src is None else src, dst_ref=slot(a, *blk),
                                            send_sem=send_sems.at[a, k], recv_sem=recv_sems.at[a, k],
                                            device_id=to, device_id_type=MESH)

    def own(a):
        return pltpu.make_async_copy(x_refs[a], slot(a, *me), local_sems.at[a])

    def first_hop(a):
        return [copy(a, 0, me, sibling, src=x_refs[a])] + [copy(a, 1 + j, me, (*chip, c), src=x_refs[a])
                                                           for j, chip in enumerate(chips)]

    def passed_on(a):
        return [copy(a, 4 + j, (*chip, c), sibling) for j, chip in enumerate(chips)]

    def start():
        for a in range(n):
            own(a).start()
        for a in range(n):
            for cp in first_hop(a):
                cp.start()

    def forward():
        for j, chip in enumerate(chips):
            for a in range(n):
                copy(a, 1 + j, (*chip, c), me).wait_recv()
                passed_on(a)[j].start()

    def finish():
        for a in range(n):
            copy(a, 0, sibling, me).wait_recv()
        for j, chip in enumerate(chips):
            for a in range(n):
                copy(a, 4 + j, (*chip, 1 - c), me).wait_recv()
        for a in range(n):
            for cp in first_hop(a) + passed_on(a):
                cp.wait_send()
            own(a).wait()

    return start, forward, finish


def _scatter_exchange(parts, landed, layers):
    n = len(parts)

    def body(*refs):
        start, finish = _scatter_phases(refs[:n], refs[2 * n:3 * n], *refs[3 * n:], layers)
        start()
        finish()

    return pl.pallas_call(body, name="scatter_exchange", out_shape=_same_shapes(landed), in_specs=_any_specs(2 * n),
                          out_specs=tuple(_any_specs(n)), scratch_shapes=_scatter_semaphores(n),
                          input_output_aliases={n + a: a for a in range(n)})(*parts, *landed)


def _same_shapes(arrays):
    return tuple(jax.ShapeDtypeStruct(a.shape, a.dtype) for a in arrays)


def _scatter_semaphores(n):
    return [pltpu.SemaphoreType.DMA((n, N_DEV - 1)), pltpu.SemaphoreType.DMA((n, N_DEV - 1)), pltpu.SemaphoreType.DMA((n,))]


def _scatter_phases(part_refs, landed_refs, send_sems, recv_sems, local_sems, layers):
    n = len(part_refs)
    x, y, c = _my_place()
    flips = [(0, 0, 1), (1, 0, 0), (0, 1, 0), (1, 1, 0), (1, 0, 1), (0, 1, 1), (1, 1, 1)]
    peers = [((1 - x) if fx else x, (1 - y) if fy else y, (1 - c) if fc else c) for fx, fy, fc in flips]
    my_k = 4 * x + 2 * y + c

    def index(peer):
        return 4 * peer[0] + 2 * peer[1] + peer[2]

    def send(a, r):
        return pltpu.make_async_remote_copy(src_ref=part_refs[a].at[index(peers[r])], dst_ref=landed_refs[a].at[my_k, layers[a]],
                                            send_sem=send_sems.at[a, r], recv_sem=recv_sems.at[a, r],
                                            device_id=peers[r], device_id_type=MESH)

    def arrival(a, r):
        return pltpu.make_async_remote_copy(src_ref=part_refs[a].at[my_k], dst_ref=landed_refs[a].at[index(peers[r]), layers[a]],
                                            send_sem=send_sems.at[a, r], recv_sem=recv_sems.at[a, r],
                                            device_id=peers[r], device_id_type=MESH)

    def own(a):
        return pltpu.make_async_copy(part_refs[a].at[my_k], landed_refs[a].at[my_k, layers[a]], local_sems.at[a])

    def start():
        for a in range(n):
            own(a).start()
        for r in range(len(peers)):
            for a in range(n):
                send(a, r).start()

    def finish():
        for r in range(len(peers)):
            for a in range(n):
                arrival(a, r).wait_recv()
        for r in range(len(peers)):
            for a in range(n):
                send(a, r).wait_send()
        for a in range(n):
            own(a).wait()

    return start, finish


def _row_block(rows):
    return ROW_BLOCK if rows % ROW_BLOCK == 0 else rows


def _sum_leading(parts):
    n_part, shape = parts.shape[0], parts.shape[1:]
    rows, cols = shape[-2:]
    rb = _row_block(rows)

    def body(p_ref, o_ref):
        acc = p_ref[0].astype(F32)
        for k in range(1, n_part):
            acc = acc + p_ref[k].astype(F32)
        o_ref[...] = acc

    if len(shape) == 3:
        grid = (shape[0], rows // rb)
        in_spec = pl.BlockSpec((n_part, None, rb, cols), lambda l, i: (0, l, i, 0))
        out_spec = pl.BlockSpec((None, rb, cols), lambda l, i: (l, i, 0))
    else:
        grid = (rows // rb,)
        in_spec, out_spec = pl.BlockSpec((n_part, rb, cols), lambda i: (0, i, 0)), _rows(rb, cols)
    return pl.pallas_call(body, name="sum_leading", grid=grid, in_specs=[in_spec], out_specs=out_spec,
                          out_shape=jax.ShapeDtypeStruct(shape, F32),
                          compiler_params=_cparams(("parallel",) * len(grid)))(parts)


def _adamw(w, g, m, v):
    shape = w.shape
    rows, cols = shape[-2:]
    rb = _row_block(rows)

    def body(w_ref, g_ref, m_ref, v_ref, d_ref, nm_ref, nv_ref):
        gv = g_ref[...]
        nm = ADAM_B1 * m_ref[...] + (1.0 - ADAM_B1) * gv
        nv = ADAM_B2 * v_ref[...] + (1.0 - ADAM_B2) * jnp.square(gv)
        m_hat = nm / (1.0 - ADAM_B1 ** ADAM_STEP)
        v_hat = nv / (1.0 - ADAM_B2 ** ADAM_STEP)
        d_ref[...] = -ADAM_LR * (m_hat / (jnp.sqrt(v_hat) + ADAM_EPS) + ADAM_WD * w_ref[...])
        nm_ref[...] = nm
        nv_ref[...] = nv

    if len(shape) == 3:
        grid, spec = (shape[0], rows // rb), pl.BlockSpec((None, rb, cols), lambda l, i: (l, i, 0))
    else:
        grid, spec = (rows // rb,), _rows(rb, cols)
    out = jax.ShapeDtypeStruct(shape, F32)
    return pl.pallas_call(body, name="adamw", grid=grid, in_specs=[spec] * 4, out_specs=(spec,) * 3,
                          out_shape=(out,) * 3, compiler_params=_cparams(("parallel",) * len(grid)))(w, g, m, v)


def _rope_tables(positions):
    t_len = positions.shape[0]
    inv_freq = 1.0 / (ROPE_THETA ** (jnp.arange(0, QK_ROPE, 2, dtype=F32) / QK_ROPE))
    ang = positions.astype(F32)[:, None] * inv_freq
    c, s = jnp.cos(ang), jnp.sin(ang)
    one, zero = jnp.ones((t_len, QK_NOPE), F32), jnp.zeros((t_len, QK_NOPE), F32)
    cos = jnp.concatenate([one, c, c, one[:, :LANES - QK_HEAD]], axis=1)
    sin = jnp.concatenate([zero, -s, s, zero[:, :LANES - QK_HEAD]], axis=1)
    idx = jnp.arange(PAIR * HEAD_PAD)
    lane, head = idx % HEAD_PAD, idx // HEAD_PAD
    grp = jnp.where(lane < QK_NOPE, 0, jnp.where(lane < QK_HEAD, 1, 2)) + 3 * head
    val = jnp.where(lane < QK_NOPE, 1.0 / QK_NOPE, jnp.where(lane < QK_HEAD, 1.0 / QK_ROPE, 0.0))
    gm = jnp.where(grp[:, None] == grp[None, :], val[None, :], 0.0).astype(BF16)
    return {"cos": cos, "sin": sin, "gm": gm}


def _head_gain(g_nope, g_rope):
    one = jnp.concatenate([g_nope, g_rope, jnp.zeros((HEAD_PAD - QK_HEAD,), F32)])
    return jnp.concatenate([one] * PAIR).reshape(1, PAIR * HEAD_PAD)


def _head_pairs(w):
    return jnp.concatenate([w[k::PAIR] for k in range(PAIR)], axis=2)


def _padded_w_in(shards):
    width = shards.shape[2]
    zeros = jnp.zeros((shards.shape[1], QK_NOPE), shards.dtype)

    def natural(start, end):
        pieces = []
        for k in range(N_DEV):
            lo, hi = max(start, k * width), min(end, (k + 1) * width)
            if lo < hi:
                pieces.append(shards[k][:, lo - k * width:hi - k * width])
        return pieces

    o2, o3 = Q_LORA + KV_LORA, Q_LORA + KV_LORA + QK_ROPE
    return jnp.concatenate(natural(0, o2) + natural(o3, N_DEV * width) + [zeros] + natural(o2, o3)
                           + [zeros[:, :LANES - QK_HEAD]], axis=1)


def _w_in_grad_shards(d_in):
    o2, o3 = Q_LORA + KV_LORA, Q_LORA + KV_LORA + QK_ROPE
    width = (Z_XIN[1] + QK_ROPE) // N_DEV
    runs = [(0, o2, 0), (o2, o3, Z_KPE[0] + QK_NOPE), (o3, N_DEV * width, o2)]
    shards = []
    for k in range(N_DEV):
        pieces = []
        for start, end, at in runs:
            lo, hi = max(start, k * width), min(end, (k + 1) * width)
            if lo < hi:
                pieces.append(d_in[:, at + lo - start:at + hi - start])
        shards.append(pieces[0] if len(pieces) == 1 else jnp.concatenate(pieces, axis=1))
    return jnp.stack(shards)


def kernel(x, p, positions, g_mix, w_in, g_q_lat, w_uq, g_kv_lat, w_ukv, g_qn_nope, g_qn_rope, g_kn_nope, g_kn_rope, conv_w, g_out_attn, g_out_conv, w_o, g_mlp, w_up, w_down, g_ple, w_ple_gate, w_ple, loss_target, m_g_mix, m_w_in, m_g_q_lat, m_w_uq, m_g_kv_lat, m_w_ukv, m_g_qn_nope, m_g_qn_rope, m_g_kn_nope, m_g_kn_rope, m_conv_w, m_g_out_attn, m_g_out_conv, m_w_o, m_g_mlp, m_w_up, m_w_down, m_g_ple, m_w_ple_gate, m_w_ple, v_g_mix, v_w_in, v_g_q_lat, v_w_uq, v_g_kv_lat, v_w_ukv, v_g_qn_nope, v_g_qn_rope, v_g_kn_nope, v_g_kn_rope, v_conv_w, v_g_out_attn, v_g_out_conv, v_w_o, v_g_mlp, v_w_up, v_w_down, v_g_ple, v_w_ple_gate, v_w_ple):
    given = dict(locals())
    weights = {n: given[n] for n in WEIGHT_NAMES}
    gains = {n: given[n] for n in GAIN_NAMES}
    depth = w_in.shape[0]
    xs, target = x[0], loss_target[0]
    d_model = xs.shape[1]
    uq_cols = w_uq.shape[2]
    n_taps = conv_w.shape[1]

    mat_names = [n for n in SHARD_NAMES if n != "conv_w"]
    local = [weights[n].astype(BF16) for n in mat_names]
    local[1] = jnp.pad(local[1], ((0, 0), (0, 0), (0, HEAD_PAD - uq_cols)))
    local = dict(zip(mat_names, local))
    front_names = ("w_in", "w_uq", "w_ukv")
    first = _all_gather([local[n][0] for n in front_names] + [conv_w])
    conv_full = jnp.transpose(first[-1], (1, 2, 0, 3)).reshape(depth, n_taps, -1)
    tabs = _rope_tables(positions[0])

    def front_weights(layer, full):
        lw = {n: gains[n][layer].reshape(1, -1) for n in GAIN_NAMES}
        lw.update({"w_in": _padded_w_in(full["w_in"]), "w_uq": _head_pairs(full["w_uq"]),
                   "w_ukv": _head_pairs(full["w_ukv"]),
                   "conv_w": jnp.pad(conv_full[layer], ((0, HALO - n_taps), (0, 0))),
                   "g_qn": _head_gain(g_qn_nope[layer], g_qn_rope[layer]),
                   "g_kn": _head_gain(g_kn_nope[layer], g_kn_rope[layer])})
        return lw

    def rest_weights(full):
        return {"w_ple": full["w_ple"], "w_up": full["w_up"], "w_down": full["w_down"],
                "w_o": full["w_o"], "w_ple_gate": full["w_ple_gate"]}

    saved, layer_w = [], []
    cur = xs
    gathered = dict(zip(front_names, first[:-1]))
    mlp_names = ("w_up", "w_down")
    for layer in range(depth):
        w = front_weights(layer, gathered)
        z, qf, kf, kv, conv = _front_fwd(cur, w, tabs)
        lots = [[n for n in mat_names if n in mlp_names], [n for n in mat_names if n not in mlp_names]]
        behind_attn, behind_mlp = (lots[1], lots[0]) if layer == 0 else lots
        if layer + 1 == depth:
            behind_attn, behind_mlp = [], []
        wanted = [(n, 0) for n in mat_names if n not in front_names] if layer == 0 else []
        wanted += [(n, layer + 1) for n in behind_attn]
        attn, lse, got = _attn_fwd(qf, kf, kv, gather=[(local[n], at) for n, at in wanted])
        coming = {}
        for (n, at), g in zip(wanted, got):
            (gathered if at == layer else coming)[n] = g
        w.update(rest_weights(gathered))
        layer_w.append(w)
        x1 = _mix_out_fwd(cur, attn, conv, w)
        x2, got = _mlp_fwd(x1, w, gather=[(local[n], layer + 1) for n in behind_mlp])
        coming.update(zip(behind_mlp, got))
        gathered = coming
        x3 = _ple_fwd(x2, p[layer, 0], w)
        saved.append(dict(x=cur, z=z, qf=qf, kf=kf, kv=kv, conv=conv, attn=attn, lse=lse, x1=x1, x2=x2))
        cur = x3

    sq, dx = _loss_and_grad(cur, target)
    loss = lax.psum(0.5 / d_model * sq[0, 0], ("x", "y", "c"))

    landed = {n: lax.empty((N_DEV, depth) + weights[n].shape[1:], BF16) for n in SHARD_NAMES}
    gain_grads = [None] * depth
    late = {}
    for layer in reversed(range(depth)):
        w, s = layer_w[layer], saved[layer]
        pl_in = p[layer, 0]
        dx2, de, h3, dpre, dg_ple = _ple_bwd(dx, s["x2"], pl_in, w)
        dx1, r, da, h2, dg_mlp = _mlp_bwd(dx2, s["x1"], w)
        mixed, dattn, dconv, dg_oa, dg_oc = _mix_out_bwd(dx1, s["attn"], s["conv"], w)
        def exchange_of(sending):
            return ([v[0] for v in sending.values()], [landed[n] for n in sending], [v[1] for v in sending.values()])

        big = {"w_up": (_wgrad(h2, da, shard_cols=w_up.shape[2]), layer),
               "w_down": (_wgrad(r, dx2, shard_rows=w_down.shape[1]), layer)}
        small = {"w_o": (_wgrad(mixed, dx1, shard_rows=w_o.shape[1]), layer),
                 "w_ple_gate": (_wgrad(h3, dpre, shard_rows=w_ple_gate.shape[1]), layer),
                 "w_ple": (_wgrad(pl_in, de, shard_cols=w_ple.shape[2]), layer), **late}
        dqf, dkf, dkv, got = _attn_bwd(s["qf"], s["kf"], s["kv"], s["attn"], dattn, s["lse"], exchange_of(big))
        landed.update(zip(big, got))
        (dx0, dz, hb, qn, kvn, dqr, dkvr, dg_mix, dg_q, dg_kv, dg_qn, dg_kn, dcw), got = _front_bwd(
            s["x"], s["z"], dx1, dqf, dkf, dkv, dconv, w, tabs, exchange_of(small))
        landed.update(zip(small, got))
        late = {"w_in": (_w_in_grad_shards(_wgrad(hb, dz, out_dtype=F32)).astype(BF16), layer),
                "w_uq": (_wgrad(qn, dqr, shard_cols=HEAD_PAD)[..., :uq_cols], layer),
                "w_ukv": (_wgrad(kvn, dkvr, shard_cols=HEAD_PAD), layer),
                "conv_w": (jnp.transpose(dcw[:n_taps].reshape(n_taps, N_DEV, -1), (1, 0, 2)).astype(BF16), layer)}
        gain_grads[layer] = jnp.concatenate([
            dg_mix[0], dg_q[0], dg_kv[0], dg_qn[0, :QK_NOPE], dg_qn[0, QK_NOPE:QK_HEAD], dg_kn[0, :QK_NOPE],
            dg_kn[0, QK_NOPE:QK_HEAD], dg_oa[0], dg_oc[0], dg_mlp[0], dg_ple[0]])
        dx = dx0
    names = list(late)
    landed.update(zip(names, _scatter_exchange([late[n][0] for n in names], [landed[n] for n in names],
                                               [late[n][1] for n in names])))
    grads = {n: _sum_leading(landed[n]) for n in SHARD_NAMES}

    gg = jnp.stack(gain_grads)
    gg_rows = -(-gg.size // (HALO * LANES)) * HALO
    gg_pad = jnp.pad(gg.reshape(-1), (0, gg_rows * LANES - gg.size)).reshape(gg_rows, LANES)
    gg_sum = _sum_leading(_all_gather([gg_pad])[0]).reshape(-1)[:gg.size].reshape(gg.shape)
    off = 0
    for n in GAIN_NAMES:
        width = gains[n].shape[1]
        grads[n] = gg_sum[:, off:off + width]
        off += width

    deltas, new_m, new_v = {}, {}, {}
    for n in WEIGHT_NAMES:
        deltas[n], new_m[n], new_v[n] = _adamw(weights[n], grads[n], given["m_" + n], given["v_" + n])
    return (loss, dx[None], *[grads[n] for n in WEIGHT_NAMES], *[deltas[n] for n in WEIGHT_NAMES],
            *[new_m[n] for n in WEIGHT_NAMES], *[new_v[n] for n in WEIGHT_NAMES])
```

```python
import functools

import jax
import jax.numpy as jnp
from jax import lax
from jax.experimental import pallas as pl
from jax.experimental.pallas import tpu as pltpu

F32 = jnp.float32
BF16 = jnp.bfloat16
MESH = pl.DeviceIdType.MESH

N_HEADS = 8
QK_NOPE = 64
QK_ROPE = 32
QK_HEAD = QK_NOPE + QK_ROPE
V_HEAD = 64
HEAD_PAD = 128
PAIR = 2
ATTN_SCALE = QK_HEAD ** -0.5
Q_LORA = 384
KV_LORA = 256
CONV_WIDTH = 512
ATTN_WIDTH = N_HEADS * V_HEAD
ROPE_THETA = 10000.0
EPS = 1e-6
ADAM_LR, ADAM_B1, ADAM_B2, ADAM_EPS, ADAM_WD, ADAM_STEP = 0.001, 0.9, 0.999, 1e-08, 0.01, 10

Z_Q = (0, 384)
Z_KV = (384, 640)
Z_GB = (640, 1152)
Z_GC = (1152, 1664)
Z_XIN = (1664, 2176)
Z_KPE = (2176, 2304)
Z_COLS = 2304

N_DEV = 8
LANES = 128
V7X_VMEM_LIMIT = 52 * 1024 * 1024
TOKEN_TILE = 256
LIGHT_TOKEN_TILE = 512
ATTN_BLOCK = 256
ATTN_FWD_BLOCK = 512
ATTN_FWD_ROWS = 1024
ATTN_Q_SUB = 2
ATTN_KV_SUB = 16
ROW_BLOCK = 512
WGRAD_TOKENS = 2048
WGRAD_TILE = 1024
HALO = 8

GAIN_NAMES = ("g_mix", "g_q_lat", "g_kv_lat", "g_qn_nope", "g_qn_rope", "g_kn_nope", "g_kn_rope",
              "g_out_attn", "g_out_conv", "g_mlp", "g_ple")
SHARD_NAMES = ("w_in", "w_uq", "w_ukv", "conv_w", "w_o", "w_up", "w_down", "w_ple_gate", "w_ple")
WEIGHT_NAMES = ("g_mix", "w_in", "g_q_lat", "w_uq", "g_kv_lat", "w_ukv", "g_qn_nope", "g_qn_rope", "g_kn_nope",
                "g_kn_rope", "conv_w", "g_out_attn", "g_out_conv", "w_o", "g_mlp", "w_up", "w_down", "g_ple",
                "w_ple_gate", "w_ple")


def _cparams(semantics=None):
    return pltpu.CompilerParams(dimension_semantics=semantics, vmem_limit_bytes=V7X_VMEM_LIMIT)


def _mm(a, b):
    return jnp.dot(a, b, preferred_element_type=F32)


def _mm_nt(a, b):
    return lax.dot_general(a, b, (((1,), (1,)), ((), ())), preferred_element_type=F32)


def _mm_tn(a, b):
    return lax.dot_general(a, b, (((0,), (0,)), ((), ())), preferred_element_type=F32)


def _rms(x, g):
    r = lax.rsqrt(jnp.mean(x * x, axis=-1, keepdims=True) + EPS)
    return (x * r) * g


def _rms_bwd(x, g, dy):
    r = lax.rsqrt(jnp.mean(x * x, axis=-1, keepdims=True) + EPS)
    xh = x * r
    dg = jnp.sum(dy * xh, axis=0, keepdims=True)
    dyg = dy * g
    dx = r * (dyg - xh * jnp.mean(dyg * xh, axis=-1, keepdims=True))
    return dx, dg


def _group_mean(t, gm):
    hi = t.astype(BF16)
    lo = (t - hi.astype(F32)).astype(BF16)
    return _mm(hi, gm) + _mm(lo, gm)


def _swap_rope_halves(x, lane):
    half = QK_ROPE // 2
    swapped = jnp.where(lane < QK_NOPE + half, pltpu.roll(x, x.shape[1] - half, 1), pltpu.roll(x, half, 1))
    return jnp.where((lane >= QK_NOPE) & (lane < QK_HEAD), swapped, 0.0)


def _qk_fwd(x, g, cos, sin, gm, lane):
    r = lax.rsqrt(_group_mean(x * x, gm) + EPS)
    n = (x * r) * g
    return n * cos + _swap_rope_halves(n, lane) * sin


def _qk_bwd(x, g, dy, cos, sin, gm, lane):
    r = lax.rsqrt(_group_mean(x * x, gm) + EPS)
    xh = x * r
    dn = dy * cos + _swap_rope_halves(dy * sin, lane)
    dg = jnp.sum(dn * xh, axis=0, keepdims=True)
    dng = dn * g
    dx = r * (dng - xh * _group_mean(dng * xh, gm))
    return dx, dg


def _row_shards_joined(ref):
    n, rows, cols = ref.shape
    return ref[...].reshape(n * rows, cols)


def _rows(tm, n):
    return pl.BlockSpec((tm, n), lambda i: (i, 0))


def _whole(shape):
    zeros = (0,) * len(shape)
    return pl.BlockSpec(shape, lambda i: zeros)


def _operands(arrays):
    return list(arrays), [_whole(a.shape) for a in arrays]


def _accumulate(ref, first, value):
    @pl.when(first)
    def _():
        ref[...] = jnp.zeros_like(ref)
    ref[...] += value


def _front_fwd(x, w, tabs, gather=None):
    t_len, d = x.shape
    tm = min(TOKEN_TILE, t_len)
    hp = N_HEADS * HEAD_PAD

    def body(x_ref, gmix, win, gq, wuq, gkv, wukv, gqn, gkn, cw_ref, gm_ref, cos_ref, sin_ref,
             z_ref, qf_ref, kf_ref, kv_ref, conv_ref, ubuf):
        i = pl.program_id(0)
        h = _rms(x_ref[...], gmix[...])
        z = _mm(h.astype(BF16), win[...])
        z_ref[...] = z
        qnb = _rms(z[:, Z_Q[0]:Z_Q[1]], gq[...]).astype(BF16)
        kvb = _rms(z[:, Z_KV[0]:Z_KV[1]], gkv[...]).astype(BF16)
        kpe = z[:, Z_KPE[0]:Z_KPE[1]]
        kpe = jnp.concatenate([kpe] * PAIR, axis=1)
        cos, sin = (jnp.concatenate([t[...]] * PAIR, axis=1) for t in (cos_ref, sin_ref))
        gm = gm_ref[...]
        lane = lax.broadcasted_iota(jnp.int32, (tm, PAIR * HEAD_PAD), 1) & (HEAD_PAD - 1)
        for pr in range(N_HEADS // PAIR):
            sl = slice(pr * PAIR * HEAD_PAD, (pr + 1) * PAIR * HEAD_PAD)
            qf_ref[:, sl] = (_qk_fwd(_mm(qnb, wuq[pr]), gqn[...], cos, sin, gm, lane) * ATTN_SCALE).astype(BF16)
            kv = _mm(kvb, wukv[pr])
            kv_ref[:, sl] = jnp.where(lane < QK_NOPE, jnp.where(lane == 0, 1.0, 0.0), kv).astype(BF16)
            kf_ref[:, sl] = _qk_fwd(jnp.where(lane < QK_NOPE, kv, 0.0) + kpe, gkn[...], cos, sin, gm, lane).astype(BF16)
        u = z[:, Z_GC[0]:Z_GC[1]] * z[:, Z_XIN[0]:Z_XIN[1]]

        @pl.when(i == 0)
        def _():
            ubuf[0:HALO, :] = jnp.zeros((HALO, CONV_WIDTH), F32)
        ubuf[HALO:HALO + tm, :] = u
        cw = cw_ref[...]
        y = cw[0:1] * u + cw[1:2] * ubuf[pl.ds(HALO - 1, tm), :] + cw[2:3] * ubuf[pl.ds(HALO - 2, tm), :]
        conv_ref[...] = z[:, Z_GB[0]:Z_GB[1]] * y
        ubuf[0:HALO, :] = u[tm - HALO:tm, :]

    consts, const_specs = _operands([w["g_mix"], w["w_in"], w["g_q_lat"], w["w_uq"], w["g_kv_lat"], w["w_ukv"],
                                     w["g_qn"], w["g_kn"], w["conv_w"], tabs["gm"]])
    out_shape = (jax.ShapeDtypeStruct((t_len, Z_COLS), F32), jax.ShapeDtypeStruct((t_len, hp), BF16),
                 jax.ShapeDtypeStruct((t_len, hp), BF16), jax.ShapeDtypeStruct((t_len, hp), BF16),
                 jax.ShapeDtypeStruct((t_len, CONV_WIDTH), F32))
    return _call_hosting_gather("front_fwd", body, t_len // tm, [_rows(tm, d)] + const_specs + [_rows(tm, HEAD_PAD)] * 2,
                                tuple(_rows(tm, s.shape[1]) for s in out_shape), out_shape,
                                [pltpu.VMEM((tm + HALO, CONV_WIDTH), F32)], (x, *consts, tabs["cos"], tabs["sin"]), gather)


def _attn_fwd(qf, kf, kv, gather=None):
    t_len = qf.shape[0]
    blk = min(ATTN_FWD_BLOCK, t_len)
    bq = blk // ATTN_Q_SUB
    span = min(ATTN_FWD_ROWS, t_len)
    nb = t_len // span
    n_sub = span // bq
    chains = [(hh, a) for hh in range(2) for a in range(n_sub)]

    def body(q_ref, k_ref, kv_ref, o_ref, lse_ref):
        lane = lax.broadcasted_iota(jnp.int32, (bq, LANES), 1)
        row = lax.broadcasted_iota(jnp.int32, (bq, blk), 0)
        col = lax.broadcasted_iota(jnp.int32, (bq, blk), 1)

        def head_cols(hh):
            return slice(hh * HEAD_PAD, (hh + 1) * HEAD_PAD)

        def softmax_step(s, kvv, state, first_row=None):
            m, acc = state
            if first_row is not None:
                s = jnp.where(col <= row + first_row, s, -jnp.inf)
            m_new = jnp.maximum(m, jnp.max(s, axis=-1, keepdims=True))
            p = jnp.exp(s - m_new)
            acc = jnp.exp(m - m_new) * acc + _mm(p.astype(BF16), kvv)
            return m_new, acc

        def finish(state):
            m, acc = state
            l = jnp.sum(jnp.where(lane == 0, acc, 0.0), axis=-1, keepdims=True)
            return acc / l, jnp.broadcast_to(m + jnp.log(l), (bq, LANES))

        def qblock(i):
            rows = [pl.ds(a * bq, bq) for a in range(n_sub)]
            first = [i * span + a * bq for a in range(n_sub)]
            states = [(jnp.full((bq, 1), -jnp.inf, F32), jnp.zeros((bq, LANES), F32))] * len(chains)
            steps = []
            for j in range((first[-1] + bq - 1) // blk + 1):
                for n, (hh, a) in enumerate(chains):
                    if j * blk <= first[a] + bq - 1:
                        diagonal = (j + 1) * blk - 1 > first[a]
                        steps.append((n, j, first[a] - j * blk if diagonal else None))
            ahead = len(chains)
            scores = []
            for t in range(len(steps) + ahead):
                if t < len(steps):
                    n, j, _ = steps[t]
                    hh, a = chains[n]
                    scores.append(_mm_nt(q_ref[rows[a], head_cols(hh)], k_ref[pl.ds(j * blk, blk), head_cols(hh)]))
                if t >= ahead:
                    n, j, offset = steps[t - ahead]
                    states[n] = softmax_step(scores[t - ahead], kv_ref[pl.ds(j * blk, blk), head_cols(chains[n][0])],
                                             states[n], offset)
            for a in range(n_sub):
                (o0, lse0), (o1, lse1) = finish(states[chains.index((0, a))]), finish(states[chains.index((1, a))])
                o_ref[rows[a], :] = jnp.where(lane < V_HEAD, pltpu.roll(o0, V_HEAD, 1), o1)
                lse_ref[rows[a], head_cols(0)] = lse0
                lse_ref[rows[a], head_cols(1)] = lse1

        mine = pl.program_id(0) % nb
        for i in range(nb):
            pl.when(mine == i)(functools.partial(qblock, i))

    q_spec = pl.BlockSpec((span, 2 * HEAD_PAD), lambda g: (g % nb, g // nb))
    keys = pl.BlockSpec((t_len, 2 * HEAD_PAD), lambda g: (0, g // nb))
    o_spec = pl.BlockSpec((span, 2 * V_HEAD), lambda g: (g % nb, g // nb))
    out_shape = (jax.ShapeDtypeStruct((t_len, ATTN_WIDTH), F32), jax.ShapeDtypeStruct((t_len, N_HEADS * LANES), F32))
    (attn, lse), gathered = _call_hosting_gather("attn_fwd", body, N_HEADS // 2 * nb, [q_spec, keys, keys],
                                                 (o_spec, q_spec), out_shape, [], (qf, kf, kv), gather)
    return attn, lse, gathered


def _call_hosting_gather(name, body, n_steps, in_specs, out_specs, out_shape, scratch_shapes, args, gather):
    if not gather:
        return pl.pallas_call(body, name=name, grid=(n_steps,), in_specs=list(in_specs), out_specs=tuple(out_specs),
                              out_shape=tuple(out_shape), scratch_shapes=list(scratch_shapes),
                              compiler_params=_cparams(("arbitrary",)))(*args), ()
    n_in, n_out, n_g = len(in_specs), len(out_shape), len(gather)

    def hosting_body(*refs):
        ins, refs = refs[:n_in], refs[n_in:]
        x_refs = [r.at[layer] for r, (_, layer) in zip(refs[:n_g], gather)]
        outs, landing, sems, scratch = (refs[n_g:n_g + n_out], refs[n_g + n_out:2 * n_g + n_out],
                                        refs[2 * n_g + n_out:2 * n_g + n_out + 3], refs[2 * n_g + n_out + 3:])
        start, forward, finish = _gather_phases(x_refs, landing, *sems)
        step = pl.program_id(0)
        pl.when(step == 0)(start)
        pl.when(step == n_steps - 1)(forward)
        body(*ins, *outs, *scratch)
        pl.when(step == n_steps - 1)(finish)

    res = pl.pallas_call(hosting_body, name=name + "_gather", grid=(n_steps,), in_specs=list(in_specs) + _any_specs(n_g),
                         out_specs=tuple(out_specs) + tuple(_any_specs(n_g)),
                         out_shape=tuple(out_shape) + _gather_out_shape([s[layer] for s, layer in gather]),
                         scratch_shapes=_gather_semaphores(n_g) + list(scratch_shapes),
                         compiler_params=_cparams(("arbitrary",)))(*args, *[s for s, _ in gather])
    return res[:n_out], res[n_out:]


def _mix_out_fwd(x, attn, conv, w):
    t_len, d = x.shape
    tm = min(LIGHT_TOKEN_TILE, t_len)

    def body(x_ref, a_ref, c_ref, goa, goc, wo, x1_ref):
        mixed = jnp.concatenate([_rms(a_ref[...], goa[...]), _rms(c_ref[...], goc[...])], axis=1)
        x1_ref[...] = x_ref[...] + _mm(mixed.astype(BF16), _row_shards_joined(wo))

    consts, const_specs = _operands([w["g_out_attn"], w["g_out_conv"], w["w_o"]])
    return pl.pallas_call(body, name="mix_out_fwd", grid=(t_len // tm,),
                          in_specs=[_rows(tm, d), _rows(tm, ATTN_WIDTH), _rows(tm, CONV_WIDTH)] + const_specs,
                          out_specs=_rows(tm, d), out_shape=jax.ShapeDtypeStruct((t_len, d), F32),
                          compiler_params=_cparams(("parallel",)))(x, attn, conv, *consts)


def _mlp_fwd(x1, w, gather=None):
    t_len, d = x1.shape
    tm = min(TOKEN_TILE, t_len)

    def body(x_ref, g, wup, wdn, x2_ref):
        x1v = x_ref[...]
        hb = _rms(x1v, g[...]).astype(BF16)
        acc = x1v
        for k in range(N_DEV):
            a = jnp.maximum(_mm(hb, wup[k]), 0.0)
            acc = acc + _mm((a * a).astype(BF16), wdn[k])
        x2_ref[...] = acc

    consts, const_specs = _operands([w["g_mlp"], w["w_up"], w["w_down"]])
    (x2,), gathered = _call_hosting_gather("mlp_fwd", body, t_len // tm, [_rows(tm, d)] + const_specs, (_rows(tm, d),),
                                           (jax.ShapeDtypeStruct((t_len, d), F32),), [], (x1, *consts), gather)
    return x2, gathered


def _ple_fwd(x2, p, w):
    t_len, d = x2.shape
    tm = min(LIGHT_TOKEN_TILE, t_len)

    def body(x_ref, p_ref, g, wg, wp, x3_ref):
        x2v = x_ref[...]
        gate = jax.nn.sigmoid(_mm(_rms(x2v, g[...]).astype(BF16), _row_shards_joined(wg)))
        pb = p_ref[...].astype(BF16)
        e = jnp.concatenate([_mm(pb, wp[k]) for k in range(N_DEV)], axis=1)
        x3_ref[...] = x2v + gate * e

    consts, const_specs = _operands([w["g_ple"], w["w_ple_gate"], w["w_ple"]])
    return pl.pallas_call(body, name="ple_fwd", grid=(t_len // tm,),
                          in_specs=[_rows(tm, d), _rows(tm, p.shape[1])] + const_specs, out_specs=_rows(tm, d),
                          out_shape=jax.ShapeDtypeStruct((t_len, d), F32),
                          compiler_params=_cparams(("parallel",)))(x2, p, *consts)


def _loss_and_grad(y, target):
    t_len, d = y.shape
    tm = min(TOKEN_TILE, t_len)

    def body(y_ref, t_ref, sq_ref, dy_ref):
        err = y_ref[...] - t_ref[...]
        dy_ref[...] = err / d
        total = jnp.sum(jnp.sum(err * err, axis=0, keepdims=True), axis=1, keepdims=True)
        _accumulate(sq_ref, pl.program_id(0) == 0, jnp.broadcast_to(total, (HALO, LANES)))

    return pl.pallas_call(body, name="loss_grad", grid=(t_len // tm,), in_specs=[_rows(tm, d), _rows(tm, d)],
                          out_specs=(_whole((HALO, LANES)), _rows(tm, d)),
                          out_shape=(jax.ShapeDtypeStruct((HALO, LANES), F32), jax.ShapeDtypeStruct((t_len, d), F32)),
                          compiler_params=_cparams(("arbitrary",)))(y, target)


def _ple_bwd(dx3, x2, p, w):
    t_len, d = x2.shape
    tm = min(LIGHT_TOKEN_TILE, t_len)

    def body(dx3_ref, x_ref, p_ref, g, wg, wp, dx2_ref, de_ref, h3_ref, dpre_ref, dg_ref):
        x2v, dx3v = x_ref[...], dx3_ref[...]
        hb = _rms(x2v, g[...]).astype(BF16)
        h3_ref[...] = hb
        w_gate = _row_shards_joined(wg)
        gate = jax.nn.sigmoid(_mm(hb, w_gate))
        pb = p_ref[...].astype(BF16)
        e = jnp.concatenate([_mm(pb, wp[k]) for k in range(N_DEV)], axis=1)
        de_ref[...] = (dx3v * gate).astype(BF16)
        dpre = ((dx3v * e) * gate * (1.0 - gate)).astype(BF16)
        dpre_ref[...] = dpre
        dx, dg = _rms_bwd(x2v, g[...], _mm_nt(dpre, w_gate))
        dx2_ref[...] = dx3v + dx
        _accumulate(dg_ref, pl.program_id(0) == 0, dg)

    consts, const_specs = _operands([w["g_ple"], w["w_ple_gate"], w["w_ple"]])
    out_shape = (jax.ShapeDtypeStruct((t_len, d), F32), jax.ShapeDtypeStruct((t_len, d), BF16),
                 jax.ShapeDtypeStruct((t_len, d), BF16), jax.ShapeDtypeStruct((t_len, d), BF16),
                 jax.ShapeDtypeStruct((1, d), F32))
    return pl.pallas_call(body, name="ple_bwd", grid=(t_len // tm,),
                          in_specs=[_rows(tm, d), _rows(tm, d), _rows(tm, p.shape[1])] + const_specs,
                          out_specs=(_rows(tm, d),) * 4 + (_whole((1, d)),), out_shape=out_shape,
                          compiler_params=_cparams(("arbitrary",)))(dx3, x2, p, *consts)


def _mlp_bwd(dx2, x1, w):
    t_len, d = x1.shape
    tm = min(TOKEN_TILE, t_len)
    fc = w["w_up"].shape[2]
    ff = N_DEV * fc

    def body(dx2_ref, x_ref, g, wup, wdn, dx1_ref, r_ref, da_ref, h2_ref, dg_ref):
        x1v, dx2v = x_ref[...], dx2_ref[...]
        hb = _rms(x1v, g[...]).astype(BF16)
        h2_ref[...] = hb
        dxb = dx2v.astype(BF16)
        dh = jnp.zeros((tm, d), F32)
        for k in range(N_DEV):
            a = jnp.maximum(_mm(hb, wup[k]), 0.0)
            r_ref[:, k * fc:(k + 1) * fc] = (a * a).astype(BF16)
            da = (_mm_nt(dxb, wdn[k]) * (2.0 * a)).astype(BF16)
            da_ref[:, k * fc:(k + 1) * fc] = da
            dh = dh + _mm_nt(da, wup[k])
        dx, dg = _rms_bwd(x1v, g[...], dh)
        dx1_ref[...] = dx2v + dx
        _accumulate(dg_ref, pl.program_id(0) == 0, dg)

    consts, const_specs = _operands([w["g_mlp"], w["w_up"], w["w_down"]])
    out_shape = (jax.ShapeDtypeStruct((t_len, d), F32), jax.ShapeDtypeStruct((t_len, ff), BF16),
                 jax.ShapeDtypeStruct((t_len, ff), BF16), jax.ShapeDtypeStruct((t_len, d), BF16),
                 jax.ShapeDtypeStruct((1, d), F32))
    return pl.pallas_call(body, name="mlp_bwd", grid=(t_len // tm,), in_specs=[_rows(tm, d), _rows(tm, d)] + const_specs,
                          out_specs=(_rows(tm, d), _rows(tm, ff), _rows(tm, ff), _rows(tm, d), _whole((1, d))),
                          out_shape=out_shape, compiler_params=_cparams(("arbitrary",)))(dx2, x1, *consts)


def _mix_out_bwd(dx1, attn, conv, w):
    t_len, d = dx1.shape
    tm = min(LIGHT_TOKEN_TILE, t_len)

    def body(dx1_ref, a_ref, c_ref, goa, goc, wo, mixed_ref, da_ref, dc_ref, dgoa_ref, dgoc_ref):
        av, cv = a_ref[...], c_ref[...]
        mixed_ref[...] = jnp.concatenate([_rms(av, goa[...]), _rms(cv, goc[...])], axis=1).astype(BF16)
        dmixed = _mm_nt(dx1_ref[...].astype(BF16), _row_shards_joined(wo))
        da, dga = _rms_bwd(av, goa[...], dmixed[:, :ATTN_WIDTH])
        dc, dgc = _rms_bwd(cv, goc[...], dmixed[:, ATTN_WIDTH:])
        da_ref[...] = da
        dc_ref[...] = dc
        first = pl.program_id(0) == 0
        _accumulate(dgoa_ref, first, dga)
        _accumulate(dgoc_ref, first, dgc)

    consts, const_specs = _operands([w["g_out_attn"], w["g_out_conv"], w["w_o"]])
    out_shape = (jax.ShapeDtypeStruct((t_len, d), BF16), jax.ShapeDtypeStruct((t_len, ATTN_WIDTH), F32),
                 jax.ShapeDtypeStruct((t_len, CONV_WIDTH), F32), jax.ShapeDtypeStruct((1, ATTN_WIDTH), F32),
                 jax.ShapeDtypeStruct((1, CONV_WIDTH), F32))
    out_specs = (_rows(tm, d), _rows(tm, ATTN_WIDTH), _rows(tm, CONV_WIDTH), _whole((1, ATTN_WIDTH)),
                 _whole((1, CONV_WIDTH)))
    return pl.pallas_call(body, name="mix_out_bwd", grid=(t_len // tm,),
                          in_specs=[_rows(tm, d), _rows(tm, ATTN_WIDTH), _rows(tm, CONV_WIDTH)] + const_specs,
                          out_specs=out_specs, out_shape=out_shape,
                          compiler_params=_cparams(("arbitrary",)))(dx1, attn, conv, *consts)


def _attn_bwd(qf, kf, kv, o, do, lse, scatter):
    t_len = qf.shape[0]
    blk = min(ATTN_BLOCK, t_len)
    nb = t_len // blk
    n_sub = min(ATTN_KV_SUB, nb)
    reps = blk // LANES

    def body(q_ref, k_ref, kv_ref, o_ref, do_ref, lse_ref, dq_ref, dk_ref, dkv_ref, delta_ref, dob_ref):
        hd = pl.program_id(0)
        lane = lax.broadcasted_iota(jnp.int32, (blk, LANES), 1)
        even = (lane * 0 + hd % 2) == 0
        mine = jnp.where(lane < V_HEAD, 0, 1) == hd % 2
        row = lax.broadcasted_iota(jnp.int32, (blk, blk), 0)
        col = lax.broadcasted_iota(jnp.int32, (blk, blk), 1)
        dq_ref[...] = jnp.zeros_like(dq_ref)

        def prepare(i, carry):
            qs = pl.ds(pl.multiple_of(i * blk, blk), blk)
            dov = do_ref[qs, :]
            prod = jnp.where(mine, dov * o_ref[qs, :], 0.0)
            delta_ref[qs, :] = jnp.broadcast_to(jnp.sum(prod, axis=-1, keepdims=True), (blk, LANES))
            moved = jnp.where(even, pltpu.roll(dov, V_HEAD, 1), dov)
            dob_ref[qs, :] = jnp.where(lane >= V_HEAD, moved, 0.0).astype(BF16)
            return carry
        lax.fori_loop(0, nb, prepare, 0)

        def kvblock(jj, carry):
            base = jj * n_sub
            kss = [pl.ds(pl.multiple_of((base + a) * blk, blk), blk) for a in range(n_sub)]
            k = [k_ref[ks, :] for ks in kss]
            kvv = [kv_ref[ks, :] for ks in kss]

            def products(i):
                qs = pl.ds(pl.multiple_of(i * blk, blk), blk)
                q, dob = q_ref[qs, :], dob_ref[qs, :]
                return tuple((_mm_nt(q, k[a]), _mm_nt(dob, kvv[a])) for a in range(n_sub))

            def qstep(i, raw, accs, kinds):
                qs = pl.ds(pl.multiple_of(i * blk, blk), blk)
                q = q_ref[qs, :]
                dob = dob_ref[qs, :]
                lse_t = jnp.concatenate([lse_ref[qs, :]] * reps, axis=1)
                delta_t = jnp.concatenate([delta_ref[qs, :]] * reps, axis=1)
                new, dq_add = [], None
                for a in range(n_sub):
                    if kinds[a] is None:
                        new.append(accs[a])
                        continue
                    dk_acc, dv_acc = accs[a]
                    s, dp = raw[a]
                    if kinds[a]:
                        s = jnp.where(col <= row, s, -jnp.inf)
                    p = jnp.exp(s - lse_t)
                    ds = (p * (dp - delta_t)).astype(BF16)
                    new.append((dk_acc + _mm_tn(ds, q), dv_acc + _mm_tn(p.astype(BF16), dob)))
                    part = _mm(ds, k[a])
                    dq_add = part if dq_add is None else dq_add + part
                dq_ref[qs, :] += dq_add
                return tuple(new)

            zero = jnp.zeros((blk, LANES), F32)
            accs = ((zero, zero),) * n_sub
            for b in range(n_sub):
                accs = qstep(base + b, products(base + b), accs, tuple((a == b) if a <= b else None for a in range(n_sub)))

            def pipelined(i, carried):
                raw, acc = carried
                return products(jnp.minimum(i + 1, nb - 1)), qstep(i, raw, acc, (False,) * n_sub)

            first = base + n_sub
            _, accs = lax.fori_loop(first, nb, pipelined, (products(jnp.minimum(first, nb - 1)), accs))
            for a in range(n_sub):
                dk_ref[kss[a], :] = accs[a][0]
                dkv_ref[kss[a], :] = accs[a][1]
            return carry
        lax.fori_loop(0, nb // n_sub, kvblock, 0)

    head = pl.BlockSpec((t_len, HEAD_PAD), lambda h: (0, h))
    pair = pl.BlockSpec((t_len, 2 * V_HEAD), lambda h: (0, h // 2))
    out = jax.ShapeDtypeStruct((t_len, N_HEADS * HEAD_PAD), F32)
    vmem_scratch = [pltpu.VMEM((t_len, LANES), F32), pltpu.VMEM((t_len, LANES), BF16)]
    (dq, dk, dkv), landed = _call_hosting_scatter("attn_bwd", body, N_HEADS, [head, head, head, pair, pair, head],
                                                  (head, head, head), (out, out, out), vmem_scratch,
                                                  (qf, kf, kv, o, do, lse), scatter)
    return dq, dk, dkv, landed


def _call_hosting_scatter(name, body, n_steps, in_specs, out_specs, out_shape, scratch_shapes, args, scatter):
    parts, landed, layers = scatter
    n_in, n_out, n_sc = len(in_specs), len(out_shape), len(parts)

    def hosting_body(*refs):
        ins, part_refs, refs = refs[:n_in], refs[n_in:n_in + n_sc], refs[n_in + 2 * n_sc:]
        outs, landed_refs, sems, scratch = (refs[:n_out], refs[n_out:n_out + n_sc], refs[n_out + n_sc:n_out + n_sc + 3],
                                            refs[n_out + n_sc + 3:])
        start, finish = _scatter_phases(part_refs, landed_refs, *sems, layers)
        step = pl.program_id(0)
        pl.when(step == 0)(start)
        body(*ins, *outs, *scratch)
        pl.when(step == n_steps - 1)(finish)

    res = pl.pallas_call(hosting_body, name=name + "_scatter", grid=(n_steps,),
                         in_specs=list(in_specs) + _any_specs(2 * n_sc),
                         out_specs=tuple(out_specs) + tuple(_any_specs(n_sc)),
                         out_shape=tuple(out_shape) + _same_shapes(landed),
                         scratch_shapes=_scatter_semaphores(n_sc) + list(scratch_shapes),
                         input_output_aliases={n_in + n_sc + a: n_out + a for a in range(n_sc)},
                         compiler_params=_cparams(("arbitrary",)))(*args, *parts, *landed)
    return res[:n_out], res[n_out:]


def _front_bwd(x, z, dx1, dqf, dkf, dkv_in, dconv, w, tabs, scatter):
    t_len, d = x.shape
    tm = min(TOKEN_TILE, t_len)
    nt = t_len // tm
    hb_per_tile = tm // HALO
    n_halo = t_len // HALO
    hp = N_HEADS * HEAD_PAD

    def body(x_ref, z_ref, zp_ref, zn_ref, dx1_ref, dqf_ref, dkf_ref, dkv_ref, dc_ref, dcn_ref,
             gmix, win, gq, wuq, gkv, wukv, gqn, gkn, cw_ref, gm_ref, cos_ref, sin_ref,
             dx_ref, dz_ref, h_ref, qn_ref, kvn_ref, dqr_ref, dkvr_ref,
             dgmix_ref, dgq_ref, dgkv_ref, dgqn_ref, dgkn_ref, dcw_ref, ubuf, dybuf):
        i = pl.program_id(0)
        first = i == 0
        xv, zv = x_ref[...], z_ref[...]
        hb = _rms(xv, gmix[...]).astype(BF16)
        h_ref[...] = hb
        zq, zkv = zv[:, Z_Q[0]:Z_Q[1]], zv[:, Z_KV[0]:Z_KV[1]]
        qnb = _rms(zq, gq[...]).astype(BF16)
        qn_ref[...] = qnb
        kvb = _rms(zkv, gkv[...]).astype(BF16)
        kvn_ref[...] = kvb
        kpe = zv[:, Z_KPE[0]:Z_KPE[1]]
        kpe = jnp.concatenate([kpe] * PAIR, axis=1)
        cos, sin = (jnp.concatenate([t[...]] * PAIR, axis=1) for t in (cos_ref, sin_ref))
        gm = gm_ref[...]
        width = PAIR * HEAD_PAD
        lane = lax.broadcasted_iota(jnp.int32, (tm, width), 1) & (HEAD_PAD - 1)
        is_nope = lane < QK_NOPE
        is_rope = (lane >= QK_NOPE) & (lane < QK_HEAD)
        dkpe = jnp.zeros((tm, width), F32)
        dgqn = jnp.zeros((1, width), F32)
        dgkn = jnp.zeros((1, width), F32)
        dqn = jnp.zeros((tm, Q_LORA), F32)
        dkvn = jnp.zeros((tm, KV_LORA), F32)
        for pr in range(N_HEADS // PAIR):
            sl = slice(pr * width, (pr + 1) * width)
            dxq, dg = _qk_bwd(_mm(qnb, wuq[pr]), gqn[...], dqf_ref[:, sl] * ATTN_SCALE, cos, sin, gm, lane)
            dxq = dxq.astype(BF16)
            dqr_ref[:, sl] = dxq
            dqn = dqn + _mm_nt(dxq, wuq[pr])
            dgqn = dgqn + dg
            k_raw = jnp.where(is_nope, _mm(kvb, wukv[pr]), 0.0) + kpe
            dxk, dg = _qk_bwd(k_raw, gkn[...], dkf_ref[:, sl], cos, sin, gm, lane)
            dkv = jnp.where(is_nope, dxk, dkv_ref[:, sl]).astype(BF16)
            dkvr_ref[:, sl] = dkv
            dkvn = dkvn + _mm_nt(dkv, wukv[pr])
            dkpe = dkpe + jnp.where(is_rope, dxk, 0.0)
            dgkn = dgkn + dg
        dkpe = dkpe[:, :HEAD_PAD] + dkpe[:, HEAD_PAD:]
        _accumulate(dgqn_ref, first, dgqn[:, :HEAD_PAD] + dgqn[:, HEAD_PAD:])
        _accumulate(dgkn_ref, first, dgkn[:, :HEAD_PAD] + dgkn[:, HEAD_PAD:])
        dzq, dg = _rms_bwd(zq, gq[...], dqn)
        _accumulate(dgq_ref, first, dg)
        dzkv, dg = _rms_bwd(zkv, gkv[...], dkvn)
        _accumulate(dgkv_ref, first, dg)

        gb, gc, xin = zv[:, Z_GB[0]:Z_GB[1]], zv[:, Z_GC[0]:Z_GC[1]], zv[:, Z_XIN[0]:Z_XIN[1]]
        u = gc * xin
        dcv = dc_ref[...]
        dy = dcv * gb
        zp, zn = zp_ref[...], zn_ref[...]
        ubuf[0:HALO, :] = (zp[:, Z_GC[0]:Z_GC[1]] * zp[:, Z_XIN[0]:Z_XIN[1]]) * jnp.where(first, 0.0, 1.0)
        ubuf[HALO:HALO + tm, :] = u
        dybuf[0:tm, :] = dy
        dybuf[tm:tm + HALO, :] = (dcn_ref[...] * zn[:, Z_GB[0]:Z_GB[1]]) * jnp.where(i == nt - 1, 0.0, 1.0)
        cw = cw_ref[...]
        u1, u2 = ubuf[pl.ds(HALO - 1, tm), :], ubuf[pl.ds(HALO - 2, tm), :]
        y = cw[0:1] * u + cw[1:2] * u1 + cw[2:3] * u2
        du = cw[0:1] * dy + cw[1:2] * dybuf[pl.ds(1, tm), :] + cw[2:3] * dybuf[pl.ds(2, tm), :]
        dcw = jnp.concatenate([jnp.sum(dy * u, axis=0, keepdims=True), jnp.sum(dy * u1, axis=0, keepdims=True),
                               jnp.sum(dy * u2, axis=0, keepdims=True), jnp.zeros((HALO - 3, CONV_WIDTH), F32)], axis=0)
        _accumulate(dcw_ref, first, dcw)

        dz_ref[:, Z_Q[0]:Z_Q[1]] = dzq.astype(BF16)
        dz_ref[:, Z_KV[0]:Z_KV[1]] = dzkv.astype(BF16)
        dz_ref[:, Z_GB[0]:Z_GB[1]] = (dcv * y).astype(BF16)
        dz_ref[:, Z_GC[0]:Z_GC[1]] = (du * xin).astype(BF16)
        dz_ref[:, Z_XIN[0]:Z_XIN[1]] = (du * gc).astype(BF16)
        dz_ref[:, Z_KPE[0]:Z_KPE[1]] = dkpe.astype(BF16)
        dx, dg = _rms_bwd(xv, gmix[...], _mm_nt(dz_ref[...], win[...]))
        dx_ref[...] = dx1_ref[...] + dx
        _accumulate(dgmix_ref, first, dg)

    prev_halo = lambda n: pl.BlockSpec((HALO, n), lambda i: (jnp.maximum(i * hb_per_tile - 1, 0), 0))
    next_halo = lambda n: pl.BlockSpec((HALO, n), lambda i: (jnp.minimum((i + 1) * hb_per_tile, n_halo - 1), 0))
    consts, const_specs = _operands([w["g_mix"], w["w_in"], w["g_q_lat"], w["w_uq"], w["g_kv_lat"], w["w_ukv"],
                                     w["g_qn"], w["g_kn"], w["conv_w"], tabs["gm"]])
    in_specs = ([_rows(tm, d), _rows(tm, Z_COLS), prev_halo(Z_COLS), next_halo(Z_COLS), _rows(tm, d), _rows(tm, hp),
                 _rows(tm, hp), _rows(tm, hp), _rows(tm, CONV_WIDTH), next_halo(CONV_WIDTH)]
                + const_specs + [_rows(tm, HEAD_PAD)] * 2)
    out_shape = (jax.ShapeDtypeStruct((t_len, d), F32), jax.ShapeDtypeStruct((t_len, Z_COLS), BF16),
                 jax.ShapeDtypeStruct((t_len, d), BF16), jax.ShapeDtypeStruct((t_len, Q_LORA), BF16),
                 jax.ShapeDtypeStruct((t_len, KV_LORA), BF16), jax.ShapeDtypeStruct((t_len, hp), BF16),
                 jax.ShapeDtypeStruct((t_len, hp), BF16),
                 jax.ShapeDtypeStruct((1, d), F32), jax.ShapeDtypeStruct((1, Q_LORA), F32),
                 jax.ShapeDtypeStruct((1, KV_LORA), F32), jax.ShapeDtypeStruct((1, LANES), F32),
                 jax.ShapeDtypeStruct((1, LANES), F32), jax.ShapeDtypeStruct((HALO, CONV_WIDTH), F32))
    out_specs = tuple(_rows(tm, s.shape[1]) for s in out_shape[:7]) + tuple(_whole(s.shape) for s in out_shape[7:])
    conv_scratch = [pltpu.VMEM((tm + HALO, CONV_WIDTH), F32), pltpu.VMEM((tm + HALO, CONV_WIDTH), F32)]
    args = (x, z, z, z, dx1, dqf, dkf, dkv_in, dconv, dconv, *consts, tabs["cos"], tabs["sin"])
    return _call_hosting_scatter("front_bwd", body, nt, in_specs, out_specs, out_shape, conv_scratch, args, scatter)


def _wgrad(a, b, shard_cols=None, shard_rows=None, out_dtype=BF16):
    t_len, kk = a.shape
    nn = b.shape[1]
    tk = min(kk, WGRAD_TILE)
    tn = next(c for c in range(min(nn, WGRAD_TILE), 0, -LANES) if nn % c == 0 and c % (shard_cols or LANES) == 0)
    tt = min(t_len, WGRAD_TOKENS)
    nt = t_len // tt
    per_block = tn // shard_cols if shard_cols else tk // shard_rows if shard_rows else 1

    def body(a_ref, b_ref, o_ref, acc):
        t = pl.program_id(2)

        @pl.when(t == 0)
        def _():
            acc[...] = jnp.zeros_like(acc)
        acc[...] += _mm_tn(a_ref[...].astype(BF16), b_ref[...].astype(BF16))

        @pl.when(t == nt - 1)
        def _():
            if shard_cols:
                for s in range(per_block):
                    o_ref[s] = acc[:, s * shard_cols:(s + 1) * shard_cols].astype(out_dtype)
            elif shard_rows:
                for s in range(per_block):
                    o_ref[s] = acc[s * shard_rows:(s + 1) * shard_rows, :].astype(out_dtype)
            else:
                o_ref[...] = acc[...].astype(out_dtype)

    if shard_cols:
        out_shape = jax.ShapeDtypeStruct((nn // shard_cols, kk, shard_cols), out_dtype)
        out_spec = pl.BlockSpec((per_block, tk, shard_cols), lambda i, j, t: (j, i, 0))
    elif shard_rows:
        out_shape = jax.ShapeDtypeStruct((kk // shard_rows, shard_rows, nn), out_dtype)
        out_spec = pl.BlockSpec((per_block, shard_rows, tn), lambda i, j, t: (i, 0, j))
    else:
        out_shape = jax.ShapeDtypeStruct((kk, nn), out_dtype)
        out_spec = pl.BlockSpec((tk, tn), lambda i, j, t: (i, j))
    return pl.pallas_call(body, name="wgrad", grid=(kk // tk, nn // tn, nt),
                          in_specs=[pl.BlockSpec((tt, tk), lambda i, j, t: (t, i)),
                                    pl.BlockSpec((tt, tn), lambda i, j, t: (t, j))],
                          out_specs=out_spec, out_shape=out_shape, scratch_shapes=[pltpu.VMEM((tk, tn), F32)],
                          compiler_params=_cparams(("parallel", "parallel", "arbitrary")))(a, b)


def _my_place():
    return lax.axis_index("x"), lax.axis_index("y"), lax.axis_index("c")


def _any_specs(n):
    return [pl.BlockSpec(memory_space=pl.ANY)] * n


def _all_gather(blocks):
    n = len(blocks)

    def body(*refs):
        start, forward, finish = _gather_phases(refs[:n], refs[n:2 * n], *refs[2 * n:])
        start()
        forward()
        finish()

    return pl.pallas_call(body, name="all_gather", out_shape=_gather_out_shape(blocks), in_specs=_any_specs(n),
                          out_specs=tuple(_any_specs(n)), scratch_shapes=_gather_semaphores(n))(*blocks)


def _gather_out_shape(blocks):
    return tuple(jax.ShapeDtypeStruct((N_DEV,) + b.shape, b.dtype) for b in blocks)


def _gather_semaphores(n):
    return [pltpu.SemaphoreType.DMA((n, 7)), pltpu.SemaphoreType.DMA((n, 7)), pltpu.SemaphoreType.DMA((n,))]


def _gather_phases(x_refs, out_refs, send_sems, recv_sems, local_sems):
    n = len(x_refs)
    x, y, c = _my_place()
    me, sibling = (x, y, c), (x, y, 1 - c)
    chips = [(1 - x, y), (x, 1 - y), (1 - x, 1 - y)]

    def slot(a, px, py, pc):
        return out_refs[a].at[4 * px + 2 * py + pc]

    def copy(a, k, blk, to, src=None):
        return pltpu.make_async_remote_copy(src_ref=slot(a, *blk) if src is None else src, dst_ref=slot(a, *blk),
                                            send_sem=send_sems.at[a, k], recv_sem=recv_sems.at[a, k],
                                            device_id=to, device_id_type=MESH)

    def own(a):
        return pltpu.make_async_copy(x_refs[a], slot(a, *me), local_sems.at[a])

    def first_hop(a):
        return [copy(a, 0, me, sibling, src=x_refs[a])] + [copy(a, 1 + j, me, (*chip, c), src=x_refs[a])
                                                           for j, chip in enumerate(chips)]

    def passed_on(a):
        return [copy(a, 4 + j, (*chip, c), sibling) for j, chip in enumerate(chips)]

    def start():
        for a in range(n):
            own(a).start()
        for a in range(n):
            for cp in first_hop(a):
                cp.start()

    def forward():
        for j, chip in enumerate(chips):
            for a in range(n):
                copy(a, 1 + j, (*chip, c), me).wait_recv()
                passed_on(a)[j].start()

    def finish():
        for a in range(n):
            copy(a, 0, sibling, me).wait_recv()
        for j, chip in enumerate(chips):
            for a in range(n):
                copy(a, 4 + j, (*chip, 1 - c), me).wait_recv()
        for a in range(n):
            for cp in first_hop(a) + passed_on(a):
                cp.wait_send()
            own(a).wait()

    return start, forward, finish


def _scatter_exchange(parts, landed, layers):
    n = len(parts)

    def body(*refs):
        start, finish = _scatter_phases(refs[:n], refs[2 * n:3 * n], *refs[3 * n:], layers)
        start()
        finish()

    return pl.pallas_call(body, name="scatter_exchange", out_shape=_same_shapes(landed), in_specs=_any_specs(2 * n),
                          out_specs=tuple(_any_specs(n)), scratch_shapes=_scatter_semaphores(n),
                          input_output_aliases={n + a: a for a in range(n)})(*parts, *landed)


def _same_shapes(arrays):
    return tuple(jax.ShapeDtypeStruct(a.shape, a.dtype) for a in arrays)


def _scatter_semaphores(n):
    return [pltpu.SemaphoreType.DMA((n, N_DEV - 1)), pltpu.SemaphoreType.DMA((n, N_DEV - 1)), pltpu.SemaphoreType.DMA((n,))]


def _scatter_phases(part_refs, landed_refs, send_sems, recv_sems, local_sems, layers):
    n = len(part_refs)
    x, y, c = _my_place()
    flips = [(0, 0, 1), (1, 0, 0), (0, 1, 0), (1, 1, 0), (1, 0, 1), (0, 1, 1), (1, 1, 1)]
    peers = [((1 - x) if fx else x, (1 - y) if fy else y, (1 - c) if fc else c) for fx, fy, fc in flips]
    my_k = 4 * x + 2 * y + c

    def index(peer):
        return 4 * peer[0] + 2 * peer[1] + peer[2]

    def send(a, r):
        return pltpu.make_async_remote_copy(src_ref=part_refs[a].at[index(peers[r])], dst_ref=landed_refs[a].at[my_k, layers[a]],
                                            send_sem=send_sems.at[a, r], recv_sem=recv_sems.at[a, r],
                                            device_id=peers[r], device_id_type=MESH)

    def arrival(a, r):
        return pltpu.make_async_remote_copy(src_ref=part_refs[a].at[my_k], dst_ref=landed_refs[a].at[index(peers[r]), layers[a]],
                                            send_sem=send_sems.at[a, r], recv_sem=recv_sems.at[a, r],
                                            device_id=peers[r], device_id_type=MESH)

    def own(a):
        return pltpu.make_async_copy(part_refs[a].at[my_k], landed_refs[a].at[my_k, layers[a]], local_sems.at[a])

    def start():
        for a in range(n):
            own(a).start()
        for r in range(len(peers)):
            for a in range(n):
                send(a, r).start()

    def finish():
        for r in range(len(peers)):
            for a in range(n):
                arrival(a, r).wait_recv()
        for r in range(len(peers)):
            for a in range(n):
                send(a, r).wait_send()
        for a in range(n):
            own(a).wait()

    return start, finish


def _row_block(rows):
    return ROW_BLOCK if rows % ROW_BLOCK == 0 else rows


def _sum_leading(parts):
    n_part, shape = parts.shape[0], parts.shape[1:]
    rows, cols = shape[-2:]
    rb = _row_block(rows)

    def body(p_ref, o_ref):
        acc = p_ref[0].astype(F32)
        for k in range(1, n_part):
            acc = acc + p_ref[k].astype(F32)
        o_ref[...] = acc

    if len(shape) == 3:
        grid = (shape[0], rows // rb)
        in_spec = pl.BlockSpec((n_part, None, rb, cols), lambda l, i: (0, l, i, 0))
        out_spec = pl.BlockSpec((None, rb, cols), lambda l, i: (l, i, 0))
    else:
        grid = (rows // rb,)
        in_spec, out_spec = pl.BlockSpec((n_part, rb, cols), lambda i: (0, i, 0)), _rows(rb, cols)
    return pl.pallas_call(body, name="sum_leading", grid=grid, in_specs=[in_spec], out_specs=out_spec,
                          out_shape=jax.ShapeDtypeStruct(shape, F32),
                          compiler_params=_cparams(("parallel",) * len(grid)))(parts)


def _adamw(w, g, m, v):
    shape = w.shape
    rows, cols = shape[-2:]
    rb = _row_block(rows)

    def body(w_ref, g_ref, m_ref, v_ref, d_ref, nm_ref, nv_ref):
        gv = g_ref[...]
        nm = ADAM_B1 * m_ref[...] + (1.0 - ADAM_B1) * gv
        nv = ADAM_B2 * v_ref[...] + (1.0 - ADAM_B2) * jnp.square(gv)
        m_hat = nm / (1.0 - ADAM_B1 ** ADAM_STEP)
        v_hat = nv / (1.0 - ADAM_B2 ** ADAM_STEP)
        d_ref[...] = -ADAM_LR * (m_hat / (jnp.sqrt(v_hat) + ADAM_EPS) + ADAM_WD * w_ref[...])
        nm_ref[...] = nm
        nv_ref[...] = nv

    if len(shape) == 3:
        grid, spec = (shape[0], rows // rb), pl.BlockSpec((None, rb, cols), lambda l, i: (l, i, 0))
    else:
        grid, spec = (rows // rb,), _rows(rb, cols)
    out = jax.ShapeDtypeStruct(shape, F32)
    return pl.pallas_call(body, name="adamw", grid=grid, in_specs=[spec] * 4, out_specs=(spec,) * 3,
                          out_shape=(out,) * 3, compiler_params=_cparams(("parallel",) * len(grid)))(w, g, m, v)


def _rope_tables(positions):
    t_len = positions.shape[0]
    inv_freq = 1.0 / (ROPE_THETA ** (jnp.arange(0, QK_ROPE, 2, dtype=F32) / QK_ROPE))
    ang = positions.astype(F32)[:, None] * inv_freq
    c, s = jnp.cos(ang), jnp.sin(ang)
    one, zero = jnp.ones((t_len, QK_NOPE), F32), jnp.zeros((t_len, QK_NOPE), F32)
    cos = jnp.concatenate([one, c, c, one[:, :LANES - QK_HEAD]], axis=1)
    sin = jnp.concatenate([zero, -s, s, zero[:, :LANES - QK_HEAD]], axis=1)
    idx = jnp.arange(PAIR * HEAD_PAD)
    lane, head = idx % HEAD_PAD, idx // HEAD_PAD
    grp = jnp.where(lane < QK_NOPE, 0, jnp.where(lane < QK_HEAD, 1, 2)) + 3 * head
    val = jnp.where(lane < QK_NOPE, 1.0 / QK_NOPE, jnp.where(lane < QK_HEAD, 1.0 / QK_ROPE, 0.0))
    gm = jnp.where(grp[:, None] == grp[None, :], val[None, :], 0.0).astype(BF16)
    return {"cos": cos, "sin": sin, "gm": gm}


def _head_gain(g_nope, g_rope):
    one = jnp.concatenate([g_nope, g_rope, jnp.zeros((HEAD_PAD - QK_HEAD,), F32)])
    return jnp.concatenate([one] * PAIR).reshape(1, PAIR * HEAD_PAD)


def _head_pairs(w):
    return jnp.concatenate([w[k::PAIR] for k in range(PAIR)], axis=2)


def _padded_w_in(shards):
    width = shards.shape[2]
    zeros = jnp.zeros((shards.shape[1], QK_NOPE), shards.dtype)

    def natural(start, end):
        pieces = []
        for k in range(N_DEV):
            lo, hi = max(start, k * width), min(end, (k + 1) * width)
            if lo < hi:
                pieces.append(shards[k][:, lo - k * width:hi - k * width])
        return pieces

    o2, o3 = Q_LORA + KV_LORA, Q_LORA + KV_LORA + QK_ROPE
    return jnp.concatenate(natural(0, o2) + natural(o3, N_DEV * width) + [zeros] + natural(o2, o3)
                           + [zeros[:, :LANES - QK_HEAD]], axis=1)


def _w_in_grad_shards(d_in):
    o2, o3 = Q_LORA + KV_LORA, Q_LORA + KV_LORA + QK_ROPE
    width = (Z_XIN[1] + QK_ROPE) // N_DEV
    runs = [(0, o2, 0), (o2, o3, Z_KPE[0] + QK_NOPE), (o3, N_DEV * width, o2)]
    shards = []
    for k in range(N_DEV):
        pieces = []
        for start, end, at in runs:
            lo, hi = max(start, k * width), min(end, (k + 1) * width)
            if lo < hi:
                pieces.append(d_in[:, at + lo - start:at + hi - start])
        shards.append(pieces[0] if len(pieces) == 1 else jnp.concatenate(pieces, axis=1))
    return jnp.stack(shards)


def kernel(x, p, positions, g_mix, w_in, g_q_lat, w_uq, g_kv_lat, w_ukv, g_qn_nope, g_qn_rope, g_kn_nope, g_kn_rope, conv_w, g_out_attn, g_out_conv, w_o, g_mlp, w_up, w_down, g_ple, w_ple_gate, w_ple, loss_target, m_g_mix, m_w_in, m_g_q_lat, m_w_uq, m_g_kv_lat, m_w_ukv, m_g_qn_nope, m_g_qn_rope, m_g_kn_nope, m_g_kn_rope, m_conv_w, m_g_out_attn, m_g_out_conv, m_w_o, m_g_mlp, m_w_up, m_w_down, m_g_ple, m_w_ple_gate, m_w_ple, v_g_mix, v_w_in, v_g_q_lat, v_w_uq, v_g_kv_lat, v_w_ukv, v_g_qn_nope, v_g_qn_rope, v_g_kn_nope, v_g_kn_rope, v_conv_w, v_g_out_attn, v_g_out_conv, v_w_o, v_g_mlp, v_w_up, v_w_down, v_g_ple, v_w_ple_gate, v_w_ple):
    given = dict(locals())
    weights = {n: given[n] for n in WEIGHT_NAMES}
    gains = {n: given[n] for n in GAIN_NAMES}
    depth = w_in.shape[0]
    xs, target = x[0], loss_target[0]
    d_model = xs.shape[1]
    uq_cols = w_uq.shape[2]
    n_taps = conv_w.shape[1]

    mat_names = [n for n in SHARD_NAMES if n != "conv_w"]
    local = [weights[n].astype(BF16) for n in mat_names]
    local[1] = jnp.pad(local[1], ((0, 0), (0, 0), (0, HEAD_PAD - uq_cols)))
    local = dict(zip(mat_names, local))
    front_names = ("w_in", "w_uq", "w_ukv")
    first = _all_gather([local[n][0] for n in front_names] + [conv_w])
    conv_full = jnp.transpose(first[-1], (1, 2, 0, 3)).reshape(depth, n_taps, -1)
    tabs = _rope_tables(positions[0])

    def front_weights(layer, full):
        lw = {n: gains[n][layer].reshape(1, -1) for n in GAIN_NAMES}
        lw.update({"w_in": _padded_w_in(full["w_in"]), "w_uq": _head_pairs(full["w_uq"]),
                   "w_ukv": _head_pairs(full["w_ukv"]),
                   "conv_w": jnp.pad(conv_full[layer], ((0, HALO - n_taps), (0, 0))),
                   "g_qn": _head_gain(g_qn_nope[layer], g_qn_rope[layer]),
                   "g_kn": _head_gain(g_kn_nope[layer], g_kn_rope[layer])})
        return lw

    def rest_weights(full):
        return {"w_ple": full["w_ple"], "w_up": full["w_up"], "w_down": full["w_down"],
                "w_o": full["w_o"], "w_ple_gate": full["w_ple_gate"]}

    saved, layer_w = [], []
    cur = xs
    gathered = dict(zip(front_names, first[:-1]))
    mlp_names = ("w_up", "w_down")
    for layer in range(depth):
        w = front_weights(layer, gathered)
        own_small = [n for n in mat_names if n not in front_names + mlp_names] if layer == 0 else []
        (z, qf, kf, kv, conv), got = _front_fwd(cur, w, tabs, gather=[(local[n], 0) for n in own_small])
        gathered.update(zip(own_small, got))
        lots = [[n for n in mat_names if n in mlp_names], [n for n in mat_names if n not in mlp_names]]
        behind_attn, behind_mlp = (lots[1], lots[0]) if layer == 0 else lots
        if layer + 1 == depth:
            behind_attn, behind_mlp = [], []
        wanted = [(n, 0) for n in mlp_names] if layer == 0 else []
        wanted += [(n, layer + 1) for n in behind_attn]
        attn, lse, got = _attn_fwd(qf, kf, kv, gather=[(local[n], at) for n, at in wanted])
        coming = {}
        for (n, at), g in zip(wanted, got):
            (gathered if at == layer else coming)[n] = g
        w.update(rest_weights(gathered))
        layer_w.append(w)
        x1 = _mix_out_fwd(cur, attn, conv, w)
        x2, got = _mlp_fwd(x1, w, gather=[(local[n], layer + 1) for n in behind_mlp])
        coming.update(zip(behind_mlp, got))
        gathered = coming
        x3 = _ple_fwd(x2, p[layer, 0], w)
        saved.append(dict(x=cur, z=z, qf=qf, kf=kf, kv=kv, conv=conv, attn=attn, lse=lse, x1=x1, x2=x2))
        cur = x3

    sq, dx = _loss_and_grad(cur, target)
    loss = lax.psum(0.5 / d_model * sq[0, 0], ("x", "y", "c"))

    landed = {n: lax.empty((N_DEV, depth) + weights[n].shape[1:], BF16) for n in SHARD_NAMES}
    gain_grads = [None] * depth
    late = {}
    for layer in reversed(range(depth)):
        w, s = layer_w[layer], saved[layer]
        pl_in = p[layer, 0]
        dx2, de, h3, dpre, dg_ple = _ple_bwd(dx, s["x2"], pl_in, w)
        dx1, r, da, h2, dg_mlp = _mlp_bwd(dx2, s["x1"], w)
        mixed, dattn, dconv, dg_oa, dg_oc = _mix_out_bwd(dx1, s["attn"], s["conv"], w)
        def exchange_of(sending):
            return ([v[0] for v in sending.values()], [landed[n] for n in sending], [v[1] for v in sending.values()])

        big = {"w_up": (_wgrad(h2, da, shard_cols=w_up.shape[2]), layer),
               "w_down": (_wgrad(r, dx2, shard_rows=w_down.shape[1]), layer)}
        small = {"w_o": (_wgrad(mixed, dx1, shard_rows=w_o.shape[1]), layer),
                 "w_ple_gate": (_wgrad(h3, dpre, shard_rows=w_ple_gate.shape[1]), layer),
                 "w_ple": (_wgrad(pl_in, de, shard_cols=w_ple.shape[2]), layer), **late}
        dqf, dkf, dkv, got = _attn_bwd(s["qf"], s["kf"], s["kv"], s["attn"], dattn, s["lse"], exchange_of(big))
        landed.update(zip(big, got))
        (dx0, dz, hb, qn, kvn, dqr, dkvr, dg_mix, dg_q, dg_kv, dg_qn, dg_kn, dcw), got = _front_bwd(
            s["x"], s["z"], dx1, dqf, dkf, dkv, dconv, w, tabs, exchange_of(small))
        landed.update(zip(small, got))
        late = {"w_in": (_w_in_grad_shards(_wgrad(hb, dz, out_dtype=F32)).astype(BF16), layer),
                "w_uq": (_wgrad(qn, dqr, shard_cols=HEAD_PAD)[..., :uq_cols], layer),
                "w_ukv": (_wgrad(kvn, dkvr, shard_cols=HEAD_PAD), layer),
                "conv_w": (jnp.transpose(dcw[:n_taps].reshape(n_taps, N_DEV, -1), (1, 0, 2)).astype(BF16), layer)}
        gain_grads[layer] = jnp.concatenate([
            dg_mix[0], dg_q[0], dg_kv[0], dg_qn[0, :QK_NOPE], dg_qn[0, QK_NOPE:QK_HEAD], dg_kn[0, :QK_NOPE],
            dg_kn[0, QK_NOPE:QK_HEAD], dg_oa[0], dg_oc[0], dg_mlp[0], dg_ple[0]])
        dx = dx0
    names = list(late)
    landed.update(zip(names, _scatter_exchange([late[n][0] for n in names], [landed[n] for n in names],
                                               [late[n][1] for n in names])))
    grads = {n: _sum_leading(landed[n]) for n in SHARD_NAMES}

    gg = jnp.stack(gain_grads)
    gg_rows = -(-gg.size // (HALO * LANES)) * HALO
    gg_pad = jnp.pad(gg.reshape(-1), (0, gg_rows * LANES - gg.size)).reshape(gg_rows, LANES)
    gg_sum = _sum_leading(_all_gather([gg_pad])[0]).reshape(-1)[:gg.size].reshape(gg.shape)
    off = 0
    for n in GAIN_NAMES:
        width = gains[n].shape[1]
        grads[n] = gg_sum[:, off:off + width]
        off += width

    deltas, new_m, new_v = {}, {}, {}
    for n in WEIGHT_NAMES:
        deltas[n], new_m[n], new_v[n] = _adamw(weights[n], grads[n], given["m_" + n], given["v_" + n])
    return (loss, dx[None], *[grads[n] for n in WEIGHT_NAMES], *[deltas[n] for n in WEIGHT_NAMES],
            *[new_m[n] for n in WEIGHT_NAMES], *[new_v[n] for n in WEIGHT_NAMES])
```

```python
import functools

import jax
import jax.numpy as jnp
from jax import lax
from jax.experimental import pallas as pl
from jax.experimental.pallas import tpu as pltpu

F32 = jnp.float32
BF16 = jnp.bfloat16
MESH = pl.DeviceIdType.MESH

N_HEADS = 8
QK_NOPE = 64
QK_ROPE = 32
QK_HEAD = QK_NOPE + QK_ROPE
V_HEAD = 64
HEAD_PAD = 128
PAIR = 2
ATTN_SCALE = QK_HEAD ** -0.5
Q_LORA = 384
KV_LORA = 256
CONV_WIDTH = 512
ATTN_WIDTH = N_HEADS * V_HEAD
ROPE_THETA = 10000.0
EPS = 1e-6
ADAM_LR, ADAM_B1, ADAM_B2, ADAM_EPS, ADAM_WD, ADAM_STEP = 0.001, 0.9, 0.999, 1e-08, 0.01, 10

Z_Q = (0, 384)
Z_KV = (384, 640)
Z_GB = (640, 1152)
Z_GC = (1152, 1664)
Z_XIN = (1664, 2176)
Z_KPE = (2176, 2304)
Z_COLS = 2304

N_DEV = 8
LANES = 128
V7X_VMEM_LIMIT = 52 * 1024 * 1024
TOKEN_TILE = 256
LIGHT_TOKEN_TILE = 512
ATTN_BLOCK = 256
ATTN_FWD_BLOCK = 512
ATTN_FWD_ROWS = 2048
ATTN_Q_SUB = 2
ATTN_KV_SUB = 16
ROW_BLOCK = 512
WGRAD_TOKENS = 2048
WGRAD_TILE = 1024
HALO = 8

GAIN_NAMES = ("g_mix", "g_q_lat", "g_kv_lat", "g_qn_nope", "g_qn_rope", "g_kn_nope", "g_kn_rope",
              "g_out_attn", "g_out_conv", "g_mlp", "g_ple")
SHARD_NAMES = ("w_in", "w_uq", "w_ukv", "conv_w", "w_o", "w_up", "w_down", "w_ple_gate", "w_ple")
WEIGHT_NAMES = ("g_mix", "w_in", "g_q_lat", "w_uq", "g_kv_lat", "w_ukv", "g_qn_nope", "g_qn_rope", "g_kn_nope",
                "g_kn_rope", "conv_w", "g_out_attn", "g_out_conv", "w_o", "g_mlp", "w_up", "w_down", "g_ple",
                "w_ple_gate", "w_ple")


def _cparams(semantics=None):
    return pltpu.CompilerParams(dimension_semantics=semantics, vmem_limit_bytes=V7X_VMEM_LIMIT)


def _mm(a, b):
    return jnp.dot(a, b, preferred_element_type=F32)


def _mm_nt(a, b):
    return lax.dot_general(a, b, (((1,), (1,)), ((), ())), preferred_element_type=F32)


def _mm_tn(a, b):
    return lax.dot_general(a, b, (((0,), (0,)), ((), ())), preferred_element_type=F32)


def _rms(x, g):
    r = lax.rsqrt(jnp.mean(x * x, axis=-1, keepdims=True) + EPS)
    return (x * r) * g


def _rms_bwd(x, g, dy):
    r = lax.rsqrt(jnp.mean(x * x, axis=-1, keepdims=True) + EPS)
    xh = x * r
    dg = jnp.sum(dy * xh, axis=0, keepdims=True)
    dyg = dy * g
    dx = r * (dyg - xh * jnp.mean(dyg * xh, axis=-1, keepdims=True))
    return dx, dg


def _group_mean(t, gm):
    hi = t.astype(BF16)
    lo = (t - hi.astype(F32)).astype(BF16)
    return _mm(hi, gm) + _mm(lo, gm)


def _swap_rope_halves(x, lane):
    half = QK_ROPE // 2
    swapped = jnp.where(lane < QK_NOPE + half, pltpu.roll(x, x.shape[1] - half, 1), pltpu.roll(x, half, 1))
    return jnp.where((lane >= QK_NOPE) & (lane < QK_HEAD), swapped, 0.0)


def _qk_fwd(x, g, cos, sin, gm, lane):
    r = lax.rsqrt(_group_mean(x * x, gm) + EPS)
    n = (x * r) * g
    return n * cos + _swap_rope_halves(n, lane) * sin


def _qk_bwd(x, g, dy, cos, sin, gm, lane):
    r = lax.rsqrt(_group_mean(x * x, gm) + EPS)
    xh = x * r
    dn = dy * cos + _swap_rope_halves(dy * sin, lane)
    dg = jnp.sum(dn * xh, axis=0, keepdims=True)
    dng = dn * g
    dx = r * (dng - xh * _group_mean(dng * xh, gm))
    return dx, dg


def _row_shards_joined(ref):
    n, rows, cols = ref.shape
    return ref[...].reshape(n * rows, cols)


def _rows(tm, n):
    return pl.BlockSpec((tm, n), lambda i: (i, 0))


def _whole(shape):
    zeros = (0,) * len(shape)
    return pl.BlockSpec(shape, lambda i: zeros)


def _operands(arrays):
    return list(arrays), [_whole(a.shape) for a in arrays]


def _accumulate(ref, first, value):
    @pl.when(first)
    def _():
        ref[...] = jnp.zeros_like(ref)
    ref[...] += value


def _front_fwd(x, w, tabs, gather=None):
    t_len, d = x.shape
    tm = min(TOKEN_TILE, t_len)
    hp = N_HEADS * HEAD_PAD

    def body(x_ref, gmix, win, gq, wuq, gkv, wukv, gqn, gkn, cw_ref, gm_ref, cos_ref, sin_ref,
             z_ref, qf_ref, kf_ref, kv_ref, conv_ref, ubuf):
        i = pl.program_id(0)
        h = _rms(x_ref[...], gmix[...])
        z = _mm(h.astype(BF16), win[...])
        z_ref[...] = z
        qnb = _rms(z[:, Z_Q[0]:Z_Q[1]], gq[...]).astype(BF16)
        kvb = _rms(z[:, Z_KV[0]:Z_KV[1]], gkv[...]).astype(BF16)
        kpe = z[:, Z_KPE[0]:Z_KPE[1]]
        kpe = jnp.concatenate([kpe] * PAIR, axis=1)
        cos, sin = (jnp.concatenate([t[...]] * PAIR, axis=1) for t in (cos_ref, sin_ref))
        gm = gm_ref[...]
        lane = lax.broadcasted_iota(jnp.int32, (tm, PAIR * HEAD_PAD), 1) & (HEAD_PAD - 1)
        for pr in range(N_HEADS // PAIR):
            sl = slice(pr * PAIR * HEAD_PAD, (pr + 1) * PAIR * HEAD_PAD)
            qf_ref[:, sl] = (_qk_fwd(_mm(qnb, wuq[pr]), gqn[...], cos, sin, gm, lane) * ATTN_SCALE).astype(BF16)
            kv = _mm(kvb, wukv[pr])
            kv_ref[:, sl] = jnp.where(lane < QK_NOPE, jnp.where(lane == 0, 1.0, 0.0), kv).astype(BF16)
            kf_ref[:, sl] = _qk_fwd(jnp.where(lane < QK_NOPE, kv, 0.0) + kpe, gkn[...], cos, sin, gm, lane).astype(BF16)
        u = z[:, Z_GC[0]:Z_GC[1]] * z[:, Z_XIN[0]:Z_XIN[1]]

        @pl.when(i == 0)
        def _():
            ubuf[0:HALO, :] = jnp.zeros((HALO, CONV_WIDTH), F32)
        ubuf[HALO:HALO + tm, :] = u
        cw = cw_ref[...]
        y = cw[0:1] * u + cw[1:2] * ubuf[pl.ds(HALO - 1, tm), :] + cw[2:3] * ubuf[pl.ds(HALO - 2, tm), :]
        conv_ref[...] = z[:, Z_GB[0]:Z_GB[1]] * y
        ubuf[0:HALO, :] = u[tm - HALO:tm, :]

    consts, const_specs = _operands([w["g_mix"], w["w_in"], w["g_q_lat"], w["w_uq"], w["g_kv_lat"], w["w_ukv"],
                                     w["g_qn"], w["g_kn"], w["conv_w"], tabs["gm"]])
    out_shape = (jax.ShapeDtypeStruct((t_len, Z_COLS), F32), jax.ShapeDtypeStruct((t_len, hp), BF16),
                 jax.ShapeDtypeStruct((t_len, hp), BF16), jax.ShapeDtypeStruct((t_len, hp), BF16),
                 jax.ShapeDtypeStruct((t_len, CONV_WIDTH), F32))
    return _call_hosting_gather("front_fwd", body, t_len // tm, [_rows(tm, d)] + const_specs + [_rows(tm, HEAD_PAD)] * 2,
                                tuple(_rows(tm, s.shape[1]) for s in out_shape), out_shape,
                                [pltpu.VMEM((tm + HALO, CONV_WIDTH), F32)], (x, *consts, tabs["cos"], tabs["sin"]), gather)


def _attn_fwd(qf, kf, kv, gather=None):
    t_len = qf.shape[0]
    blk = min(ATTN_FWD_BLOCK, t_len)
    bq = blk // ATTN_Q_SUB
    span = min(ATTN_FWD_ROWS, t_len)
    nb = t_len // span
    n_sub = span // bq
    chains = [(hh, a) for hh in range(2) for a in range(n_sub)]

    def body(q_ref, k_ref, kv_ref, o_ref, lse_ref):
        lane = lax.broadcasted_iota(jnp.int32, (bq, LANES), 1)
        row = lax.broadcasted_iota(jnp.int32, (bq, blk), 0)
        col = lax.broadcasted_iota(jnp.int32, (bq, blk), 1)

        def head_cols(hh):
            return slice(hh * HEAD_PAD, (hh + 1) * HEAD_PAD)

        def softmax_step(s, kvv, state, first_row=None):
            m, acc = state
            if first_row is not None:
                s = jnp.where(col <= row + first_row, s, -jnp.inf)
            m_new = jnp.maximum(m, jnp.max(s, axis=-1, keepdims=True))
            p = jnp.exp(s - m_new)
            acc = jnp.exp(m - m_new) * acc + _mm(p.astype(BF16), kvv)
            return m_new, acc

        def finish(state):
            m, acc = state
            l = jnp.sum(jnp.where(lane == 0, acc, 0.0), axis=-1, keepdims=True)
            return acc / l, jnp.broadcast_to(m + jnp.log(l), (bq, LANES))

        def qblock(i):
            rows = [pl.ds(a * bq, bq) for a in range(n_sub)]
            first = [i * span + a * bq for a in range(n_sub)]
            states = [(jnp.full((bq, 1), -jnp.inf, F32), jnp.zeros((bq, LANES), F32))] * len(chains)
            steps = []
            for j in range((first[-1] + bq - 1) // blk + 1):
                for n, (hh, a) in enumerate(chains):
                    if j * blk <= first[a] + bq - 1:
                        diagonal = (j + 1) * blk - 1 > first[a]
                        steps.append((n, j, first[a] - j * blk if diagonal else None))
            ahead = len(chains)
            scores = []
            for t in range(len(steps) + ahead):
                if t < len(steps):
                    n, j, _ = steps[t]
                    hh, a = chains[n]
                    scores.append(_mm_nt(q_ref[rows[a], head_cols(hh)], k_ref[pl.ds(j * blk, blk), head_cols(hh)]))
                if t >= ahead:
                    n, j, offset = steps[t - ahead]
                    states[n] = softmax_step(scores[t - ahead], kv_ref[pl.ds(j * blk, blk), head_cols(chains[n][0])],
                                             states[n], offset)
            for a in range(n_sub):
                (o0, lse0), (o1, lse1) = finish(states[chains.index((0, a))]), finish(states[chains.index((1, a))])
                o_ref[rows[a], :] = jnp.where(lane < V_HEAD, pltpu.roll(o0, V_HEAD, 1), o1)
                lse_ref[rows[a], head_cols(0)] = lse0
                lse_ref[rows[a], head_cols(1)] = lse1

        mine = pl.program_id(0) % nb
        for i in range(nb):
            pl.when(mine == i)(functools.partial(qblock, i))

    q_spec = pl.BlockSpec((span, 2 * HEAD_PAD), lambda g: (g % nb, g // nb))
    keys = pl.BlockSpec((t_len, 2 * HEAD_PAD), lambda g: (0, g // nb))
    o_spec = pl.BlockSpec((span, 2 * V_HEAD), lambda g: (g % nb, g // nb))
    out_shape = (jax.ShapeDtypeStruct((t_len, ATTN_WIDTH), F32), jax.ShapeDtypeStruct((t_len, N_HEADS * LANES), F32))
    (attn, lse), gathered = _call_hosting_gather("attn_fwd", body, N_HEADS // 2 * nb, [q_spec, keys, keys],
                                                 (o_spec, q_spec), out_shape, [], (qf, kf, kv), gather)
    return attn, lse, gathered


def _call_hosting_gather(name, body, n_steps, in_specs, out_specs, out_shape, scratch_shapes, args, gather):
    if not gather:
        return pl.pallas_call(body, name=name, grid=(n_steps,), in_specs=list(in_specs), out_specs=tuple(out_specs),
                              out_shape=tuple(out_shape), scratch_shapes=list(scratch_shapes),
                              compiler_params=_cparams(("arbitrary",)))(*args), ()
    n_in, n_out, n_g = len(in_specs), len(out_shape), len(gather)

    def hosting_body(*refs):
        ins, refs = refs[:n_in], refs[n_in:]
        x_refs = [r.at[layer] for r, (_, layer) in zip(refs[:n_g], gather)]
        outs, landing, sems, scratch = (refs[n_g:n_g + n_out], refs[n_g + n_out:2 * n_g + n_out],
                                        refs[2 * n_g + n_out:2 * n_g + n_out + 3], refs[2 * n_g + n_out + 3:])
        start, forward, finish = _gather_phases(x_refs, landing, *sems)
        step = pl.program_id(0)
        pl.when(step == 0)(start)
        pl.when(step == n_steps - 1)(forward)
        body(*ins, *outs, *scratch)
        pl.when(step == n_steps - 1)(finish)

    res = pl.pallas_call(hosting_body, name=name + "_gather", grid=(n_steps,), in_specs=list(in_specs) + _any_specs(n_g),
                         out_specs=tuple(out_specs) + tuple(_any_specs(n_g)),
                         out_shape=tuple(out_shape) + _gather_out_shape([s[layer] for s, layer in gather]),
                         scratch_shapes=_gather_semaphores(n_g) + list(scratch_shapes),
                         compiler_params=_cparams(("arbitrary",)))(*args, *[s for s, _ in gather])
    return res[:n_out], res[n_out:]


def _mix_out_fwd(x, attn, conv, w):
    t_len, d = x.shape
    tm = min(LIGHT_TOKEN_TILE, t_len)

    def body(x_ref, a_ref, c_ref, goa, goc, wo, x1_ref):
        mixed = jnp.concatenate([_rms(a_ref[...], goa[...]), _rms(c_ref[...], goc[...])], axis=1)
        x1_ref[...] = x_ref[...] + _mm(mixed.astype(BF16), _row_shards_joined(wo))

    consts, const_specs = _operands([w["g_out_attn"], w["g_out_conv"], w["w_o"]])
    return pl.pallas_call(body, name="mix_out_fwd", grid=(t_len // tm,),
                          in_specs=[_rows(tm, d), _rows(tm, ATTN_WIDTH), _rows(tm, CONV_WIDTH)] + const_specs,
                          out_specs=_rows(tm, d), out_shape=jax.ShapeDtypeStruct((t_len, d), F32),
                          compiler_params=_cparams(("parallel",)))(x, attn, conv, *consts)


def _mlp_fwd(x1, w, gather=None):
    t_len, d = x1.shape
    tm = min(TOKEN_TILE, t_len)

    def body(x_ref, g, wup, wdn, x2_ref):
        x1v = x_ref[...]
        hb = _rms(x1v, g[...]).astype(BF16)
        acc = x1v
        for k in range(N_DEV):
            a = jnp.maximum(_mm(hb, wup[k]), 0.0)
            acc = acc + _mm((a * a).astype(BF16), wdn[k])
        x2_ref[...] = acc

    consts, const_specs = _operands([w["g_mlp"], w["w_up"], w["w_down"]])
    (x2,), gathered = _call_hosting_gather("mlp_fwd", body, t_len // tm, [_rows(tm, d)] + const_specs, (_rows(tm, d),),
                                           (jax.ShapeDtypeStruct((t_len, d), F32),), [], (x1, *consts), gather)
    return x2, gathered


def _ple_fwd(x2, p, w):
    t_len, d = x2.shape
    tm = min(LIGHT_TOKEN_TILE, t_len)

    def body(x_ref, p_ref, g, wg, wp, x3_ref):
        x2v = x_ref[...]
        gate = jax.nn.sigmoid(_mm(_rms(x2v, g[...]).astype(BF16), _row_shards_joined(wg)))
        pb = p_ref[...].astype(BF16)
        e = jnp.concatenate([_mm(pb, wp[k]) for k in range(N_DEV)], axis=1)
        x3_ref[...] = x2v + gate * e

    consts, const_specs = _operands([w["g_ple"], w["w_ple_gate"], w["w_ple"]])
    return pl.pallas_call(body, name="ple_fwd", grid=(t_len // tm,),
                          in_specs=[_rows(tm, d), _rows(tm, p.shape[1])] + const_specs, out_specs=_rows(tm, d),
                          out_shape=jax.ShapeDtypeStruct((t_len, d), F32),
                          compiler_params=_cparams(("parallel",)))(x2, p, *consts)


def _loss_and_grad(y, target):
    t_len, d = y.shape
    tm = min(TOKEN_TILE, t_len)

    def body(y_ref, t_ref, sq_ref, dy_ref):
        err = y_ref[...] - t_ref[...]
        dy_ref[...] = err / d
        total = jnp.sum(jnp.sum(err * err, axis=0, keepdims=True), axis=1, keepdims=True)
        _accumulate(sq_ref, pl.program_id(0) == 0, jnp.broadcast_to(total, (HALO, LANES)))

    return pl.pallas_call(body, name="loss_grad", grid=(t_len // tm,), in_specs=[_rows(tm, d), _rows(tm, d)],
                          out_specs=(_whole((HALO, LANES)), _rows(tm, d)),
                          out_shape=(jax.ShapeDtypeStruct((HALO, LANES), F32), jax.ShapeDtypeStruct((t_len, d), F32)),
                          compiler_params=_cparams(("arbitrary",)))(y, target)


def _ple_bwd(dx3, x2, p, w):
    t_len, d = x2.shape
    tm = min(LIGHT_TOKEN_TILE, t_len)

    def body(dx3_ref, x_ref, p_ref, g, wg, wp, dx2_ref, de_ref, h3_ref, dpre_ref, dg_ref):
        x2v, dx3v = x_ref[...], dx3_ref[...]
        hb = _rms(x2v, g[...]).astype(BF16)
        h3_ref[...] = hb
        w_gate = _row_shards_joined(wg)
        gate = jax.nn.sigmoid(_mm(hb, w_gate))
        pb = p_ref[...].astype(BF16)
        e = jnp.concatenate([_mm(pb, wp[k]) for k in range(N_DEV)], axis=1)
        de_ref[...] = (dx3v * gate).astype(BF16)
        dpre = ((dx3v * e) * gate * (1.0 - gate)).astype(BF16)
        dpre_ref[...] = dpre
        dx, dg = _rms_bwd(x2v, g[...], _mm_nt(dpre, w_gate))
        dx2_ref[...] = dx3v + dx
        _accumulate(dg_ref, pl.program_id(0) == 0, dg)

    consts, const_specs = _operands([w["g_ple"], w["w_ple_gate"], w["w_ple"]])
    out_shape = (jax.ShapeDtypeStruct((t_len, d), F32), jax.ShapeDtypeStruct((t_len, d), BF16),
                 jax.ShapeDtypeStruct((t_len, d), BF16), jax.ShapeDtypeStruct((t_len, d), BF16),
                 jax.ShapeDtypeStruct((1, d), F32))
    return pl.pallas_call(body, name="ple_bwd", grid=(t_len // tm,),
                          in_specs=[_rows(tm, d), _rows(tm, d), _rows(tm, p.shape[1])] + const_specs,
                          out_specs=(_rows(tm, d),) * 4 + (_whole((1, d)),), out_shape=out_shape,
                          compiler_params=_cparams(("arbitrary",)))(dx3, x2, p, *consts)


def _mlp_bwd(dx2, x1, w):
    t_len, d = x1.shape
    tm = min(TOKEN_TILE, t_len)
    fc = w["w_up"].shape[2]
    ff = N_DEV * fc

    def body(dx2_ref, x_ref, g, wup, wdn, dx1_ref, r_ref, da_ref, h2_ref, dg_ref):
        x1v, dx2v = x_ref[...], dx2_ref[...]
        hb = _rms(x1v, g[...]).astype(BF16)
        h2_ref[...] = hb
        dxb = dx2v.astype(BF16)
        dh = jnp.zeros((tm, d), F32)
        for k in range(N_DEV):
            a = jnp.maximum(_mm(hb, wup[k]), 0.0)
            r_ref[:, k * fc:(k + 1) * fc] = (a * a).astype(BF16)
            da = (_mm_nt(dxb, wdn[k]) * (2.0 * a)).astype(BF16)
            da_ref[:, k * fc:(k + 1) * fc] = da
            dh = dh + _mm_nt(da, wup[k])
        dx, dg = _rms_bwd(x1v, g[...], dh)
        dx1_ref[...] = dx2v + dx
        _accumulate(dg_ref, pl.program_id(0) == 0, dg)

    consts, const_specs = _operands([w["g_mlp"], w["w_up"], w["w_down"]])
    out_shape = (jax.ShapeDtypeStruct((t_len, d), F32), jax.ShapeDtypeStruct((t_len, ff), BF16),
                 jax.ShapeDtypeStruct((t_len, ff), BF16), jax.ShapeDtypeStruct((t_len, d), BF16),
                 jax.ShapeDtypeStruct((1, d), F32))
    return pl.pallas_call(body, name="mlp_bwd", grid=(t_len // tm,), in_specs=[_rows(tm, d), _rows(tm, d)] + const_specs,
                          out_specs=(_rows(tm, d), _rows(tm, ff), _rows(tm, ff), _rows(tm, d), _whole((1, d))),
                          out_shape=out_shape, compiler_params=_cparams(("arbitrary",)))(dx2, x1, *consts)


def _mix_out_bwd(dx1, attn, conv, w):
    t_len, d = dx1.shape
    tm = min(LIGHT_TOKEN_TILE, t_len)

    def body(dx1_ref, a_ref, c_ref, goa, goc, wo, mixed_ref, da_ref, dc_ref, dgoa_ref, dgoc_ref):
        av, cv = a_ref[...], c_ref[...]
        mixed_ref[...] = jnp.concatenate([_rms(av, goa[...]), _rms(cv, goc[...])], axis=1).astype(BF16)
        dmixed = _mm_nt(dx1_ref[...].astype(BF16), _row_shards_joined(wo))
        da, dga = _rms_bwd(av, goa[...], dmixed[:, :ATTN_WIDTH])
        dc, dgc = _rms_bwd(cv, goc[...], dmixed[:, ATTN_WIDTH:])
        da_ref[...] = da
        dc_ref[...] = dc
        first = pl.program_id(0) == 0
        _accumulate(dgoa_ref, first, dga)
        _accumulate(dgoc_ref, first, dgc)

    consts, const_specs = _operands([w["g_out_attn"], w["g_out_conv"], w["w_o"]])
    out_shape = (jax.ShapeDtypeStruct((t_len, d), BF16), jax.ShapeDtypeStruct((t_len, ATTN_WIDTH), F32),
                 jax.ShapeDtypeStruct((t_len, CONV_WIDTH), F32), jax.ShapeDtypeStruct((1, ATTN_WIDTH), F32),
                 jax.ShapeDtypeStruct((1, CONV_WIDTH), F32))
    out_specs = (_rows(tm, d), _rows(tm, ATTN_WIDTH), _rows(tm, CONV_WIDTH), _whole((1, ATTN_WIDTH)),
                 _whole((1, CONV_WIDTH)))
    return pl.pallas_call(body, name="mix_out_bwd", grid=(t_len // tm,),
                          in_specs=[_rows(tm, d), _rows(tm, ATTN_WIDTH), _rows(tm, CONV_WIDTH)] + const_specs,
                          out_specs=out_specs, out_shape=out_shape,
                          compiler_params=_cparams(("arbitrary",)))(dx1, attn, conv, *consts)


def _attn_bwd(qf, kf, kv, o, do, lse, scatter):
    t_len = qf.shape[0]
    blk = min(ATTN_BLOCK, t_len)
    nb = t_len // blk
    n_sub = min(ATTN_KV_SUB, nb)
    reps = blk // LANES

    def body(q_ref, k_ref, kv_ref, o_ref, do_ref, lse_ref, dq_ref, dk_ref, dkv_ref, delta_ref, dob_ref):
        hd = pl.program_id(0)
        lane = lax.broadcasted_iota(jnp.int32, (blk, LANES), 1)
        even = (lane * 0 + hd % 2) == 0
        mine = jnp.where(lane < V_HEAD, 0, 1) == hd % 2
        row = lax.broadcasted_iota(jnp.int32, (blk, blk), 0)
        col = lax.broadcasted_iota(jnp.int32, (blk, blk), 1)
        dq_ref[...] = jnp.zeros_like(dq_ref)

        def prepare(i, carry):
            qs = pl.ds(pl.multiple_of(i * blk, blk), blk)
            dov = do_ref[qs, :]
            prod = jnp.where(mine, dov * o_ref[qs, :], 0.0)
            delta_ref[qs, :] = jnp.broadcast_to(jnp.sum(prod, axis=-1, keepdims=True), (blk, LANES))
            moved = jnp.where(even, pltpu.roll(dov, V_HEAD, 1), dov)
            dob_ref[qs, :] = jnp.where(lane >= V_HEAD, moved, 0.0).astype(BF16)
            return carry
        lax.fori_loop(0, nb, prepare, 0)

        def kvblock(jj, carry):
            base = jj * n_sub
            kss = [pl.ds(pl.multiple_of((base + a) * blk, blk), blk) for a in range(n_sub)]
            k = [k_ref[ks, :] for ks in kss]
            kvv = [kv_ref[ks, :] for ks in kss]

            def products(i):
                qs = pl.ds(pl.multiple_of(i * blk, blk), blk)
                q, dob = q_ref[qs, :], dob_ref[qs, :]
                return tuple((_mm_nt(q, k[a]), _mm_nt(dob, kvv[a])) for a in range(n_sub))

            def qstep(i, raw, accs, kinds):
                qs = pl.ds(pl.multiple_of(i * blk, blk), blk)
                q = q_ref[qs, :]
                dob = dob_ref[qs, :]
                lse_t = jnp.concatenate([lse_ref[qs, :]] * reps, axis=1)
                delta_t = jnp.concatenate([delta_ref[qs, :]] * reps, axis=1)
                new, dq_add = [], None
                for a in range(n_sub):
                    if kinds[a] is None:
                        new.append(accs[a])
                        continue
                    dk_acc, dv_acc = accs[a]
                    s, dp = raw[a]
                    if kinds[a]:
                        s = jnp.where(col <= row, s, -jnp.inf)
                    p = jnp.exp(s - lse_t)
                    ds = (p * (dp - delta_t)).astype(BF16)
                    new.append((dk_acc + _mm_tn(ds, q), dv_acc + _mm_tn(p.astype(BF16), dob)))
                    part = _mm(ds, k[a])
                    dq_add = part if dq_add is None else dq_add + part
                dq_ref[qs, :] += dq_add
                return tuple(new)

            zero = jnp.zeros((blk, LANES), F32)
            accs = ((zero, zero),) * n_sub
            for b in range(n_sub):
                accs = qstep(base + b, products(base + b), accs, tuple((a == b) if a <= b else None for a in range(n_sub)))

            def pipelined(i, carried):
                raw, acc = carried
                return products(jnp.minimum(i + 1, nb - 1)), qstep(i, raw, acc, (False,) * n_sub)

            first = base + n_sub
            _, accs = lax.fori_loop(first, nb, pipelined, (products(jnp.minimum(first, nb - 1)), accs))
            for a in range(n_sub):
                dk_ref[kss[a], :] = accs[a][0]
                dkv_ref[kss[a], :] = accs[a][1]
            return carry
        lax.fori_loop(0, nb // n_sub, kvblock, 0)

    head = pl.BlockSpec((t_len, HEAD_PAD), lambda h: (0, h))
    pair = pl.BlockSpec((t_len, 2 * V_HEAD), lambda h: (0, h // 2))
    out = jax.ShapeDtypeStruct((t_len, N_HEADS * HEAD_PAD), F32)
    vmem_scratch = [pltpu.VMEM((t_len, LANES), F32), pltpu.VMEM((t_len, LANES), BF16)]
    (dq, dk, dkv), landed = _call_hosting_scatter("attn_bwd", body, N_HEADS, [head, head, head, pair, pair, head],
                                                  (head, head, head), (out, out, out), vmem_scratch,
                                                  (qf, kf, kv, o, do, lse), scatter)
    return dq, dk, dkv, landed


def _call_hosting_scatter(name, body, n_steps, in_specs, out_specs, out_shape, scratch_shapes, args, scatter):
    parts, landed, layers = scatter
    n_in, n_out, n_sc = len(in_specs), len(out_shape), len(parts)

    def hosting_body(*refs):
        ins, part_refs, refs = refs[:n_in], refs[n_in:n_in + n_sc], refs[n_in + 2 * n_sc:]
        outs, landed_refs, sems, scratch = (refs[:n_out], refs[n_out:n_out + n_sc], refs[n_out + n_sc:n_out + n_sc + 3],
                                            refs[n_out + n_sc + 3:])
        start, finish = _scatter_phases(part_refs, landed_refs, *sems, layers)
        step = pl.program_id(0)
        pl.when(step == 0)(start)
        body(*ins, *outs, *scratch)
        pl.when(step == n_steps - 1)(finish)

    res = pl.pallas_call(hosting_body, name=name + "_scatter", grid=(n_steps,),
                         in_specs=list(in_specs) + _any_specs(2 * n_sc),
                         out_specs=tuple(out_specs) + tuple(_any_specs(n_sc)),
                         out_shape=tuple(out_shape) + _same_shapes(landed),
                         scratch_shapes=_scatter_semaphores(n_sc) + list(scratch_shapes),
                         input_output_aliases={n_in + n_sc + a: n_out + a for a in range(n_sc)},
                         compiler_params=_cparams(("arbitrary",)))(*args, *parts, *landed)
    return res[:n_out], res[n_out:]


def _front_bwd(x, z, dx1, dqf, dkf, dkv_in, dconv, w, tabs, scatter):
    t_len, d = x.shape
    tm = min(TOKEN_TILE, t_len)
    nt = t_len // tm
    hb_per_tile = tm // HALO
    n_halo = t_len // HALO
    hp = N_HEADS * HEAD_PAD

    def body(x_ref, z_ref, zp_ref, zn_ref, dx1_ref, dqf_ref, dkf_ref, dkv_ref, dc_ref, dcn_ref,
             gmix, win, gq, wuq, gkv, wukv, gqn, gkn, cw_ref, gm_ref, cos_ref, sin_ref,
             dx_ref, dz_ref, h_ref, qn_ref, kvn_ref, dqr_ref, dkvr_ref,
             dgmix_ref, dgq_ref, dgkv_ref, dgqn_ref, dgkn_ref, dcw_ref, ubuf, dybuf):
        i = pl.program_id(0)
        first = i == 0
        xv, zv = x_ref[...], z_ref[...]
        hb = _rms(xv, gmix[...]).astype(BF16)
        h_ref[...] = hb
        zq, zkv = zv[:, Z_Q[0]:Z_Q[1]], zv[:, Z_KV[0]:Z_KV[1]]
        qnb = _rms(zq, gq[...]).astype(BF16)
        qn_ref[...] = qnb
        kvb = _rms(zkv, gkv[...]).astype(BF16)
        kvn_ref[...] = kvb
        kpe = zv[:, Z_KPE[0]:Z_KPE[1]]
        kpe = jnp.concatenate([kpe] * PAIR, axis=1)
        cos, sin = (jnp.concatenate([t[...]] * PAIR, axis=1) for t in (cos_ref, sin_ref))
        gm = gm_ref[...]
        width = PAIR * HEAD_PAD
        lane = lax.broadcasted_iota(jnp.int32, (tm, width), 1) & (HEAD_PAD - 1)
        is_nope = lane < QK_NOPE
        is_rope = (lane >= QK_NOPE) & (lane < QK_HEAD)
        dkpe = jnp.zeros((tm, width), F32)
        dgqn = jnp.zeros((1, width), F32)
        dgkn = jnp.zeros((1, width), F32)
        dqn = jnp.zeros((tm, Q_LORA), F32)
        dkvn = jnp.zeros((tm, KV_LORA), F32)
        for pr in range(N_HEADS // PAIR):
            sl = slice(pr * width, (pr + 1) * width)
            dxq, dg = _qk_bwd(_mm(qnb, wuq[pr]), gqn[...], dqf_ref[:, sl] * ATTN_SCALE, cos, sin, gm, lane)
            dxq = dxq.astype(BF16)
            dqr_ref[:, sl] = dxq
            dqn = dqn + _mm_nt(dxq, wuq[pr])
            dgqn = dgqn + dg
            k_raw = jnp.where(is_nope, _mm(kvb, wukv[pr]), 0.0) + kpe
            dxk, dg = _qk_bwd(k_raw, gkn[...], dkf_ref[:, sl], cos, sin, gm, lane)
            dkv = jnp.where(is_nope, dxk, dkv_ref[:, sl]).astype(BF16)
            dkvr_ref[:, sl] = dkv
            dkvn = dkvn + _mm_nt(dkv, wukv[pr])
            dkpe = dkpe + jnp.where(is_rope, dxk, 0.0)
            dgkn = dgkn + dg
        dkpe = dkpe[:, :HEAD_PAD] + dkpe[:, HEAD_PAD:]
        _accumulate(dgqn_ref, first, dgqn[:, :HEAD_PAD] + dgqn[:, HEAD_PAD:])
        _accumulate(dgkn_ref, first, dgkn[:, :HEAD_PAD] + dgkn[:, HEAD_PAD:])
        dzq, dg = _rms_bwd(zq, gq[...], dqn)
        _accumulate(dgq_ref, first, dg)
        dzkv, dg = _rms_bwd(zkv, gkv[...], dkvn)
        _accumulate(dgkv_ref, first, dg)

        gb, gc, xin = zv[:, Z_GB[0]:Z_GB[1]], zv[:, Z_GC[0]:Z_GC[1]], zv[:, Z_XIN[0]:Z_XIN[1]]
        u = gc * xin
        dcv = dc_ref[...]
        dy = dcv * gb
        zp, zn = zp_ref[...], zn_ref[...]
        ubuf[0:HALO, :] = (zp[:, Z_GC[0]:Z_GC[1]] * zp[:, Z_XIN[0]:Z_XIN[1]]) * jnp.where(first, 0.0, 1.0)
        ubuf[HALO:HALO + tm, :] = u
        dybuf[0:tm, :] = dy
        dybuf[tm:tm + HALO, :] = (dcn_ref[...] * zn[:, Z_GB[0]:Z_GB[1]]) * jnp.where(i == nt - 1, 0.0, 1.0)
        cw = cw_ref[...]
        u1, u2 = ubuf[pl.ds(HALO - 1, tm), :], ubuf[pl.ds(HALO - 2, tm), :]
        y = cw[0:1] * u + cw[1:2] * u1 + cw[2:3] * u2
        du = cw[0:1] * dy + cw[1:2] * dybuf[pl.ds(1, tm), :] + cw[2:3] * dybuf[pl.ds(2, tm), :]
        dcw = jnp.concatenate([jnp.sum(dy * u, axis=0, keepdims=True), jnp.sum(dy * u1, axis=0, keepdims=True),
                               jnp.sum(dy * u2, axis=0, keepdims=True), jnp.zeros((HALO - 3, CONV_WIDTH), F32)], axis=0)
        _accumulate(dcw_ref, first, dcw)

        dz_ref[:, Z_Q[0]:Z_Q[1]] = dzq.astype(BF16)
        dz_ref[:, Z_KV[0]:Z_KV[1]] = dzkv.astype(BF16)
        dz_ref[:, Z_GB[0]:Z_GB[1]] = (dcv * y).astype(BF16)
        dz_ref[:, Z_GC[0]:Z_GC[1]] = (du * xin).astype(BF16)
        dz_ref[:, Z_XIN[0]:Z_XIN[1]] = (du * gc).astype(BF16)
        dz_ref[:, Z_KPE[0]:Z_KPE[1]] = dkpe.astype(BF16)
        dx, dg = _rms_bwd(xv, gmix[...], _mm_nt(dz_ref[...], win[...]))
        dx_ref[...] = dx1_ref[...] + dx
        _accumulate(dgmix_ref, first, dg)

    prev_halo = lambda n: pl.BlockSpec((HALO, n), lambda i: (jnp.maximum(i * hb_per_tile - 1, 0), 0))
    next_halo = lambda n: pl.BlockSpec((HALO, n), lambda i: (jnp.minimum((i + 1) * hb_per_tile, n_halo - 1), 0))
    consts, const_specs = _operands([w["g_mix"], w["w_in"], w["g_q_lat"], w["w_uq"], w["g_kv_lat"], w["w_ukv"],
                                     w["g_qn"], w["g_kn"], w["conv_w"], tabs["gm"]])
    in_specs = ([_rows(tm, d), _rows(tm, Z_COLS), prev_halo(Z_COLS), next_halo(Z_COLS), _rows(tm, d), _rows(tm, hp),
                 _rows(tm, hp), _rows(tm, hp), _rows(tm, CONV_WIDTH), next_halo(CONV_WIDTH)]
                + const_specs + [_rows(tm, HEAD_PAD)] * 2)
    out_shape = (jax.ShapeDtypeStruct((t_len, d), F32), jax.ShapeDtypeStruct((t_len, Z_COLS), BF16),
                 jax.ShapeDtypeStruct((t_len, d), BF16), jax.ShapeDtypeStruct((t_len, Q_LORA), BF16),
                 jax.ShapeDtypeStruct((t_len, KV_LORA), BF16), jax.ShapeDtypeStruct((t_len, hp), BF16),
                 jax.ShapeDtypeStruct((t_len, hp), BF16),
                 jax.ShapeDtypeStruct((1, d), F32), jax.ShapeDtypeStruct((1, Q_LORA), F32),
                 jax.ShapeDtypeStruct((1, KV_LORA), F32), jax.ShapeDtypeStruct((1, LANES), F32),
                 jax.ShapeDtypeStruct((1, LANES), F32), jax.ShapeDtypeStruct((HALO, CONV_WIDTH), F32))
    out_specs = tuple(_rows(tm, s.shape[1]) for s in out_shape[:7]) + tuple(_whole(s.shape) for s in out_shape[7:])
    conv_scratch = [pltpu.VMEM((tm + HALO, CONV_WIDTH), F32), pltpu.VMEM((tm + HALO, CONV_WIDTH), F32)]
    args = (x, z, z, z, dx1, dqf, dkf, dkv_in, dconv, dconv, *consts, tabs["cos"], tabs["sin"])
    return _call_hosting_scatter("front_bwd", body, nt, in_specs, out_specs, out_shape, conv_scratch, args, scatter)


def _wgrad(a, b, shard_cols=None, shard_rows=None, out_dtype=BF16):
    t_len, kk = a.shape
    nn = b.shape[1]
    tk = min(kk, WGRAD_TILE)
    tn = next(c for c in range(min(nn, WGRAD_TILE), 0, -LANES) if nn % c == 0 and c % (shard_cols or LANES) == 0)
    tt = min(t_len, WGRAD_TOKENS)
    nt = t_len // tt
    per_block = tn // shard_cols if shard_cols else tk // shard_rows if shard_rows else 1

    def body(a_ref, b_ref, o_ref, acc):
        t = pl.program_id(2)

        @pl.when(t == 0)
        def _():
            acc[...] = jnp.zeros_like(acc)
        acc[...] += _mm_tn(a_ref[...].astype(BF16), b_ref[...].astype(BF16))

        @pl.when(t == nt - 1)
        def _():
            if shard_cols:
                for s in range(per_block):
                    o_ref[s] = acc[:, s * shard_cols:(s + 1) * shard_cols].astype(out_dtype)
            elif shard_rows:
                for s in range(per_block):
                    o_ref[s] = acc[s * shard_rows:(s + 1) * shard_rows, :].astype(out_dtype)
            else:
                o_ref[...] = acc[...].astype(out_dtype)

    if shard_cols:
        out_shape = jax.ShapeDtypeStruct((nn // shard_cols, kk, shard_cols), out_dtype)
        out_spec = pl.BlockSpec((per_block, tk, shard_cols), lambda i, j, t: (j, i, 0))
    elif shard_rows:
        out_shape = jax.ShapeDtypeStruct((kk // shard_rows, shard_rows, nn), out_dtype)
        out_spec = pl.BlockSpec((per_block, shard_rows, tn), lambda i, j, t: (i, 0, j))
    else:
        out_shape = jax.ShapeDtypeStruct((kk, nn), out_dtype)
        out_spec = pl.BlockSpec((tk, tn), lambda i, j, t: (i, j))
    return pl.pallas_call(body, name="wgrad", grid=(kk // tk, nn // tn, nt),
                          in_specs=[pl.BlockSpec((tt, tk), lambda i, j, t: (t, i)),
                                    pl.BlockSpec((tt, tn), lambda i, j, t: (t, j))],
                          out_specs=out_spec, out_shape=out_shape, scratch_shapes=[pltpu.VMEM((tk, tn), F32)],
                          compiler_params=_cparams(("parallel", "parallel", "arbitrary")))(a, b)


def _my_place():
    return lax.axis_index("x"), lax.axis_index("y"), lax.axis_index("c")


def _any_specs(n):
    return [pl.BlockSpec(memory_space=pl.ANY)] * n


def _all_gather(blocks):
    n = len(blocks)

    def body(*refs):
        start, forward, finish = _gather_phases(refs[:n], refs[n:2 * n], *refs[2 * n:])
        start()
        forward()
        finish()

    return pl.pallas_call(body, name="all_gather", out_shape=_gather_out_shape(blocks), in_specs=_any_specs(n),
                          out_specs=tuple(_any_specs(n)), scratch_shapes=_gather_semaphores(n))(*blocks)


def _gather_out_shape(blocks):
    return tuple(jax.ShapeDtypeStruct((N_DEV,) + b.shape, b.dtype) for b in blocks)


def _gather_semaphores(n):
    return [pltpu.SemaphoreType.DMA((n, 7)), pltpu.SemaphoreType.DMA((n, 7)), pltpu.SemaphoreType.DMA((n,))]


def _gather_phases(x_refs, out_refs, send_sems, recv_sems, local_sems):
    n = len(x_refs)
    x, y, c = _my_place()
    me, sibling = (x, y, c), (x, y, 1 - c)
    chips = [(1 - x, y), (x, 1 - y), (1 - x, 1 - y)]

    def slot(a, px, py, pc):
        return out_refs[a].at[4 * px + 2 * py + pc]

    def copy(a, k, blk, to, src=None):
        return pltpu.make_async_remote_copy(src_ref=slot(a, *blk) if src is None else src, dst_ref=slot(a, *blk),
                                            send_sem=send_sems.at[a, k], recv_sem=recv_sems.at[a, k],
                                            device_id=to, device_id_type=MESH)

    def own(a):
        return pltpu.make_async_copy(x_refs[a], slot(a, *me), local_sems.at[a])

    def first_hop(a):
        return [copy(a, 0, me, sibling, src=x_refs[a])] + [copy(a, 1 + j, me, (*chip, c), src=x_refs[a])
                                                           for j, chip in enumerate(chips)]

    def passed_on(a):
        return [copy(a, 4 + j, (*chip, c), sibling) for j, chip in enumerate(chips)]

    def start():
        for a in range(n):
            own(a).start()
        for a in range(n):
            for cp in first_hop(a):
                cp.start()

    def forward():
        for j, chip in enumerate(chips):
            for a in range(n):
                copy(a, 1 + j, (*chip, c), me).wait_recv()
                passed_on(a)[j].start()

    def finish():
        for a in range(n):
            copy(a, 0, sibling, me).wait_recv()
        for j, chip in enumerate(chips):
            for a in range(n):
                copy(a, 4 + j, (*chip, 1 - c), me).wait_recv()
        for a in range(n):
            for cp in first_hop(a) + passed_on(a):
                cp.wait_send()
            own(a).wait()

    return start, forward, finish


def _scatter_exchange(parts, landed, layers):
    n = len(parts)

    def body(*refs):
        start, finish = _scatter_phases(refs[:n], refs[2 * n:3 * n], *refs[3 * n:], layers)
        start()
        finish()

    return pl.pallas_call(body, name="scatter_exchange", out_shape=_same_shapes(landed), in_specs=_any_specs(2 * n),
                          out_specs=tuple(_any_specs(n)), scratch_shapes=_scatter_semaphores(n),
                          input_output_aliases={n + a: a for a in range(n)})(*parts, *landed)


def _same_shapes(arrays):
    return tuple(jax.ShapeDtypeStruct(a.shape, a.dtype) for a in arrays)


def _scatter_semaphores(n):
    return [pltpu.SemaphoreType.DMA((n, N_DEV - 1)), pltpu.SemaphoreType.DMA((n, N_DEV - 1)), pltpu.SemaphoreType.DMA((n,))]


def _scatter_phases(part_refs, landed_refs, send_sems, recv_sems, local_sems, layers):
    n = len(part_refs)
    x, y, c = _my_place()
    flips = [(0, 0, 1), (1, 0, 0), (0, 1, 0), (1, 1, 0), (1, 0, 1), (0, 1, 1), (1, 1, 1)]
    peers = [((1 - x) if fx else x, (1 - y) if fy else y, (1 - c) if fc else c) for fx, fy, fc in flips]
    my_k = 4 * x + 2 * y + c

    def index(peer):
        return 4 * peer[0] + 2 * peer[1] + peer[2]

    def send(a, r):
        return pltpu.make_async_remote_copy(src_ref=part_refs[a].at[index(peers[r])], dst_ref=landed_refs[a].at[my_k, layers[a]],
                                            send_sem=send_sems.at[a, r], recv_sem=recv_sems.at[a, r],
                                            device_id=peers[r], device_id_type=MESH)

    def arrival(a, r):
        return pltpu.make_async_remote_copy(src_ref=part_refs[a].at[my_k], dst_ref=landed_refs[a].at[index(peers[r]), layers[a]],
                                            send_sem=send_sems.at[a, r], recv_sem=recv_sems.at[a, r],
                                            device_id=peers[r], device_id_type=MESH)

    def own(a):
        return pltpu.make_async_copy(part_refs[a].at[my_k], landed_refs[a].at[my_k, layers[a]], local_sems.at[a])

    def start():
        for a in range(n):
            own(a).start()
        for r in range(len(peers)):
            for a in range(n):
                send(a, r).start()

    def finish():
        for r in range(len(peers)):
            for a in range(n):
                arrival(a, r).wait_recv()
        for r in range(len(peers)):
            for a in range(n):
                send(a, r).wait_send()
        for a in range(n):
            own(a).wait()

    return start, finish


def _row_block(rows):
    return ROW_BLOCK if rows % ROW_BLOCK == 0 else rows


def _sum_leading(parts):
    n_part, shape = parts.shape[0], parts.shape[1:]
    rows, cols = shape[-2:]
    rb = _row_block(rows)

    def body(p_ref, o_ref):
        acc = p_ref[0].astype(F32)
        for k in range(1, n_part):
            acc = acc + p_ref[k].astype(F32)
        o_ref[...] = acc

    if len(shape) == 3:
        grid = (shape[0], rows // rb)
        in_spec = pl.BlockSpec((n_part, None, rb, cols), lambda l, i: (0, l, i, 0))
        out_spec = pl.BlockSpec((None, rb, cols), lambda l, i: (l, i, 0))
    else:
        grid = (rows // rb,)
        in_spec, out_spec = pl.BlockSpec((n_part, rb, cols), lambda i: (0, i, 0)), _rows(rb, cols)
    return pl.pallas_call(body, name="sum_leading", grid=grid, in_specs=[in_spec], out_specs=out_spec,
                          out_shape=jax.ShapeDtypeStruct(shape, F32),
                          compiler_params=_cparams(("parallel",) * len(grid)))(parts)


def _adamw(w, g, m, v):
    shape = w.shape
    rows, cols = shape[-2:]
    rb = _row_block(rows)

    def body(w_ref, g_ref, m_ref, v_ref, d_ref, nm_ref, nv_ref):
        gv = g_ref[...]
        nm = ADAM_B1 * m_ref[...] + (1.0 - ADAM_B1) * gv
        nv = ADAM_B2 * v_ref[...] + (1.0 - ADAM_B2) * jnp.square(gv)
        m_hat = nm / (1.0 - ADAM_B1 ** ADAM_STEP)
        v_hat = nv / (1.0 - ADAM_B2 ** ADAM_STEP)
        d_ref[...] = -ADAM_LR * (m_hat / (jnp.sqrt(v_hat) + ADAM_EPS) + ADAM_WD * w_ref[...])
        nm_ref[...] = nm
        nv_ref[...] = nv

    if len(shape) == 3:
        grid, spec = (shape[0], rows // rb), pl.BlockSpec((None, rb, cols), lambda l, i: (l, i, 0))
    else:
        grid, spec = (rows // rb,), _rows(rb, cols)
    out = jax.ShapeDtypeStruct(shape, F32)
    return pl.pallas_call(body, name="adamw", grid=grid, in_specs=[spec] * 4, out_specs=(spec,) * 3,
                          out_shape=(out,) * 3, compiler_params=_cparams(("parallel",) * len(grid)))(w, g, m, v)


def _rope_tables(positions):
    t_len = positions.shape[0]
    inv_freq = 1.0 / (ROPE_THETA ** (jnp.arange(0, QK_ROPE, 2, dtype=F32) / QK_ROPE))
    ang = positions.astype(F32)[:, None] * inv_freq
    c, s = jnp.cos(ang), jnp.sin(ang)
    one, zero = jnp.ones((t_len, QK_NOPE), F32), jnp.zeros((t_len, QK_NOPE), F32)
    cos = jnp.concatenate([one, c, c, one[:, :LANES - QK_HEAD]], axis=1)
    sin = jnp.concatenate([zero, -s, s, zero[:, :LANES - QK_HEAD]], axis=1)
    idx = jnp.arange(PAIR * HEAD_PAD)
    lane, head = idx % HEAD_PAD, idx // HEAD_PAD
    grp = jnp.where(lane < QK_NOPE, 0, jnp.where(lane < QK_HEAD, 1, 2)) + 3 * head
    val = jnp.where(lane < QK_NOPE, 1.0 / QK_NOPE, jnp.where(lane < QK_HEAD, 1.0 / QK_ROPE, 0.0))
    gm = jnp.where(grp[:, None] == grp[None, :], val[None, :], 0.0).astype(BF16)
    return {"cos": cos, "sin": sin, "gm": gm}


def _head_gain(g_nope, g_rope):
    one = jnp.concatenate([g_nope, g_rope, jnp.zeros((HEAD_PAD - QK_HEAD,), F32)])
    return jnp.concatenate([one] * PAIR).reshape(1, PAIR * HEAD_PAD)


def _head_pairs(w):
    return jnp.concatenate([w[k::PAIR] for k in range(PAIR)], axis=2)


def _padded_w_in(shards):
    width = shards.shape[2]
    zeros = jnp.zeros((shards.shape[1], QK_NOPE), shards.dtype)

    def natural(start, end):
        pieces = []
        for k in range(N_DEV):
            lo, hi = max(start, k * width), min(end, (k + 1) * width)
            if lo < hi:
                pieces.append(shards[k][:, lo - k * width:hi - k * width])
        return pieces

    o2, o3 = Q_LORA + KV_LORA, Q_LORA + KV_LORA + QK_ROPE
    return jnp.concatenate(natural(0, o2) + natural(o3, N_DEV * width) + [zeros] + natural(o2, o3)
                           + [zeros[:, :LANES - QK_HEAD]], axis=1)


def _w_in_grad_shards(d_in):
    o2, o3 = Q_LORA + KV_LORA, Q_LORA + KV_LORA + QK_ROPE
    width = (Z_XIN[1] + QK_ROPE) // N_DEV
    runs = [(0, o2, 0), (o2, o3, Z_KPE[0] + QK_NOPE), (o3, N_DEV * width, o2)]
    shards = []
    for k in range(N_DEV):
        pieces = []
        for start, end, at in runs:
            lo, hi = max(start, k * width), min(end, (k + 1) * width)
            if lo < hi:
                pieces.append(d_in[:, at + lo - start:at + hi - start])
        shards.append(pieces[0] if len(pieces) == 1 else jnp.concatenate(pieces, axis=1))
    return jnp.stack(shards)


def kernel(x, p, positions, g_mix, w_in, g_q_lat, w_uq, g_kv_lat, w_ukv, g_qn_nope, g_qn_rope, g_kn_nope, g_kn_rope, conv_w, g_out_attn, g_out_conv, w_o, g_mlp, w_up, w_down, g_ple, w_ple_gate, w_ple, loss_target, m_g_mix, m_w_in, m_g_q_lat, m_w_uq, m_g_kv_lat, m_w_ukv, m_g_qn_nope, m_g_qn_rope, m_g_kn_nope, m_g_kn_rope, m_conv_w, m_g_out_attn, m_g_out_conv, m_w_o, m_g_mlp, m_w_up, m_w_down, m_g_ple, m_w_ple_gate, m_w_ple, v_g_mix, v_w_in, v_g_q_lat, v_w_uq, v_g_kv_lat, v_w_ukv, v_g_qn_nope, v_g_qn_rope, v_g_kn_nope, v_g_kn_rope, v_conv_w, v_g_out_attn, v_g_out_conv, v_w_o, v_g_mlp, v_w_up, v_w_down, v_g_ple, v_w_ple_gate, v_w_ple):
    given = dict(locals())
    weights = {n: given[n] for n in WEIGHT_NAMES}
    gains = {n: given[n] for n in GAIN_NAMES}
    depth = w_in.shape[0]
    xs, target = x[0], loss_target[0]
    d_model = xs.shape[1]
    uq_cols = w_uq.shape[2]
    n_taps = conv_w.shape[1]

    mat_names = [n for n in SHARD_NAMES if n != "conv_w"]
    local = [weights[n].astype(BF16) for n in mat_names]
    local[1] = jnp.pad(local[1], ((0, 0), (0, 0), (0, HEAD_PAD - uq_cols)))
    local = dict(zip(mat_names, local))
    front_names = ("w_in", "w_uq", "w_ukv")
    first = _all_gather([local[n][0] for n in front_names] + [conv_w])
    conv_full = jnp.transpose(first[-1], (1, 2, 0, 3)).reshape(depth, n_taps, -1)
    tabs = _rope_tables(positions[0])

    def front_weights(layer, full):
        lw = {n: gains[n][layer].reshape(1, -1) for n in GAIN_NAMES}
        lw.update({"w_in": _padded_w_in(full["w_in"]), "w_uq": _head_pairs(full["w_uq"]),
                   "w_ukv": _head_pairs(full["w_ukv"]),
                   "conv_w": jnp.pad(conv_full[layer], ((0, HALO - n_taps), (0, 0))),
                   "g_qn": _head_gain(g_qn_nope[layer], g_qn_rope[layer]),
                   "g_kn": _head_gain(g_kn_nope[layer], g_kn_rope[layer])})
        return lw

    def rest_weights(full):
        return {"w_ple": full["w_ple"], "w_up": full["w_up"], "w_down": full["w_down"],
                "w_o": full["w_o"], "w_ple_gate": full["w_ple_gate"]}

    saved, layer_w = [], []
    cur = xs
    gathered = dict(zip(front_names, first[:-1]))
    mlp_names = ("w_up", "w_down")
    for layer in range(depth):
        w = front_weights(layer, gathered)
        own_small = [n for n in mat_names if n not in front_names + mlp_names] if layer == 0 else []
        (z, qf, kf, kv, conv), got = _front_fwd(cur, w, tabs, gather=[(local[n], 0) for n in own_small])
        gathered.update(zip(own_small, got))
        lots = [[n for n in mat_names if n in mlp_names], [n for n in mat_names if n not in mlp_names]]
        behind_attn, behind_mlp = (lots[1], lots[0]) if layer == 0 else lots
        if layer + 1 == depth:
            behind_attn, behind_mlp = [], []
        wanted = [(n, 0) for n in mlp_names] if layer == 0 else []
        wanted += [(n, layer + 1) for n in behind_attn]
        attn, lse, got = _attn_fwd(qf, kf, kv, gather=[(local[n], at) for n, at in wanted])
        coming = {}
        for (n, at), g in zip(wanted, got):
            (gathered if at == layer else coming)[n] = g
        w.update(rest_weights(gathered))
        layer_w.append(w)
        x1 = _mix_out_fwd(cur, attn, conv, w)
        x2, got = _mlp_fwd(x1, w, gather=[(local[n], layer + 1) for n in behind_mlp])
        coming.update(zip(behind_mlp, got))
        gathered = coming
        x3 = _ple_fwd(x2, p[layer, 0], w)
        saved.append(dict(x=cur, z=z, qf=qf, kf=kf, kv=kv, conv=conv, attn=attn, lse=lse, x1=x1, x2=x2))
        cur = x3

    sq, dx = _loss_and_grad(cur, target)
    loss = lax.psum(0.5 / d_model * sq[0, 0], ("x", "y", "c"))

    landed = {n: lax.empty((N_DEV, depth) + weights[n].shape[1:], BF16) for n in SHARD_NAMES}
    gain_grads = [None] * depth
    late = {}
    for layer in reversed(range(depth)):
        w, s = layer_w[layer], saved[layer]
        pl_in = p[layer, 0]
        dx2, de, h3, dpre, dg_ple = _ple_bwd(dx, s["x2"], pl_in, w)
        dx1, r, da, h2, dg_mlp = _mlp_bwd(dx2, s["x1"], w)
        mixed, dattn, dconv, dg_oa, dg_oc = _mix_out_bwd(dx1, s["attn"], s["conv"], w)
        def exchange_of(sending):
            return ([v[0] for v in sending.values()], [landed[n] for n in sending], [v[1] for v in sending.values()])

        big = {"w_up": (_wgrad(h2, da, shard_cols=w_up.shape[2]), layer),
               "w_down": (_wgrad(r, dx2, shard_rows=w_down.shape[1]), layer)}
        small = {"w_o": (_wgrad(mixed, dx1, shard_rows=w_o.shape[1]), layer),
                 "w_ple_gate": (_wgrad(h3, dpre, shard_rows=w_ple_gate.shape[1]), layer),
                 "w_ple": (_wgrad(pl_in, de, shard_cols=w_ple.shape[2]), layer), **late}
        dqf, dkf, dkv, got = _attn_bwd(s["qf"], s["kf"], s["kv"], s["attn"], dattn, s["lse"], exchange_of(big))
        landed.update(zip(big, got))
        (dx0, dz, hb, qn, kvn, dqr, dkvr, dg_mix, dg_q, dg_kv, dg_qn, dg_kn, dcw), got = _front_bwd(
            s["x"], s["z"], dx1, dqf, dkf, dkv, dconv, w, tabs, exchange_of(small))
        landed.update(zip(small, got))
        late = {"w_in": (_w_in_grad_shards(_wgrad(hb, dz, out_dtype=F32)).astype(BF16), layer),
                "w_uq": (_wgrad(qn, dqr, shard_cols=HEAD_PAD)[..., :uq_cols], layer),
                "w_ukv": (_wgrad(kvn, dkvr, shard_cols=HEAD_PAD), layer),
                "conv_w": (jnp.transpose(dcw[:n_taps].reshape(n_taps, N_DEV, -1), (1, 0, 2)).astype(BF16), layer)}
        gain_grads[layer] = jnp.concatenate([
            dg_mix[0], dg_q[0], dg_kv[0], dg_qn[0, :QK_NOPE], dg_qn[0, QK_NOPE:QK_HEAD], dg_kn[0, :QK_NOPE],
            dg_kn[0, QK_NOPE:QK_HEAD], dg_oa[0], dg_oc[0], dg_mlp[0], dg_ple[0]])
        dx = dx0
    names = list(late)
    landed.update(zip(names, _scatter_exchange([late[n][0] for n in names], [landed[n] for n in names],
                                               [late[n][1] for n in names])))
    grads = {n: _sum_leading(landed[n]) for n in SHARD_NAMES}

    gg = jnp.stack(gain_grads)
    gg_rows = -(-gg.size // (HALO * LANES)) * HALO
    gg_pad = jnp.pad(gg.reshape(-1), (0, gg_rows * LANES - gg.size)).reshape(gg_rows, LANES)
    gg_sum = _sum_leading(_all_gather([gg_pad])[0]).reshape(-1)[:gg.size].reshape(gg.shape)
    off = 0
    for n in GAIN_NAMES:
        width = gains[n].shape[1]
        grads[n] = gg_sum[:, off:off + width]
        off += width

    deltas, new_m, new_v = {}, {}, {}
    for n in WEIGHT_NAMES:
        deltas[n], new_m[n], new_v[n] = _adamw(weights[n], grads[n], given["m_" + n], given["v_" + n])
    return (loss, dx[None], *[grads[n] for n in WEIGHT_NAMES], *[deltas[n] for n in WEIGHT_NAMES],
            *[new_m[n] for n in WEIGHT_NAMES], *[new_v[n] for n in WEIGHT_NAMES])
```

```python
import functools

import jax
import jax.numpy as jnp
from jax import lax
from jax.experimental import pallas as pl
from jax.experimental.pallas import tpu as pltpu

F32 = jnp.float32
BF16 = jnp.bfloat16
MESH = pl.DeviceIdType.MESH

N_HEADS = 8
QK_NOPE = 64
QK_ROPE = 32
QK_HEAD = QK_NOPE + QK_ROPE
V_HEAD = 64
HEAD_PAD = 128
PAIR = 2
ATTN_SCALE = QK_HEAD ** -0.5
Q_LORA = 384
KV_LORA = 256
CONV_WIDTH = 512
ATTN_WIDTH = N_HEADS * V_HEAD
ROPE_THETA = 10000.0
EPS = 1e-6
ADAM_LR, ADAM_B1, ADAM_B2, ADAM_EPS, ADAM_WD, ADAM_STEP = 0.001, 0.9, 0.999, 1e-08, 0.01, 10

Z_Q = (0, 384)
Z_KV = (384, 640)
Z_GB = (640, 1152)
Z_GC = (1152, 1664)
Z_XIN = (1664, 2176)
Z_KPE = (2176, 2304)
Z_COLS = 2304

N_DEV = 8
LANES = 128
V7X_VMEM_LIMIT = 52 * 1024 * 1024
TOKEN_TILE = 256
LIGHT_TOKEN_TILE = 512
ATTN_BLOCK = 512
ATTN_FWD_BLOCK = 512
ATTN_FWD_ROWS = 1024
ATTN_Q_SUB = 2
ATTN_KV_SUB = 16
ROW_BLOCK = 512
WGRAD_TOKENS = 2048
WGRAD_TILE = 1024
HALO = 8

GAIN_NAMES = ("g_mix", "g_q_lat", "g_kv_lat", "g_qn_nope", "g_qn_rope", "g_kn_nope", "g_kn_rope",
              "g_out_attn", "g_out_conv", "g_mlp", "g_ple")
SHARD_NAMES = ("w_in", "w_uq", "w_ukv", "conv_w", "w_o", "w_up", "w_down", "w_ple_gate", "w_ple")
WEIGHT_NAMES = ("g_mix", "w_in", "g_q_lat", "w_uq", "g_kv_lat", "w_ukv", "g_qn_nope", "g_qn_rope", "g_kn_nope",
                "g_kn_rope", "conv_w", "g_out_attn", "g_out_conv", "w_o", "g_mlp", "w_up", "w_down", "g_ple",
                "w_ple_gate", "w_ple")


def _cparams(semantics=None):
    return pltpu.CompilerParams(dimension_semantics=semantics, vmem_limit_bytes=V7X_VMEM_LIMIT)


def _mm(a, b):
    return jnp.dot(a, b, preferred_element_type=F32)


def _mm_nt(a, b):
    return lax.dot_general(a, b, (((1,), (1,)), ((), ())), preferred_element_type=F32)


def _mm_tn(a, b):
    return lax.dot_general(a, b, (((0,), (0,)), ((), ())), preferred_element_type=F32)


def _rms(x, g):
    r = lax.rsqrt(jnp.mean(x * x, axis=-1, keepdims=True) + EPS)
    return (x * r) * g


def _rms_bwd(x, g, dy):
    r = lax.rsqrt(jnp.mean(x * x, axis=-1, keepdims=True) + EPS)
    xh = x * r
    dg = jnp.sum(dy * xh, axis=0, keepdims=True)
    dyg = dy * g
    dx = r * (dyg - xh * jnp.mean(dyg * xh, axis=-1, keepdims=True))
    return dx, dg


def _group_mean(t, gm):
    hi = t.astype(BF16)
    lo = (t - hi.astype(F32)).astype(BF16)
    return _mm(hi, gm) + _mm(lo, gm)


def _swap_rope_halves(x, lane):
    half = QK_ROPE // 2
    swapped = jnp.where(lane < QK_NOPE + half, pltpu.roll(x, x.shape[1] - half, 1), pltpu.roll(x, half, 1))
    return jnp.where((lane >= QK_NOPE) & (lane < QK_HEAD), swapped, 0.0)


def _qk_fwd(x, g, cos, sin, gm, lane):
    r = lax.rsqrt(_group_mean(x * x, gm) + EPS)
    n = (x * r) * g
    return n * cos + _swap_rope_halves(n, lane) * sin


def _qk_bwd(x, g, dy, cos, sin, gm, lane):
    r = lax.rsqrt(_group_mean(x * x, gm) + EPS)
    xh = x * r
    dn = dy * cos + _swap_rope_halves(dy * sin, lane)
    dg = jnp.sum(dn * xh, axis=0, keepdims=True)
    dng = dn * g
    dx = r * (dng - xh * _group_mean(dng * xh, gm))
    return dx, dg


def _row_shards_joined(ref):
    n, rows, cols = ref.shape
    return ref[...].reshape(n * rows, cols)


def _rows(tm, n):
    return pl.BlockSpec((tm, n), lambda i: (i, 0))


def _whole(shape):
    zeros = (0,) * len(shape)
    return pl.BlockSpec(shape, lambda i: zeros)


def _operands(arrays):
    return list(arrays), [_whole(a.shape) for a in arrays]


def _accumulate(ref, first, value):
    @pl.when(first)
    def _():
        ref[...] = jnp.zeros_like(ref)
    ref[...] += value


def _front_fwd(x, w, tabs, gather=None):
    t_len, d = x.shape
    tm = min(TOKEN_TILE, t_len)
    hp = N_HEADS * HEAD_PAD

    def body(x_ref, gmix, win, gq, wuq, gkv, wukv, gqn, gkn, cw_ref, gm_ref, cos_ref, sin_ref,
             z_ref, qf_ref, kf_ref, kv_ref, conv_ref, ubuf):
        i = pl.program_id(0)
        h = _rms(x_ref[...], gmix[...])
        z = _mm(h.astype(BF16), win[...])
        z_ref[...] = z
        qnb = _rms(z[:, Z_Q[0]:Z_Q[1]], gq[...]).astype(BF16)
        kvb = _rms(z[:, Z_KV[0]:Z_KV[1]], gkv[...]).astype(BF16)
        kpe = z[:, Z_KPE[0]:Z_KPE[1]]
        kpe = jnp.concatenate([kpe] * PAIR, axis=1)
        cos, sin = (jnp.concatenate([t[...]] * PAIR, axis=1) for t in (cos_ref, sin_ref))
        gm = gm_ref[...]
        lane = lax.broadcasted_iota(jnp.int32, (tm, PAIR * HEAD_PAD), 1) & (HEAD_PAD - 1)
        for pr in range(N_HEADS // PAIR):
            sl = slice(pr * PAIR * HEAD_PAD, (pr + 1) * PAIR * HEAD_PAD)
            qf_ref[:, sl] = (_qk_fwd(_mm(qnb, wuq[pr]), gqn[...], cos, sin, gm, lane) * ATTN_SCALE).astype(BF16)
            kv = _mm(kvb, wukv[pr])
            kv_ref[:, sl] = jnp.where(lane < QK_NOPE, jnp.where(lane == 0, 1.0, 0.0), kv).astype(BF16)
            kf_ref[:, sl] = _qk_fwd(jnp.where(lane < QK_NOPE, kv, 0.0) + kpe, gkn[...], cos, sin, gm, lane).astype(BF16)
        u = z[:, Z_GC[0]:Z_GC[1]] * z[:, Z_XIN[0]:Z_XIN[1]]

        @pl.when(i == 0)
        def _():
            ubuf[0:HALO, :] = jnp.zeros((HALO, CONV_WIDTH), F32)
        ubuf[HALO:HALO + tm, :] = u
        cw = cw_ref[...]
        y = cw[0:1] * u + cw[1:2] * ubuf[pl.ds(HALO - 1, tm), :] + cw[2:3] * ubuf[pl.ds(HALO - 2, tm), :]
        conv_ref[...] = z[:, Z_GB[0]:Z_GB[1]] * y
        ubuf[0:HALO, :] = u[tm - HALO:tm, :]

    consts, const_specs = _operands([w["g_mix"], w["w_in"], w["g_q_lat"], w["w_uq"], w["g_kv_lat"], w["w_ukv"],
                                     w["g_qn"], w["g_kn"], w["conv_w"], tabs["gm"]])
    out_shape = (jax.ShapeDtypeStruct((t_len, Z_COLS), F32), jax.ShapeDtypeStruct((t_len, hp), BF16),
                 jax.ShapeDtypeStruct((t_len, hp), BF16), jax.ShapeDtypeStruct((t_len, hp), BF16),
                 jax.ShapeDtypeStruct((t_len, CONV_WIDTH), F32))
    return _call_hosting_gather("front_fwd", body, t_len // tm, [_rows(tm, d)] + const_specs + [_rows(tm, HEAD_PAD)] * 2,
                                tuple(_rows(tm, s.shape[1]) for s in out_shape), out_shape,
                                [pltpu.VMEM((tm + HALO, CONV_WIDTH), F32)], (x, *consts, tabs["cos"], tabs["sin"]), gather)


def _attn_fwd(qf, kf, kv, gather=None):
    t_len = qf.shape[0]
    blk = min(ATTN_FWD_BLOCK, t_len)
    bq = blk // ATTN_Q_SUB
    span = min(ATTN_FWD_ROWS, t_len)
    nb = t_len // span
    n_sub = span // bq
    chains = [(hh, a) for hh in range(2) for a in range(n_sub)]

    def body(q_ref, k_ref, kv_ref, o_ref, lse_ref):
        lane = lax.broadcasted_iota(jnp.int32, (bq, LANES), 1)
        row = lax.broadcasted_iota(jnp.int32, (bq, blk), 0)
        col = lax.broadcasted_iota(jnp.int32, (bq, blk), 1)

        def head_cols(hh):
            return slice(hh * HEAD_PAD, (hh + 1) * HEAD_PAD)

        def softmax_step(s, kvv, state, first_row=None):
            m, acc = state
            if first_row is not None:
                s = jnp.where(col <= row + first_row, s, -jnp.inf)
            m_new = jnp.maximum(m, jnp.max(s, axis=-1, keepdims=True))
            p = jnp.exp(s - m_new)
            acc = jnp.exp(m - m_new) * acc + _mm(p.astype(BF16), kvv)
            return m_new, acc

        def finish(state):
            m, acc = state
            l = jnp.sum(jnp.where(lane == 0, acc, 0.0), axis=-1, keepdims=True)
            return acc / l, jnp.broadcast_to(m + jnp.log(l), (bq, LANES))

        def qblock(i):
            rows = [pl.ds(a * bq, bq) for a in range(n_sub)]
            first = [i * span + a * bq for a in range(n_sub)]
            states = [(jnp.full((bq, 1), -jnp.inf, F32), jnp.zeros((bq, LANES), F32))] * len(chains)
            steps = []
            for j in range((first[-1] + bq - 1) // blk + 1):
                for n, (hh, a) in enumerate(chains):
                    if j * blk <= first[a] + bq - 1:
                        diagonal = (j + 1) * blk - 1 > first[a]
                        steps.append((n, j, first[a] - j * blk if diagonal else None))
            ahead = len(chains)
            scores = []
            for t in range(len(steps) + ahead):
                if t < len(steps):
                    n, j, _ = steps[t]
                    hh, a = chains[n]
                    scores.append(_mm_nt(q_ref[rows[a], head_cols(hh)], k_ref[pl.ds(j * blk, blk), head_cols(hh)]))
                if t >= ahead:
                    n, j, offset = steps[t - ahead]
                    states[n] = softmax_step(scores[t - ahead], kv_ref[pl.ds(j * blk, blk), head_cols(chains[n][0])],
                                             states[n], offset)
            for a in range(n_sub):
                (o0, lse0), (o1, lse1) = finish(states[chains.index((0, a))]), finish(states[chains.index((1, a))])
                o_ref[rows[a], :] = jnp.where(lane < V_HEAD, pltpu.roll(o0, V_HEAD, 1), o1)
                lse_ref[rows[a], head_cols(0)] = lse0
                lse_ref[rows[a], head_cols(1)] = lse1

        mine = pl.program_id(0) % nb
        for i in range(nb):
            pl.when(mine == i)(functools.partial(qblock, i))

    q_spec = pl.BlockSpec((span, 2 * HEAD_PAD), lambda g: (g % nb, g // nb))
    keys = pl.BlockSpec((t_len, 2 * HEAD_PAD), lambda g: (0, g // nb))
    o_spec = pl.BlockSpec((span, 2 * V_HEAD), lambda g: (g % nb, g // nb))
    out_shape = (jax.ShapeDtypeStruct((t_len, ATTN_WIDTH), F32), jax.ShapeDtypeStruct((t_len, N_HEADS * LANES), F32))
    (attn, lse), gathered = _call_hosting_gather("attn_fwd", body, N_HEADS // 2 * nb, [q_spec, keys, keys],
                                                 (o_spec, q_spec), out_shape, [], (qf, kf, kv), gather)
    return attn, lse, gathered


def _call_hosting_gather(name, body, n_steps, in_specs, out_specs, out_shape, scratch_shapes, args, gather):
    if not gather:
        return pl.pallas_call(body, name=name, grid=(n_steps,), in_specs=list(in_specs), out_specs=tuple(out_specs),
                              out_shape=tuple(out_shape), scratch_shapes=list(scratch_shapes),
                              compiler_params=_cparams(("arbitrary",)))(*args), ()
    n_in, n_out, n_g = len(in_specs), len(out_shape), len(gather)

    def hosting_body(*refs):
        ins, refs = refs[:n_in], refs[n_in:]
        x_refs = [r.at[layer] for r, (_, layer) in zip(refs[:n_g], gather)]
        outs, landing, sems, scratch = (refs[n_g:n_g + n_out], refs[n_g + n_out:2 * n_g + n_out],
                                        refs[2 * n_g + n_out:2 * n_g + n_out + 3], refs[2 * n_g + n_out + 3:])
        start, forward, finish = _gather_phases(x_refs, landing, *sems)
        step = pl.program_id(0)
        pl.when(step == 0)(start)
        pl.when(step == n_steps - 1)(forward)
        body(*ins, *outs, *scratch)
        pl.when(step == n_steps - 1)(finish)

    res = pl.pallas_call(hosting_body, name=name + "_gather", grid=(n_steps,), in_specs=list(in_specs) + _any_specs(n_g),
                         out_specs=tuple(out_specs) + tuple(_any_specs(n_g)),
                         out_shape=tuple(out_shape) + _gather_out_shape([s[layer] for s, layer in gather]),
                         scratch_shapes=_gather_semaphores(n_g) + list(scratch_shapes),
                         compiler_params=_cparams(("arbitrary",)))(*args, *[s for s, _ in gather])
    return res[:n_out], res[n_out:]


def _mix_out_fwd(x, attn, conv, w):
    t_len, d = x.shape
    tm = min(LIGHT_TOKEN_TILE, t_len)

    def body(x_ref, a_ref, c_ref, goa, goc, wo, x1_ref):
        mixed = jnp.concatenate([_rms(a_ref[...], goa[...]), _rms(c_ref[...], goc[...])], axis=1)
        x1_ref[...] = x_ref[...] + _mm(mixed.astype(BF16), _row_shards_joined(wo))

    consts, const_specs = _operands([w["g_out_attn"], w["g_out_conv"], w["w_o"]])
    return pl.pallas_call(body, name="mix_out_fwd", grid=(t_len // tm,),
                          in_specs=[_rows(tm, d), _rows(tm, ATTN_WIDTH), _rows(tm, CONV_WIDTH)] + const_specs,
                          out_specs=_rows(tm, d), out_shape=jax.ShapeDtypeStruct((t_len, d), F32),
                          compiler_params=_cparams(("parallel",)))(x, attn, conv, *consts)


def _mlp_fwd(x1, w, gather=None):
    t_len, d = x1.shape
    tm = min(TOKEN_TILE, t_len)

    def body(x_ref, g, wup, wdn, x2_ref):
        x1v = x_ref[...]
        hb = _rms(x1v, g[...]).astype(BF16)
        acc = x1v
        for k in range(N_DEV):
            a = jnp.maximum(_mm(hb, wup[k]), 0.0)
            acc = acc + _mm((a * a).astype(BF16), wdn[k])
        x2_ref[...] = acc

    consts, const_specs = _operands([w["g_mlp"], w["w_up"], w["w_down"]])
    (x2,), gathered = _call_hosting_gather("mlp_fwd", body, t_len // tm, [_rows(tm, d)] + const_specs, (_rows(tm, d),),
                                           (jax.ShapeDtypeStruct((t_len, d), F32),), [], (x1, *consts), gather)
    return x2, gathered


def _ple_fwd(x2, p, w):
    t_len, d = x2.shape
    tm = min(LIGHT_TOKEN_TILE, t_len)

    def body(x_ref, p_ref, g, wg, wp, x3_ref):
        x2v = x_ref[...]
        gate = jax.nn.sigmoid(_mm(_rms(x2v, g[...]).astype(BF16), _row_shards_joined(wg)))
        pb = p_ref[...].astype(BF16)
        e = jnp.concatenate([_mm(pb, wp[k]) for k in range(N_DEV)], axis=1)
        x3_ref[...] = x2v + gate * e

    consts, const_specs = _operands([w["g_ple"], w["w_ple_gate"], w["w_ple"]])
    return pl.pallas_call(body, name="ple_fwd", grid=(t_len // tm,),
                          in_specs=[_rows(tm, d), _rows(tm, p.shape[1])] + const_specs, out_specs=_rows(tm, d),
                          out_shape=jax.ShapeDtypeStruct((t_len, d), F32),
                          compiler_params=_cparams(("parallel",)))(x2, p, *consts)


def _loss_and_grad(y, target):
    t_len, d = y.shape
    tm = min(TOKEN_TILE, t_len)

    def body(y_ref, t_ref, sq_ref, dy_ref):
        err = y_ref[...] - t_ref[...]
        dy_ref[...] = err / d
        total = jnp.sum(jnp.sum(err * err, axis=0, keepdims=True), axis=1, keepdims=True)
        _accumulate(sq_ref, pl.program_id(0) == 0, jnp.broadcast_to(total, (HALO, LANES)))

    return pl.pallas_call(body, name="loss_grad", grid=(t_len // tm,), in_specs=[_rows(tm, d), _rows(tm, d)],
                          out_specs=(_whole((HALO, LANES)), _rows(tm, d)),
                          out_shape=(jax.ShapeDtypeStruct((HALO, LANES), F32), jax.ShapeDtypeStruct((t_len, d), F32)),
                          compiler_params=_cparams(("arbitrary",)))(y, target)


def _ple_bwd(dx3, x2, p, w):
    t_len, d = x2.shape
    tm = min(LIGHT_TOKEN_TILE, t_len)

    def body(dx3_ref, x_ref, p_ref, g, wg, wp, dx2_ref, de_ref, h3_ref, dpre_ref, dg_ref):
        x2v, dx3v = x_ref[...], dx3_ref[...]
        hb = _rms(x2v, g[...]).astype(BF16)
        h3_ref[...] = hb
        w_gate = _row_shards_joined(wg)
        gate = jax.nn.sigmoid(_mm(hb, w_gate))
        pb = p_ref[...].astype(BF16)
        e = jnp.concatenate([_mm(pb, wp[k]) for k in range(N_DEV)], axis=1)
        de_ref[...] = (dx3v * gate).astype(BF16)
        dpre = ((dx3v * e) * gate * (1.0 - gate)).astype(BF16)
        dpre_ref[...] = dpre
        dx, dg = _rms_bwd(x2v, g[...], _mm_nt(dpre, w_gate))
        dx2_ref[...] = dx3v + dx
        _accumulate(dg_ref, pl.program_id(0) == 0, dg)

    consts, const_specs = _operands([w["g_ple"], w["w_ple_gate"], w["w_ple"]])
    out_shape = (jax.ShapeDtypeStruct((t_len, d), F32), jax.ShapeDtypeStruct((t_len, d), BF16),
                 jax.ShapeDtypeStruct((t_len, d), BF16), jax.ShapeDtypeStruct((t_len, d), BF16),
                 jax.ShapeDtypeStruct((1, d), F32))
    return pl.pallas_call(body, name="ple_bwd", grid=(t_len // tm,),
                          in_specs=[_rows(tm, d), _rows(tm, d), _rows(tm, p.shape[1])] + const_specs,
                          out_specs=(_rows(tm, d),) * 4 + (_whole((1, d)),), out_shape=out_shape,
                          compiler_params=_cparams(("arbitrary",)))(dx3, x2, p, *consts)


def _mlp_bwd(dx2, x1, w):
    t_len, d = x1.shape
    tm = min(TOKEN_TILE, t_len)
    fc = w["w_up"].shape[2]
    ff = N_DEV * fc

    def body(dx2_ref, x_ref, g, wup, wdn, dx1_ref, r_ref, da_ref, h2_ref, dg_ref):
        x1v, dx2v = x_ref[...], dx2_ref[...]
        hb = _rms(x1v, g[...]).astype(BF16)
        h2_ref[...] = hb
        dxb = dx2v.astype(BF16)
        dh = jnp.zeros((tm, d), F32)
        for k in range(N_DEV):
            a = jnp.maximum(_mm(hb, wup[k]), 0.0)
            r_ref[:, k * fc:(k + 1) * fc] = (a * a).astype(BF16)
            da = (_mm_nt(dxb, wdn[k]) * (2.0 * a)).astype(BF16)
            da_ref[:, k * fc:(k + 1) * fc] = da
            dh = dh + _mm_nt(da, wup[k])
        dx, dg = _rms_bwd(x1v, g[...], dh)
        dx1_ref[...] = dx2v + dx
        _accumulate(dg_ref, pl.program_id(0) == 0, dg)

    consts, const_specs = _operands([w["g_mlp"], w["w_up"], w["w_down"]])
    out_shape = (jax.ShapeDtypeStruct((t_len, d), F32), jax.ShapeDtypeStruct((t_len, ff), BF16),
                 jax.ShapeDtypeStruct((t_len, ff), BF16), jax.ShapeDtypeStruct((t_len, d), BF16),
                 jax.ShapeDtypeStruct((1, d), F32))
    return pl.pallas_call(body, name="mlp_bwd", grid=(t_len // tm,), in_specs=[_rows(tm, d), _rows(tm, d)] + const_specs,
                          out_specs=(_rows(tm, d), _rows(tm, ff), _rows(tm, ff), _rows(tm, d), _whole((1, d))),
                          out_shape=out_shape, compiler_params=_cparams(("arbitrary",)))(dx2, x1, *consts)


def _mix_out_bwd(dx1, attn, conv, w):
    t_len, d = dx1.shape
    tm = min(LIGHT_TOKEN_TILE, t_len)

    def body(dx1_ref, a_ref, c_ref, goa, goc, wo, mixed_ref, da_ref, dc_ref, dgoa_ref, dgoc_ref):
        av, cv = a_ref[...], c_ref[...]
        mixed_ref[...] = jnp.concatenate([_rms(av, goa[...]), _rms(cv, goc[...])], axis=1).astype(BF16)
        dmixed = _mm_nt(dx1_ref[...].astype(BF16), _row_shards_joined(wo))
        da, dga = _rms_bwd(av, goa[...], dmixed[:, :ATTN_WIDTH])
        dc, dgc = _rms_bwd(cv, goc[...], dmixed[:, ATTN_WIDTH:])
        da_ref[...] = da
        dc_ref[...] = dc
        first = pl.program_id(0) == 0
        _accumulate(dgoa_ref, first, dga)
        _accumulate(dgoc_ref, first, dgc)

    consts, const_specs = _operands([w["g_out_attn"], w["g_out_conv"], w["w_o"]])
    out_shape = (jax.ShapeDtypeStruct((t_len, d), BF16), jax.ShapeDtypeStruct((t_len, ATTN_WIDTH), F32),
                 jax.ShapeDtypeStruct((t_len, CONV_WIDTH), F32), jax.ShapeDtypeStruct((1, ATTN_WIDTH), F32),
                 jax.ShapeDtypeStruct((1, CONV_WIDTH), F32))
    out_specs = (_rows(tm, d), _rows(tm, ATTN_WIDTH), _rows(tm, CONV_WIDTH), _whole((1, ATTN_WIDTH)),
                 _whole((1, CONV_WIDTH)))
    return pl.pallas_call(body, name="mix_out_bwd", grid=(t_len // tm,),
                          in_specs=[_rows(tm, d), _rows(tm, ATTN_WIDTH), _rows(tm, CONV_WIDTH)] + const_specs,
                          out_specs=out_specs, out_shape=out_shape,
                          compiler_params=_cparams(("arbitrary",)))(dx1, attn, conv, *consts)


def _attn_bwd(qf, kf, kv, o, do, lse, scatter):
    t_len = qf.shape[0]
    blk = min(ATTN_BLOCK, t_len)
    nb = t_len // blk
    n_sub = min(ATTN_KV_SUB, nb)
    reps = blk // LANES

    def body(q_ref, k_ref, kv_ref, o_ref, do_ref, lse_ref, dq_ref, dk_ref, dkv_ref, delta_ref, dob_ref):
        hd = pl.program_id(0)
        lane = lax.broadcasted_iota(jnp.int32, (blk, LANES), 1)
        even = (lane * 0 + hd % 2) == 0
        mine = jnp.where(lane < V_HEAD, 0, 1) == hd % 2
        row = lax.broadcasted_iota(jnp.int32, (blk, blk), 0)
        col = lax.broadcasted_iota(jnp.int32, (blk, blk), 1)
        dq_ref[...] = jnp.zeros_like(dq_ref)

        def prepare(i, carry):
            qs = pl.ds(pl.multiple_of(i * blk, blk), blk)
            dov = do_ref[qs, :]
            prod = jnp.where(mine, dov * o_ref[qs, :], 0.0)
            delta_ref[qs, :] = jnp.broadcast_to(jnp.sum(prod, axis=-1, keepdims=True), (blk, LANES))
            moved = jnp.where(even, pltpu.roll(dov, V_HEAD, 1), dov)
            dob_ref[qs, :] = jnp.where(lane >= V_HEAD, moved, 0.0).astype(BF16)
            return carry
        lax.fori_loop(0, nb, prepare, 0)

        def kvblock(jj, carry):
            base = jj * n_sub
            kss = [pl.ds(pl.multiple_of((base + a) * blk, blk), blk) for a in range(n_sub)]
            k = [k_ref[ks, :] for ks in kss]
            kvv = [kv_ref[ks, :] for ks in kss]

            def products(i):
                qs = pl.ds(pl.multiple_of(i * blk, blk), blk)
                q, dob = q_ref[qs, :], dob_ref[qs, :]
                return tuple((_mm_nt(q, k[a]), _mm_nt(dob, kvv[a])) for a in range(n_sub))

            def qstep(i, raw, accs, kinds):
                qs = pl.ds(pl.multiple_of(i * blk, blk), blk)
                q = q_ref[qs, :]
                dob = dob_ref[qs, :]
                lse_t = jnp.concatenate([lse_ref[qs, :]] * reps, axis=1)
                delta_t = jnp.concatenate([delta_ref[qs, :]] * reps, axis=1)
                new, dq_add = [], None
                for a in range(n_sub):
                    if kinds[a] is None:
                        new.append(accs[a])
                        continue
                    dk_acc, dv_acc = accs[a]
                    s, dp = raw[a]
                    if kinds[a]:
                        s = jnp.where(col <= row, s, -jnp.inf)
                    p = jnp.exp(s - lse_t)
                    ds = (p * (dp - delta_t)).astype(BF16)
                    new.append((dk_acc + _mm_tn(ds, q), dv_acc + _mm_tn(p.astype(BF16), dob)))
                    part = _mm(ds, k[a])
                    dq_add = part if dq_add is None else dq_add + part
                dq_ref[qs, :] += dq_add
                return tuple(new)

            zero = jnp.zeros((blk, LANES), F32)
            accs = ((zero, zero),) * n_sub
            for b in range(n_sub):
                accs = qstep(base + b, products(base + b), accs, tuple((a == b) if a <= b else None for a in range(n_sub)))

            def pipelined(i, carried):
                raw, acc = carried
                return products(jnp.minimum(i + 1, nb - 1)), qstep(i, raw, acc, (False,) * n_sub)

            first = base + n_sub
            _, accs = lax.fori_loop(first, nb, pipelined, (products(jnp.minimum(first, nb - 1)), accs))
            for a in range(n_sub):
                dk_ref[kss[a], :] = accs[a][0]
                dkv_ref[kss[a], :] = accs[a][1]
            return carry
        lax.fori_loop(0, nb // n_sub, kvblock, 0)

    head = pl.BlockSpec((t_len, HEAD_PAD), lambda h: (0, h))
    pair = pl.BlockSpec((t_len, 2 * V_HEAD), lambda h: (0, h // 2))
    out = jax.ShapeDtypeStruct((t_len, N_HEADS * HEAD_PAD), F32)
    vmem_scratch = [pltpu.VMEM((t_len, LANES), F32), pltpu.VMEM((t_len, LANES), BF16)]
    (dq, dk, dkv), landed = _call_hosting_scatter("attn_bwd", body, N_HEADS, [head, head, head, pair, pair, head],
                                                  (head, head, head), (out, out, out), vmem_scratch,
                                                  (qf, kf, kv, o, do, lse), scatter)
    return dq, dk, dkv, landed


def _call_hosting_scatter(name, body, n_steps, in_specs, out_specs, out_shape, scratch_shapes, args, scatter):
    parts, landed, layers = scatter
    n_in, n_out, n_sc = len(in_specs), len(out_shape), len(parts)

    def hosting_body(*refs):
        ins, part_refs, refs = refs[:n_in], refs[n_in:n_in + n_sc], refs[n_in + 2 * n_sc:]
        outs, landed_refs, sems, scratch = (refs[:n_out], refs[n_out:n_out + n_sc], refs[n_out + n_sc:n_out + n_sc + 3],
                                            refs[n_out + n_sc + 3:])
        start, finish = _scatter_phases(part_refs, landed_refs, *sems, layers)
        step = pl.program_id(0)
        pl.when(step == 0)(start)
        body(*ins, *outs, *scratch)
        pl.when(step == n_steps - 1)(finish)

    res = pl.pallas_call(hosting_body, name=name + "_scatter", grid=(n_steps,),
                         in_specs=list(in_specs) + _any_specs(2 * n_sc),
                         out_specs=tuple(out_specs) + tuple(_any_specs(n_sc)),
                         out_shape=tuple(out_shape) + _same_shapes(landed),
                         scratch_shapes=_scatter_semaphores(n_sc) + list(scratch_shapes),
                         input_output_aliases={n_in + n_sc + a: n_out + a for a in range(n_sc)},
                         compiler_params=_cparams(("arbitrary",)))(*args, *parts, *landed)
    return res[:n_out], res[n_out:]


def _front_bwd(x, z, dx1, dqf, dkf, dkv_in, dconv, w, tabs, scatter):
    t_len, d = x.shape
    tm = min(TOKEN_TILE, t_len)
    nt = t_len // tm
    hb_per_tile = tm // HALO
    n_halo = t_len // HALO
    hp = N_HEADS * HEAD_PAD

    def body(x_ref, z_ref, zp_ref, zn_ref, dx1_ref, dqf_ref, dkf_ref, dkv_ref, dc_ref, dcn_ref,
             gmix, win, gq, wuq, gkv, wukv, gqn, gkn, cw_ref, gm_ref, cos_ref, sin_ref,
             dx_ref, dz_ref, h_ref, qn_ref, kvn_ref, dqr_ref, dkvr_ref,
             dgmix_ref, dgq_ref, dgkv_ref, dgqn_ref, dgkn_ref, dcw_ref, ubuf, dybuf):
        i = pl.program_id(0)
        first = i == 0
        xv, zv = x_ref[...], z_ref[...]
        hb = _rms(xv, gmix[...]).astype(BF16)
        h_ref[...] = hb
        zq, zkv = zv[:, Z_Q[0]:Z_Q[1]], zv[:, Z_KV[0]:Z_KV[1]]
        qnb = _rms(zq, gq[...]).astype(BF16)
        qn_ref[...] = qnb
        kvb = _rms(zkv, gkv[...]).astype(BF16)
        kvn_ref[...] = kvb
        kpe = zv[:, Z_KPE[0]:Z_KPE[1]]
        kpe = jnp.concatenate([kpe] * PAIR, axis=1)
        cos, sin = (jnp.concatenate([t[...]] * PAIR, axis=1) for t in (cos_ref, sin_ref))
        gm = gm_ref[...]
        width = PAIR * HEAD_PAD
        lane = lax.broadcasted_iota(jnp.int32, (tm, width), 1) & (HEAD_PAD - 1)
        is_nope = lane < QK_NOPE
        is_rope = (lane >= QK_NOPE) & (lane < QK_HEAD)
        dkpe = jnp.zeros((tm, width), F32)
        dgqn = jnp.zeros((1, width), F32)
        dgkn = jnp.zeros((1, width), F32)
        dqn = jnp.zeros((tm, Q_LORA), F32)
        dkvn = jnp.zeros((tm, KV_LORA), F32)
        for pr in range(N_HEADS // PAIR):
            sl = slice(pr * width, (pr + 1) * width)
            dxq, dg = _qk_bwd(_mm(qnb, wuq[pr]), gqn[...], dqf_ref[:, sl] * ATTN_SCALE, cos, sin, gm, lane)
            dxq = dxq.astype(BF16)
            dqr_ref[:, sl] = dxq
            dqn = dqn + _mm_nt(dxq, wuq[pr])
            dgqn = dgqn + dg
            k_raw = jnp.where(is_nope, _mm(kvb, wukv[pr]), 0.0) + kpe
            dxk, dg = _qk_bwd(k_raw, gkn[...], dkf_ref[:, sl], cos, sin, gm, lane)
            dkv = jnp.where(is_nope, dxk, dkv_ref[:, sl]).astype(BF16)
            dkvr_ref[:, sl] = dkv
            dkvn = dkvn + _mm_nt(dkv, wukv[pr])
            dkpe = dkpe + jnp.where(is_rope, dxk, 0.0)
            dgkn = dgkn + dg
        dkpe = dkpe[:, :HEAD_PAD] + dkpe[:, HEAD_PAD:]
        _accumulate(dgqn_ref, first, dgqn[:, :HEAD_PAD] + dgqn[:, HEAD_PAD:])
        _accumulate(dgkn_ref, first, dgkn[:, :HEAD_PAD] + dgkn[:, HEAD_PAD:])
        dzq, dg = _rms_bwd(zq, gq[...], dqn)
        _accumulate(dgq_ref, first, dg)
        dzkv, dg = _rms_bwd(zkv, gkv[...], dkvn)
        _accumulate(dgkv_ref, first, dg)

        gb, gc, xin = zv[:, Z_GB[0]:Z_GB[1]], zv[:, Z_GC[0]:Z_GC[1]], zv[:, Z_XIN[0]:Z_XIN[1]]
        u = gc * xin
        dcv = dc_ref[...]
        dy = dcv * gb
        zp, zn = zp_ref[...], zn_ref[...]
        ubuf[0:HALO, :] = (zp[:, Z_GC[0]:Z_GC[1]] * zp[:, Z_XIN[0]:Z_XIN[1]]) * jnp.where(first, 0.0, 1.0)
        ubuf[HALO:HALO + tm, :] = u
        dybuf[0:tm, :] = dy
        dybuf[tm:tm + HALO, :] = (dcn_ref[...] * zn[:, Z_GB[0]:Z_GB[1]]) * jnp.where(i == nt - 1, 0.0, 1.0)
        cw = cw_ref[...]
        u1, u2 = ubuf[pl.ds(HALO - 1, tm), :], ubuf[pl.ds(HALO - 2, tm), :]
        y = cw[0:1] * u + cw[1:2] * u1 + cw[2:3] * u2
        du = cw[0:1] * dy + cw[1:2] * dybuf[pl.ds(1, tm), :] + cw[2:3] * dybuf[pl.ds(2, tm), :]
        dcw = jnp.concatenate([jnp.sum(dy * u, axis=0, keepdims=True), jnp.sum(dy * u1, axis=0, keepdims=True),
                               jnp.sum(dy * u2, axis=0, keepdims=True), jnp.zeros((HALO - 3, CONV_WIDTH), F32)], axis=0)
        _accumulate(dcw_ref, first, dcw)

        dz_ref[:, Z_Q[0]:Z_Q[1]] = dzq.astype(BF16)
        dz_ref[:, Z_KV[0]:Z_KV[1]] = dzkv.astype(BF16)
        dz_ref[:, Z_GB[0]:Z_GB[1]] = (dcv * y).astype(BF16)
        dz_ref[:, Z_GC[0]:Z_GC[1]] = (du * xin).astype(BF16)
        dz_ref[:, Z_XIN[0]:Z_XIN[1]] = (du * gc).astype(BF16)
        dz_ref[:, Z_KPE[0]:Z_KPE[1]] = dkpe.astype(BF16)
        dx, dg = _rms_bwd(xv, gmix[...], _mm_nt(dz_ref[...], win[...]))
        dx_ref[...] = dx1_ref[...] + dx
        _accumulate(dgmix_ref, first, dg)

    prev_halo = lambda n: pl.BlockSpec((HALO, n), lambda i: (jnp.maximum(i * hb_per_tile - 1, 0), 0))
    next_halo = lambda n: pl.BlockSpec((HALO, n), lambda i: (jnp.minimum((i + 1) * hb_per_tile, n_halo - 1), 0))
    consts, const_specs = _operands([w["g_mix"], w["w_in"], w["g_q_lat"], w["w_uq"], w["g_kv_lat"], w["w_ukv"],
                                     w["g_qn"], w["g_kn"], w["conv_w"], tabs["gm"]])
    in_specs = ([_rows(tm, d), _rows(tm, Z_COLS), prev_halo(Z_COLS), next_halo(Z_COLS), _rows(tm, d), _rows(tm, hp),
                 _rows(tm, hp), _rows(tm, hp), _rows(tm, CONV_WIDTH), next_halo(CONV_WIDTH)]
                + const_specs + [_rows(tm, HEAD_PAD)] * 2)
    out_shape = (jax.ShapeDtypeStruct((t_len, d), F32), jax.ShapeDtypeStruct((t_len, Z_COLS), BF16),
                 jax.ShapeDtypeStruct((t_len, d), BF16), jax.ShapeDtypeStruct((t_len, Q_LORA), BF16),
                 jax.ShapeDtypeStruct((t_len, KV_LORA), BF16), jax.ShapeDtypeStruct((t_len, hp), BF16),
                 jax.ShapeDtypeStruct((t_len, hp), BF16),
                 jax.ShapeDtypeStruct((1, d), F32), jax.ShapeDtypeStruct((1, Q_LORA), F32),
                 jax.ShapeDtypeStruct((1, KV_LORA), F32), jax.ShapeDtypeStruct((1, LANES), F32),
                 jax.ShapeDtypeStruct((1, LANES), F32), jax.ShapeDtypeStruct((HALO, CONV_WIDTH), F32))
    out_specs = tuple(_rows(tm, s.shape[1]) for s in out_shape[:7]) + tuple(_whole(s.shape) for s in out_shape[7:])
    conv_scratch = [pltpu.VMEM((tm + HALO, CONV_WIDTH), F32), pltpu.VMEM((tm + HALO, CONV_WIDTH), F32)]
    args = (x, z, z, z, dx1, dqf, dkf, dkv_in, dconv, dconv, *consts, tabs["cos"], tabs["sin"])
    return _call_hosting_scatter("front_bwd", body, nt, in_specs, out_specs, out_shape, conv_scratch, args, scatter)


def _wgrad(a, b, shard_cols=None, shard_rows=None, out_dtype=BF16):
    t_len, kk = a.shape
    nn = b.shape[1]
    tk = min(kk, WGRAD_TILE)
    tn = next(c for c in range(min(nn, WGRAD_TILE), 0, -LANES) if nn % c == 0 and c % (shard_cols or LANES) == 0)
    tt = min(t_len, WGRAD_TOKENS)
    nt = t_len // tt
    per_block = tn // shard_cols if shard_cols else tk // shard_rows if shard_rows else 1

    def body(a_ref, b_ref, o_ref, acc):
        t = pl.program_id(2)

        @pl.when(t == 0)
        def _():
            acc[...] = jnp.zeros_like(acc)
        acc[...] += _mm_tn(a_ref[...].astype(BF16), b_ref[...].astype(BF16))

        @pl.when(t == nt - 1)
        def _():
            if shard_cols:
                for s in range(per_block):
                    o_ref[s] = acc[:, s * shard_cols:(s + 1) * shard_cols].astype(out_dtype)
            elif shard_rows:
                for s in range(per_block):
                    o_ref[s] = acc[s * shard_rows:(s + 1) * shard_rows, :].astype(out_dtype)
            else:
                o_ref[...] = acc[...].astype(out_dtype)

    if shard_cols:
        out_shape = jax.ShapeDtypeStruct((nn // shard_cols, kk, shard_cols), out_dtype)
        out_spec = pl.BlockSpec((per_block, tk, shard_cols), lambda i, j, t: (j, i, 0))
    elif shard_rows:
        out_shape = jax.ShapeDtypeStruct((kk // shard_rows, shard_rows, nn), out_dtype)
        out_spec = pl.BlockSpec((per_block, shard_rows, tn), lambda i, j, t: (i, 0, j))
    else:
        out_shape = jax.ShapeDtypeStruct((kk, nn), out_dtype)
        out_spec = pl.BlockSpec((tk, tn), lambda i, j, t: (i, j))
    return pl.pallas_call(body, name="wgrad", grid=(kk // tk, nn // tn, nt),
                          in_specs=[pl.BlockSpec((tt, tk), lambda i, j, t: (t, i)),
                                    pl.BlockSpec((tt, tn), lambda i, j, t: (t, j))],
                          out_specs=out_spec, out_shape=out_shape, scratch_shapes=[pltpu.VMEM((tk, tn), F32)],
                          compiler_params=_cparams(("parallel", "parallel", "arbitrary")))(a, b)


def _my_place():
    return lax.axis_index("x"), lax.axis_index("y"), lax.axis_index("c")


def _any_specs(n):
    return [pl.BlockSpec(memory_space=pl.ANY)] * n


def _all_gather(blocks):
    n = len(blocks)

    def body(*refs):
        start, forward, finish = _gather_phases(refs[:n], refs[n:2 * n], *refs[2 * n:])
        start()
        forward()
        finish()

    return pl.pallas_call(body, name="all_gather", out_shape=_gather_out_shape(blocks), in_specs=_any_specs(n),
                          out_specs=tuple(_any_specs(n)), scratch_shapes=_gather_semaphores(n))(*blocks)


def _gather_out_shape(blocks):
    return tuple(jax.ShapeDtypeStruct((N_DEV,) + b.shape, b.dtype) for b in blocks)


def _gather_semaphores(n):
    return [pltpu.SemaphoreType.DMA((n, 7)), pltpu.SemaphoreType.DMA((n, 7)), pltpu.SemaphoreType.DMA((n,))]


def _gather_phases(x_refs, out_refs, send_sems, recv_sems, local_sems):
    n = len(x_refs)
    x, y, c = _my_place()
    me, sibling = (x, y, c), (x, y, 1 - c)
    chips = [(1 - x, y), (x, 1 - y), (1 - x, 1 - y)]

    def slot(a, px, py, pc):
        return out_refs[a].at[4 * px + 2 * py + pc]

    def copy(a, k, blk, to, src=None):
        return pltpu.make_async_remote_copy(src_ref=slot(a, *blk) if src is None else src, dst_ref=slot(a, *blk),
                                            send_sem=send_sems.at[a, k], recv_sem=recv_sems.at[a, k],
                                            device_id=to, device_id_type=MESH)

    def own(a):
        return pltpu.make_async_copy(x_refs[a], slot(a, *me), local_sems.at[a])

    def first_hop(a):
        return [copy(a, 0, me, sibling, src=x_refs[a])] + [copy(a, 1 + j, me, (*chip, c), src=x_refs[a])
                                                           for j, chip in enumerate(chips)]

    def passed_on(a):
        return [copy(a, 4 + j, (*chip, c), sibling) for j, chip in enumerate(chips)]

    def start():
        for a in range(n):
            own(a).start()
        for a in range(n):
            for cp in first_hop(a):
                cp.start()

    def forward():
        for j, chip in enumerate(chips):
            for a in range(n):
                copy(a, 1 + j, (*chip, c), me).wait_recv()
                passed_on(a)[j].start()

    def finish():
        for a in range(n):
            copy(a, 0, sibling, me).wait_recv()
        for j, chip in enumerate(chips):
            for a in range(n):
                copy(a, 4 + j, (*chip, 1 - c), me).wait_recv()
        for a in range(n):
            for cp in first_hop(a) + passed_on(a):
                cp.wait_send()
            own(a).wait()

    return start, forward, finish


def _scatter_exchange(parts, landed, layers):
    n = len(parts)

    def body(*refs):
        start, finish = _scatter_phases(refs[:n], refs[2 * n:3 * n], *refs[3 * n:], layers)
        start()
        finish()

    return pl.pallas_call(body, name="scatter_exchange", out_shape=_same_shapes(landed), in_specs=_any_specs(2 * n),
                          out_specs=tuple(_any_specs(n)), scratch_shapes=_scatter_semaphores(n),
                          input_output_aliases={n + a: a for a in range(n)})(*parts, *landed)


def _same_shapes(arrays):
    return tuple(jax.ShapeDtypeStruct(a.shape, a.dtype) for a in arrays)


def _scatter_semaphores(n):
    return [pltpu.SemaphoreType.DMA((n, N_DEV - 1)), pltpu.SemaphoreType.DMA((n, N_DEV - 1)), pltpu.SemaphoreType.DMA((n,))]


def _scatter_phases(part_refs, landed_refs, send_sems, recv_sems, local_sems, layers):
    n = len(part_refs)
    x, y, c = _my_place()
    flips = [(0, 0, 1), (1, 0, 0), (0, 1, 0), (1, 1, 0), (1, 0, 1), (0, 1, 1), (1, 1, 1)]
    peers = [((1 - x) if fx else x, (1 - y) if fy else y, (1 - c) if fc else c) for fx, fy, fc in flips]
    my_k = 4 * x + 2 * y + c

    def index(peer):
        return 4 * peer[0] + 2 * peer[1] + peer[2]

    def send(a, r):
        return pltpu.make_async_remote_copy(src_ref=part_refs[a].at[index(peers[r])], dst_ref=landed_refs[a].at[my_k, layers[a]],
                                            send_sem=send_sems.at[a, r], recv_sem=recv_sems.at[a, r],
                                            device_id=peers[r], device_id_type=MESH)

    def arrival(a, r):
        return pltpu.make_async_remote_copy(src_ref=part_refs[a].at[my_k], dst_ref=landed_refs[a].at[index(peers[r]), layers[a]],
                                            send_sem=send_sems.at[a, r], recv_sem=recv_sems.at[a, r],
                                            device_id=peers[r], device_id_type=MESH)

    def own(a):
        return pltpu.make_async_copy(part_refs[a].at[my_k], landed_refs[a].at[my_k, layers[a]], local_sems.at[a])

    def start():
        for a in range(n):
            own(a).start()
        for r in range(len(peers)):
            for a in range(n):
                send(a, r).start()

    def finish():
        for r in range(len(peers)):
            for a in range(n):
                arrival(a, r).wait_recv()
        for r in range(len(peers)):
            for a in range(n):
                send(a, r).wait_send()
        for a in range(n):
            own(a).wait()

    return start, finish


def _row_block(rows):
    return ROW_BLOCK if rows % ROW_BLOCK == 0 else rows


def _sum_leading(parts):
    n_part, shape = parts.shape[0], parts.shape[1:]
    rows, cols = shape[-2:]
    rb = _row_block(rows)

    def body(p_ref, o_ref):
        acc = p_ref[0].astype(F32)
        for k in range(1, n_part):
            acc = acc + p_ref[k].astype(F32)
        o_ref[...] = acc

    if len(shape) == 3:
        grid = (shape[0], rows // rb)
        in_spec = pl.BlockSpec((n_part, None, rb, cols), lambda l, i: (0, l, i, 0))
        out_spec = pl.BlockSpec((None, rb, cols), lambda l, i: (l, i, 0))
    else:
        grid = (rows // rb,)
        in_spec, out_spec = pl.BlockSpec((n_part, rb, cols), lambda i: (0, i, 0)), _rows(rb, cols)
    return pl.pallas_call(body, name="sum_leading", grid=grid, in_specs=[in_spec], out_specs=out_spec,
                          out_shape=jax.ShapeDtypeStruct(shape, F32),
                          compiler_params=_cparams(("parallel",) * len(grid)))(parts)


def _adamw(w, g, m, v):
    shape = w.shape
    rows, cols = shape[-2:]
    rb = _row_block(rows)

    def body(w_ref, g_ref, m_ref, v_ref, d_ref, nm_ref, nv_ref):
        gv = g_ref[...]
        nm = ADAM_B1 * m_ref[...] + (1.0 - ADAM_B1) * gv
        nv = ADAM_B2 * v_ref[...] + (1.0 - ADAM_B2) * jnp.square(gv)
        m_hat = nm / (1.0 - ADAM_B1 ** ADAM_STEP)
        v_hat = nv / (1.0 - ADAM_B2 ** ADAM_STEP)
        d_ref[...] = -ADAM_LR * (m_hat / (jnp.sqrt(v_hat) + ADAM_EPS) + ADAM_WD * w_ref[...])
        nm_ref[...] = nm
        nv_ref[...] = nv

    if len(shape) == 3:
        grid, spec = (shape[0], rows // rb), pl.BlockSpec((None, rb, cols), lambda l, i: (l, i, 0))
    else:
        grid, spec = (rows // rb,), _rows(rb, cols)
    out = jax.ShapeDtypeStruct(shape, F32)
    return pl.pallas_call(body, name="adamw", grid=grid, in_specs=[spec] * 4, out_specs=(spec,) * 3,
                          out_shape=(out,) * 3, compiler_params=_cparams(("parallel",) * len(grid)))(w, g, m, v)


def _rope_tables(positions):
    t_len = positions.shape[0]
    inv_freq = 1.0 / (ROPE_THETA ** (jnp.arange(0, QK_ROPE, 2, dtype=F32) / QK_ROPE))
    ang = positions.astype(F32)[:, None] * inv_freq
    c, s = jnp.cos(ang), jnp.sin(ang)
    one, zero = jnp.ones((t_len, QK_NOPE), F32), jnp.zeros((t_len, QK_NOPE), F32)
    cos = jnp.concatenate([one, c, c, one[:, :LANES - QK_HEAD]], axis=1)
    sin = jnp.concatenate([zero, -s, s, zero[:, :LANES - QK_HEAD]], axis=1)
    idx = jnp.arange(PAIR * HEAD_PAD)
    lane, head = idx % HEAD_PAD, idx // HEAD_PAD
    grp = jnp.where(lane < QK_NOPE, 0, jnp.where(lane < QK_HEAD, 1, 2)) + 3 * head
    val = jnp.where(lane < QK_NOPE, 1.0 / QK_NOPE, jnp.where(lane < QK_HEAD, 1.0 / QK_ROPE, 0.0))
    gm = jnp.where(grp[:, None] == grp[None, :], val[None, :], 0.0).astype(BF16)
    return {"cos": cos, "sin": sin, "gm": gm}


def _head_gain(g_nope, g_rope):
    one = jnp.concatenate([g_nope, g_rope, jnp.zeros((HEAD_PAD - QK_HEAD,), F32)])
    return jnp.concatenate([one] * PAIR).reshape(1, PAIR * HEAD_PAD)


def _head_pairs(w):
    return jnp.concatenate([w[k::PAIR] for k in range(PAIR)], axis=2)


def _padded_w_in(shards):
    width = shards.shape[2]
    zeros = jnp.zeros((shards.shape[1], QK_NOPE), shards.dtype)

    def natural(start, end):
        pieces = []
        for k in range(N_DEV):
            lo, hi = max(start, k * width), min(end, (k + 1) * width)
            if lo < hi:
                pieces.append(shards[k][:, lo - k * width:hi - k * width])
        return pieces

    o2, o3 = Q_LORA + KV_LORA, Q_LORA + KV_LORA + QK_ROPE
    return jnp.concatenate(natural(0, o2) + natural(o3, N_DEV * width) + [zeros] + natural(o2, o3)
                           + [zeros[:, :LANES - QK_HEAD]], axis=1)


def _w_in_grad_shards(d_in):
    o2, o3 = Q_LORA + KV_LORA, Q_LORA + KV_LORA + QK_ROPE
    width = (Z_XIN[1] + QK_ROPE) // N_DEV
    runs = [(0, o2, 0), (o2, o3, Z_KPE[0] + QK_NOPE), (o3, N_DEV * width, o2)]
    shards = []
    for k in range(N_DEV):
        pieces = []
        for start, end, at in runs:
            lo, hi = max(start, k * width), min(end, (k + 1) * width)
            if lo < hi:
                pieces.append(d_in[:, at + lo - start:at + hi - start])
        shards.append(pieces[0] if len(pieces) == 1 else jnp.concatenate(pieces, axis=1))
    return jnp.stack(shards)


def kernel(x, p, positions, g_mix, w_in, g_q_lat, w_uq, g_kv_lat, w_ukv, g_qn_nope, g_qn_rope, g_kn_nope, g_kn_rope, conv_w, g_out_attn, g_out_conv, w_o, g_mlp, w_up, w_down, g_ple, w_ple_gate, w_ple, loss_target, m_g_mix, m_w_in, m_g_q_lat, m_w_uq, m_g_kv_lat, m_w_ukv, m_g_qn_nope, m_g_qn_rope, m_g_kn_nope, m_g_kn_rope, m_conv_w, m_g_out_attn, m_g_out_conv, m_w_o, m_g_mlp, m_w_up, m_w_down, m_g_ple, m_w_ple_gate, m_w_ple, v_g_mix, v_w_in, v_g_q_lat, v_w_uq, v_g_kv_lat, v_w_ukv, v_g_qn_nope, v_g_qn_rope, v_g_kn_nope, v_g_kn_rope, v_conv_w, v_g_out_attn, v_g_out_conv, v_w_o, v_g_mlp, v_w_up, v_w_down, v_g_ple, v_w_ple_gate, v_w_ple):
    given = dict(locals())
    weights = {n: given[n] for n in WEIGHT_NAMES}
    gains = {n: given[n] for n in GAIN_NAMES}
    depth = w_in.shape[0]
    xs, target = x[0], loss_target[0]
    d_model = xs.shape[1]
    uq_cols = w_uq.shape[2]
    n_taps = conv_w.shape[1]

    mat_names = [n for n in SHARD_NAMES if n != "conv_w"]
    local = [weights[n].astype(BF16) for n in mat_names]
    local[1] = jnp.pad(local[1], ((0, 0), (0, 0), (0, HEAD_PAD - uq_cols)))
    local = dict(zip(mat_names, local))
    front_names = ("w_in", "w_uq", "w_ukv")
    first = _all_gather([local[n][0] for n in front_names] + [conv_w])
    conv_full = jnp.transpose(first[-1], (1, 2, 0, 3)).reshape(depth, n_taps, -1)
    tabs = _rope_tables(positions[0])

    def front_weights(layer, full):
        lw = {n: gains[n][layer].reshape(1, -1) for n in GAIN_NAMES}
        lw.update({"w_in": _padded_w_in(full["w_in"]), "w_uq": _head_pairs(full["w_uq"]),
                   "w_ukv": _head_pairs(full["w_ukv"]),
                   "conv_w": jnp.pad(conv_full[layer], ((0, HALO - n_taps), (0, 0))),
                   "g_qn": _head_gain(g_qn_nope[layer], g_qn_rope[layer]),
                   "g_kn": _head_gain(g_kn_nope[layer], g_kn_rope[layer])})
        return lw

    def rest_weights(full):
        return {"w_ple": full["w_ple"], "w_up": full["w_up"], "w_down": full["w_down"],
                "w_o": full["w_o"], "w_ple_gate": full["w_ple_gate"]}

    saved, layer_w = [], []
    cur = xs
    gathered = dict(zip(front_names, first[:-1]))
    mlp_names = ("w_up", "w_down")
    for layer in range(depth):
        w = front_weights(layer, gathered)
        own_small = [n for n in mat_names if n not in front_names + mlp_names] if layer == 0 else []
        (z, qf, kf, kv, conv), got = _front_fwd(cur, w, tabs, gather=[(local[n], 0) for n in own_small])
        gathered.update(zip(own_small, got))
        lots = [[n for n in mat_names if n in mlp_names], [n for n in mat_names if n not in mlp_names]]
        behind_attn, behind_mlp = (lots[1], lots[0]) if layer == 0 else lots
        if layer + 1 == depth:
            behind_attn, behind_mlp = [], []
        wanted = [(n, 0) for n in mlp_names] if layer == 0 else []
        wanted += [(n, layer + 1) for n in behind_attn]
        attn, lse, got = _attn_fwd(qf, kf, kv, gather=[(local[n], at) for n, at in wanted])
        coming = {}
        for (n, at), g in zip(wanted, got):
            (gathered if at == layer else coming)[n] = g
        w.update(rest_weights(gathered))
        layer_w.append(w)
        x1 = _mix_out_fwd(cur, attn, conv, w)
        x2, got = _mlp_fwd(x1, w, gather=[(local[n], layer + 1) for n in behind_mlp])
        coming.update(zip(behind_mlp, got))
        gathered = coming
        x3 = _ple_fwd(x2, p[layer, 0], w)
        saved.append(dict(x=cur, z=z, qf=qf, kf=kf, kv=kv, conv=conv, attn=attn, lse=lse, x1=x1, x2=x2))
        cur = x3

    sq, dx = _loss_and_grad(cur, target)
    loss = lax.psum(0.5 / d_model * sq[0, 0], ("x", "y", "c"))

    landed = {n: lax.empty((N_DEV, depth) + weights[n].shape[1:], BF16) for n in SHARD_NAMES}
    gain_grads = [None] * depth
    late = {}
    for layer in reversed(range(depth)):
        w, s = layer_w[layer], saved[layer]
        pl_in = p[layer, 0]
        dx2, de, h3, dpre, dg_ple = _ple_bwd(dx, s["x2"], pl_in, w)
        dx1, r, da, h2, dg_mlp = _mlp_bwd(dx2, s["x1"], w)
        mixed, dattn, dconv, dg_oa, dg_oc = _mix_out_bwd(dx1, s["attn"], s["conv"], w)
        def exchange_of(sending):
            return ([v[0] for v in sending.values()], [landed[n] for n in sending], [v[1] for v in sending.values()])

        big = {"w_up": (_wgrad(h2, da, shard_cols=w_up.shape[2]), layer),
               "w_down": (_wgrad(r, dx2, shard_rows=w_down.shape[1]), layer)}
        small = {"w_o": (_wgrad(mixed, dx1, shard_rows=w_o.shape[1]), layer),
                 "w_ple_gate": (_wgrad(h3, dpre, shard_rows=w_ple_gate.shape[1]), layer),
                 "w_ple": (_wgrad(pl_in, de, shard_cols=w_ple.shape[2]), layer), **late}
        dqf, dkf, dkv, got = _attn_bwd(s["qf"], s["kf"], s["kv"], s["attn"], dattn, s["lse"], exchange_of(big))
        landed.update(zip(big, got))
        (dx0, dz, hb, qn, kvn, dqr, dkvr, dg_mix, dg_q, dg_kv, dg_qn, dg_kn, dcw), got = _front_bwd(
            s["x"], s["z"], dx1, dqf, dkf, dkv, dconv, w, tabs, exchange_of(small))
        landed.update(zip(small, got))
        late = {"w_in": (_w_in_grad_shards(_wgrad(hb, dz, out_dtype=F32)).astype(BF16), layer),
                "w_uq": (_wgrad(qn, dqr, shard_cols=HEAD_PAD)[..., :uq_cols], layer),
                "w_ukv": (_wgrad(kvn, dkvr, shard_cols=HEAD_PAD), layer),
                "conv_w": (jnp.transpose(dcw[:n_taps].reshape(n_taps, N_DEV, -1), (1, 0, 2)).astype(BF16), layer)}
        gain_grads[layer] = jnp.concatenate([
            dg_mix[0], dg_q[0], dg_kv[0], dg_qn[0, :QK_NOPE], dg_qn[0, QK_NOPE:QK_HEAD], dg_kn[0, :QK_NOPE],
            dg_kn[0, QK_NOPE:QK_HEAD], dg_oa[0], dg_oc[0], dg_mlp[0], dg_ple[0]])
        dx = dx0
    names = list(late)
    landed.update(zip(names, _scatter_exchange([late[n][0] for n in names], [landed[n] for n in names],
                                               [late[n][1] for n in names])))
    grads = {n: _sum_leading(landed[n]) for n in SHARD_NAMES}

    gg = jnp.stack(gain_grads)
    gg_rows = -(-gg.size // (HALO * LANES)) * HALO
    gg_pad = jnp.pad(gg.reshape(-1), (0, gg_rows * LANES - gg.size)).reshape(gg_rows, LANES)
    gg_sum = _sum_leading(_all_gather([gg_pad])[0]).reshape(-1)[:gg.size].reshape(gg.shape)
    off = 0
    for n in GAIN_NAMES:
        width = gains[n].shape[1]
        grads[n] = gg_sum[:, off:off + width]
        off += width

    deltas, new_m, new_v = {}, {}, {}
    for n in WEIGHT_NAMES:
        deltas[n], new_m[n], new_v[n] = _adamw(weights[n], grads[n], given["m_" + n], given["v_" + n])
    return (loss, dx[None], *[grads[n] for n in WEIGHT_NAMES], *[deltas[n] for n in WEIGHT_NAMES],
            *[new_m[n] for n in WEIGHT_NAMES], *[new_v[n] for n in WEIGHT_NAMES])
```
